```python
import math
import jax, jax.numpy as jnp
from jax import lax
import numpy as np

D_MODEL = 1024
BATCH = 8
SEQ = 16384
DEPTH = 2

HEAD_DIM = 64
SB_HEADS = 4
DIL_HEADS = 4
HGRN_HEADS = 4
HGRN_DK = 128
HGRN_DV = 128
SB_WIDTH = SB_HEADS * HEAD_DIM
DIL_WIDTH = DIL_HEADS * HEAD_DIM
HGRN_WIDTH = HGRN_HEADS * HGRN_DV
HGRN_KDIM = HGRN_HEADS * HGRN_DK
MIX_WIDTH = SB_WIDTH + DIL_WIDTH + HGRN_WIDTH
IN_SPLITS = (SB_WIDTH, SB_WIDTH, SB_WIDTH, DIL_WIDTH, DIL_WIDTH, DIL_WIDTH, HGRN_KDIM, HGRN_KDIM, HGRN_WIDTH, HGRN_WIDTH)
IN_WIDTH = 3 * SB_WIDTH + 3 * DIL_WIDTH + 2 * HGRN_KDIM + 2 * HGRN_WIDTH
D_FF = 2816
QBLK = 128
HGRN_CHUNK = 64
DIL_PATTERNS = ((128, 1), (512, 4), (2048, 16))
ROPE_THETA = 10000.0
EPS = 1e-6
LB_FLOOR = 1e-30
NEG_BIG = -1e30
N_MOD = 9
HALF_STEP = 0.5

kernel_name = "hymba_style_sb_dilated_hgrn2_macaron_block"


def _rmsnorm(x):
    xf = x.astype(jnp.float32)
    return (xf * lax.rsqrt(jnp.mean(xf * xf, axis=-1, keepdims=True) + EPS)).astype(x.dtype)


def _split_heads(x, n_heads):
    b, s, w = x.shape
    return x.reshape(b, s, n_heads, w // n_heads).transpose(0, 2, 1, 3)


def _merge_heads(x):
    b, h, s, d = x.shape
    return x.transpose(0, 2, 1, 3).reshape(b, s, h * d)


def _rope(x):
    s, d = x.shape[2], x.shape[3]
    half = d // 2
    inv_freq = ROPE_THETA ** (-jnp.arange(half, dtype=jnp.float32) * 2.0 / d)
    ang = jnp.arange(s, dtype=jnp.float32)[:, None] * inv_freq[None, :]
    cos, sin = jnp.cos(ang), jnp.sin(ang)
    xf = x.astype(jnp.float32)
    x1, x2 = xf[..., :half], xf[..., half:]
    return jnp.concatenate([x1 * cos - x2 * sin, x2 * cos + x1 * sin], axis=-1).astype(x.dtype)


def _swiglu(h, w_gate, w_up, w_down):
    return (jax.nn.silu(h @ w_gate) * (h @ w_up)) @ w_down


def _stick_breaking(q, k, v):
    b, h, s, d = q.shape
    nb = s // QBLK
    qf = q.astype(jnp.float32) * (d ** -0.5)
    kf = k.astype(jnp.float32)
    vf = v.astype(jnp.float32)
    q_blocks = qf.reshape(b, h, nb, QBLK, d).transpose(2, 0, 1, 3, 4)
    starts = jnp.arange(nb, dtype=jnp.int32) * QBLK
    kpos = jnp.arange(s, dtype=jnp.int32)

    def block(args):
        qb, start = args
        z = jnp.einsum('bhqd,bhkd->bhqk', qb, kf)
        qpos = start + jnp.arange(QBLK, dtype=jnp.int32)
        causal = kpos[None, :] < qpos[:, None]
        log_not_beta = jnp.where(causal, jax.nn.log_sigmoid(-z), 0.0)
        tail = lax.cumsum(log_not_beta, axis=3, reverse=True) - log_not_beta
        log_w = jnp.where(causal, jax.nn.log_sigmoid(z) + tail, NEG_BIG)
        w = jnp.exp(log_w)
        return jnp.einsum('bhqk,bhkd->bhqd', w, vf)

    out = lax.map(block, (q_blocks, starts))
    return out.transpose(1, 2, 0, 3, 4).reshape(b, h, s, d)


def _dilated_partial(q, k, v, window, dilation):
    b, h, s, d = q.shape
    steps = window // dilation
    sub_len = -(-s // (QBLK * dilation)) * QBLK
    pad = sub_len * dilation - s
    nb = sub_len // QBLK

    def to_residue(x):
        x = jnp.pad(x.astype(jnp.float32), ((0, 0), (0, 0), (0, pad), (0, 0)))
        x = x.reshape(b, h, sub_len, dilation, d).transpose(0, 1, 3, 2, 4)
        return x.reshape(b, h, dilation, nb, QBLK, d)

    qr, kr, vr = to_residue(q), to_residue(k), to_residue(v)

    def with_prev_block(x):
        prev = jnp.pad(x, ((0, 0), (0, 0), (0, 0), (1, 0), (0, 0), (0, 0)))[:, :, :, :-1]
        return jnp.concatenate([prev, x], axis=4)

    kw, vw = with_prev_block(kr), with_prev_block(vr)
    scores = jnp.einsum('bhrnqd,bhrnkd->bhrnqk', qr, kw)
    a_idx = jnp.arange(QBLK)[:, None]
    k_idx = jnp.arange(2 * QBLK)[None, :]
    dist = a_idx + QBLK - k_idx
    band = (dist >= 0) & (dist <= steps)
    valid = band[None] & ((jnp.arange(nb)[:, None, None] > 0) | (k_idx[None] >= QBLK))
    scores = jnp.where(valid, scores, NEG_BIG)
    mx = jnp.max(scores, axis=-1)
    p = jnp.where(valid, jnp.exp(scores - mx[..., None]), 0.0)
    den = jnp.sum(p, axis=-1)
    num = jnp.einsum('bhrnqk,bhrnkd->bhrnqd', p, vw)

    def back(x):
        x = x.reshape((b, h, dilation, sub_len) + x.shape[5:])
        perm = (0, 1, 3, 2) + tuple(range(4, x.ndim))
        x = x.transpose(perm).reshape((b, h, sub_len * dilation) + x.shape[4:])
        return x[:, :, :s]

    return back(num), back(den), back(mx)


def _dilated_mixture(q, k, v):
    parts = [_dilated_partial(q, k, v, w, r) for (w, r) in DIL_PATTERNS]
    m_all = parts[0][2]
    for _, _, m in parts[1:]:
        m_all = jnp.maximum(m_all, m)
    num = 0.0
    den = 0.0
    for n_i, d_i, m_i in parts:
        scale = jnp.exp(m_i - m_all)
        num = num + n_i * scale[..., None]
        den = den + d_i * scale
    return num / den[..., None]


def _hgrn2_chunkwise(q, k, v, log_f):
    b, h, s, dk = q.shape
    dv = v.shape[-1]
    n = s // HGRN_CHUNK

    def chunks(x):
        return x.astype(jnp.float32).reshape(b, h, n, HGRN_CHUNK, x.shape[-1]).transpose(2, 0, 1, 3, 4)

    qc, kc, vc = chunks(q), chunks(k), chunks(v)
    g_cum = jnp.cumsum(chunks(log_f), axis=3)
    causal = jnp.tril(jnp.ones((HGRN_CHUNK, HGRN_CHUNK), dtype=bool))[:, :, None]

    def step(state, xs):
        qb, kb, vb, gb = xs
        diff = gb[:, :, :, None, :] - gb[:, :, None, :, :]
        decay = jnp.where(causal, jnp.exp(jnp.where(causal, diff, 0.0)), 0.0)
        scores = jnp.einsum('bhtd,bhsd,bhtsd->bhts', qb, kb, decay)
        o = jnp.einsum('bhts,bhsv->bhtv', scores, vb) + jnp.einsum('bhtd,bhdv->bhtv', qb * jnp.exp(gb), state)
        g_last = gb[:, :, -1, :]
        new_state = jnp.exp(g_last)[..., None] * state + jnp.einsum('bhsd,bhsv->bhdv', kb * jnp.exp(g_last[:, :, None, :] - gb), vb)
        return new_state, o

    state0 = jnp.zeros((b, h, dk, dv), jnp.float32)
    _, o = lax.scan(step, state0, (qc, kc, vc, g_cum))
    return o.transpose(1, 2, 0, 3, 4).reshape(b, h, s, dv)


def _hybrid_mixer(h, w_in, w_out, q_norm_g, k_norm_g, hgrn_norm_g, lower_bound):
    dt = h.dtype
    proj = h @ w_in
    offsets = []
    acc = 0
    for wdt in IN_SPLITS[:-1]:
        acc += wdt
        offsets.append(acc)
    qa, ka, va, qd, kd, vd, qh, fh, ih, gh = jnp.split(proj, offsets, axis=-1)

    o_a = _stick_breaking(_split_heads(qa, SB_HEADS), _split_heads(ka, SB_HEADS), _split_heads(va, SB_HEADS))

    qd = _rope(_rmsnorm(_split_heads(qd, DIL_HEADS)) * q_norm_g) * (HEAD_DIM ** -0.5)
    kd = _rope(_rmsnorm(_split_heads(kd, DIL_HEADS)) * k_norm_g)
    o_d = _dilated_mixture(qd, kd, _split_heads(vd, DIL_HEADS))

    lb = lower_bound.reshape(HGRN_HEADS, 1, HGRN_DK).astype(jnp.float32)
    z = _split_heads(fh, HGRN_HEADS).astype(jnp.float32)
    log_f = jnp.logaddexp(jnp.log(jnp.maximum(lb, LB_FLOOR)), jnp.log1p(-lb) + jax.nn.log_sigmoid(z))
    k_h = -jnp.expm1(log_f)
    q_h = jax.nn.silu(_split_heads(qh, HGRN_HEADS))
    o_h = _hgrn2_chunkwise(q_h, k_h, _split_heads(ih, HGRN_HEADS), log_f)
    o_h = _rmsnorm(o_h) * hgrn_norm_g * jax.nn.silu(_split_heads(gh, HGRN_HEADS).astype(jnp.float32))

    y = jnp.concatenate([_merge_heads(o_a).astype(dt), _merge_heads(o_d).astype(dt), _merge_heads(o_h).astype(dt)], axis=-1)
    return y @ w_out


def _fwd_setup_inputs(seed: int = 0) -> dict:
    key = jax.random.key(seed)
    ks = jax.random.split(key, 16)

    def nrm(k, shape, scale):
        return jax.random.normal(k, shape, jnp.float32) * scale

    return {
        "x": nrm(ks[0], (BATCH, SEQ, D_MODEL), 1.0),
        "c": nrm(ks[1], (BATCH, D_MODEL), 1.0),
        "w_mod": nrm(ks[2], (DEPTH, D_MODEL, N_MOD * D_MODEL), D_MODEL ** -0.5),
        "b_mod": nrm(ks[3], (DEPTH, N_MOD * D_MODEL), 0.02),
        "ffn1_w_gate": nrm(ks[4], (DEPTH, D_MODEL, D_FF), D_MODEL ** -0.5),
        "ffn1_w_up": nrm(ks[5], (DEPTH, D_MODEL, D_FF), D_MODEL ** -0.5),
        "ffn1_w_down": nrm(ks[6], (DEPTH, D_FF, D_MODEL), D_FF ** -0.5),
        "w_in": nrm(ks[7], (DEPTH, D_MODEL, IN_WIDTH), D_MODEL ** -0.5),
        "w_out": nrm(ks[8], (DEPTH, MIX_WIDTH, D_MODEL), MIX_WIDTH ** -0.5),
        "q_norm_g": 1.0 + nrm(ks[9], (DEPTH, HEAD_DIM), 0.02),
        "k_norm_g": 1.0 + nrm(ks[10], (DEPTH, HEAD_DIM), 0.02),
        "hgrn_norm_g": 1.0 + nrm(ks[11], (DEPTH, HGRN_DV), 0.02),
        "hgrn_lb_logits": nrm(ks[12], (DEPTH, HGRN_KDIM), 0.5),
        "ffn2_w_gate": nrm(ks[13], (DEPTH, D_MODEL, D_FF), D_MODEL ** -0.5),
        "ffn2_w_up": nrm(ks[14], (DEPTH, D_MODEL, D_FF), D_MODEL ** -0.5),
        "ffn2_w_down": nrm(ks[15], (DEPTH, D_FF, D_MODEL), D_FF ** -0.5),
    }


def _fwd_reference(x, c, w_mod, b_mod, ffn1_w_gate, ffn1_w_up, ffn1_w_down, w_in, w_out, q_norm_g, k_norm_g, hgrn_norm_g, hgrn_lb_logits, ffn2_w_gate, ffn2_w_up, ffn2_w_down):
    lb_sm = jax.nn.softmax(hgrn_lb_logits.astype(jnp.float32), axis=0)
    lower_bounds = jnp.clip(jnp.cumsum(lb_sm, axis=0) - lb_sm[0:1], 0.0, 1.0 - EPS)
    for l in range(DEPTH):
        mod = jax.nn.silu(c) @ w_mod[l] + b_mod[l]
        sh1, sc1, g1, sh2, sc2, g2, sh3, sc3, g3 = jnp.split(mod[:, None, :], N_MOD, axis=-1)
        h = _rmsnorm(x) * (1.0 + sc1) + sh1
        x = x + HALF_STEP * g1 * _swiglu(h, ffn1_w_gate[l], ffn1_w_up[l], ffn1_w_down[l])
        h = _rmsnorm(x) * (1.0 + sc2) + sh2
        x = x + g2 * _hybrid_mixer(h, w_in[l], w_out[l], q_norm_g[l], k_norm_g[l], hgrn_norm_g[l], lower_bounds[l])
        h = _rmsnorm(x) * (1.0 + sc3) + sh3
        x = x + HALF_STEP * g3 * _swiglu(h, ffn2_w_gate[l], ffn2_w_up[l], ffn2_w_down[l])
    return x


import jax as _jax
import jax.numpy as _jnp

TWIN_FORMAT = 'train_step'
FWD_PARAMS = ['x', 'c', 'w_mod', 'b_mod', 'ffn1_w_gate', 'ffn1_w_up', 'ffn1_w_down', 'w_in', 'w_out', 'q_norm_g', 'k_norm_g', 'hgrn_norm_g', 'hgrn_lb_logits', 'ffn2_w_gate', 'ffn2_w_up', 'ffn2_w_down']
TWIN_WEIGHTS = ['w_mod', 'b_mod', 'ffn1_w_gate', 'ffn1_w_up', 'ffn1_w_down', 'w_in', 'w_out', 'q_norm_g', 'k_norm_g', 'hgrn_norm_g', 'hgrn_lb_logits', 'ffn2_w_gate', 'ffn2_w_up', 'ffn2_w_down']
TWIN_DIFF_INPUT = 'x'
TWIN_INPUTS = ['x', 'c', 'w_mod', 'b_mod', 'ffn1_w_gate', 'ffn1_w_up', 'ffn1_w_down', 'w_in', 'w_out', 'q_norm_g', 'k_norm_g', 'hgrn_norm_g', 'hgrn_lb_logits', 'ffn2_w_gate', 'ffn2_w_up', 'ffn2_w_down', 'loss_target', 'm_w_mod', 'm_b_mod', 'm_ffn1_w_gate', 'm_ffn1_w_up', 'm_ffn1_w_down', 'm_w_in', 'm_w_out', 'm_q_norm_g', 'm_k_norm_g', 'm_hgrn_norm_g', 'm_hgrn_lb_logits', 'm_ffn2_w_gate', 'm_ffn2_w_up', 'm_ffn2_w_down', 'v_w_mod', 'v_b_mod', 'v_ffn1_w_gate', 'v_ffn1_w_up', 'v_ffn1_w_down', 'v_w_in', 'v_w_out', 'v_q_norm_g', 'v_k_norm_g', 'v_hgrn_norm_g', 'v_hgrn_lb_logits', 'v_ffn2_w_gate', 'v_ffn2_w_up', 'v_ffn2_w_down']
TWIN_OUTPUTS = ['loss', 'grad_x', 'grad_w_mod', 'grad_b_mod', 'grad_ffn1_w_gate', 'grad_ffn1_w_up', 'grad_ffn1_w_down', 'grad_w_in', 'grad_w_out', 'grad_q_norm_g', 'grad_k_norm_g', 'grad_hgrn_norm_g', 'grad_hgrn_lb_logits', 'grad_ffn2_w_gate', 'grad_ffn2_w_up', 'grad_ffn2_w_down', 'delta_w_mod', 'delta_b_mod', 'delta_ffn1_w_gate', 'delta_ffn1_w_up', 'delta_ffn1_w_down', 'delta_w_in', 'delta_w_out', 'delta_q_norm_g', 'delta_k_norm_g', 'delta_hgrn_norm_g', 'delta_hgrn_lb_logits', 'delta_ffn2_w_gate', 'delta_ffn2_w_up', 'delta_ffn2_w_down', 'new_m_w_mod', 'new_m_b_mod', 'new_m_ffn1_w_gate', 'new_m_ffn1_w_up', 'new_m_ffn1_w_down', 'new_m_w_in', 'new_m_w_out', 'new_m_q_norm_g', 'new_m_k_norm_g', 'new_m_hgrn_norm_g', 'new_m_hgrn_lb_logits', 'new_m_ffn2_w_gate', 'new_m_ffn2_w_up', 'new_m_ffn2_w_down', 'new_v_w_mod', 'new_v_b_mod', 'new_v_ffn1_w_gate', 'new_v_ffn1_w_up', 'new_v_ffn1_w_down', 'new_v_w_in', 'new_v_w_out', 'new_v_q_norm_g', 'new_v_k_norm_g', 'new_v_hgrn_norm_g', 'new_v_hgrn_lb_logits', 'new_v_ffn2_w_gate', 'new_v_ffn2_w_up', 'new_v_ffn2_w_down']
TWIN_LEAF_KINDS = {'loss': 'loss', 'grad_x': 'grad_x', 'grad_w_mod': 'grad_w', 'grad_b_mod': 'grad_w', 'grad_ffn1_w_gate': 'grad_w', 'grad_ffn1_w_up': 'grad_w', 'grad_ffn1_w_down': 'grad_w', 'grad_w_in': 'grad_w', 'grad_w_out': 'grad_w', 'grad_q_norm_g': 'grad_w', 'grad_k_norm_g': 'grad_w', 'grad_hgrn_norm_g': 'grad_w', 'grad_hgrn_lb_logits': 'grad_w', 'grad_ffn2_w_gate': 'grad_w', 'grad_ffn2_w_up': 'grad_w', 'grad_ffn2_w_down': 'grad_w', 'delta_w_mod': 'delta_w', 'delta_b_mod': 'delta_w', 'delta_ffn1_w_gate': 'delta_w', 'delta_ffn1_w_up': 'delta_w', 'delta_ffn1_w_down': 'delta_w', 'delta_w_in': 'delta_w', 'delta_w_out': 'delta_w', 'delta_q_norm_g': 'delta_w', 'delta_k_norm_g': 'delta_w', 'delta_hgrn_norm_g': 'delta_w', 'delta_hgrn_lb_logits': 'delta_w', 'delta_ffn2_w_gate': 'delta_w', 'delta_ffn2_w_up': 'delta_w', 'delta_ffn2_w_down': 'delta_w', 'new_m_w_mod': 'new_m', 'new_m_b_mod': 'new_m', 'new_m_ffn1_w_gate': 'new_m', 'new_m_ffn1_w_up': 'new_m', 'new_m_ffn1_w_down': 'new_m', 'new_m_w_in': 'new_m', 'new_m_w_out': 'new_m', 'new_m_q_norm_g': 'new_m', 'new_m_k_norm_g': 'new_m', 'new_m_hgrn_norm_g': 'new_m', 'new_m_hgrn_lb_logits': 'new_m', 'new_m_ffn2_w_gate': 'new_m', 'new_m_ffn2_w_up': 'new_m', 'new_m_ffn2_w_down': 'new_m', 'new_v_w_mod': 'new_v', 'new_v_b_mod': 'new_v', 'new_v_ffn1_w_gate': 'new_v', 'new_v_ffn1_w_up': 'new_v', 'new_v_ffn1_w_down': 'new_v', 'new_v_w_in': 'new_v', 'new_v_w_out': 'new_v', 'new_v_q_norm_g': 'new_v', 'new_v_k_norm_g': 'new_v', 'new_v_hgrn_norm_g': 'new_v', 'new_v_hgrn_lb_logits': 'new_v', 'new_v_ffn2_w_gate': 'new_v', 'new_v_ffn2_w_up': 'new_v', 'new_v_ffn2_w_down': 'new_v'}


def _forward(args):
    return _fwd_reference(*[args[k] for k in FWD_PARAMS])


def _output_shape():
    def fwd():
        inp = _fwd_setup_inputs(0)
        return _fwd_reference(*[inp[k] for k in FWD_PARAMS])
    out = _jax.eval_shape(fwd)
    return out.shape, out.dtype

N_MICROBATCH = 1
ADAM_LR = 0.001
ADAM_B1 = 0.9
ADAM_B2 = 0.999
ADAM_EPS = 1e-08
ADAM_WD = 0.01
ADAM_STEP = 10
PER_EXAMPLE_BATCH_AXIS = {'x': 0, 'c': 0, 'loss_target': 0}
SHARED_INPUTS = []
_WEIGHT_DTYPES = {'w_mod': _jnp.float32, 'b_mod': _jnp.float32, 'ffn1_w_gate': _jnp.float32, 'ffn1_w_up': _jnp.float32, 'ffn1_w_down': _jnp.float32, 'w_in': _jnp.float32, 'w_out': _jnp.float32, 'q_norm_g': _jnp.float32, 'k_norm_g': _jnp.float32, 'hgrn_norm_g': _jnp.float32, 'hgrn_lb_logits': _jnp.float32, 'ffn2_w_gate': _jnp.float32, 'ffn2_w_up': _jnp.float32, 'ffn2_w_down': _jnp.float32}
MOMENT_SCALE = {'w_mod': 8.792798e+00, 'b_mod': 1.919220e+01, 'ffn1_w_gate': 1.506877e+00, 'ffn1_w_up': 1.848256e+00, 'ffn1_w_down': 2.917610e+00, 'w_in': 7.781200e+00, 'w_out': 1.076522e+01, 'q_norm_g': 1.221866e+00, 'k_norm_g': 1.200504e+00, 'hgrn_norm_g': 1.313659e+02, 'hgrn_lb_logits': 6.944130e-02, 'ffn2_w_gate': 1.626358e+00, 'ffn2_w_up': 1.711016e+00, 'ffn2_w_down': 2.608949e+00}


def _to_microbatches(a, axis):
    t = _jnp.moveaxis(a, axis, 0)
    t = t.reshape((N_MICROBATCH, t.shape[0] // N_MICROBATCH) + t.shape[1:])
    return _jnp.moveaxis(t, 1, axis + 1)


def setup_inputs(seed: int = 0) -> dict:
    inp = _fwd_setup_inputs(seed)
    key = _jax.random.fold_in(_jax.random.key(seed), 7919)
    shape, _ = _output_shape()
    out = dict(inp)
    out["loss_target"] = _jax.random.normal(_jax.random.fold_in(key, 0), shape, _jnp.float32)
    for i, name in enumerate(TWIN_WEIGHTS):
        w = inp[name].astype(_jnp.float32)
        if MOMENT_SCALE is None:
            s = _jnp.sqrt(_jnp.mean(_jnp.square(w)) + 1e-30)
        else:
            s = MOMENT_SCALE[name]
        km, kv = _jax.random.split(_jax.random.fold_in(key, i + 1))
        out[name] = w
        out["m_" + name] = s * _jax.random.normal(km, w.shape, _jnp.float32)
        out["v_" + name] = (s * s) * _jax.random.uniform(kv, w.shape, _jnp.float32, 0.5, 1.5)
    if N_MICROBATCH > 1:
        for name, axis in PER_EXAMPLE_BATCH_AXIS.items():
            out[name] = _to_microbatches(out[name], axis)
    return {'x': out['x'], 'c': out['c'], 'w_mod': out['w_mod'], 'b_mod': out['b_mod'], 'ffn1_w_gate': out['ffn1_w_gate'], 'ffn1_w_up': out['ffn1_w_up'], 'ffn1_w_down': out['ffn1_w_down'], 'w_in': out['w_in'], 'w_out': out['w_out'], 'q_norm_g': out['q_norm_g'], 'k_norm_g': out['k_norm_g'], 'hgrn_norm_g': out['hgrn_norm_g'], 'hgrn_lb_logits': out['hgrn_lb_logits'], 'ffn2_w_gate': out['ffn2_w_gate'], 'ffn2_w_up': out['ffn2_w_up'], 'ffn2_w_down': out['ffn2_w_down'], 'loss_target': out['loss_target'], 'm_w_mod': out['m_w_mod'], 'm_b_mod': out['m_b_mod'], 'm_ffn1_w_gate': out['m_ffn1_w_gate'], 'm_ffn1_w_up': out['m_ffn1_w_up'], 'm_ffn1_w_down': out['m_ffn1_w_down'], 'm_w_in': out['m_w_in'], 'm_w_out': out['m_w_out'], 'm_q_norm_g': out['m_q_norm_g'], 'm_k_norm_g': out['m_k_norm_g'], 'm_hgrn_norm_g': out['m_hgrn_norm_g'], 'm_hgrn_lb_logits': out['m_hgrn_lb_logits'], 'm_ffn2_w_gate': out['m_ffn2_w_gate'], 'm_ffn2_w_up': out['m_ffn2_w_up'], 'm_ffn2_w_down': out['m_ffn2_w_down'], 'v_w_mod': out['v_w_mod'], 'v_b_mod': out['v_b_mod'], 'v_ffn1_w_gate': out['v_ffn1_w_gate'], 'v_ffn1_w_up': out['v_ffn1_w_up'], 'v_ffn1_w_down': out['v_ffn1_w_down'], 'v_w_in': out['v_w_in'], 'v_w_out': out['v_w_out'], 'v_q_norm_g': out['v_q_norm_g'], 'v_k_norm_g': out['v_k_norm_g'], 'v_hgrn_norm_g': out['v_hgrn_norm_g'], 'v_hgrn_lb_logits': out['v_hgrn_lb_logits'], 'v_ffn2_w_gate': out['v_ffn2_w_gate'], 'v_ffn2_w_up': out['v_ffn2_w_up'], 'v_ffn2_w_down': out['v_ffn2_w_down']}


def _loss(weights, diff, rest, loss_target):
    with _jax.named_scope("forward"):
        args = {**rest, TWIN_DIFF_INPUT: diff, **{k: w.astype(_WEIGHT_DTYPES[k]) for k, w in weights.items()}}
        y = _forward(args)
    with _jax.named_scope("loss_head"):
        err = _jnp.square(y.astype(_jnp.float32) - loss_target)
        return 0.5 * _jnp.sum(_jnp.mean(err, axis=-1)) if err.ndim else 0.5 * err


def _adamw(w, g, m, v):
    m = ADAM_B1 * m + (1.0 - ADAM_B1) * g
    v = ADAM_B2 * v + (1.0 - ADAM_B2) * _jnp.square(g)
    m_hat = m / (1.0 - ADAM_B1 ** ADAM_STEP)
    v_hat = v / (1.0 - ADAM_B2 ** ADAM_STEP)
    delta = -ADAM_LR * (m_hat / (_jnp.sqrt(v_hat) + ADAM_EPS) + ADAM_WD * w)
    return delta, m, v


def reference(x, c, w_mod, b_mod, ffn1_w_gate, ffn1_w_up, ffn1_w_down, w_in, w_out, q_norm_g, k_norm_g, hgrn_norm_g, hgrn_lb_logits, ffn2_w_gate, ffn2_w_up, ffn2_w_down, loss_target, m_w_mod, m_b_mod, m_ffn1_w_gate, m_ffn1_w_up, m_ffn1_w_down, m_w_in, m_w_out, m_q_norm_g, m_k_norm_g, m_hgrn_norm_g, m_hgrn_lb_logits, m_ffn2_w_gate, m_ffn2_w_up, m_ffn2_w_down, v_w_mod, v_b_mod, v_ffn1_w_gate, v_ffn1_w_up, v_ffn1_w_down, v_w_in, v_w_out, v_q_norm_g, v_k_norm_g, v_hgrn_norm_g, v_hgrn_lb_logits, v_ffn2_w_gate, v_ffn2_w_up, v_ffn2_w_down):
    given = dict(x=x, c=c, w_mod=w_mod, b_mod=b_mod, ffn1_w_gate=ffn1_w_gate, ffn1_w_up=ffn1_w_up, ffn1_w_down=ffn1_w_down, w_in=w_in, w_out=w_out, q_norm_g=q_norm_g, k_norm_g=k_norm_g, hgrn_norm_g=hgrn_norm_g, hgrn_lb_logits=hgrn_lb_logits, ffn2_w_gate=ffn2_w_gate, ffn2_w_up=ffn2_w_up, ffn2_w_down=ffn2_w_down, loss_target=loss_target, m_w_mod=m_w_mod, m_b_mod=m_b_mod, m_ffn1_w_gate=m_ffn1_w_gate, m_ffn1_w_up=m_ffn1_w_up, m_ffn1_w_down=m_ffn1_w_down, m_w_in=m_w_in, m_w_out=m_w_out, m_q_norm_g=m_q_norm_g, m_k_norm_g=m_k_norm_g, m_hgrn_norm_g=m_hgrn_norm_g, m_hgrn_lb_logits=m_hgrn_lb_logits, m_ffn2_w_gate=m_ffn2_w_gate, m_ffn2_w_up=m_ffn2_w_up, m_ffn2_w_down=m_ffn2_w_down, v_w_mod=v_w_mod, v_b_mod=v_b_mod, v_ffn1_w_gate=v_ffn1_w_gate, v_ffn1_w_up=v_ffn1_w_up, v_ffn1_w_down=v_ffn1_w_down, v_w_in=v_w_in, v_w_out=v_w_out, v_q_norm_g=v_q_norm_g, v_k_norm_g=v_k_norm_g, v_hgrn_norm_g=v_hgrn_norm_g, v_hgrn_lb_logits=v_hgrn_lb_logits, v_ffn2_w_gate=v_ffn2_w_gate, v_ffn2_w_up=v_ffn2_w_up, v_ffn2_w_down=v_ffn2_w_down)
    weights = {n: given[n] for n in TWIN_WEIGHTS}
    shared = {n: given[n] for n in SHARED_INPUTS}
    per_example = {n: given[n] for n in ['x', 'c']}
    grad_fn = _jax.value_and_grad(_loss, argnums=(0, 1))

    def one_microbatch(ex, loss_target):
        ex = dict(ex)
        diff = ex.pop(TWIN_DIFF_INPUT)
        return grad_fn(weights, diff, {**shared, **ex}, loss_target)

    if N_MICROBATCH == 1:
        loss, (grad_w, grad_x) = one_microbatch(per_example, given["loss_target"])
    else:
        def body(carry, xs):
            loss_sum, grad_sum = carry
            l_k, (gw_k, gx_k) = one_microbatch(xs[0], xs[1])
            with _jax.named_scope("update"):
                return (loss_sum + l_k, _jax.tree.map(_jnp.add, grad_sum, gw_k)), gx_k

        init = (_jnp.zeros((), _jnp.float32), _jax.tree.map(_jnp.zeros_like, weights))
        (loss, grad_w), grad_x = _jax.lax.scan(body, init, (per_example, given["loss_target"]))
    with _jax.named_scope("update"):
        delta_w, new_m, new_v = {}, {}, {}
        for n in TWIN_WEIGHTS:
            delta_w[n], new_m[n], new_v[n] = _adamw(weights[n], grad_w[n], given["m_" + n], given["v_" + n])
    return (loss, grad_x, *[grad_w[n] for n in TWIN_WEIGHTS], *[delta_w[n] for n in TWIN_WEIGHTS],
            *[new_m[n] for n in TWIN_WEIGHTS], *[new_v[n] for n in TWIN_WEIGHTS])
```

```python
import functools
import math

import jax
import jax.numpy as jnp
from jax import lax
from jax.experimental import pallas as pl
from jax.experimental.pallas import tpu as pltpu

F32 = jnp.float32
BF16 = jnp.bfloat16
MESH_ID = pl.DeviceIdType.MESH

HEAD_DIM = 64
SB_W = 256
DIL_W = 256
HG_W = 512
HG_D = 128
IN_W = 3 * SB_W + 3 * DIL_W + 4 * HG_W
MIX_W = SB_W + DIL_W + HG_W
DIL_PATTERNS = ((128, 1), (512, 4), (2048, 16))
ROPE_THETA = 10000.0
EPS = 1e-6
LB_FLOOR = 1e-30
NEG_BIG = -1e30
N_MOD = 9
ADAM_LR = 0.001
ADAM_B1 = 0.9
ADAM_B2 = 0.999
ADAM_EPS = 1e-08
ADAM_WD = 0.01
ADAM_STEP = 10

LANES = 128
QBLK = 128
HG_BLK = 16
HG_TILE = 256
SB_EXIT = 104.0
VMEM_LIMIT = 48 * 1024 * 1024

NN = (((1,), (0,)), ((), ()))
NT = (((1,), (1,)), ((), ()))
TN = (((0,), (0,)), ((), ()))


def _pcall(body, **kw):
    return pl.pallas_call(body, **kw)


def _cparams(*sem):
    return pltpu.CompilerParams(dimension_semantics=sem if sem else None, vmem_limit_bytes=VMEM_LIMIT)


def _dot(a, b, dims=NN):
    return lax.dot_general(a, b, dims, preferred_element_type=F32)


def _split(x, n):
    parts = []
    r = x
    for i in range(n):
        p = r.astype(BF16)
        parts.append(p)
        if i + 1 < n:
            r = r - p.astype(F32)
    return parts


def _xdot(x, m, n=2):
    return sum(_dot(p, m) for p in _split(x, n))


def _xdot_left(m, x, n=3):
    return sum(_dot(m, p) for p in _split(x, n))


def _iota(shape, dim):
    return lax.broadcasted_iota(jnp.int32, shape, dim)


def _sigmoid(x):
    return 1.0 / (1.0 + jnp.exp(-x))


def _tile(dim, pref, mult=LANES):
    t = (min(pref, dim) // mult) * mult
    while t >= mult:
        if dim % t == 0:
            return t
        t -= mult
    return dim


def _rows(dim, pref):
    return _tile(dim, pref, 8)


def _mm(a, b, *, name, tb=False, tm=512, tn=1024, tk=1024, out_dtype=F32, res=None, scale=None):
    m, kd = a.shape
    n = b.shape[0] if tb else b.shape[1]
    tm, tn, tk = _rows(m, tm), _tile(n, tn), _tile(kd, tk)
    nk = kd // tk
    epi = res is not None

    def body(*refs):
        if epi:
            a_ref, b_ref, r_ref, s_ref, o_ref, x_ref, acc = refs
        else:
            a_ref, b_ref, o_ref, acc = refs
        k = pl.program_id(2)

        @pl.when(k == 0)
        def _():
            acc[...] = jnp.zeros_like(acc)

        acc[...] += _dot(a_ref[...], b_ref[...], NT if tb else NN)

        @pl.when(k == nk - 1)
        def _():
            o_ref[...] = acc[...].astype(o_ref.dtype)
            if epi:
                x_ref[...] = r_ref[...] + s_ref[...] * acc[...]

    in_specs = [
        pl.BlockSpec((tm, tk), lambda i, j, k: (i, k)),
        pl.BlockSpec((tn, tk), lambda i, j, k: (j, k)) if tb else pl.BlockSpec((tk, tn), lambda i, j, k: (k, j)),
    ]
    out_shape = [jax.ShapeDtypeStruct((m, n), out_dtype)]
    out_specs = [pl.BlockSpec((tm, tn), lambda i, j, k: (i, j))]
    args = [a, b]
    if epi:
        in_specs += [pl.BlockSpec((tm, tn), lambda i, j, k: (i, j)), pl.BlockSpec((1, tn), lambda i, j, k: (0, j))]
        out_shape.append(jax.ShapeDtypeStruct((m, n), F32))
        out_specs.append(pl.BlockSpec((tm, tn), lambda i, j, k: (i, j)))
        args += [res, scale]
    out = _pcall(
        body, name=name, grid=(m // tm, n // tn, nk), in_specs=in_specs, out_specs=out_specs, out_shape=out_shape,
        scratch_shapes=[pltpu.VMEM((tm, tn), F32)], compiler_params=_cparams("parallel", "parallel", "arbitrary"),
    )(*args)
    return out if epi else out[0]


def _mm_tn(a, b, *, name, tm=1024, tn=1408, tk=512):
    s, m = a.shape
    n = b.shape[1]
    tm, tn, tk = _tile(m, tm), _tile(n, tn), _rows(s, tk)
    nk = s // tk

    def body(a_ref, b_ref, o_ref, acc):
        k = pl.program_id(2)

        @pl.when(k == 0)
        def _():
            acc[...] = jnp.zeros_like(acc)

        acc[...] += _dot(a_ref[...], b_ref[...], TN)

        @pl.when(k == nk - 1)
        def _():
            o_ref[...] = acc[...]

    return _pcall(
        body, name=name, grid=(m // tm, n // tn, nk),
        in_specs=[pl.BlockSpec((tk, tm), lambda i, j, k: (k, i)), pl.BlockSpec((tk, tn), lambda i, j, k: (k, j))],
        out_specs=pl.BlockSpec((tm, tn), lambda i, j, k: (i, j)), out_shape=jax.ShapeDtypeStruct((m, n), F32),
        scratch_shapes=[pltpu.VMEM((tm, tn), F32)], compiler_params=_cparams("parallel", "parallel", "arbitrary"),
    )(a, b)


TE = 512


def _row_spec(t, w, col=0):
    return pl.BlockSpec((t, w), lambda i, col=col: (i, col))


def _vec_spec(w, col=0):
    return pl.BlockSpec((1, w), lambda i, col=col: (0, col))


def _norm_mod(x, sc, sh, *, name):
    s, d = x.shape
    t = _rows(s, TE)

    def body(x_ref, sc_ref, sh_ref, h_ref):
        xv = x_ref[...]
        r = lax.rsqrt(jnp.mean(xv * xv, axis=1, keepdims=True) + EPS)
        h_ref[...] = ((xv * r) * (1.0 + sc_ref[...]) + sh_ref[...]).astype(BF16)

    return _pcall(body, name=name, grid=(s // t,), in_specs=[_row_spec(t, d), _vec_spec(d), _vec_spec(d)],
                  out_specs=_row_spec(t, d), out_shape=jax.ShapeDtypeStruct((s, d), BF16),
                  compiler_params=_cparams("parallel"))(x, sc, sh)


def _norm_mod_bwd(dh, x, sc, dxo, *, name):
    s, d = x.shape
    t = _rows(s, TE)

    def body(dh_ref, x_ref, sc_ref, dxo_ref, dx_ref, dsc_ref, dsh_ref):
        @pl.when(pl.program_id(0) == 0)
        def _():
            dsc_ref[...] = jnp.zeros_like(dsc_ref)
            dsh_ref[...] = jnp.zeros_like(dsh_ref)

        xv = x_ref[...]
        dhv = dh_ref[...]
        r = lax.rsqrt(jnp.mean(xv * xv, axis=1, keepdims=True) + EPS)
        xn = xv * r
        dxn = dhv * (1.0 + sc_ref[...])
        dx_ref[...] = dxo_ref[...] + r * (dxn - xn * jnp.mean(dxn * xn, axis=1, keepdims=True))
        dsc_ref[...] += jnp.sum(dhv * xn, axis=0, keepdims=True)
        dsh_ref[...] += jnp.sum(dhv, axis=0, keepdims=True)

    return _pcall(
        body, name=name, grid=(s // t,),
        in_specs=[_row_spec(t, d), _row_spec(t, d), _vec_spec(d), _row_spec(t, d)],
        out_specs=[_row_spec(t, d), _vec_spec(d), _vec_spec(d)],
        out_shape=[jax.ShapeDtypeStruct((s, d), F32), jax.ShapeDtypeStruct((1, d), F32), jax.ShapeDtypeStruct((1, d), F32)],
        compiler_params=_cparams("arbitrary"))(dh, x, sc, dxo)


def _swiglu(uv, *, name):
    s, f2 = uv.shape
    f = f2 // 2
    t = _rows(s, 256)

    def body(uv_ref, a_ref):
        u = uv_ref[:, :f]
        v = uv_ref[:, f:]
        a_ref[...] = (u * _sigmoid(u) * v).astype(BF16)

    return _pcall(body, name=name, grid=(s // t,), in_specs=[_row_spec(t, f2)], out_specs=_row_spec(t, f),
                  out_shape=jax.ShapeDtypeStruct((s, f), BF16), compiler_params=_cparams("parallel"))(uv)


def _swiglu_bwd(da, uv, *, name):
    s, f2 = uv.shape
    f = f2 // 2
    t = _rows(s, 256)

    def body(da_ref, uv_ref, d_ref):
        u = uv_ref[:, :f]
        v = uv_ref[:, f:]
        dav = da_ref[...]
        sg = _sigmoid(u)
        d_ref[:, :f] = (dav * v * (sg * (1.0 + u * (1.0 - sg)))).astype(BF16)
        d_ref[:, f:] = (dav * (u * sg)).astype(BF16)

    return _pcall(body, name=name, grid=(s // t,), in_specs=[_row_spec(t, f), _row_spec(t, f2)],
                  out_specs=_row_spec(t, f2), out_shape=jax.ShapeDtypeStruct((s, f2), BF16),
                  compiler_params=_cparams("parallel"))(da, uv)


def _gate_bwd(dxo, y, sg, *, name):
    s, d = y.shape
    t = _rows(s, TE)

    def body(dxo_ref, y_ref, sg_ref, dy_ref, ds_ref):
        @pl.when(pl.program_id(0) == 0)
        def _():
            ds_ref[...] = jnp.zeros_like(ds_ref)

        dv = dxo_ref[...]
        dy_ref[...] = (sg_ref[...] * dv).astype(BF16)
        ds_ref[...] += jnp.sum(dv * y_ref[...], axis=0, keepdims=True)

    return _pcall(
        body, name=name, grid=(s // t,), in_specs=[_row_spec(t, d), _row_spec(t, d), _vec_spec(d)],
        out_specs=[_row_spec(t, d), _vec_spec(d)],
        out_shape=[jax.ShapeDtypeStruct((s, d), BF16), jax.ShapeDtypeStruct((1, d), F32)],
        compiler_params=_cparams("arbitrary"))(dxo, y, sg)


def _loss_grad(y, tgt, *, name):
    s, d = y.shape
    t = _rows(s, TE)
    nt = s // t

    def body(y_ref, t_ref, dy_ref, l_ref, acc):
        i = pl.program_id(0)

        @pl.when(i == 0)
        def _():
            acc[...] = jnp.zeros_like(acc)

        e = y_ref[...] - t_ref[...]
        dy_ref[...] = e * (1.0 / d)
        acc[...] += jnp.sum(e * e, axis=0, keepdims=True)

        @pl.when(i == nt - 1)
        def _():
            l_ref[...] = jnp.broadcast_to(jnp.sum(acc[...], axis=1, keepdims=True) * (0.5 / d), l_ref.shape)

    return _pcall(
        body, name=name, grid=(nt,), in_specs=[_row_spec(t, d), _row_spec(t, d)],
        out_specs=[_row_spec(t, d), pl.BlockSpec((1, LANES), lambda i: (0, 0))],
        out_shape=[jax.ShapeDtypeStruct((s, d), F32), jax.ShapeDtypeStruct((1, LANES), F32)],
        scratch_shapes=[pltpu.VMEM((1, d), F32)], compiler_params=_cparams("arbitrary"))(y, tgt)


def _sb_consts():
    row = _iota((QBLK, LANES), 0)
    lane = _iota((QBLK, LANES), 1)
    ones = jnp.ones((QBLK, LANES), BF16)
    after = jnp.concatenate([jnp.where(row > lane, 1.0, 0.0).astype(BF16), ones], axis=1)
    from_ = jnp.concatenate([jnp.where(row >= lane, 1.0, 0.0).astype(BF16), ones], axis=1)
    return row, lane, after, from_


def _sb_scores(qm, kb, strict):
    z = _dot(qm, kb, NT) * (HEAD_DIM ** -0.5)
    sp = jnp.log(1.0 + jnp.exp(-jnp.abs(z)))
    lnb = -(jnp.maximum(z, 0.0) + sp)
    lb = jnp.minimum(z, 0.0) - sp
    if strict is not None:
        lnb = jnp.where(strict, lnb, 0.0)
    return lnb, lb


def _sb_fwd(qkv, *, name):
    s = qkv.shape[0]
    nq = s // QBLK

    def body(q_ref, k_ref, v_ref, o_ref, acc, osc):
        qi = pl.program_id(1)
        row, lane, after, _ = _sb_consts()
        strict = lane < row
        q = q_ref[...]
        outs = []
        for h in range(2):
            hm = (lane < HEAD_DIM) if h == 0 else (lane >= HEAD_DIM)
            qm = jnp.where(hm, q, jnp.zeros_like(q))

            def block(kj, mask, qm=qm):
                off = pl.multiple_of(kj * QBLK, QBLK)
                kb = k_ref[pl.ds(off, QBLK), :]
                vb = v_ref[pl.ds(off, QBLK), :]
                lnb, lb = _sb_scores(qm, kb, mask)
                tr = _xdot(lnb, after)
                w = jnp.exp(lb + tr[:, :QBLK] + acc[...])
                if mask is not None:
                    w = jnp.where(mask, w, 0.0)
                osc[...] += _xdot(w, vb)
                new = acc[...] + tr[:, QBLK:]
                acc[...] = new
                return jnp.max(new)

            acc[...] = jnp.zeros_like(acc)
            osc[...] = jnp.zeros_like(osc)
            top = block(qi, strict)
            lax.while_loop(lambda c: (c[0] >= 0) & (c[1] > -SB_EXIT), lambda c: (c[0] - 1, block(c[0], None)), (qi - 1, top))
            outs.append(osc[...])
        o_ref[...] = jnp.where(lane < HEAD_DIM, outs[0], outs[1])

    return _pcall(
        body, name=name, grid=(2, nq),
        in_specs=[pl.BlockSpec((QBLK, LANES), lambda p, i: (i, p)),
                  pl.BlockSpec((s, LANES), lambda p, i: (0, 2 + p)),
                  pl.BlockSpec((s, LANES), lambda p, i: (0, 4 + p))],
        out_specs=pl.BlockSpec((QBLK, LANES), lambda p, i: (i, p)),
        out_shape=jax.ShapeDtypeStruct((s, SB_W), F32),
        scratch_shapes=[pltpu.VMEM((QBLK, LANES), F32), pltpu.VMEM((QBLK, LANES), F32)],
        compiler_params=_cparams("arbitrary", "arbitrary"))(qkv, qkv, qkv)


def _sb_bwd(qkv, o, dmix, *, name):
    s = qkv.shape[0]
    nq = s // QBLK
    scale = HEAD_DIM ** -0.5

    def body(q_ref, k_ref, v_ref, o_ref, do_ref, dq_ref, dk_ref, dv_ref, acc, racc, dqs, dks, dvs):
        i = pl.program_id(1)
        qi = nq - 1 - i
        row, lane, after, from_ = _sb_consts()
        strict = lane < row

        @pl.when(i == 0)
        def _():
            dks[...] = jnp.zeros_like(dks)
            dvs[...] = jnp.zeros_like(dvs)

        q = q_ref[...]
        do = do_ref[...]
        dob = do.astype(BF16)
        dd = do * o_ref[...]
        dqs[...] = jnp.zeros_like(dqs)
        for h in range(2):
            hm = (lane < HEAD_DIM) if h == 0 else (lane >= HEAD_DIM)
            qm = jnp.where(hm, q, jnp.zeros_like(q))
            dom = jnp.where(hm, dob, jnp.zeros_like(dob))
            dol = jnp.where(hm, (do - dob.astype(F32)).astype(BF16), jnp.zeros_like(dob))
            dsum = jnp.sum(jnp.where(hm, dd, 0.0), axis=1, keepdims=True)

            def block(kj, mask, qm=qm, dom=dom, dsum=dsum, hm=hm):
                off = pl.multiple_of(kj * QBLK, QBLK)
                kb = k_ref[pl.ds(off, QBLK), :]
                vb = v_ref[pl.ds(off, QBLK), :]
                lnb, lb = _sb_scores(qm, kb, mask)
                tr = _xdot(lnb, after)
                w = jnp.exp(lb + tr[:, :QBLK] + acc[...])
                if mask is not None:
                    w = jnp.where(mask, w, 0.0)
                g = w * (_dot(dom, vb, NT) + _dot(dol, vb, NT))
                tg = _xdot(g, from_)
                before = dsum - (tg[:, :QBLK] + racc[...])
                dz = g - jnp.exp(lb) * (g + before)
                if mask is not None:
                    dz = jnp.where(mask, dz, 0.0)
                dzb = (dz * scale).astype(BF16)
                dqs[...] += _dot(dzb, jnp.where(hm, kb, jnp.zeros_like(kb)))
                dks[pl.ds(off, QBLK), :] += _dot(dzb, qm, TN)
                dvs[pl.ds(off, QBLK), :] += _dot(w.astype(BF16), dom, TN)
                new = acc[...] + tr[:, QBLK:]
                acc[...] = new
                racc[...] += tg[:, QBLK:]
                return jnp.max(new)

            acc[...] = jnp.zeros_like(acc)
            racc[...] = jnp.zeros_like(racc)
            top = block(qi, strict)
            lax.while_loop(lambda c: (c[0] >= 0) & (c[1] > -SB_EXIT), lambda c: (c[0] - 1, block(c[0], None)), (qi - 1, top))
        dq_ref[...] = dqs[...]
        fin = pl.multiple_of(qi * QBLK, QBLK)
        dk_ref[...] = dks[pl.ds(fin, QBLK), :]
        dv_ref[...] = dvs[pl.ds(fin, QBLK), :]

    blk = lambda c0: pl.BlockSpec((QBLK, LANES), lambda p, i, c0=c0: (nq - 1 - i, c0 + p))
    return _pcall(
        body, name=name, grid=(2, nq),
        in_specs=[blk(0), pl.BlockSpec((s, LANES), lambda p, i: (0, 2 + p)), pl.BlockSpec((s, LANES), lambda p, i: (0, 4 + p)),
                  blk(0), blk(0)],
        out_specs=[blk(0), blk(0), blk(0)],
        out_shape=[jax.ShapeDtypeStruct((s, SB_W), F32)] * 3,
        scratch_shapes=[pltpu.VMEM((QBLK, LANES), F32), pltpu.VMEM((QBLK, LANES), F32), pltpu.VMEM((QBLK, LANES), F32),
                        pltpu.VMEM((s, LANES), F32), pltpu.VMEM((s, LANES), F32)],
        compiler_params=_cparams("arbitrary", "arbitrary"))(qkv, qkv, qkv, o, dmix)


def _seg_consts():
    r = _iota((LANES, LANES), 0)
    c = _iota((LANES, LANES), 1)
    return jnp.where((r >> 6) == (c >> 6), 1.0, 0.0).astype(BF16)


def _rot_half(x, lane):
    half = HEAD_DIM // 2
    return jnp.where((lane & (HEAD_DIM - 1)) < half, pltpu.roll(x, LANES - half, 1), pltpu.roll(x, half, 1))


def _rope_tables(s):
    half = HEAD_DIM // 2
    inv_freq = ROPE_THETA ** (-jnp.arange(half, dtype=F32) * 2.0 / HEAD_DIM)
    ang = jnp.arange(s, dtype=F32)[:, None] * inv_freq[None, :]
    cos, sin = jnp.cos(ang), jnp.sin(ang)
    return jnp.tile(jnp.concatenate([cos, cos], axis=1), (1, 2)), jnp.tile(jnp.concatenate([-sin, sin], axis=1), (1, 2))


def _dil_prep(proj, gq, gk, cos, sin, *, name):
    s = proj.shape[0]
    t = _rows(s, TE)
    c0 = 3 * SB_W // LANES

    def body(q_ref, k_ref, gq_ref, gk_ref, cos_ref, sin_ref, qo_ref, ko_ref):
        seg = _seg_consts()
        lane = _iota((t, LANES), 1)
        cs, sn = cos_ref[...], sin_ref[...]
        for x_ref, g_ref, o_ref, mul in ((q_ref, gq_ref, qo_ref, HEAD_DIM ** -0.5), (k_ref, gk_ref, ko_ref, 1.0)):
            for j in range(2):
                xv = x_ref[:, j * LANES:(j + 1) * LANES]
                ms = _xdot(xv * xv, seg, 3) * (1.0 / HEAD_DIM)
                xn = xv * lax.rsqrt(ms + EPS) * g_ref[...]
                o_ref[:, j * LANES:(j + 1) * LANES] = ((xn * cs + _rot_half(xn, lane) * sn) * mul).astype(BF16)

    return _pcall(
        body, name=name, grid=(s // t,),
        in_specs=[pl.BlockSpec((t, DIL_W), lambda i: (i, c0 // 2)), pl.BlockSpec((t, DIL_W), lambda i: (i, c0 // 2 + 1)),
                  _vec_spec(LANES), _vec_spec(LANES), _row_spec(t, LANES), _row_spec(t, LANES)],
        out_specs=[_row_spec(t, DIL_W), _row_spec(t, DIL_W)],
        out_shape=[jax.ShapeDtypeStruct((s, DIL_W), BF16)] * 2, compiler_params=_cparams("parallel"))(proj, proj, gq, gk, cos, sin)


def _dil_prep_bwd(proj, gq, gk, cos, sin, dqs, dks, dvs, *, name):
    s = proj.shape[0]
    t = _rows(s, TE)
    c0 = 3 * SB_W // LANES

    def body(q_ref, k_ref, gq_ref, gk_ref, cos_ref, sin_ref, a0, a1, a2, b0, b1, b2, c0_ref, c1_ref, c2_ref,
             dq_ref, dk_ref, dv_ref, dgq_ref, dgk_ref):
        @pl.when(pl.program_id(0) == 0)
        def _():
            dgq_ref[...] = jnp.zeros_like(dgq_ref)
            dgk_ref[...] = jnp.zeros_like(dgk_ref)

        dv_ref[...] = c0_ref[...] + c1_ref[...] + c2_ref[...]
        seg = _seg_consts()
        lane = _iota((t, LANES), 1)
        cs, sn = cos_ref[...], sin_ref[...]
        for x_ref, g_ref, parts, o_ref, dg_ref, mul in ((q_ref, gq_ref, (a0, a1, a2), dq_ref, dgq_ref, HEAD_DIM ** -0.5),
                                                          (k_ref, gk_ref, (b0, b1, b2), dk_ref, dgk_ref, 1.0)):
            for j in range(2):
                sl = slice(j * LANES, (j + 1) * LANES)
                dout = (parts[0][:, sl] + parts[1][:, sl] + parts[2][:, sl]) * mul
                dxn = dout * cs + _rot_half(dout * sn, lane)
                xv = x_ref[:, sl]
                r = lax.rsqrt(_xdot(xv * xv, seg, 3) * (1.0 / HEAD_DIM) + EPS)
                xh = xv * r
                dg_ref[...] += jnp.sum(dxn * xh, axis=0, keepdims=True)
                dxh = dxn * g_ref[...]
                o_ref[:, sl] = r * (dxh - xh * (_xdot(dxh * xh, seg, 3) * (1.0 / HEAD_DIM)))

    rs = _row_spec(t, DIL_W)
    return _pcall(
        body, name=name, grid=(s // t,),
        in_specs=[pl.BlockSpec((t, DIL_W), lambda i: (i, c0 // 2)), pl.BlockSpec((t, DIL_W), lambda i: (i, c0 // 2 + 1)),
                  _vec_spec(LANES), _vec_spec(LANES), _row_spec(t, LANES), _row_spec(t, LANES)] + [rs] * 9,
        out_specs=[rs, rs, rs, _vec_spec(LANES), _vec_spec(LANES)],
        out_shape=[jax.ShapeDtypeStruct((s, DIL_W), F32)] * 3 + [jax.ShapeDtypeStruct((1, LANES), F32)] * 2,
        compiler_params=_cparams("arbitrary"))(proj, proj, gq, gk, cos, sin, *dqs, *dks, *dvs)


def _dil_masks(n):
    row = _iota((QBLK, LANES), 0)
    col = _iota((QBLK, LANES), 1)
    return col <= row, (col >= row) & (n > 0)


def _dil_fwd(q, k, v, r, *, name):
    s = q.shape[0]
    nb = s // (r * QBLK)
    shape = (s // r, r * DIL_W)
    q, k, v = (a.reshape(shape) for a in (q, k, v))

    def body(q_ref, kc_ref, kp_ref, vc_ref, vp_ref, num_ref, den_ref, mx_ref):
        vc_m, vp_m = _dil_masks(pl.program_id(2))
        lane = _iota((QBLK, LANES), 1)
        qv = q_ref[...]
        res = []
        for h in range(2):
            hm = (lane < HEAD_DIM) if h == 0 else (lane >= HEAD_DIM)
            qm = jnp.where(hm, qv, jnp.zeros_like(qv))
            sc = jnp.where(vc_m, _dot(qm, kc_ref[...], NT), NEG_BIG)
            sp = jnp.where(vp_m, _dot(qm, kp_ref[...], NT), NEG_BIG)
            mx = jnp.maximum(jnp.max(sc, axis=1, keepdims=True), jnp.max(sp, axis=1, keepdims=True))
            pc = jnp.exp(sc - mx)
            pp = jnp.exp(sp - mx)
            den = jnp.sum(pc, axis=1, keepdims=True) + jnp.sum(pp, axis=1, keepdims=True)
            num = _dot(pc.astype(BF16), vc_ref[...]) + _dot(pp.astype(BF16), vp_ref[...])
            res.append((num, den, mx))
        h0 = lane < HEAD_DIM
        num_ref[...] = jnp.where(h0, res[0][0], res[1][0])
        den_ref[...] = jnp.where(h0, res[0][1], res[1][1])
        mx_ref[...] = jnp.where(h0, res[0][2], res[1][2])

    cur = pl.BlockSpec((QBLK, LANES), lambda rho, p, n: (n, 2 * rho + p))
    prev = pl.BlockSpec((QBLK, LANES), lambda rho, p, n: (jnp.maximum(n - 1, 0), 2 * rho + p))
    return _pcall(
        body, name=name, grid=(r, 2, nb), in_specs=[cur, cur, prev, cur, prev], out_specs=[cur, cur, cur],
        out_shape=[jax.ShapeDtypeStruct(shape, F32)] * 3,
        compiler_params=_cparams("parallel", "parallel", "arbitrary"))(q, k, k, v, v)


def _dil_bwd(q, k, v, do, mall, zall, delta, r, *, name):
    s = q.shape[0]
    nb = s // (r * QBLK)
    shape = (s // r, r * DIL_W)
    q, k, v, do, mall, zall, delta = (a.reshape(shape) for a in (q, k, v, do, mall, zall, delta))

    def body(q_ref, kc_ref, kp_ref, vc_ref, vp_ref, do_ref, m_ref, z_ref, dl_ref, dq_ref, dk_ref, dv_ref, pk, pv):
        n = pl.program_id(2)
        vc_m, vp_m = _dil_masks(n)
        lane = _iota((QBLK, LANES), 1)

        @pl.when(n == 0)
        def _():
            pk[...] = jnp.zeros_like(pk)
            pv[...] = jnp.zeros_like(pv)

        @pl.when(n < nb)
        def _():
            qv = q_ref[...]
            dob = do_ref[...].astype(BF16)
            dq = jnp.zeros((QBLK, LANES), F32)
            dkc = jnp.zeros((QBLK, LANES), F32)
            dkp = jnp.zeros((QBLK, LANES), F32)
            dvc = jnp.zeros((QBLK, LANES), F32)
            dvp = jnp.zeros((QBLK, LANES), F32)
            for h in range(2):
                hm = (lane < HEAD_DIM) if h == 0 else (lane >= HEAD_DIM)
                zero = jnp.zeros_like(qv)
                qm = jnp.where(hm, qv, zero)
                dom = jnp.where(hm, dob, zero)
                c = h * HEAD_DIM
                mrow, dlrow = m_ref[:, c:c + 1], dl_ref[:, c:c + 1]
                rz = 1.0 / z_ref[:, c:c + 1]
                for k_ref, v_ref, valid, cur in ((kc_ref, vc_ref, vc_m, True), (kp_ref, vp_ref, vp_m, False)):
                    kb, vb = k_ref[...], v_ref[...]
                    sc = jnp.where(valid, _dot(qm, kb, NT), NEG_BIG)
                    p = jnp.exp(sc - mrow) * rz
                    ds = (p * (_dot(dom, vb, NT) - dlrow)).astype(BF16)
                    dq = dq + _dot(ds, jnp.where(hm, kb, zero))
                    dkx = _dot(ds, qm, TN)
                    dvx = _dot(p.astype(BF16), dom, TN)
                    if cur:
                        dkc, dvc = dkc + dkx, dvc + dvx
                    else:
                        dkp, dvp = dkp + dkx, dvp + dvx
            dq_ref[...] = dq
            dk_ref[...] = pk[...] + dkp
            dv_ref[...] = pv[...] + dvp
            pk[...] = dkc
            pv[...] = dvc

        @pl.when(n == nb)
        def _():
            dk_ref[...] = pk[...]
            dv_ref[...] = pv[...]

    last = nb - 1
    cur = pl.BlockSpec((QBLK, LANES), lambda rho, p, n: (jnp.minimum(n, last), 2 * rho + p))
    prev = pl.BlockSpec((QBLK, LANES), lambda rho, p, n: (jnp.maximum(jnp.minimum(n, last) - 1, 0), 2 * rho + p))
    late = pl.BlockSpec((QBLK, LANES), lambda rho, p, n: (jnp.maximum(n - 1, 0), 2 * rho + p))
    return _pcall(
        body, name=name, grid=(r, 2, nb + 1), in_specs=[cur, cur, prev, cur, prev, cur, cur, cur, cur],
        out_specs=[cur, late, late], out_shape=[jax.ShapeDtypeStruct(shape, F32)] * 3,
        scratch_shapes=[pltpu.VMEM((QBLK, LANES), F32)] * 2,
        compiler_params=_cparams("parallel", "parallel", "arbitrary"))(q, k, k, v, v, do, mall, zall, delta)


HG_SHIFT = HG_BLK.bit_length() - 1
HG_Q0, HG_F0, HG_I0 = (3 * SB_W + 3 * DIL_W) // HG_D, (3 * SB_W + 3 * DIL_W + HG_W) // HG_D, (3 * SB_W + 3 * DIL_W + 2 * HG_W) // HG_D


def _hg_inputs(qh, z, v, la, lc, t):
    r = _iota((t, t), 0)
    c = _iota((t, t), 1)
    same = (r >> HG_SHIFT) == (c >> HG_SHIFT)
    tri = jnp.where(same & (c <= r), 1.0, 0.0).astype(BF16)
    blk = jnp.where(same, 1.0, 0.0).astype(BF16)
    lsg = jnp.minimum(z, 0.0) - jnp.log(1.0 + jnp.exp(-jnp.abs(z)))
    b = lc + lsg
    lf = jnp.maximum(la, b) + jnp.log(1.0 + jnp.exp(-jnp.abs(la - b)))
    f = jnp.exp(lf)
    sq = _sigmoid(qh)
    g = _xdot_left(tri, lf)
    gl = _xdot_left(blk, lf)
    return dict(lf=lf, b=b, f=f, k=1.0 - f, sq=sq, qs=qh * sq, g=g, eg=jnp.exp(g), egl=jnp.exp(gl - g), dec=jnp.exp(gl))


def _hgrn_fwd(proj, la, lc, *, name):
    s = proj.shape[0]
    t = _rows(s, HG_TILE)
    nt, nb = s // t, t // HG_BLK

    def body(q_ref, f_ref, i_ref, la_ref, lc_ref, o_ref, st_ref, state, osc):
        @pl.when(pl.program_id(1) == 0)
        def _():
            state[...] = jnp.zeros_like(state)

        v = i_ref[...]
        a = _hg_inputs(q_ref[...], f_ref[...], v, la_ref[...], lc_ref[...], t)
        qs, k, g = a["qs"], a["k"], a["g"]
        rb = _iota((t, LANES), 0) & (HG_BLK - 1)
        o = jnp.sum(qs * k, axis=1, keepdims=True) * v
        for d in range(1, HG_BLK):
            m = rb >= d
            e = jnp.exp(jnp.where(m, g - pltpu.roll(g, d, 0), 0.0))
            cd = jnp.sum(qs * pltpu.roll(k, d, 0) * e, axis=1, keepdims=True)
            o = o + jnp.where(m, cd, 0.0) * pltpu.roll(v, d, 0)
        osc[...] = o
        qt = (qs * a["eg"]).astype(BF16)
        kt = (k * a["egl"]).astype(BF16)
        vb = v.astype(BF16)
        for blk in range(nb):
            sl = slice(blk * HG_BLK, (blk + 1) * HG_BLK)
            st = state[...]
            stb = st.astype(BF16)
            st_ref[blk * HG_D:(blk + 1) * HG_D, :] = stb
            osc[sl, :] += _dot(qt[sl], stb, NT)
            state[...] = a["dec"][blk * HG_BLK:blk * HG_BLK + 1] * st + _dot(vb[sl], kt[sl], TN)
        o_ref[...] = osc[...]

    col = lambda c0: pl.BlockSpec((t, HG_D), lambda hd, i, c0=c0: (i, c0 + hd))
    vec = pl.BlockSpec((1, HG_D), lambda hd, i: (0, hd))
    return _pcall(
        body, name=name, grid=(4, nt), in_specs=[col(HG_Q0), col(HG_F0), col(HG_I0), vec, vec],
        out_specs=[col(0), pl.BlockSpec((None, nb * HG_D, HG_D), lambda hd, i: (hd, i, 0))],
        out_shape=[jax.ShapeDtypeStruct((s, HG_W), F32), jax.ShapeDtypeStruct((4, s // HG_BLK * HG_D, HG_D), BF16)],
        scratch_shapes=[pltpu.VMEM((HG_D, HG_D), F32), pltpu.VMEM((t, HG_D), F32)],
        compiler_params=_cparams("arbitrary", "arbitrary"))(proj, proj, proj, la, lc)


def _hgrn_bwd(proj, la, lc, st, doh, *, name):
    s = proj.shape[0]
    t = _rows(s, HG_TILE)
    nt, nb = s // t, t // HG_BLK

    def body(q_ref, f_ref, i_ref, la_ref, lc_ref, st_ref, do_ref, dq_ref, df_ref, di_ref, dla_ref, dlc_ref, dstate):
        @pl.when(pl.program_id(1) == 0)
        def _():
            dstate[...] = jnp.zeros_like(dstate)
            dla_ref[...] = jnp.zeros_like(dla_ref)
            dlc_ref[...] = jnp.zeros_like(dlc_ref)

        qh, z, v, do = q_ref[...], f_ref[...], i_ref[...], do_ref[...]
        la = la_ref[...]
        a = _hg_inputs(qh, z, v, la, lc_ref[...], t)
        qs, k, g = a["qs"], a["k"], a["g"]
        rb = _iota((t, LANES), 0) & (HG_BLK - 1)
        dc0 = jnp.sum(do * v, axis=1, keepdims=True)
        dq = dc0 * k
        dk = dc0 * qs
        dv = jnp.sum(qs * k, axis=1, keepdims=True) * do
        for d in range(1, HG_BLK):
            m = rb >= d
            e = jnp.exp(jnp.where(m, g - pltpu.roll(g, d, 0), 0.0))
            ks = pltpu.roll(k, d, 0)
            qe = qs * e
            cd = jnp.where(m, jnp.sum(qe * ks, axis=1, keepdims=True), 0.0)
            dcd = jnp.where(m, jnp.sum(do * pltpu.roll(v, d, 0), axis=1, keepdims=True), 0.0)
            dq = dq + dcd * ks * e
            dk = dk + pltpu.roll(dcd * qe, t - d, 0)
            dv = dv + pltpu.roll(cd * do, t - d, 0)
        qt = (qs * a["eg"]).astype(BF16)
        kt = (k * a["egl"]).astype(BF16)
        vb = v.astype(BF16)
        dob = do.astype(BF16)
        dqt, dkt, dvi, dgs = [None] * nb, [None] * nb, [None] * nb, [None] * nb
        for blk in reversed(range(nb)):
            sl = slice(blk * HG_BLK, (blk + 1) * HG_BLK)
            stb = st_ref[blk * HG_D:(blk + 1) * HG_D, :]
            ds1 = dstate[...]
            ds1b = ds1.astype(BF16)
            dec = a["dec"][blk * HG_BLK:blk * HG_BLK + 1]
            dqt[blk] = _dot(dob[sl], stb)
            dkt[blk] = _dot(vb[sl], ds1b)
            dvi[blk] = _dot(kt[sl], ds1b, NT)
            dgs[blk] = jnp.broadcast_to(jnp.sum(ds1 * stb.astype(F32), axis=0, keepdims=True) * dec, (HG_BLK, HG_D))
            dstate[...] = dec * ds1 + _dot(dob[sl], qt[sl], TN)
        dki = jnp.concatenate(dkt, axis=0) * a["egl"]
        dq = dq + jnp.concatenate(dqt, axis=0) * a["eg"]
        dk = dk + dki
        dv = dv + jnp.concatenate(dvi, axis=0)
        r = _iota((t, t), 0)
        c = _iota((t, t), 1)
        same = (r >> HG_SHIFT) == (c >> HG_SHIFT)
        later = jnp.where(same & (c >= r), 1.0, 0.0).astype(BF16)
        whole = jnp.where(same, 1.0, 0.0).astype(BF16)
        dlf = (_xdot_left(later, qs * dq - k * dk) + _xdot_left(whole, k * dki) + jnp.concatenate(dgs, axis=0)
               - a["f"] * dk)
        wb = jnp.exp(a["b"] - a["lf"])
        wa = jnp.exp(la - a["lf"])
        sq = a["sq"]
        dq_ref[...] = dq * (sq * (1.0 + qh * (1.0 - sq)))
        df_ref[...] = dlf * wb * (1.0 - _sigmoid(z))
        di_ref[...] = dv
        dla_ref[...] += jnp.sum(dlf * wa, axis=0, keepdims=True)
        dlc_ref[...] += jnp.sum(dlf * wb, axis=0, keepdims=True)

    col = lambda c0: pl.BlockSpec((t, HG_D), lambda hd, i, c0=c0: (nt - 1 - i, c0 + hd))
    vec = pl.BlockSpec((1, HG_D), lambda hd, i: (0, hd))
    return _pcall(
        body, name=name, grid=(4, nt),
        in_specs=[col(HG_Q0), col(HG_F0), col(HG_I0), vec, vec,
                  pl.BlockSpec((None, nb * HG_D, HG_D), lambda hd, i: (hd, nt - 1 - i, 0)), col(0)],
        out_specs=[col(0), col(0), col(0), vec, vec],
        out_shape=[jax.ShapeDtypeStruct((s, HG_W), F32)] * 3 + [jax.ShapeDtypeStruct((1, HG_W), F32)] * 2,
        scratch_shapes=[pltpu.VMEM((HG_D, HG_D), F32)],
        compiler_params=_cparams("arbitrary", "arbitrary"))(proj, proj, proj, la, lc, st, doh)


GH0 = (IN_W - HG_W) // HG_W


def _mix_out(o_a, nums, dens, mxs, oh, proj, hg, *, name):
    s = o_a.shape[0]
    t = _rows(s, TE)

    def body(oa_ref, n0, n1, n2, d0, d1, d2, m0, m1, m2, oh_ref, gh_ref, hg_ref, y_ref, od_ref, mall_ref, z_ref):
        y_ref[:, :SB_W] = oa_ref[...].astype(BF16)
        m = jnp.maximum(jnp.maximum(m0[...], m1[...]), m2[...])
        num = jnp.zeros((t, DIL_W), F32)
        z = jnp.zeros((t, DIL_W), F32)
        for n_ref, d_ref, m_ref in ((n0, d0, m0), (n1, d1, m1), (n2, d2, m2)):
            sc = jnp.exp(m_ref[...] - m)
            num = num + n_ref[...] * sc
            z = z + d_ref[...] * sc
        od = num / z
        od_ref[...] = od
        mall_ref[...] = m
        z_ref[...] = z
        y_ref[:, SB_W:SB_W + DIL_W] = od.astype(BF16)
        for h in range(4):
            sl = slice(h * HG_D, (h + 1) * HG_D)
            ov = oh_ref[:, sl]
            g = gh_ref[:, sl]
            r = lax.rsqrt(jnp.mean(ov * ov, axis=1, keepdims=True) + EPS)
            y_ref[:, SB_W + DIL_W + h * HG_D:SB_W + DIL_W + (h + 1) * HG_D] = (ov * r * hg_ref[...] * (g * _sigmoid(g))).astype(BF16)

    rd = _row_spec(t, DIL_W)
    return _pcall(
        body, name=name, grid=(s // t,),
        in_specs=[rd] * 10 + [_row_spec(t, HG_W), _row_spec(t, HG_W, GH0), _vec_spec(HG_D)],
        out_specs=[_row_spec(t, MIX_W), rd, rd, rd],
        out_shape=[jax.ShapeDtypeStruct((s, MIX_W), BF16)] + [jax.ShapeDtypeStruct((s, DIL_W), F32)] * 3,
        compiler_params=_cparams("parallel"))(o_a, *nums, *dens, *mxs, oh, proj, hg)


def _mix_out_bwd(dmix, oh, proj, hg, od, *, name):
    s = oh.shape[0]
    t = _rows(s, TE)

    def body(dm_ref, oh_ref, gh_ref, hg_ref, od_ref, doh_ref, dgh_ref, dl_ref, dhg_ref):
        @pl.when(pl.program_id(0) == 0)
        def _():
            dhg_ref[...] = jnp.zeros_like(dhg_ref)

        seg = _seg_consts()
        for j in range(2):
            sl = slice(j * LANES, (j + 1) * LANES)
            dl_ref[:, sl] = _xdot(dm_ref[:, SB_W + j * LANES:SB_W + (j + 1) * LANES] * od_ref[:, sl], seg, 3)
        hgv = hg_ref[...]
        for h in range(4):
            sl = slice(h * HG_D, (h + 1) * HG_D)
            dy = dm_ref[:, SB_W + DIL_W + h * HG_D:SB_W + DIL_W + (h + 1) * HG_D]
            ov = oh_ref[:, sl]
            g = gh_ref[:, sl]
            sg = _sigmoid(g)
            silu = g * sg
            r = lax.rsqrt(jnp.mean(ov * ov, axis=1, keepdims=True) + EPS)
            nrm = ov * r
            dhg_ref[...] += jnp.sum(dy * nrm * silu, axis=0, keepdims=True)
            dgh_ref[:, sl] = dy * nrm * hgv * (sg * (1.0 + g * (1.0 - sg)))
            dn = dy * hgv * silu
            doh_ref[:, sl] = r * (dn - nrm * jnp.mean(dn * nrm, axis=1, keepdims=True))

    rh = _row_spec(t, HG_W)
    return _pcall(
        body, name=name, grid=(s // t,),
        in_specs=[_row_spec(t, MIX_W), rh, _row_spec(t, HG_W, GH0), _vec_spec(HG_D), _row_spec(t, DIL_W)],
        out_specs=[rh, rh, _row_spec(t, DIL_W), _vec_spec(HG_D)],
        out_shape=[jax.ShapeDtypeStruct((s, HG_W), F32)] * 2 + [jax.ShapeDtypeStruct((s, DIL_W), F32), jax.ShapeDtypeStruct((1, HG_D), F32)],
        compiler_params=_cparams("arbitrary"))(dmix, oh, proj, hg, od)


def _lb_terms(l):
    l0, l1 = l[0:1], l[1:2]
    m = jnp.maximum(l0, l1)
    e0, e1 = jnp.exp(l0 - m), jnp.exp(l1 - m)
    s0, s1 = e0 / (e0 + e1), e1 / (e0 + e1)
    args = (s0 - s0, (s0 + s1) - s0)
    lbs = tuple(jnp.minimum(jnp.maximum(a, 0.0), 1.0 - EPS) for a in args)
    return s0, s1, args, lbs


def _lb_prep(logits, *, name):
    def body(l_ref, lb_ref, la_ref, lc_ref):
        _, _, _, lbs = _lb_terms(l_ref[...])
        lb = jnp.concatenate(lbs, axis=0)
        lb_ref[...] = lb
        la_ref[...] = jnp.log(jnp.maximum(lb, LB_FLOOR))
        lc_ref[...] = jnp.log1p(-lb)

    return _pcall(body, name=name, out_shape=[jax.ShapeDtypeStruct(logits.shape, F32)] * 3)(logits)


def _lb_bwd(logits, dla, dlc, *, name):
    def half(hi, eq):
        return jnp.where(hi, 1.0, jnp.where(eq, 0.5, 0.0))

    def body(l_ref, dla_ref, dlc_ref, o_ref):
        s0, s1, args, lbs = _lb_terms(l_ref[...])
        da = []
        for i in range(2):
            a, lb = args[i], lbs[i]
            dlb = dla_ref[i:i + 1] * half(lb > LB_FLOOR, lb == LB_FLOOR) / jnp.maximum(lb, LB_FLOOR) - dlc_ref[i:i + 1] / (1.0 - lb)
            t = jnp.maximum(a, 0.0)
            da.append(dlb * half(a > 0.0, a == 0.0) * half(t < 1.0 - EPS, t == 1.0 - EPS))
        ds0 = (da[0] + da[1]) - (da[0] + da[1])
        ds1 = da[1]
        dot = s0 * ds0 + s1 * ds1
        o_ref[...] = jnp.concatenate([s0 * (ds0 - dot), s1 * (ds1 - dot)], axis=0)

    return _pcall(body, name=name, out_shape=jax.ShapeDtypeStruct(logits.shape, F32))(logits, dla, dlc)


def _mod_fwd(c8, w, b, *, name):
    _, d, n = w.shape
    tn = _tile(n, 768)

    def body(c_ref, w_ref, b_ref, o_ref):
        cv = c_ref[...]
        o_ref[...] = _dot((cv * _sigmoid(cv)).astype(BF16), w_ref[...].astype(BF16)) + b_ref[...]

    return _pcall(
        body, name=name, grid=(2, n // tn),
        in_specs=[pl.BlockSpec((8, d), lambda l, j: (0, 0)), pl.BlockSpec((None, d, tn), lambda l, j: (l, 0, j)),
                  pl.BlockSpec((None, 1, tn), lambda l, j: (l, 0, j))],
        out_specs=pl.BlockSpec((None, 8, tn), lambda l, j: (l, 0, j)),
        out_shape=jax.ShapeDtypeStruct((2, 8, n), F32), compiler_params=_cparams("parallel", "parallel"))(c8, w, b)


def _mod_bwd(ct, dm, *, name):
    d = ct.shape[0]
    n = dm.shape[2]
    tn = _tile(n, 768)

    def body(c_ref, dm_ref, o_ref):
        cv = c_ref[...]
        sc = cv * _sigmoid(cv)
        dv = dm_ref[...]
        acc = sc[:, 0:1] * dv[0:1, :]
        for b in range(1, 8):
            acc = acc + sc[:, b:b + 1] * dv[b:b + 1, :]
        o_ref[...] = acc

    return _pcall(
        body, name=name, grid=(2, n // tn),
        in_specs=[pl.BlockSpec((d, 8), lambda l, j: (0, 0)), pl.BlockSpec((None, 8, tn), lambda l, j: (l, 0, j))],
        out_specs=pl.BlockSpec((None, d, tn), lambda l, j: (l, 0, j)),
        out_shape=jax.ShapeDtypeStruct((2, d, n), F32), compiler_params=_cparams("parallel", "parallel"))(ct, dm)


_PEERS = {
    "chips": ((1, 0, 0), (0, 1, 0), (1, 1, 0)),
    "all": tuple((a, b, c) for a in (0, 1) for b in (0, 1) for c in (0, 1) if a + b + c),
    "sib": ((0, 0, 1),),
}
_SLOTS = {"chips": 4, "all": 8, "sib": 2}


def _slot(kind, x, y, c):
    return {"chips": 2 * x + y, "all": 4 * x + 2 * y + c, "sib": c}[kind]


def _exchange(arrs, kind, scatter, *, name):
    n = len(arrs)
    peers = _PEERS[kind]
    ns = _SLOTS[kind]
    np_ = len(peers)

    def body(*refs):
        ins, outs = refs[:n], refs[n:2 * n]
        send, recv, loc = refs[2 * n:]
        x, y, c = lax.axis_index("x"), lax.axis_index("y"), lax.axis_index("c")
        me = _slot(kind, x, y, c)
        copies = []
        for a in range(n):
            own = pltpu.make_async_copy(ins[a].at[me] if scatter else ins[a], outs[a].at[me], loc.at[a])
            own.start()
            copies.append(own)
            for j, (dx, dy, dc) in enumerate(peers):
                px, py, pc = (1 - x if dx else x), (1 - y if dy else y), (1 - c if dc else c)
                cp = pltpu.make_async_remote_copy(
                    src_ref=ins[a].at[_slot(kind, px, py, pc)] if scatter else ins[a], dst_ref=outs[a].at[me],
                    send_sem=send.at[a * np_ + j], recv_sem=recv.at[a * np_ + j],
                    device_id=(px, py, pc), device_id_type=MESH_ID)
                cp.start()
                copies.append(cp)
        for cp in copies:
            cp.wait()

    hbm = pl.BlockSpec(memory_space=pl.ANY)
    shapes = [jax.ShapeDtypeStruct(a.shape if scatter else (ns,) + a.shape, a.dtype) for a in arrs]
    return _pcall(
        body, name=name, in_specs=[hbm] * n, out_specs=[hbm] * n, out_shape=shapes,
        scratch_shapes=[pltpu.SemaphoreType.DMA((n * np_,)), pltpu.SemaphoreType.DMA((n * np_,)), pltpu.SemaphoreType.DMA((n,))],
    )(*arrs)


def _sum_slots(a, *, name):
    ns, r, c = a.shape
    t = _rows(r, max(8, (1 << 18) // c // 8 * 8))

    def body(a_ref, o_ref):
        acc = a_ref[0]
        for i in range(1, ns):
            acc = acc + a_ref[i]
        o_ref[...] = acc

    return _pcall(body, name=name, grid=(r // t,), in_specs=[pl.BlockSpec((ns, t, c), lambda i: (0, i, 0))],
                  out_specs=pl.BlockSpec((t, c), lambda i: (i, 0)), out_shape=jax.ShapeDtypeStruct((r, c), F32),
                  compiler_params=_cparams("parallel"))(a)


def _adamw(w, gparts, m, v, *, name):
    r, c = w.shape
    t = _rows(r, max(8, (1 << 17) // c // 8 * 8))
    ng = len(gparts)

    def body(*refs):
        w_ref, m_ref, v_ref = refs[0], refs[1 + ng], refs[2 + ng]
        g_ref, d_ref, nm_ref, nv_ref = refs[3 + ng:]
        g = refs[1][...]
        for i in range(1, ng):
            g = g + refs[1 + i][...]
        mn = ADAM_B1 * m_ref[...] + (1.0 - ADAM_B1) * g
        vn = ADAM_B2 * v_ref[...] + (1.0 - ADAM_B2) * (g * g)
        m_hat = mn / (1.0 - ADAM_B1 ** ADAM_STEP)
        v_hat = vn / (1.0 - ADAM_B2 ** ADAM_STEP)
        g_ref[...] = g
        d_ref[...] = -ADAM_LR * (m_hat / (jnp.sqrt(v_hat) + ADAM_EPS) + ADAM_WD * w_ref[...])
        nm_ref[...] = mn
        nv_ref[...] = vn

    spec = pl.BlockSpec((t, c), lambda i: (i, 0))
    return _pcall(body, name=name, grid=(r // t,), in_specs=[spec] * (3 + ng), out_specs=[spec] * 4,
                  out_shape=[jax.ShapeDtypeStruct((r, c), F32)] * 4, compiler_params=_cparams("parallel"))(w, *gparts, m, v)


def _ffn_fwd(x, sh, sc, g, wgu, wd, tag):
    h = _norm_mod(x, sc, sh, name=f"{tag}_norm")
    uv = _mm(h, wgu, name=f"{tag}_up", tm=512, tn=1408, tk=1024)
    a = _swiglu(uv, name=f"{tag}_act")
    y, xo = _mm(a, wd, name=f"{tag}_down", tm=512, tn=1024, tk=1408, res=x, scale=0.5 * g)
    return xo, (x, h, uv, a, y)


def _ffn_bwd(dxo, saved, sc, g, wgu, wd, tag):
    x, h, uv, a, y = saved
    dyb, dgs = _gate_bwd(dxo, y, 0.5 * g, name=f"{tag}_dgate")
    da = _mm(dyb, wd, name=f"{tag}_dact", tb=True, tm=512, tn=1408, tk=1024)
    duv = _swiglu_bwd(da, uv, name=f"{tag}_dswi")
    dh = _mm(duv, wgu, name=f"{tag}_dh", tb=True, tm=512, tn=1024, tk=1408)
    dx, dsc, dsh = _norm_mod_bwd(dh, x, sc, dxo, name=f"{tag}_dnorm")
    dwgu = _mm_tn(h, duv, name=f"{tag}_dwgu", tm=1024, tn=1408, tk=512)
    dwd = _mm_tn(a, dyb, name=f"{tag}_dwd", tm=1408, tn=1024, tk=512)
    return dx, dwgu, dwd, dsh, dsc, 0.5 * dgs


def _layer_fwd(x0, mod, w, par, tag):
    s = x0.shape[0]
    sh1, sc1, g1, sh2, sc2, g2, sh3, sc3, g3 = (mod[i:i + 1] for i in range(N_MOD))
    x1, f1 = _ffn_fwd(x0, sh1, sc1, g1, w["gu1"], w["d1"], f"{tag}_ffn1")
    h2 = _norm_mod(x1, sc2, sh2, name=f"{tag}_mix_norm")
    proj = _mm(h2, w["in"], name=f"{tag}_in", tm=512, tn=512, tk=1024)
    qkv = proj[:, :3 * SB_W].astype(BF16)
    vd = proj[:, 3 * SB_W + 2 * DIL_W:3 * SB_W + 3 * DIL_W].astype(BF16)
    o_a = _sb_fwd(qkv, name=f"{tag}_sb")
    qd, kd = _dil_prep(proj, par["gq"], par["gk"], par["cos"], par["sin"], name=f"{tag}_dil_prep")
    nums, dens, mxs = [], [], []
    for _, r in DIL_PATTERNS:
        nu, de, mx = _dil_fwd(qd, kd, vd, r, name=f"{tag}_dil{r}")
        nums.append(nu.reshape(s, DIL_W))
        dens.append(de.reshape(s, DIL_W))
        mxs.append(mx.reshape(s, DIL_W))
    oh, st = _hgrn_fwd(proj, par["la"], par["lc"], name=f"{tag}_hgrn")
    ymix, od, mall, zall = _mix_out(o_a, nums, dens, mxs, oh, proj, par["hg"], name=f"{tag}_mix_out")
    out, x2 = _mm(ymix, w["out"], name=f"{tag}_out", tm=512, tn=1024, tk=1024, res=x1, scale=g2)
    x3, f2 = _ffn_fwd(x2, sh3, sc3, g3, w["gu2"], w["d2"], f"{tag}_ffn2")
    return x3, dict(f1=f1, f2=f2, x1=x1, h2=h2, proj=proj, qkv=qkv, vd=vd, o_a=o_a, qd=qd, kd=kd, oh=oh, st=st,
                    ymix=ymix, od=od, mall=mall, zall=zall, out=out)


def _layer_bwd(dx3, sv, mod, w, par, tag):
    s = dx3.shape[0]
    sh1, sc1, g1, sh2, sc2, g2, sh3, sc3, g3 = (mod[i:i + 1] for i in range(N_MOD))
    dx2, dwgu2, dwd2, dsh3, dsc3, dg3 = _ffn_bwd(dx3, sv["f2"], sc3, g3, w["gu2"], w["d2"], f"{tag}_ffn2")
    doutb, dg2 = _gate_bwd(dx2, sv["out"], g2, name=f"{tag}_dgate2")
    dmix = _mm(doutb, w["out"], name=f"{tag}_dmix", tb=True, tm=512, tn=1024, tk=1024)
    dwout = _mm_tn(sv["ymix"], doutb, name=f"{tag}_dwout", tm=1024, tn=1024, tk=512)
    proj = sv["proj"]
    doh, dgh, delta, dhg = _mix_out_bwd(dmix, sv["oh"], proj, par["hg"], sv["od"], name=f"{tag}_dmix_out")
    dqa, dka, dva = _sb_bwd(sv["qkv"], sv["o_a"], dmix, name=f"{tag}_dsb")
    do_d = dmix[:, SB_W:SB_W + DIL_W]
    dqs, dks, dvs = [], [], []
    for _, r in DIL_PATTERNS:
        a, b, c = _dil_bwd(sv["qd"], sv["kd"], sv["vd"], do_d, sv["mall"], sv["zall"], delta, r, name=f"{tag}_ddil{r}")
        dqs.append(a.reshape(s, DIL_W))
        dks.append(b.reshape(s, DIL_W))
        dvs.append(c.reshape(s, DIL_W))
    dqd, dkd, dvd, dgq, dgk = _dil_prep_bwd(proj, par["gq"], par["gk"], par["cos"], par["sin"], dqs, dks, dvs,
                                             name=f"{tag}_ddil_prep")
    dqh, dfh, dih, dla, dlc = _hgrn_bwd(proj, par["la"], par["lc"], sv["st"], doh, name=f"{tag}_dhgrn")
    dproj = jnp.concatenate([dqa, dka, dva, dqd, dkd, dvd, dqh, dfh, dih, dgh], axis=1).astype(BF16)
    dh2 = _mm(dproj, w["in"], name=f"{tag}_dh2", tb=True, tm=512, tn=1024, tk=512)
    dwin = _mm_tn(sv["h2"], dproj, name=f"{tag}_dwin", tm=1024, tn=1792, tk=512)
    dx1, dsc2, dsh2 = _norm_mod_bwd(dh2, sv["x1"], sc2, dx2, name=f"{tag}_dmix_norm")
    dx0, dwgu1, dwd1, dsh1, dsc1, dg1 = _ffn_bwd(dx1, sv["f1"], sc1, g1, w["gu1"], w["d1"], f"{tag}_ffn1")
    dmod = jnp.concatenate([dsh1, dsc1, dg1, dsh2, dsc2, dg2, dsh3, dsc3, dg3], axis=0)
    fold = lambda v: v.reshape(2, HEAD_DIM).sum(axis=0)
    grads = dict(gu1=dwgu1, d1=dwd1, gu2=dwgu2, d2=dwd2, win=dwin, wout=dwout, dmod=dmod, gq=fold(dgq), gk=fold(dgk),
                 hg=dhg[0], la=dla[0], lc=dlc[0])
    return dx0, grads


def _pack(pieces):
    flat = jnp.concatenate([p.reshape(-1) for p in pieces])
    pad = (-flat.shape[0]) % (8 * LANES)
    return jnp.pad(flat, (0, pad)).reshape(-1, LANES)


def _unpack(flat, like):
    out, off = [], 0
    for p in like:
        out.append(flat[off:off + p.size].reshape(p.shape))
        off += p.size
    return out


def kernel(x, c, w_mod, b_mod, ffn1_w_gate, ffn1_w_up, ffn1_w_down, w_in, w_out, q_norm_g, k_norm_g, hgrn_norm_g, hgrn_lb_logits, ffn2_w_gate, ffn2_w_up, ffn2_w_down, loss_target, m_w_mod, m_b_mod, m_ffn1_w_gate, m_ffn1_w_up, m_ffn1_w_down, m_w_in, m_w_out, m_q_norm_g, m_k_norm_g, m_hgrn_norm_g, m_hgrn_lb_logits, m_ffn2_w_gate, m_ffn2_w_up, m_ffn2_w_down, v_w_mod, v_b_mod, v_ffn1_w_gate, v_ffn1_w_up, v_ffn1_w_down, v_w_in, v_w_out, v_q_norm_g, v_k_norm_g, v_hgrn_norm_g, v_hgrn_lb_logits, v_ffn2_w_gate, v_ffn2_w_up, v_ffn2_w_down):
    names = ["w_mod", "b_mod", "ffn1_w_gate", "ffn1_w_up", "ffn1_w_down", "w_in", "w_out", "q_norm_g", "k_norm_g",
             "hgrn_norm_g", "hgrn_lb_logits", "ffn2_w_gate", "ffn2_w_up", "ffn2_w_down"]
    wts = dict(zip(names, (w_mod, b_mod, ffn1_w_gate, ffn1_w_up, ffn1_w_down, w_in, w_out, q_norm_g, k_norm_g, hgrn_norm_g,
                           hgrn_lb_logits, ffn2_w_gate, ffn2_w_up, ffn2_w_down)))
    mom = dict(zip(names, (m_w_mod, m_b_mod, m_ffn1_w_gate, m_ffn1_w_up, m_ffn1_w_down, m_w_in, m_w_out, m_q_norm_g, m_k_norm_g,
                           m_hgrn_norm_g, m_hgrn_lb_logits, m_ffn2_w_gate, m_ffn2_w_up, m_ffn2_w_down)))
    var = dict(zip(names, (v_w_mod, v_b_mod, v_ffn1_w_gate, v_ffn1_w_up, v_ffn1_w_down, v_w_in, v_w_out, v_q_norm_g, v_k_norm_g,
                           v_hgrn_norm_g, v_hgrn_lb_logits, v_ffn2_w_gate, v_ffn2_w_up, v_ffn2_w_down)))
    depth = w_mod.shape[0]
    assert depth == 2 and x.shape[0] == 1
    s, d = x.shape[1:]
    assert s % (DIL_PATTERNS[-1][1] * QBLK) == 0 and d % LANES == 0
    xi, yi, ci = lax.axis_index("x"), lax.axis_index("y"), lax.axis_index("c")
    chip = 2 * xi + yi
    dev = 2 * chip + ci
    x0, tgt = x[0], loss_target[0]

    c8 = _exchange([c.reshape(d // LANES, LANES)], "all", False, name="gather_c")[0].reshape(8, d)
    ncol = w_mod.shape[2]
    b_loc = lax.dynamic_slice_in_dim(b_mod, chip * ncol, ncol, axis=1)
    m_loc = _mod_fwd(c8, w_mod, b_loc.reshape(depth, 1, ncol), name="mod_fwd")
    m_all = _exchange([m_loc], "chips", False, name="gather_mod")[0]
    mod = jnp.transpose(lax.dynamic_index_in_dim(m_all, dev, axis=2, keepdims=False), (1, 0, 2)).reshape(depth, N_MOD, d)

    col_sharded = ["ffn1_w_gate", "ffn1_w_up", "w_in", "ffn2_w_gate", "ffn2_w_up"]
    row_sharded = ["ffn1_w_down", "w_out", "ffn2_w_down"]
    big = col_sharded + row_sharded
    gathered = dict(zip(big, _exchange([wts[n].astype(BF16) for n in big], "chips", False, name="gather_w")))
    full = {}
    for n in col_sharded:
        g = gathered[n]
        full[n] = jnp.moveaxis(g, 0, 2).reshape(depth, g.shape[2], -1)
    for n in row_sharded:
        g = gathered[n]
        full[n] = jnp.moveaxis(g, 0, 1).reshape(depth, -1, g.shape[3])
    ws = [dict(gu1=jnp.concatenate([full["ffn1_w_gate"][l], full["ffn1_w_up"][l]], axis=1), d1=full["ffn1_w_down"][l],
               gu2=jnp.concatenate([full["ffn2_w_gate"][l], full["ffn2_w_up"][l]], axis=1), d2=full["ffn2_w_down"][l],
               **{"in": full["w_in"][l], "out": full["w_out"][l]}) for l in range(depth)]

    _, la, lc = _lb_prep(hgrn_lb_logits, name="lb_prep")
    cos, sin = _rope_tables(s)
    pars = [dict(gq=jnp.tile(q_norm_g[l], 2)[None], gk=jnp.tile(k_norm_g[l], 2)[None], hg=hgrn_norm_g[l][None],
                 la=la[l:l + 1], lc=lc[l:l + 1], cos=cos, sin=sin) for l in range(depth)]

    xs, saved = x0, []
    for l in range(depth):
        xs, sv = _layer_fwd(xs, mod[l], ws[l], pars[l], f"l{l}")
        saved.append(sv)
    dx, lpart = _loss_grad(xs, tgt, name="loss")
    grads = [None] * depth
    for l in reversed(range(depth)):
        dx, grads[l] = _layer_bwd(dx, saved[l], mod[l], ws[l], pars[l], f"l{l}")

    stack = lambda k: jnp.stack([grads[l][k] for l in range(depth)])
    small = [stack("dmod"), stack("gq"), stack("gk"), stack("hg"), stack("la"), stack("lc"), lpart[0, :1]]
    packed = _pack(small)
    allp = _exchange([packed], "all", False, name="gather_small")[0]
    tot = _unpack(_sum_slots(allp, name="sum_small").reshape(-1), small)
    g_b_mod = tot[0].reshape(depth, N_MOD * d)
    loss = tot[6][0]
    g_small = {"b_mod": g_b_mod, "q_norm_g": tot[1], "k_norm_g": tot[2], "hgrn_norm_g": tot[3],
               "hgrn_lb_logits": _lb_bwd(hgrn_lb_logits, tot[4], tot[5], name="lb_bwd")}

    dm_all = allp.reshape(8, -1)[:, :depth * N_MOD * d].reshape(8, depth, N_MOD * d)
    dm_loc = jnp.transpose(lax.dynamic_slice_in_dim(dm_all, chip * ncol, ncol, axis=2), (1, 0, 2))
    g_w_mod = _mod_bwd(c8.T, dm_loc, name="mod_bwd")

    fgrad = {
        "ffn1_w_gate": jnp.stack([grads[l]["gu1"][:, :grads[l]["gu1"].shape[1] // 2] for l in range(depth)]),
        "ffn1_w_up": jnp.stack([grads[l]["gu1"][:, grads[l]["gu1"].shape[1] // 2:] for l in range(depth)]),
        "ffn2_w_gate": jnp.stack([grads[l]["gu2"][:, :grads[l]["gu2"].shape[1] // 2] for l in range(depth)]),
        "ffn2_w_up": jnp.stack([grads[l]["gu2"][:, grads[l]["gu2"].shape[1] // 2:] for l in range(depth)]),
        "w_in": stack("win"), "ffn1_w_down": stack("d1"), "ffn2_w_down": stack("d2"), "w_out": stack("wout"),
    }
    by_chip = []
    for n in big:
        g = fgrad[n]
        if n in col_sharded:
            g = jnp.moveaxis(g.reshape(depth, g.shape[1], 4, -1), 2, 0)
        else:
            g = jnp.moveaxis(g.reshape(depth, 4, -1, g.shape[2]), 1, 0)
        by_chip.append(g.reshape(4, -1, g.shape[-1]))
    got = _exchange(by_chip, "chips", True, name="scatter_grads")
    parts = [_sum_slots(g, name=f"sum_{n}") for n, g in zip(big, got)]
    both = dict(zip(big, _exchange(parts, "sib", False, name="swap_grads")))

    outs = {}
    for n in names:
        w2 = wts[n].reshape(-1, wts[n].shape[-1])
        if n in both:
            gp = [both[n][0], both[n][1]]
        elif n == "w_mod":
            gp = [g_w_mod.reshape(w2.shape)]
        else:
            gp = [g_small[n].reshape(w2.shape)]
        res = _adamw(w2, gp, mom[n].reshape(w2.shape), var[n].reshape(w2.shape), name=f"adamw_{n}")
        outs[n] = [r.reshape(wts[n].shape) for r in res]
    return (loss, dx[None], *[outs[n][0] for n in names], *[outs[n][1] for n in names], *[outs[n][2] for n in names],
            *[outs[n][3] for n in names])
```

```python
import functools
import math

import jax
import jax.numpy as jnp
from jax import lax
from jax.experimental import pallas as pl
from jax.experimental.pallas import tpu as pltpu

F32 = jnp.float32
BF16 = jnp.bfloat16
MESH_ID = pl.DeviceIdType.MESH

HEAD_DIM = 64
SB_W = 256
DIL_W = 256
HG_W = 512
HG_D = 128
IN_W = 3 * SB_W + 3 * DIL_W + 4 * HG_W
MIX_W = SB_W + DIL_W + HG_W
DIL_PATTERNS = ((128, 1), (512, 4), (2048, 16))
ROPE_THETA = 10000.0
EPS = 1e-6
LB_FLOOR = 1e-30
NEG_BIG = -1e30
N_MOD = 9
ADAM_LR = 0.001
ADAM_B1 = 0.9
ADAM_B2 = 0.999
ADAM_EPS = 1e-08
ADAM_WD = 0.01
ADAM_STEP = 10

LANES = 128
QBLK = 128
DIL_TILE = 512
HG_BLK = 16
HG_TILE = 256
SB_EXIT = 104.0
VMEM_LIMIT = 48 * 1024 * 1024
XCHG_CHUNKS = 8
XCHG_MIN_BYTES = 1 << 19

NN = (((1,), (0,)), ((), ()))
NT = (((1,), (1,)), ((), ()))
TN = (((0,), (0,)), ((), ()))


def _pcall(body, **kw):
    return pl.pallas_call(body, **kw)


def _cparams(*sem):
    return pltpu.CompilerParams(dimension_semantics=sem if sem else None, vmem_limit_bytes=VMEM_LIMIT)


def _dot(a, b, dims=NN):
    return lax.dot_general(a, b, dims, preferred_element_type=F32)


def _split(x, n):
    parts = []
    r = x
    for i in range(n):
        p = r.astype(BF16)
        parts.append(p)
        if i + 1 < n:
            r = r - p.astype(F32)
    return parts


def _xdot(x, m, n=2):
    return sum(_dot(p, m) for p in _split(x, n))


def _xdot_left(m, x, n=3):
    return sum(_dot(m, p) for p in _split(x, n))


def _iota(shape, dim):
    return lax.broadcasted_iota(jnp.int32, shape, dim)


def _sigmoid(x):
    return 1.0 / (1.0 + jnp.exp(-x))


def _tile(dim, pref, mult=LANES):
    t = (min(pref, dim) // mult) * mult
    while t >= mult:
        if dim % t == 0:
            return t
        t -= mult
    return dim


def _rows(dim, pref):
    return _tile(dim, pref, 8)


def _mm(a, b, *, name, tb=False, tm=512, tn=1024, tk=1024, out_dtype=F32, res=None, scale=None):
    m, kd = a.shape
    n = b.shape[0] if tb else b.shape[1]
    tm, tn, tk = _rows(m, tm), _tile(n, tn), _tile(kd, tk)
    nk = kd // tk
    epi = res is not None

    def body(*refs):
        if epi:
            a_ref, b_ref, r_ref, s_ref, o_ref, x_ref, acc = refs
        else:
            a_ref, b_ref, o_ref, acc = refs
        k = pl.program_id(2)

        @pl.when(k == 0)
        def _():
            acc[...] = jnp.zeros_like(acc)

        acc[...] += _dot(a_ref[...], b_ref[...], NT if tb else NN)

        @pl.when(k == nk - 1)
        def _():
            o_ref[...] = acc[...].astype(o_ref.dtype)
            if epi:
                x_ref[...] = r_ref[...] + s_ref[...] * acc[...]

    in_specs = [
        pl.BlockSpec((tm, tk), lambda i, j, k: (i, k)),
        pl.BlockSpec((tn, tk), lambda i, j, k: (j, k)) if tb else pl.BlockSpec((tk, tn), lambda i, j, k: (k, j)),
    ]
    out_shape = [jax.ShapeDtypeStruct((m, n), out_dtype)]
    out_specs = [pl.BlockSpec((tm, tn), lambda i, j, k: (i, j))]
    args = [a, b]
    if epi:
        in_specs += [pl.BlockSpec((tm, tn), lambda i, j, k: (i, j)), pl.BlockSpec((1, tn), lambda i, j, k: (0, j))]
        out_shape.append(jax.ShapeDtypeStruct((m, n), F32))
        out_specs.append(pl.BlockSpec((tm, tn), lambda i, j, k: (i, j)))
        args += [res, scale]
    out = _pcall(
        body, name=name, grid=(m // tm, n // tn, nk), in_specs=in_specs, out_specs=out_specs, out_shape=out_shape,
        scratch_shapes=[pltpu.VMEM((tm, tn), F32)], compiler_params=_cparams("parallel", "parallel", "arbitrary"),
    )(*args)
    return out if epi else out[0]


def _mm_tn(a, b, *, name, tm=1024, tn=1408, tk=512, out_dtype=BF16):
    s, m = a.shape
    n = b.shape[1]
    tm, tn, tk = _tile(m, tm), _tile(n, tn), _rows(s, tk)
    nk = s // tk

    def body(a_ref, b_ref, o_ref, acc):
        k = pl.program_id(2)

        @pl.when(k == 0)
        def _():
            acc[...] = jnp.zeros_like(acc)

        acc[...] += _dot(a_ref[...], b_ref[...], TN)

        @pl.when(k == nk - 1)
        def _():
            o_ref[...] = acc[...].astype(o_ref.dtype)

    return _pcall(
        body, name=name, grid=(m // tm, n // tn, nk),
        in_specs=[pl.BlockSpec((tk, tm), lambda i, j, k: (k, i)), pl.BlockSpec((tk, tn), lambda i, j, k: (k, j))],
        out_specs=pl.BlockSpec((tm, tn), lambda i, j, k: (i, j)), out_shape=jax.ShapeDtypeStruct((m, n), out_dtype),
        scratch_shapes=[pltpu.VMEM((tm, tn), F32)], compiler_params=_cparams("parallel", "parallel", "arbitrary"),
    )(a, b)


TE = 512


def _row_spec(t, w, col=0):
    return pl.BlockSpec((t, w), lambda i, col=col: (i, col))


def _vec_spec(w, col=0):
    return pl.BlockSpec((1, w), lambda i, col=col: (0, col))


def _norm_mod(x, sc, sh, *, name):
    s, d = x.shape
    t = _rows(s, TE)

    def body(x_ref, sc_ref, sh_ref, h_ref):
        xv = x_ref[...]
        r = lax.rsqrt(jnp.mean(xv * xv, axis=1, keepdims=True) + EPS)
        h_ref[...] = ((xv * r) * (1.0 + sc_ref[...]) + sh_ref[...]).astype(BF16)

    return _pcall(body, name=name, grid=(s // t,), in_specs=[_row_spec(t, d), _vec_spec(d), _vec_spec(d)],
                  out_specs=_row_spec(t, d), out_shape=jax.ShapeDtypeStruct((s, d), BF16),
                  compiler_params=_cparams("parallel"))(x, sc, sh)


def _norm_mod_bwd(dh, x, sc, dxo, *, name):
    s, d = x.shape
    t = _rows(s, TE)

    def body(dh_ref, x_ref, sc_ref, dxo_ref, dx_ref, dsc_ref, dsh_ref):
        @pl.when(pl.program_id(0) == 0)
        def _():
            dsc_ref[...] = jnp.zeros_like(dsc_ref)
            dsh_ref[...] = jnp.zeros_like(dsh_ref)

        xv = x_ref[...]
        dhv = dh_ref[...]
        r = lax.rsqrt(jnp.mean(xv * xv, axis=1, keepdims=True) + EPS)
        xn = xv * r
        dxn = dhv * (1.0 + sc_ref[...])
        dx_ref[...] = dxo_ref[...] + r * (dxn - xn * jnp.mean(dxn * xn, axis=1, keepdims=True))
        dsc_ref[...] += jnp.sum(dhv * xn, axis=0, keepdims=True)
        dsh_ref[...] += jnp.sum(dhv, axis=0, keepdims=True)

    return _pcall(
        body, name=name, grid=(s // t,),
        in_specs=[_row_spec(t, d), _row_spec(t, d), _vec_spec(d), _row_spec(t, d)],
        out_specs=[_row_spec(t, d), _vec_spec(d), _vec_spec(d)],
        out_shape=[jax.ShapeDtypeStruct((s, d), F32), jax.ShapeDtypeStruct((1, d), F32), jax.ShapeDtypeStruct((1, d), F32)],
        compiler_params=_cparams("arbitrary"))(dh, x, sc, dxo)


def _swiglu(uv, *, name):
    s, f2 = uv.shape
    f = f2 // 2
    t = _rows(s, 256)

    def body(uv_ref, a_ref):
        u = uv_ref[:, :f]
        v = uv_ref[:, f:]
        a_ref[...] = (u * _sigmoid(u) * v).astype(BF16)

    return _pcall(body, name=name, grid=(s // t,), in_specs=[_row_spec(t, f2)], out_specs=_row_spec(t, f),
                  out_shape=jax.ShapeDtypeStruct((s, f), BF16), compiler_params=_cparams("parallel"))(uv)


def _swiglu_bwd(da, uv, *, name):
    s, f2 = uv.shape
    f = f2 // 2
    t = _rows(s, 256)

    def body(da_ref, uv_ref, d_ref):
        u = uv_ref[:, :f]
        v = uv_ref[:, f:]
        dav = da_ref[...]
        sg = _sigmoid(u)
        d_ref[:, :f] = (dav * v * (sg * (1.0 + u * (1.0 - sg)))).astype(BF16)
        d_ref[:, f:] = (dav * (u * sg)).astype(BF16)

    return _pcall(body, name=name, grid=(s // t,), in_specs=[_row_spec(t, f), _row_spec(t, f2)],
                  out_specs=_row_spec(t, f2), out_shape=jax.ShapeDtypeStruct((s, f2), BF16),
                  compiler_params=_cparams("parallel"))(da, uv)


def _gate_bwd(dxo, y, sg, *, name):
    s, d = y.shape
    t = _rows(s, TE)

    def body(dxo_ref, y_ref, sg_ref, dy_ref, ds_ref):
        @pl.when(pl.program_id(0) == 0)
        def _():
            ds_ref[...] = jnp.zeros_like(ds_ref)

        dv = dxo_ref[...]
        dy_ref[...] = (sg_ref[...] * dv).astype(BF16)
        ds_ref[...] += jnp.sum(dv * y_ref[...], axis=0, keepdims=True)

    return _pcall(
        body, name=name, grid=(s // t,), in_specs=[_row_spec(t, d), _row_spec(t, d), _vec_spec(d)],
        out_specs=[_row_spec(t, d), _vec_spec(d)],
        out_shape=[jax.ShapeDtypeStruct((s, d), BF16), jax.ShapeDtypeStruct((1, d), F32)],
        compiler_params=_cparams("arbitrary"))(dxo, y, sg)


def _loss_grad(y, tgt, *, name):
    s, d = y.shape
    t = _rows(s, TE)
    nt = s // t

    def body(y_ref, t_ref, dy_ref, l_ref, acc):
        i = pl.program_id(0)

        @pl.when(i == 0)
        def _():
            acc[...] = jnp.zeros_like(acc)

        e = y_ref[...] - t_ref[...]
        dy_ref[...] = e * (1.0 / d)
        acc[...] += jnp.sum(e * e, axis=0, keepdims=True)

        @pl.when(i == nt - 1)
        def _():
            l_ref[...] = jnp.broadcast_to(jnp.sum(acc[...], axis=1, keepdims=True) * (0.5 / d), l_ref.shape)

    return _pcall(
        body, name=name, grid=(nt,), in_specs=[_row_spec(t, d), _row_spec(t, d)],
        out_specs=[_row_spec(t, d), pl.BlockSpec((1, LANES), lambda i: (0, 0))],
        out_shape=[jax.ShapeDtypeStruct((s, d), F32), jax.ShapeDtypeStruct((1, LANES), F32)],
        scratch_shapes=[pltpu.VMEM((1, d), F32)], compiler_params=_cparams("arbitrary"))(y, tgt)


def _sb_consts():
    row = _iota((QBLK, LANES), 0)
    lane = _iota((QBLK, LANES), 1)
    ones = jnp.ones((QBLK, LANES), BF16)
    after = jnp.concatenate([jnp.where(row > lane, 1.0, 0.0).astype(BF16), ones], axis=1)
    from_ = jnp.concatenate([jnp.where(row >= lane, 1.0, 0.0).astype(BF16), ones], axis=1)
    return row, lane, after, from_


def _sb_scores(qm, kb, strict):
    z = _dot(qm, kb, NT) * (HEAD_DIM ** -0.5)
    sp = jnp.log(1.0 + jnp.exp(-jnp.abs(z)))
    lnb = -(jnp.maximum(z, 0.0) + sp)
    lb = jnp.minimum(z, 0.0) - sp
    if strict is not None:
        lnb = jnp.where(strict, lnb, 0.0)
    return lnb, lb


def _sb_fwd(qkv, *, name):
    s = qkv.shape[0]
    nq = s // QBLK

    def body(q_ref, k_ref, v_ref, o_ref, acc, osc):
        qi = pl.program_id(0)
        row, lane, after, _ = _sb_consts()
        strict = lane < row
        h0 = lane < HEAD_DIM
        q = q_ref[...]
        qms = []
        for p in range(2):
            qp = q[:, p * LANES:(p + 1) * LANES]
            qms += [jnp.where(h0, qp, jnp.zeros_like(qp)), jnp.where(h0, jnp.zeros_like(qp), qp)]

        def block(kj, mask):
            off = pl.multiple_of(kj * QBLK, QBLK)
            top = None
            for c in range(4):
                p = c // 2
                kb = k_ref[pl.ds(off, QBLK), p * LANES:(p + 1) * LANES]
                vb = v_ref[pl.ds(off, QBLK), p * LANES:(p + 1) * LANES]
                lnb, lb = _sb_scores(qms[c], kb, mask)
                tr = _xdot(lnb, after)
                w = jnp.exp(lb + tr[:, :QBLK] + acc[c])
                if mask is not None:
                    w = jnp.where(mask, w, 0.0)
                osc[c] += _xdot(w, vb)
                new = acc[c] + tr[:, QBLK:]
                acc[c] = new
                top = new if top is None else jnp.maximum(top, new)
            return jnp.max(top)

        acc[...] = jnp.zeros_like(acc)
        osc[...] = jnp.zeros_like(osc)
        top = block(qi, strict)
        lax.while_loop(lambda c: (c[0] >= 0) & (c[1] > -SB_EXIT), lambda c: (c[0] - 1, block(c[0], None)), (qi - 1, top))
        for p in range(2):
            o_ref[:, p * LANES:(p + 1) * LANES] = jnp.where(h0, osc[2 * p], osc[2 * p + 1])

    return _pcall(
        body, name=name, grid=(nq,),
        in_specs=[pl.BlockSpec((QBLK, SB_W), lambda i: (i, 0)),
                  pl.BlockSpec((s, SB_W), lambda i: (0, 1)),
                  pl.BlockSpec((s, SB_W), lambda i: (0, 2))],
        out_specs=pl.BlockSpec((QBLK, SB_W), lambda i: (i, 0)),
        out_shape=jax.ShapeDtypeStruct((s, SB_W), F32),
        scratch_shapes=[pltpu.VMEM((4, QBLK, LANES), F32), pltpu.VMEM((4, QBLK, LANES), F32)],
        compiler_params=_cparams("arbitrary"))(qkv, qkv, qkv)


def _sb_bwd(qkv, o, dmix, *, name):
    s = qkv.shape[0]
    nq = s // QBLK
    scale = HEAD_DIM ** -0.5

    def body(q_ref, k_ref, v_ref, o_ref, do_ref, dq_ref, dk_ref, dv_ref, acc, racc, dqs, dks, dvs):
        i = pl.program_id(1)
        qi = nq - 1 - i
        row, lane, after, from_ = _sb_consts()
        strict = lane < row

        @pl.when(i == 0)
        def _():
            dks[...] = jnp.zeros_like(dks)
            dvs[...] = jnp.zeros_like(dvs)

        q = q_ref[...]
        do = do_ref[...]
        dob = do.astype(BF16)
        dd = do * o_ref[...]
        dol = (do - dob.astype(F32)).astype(BF16)
        zero = jnp.zeros_like(q)
        hms = (lane < HEAD_DIM, lane >= HEAD_DIM)
        qms = [jnp.where(hm, q, zero) for hm in hms]
        doms = [jnp.where(hm, dob, zero) for hm in hms]
        dols = [jnp.where(hm, dol, zero) for hm in hms]
        dsums = [jnp.sum(jnp.where(hm, dd, 0.0), axis=1, keepdims=True) for hm in hms]

        def block(kj, mask):
            off = pl.multiple_of(kj * QBLK, QBLK)
            kb = k_ref[pl.ds(off, QBLK), :]
            vb = v_ref[pl.ds(off, QBLK), :]
            top, dq, dk, dv = None, None, None, None
            for h in range(2):
                lnb, lb = _sb_scores(qms[h], kb, mask)
                tr = _xdot(lnb, after)
                w = jnp.exp(lb + tr[:, :QBLK] + acc[h])
                if mask is not None:
                    w = jnp.where(mask, w, 0.0)
                g = w * (_dot(doms[h], vb, NT) + _dot(dols[h], vb, NT))
                tg = _xdot(g, from_)
                before = dsums[h] - (tg[:, :QBLK] + racc[h])
                dz = g - jnp.exp(lb) * (g + before)
                if mask is not None:
                    dz = jnp.where(mask, dz, 0.0)
                dzb = (dz * scale).astype(BF16)
                dqh = _dot(dzb, jnp.where(hms[h], kb, zero))
                dkh = _dot(dzb, qms[h], TN)
                dvh = _dot(w.astype(BF16), doms[h], TN)
                dq, dk, dv = (dqh, dkh, dvh) if h == 0 else (dq + dqh, dk + dkh, dv + dvh)
                new = acc[h] + tr[:, QBLK:]
                acc[h] = new
                racc[h] += tg[:, QBLK:]
                top = new if top is None else jnp.maximum(top, new)
            dqs[...] += dq
            dks[pl.ds(off, QBLK), :] += dk
            dvs[pl.ds(off, QBLK), :] += dv
            return jnp.max(top)

        dqs[...] = jnp.zeros_like(dqs)
        acc[...] = jnp.zeros_like(acc)
        racc[...] = jnp.zeros_like(racc)
        top = block(qi, strict)
        lax.while_loop(lambda c: (c[0] >= 0) & (c[1] > -SB_EXIT), lambda c: (c[0] - 1, block(c[0], None)), (qi - 1, top))
        dq_ref[...] = dqs[...]
        fin = pl.multiple_of(qi * QBLK, QBLK)
        dk_ref[...] = dks[pl.ds(fin, QBLK), :]
        dv_ref[...] = dvs[pl.ds(fin, QBLK), :]

    blk = lambda c0: pl.BlockSpec((QBLK, LANES), lambda p, i, c0=c0: (nq - 1 - i, c0 + p))
    return _pcall(
        body, name=name, grid=(2, nq),
        in_specs=[blk(0), pl.BlockSpec((s, LANES), lambda p, i: (0, 2 + p)), pl.BlockSpec((s, LANES), lambda p, i: (0, 4 + p)),
                  blk(0), blk(0)],
        out_specs=[blk(0), blk(0), blk(0)],
        out_shape=[jax.ShapeDtypeStruct((s, SB_W), F32)] * 3,
        scratch_shapes=[pltpu.VMEM((2, QBLK, LANES), F32), pltpu.VMEM((2, QBLK, LANES), F32), pltpu.VMEM((QBLK, LANES), F32),
                        pltpu.VMEM((s, LANES), F32), pltpu.VMEM((s, LANES), F32)],
        compiler_params=_cparams("arbitrary", "arbitrary"))(qkv, qkv, qkv, o, dmix)


def _seg_consts():
    r = _iota((LANES, LANES), 0)
    c = _iota((LANES, LANES), 1)
    return jnp.where((r >> 6) == (c >> 6), 1.0, 0.0).astype(BF16)


def _rot_half(x, lane):
    half = HEAD_DIM // 2
    return jnp.where((lane & (HEAD_DIM - 1)) < half, pltpu.roll(x, LANES - half, 1), pltpu.roll(x, half, 1))


def _rope_tables(s):
    half = HEAD_DIM // 2
    inv_freq = ROPE_THETA ** (-jnp.arange(half, dtype=F32) * 2.0 / HEAD_DIM)
    ang = jnp.arange(s, dtype=F32)[:, None] * inv_freq[None, :]
    cos, sin = jnp.cos(ang), jnp.sin(ang)
    return jnp.tile(jnp.concatenate([cos, cos], axis=1), (1, 2)), jnp.tile(jnp.concatenate([-sin, sin], axis=1), (1, 2))


def _dil_prep(proj, gq, gk, cos, sin, *, name):
    s = proj.shape[0]
    t = _rows(s, TE)
    c0 = 3 * SB_W // LANES

    def body(q_ref, k_ref, gq_ref, gk_ref, cos_ref, sin_ref, qo_ref, ko_ref):
        seg = _seg_consts()
        lane = _iota((t, LANES), 1)
        cs, sn = cos_ref[...], sin_ref[...]
        for x_ref, g_ref, o_ref, mul in ((q_ref, gq_ref, qo_ref, HEAD_DIM ** -0.5), (k_ref, gk_ref, ko_ref, 1.0)):
            for j in range(2):
                xv = x_ref[:, j * LANES:(j + 1) * LANES]
                ms = _xdot(xv * xv, seg, 3) * (1.0 / HEAD_DIM)
                xn = xv * lax.rsqrt(ms + EPS) * g_ref[...]
                o_ref[:, j * LANES:(j + 1) * LANES] = ((xn * cs + _rot_half(xn, lane) * sn) * mul).astype(BF16)

    return _pcall(
        body, name=name, grid=(s // t,),
        in_specs=[pl.BlockSpec((t, DIL_W), lambda i: (i, c0 // 2)), pl.BlockSpec((t, DIL_W), lambda i: (i, c0 // 2 + 1)),
                  _vec_spec(LANES), _vec_spec(LANES), _row_spec(t, LANES), _row_spec(t, LANES)],
        out_specs=[_row_spec(t, DIL_W), _row_spec(t, DIL_W)],
        out_shape=[jax.ShapeDtypeStruct((s, DIL_W), BF16)] * 2, compiler_params=_cparams("parallel"))(proj, proj, gq, gk, cos, sin)


def _dil_prep_bwd(proj, gq, gk, cos, sin, dqs, dks, dvs, *, name):
    s = proj.shape[0]
    t = _rows(s, TE)
    c0 = 3 * SB_W // LANES

    def body(q_ref, k_ref, gq_ref, gk_ref, cos_ref, sin_ref, a0, a1, a2, b0, b1, b2, c0_ref, c1_ref, c2_ref,
             dq_ref, dk_ref, dv_ref, dgq_ref, dgk_ref):
        @pl.when(pl.program_id(0) == 0)
        def _():
            dgq_ref[...] = jnp.zeros_like(dgq_ref)
            dgk_ref[...] = jnp.zeros_like(dgk_ref)

        dv_ref[...] = c0_ref[...] + c1_ref[...] + c2_ref[...]
        seg = _seg_consts()
        lane = _iota((t, LANES), 1)
        cs, sn = cos_ref[...], sin_ref[...]
        for x_ref, g_ref, parts, o_ref, dg_ref, mul in ((q_ref, gq_ref, (a0, a1, a2), dq_ref, dgq_ref, HEAD_DIM ** -0.5),
                                                          (k_ref, gk_ref, (b0, b1, b2), dk_ref, dgk_ref, 1.0)):
            for j in range(2):
                sl = slice(j * LANES, (j + 1) * LANES)
                dout = (parts[0][:, sl] + parts[1][:, sl] + parts[2][:, sl]) * mul
                dxn = dout * cs + _rot_half(dout * sn, lane)
                xv = x_ref[:, sl]
                r = lax.rsqrt(_xdot(xv * xv, seg, 3) * (1.0 / HEAD_DIM) + EPS)
                xh = xv * r
                dg_ref[...] += jnp.sum(dxn * xh, axis=0, keepdims=True)
                dxh = dxn * g_ref[...]
                o_ref[:, sl] = r * (dxh - xh * (_xdot(dxh * xh, seg, 3) * (1.0 / HEAD_DIM)))

    rs = _row_spec(t, DIL_W)
    return _pcall(
        body, name=name, grid=(s // t,),
        in_specs=[pl.BlockSpec((t, DIL_W), lambda i: (i, c0 // 2)), pl.BlockSpec((t, DIL_W), lambda i: (i, c0 // 2 + 1)),
                  _vec_spec(LANES), _vec_spec(LANES), _row_spec(t, LANES), _row_spec(t, LANES)] + [rs] * 9,
        out_specs=[rs, rs, rs, _vec_spec(LANES), _vec_spec(LANES)],
        out_shape=[jax.ShapeDtypeStruct((s, DIL_W), F32)] * 3 + [jax.ShapeDtypeStruct((1, LANES), F32)] * 2,
        compiler_params=_cparams("arbitrary"))(proj, proj, gq, gk, cos, sin, *dqs, *dks, *dvs)


def _dil_masks(n):
    row = _iota((QBLK, LANES), 0)
    col = _iota((QBLK, LANES), 1)
    return col <= row, (col >= row) & (n > 0)


def _dil_fwd(q, k, v, r, *, name):
    s = q.shape[0]
    rows = s // r
    tb = _tile(rows, DIL_TILE, QBLK)
    nsub = tb // QBLK
    shape = (rows, r * DIL_W)
    q, k, v = (a.reshape(shape) for a in (q, k, v))

    def body(q_ref, kc_ref, kp_ref, vc_ref, vp_ref, num_ref, den_ref, mx_ref):
        n = pl.program_id(1)
        lane = _iota((QBLK, LANES), 1)
        h0 = lane < HEAD_DIM
        for j in range(nsub):
            rs = slice(j * QBLK, (j + 1) * QBLK)
            vc_m, vp_m = _dil_masks(n if j == 0 else 1)
            for p in range(2):
                ls = slice(p * LANES, (p + 1) * LANES)
                qv = q_ref[rs, ls]
                kc, vc = kc_ref[rs, ls], vc_ref[rs, ls]
                if j == 0:
                    kp, vp = kp_ref[:, ls], vp_ref[:, ls]
                else:
                    kp, vp = kc_ref[(j - 1) * QBLK:j * QBLK, ls], vc_ref[(j - 1) * QBLK:j * QBLK, ls]
                res = []
                for h in range(2):
                    qm = jnp.where(h0 if h == 0 else ~h0, qv, jnp.zeros_like(qv))
                    sc = jnp.where(vc_m, _dot(qm, kc, NT), NEG_BIG)
                    sp = jnp.where(vp_m, _dot(qm, kp, NT), NEG_BIG)
                    mx = jnp.maximum(jnp.max(sc, axis=1, keepdims=True), jnp.max(sp, axis=1, keepdims=True))
                    pc = jnp.exp(sc - mx)
                    pp = jnp.exp(sp - mx)
                    den = jnp.sum(pc, axis=1, keepdims=True) + jnp.sum(pp, axis=1, keepdims=True)
                    res.append((_dot(pc.astype(BF16), vc) + _dot(pp.astype(BF16), vp), den, mx))
                num_ref[rs, ls] = jnp.where(h0, res[0][0], res[1][0])
                den_ref[rs, ls] = jnp.where(h0, res[0][1], res[1][1])
                mx_ref[rs, ls] = jnp.where(h0, res[0][2], res[1][2])

    cur = pl.BlockSpec((tb, DIL_W), lambda rho, n: (n, rho))
    prev = pl.BlockSpec((QBLK, DIL_W), lambda rho, n: (jnp.maximum(n * nsub - 1, 0), rho))
    return _pcall(
        body, name=name, grid=(r, rows // tb), in_specs=[cur, cur, prev, cur, prev], out_specs=[cur, cur, cur],
        out_shape=[jax.ShapeDtypeStruct(shape, F32)] * 3,
        compiler_params=_cparams("parallel", "arbitrary"))(q, k, k, v, v)


def _dil_bwd(q, k, v, do, mall, zall, delta, r, *, name):
    s = q.shape[0]
    rows = s // r
    tb = _tile(rows, DIL_TILE, QBLK)
    nsub, nbig = tb // QBLK, rows // tb
    shape = (rows, r * DIL_W)
    q, k, v, do, mall, zall, delta = (a.reshape(shape) for a in (q, k, v, do, mall, zall, delta))

    def body(q_ref, kc_ref, kp_ref, vc_ref, vp_ref, do_ref, m_ref, z_ref, dl_ref, dq_ref, dk_ref, dv_ref, pk, pv):
        n = pl.program_id(1)
        lane = _iota((QBLK, LANES), 1)
        h0 = lane < HEAD_DIM

        @pl.when(n == 0)
        def _():
            pk[...] = jnp.zeros_like(pk)
            pv[...] = jnp.zeros_like(pv)

        @pl.when(n < nbig)
        def _():
            dk_ref[...] = pk[...]
            dv_ref[...] = pv[...]
            for j in range(nsub):
                rs = slice(j * QBLK, (j + 1) * QBLK)
                ps = slice((j - 1) * QBLK, j * QBLK)
                vc_m, vp_m = _dil_masks(n if j == 0 else 1)
                for p in range(2):
                    ls = slice(p * LANES, (p + 1) * LANES)
                    qv = q_ref[rs, ls]
                    dob = do_ref[rs, ls].astype(BF16)
                    zero = jnp.zeros_like(qv)
                    kc, vc = kc_ref[rs, ls], vc_ref[rs, ls]
                    kp, vp = (kp_ref[:, ls], vp_ref[:, ls]) if j == 0 else (kc_ref[ps, ls], vc_ref[ps, ls])
                    dq = None
                    acc = [None] * 4
                    for h in range(2):
                        hm = h0 if h == 0 else ~h0
                        qm = jnp.where(hm, qv, zero)
                        dom = jnp.where(hm, dob, zero)
                        c = p * LANES + h * HEAD_DIM
                        mrow, dlrow = m_ref[rs, c:c + 1], dl_ref[rs, c:c + 1]
                        rz = 1.0 / z_ref[rs, c:c + 1]
                        for kb, vb, valid, o in ((kc, vc, vc_m, 0), (kp, vp, vp_m, 2)):
                            sc = jnp.where(valid, _dot(qm, kb, NT), NEG_BIG)
                            pr = jnp.exp(sc - mrow) * rz
                            ds = (pr * (_dot(dom, vb, NT) - dlrow)).astype(BF16)
                            dqx = _dot(ds, jnp.where(hm, kb, zero))
                            dq = dqx if dq is None else dq + dqx
                            for i, x in ((o, _dot(ds, qm, TN)), (o + 1, _dot(pr.astype(BF16), dom, TN))):
                                acc[i] = x if acc[i] is None else acc[i] + x
                    dq_ref[rs, ls] = dq
                    if j == 0:
                        pk[rs, ls] = acc[0]
                        pv[rs, ls] = acc[1]
                        dk_ref[tb - QBLK:, ls] += acc[2]
                        dv_ref[tb - QBLK:, ls] += acc[3]
                    else:
                        pk[rs, ls] = acc[0]
                        pv[rs, ls] = acc[1]
                        pk[ps, ls] += acc[2]
                        pv[ps, ls] += acc[3]

        @pl.when(n == nbig)
        def _():
            dk_ref[...] = pk[...]
            dv_ref[...] = pv[...]

    last = nbig - 1
    cur = pl.BlockSpec((tb, DIL_W), lambda rho, n: (jnp.minimum(n, last), rho))
    prev = pl.BlockSpec((QBLK, DIL_W), lambda rho, n: (jnp.maximum(jnp.minimum(n, last) * nsub - 1, 0), rho))
    late = pl.BlockSpec((tb, DIL_W), lambda rho, n: (jnp.maximum(n - 1, 0), rho))
    return _pcall(
        body, name=name, grid=(r, nbig + 1), in_specs=[cur, cur, prev, cur, prev, cur, cur, cur, cur],
        out_specs=[cur, late, late], out_shape=[jax.ShapeDtypeStruct(shape, F32)] * 3,
        scratch_shapes=[pltpu.VMEM((tb, DIL_W), F32)] * 2,
        compiler_params=_cparams("parallel", "arbitrary"))(q, k, k, v, v, do, mall, zall, delta)


HG_SHIFT = HG_BLK.bit_length() - 1
HG_Q0, HG_F0, HG_I0 = (3 * SB_W + 3 * DIL_W) // HG_D, (3 * SB_W + 3 * DIL_W + HG_W) // HG_D, (3 * SB_W + 3 * DIL_W + 2 * HG_W) // HG_D


def _hg_inputs(qh, z, v, la, lc, t):
    r = _iota((t, t), 0)
    c = _iota((t, t), 1)
    same = (r >> HG_SHIFT) == (c >> HG_SHIFT)
    tri = jnp.where(same & (c <= r), 1.0, 0.0).astype(BF16)
    blk = jnp.where(same, 1.0, 0.0).astype(BF16)
    lsg = jnp.minimum(z, 0.0) - jnp.log(1.0 + jnp.exp(-jnp.abs(z)))
    b = lc + lsg
    lf = jnp.maximum(la, b) + jnp.log(1.0 + jnp.exp(-jnp.abs(la - b)))
    f = jnp.exp(lf)
    sq = _sigmoid(qh)
    g = _xdot_left(tri, lf)
    gl = _xdot_left(blk, lf)
    return dict(lf=lf, b=b, f=f, k=1.0 - f, sq=sq, qs=qh * sq, g=g, eg=jnp.exp(g), egl=jnp.exp(gl - g), dec=jnp.exp(gl))


def _hgrn_fwd(proj, la, lc, *, name):
    s = proj.shape[0]
    t = _rows(s, HG_TILE)
    nt, nb = s // t, t // HG_BLK

    def body(q_ref, f_ref, i_ref, la_ref, lc_ref, o_ref, st_ref, state, osc):
        @pl.when(pl.program_id(1) == 0)
        def _():
            state[...] = jnp.zeros_like(state)

        v = i_ref[...]
        a = _hg_inputs(q_ref[...], f_ref[...], v, la_ref[...], lc_ref[...], t)
        qs, k, g = a["qs"], a["k"], a["g"]
        rb = _iota((t, LANES), 0) & (HG_BLK - 1)
        o = jnp.sum(qs * k, axis=1, keepdims=True) * v
        for d in range(1, HG_BLK):
            m = rb >= d
            e = jnp.exp(jnp.where(m, g - pltpu.roll(g, d, 0), 0.0))
            cd = jnp.sum(qs * pltpu.roll(k, d, 0) * e, axis=1, keepdims=True)
            o = o + jnp.where(m, cd, 0.0) * pltpu.roll(v, d, 0)
        osc[...] = o
        qt = (qs * a["eg"]).astype(BF16)
        kt = (k * a["egl"]).astype(BF16)
        vb = v.astype(BF16)
        for blk in range(nb):
            sl = slice(blk * HG_BLK, (blk + 1) * HG_BLK)
            st = state[...]
            stb = st.astype(BF16)
            st_ref[blk * HG_D:(blk + 1) * HG_D, :] = stb
            osc[sl, :] += _dot(qt[sl], stb, NT)
            state[...] = a["dec"][blk * HG_BLK:blk * HG_BLK + 1] * st + _dot(vb[sl], kt[sl], TN)
        o_ref[...] = osc[...]

    col = lambda c0: pl.BlockSpec((t, HG_D), lambda hd, i, c0=c0: (i, c0 + hd))
    vec = pl.BlockSpec((1, HG_D), lambda hd, i: (0, hd))
    return _pcall(
        body, name=name, grid=(4, nt), in_specs=[col(HG_Q0), col(HG_F0), col(HG_I0), vec, vec],
        out_specs=[col(0), pl.BlockSpec((None, nb * HG_D, HG_D), lambda hd, i: (hd, i, 0))],
        out_shape=[jax.ShapeDtypeStruct((s, HG_W), F32), jax.ShapeDtypeStruct((4, s // HG_BLK * HG_D, HG_D), BF16)],
        scratch_shapes=[pltpu.VMEM((HG_D, HG_D), F32), pltpu.VMEM((t, HG_D), F32)],
        compiler_params=_cparams("arbitrary", "arbitrary"))(proj, proj, proj, la, lc)


def _hgrn_bwd(proj, la, lc, st, doh, *, name):
    s = proj.shape[0]
    t = _rows(s, HG_TILE)
    nt, nb = s // t, t // HG_BLK

    def body(q_ref, f_ref, i_ref, la_ref, lc_ref, st_ref, do_ref, dq_ref, df_ref, di_ref, dla_ref, dlc_ref, dstate):
        @pl.when(pl.program_id(1) == 0)
        def _():
            dstate[...] = jnp.zeros_like(dstate)
            dla_ref[...] = jnp.zeros_like(dla_ref)
            dlc_ref[...] = jnp.zeros_like(dlc_ref)

        qh, z, v, do = q_ref[...], f_ref[...], i_ref[...], do_ref[...]
        la = la_ref[...]
        a = _hg_inputs(qh, z, v, la, lc_ref[...], t)
        qs, k, g = a["qs"], a["k"], a["g"]
        rb = _iota((t, LANES), 0) & (HG_BLK - 1)
        dc0 = jnp.sum(do * v, axis=1, keepdims=True)
        dq = dc0 * k
        dk = dc0 * qs
        dv = jnp.sum(qs * k, axis=1, keepdims=True) * do
        for d in range(1, HG_BLK):
            m = rb >= d
            e = jnp.exp(jnp.where(m, g - pltpu.roll(g, d, 0), 0.0))
            ks = pltpu.roll(k, d, 0)
            qe = qs * e
            cd = jnp.where(m, jnp.sum(qe * ks, axis=1, keepdims=True), 0.0)
            dcd = jnp.where(m, jnp.sum(do * pltpu.roll(v, d, 0), axis=1, keepdims=True), 0.0)
            dq = dq + dcd * ks * e
            dk = dk + pltpu.roll(dcd * qe, t - d, 0)
            dv = dv + pltpu.roll(cd * do, t - d, 0)
        qt = (qs * a["eg"]).astype(BF16)
        kt = (k * a["egl"]).astype(BF16)
        vb = v.astype(BF16)
        dob = do.astype(BF16)
        dqt, dkt, dvi, dgs = [None] * nb, [None] * nb, [None] * nb, [None] * nb
        for blk in reversed(range(nb)):
            sl = slice(blk * HG_BLK, (blk + 1) * HG_BLK)
            stb = st_ref[blk * HG_D:(blk + 1) * HG_D, :]
            ds1 = dstate[...]
            ds1b = ds1.astype(BF16)
            dec = a["dec"][blk * HG_BLK:blk * HG_BLK + 1]
            dqt[blk] = _dot(dob[sl], stb)
            dkt[blk] = _dot(vb[sl], ds1b)
            dvi[blk] = _dot(kt[sl], ds1b, NT)
            dgs[blk] = jnp.broadcast_to(jnp.sum(ds1 * stb.astype(F32), axis=0, keepdims=True) * dec, (HG_BLK, HG_D))
            dstate[...] = dec * ds1 + _dot(dob[sl], qt[sl], TN)
        dki = jnp.concatenate(dkt, axis=0) * a["egl"]
        dq = dq + jnp.concatenate(dqt, axis=0) * a["eg"]
        dk = dk + dki
        dv = dv + jnp.concatenate(dvi, axis=0)
        r = _iota((t, t), 0)
        c = _iota((t, t), 1)
        same = (r >> HG_SHIFT) == (c >> HG_SHIFT)
        later = jnp.where(same & (c >= r), 1.0, 0.0).astype(BF16)
        whole = jnp.where(same, 1.0, 0.0).astype(BF16)
        dlf = (_xdot_left(later, qs * dq - k * dk) + _xdot_left(whole, k * dki) + jnp.concatenate(dgs, axis=0)
               - a["f"] * dk)
        wb = jnp.exp(a["b"] - a["lf"])
        wa = jnp.exp(la - a["lf"])
        sq = a["sq"]
        dq_ref[...] = dq * (sq * (1.0 + qh * (1.0 - sq)))
        df_ref[...] = dlf * wb * (1.0 - _sigmoid(z))
        di_ref[...] = dv
        dla_ref[...] += jnp.sum(dlf * wa, axis=0, keepdims=True)
        dlc_ref[...] += jnp.sum(dlf * wb, axis=0, keepdims=True)

    col = lambda c0: pl.BlockSpec((t, HG_D), lambda hd, i, c0=c0: (nt - 1 - i, c0 + hd))
    vec = pl.BlockSpec((1, HG_D), lambda hd, i: (0, hd))
    return _pcall(
        body, name=name, grid=(4, nt),
        in_specs=[col(HG_Q0), col(HG_F0), col(HG_I0), vec, vec,
                  pl.BlockSpec((None, nb * HG_D, HG_D), lambda hd, i: (hd, nt - 1 - i, 0)), col(0)],
        out_specs=[col(0), col(0), col(0), vec, vec],
        out_shape=[jax.ShapeDtypeStruct((s, HG_W), F32)] * 3 + [jax.ShapeDtypeStruct((1, HG_W), F32)] * 2,
        scratch_shapes=[pltpu.VMEM((HG_D, HG_D), F32)],
        compiler_params=_cparams("arbitrary", "arbitrary"))(proj, proj, proj, la, lc, st, doh)


GH0 = (IN_W - HG_W) // HG_W


def _mix_out(o_a, nums, dens, mxs, oh, proj, hg, *, name):
    s = o_a.shape[0]
    t = _rows(s, TE)

    def body(oa_ref, n0, n1, n2, d0, d1, d2, m0, m1, m2, oh_ref, gh_ref, hg_ref, y_ref, od_ref, mall_ref, z_ref):
        y_ref[:, :SB_W] = oa_ref[...].astype(BF16)
        m = jnp.maximum(jnp.maximum(m0[...], m1[...]), m2[...])
        num = jnp.zeros((t, DIL_W), F32)
        z = jnp.zeros((t, DIL_W), F32)
        for n_ref, d_ref, m_ref in ((n0, d0, m0), (n1, d1, m1), (n2, d2, m2)):
            sc = jnp.exp(m_ref[...] - m)
            num = num + n_ref[...] * sc
            z = z + d_ref[...] * sc
        od = num / z
        od_ref[...] = od
        mall_ref[...] = m
        z_ref[...] = z
        y_ref[:, SB_W:SB_W + DIL_W] = od.astype(BF16)
        for h in range(4):
            sl = slice(h * HG_D, (h + 1) * HG_D)
            ov = oh_ref[:, sl]
            g = gh_ref[:, sl]
            r = lax.rsqrt(jnp.mean(ov * ov, axis=1, keepdims=True) + EPS)
            y_ref[:, SB_W + DIL_W + h * HG_D:SB_W + DIL_W + (h + 1) * HG_D] = (ov * r * hg_ref[...] * (g * _sigmoid(g))).astype(BF16)

    rd = _row_spec(t, DIL_W)
    return _pcall(
        body, name=name, grid=(s // t,),
        in_specs=[rd] * 10 + [_row_spec(t, HG_W), _row_spec(t, HG_W, GH0), _vec_spec(HG_D)],
        out_specs=[_row_spec(t, MIX_W), rd, rd, rd],
        out_shape=[jax.ShapeDtypeStruct((s, MIX_W), BF16)] + [jax.ShapeDtypeStruct((s, DIL_W), F32)] * 3,
        compiler_params=_cparams("parallel"))(o_a, *nums, *dens, *mxs, oh, proj, hg)


def _mix_out_bwd(dmix, oh, proj, hg, od, *, name):
    s = oh.shape[0]
    t = _rows(s, TE)

    def body(dm_ref, oh_ref, gh_ref, hg_ref, od_ref, doh_ref, dgh_ref, dl_ref, dhg_ref):
        @pl.when(pl.program_id(0) == 0)
        def _():
            dhg_ref[...] = jnp.zeros_like(dhg_ref)

        seg = _seg_consts()
        for j in range(2):
            sl = slice(j * LANES, (j + 1) * LANES)
            dl_ref[:, sl] = _xdot(dm_ref[:, SB_W + j * LANES:SB_W + (j + 1) * LANES] * od_ref[:, sl], seg, 3)
        hgv = hg_ref[...]
        for h in range(4):
            sl = slice(h * HG_D, (h + 1) * HG_D)
            dy = dm_ref[:, SB_W + DIL_W + h * HG_D:SB_W + DIL_W + (h + 1) * HG_D]
            ov = oh_ref[:, sl]
            g = gh_ref[:, sl]
            sg = _sigmoid(g)
            silu = g * sg
            r = lax.rsqrt(jnp.mean(ov * ov, axis=1, keepdims=True) + EPS)
            nrm = ov * r
            dhg_ref[...] += jnp.sum(dy * nrm * silu, axis=0, keepdims=True)
            dgh_ref[:, sl] = dy * nrm * hgv * (sg * (1.0 + g * (1.0 - sg)))
            dn = dy * hgv * silu
            doh_ref[:, sl] = r * (dn - nrm * jnp.mean(dn * nrm, axis=1, keepdims=True))

    rh = _row_spec(t, HG_W)
    return _pcall(
        body, name=name, grid=(s // t,),
        in_specs=[_row_spec(t, MIX_W), rh, _row_spec(t, HG_W, GH0), _vec_spec(HG_D), _row_spec(t, DIL_W)],
        out_specs=[rh, rh, _row_spec(t, DIL_W), _vec_spec(HG_D)],
        out_shape=[jax.ShapeDtypeStruct((s, HG_W), F32)] * 2 + [jax.ShapeDtypeStruct((s, DIL_W), F32), jax.ShapeDtypeStruct((1, HG_D), F32)],
        compiler_params=_cparams("arbitrary"))(dmix, oh, proj, hg, od)


def _lb_terms(l):
    l0, l1 = l[0:1], l[1:2]
    m = jnp.maximum(l0, l1)
    e0, e1 = jnp.exp(l0 - m), jnp.exp(l1 - m)
    s0, s1 = e0 / (e0 + e1), e1 / (e0 + e1)
    args = (s0 - s0, (s0 + s1) - s0)
    lbs = tuple(jnp.minimum(jnp.maximum(a, 0.0), 1.0 - EPS) for a in args)
    return s0, s1, args, lbs


def _lb_prep(logits, *, name):
    def body(l_ref, lb_ref, la_ref, lc_ref):
        _, _, _, lbs = _lb_terms(l_ref[...])
        lb = jnp.concatenate(lbs, axis=0)
        lb_ref[...] = lb
        la_ref[...] = jnp.log(jnp.maximum(lb, LB_FLOOR))
        lc_ref[...] = jnp.log1p(-lb)

    return _pcall(body, name=name, out_shape=[jax.ShapeDtypeStruct(logits.shape, F32)] * 3)(logits)


def _lb_bwd(logits, dla, dlc, *, name):
    def half(hi, eq):
        return jnp.where(hi, 1.0, jnp.where(eq, 0.5, 0.0))

    def body(l_ref, dla_ref, dlc_ref, o_ref):
        s0, s1, args, lbs = _lb_terms(l_ref[...])
        da = []
        for i in range(2):
            a, lb = args[i], lbs[i]
            dlb = dla_ref[i:i + 1] * half(lb > LB_FLOOR, lb == LB_FLOOR) / jnp.maximum(lb, LB_FLOOR) - dlc_ref[i:i + 1] / (1.0 - lb)
            t = jnp.maximum(a, 0.0)
            da.append(dlb * half(a > 0.0, a == 0.0) * half(t < 1.0 - EPS, t == 1.0 - EPS))
        ds0 = (da[0] + da[1]) - (da[0] + da[1])
        ds1 = da[1]
        dot = s0 * ds0 + s1 * ds1
        o_ref[...] = jnp.concatenate([s0 * (ds0 - dot), s1 * (ds1 - dot)], axis=0)

    return _pcall(body, name=name, out_shape=jax.ShapeDtypeStruct(logits.shape, F32))(logits, dla, dlc)


def _mod_fwd(c8, w, b, *, name):
    _, d, n = w.shape
    tn = _tile(n, 768)

    def body(c_ref, w_ref, b_ref, o_ref):
        cv = c_ref[...]
        o_ref[...] = _dot((cv * _sigmoid(cv)).astype(BF16), w_ref[...].astype(BF16)) + b_ref[...]

    return _pcall(
        body, name=name, grid=(2, n // tn),
        in_specs=[pl.BlockSpec((8, d), lambda l, j: (0, 0)), pl.BlockSpec((None, d, tn), lambda l, j: (l, 0, j)),
                  pl.BlockSpec((None, 1, tn), lambda l, j: (l, 0, j))],
        out_specs=pl.BlockSpec((None, 8, tn), lambda l, j: (l, 0, j)),
        out_shape=jax.ShapeDtypeStruct((2, 8, n), F32), compiler_params=_cparams("parallel", "parallel"))(c8, w, b)


def _mod_bwd(ct, dm, *, name):
    d = ct.shape[0]
    n = dm.shape[2]
    tn = _tile(n, 768)

    def body(c_ref, dm_ref, o_ref):
        cv = c_ref[...]
        sc = cv * _sigmoid(cv)
        dv = dm_ref[...]
        acc = sc[:, 0:1] * dv[0:1, :]
        for b in range(1, 8):
            acc = acc + sc[:, b:b + 1] * dv[b:b + 1, :]
        o_ref[...] = acc

    return _pcall(
        body, name=name, grid=(2, n // tn),
        in_specs=[pl.BlockSpec((d, 8), lambda l, j: (0, 0)), pl.BlockSpec((None, 8, tn), lambda l, j: (l, 0, j))],
        out_specs=pl.BlockSpec((None, d, tn), lambda l, j: (l, 0, j)),
        out_shape=jax.ShapeDtypeStruct((2, d, n), F32), compiler_params=_cparams("parallel", "parallel"))(ct, dm)


_PEERS = {
    "chips": ((1, 0, 0), (0, 1, 0), (1, 1, 0)),
    "all": tuple((a, b, c) for a in (0, 1) for b in (0, 1) for c in (0, 1) if a + b + c),
    "sib": ((0, 0, 1),),
}
_SLOTS = {"chips": 4, "all": 8, "sib": 2}


def _slot(kind, x, y, c):
    return {"chips": 2 * x + y, "all": 4 * x + 2 * y + c, "sib": c}[kind]


def _exchange(arrs, kind, scatter, *, name):
    n = len(arrs)
    peers = _PEERS[kind]
    ns = _SLOTS[kind]
    np_ = len(peers)

    def pieces(a):
        shape = a.shape[1:] if scatter else a.shape
        if len(shape) == 2:
            for k in (XCHG_CHUNKS, XCHG_CHUNKS // 2, XCHG_CHUNKS // 4):
                if k > 1 and shape[0] % (16 * k) == 0 and shape[0] * shape[1] * a.dtype.itemsize >= k * XCHG_MIN_BYTES:
                    return [(i * (shape[0] // k), shape[0] // k) for i in range(k)]
        return [None]

    chunks = [pieces(a) for a in arrs]
    base = [sum(len(c) for c in chunks[:a]) * np_ for a in range(n)]
    total = sum(len(c) for c in chunks) * np_

    def body(*refs):
        ins, outs = refs[:n], refs[n:2 * n]
        send, recv, loc = refs[2 * n:]
        x, y, c = lax.axis_index("x"), lax.axis_index("y"), lax.axis_index("c")
        me = _slot(kind, x, y, c)
        copies = []
        for a in range(n):
            own = pltpu.make_async_copy(ins[a].at[me] if scatter else ins[a], outs[a].at[me], loc.at[a])
            own.start()
            copies.append(own)
            for j, (dx, dy, dc) in enumerate(peers):
                px, py, pc = (1 - x if dx else x), (1 - y if dy else y), (1 - c if dc else c)
                src = ins[a].at[_slot(kind, px, py, pc)] if scatter else ins[a]
                for i, piece in enumerate(chunks[a]):
                    rows = slice(None) if piece is None else pl.ds(piece[0], piece[1])
                    sem = base[a] + j * len(chunks[a]) + i
                    cp = pltpu.make_async_remote_copy(
                        src_ref=src if piece is None else src.at[rows], dst_ref=outs[a].at[me] if piece is None else outs[a].at[me, rows],
                        send_sem=send.at[sem], recv_sem=recv.at[sem], device_id=(px, py, pc), device_id_type=MESH_ID)
                    cp.start()
                    copies.append(cp)
        for cp in copies:
            cp.wait()

    hbm = pl.BlockSpec(memory_space=pl.ANY)
    shapes = [jax.ShapeDtypeStruct(a.shape if scatter else (ns,) + a.shape, a.dtype) for a in arrs]
    return _pcall(
        body, name=name, in_specs=[hbm] * n, out_specs=[hbm] * n, out_shape=shapes,
        scratch_shapes=[pltpu.SemaphoreType.DMA((total,)), pltpu.SemaphoreType.DMA((total,)), pltpu.SemaphoreType.DMA((n,))],
    )(*arrs)


def _sum_slots(a, *, name, out_dtype=F32):
    ns, r, c = a.shape
    t = _tile(r, max(16, (1 << 18) // c // 16 * 16), 16)

    def body(a_ref, o_ref):
        acc = a_ref[0].astype(F32)
        for i in range(1, ns):
            acc = acc + a_ref[i].astype(F32)
        o_ref[...] = acc.astype(o_ref.dtype)

    return _pcall(body, name=name, grid=(r // t,), in_specs=[pl.BlockSpec((ns, t, c), lambda i: (0, i, 0))],
                  out_specs=pl.BlockSpec((t, c), lambda i: (i, 0)), out_shape=jax.ShapeDtypeStruct((r, c), out_dtype),
                  compiler_params=_cparams("parallel"))(a)


def _adamw(w, gparts, m, v, *, name):
    r, c = w.shape
    t = _tile(r, max(16, (1 << 17) // c // 16 * 16), 16)
    ng = len(gparts)

    def body(*refs):
        w_ref, m_ref, v_ref = refs[0], refs[1 + ng], refs[2 + ng]
        g_ref, d_ref, nm_ref, nv_ref = refs[3 + ng:]
        g = refs[1][...].astype(F32)
        for i in range(1, ng):
            g = g + refs[1 + i][...].astype(F32)
        mn = ADAM_B1 * m_ref[...] + (1.0 - ADAM_B1) * g
        vn = ADAM_B2 * v_ref[...] + (1.0 - ADAM_B2) * (g * g)
        m_hat = mn / (1.0 - ADAM_B1 ** ADAM_STEP)
        v_hat = vn / (1.0 - ADAM_B2 ** ADAM_STEP)
        g_ref[...] = g
        d_ref[...] = -ADAM_LR * (m_hat / (jnp.sqrt(v_hat) + ADAM_EPS) + ADAM_WD * w_ref[...])
        nm_ref[...] = mn
        nv_ref[...] = vn

    spec = pl.BlockSpec((t, c), lambda i: (i, 0))
    return _pcall(body, name=name, grid=(r // t,), in_specs=[spec] * (3 + ng), out_specs=[spec] * 4,
                  out_shape=[jax.ShapeDtypeStruct((r, c), F32)] * 4, compiler_params=_cparams("parallel"))(w, *gparts, m, v)


def _ffn_fwd(x, sh, sc, g, wgu, wd, tag):
    h = _norm_mod(x, sc, sh, name=f"{tag}_norm")
    uv = _mm(h, wgu, name=f"{tag}_up", tm=512, tn=1408, tk=1024)
    a = _swiglu(uv, name=f"{tag}_act")
    y, xo = _mm(a, wd, name=f"{tag}_down", tm=512, tn=1024, tk=1408, res=x, scale=0.5 * g)
    return xo, (x, h, uv, a, y)


def _ffn_bwd(dxo, saved, sc, g, wgu, wd, tag):
    x, h, uv, a, y = saved
    dyb, dgs = _gate_bwd(dxo, y, 0.5 * g, name=f"{tag}_dgate")
    da = _mm(dyb, wd, name=f"{tag}_dact", tb=True, tm=512, tn=1408, tk=1024)
    duv = _swiglu_bwd(da, uv, name=f"{tag}_dswi")
    dh = _mm(duv, wgu, name=f"{tag}_dh", tb=True, tm=512, tn=1024, tk=1408)
    dx, dsc, dsh = _norm_mod_bwd(dh, x, sc, dxo, name=f"{tag}_dnorm")
    dwgu = _mm_tn(h, duv, name=f"{tag}_dwgu", tm=1024, tn=1408, tk=512)
    dwd = _mm_tn(a, dyb, name=f"{tag}_dwd", tm=1408, tn=1024, tk=512)
    return dx, dwgu, dwd, dsh, dsc, 0.5 * dgs


def _layer_fwd(x0, mod, w, par, tag):
    s = x0.shape[0]
    sh1, sc1, g1, sh2, sc2, g2, sh3, sc3, g3 = (mod[i:i + 1] for i in range(N_MOD))
    x1, f1 = _ffn_fwd(x0, sh1, sc1, g1, w["gu1"], w["d1"], f"{tag}_ffn1")
    h2 = _norm_mod(x1, sc2, sh2, name=f"{tag}_mix_norm")
    proj = _mm(h2, w["in"], name=f"{tag}_in", tm=512, tn=512, tk=1024)
    qkv = proj[:, :3 * SB_W].astype(BF16)
    vd = proj[:, 3 * SB_W + 2 * DIL_W:3 * SB_W + 3 * DIL_W].astype(BF16)
    o_a = _sb_fwd(qkv, name=f"{tag}_sb")
    qd, kd = _dil_prep(proj, par["gq"], par["gk"], par["cos"], par["sin"], name=f"{tag}_dil_prep")
    nums, dens, mxs = [], [], []
    for _, r in DIL_PATTERNS:
        nu, de, mx = _dil_fwd(qd, kd, vd, r, name=f"{tag}_dil{r}")
        nums.append(nu.reshape(s, DIL_W))
        dens.append(de.reshape(s, DIL_W))
        mxs.append(mx.reshape(s, DIL_W))
    oh, st = _hgrn_fwd(proj, par["la"], par["lc"], name=f"{tag}_hgrn")
    ymix, od, mall, zall = _mix_out(o_a, nums, dens, mxs, oh, proj, par["hg"], name=f"{tag}_mix_out")
    out, x2 = _mm(ymix, w["out"], name=f"{tag}_out", tm=512, tn=1024, tk=1024, res=x1, scale=g2)
    x3, f2 = _ffn_fwd(x2, sh3, sc3, g3, w["gu2"], w["d2"], f"{tag}_ffn2")
    return x3, dict(f1=f1, f2=f2, x1=x1, h2=h2, proj=proj, qkv=qkv, vd=vd, o_a=o_a, qd=qd, kd=kd, oh=oh, st=st,
                    ymix=ymix, od=od, mall=mall, zall=zall, out=out)


def _layer_bwd(dx3, sv, mod, w, par, tag):
    s = dx3.shape[0]
    sh1, sc1, g1, sh2, sc2, g2, sh3, sc3, g3 = (mod[i:i + 1] for i in range(N_MOD))
    dx2, dwgu2, dwd2, dsh3, dsc3, dg3 = _ffn_bwd(dx3, sv["f2"], sc3, g3, w["gu2"], w["d2"], f"{tag}_ffn2")
    doutb, dg2 = _gate_bwd(dx2, sv["out"], g2, name=f"{tag}_dgate2")
    dmix = _mm(doutb, w["out"], name=f"{tag}_dmix", tb=True, tm=512, tn=1024, tk=1024)
    dwout = _mm_tn(sv["ymix"], doutb, name=f"{tag}_dwout", tm=1024, tn=1024, tk=512)
    proj = sv["proj"]
    doh, dgh, delta, dhg = _mix_out_bwd(dmix, sv["oh"], proj, par["hg"], sv["od"], name=f"{tag}_dmix_out")
    dqa, dka, dva = _sb_bwd(sv["qkv"], sv["o_a"], dmix, name=f"{tag}_dsb")
    do_d = dmix[:, SB_W:SB_W + DIL_W]
    dqs, dks, dvs = [], [], []
    for _, r in DIL_PATTERNS:
        a, b, c = _dil_bwd(sv["qd"], sv["kd"], sv["vd"], do_d, sv["mall"], sv["zall"], delta, r, name=f"{tag}_ddil{r}")
        dqs.append(a.reshape(s, DIL_W))
        dks.append(b.reshape(s, DIL_W))
        dvs.append(c.reshape(s, DIL_W))
    dqd, dkd, dvd, dgq, dgk = _dil_prep_bwd(proj, par["gq"], par["gk"], par["cos"], par["sin"], dqs, dks, dvs,
                                             name=f"{tag}_ddil_prep")
    dqh, dfh, dih, dla, dlc = _hgrn_bwd(proj, par["la"], par["lc"], sv["st"], doh, name=f"{tag}_dhgrn")
    dproj = jnp.concatenate([dqa, dka, dva, dqd, dkd, dvd, dqh, dfh, dih, dgh], axis=1).astype(BF16)
    dh2 = _mm(dproj, w["in"], name=f"{tag}_dh2", tb=True, tm=512, tn=1024, tk=512)
    dwin = _mm_tn(sv["h2"], dproj, name=f"{tag}_dwin", tm=1024, tn=1792, tk=512)
    dx1, dsc2, dsh2 = _norm_mod_bwd(dh2, sv["x1"], sc2, dx2, name=f"{tag}_dmix_norm")
    dx0, dwgu1, dwd1, dsh1, dsc1, dg1 = _ffn_bwd(dx1, sv["f1"], sc1, g1, w["gu1"], w["d1"], f"{tag}_ffn1")
    dmod = jnp.concatenate([dsh1, dsc1, dg1, dsh2, dsc2, dg2, dsh3, dsc3, dg3], axis=0)
    fold = lambda v: v.reshape(2, HEAD_DIM).sum(axis=0)
    grads = dict(gu1=dwgu1, d1=dwd1, gu2=dwgu2, d2=dwd2, win=dwin, wout=dwout, dmod=dmod, gq=fold(dgq), gk=fold(dgk),
                 hg=dhg[0], la=dla[0], lc=dlc[0])
    return dx0, grads


def _pack(pieces):
    flat = jnp.concatenate([p.reshape(-1) for p in pieces])
    pad = (-flat.shape[0]) % (8 * LANES)
    return jnp.pad(flat, (0, pad)).reshape(-1, LANES)


def _unpack(flat, like):
    out, off = [], 0
    for p in like:
        out.append(flat[off:off + p.size].reshape(p.shape))
        off += p.size
    return out


def kernel(x, c, w_mod, b_mod, ffn1_w_gate, ffn1_w_up, ffn1_w_down, w_in, w_out, q_norm_g, k_norm_g, hgrn_norm_g, hgrn_lb_logits, ffn2_w_gate, ffn2_w_up, ffn2_w_down, loss_target, m_w_mod, m_b_mod, m_ffn1_w_gate, m_ffn1_w_up, m_ffn1_w_down, m_w_in, m_w_out, m_q_norm_g, m_k_norm_g, m_hgrn_norm_g, m_hgrn_lb_logits, m_ffn2_w_gate, m_ffn2_w_up, m_ffn2_w_down, v_w_mod, v_b_mod, v_ffn1_w_gate, v_ffn1_w_up, v_ffn1_w_down, v_w_in, v_w_out, v_q_norm_g, v_k_norm_g, v_hgrn_norm_g, v_hgrn_lb_logits, v_ffn2_w_gate, v_ffn2_w_up, v_ffn2_w_down):
    names = ["w_mod", "b_mod", "ffn1_w_gate", "ffn1_w_up", "ffn1_w_down", "w_in", "w_out", "q_norm_g", "k_norm_g",
             "hgrn_norm_g", "hgrn_lb_logits", "ffn2_w_gate", "ffn2_w_up", "ffn2_w_down"]
    wts = dict(zip(names, (w_mod, b_mod, ffn1_w_gate, ffn1_w_up, ffn1_w_down, w_in, w_out, q_norm_g, k_norm_g, hgrn_norm_g,
                           hgrn_lb_logits, ffn2_w_gate, ffn2_w_up, ffn2_w_down)))
    mom = dict(zip(names, (m_w_mod, m_b_mod, m_ffn1_w_gate, m_ffn1_w_up, m_ffn1_w_down, m_w_in, m_w_out, m_q_norm_g, m_k_norm_g,
                           m_hgrn_norm_g, m_hgrn_lb_logits, m_ffn2_w_gate, m_ffn2_w_up, m_ffn2_w_down)))
    var = dict(zip(names, (v_w_mod, v_b_mod, v_ffn1_w_gate, v_ffn1_w_up, v_ffn1_w_down, v_w_in, v_w_out, v_q_norm_g, v_k_norm_g,
                           v_hgrn_norm_g, v_hgrn_lb_logits, v_ffn2_w_gate, v_ffn2_w_up, v_ffn2_w_down)))
    depth = w_mod.shape[0]
    assert depth == 2 and x.shape[0] == 1
    s, d = x.shape[1:]
    assert s % (DIL_PATTERNS[-1][1] * QBLK) == 0 and d % LANES == 0
    xi, yi, ci = lax.axis_index("x"), lax.axis_index("y"), lax.axis_index("c")
    chip = 2 * xi + yi
    dev = 2 * chip + ci
    x0, tgt = x[0], loss_target[0]

    c8 = _exchange([c.reshape(d // LANES, LANES)], "all", False, name="gather_c")[0].reshape(8, d)
    ncol = w_mod.shape[2]
    b_loc = lax.dynamic_slice_in_dim(b_mod, chip * ncol, ncol, axis=1)
    m_loc = _mod_fwd(c8, w_mod, b_loc.reshape(depth, 1, ncol), name="mod_fwd")
    m_all = _exchange([m_loc], "chips", False, name="gather_mod")[0]
    mod = jnp.transpose(lax.dynamic_index_in_dim(m_all, dev, axis=2, keepdims=False), (1, 0, 2)).reshape(depth, N_MOD, d)

    col_sharded = ["ffn1_w_gate", "ffn1_w_up", "w_in", "ffn2_w_gate", "ffn2_w_up"]
    row_sharded = ["ffn1_w_down", "w_out", "ffn2_w_down"]
    big = col_sharded + row_sharded
    flat = [wts[n].astype(BF16).reshape(-1, wts[n].shape[-1]) for n in big]
    gathered = {n: g.reshape((4,) + wts[n].shape) for n, g in zip(big, _exchange(flat, "chips", False, name="gather_w"))}
    full = {}
    for n in col_sharded:
        g = gathered[n]
        full[n] = jnp.moveaxis(g, 0, 2).reshape(depth, g.shape[2], -1)
    for n in row_sharded:
        g = gathered[n]
        full[n] = jnp.moveaxis(g, 0, 1).reshape(depth, -1, g.shape[3])
    ws = [dict(gu1=jnp.concatenate([full["ffn1_w_gate"][l], full["ffn1_w_up"][l]], axis=1), d1=full["ffn1_w_down"][l],
               gu2=jnp.concatenate([full["ffn2_w_gate"][l], full["ffn2_w_up"][l]], axis=1), d2=full["ffn2_w_down"][l],
               **{"in": full["w_in"][l], "out": full["w_out"][l]}) for l in range(depth)]

    _, la, lc = _lb_prep(hgrn_lb_logits, name="lb_prep")
    cos, sin = _rope_tables(s)
    pars = [dict(gq=jnp.tile(q_norm_g[l], 2)[None], gk=jnp.tile(k_norm_g[l], 2)[None], hg=hgrn_norm_g[l][None],
                 la=la[l:l + 1], lc=lc[l:l + 1], cos=cos, sin=sin) for l in range(depth)]

    xs, saved = x0, []
    for l in range(depth):
        xs, sv = _layer_fwd(xs, mod[l], ws[l], pars[l], f"l{l}")
        saved.append(sv)
    dx, lpart = _loss_grad(xs, tgt, name="loss")
    grads = [None] * depth
    for l in reversed(range(depth)):
        dx, grads[l] = _layer_bwd(dx, saved[l], mod[l], ws[l], pars[l], f"l{l}")

    stack = lambda k: jnp.stack([grads[l][k] for l in range(depth)])
    small = [stack("dmod"), stack("gq"), stack("gk"), stack("hg"), stack("la"), stack("lc"), lpart[0, :1]]
    packed = _pack(small)
    allp = _exchange([packed], "all", False, name="gather_small")[0]
    tot = _unpack(_sum_slots(allp, name="sum_small").reshape(-1), small)
    g_b_mod = tot[0].reshape(depth, N_MOD * d)
    loss = tot[6][0]
    g_small = {"b_mod": g_b_mod, "q_norm_g": tot[1], "k_norm_g": tot[2], "hgrn_norm_g": tot[3],
               "hgrn_lb_logits": _lb_bwd(hgrn_lb_logits, tot[4], tot[5], name="lb_bwd")}

    dm_all = allp.reshape(8, -1)[:, :depth * N_MOD * d].reshape(8, depth, N_MOD * d)
    dm_loc = jnp.transpose(lax.dynamic_slice_in_dim(dm_all, chip * ncol, ncol, axis=2), (1, 0, 2))
    g_w_mod = _mod_bwd(c8.T, dm_loc, name="mod_bwd")

    fgrad = {
        "ffn1_w_gate": jnp.stack([grads[l]["gu1"][:, :grads[l]["gu1"].shape[1] // 2] for l in range(depth)]),
        "ffn1_w_up": jnp.stack([grads[l]["gu1"][:, grads[l]["gu1"].shape[1] // 2:] for l in range(depth)]),
        "ffn2_w_gate": jnp.stack([grads[l]["gu2"][:, :grads[l]["gu2"].shape[1] // 2] for l in range(depth)]),
        "ffn2_w_up": jnp.stack([grads[l]["gu2"][:, grads[l]["gu2"].shape[1] // 2:] for l in range(depth)]),
        "w_in": stack("win"), "ffn1_w_down": stack("d1"), "ffn2_w_down": stack("d2"), "w_out": stack("wout"),
    }
    by_chip = []
    for n in big:
        g = fgrad[n]
        if n in col_sharded:
            g = jnp.moveaxis(g.reshape(depth, g.shape[1], 4, -1), 2, 0)
        else:
            g = jnp.moveaxis(g.reshape(depth, 4, -1, g.shape[2]), 1, 0)
        by_chip.append(g.reshape(4, -1, g.shape[-1]))
    got = _exchange(by_chip, "chips", True, name="scatter_grads")
    parts = [_sum_slots(g, name=f"sum_{n}", out_dtype=BF16) for n, g in zip(big, got)]
    both = dict(zip(big, _exchange(parts, "sib", False, name="swap_grads")))

    outs = {}
    for n in names:
        w2 = wts[n].reshape(-1, wts[n].shape[-1])
        if n in both:
            gp = [both[n][0], both[n][1]]
        elif n == "w_mod":
            gp = [g_w_mod.reshape(w2.shape)]
        else:
            gp = [g_small[n].reshape(w2.shape)]
        res = _adamw(w2, gp, mom[n].reshape(w2.shape), var[n].reshape(w2.shape), name=f"adamw_{n}")
        outs[n] = [r.reshape(wts[n].shape) for r in res]
    return (loss, dx[None], *[outs[n][0] for n in names], *[outs[n][1] for n in names], *[outs[n][2] for n in names],
            *[outs[n][3] for n in names])
```

```python
import functools
import math

import jax
import jax.numpy as jnp
from jax import lax
from jax.experimental import pallas as pl
from jax.experimental.pallas import tpu as pltpu

F32 = jnp.float32
BF16 = jnp.bfloat16
MESH_ID = pl.DeviceIdType.MESH

HEAD_DIM = 64
SB_W = 256
DIL_W = 256
HG_W = 512
HG_D = 128
IN_W = 3 * SB_W + 3 * DIL_W + 4 * HG_W
MIX_W = SB_W + DIL_W + HG_W
DIL_PATTERNS = ((128, 1), (512, 4), (2048, 16))
ROPE_THETA = 10000.0
EPS = 1e-6
LB_FLOOR = 1e-30
NEG_BIG = -1e30
N_MOD = 9
ADAM_LR = 0.001
ADAM_B1 = 0.9
ADAM_B2 = 0.999
ADAM_EPS = 1e-08
ADAM_WD = 0.01
ADAM_STEP = 10

LANES = 128
QBLK = 128
DIL_TILE = 512
HG_BLK = 16
HG_TILE = 256
SB_EXIT = 88.0
VMEM_LIMIT = 48 * 1024 * 1024
XCHG_CHUNKS = 8
XCHG_MIN_BYTES = 1 << 19

NN = (((1,), (0,)), ((), ()))
NT = (((1,), (1,)), ((), ()))
TN = (((0,), (0,)), ((), ()))


def _pcall(body, **kw):
    return pl.pallas_call(body, **kw)


def _cparams(*sem):
    return pltpu.CompilerParams(dimension_semantics=sem if sem else None, vmem_limit_bytes=VMEM_LIMIT)


def _dot(a, b, dims=NN):
    return lax.dot_general(a, b, dims, preferred_element_type=F32)


def _split(x, n):
    parts = []
    r = x
    for i in range(n):
        p = r.astype(BF16)
        parts.append(p)
        if i + 1 < n:
            r = r - p.astype(F32)
    return parts


def _xdot(x, m, n=2):
    return sum(_dot(p, m) for p in _split(x, n))


def _xdot_left(m, x, n=3):
    return sum(_dot(m, p) for p in _split(x, n))


def _iota(shape, dim):
    return lax.broadcasted_iota(jnp.int32, shape, dim)


def _sigmoid(x):
    return 1.0 / (1.0 + jnp.exp(-x))


def _tile(dim, pref, mult=LANES):
    t = (min(pref, dim) // mult) * mult
    while t >= mult:
        if dim % t == 0:
            return t
        t -= mult
    return dim


def _rows(dim, pref):
    return _tile(dim, pref, 8)


def _mm(a, b, *, name, tb=False, tm=512, tn=1024, tk=1024, out_dtype=F32, res=None, scale=None):
    m, kd = a.shape
    n = b.shape[0] if tb else b.shape[1]
    tm, tn, tk = _rows(m, tm), _tile(n, tn), _tile(kd, tk)
    nk = kd // tk
    epi = res is not None

    def body(*refs):
        if epi:
            a_ref, b_ref, r_ref, s_ref, o_ref, x_ref, acc = refs
        else:
            a_ref, b_ref, o_ref, acc = refs
        k = pl.program_id(2)

        @pl.when(k == 0)
        def _():
            acc[...] = jnp.zeros_like(acc)

        acc[...] += _dot(a_ref[...], b_ref[...], NT if tb else NN)

        @pl.when(k == nk - 1)
        def _():
            o_ref[...] = acc[...].astype(o_ref.dtype)
            if epi:
                x_ref[...] = r_ref[...] + s_ref[...] * acc[...]

    in_specs = [
        pl.BlockSpec((tm, tk), lambda i, j, k: (i, k)),
        pl.BlockSpec((tn, tk), lambda i, j, k: (j, k)) if tb else pl.BlockSpec((tk, tn), lambda i, j, k: (k, j)),
    ]
    out_shape = [jax.ShapeDtypeStruct((m, n), out_dtype)]
    out_specs = [pl.BlockSpec((tm, tn), lambda i, j, k: (i, j))]
    args = [a, b]
    if epi:
        in_specs += [pl.BlockSpec((tm, tn), lambda i, j, k: (i, j)), pl.BlockSpec((1, tn), lambda i, j, k: (0, j))]
        out_shape.append(jax.ShapeDtypeStruct((m, n), F32))
        out_specs.append(pl.BlockSpec((tm, tn), lambda i, j, k: (i, j)))
        args += [res, scale]
    out = _pcall(
        body, name=name, grid=(m // tm, n // tn, nk), in_specs=in_specs, out_specs=out_specs, out_shape=out_shape,
        scratch_shapes=[pltpu.VMEM((tm, tn), F32)], compiler_params=_cparams("parallel", "parallel", "arbitrary"),
    )(*args)
    return out if epi else out[0]


def _mm_tn(a, b, *, name, tm=1024, tn=1408, tk=512, out_dtype=BF16):
    s, m = a.shape
    n = b.shape[1]
    tm, tn, tk = _tile(m, tm), _tile(n, tn), _rows(s, tk)
    nk = s // tk

    def body(a_ref, b_ref, o_ref, acc):
        k = pl.program_id(2)

        @pl.when(k == 0)
        def _():
            acc[...] = jnp.zeros_like(acc)

        acc[...] += _dot(a_ref[...], b_ref[...], TN)

        @pl.when(k == nk - 1)
        def _():
            o_ref[...] = acc[...].astype(o_ref.dtype)

    return _pcall(
        body, name=name, grid=(m // tm, n // tn, nk),
        in_specs=[pl.BlockSpec((tk, tm), lambda i, j, k: (k, i)), pl.BlockSpec((tk, tn), lambda i, j, k: (k, j))],
        out_specs=pl.BlockSpec((tm, tn), lambda i, j, k: (i, j)), out_shape=jax.ShapeDtypeStruct((m, n), out_dtype),
        scratch_shapes=[pltpu.VMEM((tm, tn), F32)], compiler_params=_cparams("parallel", "parallel", "arbitrary"),
    )(a, b)


TE = 512


def _row_spec(t, w, col=0):
    return pl.BlockSpec((t, w), lambda i, col=col: (i, col))


def _vec_spec(w, col=0):
    return pl.BlockSpec((1, w), lambda i, col=col: (0, col))


def _norm_mod(x, sc, sh, *, name):
    s, d = x.shape
    t = _rows(s, TE)

    def body(x_ref, sc_ref, sh_ref, h_ref):
        xv = x_ref[...]
        r = lax.rsqrt(jnp.mean(xv * xv, axis=1, keepdims=True) + EPS)
        h_ref[...] = ((xv * r) * (1.0 + sc_ref[...]) + sh_ref[...]).astype(BF16)

    return _pcall(body, name=name, grid=(s // t,), in_specs=[_row_spec(t, d), _vec_spec(d), _vec_spec(d)],
                  out_specs=_row_spec(t, d), out_shape=jax.ShapeDtypeStruct((s, d), BF16),
                  compiler_params=_cparams("parallel"))(x, sc, sh)


def _norm_mod_bwd(dh, x, sc, dxo, *, name):
    s, d = x.shape
    t = _rows(s, TE)

    def body(dh_ref, x_ref, sc_ref, dxo_ref, dx_ref, dsc_ref, dsh_ref):
        @pl.when(pl.program_id(0) == 0)
        def _():
            dsc_ref[...] = jnp.zeros_like(dsc_ref)
            dsh_ref[...] = jnp.zeros_like(dsh_ref)

        xv = x_ref[...]
        dhv = dh_ref[...]
        r = lax.rsqrt(jnp.mean(xv * xv, axis=1, keepdims=True) + EPS)
        xn = xv * r
        dxn = dhv * (1.0 + sc_ref[...])
        dx_ref[...] = dxo_ref[...] + r * (dxn - xn * jnp.mean(dxn * xn, axis=1, keepdims=True))
        dsc_ref[...] += jnp.sum(dhv * xn, axis=0, keepdims=True)
        dsh_ref[...] += jnp.sum(dhv, axis=0, keepdims=True)

    return _pcall(
        body, name=name, grid=(s // t,),
        in_specs=[_row_spec(t, d), _row_spec(t, d), _vec_spec(d), _row_spec(t, d)],
        out_specs=[_row_spec(t, d), _vec_spec(d), _vec_spec(d)],
        out_shape=[jax.ShapeDtypeStruct((s, d), F32), jax.ShapeDtypeStruct((1, d), F32), jax.ShapeDtypeStruct((1, d), F32)],
        compiler_params=_cparams("arbitrary"))(dh, x, sc, dxo)


def _swiglu(uv, *, name):
    s, f2 = uv.shape
    f = f2 // 2
    t = _rows(s, 256)

    def body(uv_ref, a_ref):
        u = uv_ref[:, :f]
        v = uv_ref[:, f:]
        a_ref[...] = (u * _sigmoid(u) * v).astype(BF16)

    return _pcall(body, name=name, grid=(s // t,), in_specs=[_row_spec(t, f2)], out_specs=_row_spec(t, f),
                  out_shape=jax.ShapeDtypeStruct((s, f), BF16), compiler_params=_cparams("parallel"))(uv)


def _swiglu_bwd(da, uv, *, name):
    s, f2 = uv.shape
    f = f2 // 2
    t = _rows(s, 256)

    def body(da_ref, uv_ref, d_ref):
        u = uv_ref[:, :f]
        v = uv_ref[:, f:]
        dav = da_ref[...]
        sg = _sigmoid(u)
        d_ref[:, :f] = (dav * v * (sg * (1.0 + u * (1.0 - sg)))).astype(BF16)
        d_ref[:, f:] = (dav * (u * sg)).astype(BF16)

    return _pcall(body, name=name, grid=(s // t,), in_specs=[_row_spec(t, f), _row_spec(t, f2)],
                  out_specs=_row_spec(t, f2), out_shape=jax.ShapeDtypeStruct((s, f2), BF16),
                  compiler_params=_cparams("parallel"))(da, uv)


def _gate_bwd(dxo, y, sg, *, name):
    s, d = y.shape
    t = _rows(s, TE)

    def body(dxo_ref, y_ref, sg_ref, dy_ref, ds_ref):
        @pl.when(pl.program_id(0) == 0)
        def _():
            ds_ref[...] = jnp.zeros_like(ds_ref)

        dv = dxo_ref[...]
        dy_ref[...] = (sg_ref[...] * dv).astype(BF16)
        ds_ref[...] += jnp.sum(dv * y_ref[...], axis=0, keepdims=True)

    return _pcall(
        body, name=name, grid=(s // t,), in_specs=[_row_spec(t, d), _row_spec(t, d), _vec_spec(d)],
        out_specs=[_row_spec(t, d), _vec_spec(d)],
        out_shape=[jax.ShapeDtypeStruct((s, d), BF16), jax.ShapeDtypeStruct((1, d), F32)],
        compiler_params=_cparams("arbitrary"))(dxo, y, sg)


def _loss_grad(y, tgt, *, name):
    s, d = y.shape
    t = _rows(s, TE)
    nt = s // t

    def body(y_ref, t_ref, dy_ref, l_ref, acc):
        i = pl.program_id(0)

        @pl.when(i == 0)
        def _():
            acc[...] = jnp.zeros_like(acc)

        e = y_ref[...] - t_ref[...]
        dy_ref[...] = e * (1.0 / d)
        acc[...] += jnp.sum(e * e, axis=0, keepdims=True)

        @pl.when(i == nt - 1)
        def _():
            l_ref[...] = jnp.broadcast_to(jnp.sum(acc[...], axis=1, keepdims=True) * (0.5 / d), l_ref.shape)

    return _pcall(
        body, name=name, grid=(nt,), in_specs=[_row_spec(t, d), _row_spec(t, d)],
        out_specs=[_row_spec(t, d), pl.BlockSpec((1, LANES), lambda i: (0, 0))],
        out_shape=[jax.ShapeDtypeStruct((s, d), F32), jax.ShapeDtypeStruct((1, LANES), F32)],
        scratch_shapes=[pltpu.VMEM((1, d), F32)], compiler_params=_cparams("arbitrary"))(y, tgt)


def _sb_consts():
    row = _iota((QBLK, LANES), 0)
    lane = _iota((QBLK, LANES), 1)
    ones = jnp.ones((QBLK, LANES), BF16)
    after = jnp.concatenate([jnp.where(row > lane, 1.0, 0.0).astype(BF16), ones], axis=1)
    from_ = jnp.concatenate([jnp.where(row >= lane, 1.0, 0.0).astype(BF16), ones], axis=1)
    return row, lane, after, from_


def _sb_scores(qm, kb, strict):
    z = _dot(qm, kb, NT) * (HEAD_DIM ** -0.5)
    sp = jnp.log(1.0 + jnp.exp(-jnp.abs(z)))
    lnb = -(jnp.maximum(z, 0.0) + sp)
    lb = jnp.minimum(z, 0.0) - sp
    if strict is not None:
        lnb = jnp.where(strict, lnb, 0.0)
    return lnb, lb


def _sb_fwd(qkv, *, name):
    s = qkv.shape[0]
    nq = s // QBLK

    def body(q_ref, k_ref, v_ref, o_ref, *scr):
        acc, osc = scr[:4], scr[4:]
        qi = pl.program_id(0)
        row, lane, after, _ = _sb_consts()
        strict = lane < row
        h0 = lane < HEAD_DIM
        q = q_ref[...]
        qms = []
        for p in range(2):
            qp = q[:, p * LANES:(p + 1) * LANES]
            qms += [jnp.where(h0, qp, jnp.zeros_like(qp)), jnp.where(h0, jnp.zeros_like(qp), qp)]

        def block(kj, mask):
            off = pl.multiple_of(kj * QBLK, QBLK)
            kbs = [k_ref[pl.ds(off, QBLK), p * LANES:(p + 1) * LANES] for p in range(2)]
            vbs = [v_ref[pl.ds(off, QBLK), p * LANES:(p + 1) * LANES] for p in range(2)]
            sc = [_sb_scores(qms[c], kbs[c // 2], mask) for c in range(4)]
            trs = [_xdot(sc[c][0], after) for c in range(4)]
            top = None
            for c in range(4):
                w = jnp.exp(sc[c][1] + trs[c][:, :QBLK] + acc[c][...])
                if mask is not None:
                    w = jnp.where(mask, w, 0.0)
                osc[c][...] += _xdot(w, vbs[c // 2])
                new = acc[c][...] + trs[c][:, QBLK:]
                acc[c][...] = new
                top = new if top is None else jnp.maximum(top, new)
            return jnp.max(top)

        for ref in scr:
            ref[...] = jnp.zeros_like(ref)
        top = block(qi, strict)
        lax.while_loop(lambda c: (c[0] >= 0) & (c[1] > -SB_EXIT), lambda c: (c[0] - 1, block(c[0], None)), (qi - 1, top))
        for p in range(2):
            o_ref[:, p * LANES:(p + 1) * LANES] = jnp.where(h0, osc[2 * p][...], osc[2 * p + 1][...])

    return _pcall(
        body, name=name, grid=(nq,),
        in_specs=[pl.BlockSpec((QBLK, SB_W), lambda i: (i, 0)),
                  pl.BlockSpec((s, SB_W), lambda i: (0, 1)),
                  pl.BlockSpec((s, SB_W), lambda i: (0, 2))],
        out_specs=pl.BlockSpec((QBLK, SB_W), lambda i: (i, 0)),
        out_shape=jax.ShapeDtypeStruct((s, SB_W), F32),
        scratch_shapes=[pltpu.VMEM((QBLK, LANES), F32)] * 8,
        compiler_params=_cparams("arbitrary"))(qkv, qkv, qkv)


def _sb_bwd(qkv, o, dmix, *, name):
    s = qkv.shape[0]
    nq = s // QBLK
    scale = HEAD_DIM ** -0.5

    def body(q_ref, k_ref, v_ref, o_ref, do_ref, dq_ref, dk_ref, dv_ref, a0, a1, r0, r1, dqs, dks, dvs):
        acc, racc = (a0, a1), (r0, r1)
        i = pl.program_id(1)
        qi = nq - 1 - i
        row, lane, after, from_ = _sb_consts()
        strict = lane < row

        @pl.when(i == 0)
        def _():
            dks[...] = jnp.zeros_like(dks)
            dvs[...] = jnp.zeros_like(dvs)

        q = q_ref[...]
        do = do_ref[...]
        dob = do.astype(BF16)
        dd = do * o_ref[...]
        dol = (do - dob.astype(F32)).astype(BF16)
        zero = jnp.zeros_like(q)
        hms = (lane < HEAD_DIM, lane >= HEAD_DIM)
        qms = [jnp.where(hm, q, zero) for hm in hms]
        doms = [jnp.where(hm, dob, zero) for hm in hms]
        dols = [jnp.where(hm, dol, zero) for hm in hms]
        dsums = [jnp.sum(jnp.where(hm, dd, 0.0), axis=1, keepdims=True) for hm in hms]

        def block(kj, mask):
            off = pl.multiple_of(kj * QBLK, QBLK)
            kb = k_ref[pl.ds(off, QBLK), :]
            vb = v_ref[pl.ds(off, QBLK), :]
            top, dq, dk, dv = None, None, None, None
            sc = [_sb_scores(qms[h], kb, mask) for h in range(2)]
            trs = [_xdot(sc[h][0], after) for h in range(2)]
            dws = [_dot(doms[h], vb, NT) + _dot(dols[h], vb, NT) for h in range(2)]
            for h in range(2):
                lb, tr = sc[h][1], trs[h]
                w = jnp.exp(lb + tr[:, :QBLK] + acc[h][...])
                if mask is not None:
                    w = jnp.where(mask, w, 0.0)
                g = w * dws[h]
                tg = _xdot(g, from_)
                before = dsums[h] - (tg[:, :QBLK] + racc[h][...])
                dz = g - jnp.exp(lb) * (g + before)
                if mask is not None:
                    dz = jnp.where(mask, dz, 0.0)
                dzb = (dz * scale).astype(BF16)
                dqh = _dot(dzb, jnp.where(hms[h], kb, zero))
                dkh = _dot(dzb, qms[h], TN)
                dvh = _dot(w.astype(BF16), doms[h], TN)
                dq, dk, dv = (dqh, dkh, dvh) if h == 0 else (dq + dqh, dk + dkh, dv + dvh)
                new = acc[h][...] + tr[:, QBLK:]
                acc[h][...] = new
                racc[h][...] += tg[:, QBLK:]
                top = new if top is None else jnp.maximum(top, new)
            dqs[...] += dq
            dks[pl.ds(off, QBLK), :] += dk
            dvs[pl.ds(off, QBLK), :] += dv
            return jnp.max(top)

        for ref in (dqs, a0, a1, r0, r1):
            ref[...] = jnp.zeros_like(ref)
        top = block(qi, strict)
        lax.while_loop(lambda c: (c[0] >= 0) & (c[1] > -SB_EXIT), lambda c: (c[0] - 1, block(c[0], None)), (qi - 1, top))
        dq_ref[...] = dqs[...]
        fin = pl.multiple_of(qi * QBLK, QBLK)
        dk_ref[...] = dks[pl.ds(fin, QBLK), :]
        dv_ref[...] = dvs[pl.ds(fin, QBLK), :]

    blk = lambda c0: pl.BlockSpec((QBLK, LANES), lambda p, i, c0=c0: (nq - 1 - i, c0 + p))
    return _pcall(
        body, name=name, grid=(2, nq),
        in_specs=[blk(0), pl.BlockSpec((s, LANES), lambda p, i: (0, 2 + p)), pl.BlockSpec((s, LANES), lambda p, i: (0, 4 + p)),
                  blk(0), blk(0)],
        out_specs=[blk(0), blk(0), blk(0)],
        out_shape=[jax.ShapeDtypeStruct((s, SB_W), F32)] * 3,
        scratch_shapes=[pltpu.VMEM((QBLK, LANES), F32)] * 5 + [pltpu.VMEM((s, LANES), F32), pltpu.VMEM((s, LANES), F32)],
        compiler_params=_cparams("arbitrary", "arbitrary"))(qkv, qkv, qkv, o, dmix)


def _seg_consts():
    r = _iota((LANES, LANES), 0)
    c = _iota((LANES, LANES), 1)
    return jnp.where((r >> 6) == (c >> 6), 1.0, 0.0).astype(BF16)


def _rot_half(x, lane):
    half = HEAD_DIM // 2
    return jnp.where((lane & (HEAD_DIM - 1)) < half, pltpu.roll(x, LANES - half, 1), pltpu.roll(x, half, 1))


def _rope_tables(s):
    half = HEAD_DIM // 2
    inv_freq = ROPE_THETA ** (-jnp.arange(half, dtype=F32) * 2.0 / HEAD_DIM)
    ang = jnp.arange(s, dtype=F32)[:, None] * inv_freq[None, :]
    cos, sin = jnp.cos(ang), jnp.sin(ang)
    return jnp.tile(jnp.concatenate([cos, cos], axis=1), (1, 2)), jnp.tile(jnp.concatenate([-sin, sin], axis=1), (1, 2))


def _dil_prep(proj, gq, gk, cos, sin, *, name):
    s = proj.shape[0]
    t = _rows(s, TE)
    c0 = 3 * SB_W // LANES

    def body(q_ref, k_ref, gq_ref, gk_ref, cos_ref, sin_ref, qo_ref, ko_ref):
        seg = _seg_consts()
        lane = _iota((t, LANES), 1)
        cs, sn = cos_ref[...], sin_ref[...]
        for x_ref, g_ref, o_ref, mul in ((q_ref, gq_ref, qo_ref, HEAD_DIM ** -0.5), (k_ref, gk_ref, ko_ref, 1.0)):
            for j in range(2):
                xv = x_ref[:, j * LANES:(j + 1) * LANES]
                ms = _xdot(xv * xv, seg, 3) * (1.0 / HEAD_DIM)
                xn = xv * lax.rsqrt(ms + EPS) * g_ref[...]
                o_ref[:, j * LANES:(j + 1) * LANES] = ((xn * cs + _rot_half(xn, lane) * sn) * mul).astype(BF16)

    return _pcall(
        body, name=name, grid=(s // t,),
        in_specs=[pl.BlockSpec((t, DIL_W), lambda i: (i, c0 // 2)), pl.BlockSpec((t, DIL_W), lambda i: (i, c0 // 2 + 1)),
                  _vec_spec(LANES), _vec_spec(LANES), _row_spec(t, LANES), _row_spec(t, LANES)],
        out_specs=[_row_spec(t, DIL_W), _row_spec(t, DIL_W)],
        out_shape=[jax.ShapeDtypeStruct((s, DIL_W), BF16)] * 2, compiler_params=_cparams("parallel"))(proj, proj, gq, gk, cos, sin)


def _dil_prep_bwd(proj, gq, gk, cos, sin, dqs, dks, dvs, *, name):
    s = proj.shape[0]
    t = _rows(s, TE)
    c0 = 3 * SB_W // LANES

    def body(q_ref, k_ref, gq_ref, gk_ref, cos_ref, sin_ref, a0, a1, a2, b0, b1, b2, c0_ref, c1_ref, c2_ref,
             dq_ref, dk_ref, dv_ref, dgq_ref, dgk_ref):
        @pl.when(pl.program_id(0) == 0)
        def _():
            dgq_ref[...] = jnp.zeros_like(dgq_ref)
            dgk_ref[...] = jnp.zeros_like(dgk_ref)

        dv_ref[...] = c0_ref[...] + c1_ref[...] + c2_ref[...]
        seg = _seg_consts()
        lane = _iota((t, LANES), 1)
        cs, sn = cos_ref[...], sin_ref[...]
        for x_ref, g_ref, parts, o_ref, dg_ref, mul in ((q_ref, gq_ref, (a0, a1, a2), dq_ref, dgq_ref, HEAD_DIM ** -0.5),
                                                          (k_ref, gk_ref, (b0, b1, b2), dk_ref, dgk_ref, 1.0)):
            for j in range(2):
                sl = slice(j * LANES, (j + 1) * LANES)
                dout = (parts[0][:, sl] + parts[1][:, sl] + parts[2][:, sl]) * mul
                dxn = dout * cs + _rot_half(dout * sn, lane)
                xv = x_ref[:, sl]
                r = lax.rsqrt(_xdot(xv * xv, seg, 3) * (1.0 / HEAD_DIM) + EPS)
                xh = xv * r
                dg_ref[...] += jnp.sum(dxn * xh, axis=0, keepdims=True)
                dxh = dxn * g_ref[...]
                o_ref[:, sl] = r * (dxh - xh * (_xdot(dxh * xh, seg, 3) * (1.0 / HEAD_DIM)))

    rs = _row_spec(t, DIL_W)
    return _pcall(
        body, name=name, grid=(s // t,),
        in_specs=[pl.BlockSpec((t, DIL_W), lambda i: (i, c0 // 2)), pl.BlockSpec((t, DIL_W), lambda i: (i, c0 // 2 + 1)),
                  _vec_spec(LANES), _vec_spec(LANES), _row_spec(t, LANES), _row_spec(t, LANES)] + [rs] * 9,
        out_specs=[rs, rs, rs, _vec_spec(LANES), _vec_spec(LANES)],
        out_shape=[jax.ShapeDtypeStruct((s, DIL_W), F32)] * 3 + [jax.ShapeDtypeStruct((1, LANES), F32)] * 2,
        compiler_params=_cparams("arbitrary"))(proj, proj, gq, gk, cos, sin, *dqs, *dks, *dvs)


def _dil_masks(n):
    row = _iota((QBLK, LANES), 0)
    col = _iota((QBLK, LANES), 1)
    return col <= row, (col >= row) & (n > 0)


def _dil_fwd(q, k, v, r, *, name):
    s = q.shape[0]
    rows = s // r
    tb = _tile(rows, DIL_TILE, QBLK)
    nsub = tb // QBLK
    shape = (rows, r * DIL_W)
    q, k, v = (a.reshape(shape) for a in (q, k, v))

    def body(q_ref, kc_ref, kp_ref, vc_ref, vp_ref, num_ref, den_ref, mx_ref):
        n = pl.program_id(1)
        lane = _iota((QBLK, LANES), 1)
        h0 = lane < HEAD_DIM
        for j in range(nsub):
            rs = slice(j * QBLK, (j + 1) * QBLK)
            vc_m, vp_m = _dil_masks(n if j == 0 else 1)
            for p in range(2):
                ls = slice(p * LANES, (p + 1) * LANES)
                qv = q_ref[rs, ls]
                kc, vc = kc_ref[rs, ls], vc_ref[rs, ls]
                if j == 0:
                    kp, vp = kp_ref[:, ls], vp_ref[:, ls]
                else:
                    kp, vp = kc_ref[(j - 1) * QBLK:j * QBLK, ls], vc_ref[(j - 1) * QBLK:j * QBLK, ls]
                res = []
                for h in range(2):
                    qm = jnp.where(h0 if h == 0 else ~h0, qv, jnp.zeros_like(qv))
                    sc = jnp.where(vc_m, _dot(qm, kc, NT), NEG_BIG)
                    sp = jnp.where(vp_m, _dot(qm, kp, NT), NEG_BIG)
                    mx = jnp.maximum(jnp.max(sc, axis=1, keepdims=True), jnp.max(sp, axis=1, keepdims=True))
                    pc = jnp.exp(sc - mx)
                    pp = jnp.exp(sp - mx)
                    den = jnp.sum(pc, axis=1, keepdims=True) + jnp.sum(pp, axis=1, keepdims=True)
                    res.append((_dot(pc.astype(BF16), vc) + _dot(pp.astype(BF16), vp), den, mx))
                num_ref[rs, ls] = jnp.where(h0, res[0][0], res[1][0])
                den_ref[rs, ls] = jnp.where(h0, res[0][1], res[1][1])
                mx_ref[rs, ls] = jnp.where(h0, res[0][2], res[1][2])

    cur = pl.BlockSpec((tb, DIL_W), lambda rho, n: (n, rho))
    prev = pl.BlockSpec((QBLK, DIL_W), lambda rho, n: (jnp.maximum(n * nsub - 1, 0), rho))
    return _pcall(
        body, name=name, grid=(r, rows // tb), in_specs=[cur, cur, prev, cur, prev], out_specs=[cur, cur, cur],
        out_shape=[jax.ShapeDtypeStruct(shape, F32)] * 3,
        compiler_params=_cparams("parallel", "arbitrary"))(q, k, k, v, v)


def _dil_bwd(q, k, v, do, mall, zall, delta, r, *, name):
    s = q.shape[0]
    rows = s // r
    tb = _tile(rows, DIL_TILE, QBLK)
    nsub, nbig = tb // QBLK, rows // tb
    shape = (rows, r * DIL_W)
    q, k, v, do, mall, zall, delta = (a.reshape(shape) for a in (q, k, v, do, mall, zall, delta))

    def body(q_ref, kc_ref, kp_ref, vc_ref, vp_ref, do_ref, m_ref, z_ref, dl_ref, dq_ref, dk_ref, dv_ref, pk, pv):
        n = pl.program_id(1)
        lane = _iota((QBLK, LANES), 1)
        h0 = lane < HEAD_DIM

        @pl.when(n == 0)
        def _():
            pk[...] = jnp.zeros_like(pk)
            pv[...] = jnp.zeros_like(pv)

        @pl.when(n < nbig)
        def _():
            dk_ref[...] = pk[...]
            dv_ref[...] = pv[...]
            for j in range(nsub):
                rs = slice(j * QBLK, (j + 1) * QBLK)
                ps = slice((j - 1) * QBLK, j * QBLK)
                vc_m, vp_m = _dil_masks(n if j == 0 else 1)
                for p in range(2):
                    ls = slice(p * LANES, (p + 1) * LANES)
                    qv = q_ref[rs, ls]
                    dob = do_ref[rs, ls].astype(BF16)
                    zero = jnp.zeros_like(qv)
                    kc, vc = kc_ref[rs, ls], vc_ref[rs, ls]
                    kp, vp = (kp_ref[:, ls], vp_ref[:, ls]) if j == 0 else (kc_ref[ps, ls], vc_ref[ps, ls])
                    dq = None
                    acc = [None] * 4
                    for h in range(2):
                        hm = h0 if h == 0 else ~h0
                        qm = jnp.where(hm, qv, zero)
                        dom = jnp.where(hm, dob, zero)
                        c = p * LANES + h * HEAD_DIM
                        mrow, dlrow = m_ref[rs, c:c + 1], dl_ref[rs, c:c + 1]
                        rz = 1.0 / z_ref[rs, c:c + 1]
                        for kb, vb, valid, o in ((kc, vc, vc_m, 0), (kp, vp, vp_m, 2)):
                            sc = jnp.where(valid, _dot(qm, kb, NT), NEG_BIG)
                            pr = jnp.exp(sc - mrow) * rz
                            ds = (pr * (_dot(dom, vb, NT) - dlrow)).astype(BF16)
                            dqx = _dot(ds, jnp.where(hm, kb, zero))
                            dq = dqx if dq is None else dq + dqx
                            for i, x in ((o, _dot(ds, qm, TN)), (o + 1, _dot(pr.astype(BF16), dom, TN))):
                                acc[i] = x if acc[i] is None else acc[i] + x
                    dq_ref[rs, ls] = dq
                    if j == 0:
                        pk[rs, ls] = acc[0]
                        pv[rs, ls] = acc[1]
                        dk_ref[tb - QBLK:, ls] += acc[2]
                        dv_ref[tb - QBLK:, ls] += acc[3]
                    else:
                        pk[rs, ls] = acc[0]
                        pv[rs, ls] = acc[1]
                        pk[ps, ls] += acc[2]
                        pv[ps, ls] += acc[3]

        @pl.when(n == nbig)
        def _():
            dk_ref[...] = pk[...]
            dv_ref[...] = pv[...]

    last = nbig - 1
    cur = pl.BlockSpec((tb, DIL_W), lambda rho, n: (jnp.minimum(n, last), rho))
    prev = pl.BlockSpec((QBLK, DIL_W), lambda rho, n: (jnp.maximum(jnp.minimum(n, last) * nsub - 1, 0), rho))
    late = pl.BlockSpec((tb, DIL_W), lambda rho, n: (jnp.maximum(n - 1, 0), rho))
    return _pcall(
        body, name=name, grid=(r, nbig + 1), in_specs=[cur, cur, prev, cur, prev, cur, cur, cur, cur],
        out_specs=[cur, late, late], out_shape=[jax.ShapeDtypeStruct(shape, F32)] * 3,
        scratch_shapes=[pltpu.VMEM((tb, DIL_W), F32)] * 2,
        compiler_params=_cparams("parallel", "arbitrary"))(q, k, k, v, v, do, mall, zall, delta)


HG_SHIFT = HG_BLK.bit_length() - 1
HG_Q0, HG_F0, HG_I0 = (3 * SB_W + 3 * DIL_W) // HG_D, (3 * SB_W + 3 * DIL_W + HG_W) // HG_D, (3 * SB_W + 3 * DIL_W + 2 * HG_W) // HG_D


def _hg_inputs(qh, z, v, la, lc, t):
    r = _iota((t, t), 0)
    c = _iota((t, t), 1)
    same = (r >> HG_SHIFT) == (c >> HG_SHIFT)
    tri = jnp.where(same & (c <= r), 1.0, 0.0).astype(BF16)
    blk = jnp.where(same, 1.0, 0.0).astype(BF16)
    lsg = jnp.minimum(z, 0.0) - jnp.log(1.0 + jnp.exp(-jnp.abs(z)))
    b = lc + lsg
    lf = jnp.maximum(la, b) + jnp.log(1.0 + jnp.exp(-jnp.abs(la - b)))
    f = jnp.exp(lf)
    sq = _sigmoid(qh)
    g = _xdot_left(tri, lf)
    gl = _xdot_left(blk, lf)
    return dict(lf=lf, b=b, f=f, k=1.0 - f, sq=sq, qs=qh * sq, g=g, eg=jnp.exp(g), egl=jnp.exp(gl - g), dec=jnp.exp(gl))


def _hgrn_fwd(proj, la, lc, *, name):
    s = proj.shape[0]
    t = _rows(s, HG_TILE)
    nt, nb = s // t, t // HG_BLK

    def body(q_ref, f_ref, i_ref, la_ref, lc_ref, o_ref, st_ref, state, osc):
        @pl.when(pl.program_id(1) == 0)
        def _():
            state[...] = jnp.zeros_like(state)

        v = i_ref[...]
        a = _hg_inputs(q_ref[...], f_ref[...], v, la_ref[...], lc_ref[...], t)
        qs, k, g = a["qs"], a["k"], a["g"]
        rb = _iota((t, LANES), 0) & (HG_BLK - 1)
        o = jnp.sum(qs * k, axis=1, keepdims=True) * v
        for d in range(1, HG_BLK):
            m = rb >= d
            e = jnp.exp(jnp.where(m, g - pltpu.roll(g, d, 0), 0.0))
            cd = jnp.sum(qs * pltpu.roll(k, d, 0) * e, axis=1, keepdims=True)
            o = o + jnp.where(m, cd, 0.0) * pltpu.roll(v, d, 0)
        osc[...] = o
        qt = (qs * a["eg"]).astype(BF16)
        kt = (k * a["egl"]).astype(BF16)
        vb = v.astype(BF16)
        for blk in range(nb):
            sl = slice(blk * HG_BLK, (blk + 1) * HG_BLK)
            st = state[...]
            stb = st.astype(BF16)
            st_ref[blk * HG_D:(blk + 1) * HG_D, :] = stb
            osc[sl, :] += _dot(qt[sl], stb, NT)
            state[...] = a["dec"][blk * HG_BLK:blk * HG_BLK + 1] * st + _dot(vb[sl], kt[sl], TN)
        o_ref[...] = osc[...]

    col = lambda c0: pl.BlockSpec((t, HG_D), lambda hd, i, c0=c0: (i, c0 + hd))
    vec = pl.BlockSpec((1, HG_D), lambda hd, i: (0, hd))
    return _pcall(
        body, name=name, grid=(4, nt), in_specs=[col(HG_Q0), col(HG_F0), col(HG_I0), vec, vec],
        out_specs=[col(0), pl.BlockSpec((None, nb * HG_D, HG_D), lambda hd, i: (hd, i, 0))],
        out_shape=[jax.ShapeDtypeStruct((s, HG_W), F32), jax.ShapeDtypeStruct((4, s // HG_BLK * HG_D, HG_D), BF16)],
        scratch_shapes=[pltpu.VMEM((HG_D, HG_D), F32), pltpu.VMEM((t, HG_D), F32)],
        compiler_params=_cparams("arbitrary", "arbitrary"))(proj, proj, proj, la, lc)


def _hgrn_bwd(proj, la, lc, st, doh, *, name):
    s = proj.shape[0]
    t = _rows(s, HG_TILE)
    nt, nb = s // t, t // HG_BLK

    def body(q_ref, f_ref, i_ref, la_ref, lc_ref, st_ref, do_ref, dq_ref, df_ref, di_ref, dla_ref, dlc_ref, dstate):
        @pl.when(pl.program_id(1) == 0)
        def _():
            dstate[...] = jnp.zeros_like(dstate)
            dla_ref[...] = jnp.zeros_like(dla_ref)
            dlc_ref[...] = jnp.zeros_like(dlc_ref)

        qh, z, v, do = q_ref[...], f_ref[...], i_ref[...], do_ref[...]
        la = la_ref[...]
        a = _hg_inputs(qh, z, v, la, lc_ref[...], t)
        qs, k, g = a["qs"], a["k"], a["g"]
        rb = _iota((t, LANES), 0) & (HG_BLK - 1)
        dc0 = jnp.sum(do * v, axis=1, keepdims=True)
        dq = dc0 * k
        dk = dc0 * qs
        dv = jnp.sum(qs * k, axis=1, keepdims=True) * do
        for d in range(1, HG_BLK):
            m = rb >= d
            e = jnp.exp(jnp.where(m, g - pltpu.roll(g, d, 0), 0.0))
            ks = pltpu.roll(k, d, 0)
            qe = qs * e
            cd = jnp.where(m, jnp.sum(qe * ks, axis=1, keepdims=True), 0.0)
            dcd = jnp.where(m, jnp.sum(do * pltpu.roll(v, d, 0), axis=1, keepdims=True), 0.0)
            dq = dq + dcd * ks * e
            dk = dk + pltpu.roll(dcd * qe, t - d, 0)
            dv = dv + pltpu.roll(cd * do, t - d, 0)
        qt = (qs * a["eg"]).astype(BF16)
        kt = (k * a["egl"]).astype(BF16)
        vb = v.astype(BF16)
        dob = do.astype(BF16)
        dqt, dkt, dvi, dgs = [None] * nb, [None] * nb, [None] * nb, [None] * nb
        for blk in reversed(range(nb)):
            sl = slice(blk * HG_BLK, (blk + 1) * HG_BLK)
            stb = st_ref[blk * HG_D:(blk + 1) * HG_D, :]
            ds1 = dstate[...]
            ds1b = ds1.astype(BF16)
            dec = a["dec"][blk * HG_BLK:blk * HG_BLK + 1]
            dqt[blk] = _dot(dob[sl], stb)
            dkt[blk] = _dot(vb[sl], ds1b)
            dvi[blk] = _dot(kt[sl], ds1b, NT)
            dgs[blk] = jnp.broadcast_to(jnp.sum(ds1 * stb.astype(F32), axis=0, keepdims=True) * dec, (HG_BLK, HG_D))
            dstate[...] = dec * ds1 + _dot(dob[sl], qt[sl], TN)
        dki = jnp.concatenate(dkt, axis=0) * a["egl"]
        dq = dq + jnp.concatenate(dqt, axis=0) * a["eg"]
        dk = dk + dki
        dv = dv + jnp.concatenate(dvi, axis=0)
        r = _iota((t, t), 0)
        c = _iota((t, t), 1)
        same = (r >> HG_SHIFT) == (c >> HG_SHIFT)
        later = jnp.where(same & (c >= r), 1.0, 0.0).astype(BF16)
        whole = jnp.where(same, 1.0, 0.0).astype(BF16)
        dlf = (_xdot_left(later, qs * dq - k * dk) + _xdot_left(whole, k * dki) + jnp.concatenate(dgs, axis=0)
               - a["f"] * dk)
        wb = jnp.exp(a["b"] - a["lf"])
        wa = jnp.exp(la - a["lf"])
        sq = a["sq"]
        dq_ref[...] = dq * (sq * (1.0 + qh * (1.0 - sq)))
        df_ref[...] = dlf * wb * (1.0 - _sigmoid(z))
        di_ref[...] = dv
        dla_ref[...] += jnp.sum(dlf * wa, axis=0, keepdims=True)
        dlc_ref[...] += jnp.sum(dlf * wb, axis=0, keepdims=True)

    col = lambda c0: pl.BlockSpec((t, HG_D), lambda hd, i, c0=c0: (nt - 1 - i, c0 + hd))
    vec = pl.BlockSpec((1, HG_D), lambda hd, i: (0, hd))
    return _pcall(
        body, name=name, grid=(4, nt),
        in_specs=[col(HG_Q0), col(HG_F0), col(HG_I0), vec, vec,
                  pl.BlockSpec((None, nb * HG_D, HG_D), lambda hd, i: (hd, nt - 1 - i, 0)), col(0)],
        out_specs=[col(0), col(0), col(0), vec, vec],
        out_shape=[jax.ShapeDtypeStruct((s, HG_W), F32)] * 3 + [jax.ShapeDtypeStruct((1, HG_W), F32)] * 2,
        scratch_shapes=[pltpu.VMEM((HG_D, HG_D), F32)],
        compiler_params=_cparams("arbitrary", "arbitrary"))(proj, proj, proj, la, lc, st, doh)


GH0 = (IN_W - HG_W) // HG_W


def _mix_out(o_a, nums, dens, mxs, oh, proj, hg, *, name):
    s = o_a.shape[0]
    t = _rows(s, TE)

    def body(oa_ref, n0, n1, n2, d0, d1, d2, m0, m1, m2, oh_ref, gh_ref, hg_ref, y_ref, od_ref, mall_ref, z_ref):
        y_ref[:, :SB_W] = oa_ref[...].astype(BF16)
        m = jnp.maximum(jnp.maximum(m0[...], m1[...]), m2[...])
        num = jnp.zeros((t, DIL_W), F32)
        z = jnp.zeros((t, DIL_W), F32)
        for n_ref, d_ref, m_ref in ((n0, d0, m0), (n1, d1, m1), (n2, d2, m2)):
            sc = jnp.exp(m_ref[...] - m)
            num = num + n_ref[...] * sc
            z = z + d_ref[...] * sc
        od = num / z
        od_ref[...] = od
        mall_ref[...] = m
        z_ref[...] = z
        y_ref[:, SB_W:SB_W + DIL_W] = od.astype(BF16)
        for h in range(4):
            sl = slice(h * HG_D, (h + 1) * HG_D)
            ov = oh_ref[:, sl]
            g = gh_ref[:, sl]
            r = lax.rsqrt(jnp.mean(ov * ov, axis=1, keepdims=True) + EPS)
            y_ref[:, SB_W + DIL_W + h * HG_D:SB_W + DIL_W + (h + 1) * HG_D] = (ov * r * hg_ref[...] * (g * _sigmoid(g))).astype(BF16)

    rd = _row_spec(t, DIL_W)
    return _pcall(
        body, name=name, grid=(s // t,),
        in_specs=[rd] * 10 + [_row_spec(t, HG_W), _row_spec(t, HG_W, GH0), _vec_spec(HG_D)],
        out_specs=[_row_spec(t, MIX_W), rd, rd, rd],
        out_shape=[jax.ShapeDtypeStruct((s, MIX_W), BF16)] + [jax.ShapeDtypeStruct((s, DIL_W), F32)] * 3,
        compiler_params=_cparams("parallel"))(o_a, *nums, *dens, *mxs, oh, proj, hg)


def _mix_out_bwd(dmix, oh, proj, hg, od, *, name):
    s = oh.shape[0]
    t = _rows(s, TE)

    def body(dm_ref, oh_ref, gh_ref, hg_ref, od_ref, doh_ref, dgh_ref, dl_ref, dhg_ref):
        @pl.when(pl.program_id(0) == 0)
        def _():
            dhg_ref[...] = jnp.zeros_like(dhg_ref)

        seg = _seg_consts()
        for j in range(2):
            sl = slice(j * LANES, (j + 1) * LANES)
            dl_ref[:, sl] = _xdot(dm_ref[:, SB_W + j * LANES:SB_W + (j + 1) * LANES] * od_ref[:, sl], seg, 3)
        hgv = hg_ref[...]
        for h in range(4):
            sl = slice(h * HG_D, (h + 1) * HG_D)
            dy = dm_ref[:, SB_W + DIL_W + h * HG_D:SB_W + DIL_W + (h + 1) * HG_D]
            ov = oh_ref[:, sl]
            g = gh_ref[:, sl]
            sg = _sigmoid(g)
            silu = g * sg
            r = lax.rsqrt(jnp.mean(ov * ov, axis=1, keepdims=True) + EPS)
            nrm = ov * r
            dhg_ref[...] += jnp.sum(dy * nrm * silu, axis=0, keepdims=True)
            dgh_ref[:, sl] = dy * nrm * hgv * (sg * (1.0 + g * (1.0 - sg)))
            dn = dy * hgv * silu
            doh_ref[:, sl] = r * (dn - nrm * jnp.mean(dn * nrm, axis=1, keepdims=True))

    rh = _row_spec(t, HG_W)
    return _pcall(
        body, name=name, grid=(s // t,),
        in_specs=[_row_spec(t, MIX_W), rh, _row_spec(t, HG_W, GH0), _vec_spec(HG_D), _row_spec(t, DIL_W)],
        out_specs=[rh, rh, _row_spec(t, DIL_W), _vec_spec(HG_D)],
        out_shape=[jax.ShapeDtypeStruct((s, HG_W), F32)] * 2 + [jax.ShapeDtypeStruct((s, DIL_W), F32), jax.ShapeDtypeStruct((1, HG_D), F32)],
        compiler_params=_cparams("arbitrary"))(dmix, oh, proj, hg, od)


def _lb_terms(l):
    l0, l1 = l[0:1], l[1:2]
    m = jnp.maximum(l0, l1)
    e0, e1 = jnp.exp(l0 - m), jnp.exp(l1 - m)
    s0, s1 = e0 / (e0 + e1), e1 / (e0 + e1)
    args = (s0 - s0, (s0 + s1) - s0)
    lbs = tuple(jnp.minimum(jnp.maximum(a, 0.0), 1.0 - EPS) for a in args)
    return s0, s1, args, lbs


def _lb_prep(logits, *, name):
    def body(l_ref, lb_ref, la_ref, lc_ref):
        _, _, _, lbs = _lb_terms(l_ref[...])
        lb = jnp.concatenate(lbs, axis=0)
        lb_ref[...] = lb
        la_ref[...] = jnp.log(jnp.maximum(lb, LB_FLOOR))
        lc_ref[...] = jnp.log1p(-lb)

    return _pcall(body, name=name, out_shape=[jax.ShapeDtypeStruct(logits.shape, F32)] * 3)(logits)


def _lb_bwd(logits, dla, dlc, *, name):
    def half(hi, eq):
        return jnp.where(hi, 1.0, jnp.where(eq, 0.5, 0.0))

    def body(l_ref, dla_ref, dlc_ref, o_ref):
        s0, s1, args, lbs = _lb_terms(l_ref[...])
        da = []
        for i in range(2):
            a, lb = args[i], lbs[i]
            dlb = dla_ref[i:i + 1] * half(lb > LB_FLOOR, lb == LB_FLOOR) / jnp.maximum(lb, LB_FLOOR) - dlc_ref[i:i + 1] / (1.0 - lb)
            t = jnp.maximum(a, 0.0)
            da.append(dlb * half(a > 0.0, a == 0.0) * half(t < 1.0 - EPS, t == 1.0 - EPS))
        ds0 = (da[0] + da[1]) - (da[0] + da[1])
        ds1 = da[1]
        dot = s0 * ds0 + s1 * ds1
        o_ref[...] = jnp.concatenate([s0 * (ds0 - dot), s1 * (ds1 - dot)], axis=0)

    return _pcall(body, name=name, out_shape=jax.ShapeDtypeStruct(logits.shape, F32))(logits, dla, dlc)


def _mod_fwd(c8, w, b, *, name):
    _, d, n = w.shape
    tn = _tile(n, 768)

    def body(c_ref, w_ref, b_ref, o_ref):
        cv = c_ref[...]
        o_ref[...] = _dot((cv * _sigmoid(cv)).astype(BF16), w_ref[...].astype(BF16)) + b_ref[...]

    return _pcall(
        body, name=name, grid=(2, n // tn),
        in_specs=[pl.BlockSpec((8, d), lambda l, j: (0, 0)), pl.BlockSpec((None, d, tn), lambda l, j: (l, 0, j)),
                  pl.BlockSpec((None, 1, tn), lambda l, j: (l, 0, j))],
        out_specs=pl.BlockSpec((None, 8, tn), lambda l, j: (l, 0, j)),
        out_shape=jax.ShapeDtypeStruct((2, 8, n), F32), compiler_params=_cparams("parallel", "parallel"))(c8, w, b)


def _mod_bwd(ct, dm, *, name):
    d = ct.shape[0]
    n = dm.shape[2]
    tn = _tile(n, 768)

    def body(c_ref, dm_ref, o_ref):
        cv = c_ref[...]
        sc = cv * _sigmoid(cv)
        dv = dm_ref[...]
        acc = sc[:, 0:1] * dv[0:1, :]
        for b in range(1, 8):
            acc = acc + sc[:, b:b + 1] * dv[b:b + 1, :]
        o_ref[...] = acc

    return _pcall(
        body, name=name, grid=(2, n // tn),
        in_specs=[pl.BlockSpec((d, 8), lambda l, j: (0, 0)), pl.BlockSpec((None, 8, tn), lambda l, j: (l, 0, j))],
        out_specs=pl.BlockSpec((None, d, tn), lambda l, j: (l, 0, j)),
        out_shape=jax.ShapeDtypeStruct((2, d, n), F32), compiler_params=_cparams("parallel", "parallel"))(ct, dm)


_PEERS = {
    "chips": ((1, 0, 0), (0, 1, 0), (1, 1, 0)),
    "all": tuple((a, b, c) for a in (0, 1) for b in (0, 1) for c in (0, 1) if a + b + c),
    "sib": ((0, 0, 1),),
}
_SLOTS = {"chips": 4, "all": 8, "sib": 2}


def _slot(kind, x, y, c):
    return {"chips": 2 * x + y, "all": 4 * x + 2 * y + c, "sib": c}[kind]


def _exchange(arrs, kind, scatter, *, name):
    n = len(arrs)
    peers = _PEERS[kind]
    ns = _SLOTS[kind]
    np_ = len(peers)

    def pieces(a):
        shape = a.shape[1:] if scatter else a.shape
        if len(shape) == 2:
            for k in (XCHG_CHUNKS, XCHG_CHUNKS // 2, XCHG_CHUNKS // 4):
                if k > 1 and shape[0] % (16 * k) == 0 and shape[0] * shape[1] * a.dtype.itemsize >= k * XCHG_MIN_BYTES:
                    return [(i * (shape[0] // k), shape[0] // k) for i in range(k)]
        return [None]

    chunks = [pieces(a) for a in arrs]
    base = [sum(len(c) for c in chunks[:a]) * np_ for a in range(n)]
    total = sum(len(c) for c in chunks) * np_

    def body(*refs):
        ins, outs = refs[:n], refs[n:2 * n]
        send, recv, loc = refs[2 * n:]
        x, y, c = lax.axis_index("x"), lax.axis_index("y"), lax.axis_index("c")
        me = _slot(kind, x, y, c)
        copies = []
        for a in range(n):
            own = pltpu.make_async_copy(ins[a].at[me] if scatter else ins[a], outs[a].at[me], loc.at[a])
            own.start()
            copies.append(own)
            for j, (dx, dy, dc) in enumerate(peers):
                px, py, pc = (1 - x if dx else x), (1 - y if dy else y), (1 - c if dc else c)
                src = ins[a].at[_slot(kind, px, py, pc)] if scatter else ins[a]
                for i, piece in enumerate(chunks[a]):
                    rows = slice(None) if piece is None else pl.ds(piece[0], piece[1])
                    sem = base[a] + j * len(chunks[a]) + i
                    cp = pltpu.make_async_remote_copy(
                        src_ref=src if piece is None else src.at[rows], dst_ref=outs[a].at[me] if piece is None else outs[a].at[me, rows],
                        send_sem=send.at[sem], recv_sem=recv.at[sem], device_id=(px, py, pc), device_id_type=MESH_ID)
                    cp.start()
                    copies.append(cp)
        for cp in copies:
            cp.wait()

    hbm = pl.BlockSpec(memory_space=pl.ANY)
    shapes = [jax.ShapeDtypeStruct(a.shape if scatter else (ns,) + a.shape, a.dtype) for a in arrs]
    return _pcall(
        body, name=name, in_specs=[hbm] * n, out_specs=[hbm] * n, out_shape=shapes,
        scratch_shapes=[pltpu.SemaphoreType.DMA((total,)), pltpu.SemaphoreType.DMA((total,)), pltpu.SemaphoreType.DMA((n,))],
    )(*arrs)


def _sum_slots(a, *, name, out_dtype=F32):
    ns, r, c = a.shape
    t = _tile(r, max(16, (1 << 18) // c // 16 * 16), 16)

    def body(a_ref, o_ref):
        acc = a_ref[0].astype(F32)
        for i in range(1, ns):
            acc = acc + a_ref[i].astype(F32)
        o_ref[...] = acc.astype(o_ref.dtype)

    return _pcall(body, name=name, grid=(r // t,), in_specs=[pl.BlockSpec((ns, t, c), lambda i: (0, i, 0))],
                  out_specs=pl.BlockSpec((t, c), lambda i: (i, 0)), out_shape=jax.ShapeDtypeStruct((r, c), out_dtype),
                  compiler_params=_cparams("parallel"))(a)


def _adamw(w, gparts, m, v, *, name):
    r, c = w.shape
    t = _tile(r, max(16, (1 << 17) // c // 16 * 16), 16)
    ng = len(gparts)

    def body(*refs):
        w_ref, m_ref, v_ref = refs[0], refs[1 + ng], refs[2 + ng]
        g_ref, d_ref, nm_ref, nv_ref = refs[3 + ng:]
        g = refs[1][...].astype(F32)
        for i in range(1, ng):
            g = g + refs[1 + i][...].astype(F32)
        mn = ADAM_B1 * m_ref[...] + (1.0 - ADAM_B1) * g
        vn = ADAM_B2 * v_ref[...] + (1.0 - ADAM_B2) * (g * g)
        m_hat = mn / (1.0 - ADAM_B1 ** ADAM_STEP)
        v_hat = vn / (1.0 - ADAM_B2 ** ADAM_STEP)
        g_ref[...] = g
        d_ref[...] = -ADAM_LR * (m_hat / (jnp.sqrt(v_hat) + ADAM_EPS) + ADAM_WD * w_ref[...])
        nm_ref[...] = mn
        nv_ref[...] = vn

    spec = pl.BlockSpec((t, c), lambda i: (i, 0))
    return _pcall(body, name=name, grid=(r // t,), in_specs=[spec] * (3 + ng), out_specs=[spec] * 4,
                  out_shape=[jax.ShapeDtypeStruct((r, c), F32)] * 4, compiler_params=_cparams("parallel"))(w, *gparts, m, v)


FFN_TM = 512
FFN_CHUNK = 1408


def _resident(shape):
    return pl.BlockSpec(shape, lambda i: (0,) * len(shape), pipeline_mode=pl.Buffered(1))


def _ffn_up(x, sc, sh, wgu, *, name):
    s, d = x.shape
    f = wgu.shape[1] // 2
    t, fc = _rows(s, FFN_TM), _tile(f, FFN_CHUNK)

    def body(x_ref, sc_ref, sh_ref, w_ref, h_ref, uv_ref, a_ref):
        xv = x_ref[...]
        r = lax.rsqrt(jnp.mean(xv * xv, axis=1, keepdims=True) + EPS)
        hb = ((xv * r) * (1.0 + sc_ref[...]) + sh_ref[...]).astype(BF16)
        h_ref[...] = hb
        for j in range(f // fc):
            u = _dot(hb, w_ref[:, j * fc:(j + 1) * fc])
            v = _dot(hb, w_ref[:, f + j * fc:f + (j + 1) * fc])
            uv_ref[:, j * fc:(j + 1) * fc] = u.astype(BF16)
            uv_ref[:, f + j * fc:f + (j + 1) * fc] = v.astype(BF16)
            a_ref[:, j * fc:(j + 1) * fc] = (u * _sigmoid(u) * v).astype(BF16)

    return _pcall(
        body, name=name, grid=(s // t,), in_specs=[_row_spec(t, d), _vec_spec(d), _vec_spec(d), _resident(wgu.shape)],
        out_specs=[_row_spec(t, d), _row_spec(t, 2 * f), _row_spec(t, f)],
        out_shape=[jax.ShapeDtypeStruct((s, d), BF16), jax.ShapeDtypeStruct((s, 2 * f), BF16), jax.ShapeDtypeStruct((s, f), BF16)],
        compiler_params=_cparams("parallel"))(x, sc, sh, wgu)


def _norm_mm(x, sc, sh, w, *, name):
    s, d = x.shape
    n = w.shape[1]
    t, nc = _rows(s, FFN_TM), _tile(n, 1792)

    def body(x_ref, sc_ref, sh_ref, w_ref, h_ref, o_ref):
        xv = x_ref[...]
        r = lax.rsqrt(jnp.mean(xv * xv, axis=1, keepdims=True) + EPS)
        hb = ((xv * r) * (1.0 + sc_ref[...]) + sh_ref[...]).astype(BF16)
        h_ref[...] = hb
        for j in range(n // nc):
            o_ref[:, j * nc:(j + 1) * nc] = _dot(hb, w_ref[:, j * nc:(j + 1) * nc])

    return _pcall(
        body, name=name, grid=(s // t,), in_specs=[_row_spec(t, d), _vec_spec(d), _vec_spec(d), _resident(w.shape)],
        out_specs=[_row_spec(t, d), _row_spec(t, n)],
        out_shape=[jax.ShapeDtypeStruct((s, d), BF16), jax.ShapeDtypeStruct((s, n), F32)],
        compiler_params=_cparams("parallel"))(x, sc, sh, w)


def _ffn_dact(dxo, y, sg, wd, uv, *, name):
    s, d = y.shape
    f = wd.shape[0]
    t, fc = _rows(s, FFN_TM), _tile(f, FFN_CHUNK)

    def body(dxo_ref, y_ref, sg_ref, w_ref, uv_ref, dy_ref, duv_ref, ds_ref):
        @pl.when(pl.program_id(0) == 0)
        def _():
            ds_ref[...] = jnp.zeros_like(ds_ref)

        dv = dxo_ref[...]
        dyb = (sg_ref[...] * dv).astype(BF16)
        dy_ref[...] = dyb
        ds_ref[...] += jnp.sum(dv * y_ref[...], axis=0, keepdims=True)
        for j in range(f // fc):
            da = _dot(dyb, w_ref[j * fc:(j + 1) * fc, :], NT)
            u = uv_ref[:, j * fc:(j + 1) * fc].astype(F32)
            v = uv_ref[:, f + j * fc:f + (j + 1) * fc].astype(F32)
            sgm = _sigmoid(u)
            duv_ref[:, j * fc:(j + 1) * fc] = (da * v * (sgm * (1.0 + u * (1.0 - sgm)))).astype(BF16)
            duv_ref[:, f + j * fc:f + (j + 1) * fc] = (da * (u * sgm)).astype(BF16)

    return _pcall(
        body, name=name, grid=(s // t,),
        in_specs=[_row_spec(t, d), _row_spec(t, d), _vec_spec(d), _resident(wd.shape), _row_spec(t, 2 * f)],
        out_specs=[_row_spec(t, d), _row_spec(t, 2 * f), _vec_spec(d)],
        out_shape=[jax.ShapeDtypeStruct((s, d), BF16), jax.ShapeDtypeStruct((s, 2 * f), BF16), jax.ShapeDtypeStruct((1, d), F32)],
        compiler_params=_cparams("arbitrary"))(dxo, y, sg, wd, uv)


def _ffn_dh(duv, wgu, x, sc, dxo, *, name):
    s, d = x.shape
    f2 = wgu.shape[1]
    t = _rows(s, FFN_TM)

    def body(duv_ref, w_ref, x_ref, sc_ref, dxo_ref, dx_ref, dsc_ref, dsh_ref):
        @pl.when(pl.program_id(0) == 0)
        def _():
            dsc_ref[...] = jnp.zeros_like(dsc_ref)
            dsh_ref[...] = jnp.zeros_like(dsh_ref)

        dhv = _dot(duv_ref[...], w_ref[...], NT)
        xv = x_ref[...]
        r = lax.rsqrt(jnp.mean(xv * xv, axis=1, keepdims=True) + EPS)
        xn = xv * r
        dxn = dhv * (1.0 + sc_ref[...])
        dx_ref[...] = dxo_ref[...] + r * (dxn - xn * jnp.mean(dxn * xn, axis=1, keepdims=True))
        dsc_ref[...] += jnp.sum(dhv * xn, axis=0, keepdims=True)
        dsh_ref[...] += jnp.sum(dhv, axis=0, keepdims=True)

    return _pcall(
        body, name=name, grid=(s // t,),
        in_specs=[_row_spec(t, f2), _resident(wgu.shape), _row_spec(t, d), _vec_spec(d), _row_spec(t, d)],
        out_specs=[_row_spec(t, d), _vec_spec(d), _vec_spec(d)],
        out_shape=[jax.ShapeDtypeStruct((s, d), F32), jax.ShapeDtypeStruct((1, d), F32), jax.ShapeDtypeStruct((1, d), F32)],
        compiler_params=_cparams("arbitrary"))(duv, wgu, x, sc, dxo)


def _ffn_fwd(x, sh, sc, g, wgu, wd, tag):
    h, uv, a = _ffn_up(x, sc, sh, wgu, name=f"{tag}_up")
    y, xo = _mm(a, wd, name=f"{tag}_down", tm=512, tn=1024, tk=wd.shape[0], res=x, scale=0.5 * g)
    return xo, (x, h, uv, a, y)


def _ffn_bwd(dxo, saved, sc, g, wgu, wd, tag):
    x, h, uv, a, y = saved
    dyb, duv, dgs = _ffn_dact(dxo, y, 0.5 * g, wd, uv, name=f"{tag}_dact")
    dx, dsc, dsh = _ffn_dh(duv, wgu, x, sc, dxo, name=f"{tag}_dh")
    dwgu = _mm_tn(h, duv, name=f"{tag}_dwgu", tm=1024, tn=1408, tk=512)
    dwd = _mm_tn(a, dyb, name=f"{tag}_dwd", tm=1408, tn=1024, tk=512)
    return dx, dwgu, dwd, dsh, dsc, 0.5 * dgs


def _layer_fwd(x0, mod, w, par, tag):
    s = x0.shape[0]
    sh1, sc1, g1, sh2, sc2, g2, sh3, sc3, g3 = (mod[i:i + 1] for i in range(N_MOD))
    x1, f1 = _ffn_fwd(x0, sh1, sc1, g1, w["gu1"], w["d1"], f"{tag}_ffn1")
    h2, proj = _norm_mm(x1, sc2, sh2, w["in"], name=f"{tag}_in")
    qkv = proj[:, :3 * SB_W].astype(BF16)
    vd = proj[:, 3 * SB_W + 2 * DIL_W:3 * SB_W + 3 * DIL_W].astype(BF16)
    o_a = _sb_fwd(qkv, name=f"{tag}_sb")
    qd, kd = _dil_prep(proj, par["gq"], par["gk"], par["cos"], par["sin"], name=f"{tag}_dil_prep")
    nums, dens, mxs = [], [], []
    for _, r in DIL_PATTERNS:
        nu, de, mx = _dil_fwd(qd, kd, vd, r, name=f"{tag}_dil{r}")
        nums.append(nu.reshape(s, DIL_W))
        dens.append(de.reshape(s, DIL_W))
        mxs.append(mx.reshape(s, DIL_W))
    oh, st = _hgrn_fwd(proj, par["la"], par["lc"], name=f"{tag}_hgrn")
    ymix, od, mall, zall = _mix_out(o_a, nums, dens, mxs, oh, proj, par["hg"], name=f"{tag}_mix_out")
    out, x2 = _mm(ymix, w["out"], name=f"{tag}_out", tm=512, tn=1024, tk=1024, res=x1, scale=g2)
    x3, f2 = _ffn_fwd(x2, sh3, sc3, g3, w["gu2"], w["d2"], f"{tag}_ffn2")
    return x3, dict(f1=f1, f2=f2, x1=x1, h2=h2, proj=proj, qkv=qkv, vd=vd, o_a=o_a, qd=qd, kd=kd, oh=oh, st=st,
                    ymix=ymix, od=od, mall=mall, zall=zall, out=out)


def _layer_bwd(dx3, sv, mod, w, par, tag):
    s = dx3.shape[0]
    sh1, sc1, g1, sh2, sc2, g2, sh3, sc3, g3 = (mod[i:i + 1] for i in range(N_MOD))
    dx2, dwgu2, dwd2, dsh3, dsc3, dg3 = _ffn_bwd(dx3, sv["f2"], sc3, g3, w["gu2"], w["d2"], f"{tag}_ffn2")
    doutb, dg2 = _gate_bwd(dx2, sv["out"], g2, name=f"{tag}_dgate2")
    dmix = _mm(doutb, w["out"], name=f"{tag}_dmix", tb=True, tm=512, tn=1024, tk=1024)
    dwout = _mm_tn(sv["ymix"], doutb, name=f"{tag}_dwout", tm=1024, tn=1024, tk=512)
    proj = sv["proj"]
    doh, dgh, delta, dhg = _mix_out_bwd(dmix, sv["oh"], proj, par["hg"], sv["od"], name=f"{tag}_dmix_out")
    dqa, dka, dva = _sb_bwd(sv["qkv"], sv["o_a"], dmix, name=f"{tag}_dsb")
    do_d = dmix[:, SB_W:SB_W + DIL_W]
    dqs, dks, dvs = [], [], []
    for _, r in DIL_PATTERNS:
        a, b, c = _dil_bwd(sv["qd"], sv["kd"], sv["vd"], do_d, sv["mall"], sv["zall"], delta, r, name=f"{tag}_ddil{r}")
        dqs.append(a.reshape(s, DIL_W))
        dks.append(b.reshape(s, DIL_W))
        dvs.append(c.reshape(s, DIL_W))
    dqd, dkd, dvd, dgq, dgk = _dil_prep_bwd(proj, par["gq"], par["gk"], par["cos"], par["sin"], dqs, dks, dvs,
                                             name=f"{tag}_ddil_prep")
    dqh, dfh, dih, dla, dlc = _hgrn_bwd(proj, par["la"], par["lc"], sv["st"], doh, name=f"{tag}_dhgrn")
    dproj = jnp.concatenate([dqa, dka, dva, dqd, dkd, dvd, dqh, dfh, dih, dgh], axis=1).astype(BF16)
    dx1, dsc2, dsh2 = _ffn_dh(dproj, w["in"], sv["x1"], sc2, dx2, name=f"{tag}_dh2")
    dwin = _mm_tn(sv["h2"], dproj, name=f"{tag}_dwin", tm=1024, tn=1792, tk=512)
    dx0, dwgu1, dwd1, dsh1, dsc1, dg1 = _ffn_bwd(dx1, sv["f1"], sc1, g1, w["gu1"], w["d1"], f"{tag}_ffn1")
    dmod = jnp.concatenate([dsh1, dsc1, dg1, dsh2, dsc2, dg2, dsh3, dsc3, dg3], axis=0)
    fold = lambda v: v.reshape(2, HEAD_DIM).sum(axis=0)
    grads = dict(gu1=dwgu1, d1=dwd1, gu2=dwgu2, d2=dwd2, win=dwin, wout=dwout, dmod=dmod, gq=fold(dgq), gk=fold(dgk),
                 hg=dhg[0], la=dla[0], lc=dlc[0])
    return dx0, grads


def _pack(pieces):
    flat = jnp.concatenate([p.reshape(-1) for p in pieces])
    pad = (-flat.shape[0]) % (8 * LANES)
    return jnp.pad(flat, (0, pad)).reshape(-1, LANES)


def _unpack(flat, like):
    out, off = [], 0
    for p in like:
        out.append(flat[off:off + p.size].reshape(p.shape))
        off += p.size
    return out


def kernel(x, c, w_mod, b_mod, ffn1_w_gate, ffn1_w_up, ffn1_w_down, w_in, w_out, q_norm_g, k_norm_g, hgrn_norm_g, hgrn_lb_logits, ffn2_w_gate, ffn2_w_up, ffn2_w_down, loss_target, m_w_mod, m_b_mod, m_ffn1_w_gate, m_ffn1_w_up, m_ffn1_w_down, m_w_in, m_w_out, m_q_norm_g, m_k_norm_g, m_hgrn_norm_g, m_hgrn_lb_logits, m_ffn2_w_gate, m_ffn2_w_up, m_ffn2_w_down, v_w_mod, v_b_mod, v_ffn1_w_gate, v_ffn1_w_up, v_ffn1_w_down, v_w_in, v_w_out, v_q_norm_g, v_k_norm_g, v_hgrn_norm_g, v_hgrn_lb_logits, v_ffn2_w_gate, v_ffn2_w_up, v_ffn2_w_down):
    names = ["w_mod", "b_mod", "ffn1_w_gate", "ffn1_w_up", "ffn1_w_down", "w_in", "w_out", "q_norm_g", "k_norm_g",
             "hgrn_norm_g", "hgrn_lb_logits", "ffn2_w_gate", "ffn2_w_up", "ffn2_w_down"]
    wts = dict(zip(names, (w_mod, b_mod, ffn1_w_gate, ffn1_w_up, ffn1_w_down, w_in, w_out, q_norm_g, k_norm_g, hgrn_norm_g,
                           hgrn_lb_logits, ffn2_w_gate, ffn2_w_up, ffn2_w_down)))
    mom = dict(zip(names, (m_w_mod, m_b_mod, m_ffn1_w_gate, m_ffn1_w_up, m_ffn1_w_down, m_w_in, m_w_out, m_q_norm_g, m_k_norm_g,
                           m_hgrn_norm_g, m_hgrn_lb_logits, m_ffn2_w_gate, m_ffn2_w_up, m_ffn2_w_down)))
    var = dict(zip(names, (v_w_mod, v_b_mod, v_ffn1_w_gate, v_ffn1_w_up, v_ffn1_w_down, v_w_in, v_w_out, v_q_norm_g, v_k_norm_g,
                           v_hgrn_norm_g, v_hgrn_lb_logits, v_ffn2_w_gate, v_ffn2_w_up, v_ffn2_w_down)))
    depth = w_mod.shape[0]
    assert depth == 2 and x.shape[0] == 1
    s, d = x.shape[1:]
    assert s % (DIL_PATTERNS[-1][1] * QBLK) == 0 and d % LANES == 0
    xi, yi, ci = lax.axis_index("x"), lax.axis_index("y"), lax.axis_index("c")
    chip = 2 * xi + yi
    dev = 2 * chip + ci
    x0, tgt = x[0], loss_target[0]

    c8 = _exchange([c.reshape(d // LANES, LANES)], "all", False, name="gather_c")[0].reshape(8, d)
    ncol = w_mod.shape[2]
    b_loc = lax.dynamic_slice_in_dim(b_mod, chip * ncol, ncol, axis=1)
    m_loc = _mod_fwd(c8, w_mod, b_loc.reshape(depth, 1, ncol), name="mod_fwd")
    m_all = _exchange([m_loc], "chips", False, name="gather_mod")[0]
    mod = jnp.transpose(lax.dynamic_index_in_dim(m_all, dev, axis=2, keepdims=False), (1, 0, 2)).reshape(depth, N_MOD, d)

    col_sharded = ["ffn1_w_gate", "ffn1_w_up", "w_in", "ffn2_w_gate", "ffn2_w_up"]
    row_sharded = ["ffn1_w_down", "w_out", "ffn2_w_down"]
    big = col_sharded + row_sharded
    flat = [wts[n].astype(BF16).reshape(-1, wts[n].shape[-1]) for n in big]
    gathered = {n: g.reshape((4,) + wts[n].shape) for n, g in zip(big, _exchange(flat, "chips", False, name="gather_w"))}
    full = {}
    for n in col_sharded:
        g = gathered[n]
        full[n] = jnp.moveaxis(g, 0, 2).reshape(depth, g.shape[2], -1)
    for n in row_sharded:
        g = gathered[n]
        full[n] = jnp.moveaxis(g, 0, 1).reshape(depth, -1, g.shape[3])
    ws = [dict(gu1=jnp.concatenate([full["ffn1_w_gate"][l], full["ffn1_w_up"][l]], axis=1), d1=full["ffn1_w_down"][l],
               gu2=jnp.concatenate([full["ffn2_w_gate"][l], full["ffn2_w_up"][l]], axis=1), d2=full["ffn2_w_down"][l],
               **{"in": full["w_in"][l], "out": full["w_out"][l]}) for l in range(depth)]

    _, la, lc = _lb_prep(hgrn_lb_logits, name="lb_prep")
    cos, sin = _rope_tables(s)
    pars = [dict(gq=jnp.tile(q_norm_g[l], 2)[None], gk=jnp.tile(k_norm_g[l], 2)[None], hg=hgrn_norm_g[l][None],
                 la=la[l:l + 1], lc=lc[l:l + 1], cos=cos, sin=sin) for l in range(depth)]

    xs, saved = x0, []
    for l in range(depth):
        xs, sv = _layer_fwd(xs, mod[l], ws[l], pars[l], f"l{l}")
        saved.append(sv)
    dx, lpart = _loss_grad(xs, tgt, name="loss")
    grads = [None] * depth
    for l in reversed(range(depth)):
        dx, grads[l] = _layer_bwd(dx, saved[l], mod[l], ws[l], pars[l], f"l{l}")

    stack = lambda k: jnp.stack([grads[l][k] for l in range(depth)])
    small = [stack("dmod"), stack("gq"), stack("gk"), stack("hg"), stack("la"), stack("lc"), lpart[0, :1]]
    packed = _pack(small)
    allp = _exchange([packed], "all", False, name="gather_small")[0]
    tot = _unpack(_sum_slots(allp, name="sum_small").reshape(-1), small)
    g_b_mod = tot[0].reshape(depth, N_MOD * d)
    loss = tot[6][0]
    g_small = {"b_mod": g_b_mod, "q_norm_g": tot[1], "k_norm_g": tot[2], "hgrn_norm_g": tot[3],
               "hgrn_lb_logits": _lb_bwd(hgrn_lb_logits, tot[4], tot[5], name="lb_bwd")}

    dm_all = allp.reshape(8, -1)[:, :depth * N_MOD * d].reshape(8, depth, N_MOD * d)
    dm_loc = jnp.transpose(lax.dynamic_slice_in_dim(dm_all, chip * ncol, ncol, axis=2), (1, 0, 2))
    g_w_mod = _mod_bwd(c8.T, dm_loc, name="mod_bwd")

    fgrad = {
        "ffn1_w_gate": jnp.stack([grads[l]["gu1"][:, :grads[l]["gu1"].shape[1] // 2] for l in range(depth)]),
        "ffn1_w_up": jnp.stack([grads[l]["gu1"][:, grads[l]["gu1"].shape[1] // 2:] for l in range(depth)]),
        "ffn2_w_gate": jnp.stack([grads[l]["gu2"][:, :grads[l]["gu2"].shape[1] // 2] for l in range(depth)]),
        "ffn2_w_up": jnp.stack([grads[l]["gu2"][:, grads[l]["gu2"].shape[1] // 2:] for l in range(depth)]),
        "w_in": stack("win"), "ffn1_w_down": stack("d1"), "ffn2_w_down": stack("d2"), "w_out": stack("wout"),
    }
    by_chip = []
    for n in big:
        g = fgrad[n]
        if n in col_sharded:
            g = jnp.moveaxis(g.reshape(depth, g.shape[1], 4, -1), 2, 0)
        else:
            g = jnp.moveaxis(g.reshape(depth, 4, -1, g.shape[2]), 1, 0)
        by_chip.append(g.reshape(4, -1, g.shape[-1]))
    got = _exchange(by_chip, "chips", True, name="scatter_grads")
    parts = [_sum_slots(g, name=f"sum_{n}", out_dtype=BF16) for n, g in zip(big, got)]
    both = dict(zip(big, _exchange(parts, "sib", False, name="swap_grads")))

    outs = {}
    for n in names:
        w2 = wts[n].reshape(-1, wts[n].shape[-1])
        if n in both:
            gp = [both[n][0], both[n][1]]
        elif n == "w_mod":
            gp = [g_w_mod.reshape(w2.shape)]
        else:
            gp = [g_small[n].reshape(w2.shape)]
        res = _adamw(w2, gp, mom[n].reshape(w2.shape), var[n].reshape(w2.shape), name=f"adamw_{n}")
        outs[n] = [r.reshape(wts[n].shape) for r in res]
    return (loss, dx[None], *[outs[n][0] for n in names], *[outs[n][1] for n in names], *[outs[n][2] for n in names],
            *[outs[n][3] for n in names])
```

```python
import functools
import math

import jax
import jax.numpy as jnp
from jax import lax
from jax.experimental import pallas as pl
from jax.experimental.pallas import tpu as pltpu

F32 = jnp.float32
BF16 = jnp.bfloat16
MESH_ID = pl.DeviceIdType.MESH

HEAD_DIM = 64
SB_W = 256
DIL_W = 256
HG_W = 512
HG_D = 128
IN_W = 3 * SB_W + 3 * DIL_W + 4 * HG_W
MIX_W = SB_W + DIL_W + HG_W
DIL_PATTERNS = ((128, 1), (512, 4), (2048, 16))
ROPE_THETA = 10000.0
EPS = 1e-6
LB_FLOOR = 1e-30
NEG_BIG = -1e30
N_MOD = 9
ADAM_LR = 0.001
ADAM_B1 = 0.9
ADAM_B2 = 0.999
ADAM_EPS = 1e-08
ADAM_WD = 0.01
ADAM_STEP = 10

LANES = 128
QBLK = 128
DIL_TILE = 512
HG_BLK = 16
HG_TILE = 256
SB_EXIT = 88.0
VMEM_LIMIT = 48 * 1024 * 1024
XCHG_CHUNKS = 8
XCHG_MIN_BYTES = 1 << 19

NN = (((1,), (0,)), ((), ()))
NT = (((1,), (1,)), ((), ()))
TN = (((0,), (0,)), ((), ()))


def _pcall(body, **kw):
    return pl.pallas_call(body, **kw)


def _cparams(*sem):
    return pltpu.CompilerParams(dimension_semantics=sem if sem else None, vmem_limit_bytes=VMEM_LIMIT)


def _dot(a, b, dims=NN):
    return lax.dot_general(a, b, dims, preferred_element_type=F32)


def _split(x, n):
    parts = []
    r = x
    for i in range(n):
        p = r.astype(BF16)
        parts.append(p)
        if i + 1 < n:
            r = r - p.astype(F32)
    return parts


def _xdot(x, m, n=2):
    return sum(_dot(p, m) for p in _split(x, n))


def _xdot_left(m, x, n=3):
    return sum(_dot(m, p) for p in _split(x, n))


def _iota(shape, dim):
    return lax.broadcasted_iota(jnp.int32, shape, dim)


def _sigmoid(x):
    return 1.0 / (1.0 + jnp.exp(-x))


def _tile(dim, pref, mult=LANES):
    t = (min(pref, dim) // mult) * mult
    while t >= mult:
        if dim % t == 0:
            return t
        t -= mult
    return dim


def _rows(dim, pref):
    return _tile(dim, pref, 8)


def _mm(a, b, *, name, tb=False, tm=512, tn=1024, tk=1024, out_dtype=F32, res=None, scale=None):
    m, kd = a.shape
    n = b.shape[0] if tb else b.shape[1]
    tm, tn, tk = _rows(m, tm), _tile(n, tn), _tile(kd, tk)
    nk = kd // tk
    epi = res is not None

    def body(*refs):
        if epi:
            a_ref, b_ref, r_ref, s_ref, o_ref, x_ref, acc = refs
        else:
            a_ref, b_ref, o_ref, acc = refs
        k = pl.program_id(2)

        @pl.when(k == 0)
        def _():
            acc[...] = jnp.zeros_like(acc)

        acc[...] += _dot(a_ref[...], b_ref[...], NT if tb else NN)

        @pl.when(k == nk - 1)
        def _():
            o_ref[...] = acc[...].astype(o_ref.dtype)
            if epi:
                x_ref[...] = r_ref[...] + s_ref[...] * acc[...]

    in_specs = [
        pl.BlockSpec((tm, tk), lambda i, j, k: (i, k)),
        pl.BlockSpec((tn, tk), lambda i, j, k: (j, k)) if tb else pl.BlockSpec((tk, tn), lambda i, j, k: (k, j)),
    ]
    out_shape = [jax.ShapeDtypeStruct((m, n), out_dtype)]
    out_specs = [pl.BlockSpec((tm, tn), lambda i, j, k: (i, j))]
    args = [a, b]
    if epi:
        in_specs += [pl.BlockSpec((tm, tn), lambda i, j, k: (i, j)), pl.BlockSpec((1, tn), lambda i, j, k: (0, j))]
        out_shape.append(jax.ShapeDtypeStruct((m, n), F32))
        out_specs.append(pl.BlockSpec((tm, tn), lambda i, j, k: (i, j)))
        args += [res, scale]
    out = _pcall(
        body, name=name, grid=(m // tm, n // tn, nk), in_specs=in_specs, out_specs=out_specs, out_shape=out_shape,
        scratch_shapes=[pltpu.VMEM((tm, tn), F32)], compiler_params=_cparams("parallel", "parallel", "arbitrary"),
    )(*args)
    return out if epi else out[0]


def _mm_tn(a, b, *, name, tm=1024, tn=1408, tk=512, out_dtype=BF16):
    s, m = a.shape
    n = b.shape[1]
    tm, tn, tk = _tile(m, tm), _tile(n, tn), _rows(s, tk)
    nk = s // tk

    def body(a_ref, b_ref, o_ref, acc):
        k = pl.program_id(2)

        @pl.when(k == 0)
        def _():
            acc[...] = jnp.zeros_like(acc)

        acc[...] += _dot(a_ref[...], b_ref[...], TN)

        @pl.when(k == nk - 1)
        def _():
            o_ref[...] = acc[...].astype(o_ref.dtype)

    return _pcall(
        body, name=name, grid=(m // tm, n // tn, nk),
        in_specs=[pl.BlockSpec((tk, tm), lambda i, j, k: (k, i)), pl.BlockSpec((tk, tn), lambda i, j, k: (k, j))],
        out_specs=pl.BlockSpec((tm, tn), lambda i, j, k: (i, j)), out_shape=jax.ShapeDtypeStruct((m, n), out_dtype),
        scratch_shapes=[pltpu.VMEM((tm, tn), F32)], compiler_params=_cparams("parallel", "parallel", "arbitrary"),
    )(a, b)


TE = 512


def _row_spec(t, w, col=0):
    return pl.BlockSpec((t, w), lambda i, col=col: (i, col))


def _vec_spec(w, col=0):
    return pl.BlockSpec((1, w), lambda i, col=col: (0, col))


def _norm_mod(x, sc, sh, *, name):
    s, d = x.shape
    t = _rows(s, TE)

    def body(x_ref, sc_ref, sh_ref, h_ref):
        xv = x_ref[...]
        r = lax.rsqrt(jnp.mean(xv * xv, axis=1, keepdims=True) + EPS)
        h_ref[...] = ((xv * r) * (1.0 + sc_ref[...]) + sh_ref[...]).astype(BF16)

    return _pcall(body, name=name, grid=(s // t,), in_specs=[_row_spec(t, d), _vec_spec(d), _vec_spec(d)],
                  out_specs=_row_spec(t, d), out_shape=jax.ShapeDtypeStruct((s, d), BF16),
                  compiler_params=_cparams("parallel"))(x, sc, sh)


def _norm_mod_bwd(dh, x, sc, dxo, *, name):
    s, d = x.shape
    t = _rows(s, TE)

    def body(dh_ref, x_ref, sc_ref, dxo_ref, dx_ref, dsc_ref, dsh_ref):
        @pl.when(pl.program_id(0) == 0)
        def _():
            dsc_ref[...] = jnp.zeros_like(dsc_ref)
            dsh_ref[...] = jnp.zeros_like(dsh_ref)

        xv = x_ref[...]
        dhv = dh_ref[...]
        r = lax.rsqrt(jnp.mean(xv * xv, axis=1, keepdims=True) + EPS)
        xn = xv * r
        dxn = dhv * (1.0 + sc_ref[...])
        dx_ref[...] = dxo_ref[...] + r * (dxn - xn * jnp.mean(dxn * xn, axis=1, keepdims=True))
        dsc_ref[...] += jnp.sum(dhv * xn, axis=0, keepdims=True)
        dsh_ref[...] += jnp.sum(dhv, axis=0, keepdims=True)

    return _pcall(
        body, name=name, grid=(s // t,),
        in_specs=[_row_spec(t, d), _row_spec(t, d), _vec_spec(d), _row_spec(t, d)],
        out_specs=[_row_spec(t, d), _vec_spec(d), _vec_spec(d)],
        out_shape=[jax.ShapeDtypeStruct((s, d), F32), jax.ShapeDtypeStruct((1, d), F32), jax.ShapeDtypeStruct((1, d), F32)],
        compiler_params=_cparams("arbitrary"))(dh, x, sc, dxo)


def _swiglu(uv, *, name):
    s, f2 = uv.shape
    f = f2 // 2
    t = _rows(s, 256)

    def body(uv_ref, a_ref):
        u = uv_ref[:, :f]
        v = uv_ref[:, f:]
        a_ref[...] = (u * _sigmoid(u) * v).astype(BF16)

    return _pcall(body, name=name, grid=(s // t,), in_specs=[_row_spec(t, f2)], out_specs=_row_spec(t, f),
                  out_shape=jax.ShapeDtypeStruct((s, f), BF16), compiler_params=_cparams("parallel"))(uv)


def _swiglu_bwd(da, uv, *, name):
    s, f2 = uv.shape
    f = f2 // 2
    t = _rows(s, 256)

    def body(da_ref, uv_ref, d_ref):
        u = uv_ref[:, :f]
        v = uv_ref[:, f:]
        dav = da_ref[...]
        sg = _sigmoid(u)
        d_ref[:, :f] = (dav * v * (sg * (1.0 + u * (1.0 - sg)))).astype(BF16)
        d_ref[:, f:] = (dav * (u * sg)).astype(BF16)

    return _pcall(body, name=name, grid=(s // t,), in_specs=[_row_spec(t, f), _row_spec(t, f2)],
                  out_specs=_row_spec(t, f2), out_shape=jax.ShapeDtypeStruct((s, f2), BF16),
                  compiler_params=_cparams("parallel"))(da, uv)


def _gate_bwd(dxo, y, sg, *, name):
    s, d = y.shape
    t = _rows(s, TE)

    def body(dxo_ref, y_ref, sg_ref, dy_ref, ds_ref):
        @pl.when(pl.program_id(0) == 0)
        def _():
            ds_ref[...] = jnp.zeros_like(ds_ref)

        dv = dxo_ref[...]
        dy_ref[...] = (sg_ref[...] * dv).astype(BF16)
        ds_ref[...] += jnp.sum(dv * y_ref[...], axis=0, keepdims=True)

    return _pcall(
        body, name=name, grid=(s // t,), in_specs=[_row_spec(t, d), _row_spec(t, d), _vec_spec(d)],
        out_specs=[_row_spec(t, d), _vec_spec(d)],
        out_shape=[jax.ShapeDtypeStruct((s, d), BF16), jax.ShapeDtypeStruct((1, d), F32)],
        compiler_params=_cparams("arbitrary"))(dxo, y, sg)


def _loss_grad(y, tgt, *, name):
    s, d = y.shape
    t = _rows(s, TE)
    nt = s // t

    def body(y_ref, t_ref, dy_ref, l_ref, acc):
        i = pl.program_id(0)

        @pl.when(i == 0)
        def _():
            acc[...] = jnp.zeros_like(acc)

        e = y_ref[...] - t_ref[...]
        dy_ref[...] = e * (1.0 / d)
        acc[...] += jnp.sum(e * e, axis=0, keepdims=True)

        @pl.when(i == nt - 1)
        def _():
            l_ref[...] = jnp.broadcast_to(jnp.sum(acc[...], axis=1, keepdims=True) * (0.5 / d), l_ref.shape)

    return _pcall(
        body, name=name, grid=(nt,), in_specs=[_row_spec(t, d), _row_spec(t, d)],
        out_specs=[_row_spec(t, d), pl.BlockSpec((1, LANES), lambda i: (0, 0))],
        out_shape=[jax.ShapeDtypeStruct((s, d), F32), jax.ShapeDtypeStruct((1, LANES), F32)],
        scratch_shapes=[pltpu.VMEM((1, d), F32)], compiler_params=_cparams("arbitrary"))(y, tgt)


def _sb_consts():
    row = _iota((QBLK, LANES), 0)
    lane = _iota((QBLK, LANES), 1)
    ones = jnp.ones((QBLK, LANES), BF16)
    after = jnp.concatenate([jnp.where(row > lane, 1.0, 0.0).astype(BF16), ones], axis=1)
    from_ = jnp.concatenate([jnp.where(row >= lane, 1.0, 0.0).astype(BF16), ones], axis=1)
    return row, lane, after, from_


def _sb_scores(qm, kb, strict):
    z = _dot(qm, kb, NT) * (HEAD_DIM ** -0.5)
    sp = jnp.log(1.0 + jnp.exp(-jnp.abs(z)))
    lnb = -(jnp.maximum(z, 0.0) + sp)
    lb = jnp.minimum(z, 0.0) - sp
    if strict is not None:
        lnb = jnp.where(strict, lnb, 0.0)
    return lnb, lb


def _sb_fwd(qkv, *, name):
    s = qkv.shape[0]
    nq = s // QBLK

    def body(q_ref, k_ref, v_ref, o_ref, *scr):
        acc, osc = scr[:4], scr[4:]
        qi = pl.program_id(0)
        row, lane, after, _ = _sb_consts()
        strict = lane < row
        h0 = lane < HEAD_DIM
        q = q_ref[...]
        qms = []
        for p in range(2):
            qp = q[:, p * LANES:(p + 1) * LANES]
            qms += [jnp.where(h0, qp, jnp.zeros_like(qp)), jnp.where(h0, jnp.zeros_like(qp), qp)]

        def block(kj, mask):
            off = pl.multiple_of(kj * QBLK, QBLK)
            kbs = [k_ref[pl.ds(off, QBLK), p * LANES:(p + 1) * LANES] for p in range(2)]
            vbs = [v_ref[pl.ds(off, QBLK), p * LANES:(p + 1) * LANES] for p in range(2)]
            sc = [_sb_scores(qms[c], kbs[c // 2], mask) for c in range(4)]
            trs = [_xdot(sc[c][0], after) for c in range(4)]
            top = None
            for c in range(4):
                w = jnp.exp(sc[c][1] + trs[c][:, :QBLK] + acc[c][...])
                if mask is not None:
                    w = jnp.where(mask, w, 0.0)
                osc[c][...] += _xdot(w, vbs[c // 2])
                new = acc[c][...] + trs[c][:, QBLK:]
                acc[c][...] = new
                top = new if top is None else jnp.maximum(top, new)
            return jnp.max(top)

        for ref in scr:
            ref[...] = jnp.zeros_like(ref)
        top = block(qi, strict)
        lax.while_loop(lambda c: (c[0] >= 0) & (c[1] > -SB_EXIT), lambda c: (c[0] - 1, block(c[0], None)), (qi - 1, top))
        for p in range(2):
            o_ref[:, p * LANES:(p + 1) * LANES] = jnp.where(h0, osc[2 * p][...], osc[2 * p + 1][...])

    return _pcall(
        body, name=name, grid=(nq,),
        in_specs=[pl.BlockSpec((QBLK, SB_W), lambda i: (i, 0)),
                  pl.BlockSpec((s, SB_W), lambda i: (0, 1)),
                  pl.BlockSpec((s, SB_W), lambda i: (0, 2))],
        out_specs=pl.BlockSpec((QBLK, SB_W), lambda i: (i, 0)),
        out_shape=jax.ShapeDtypeStruct((s, SB_W), F32),
        scratch_shapes=[pltpu.VMEM((QBLK, LANES), F32)] * 8,
        compiler_params=_cparams("arbitrary"))(qkv, qkv, qkv)


def _sb_bwd(qkv, o, dmix, *, name):
    s = qkv.shape[0]
    nq = s // QBLK
    scale = HEAD_DIM ** -0.5

    def body(q_ref, k_ref, v_ref, o_ref, do_ref, dq_ref, dk_ref, dv_ref, a0, a1, r0, r1, dqs, dks, dvs):
        acc, racc = (a0, a1), (r0, r1)
        i = pl.program_id(1)
        qi = nq - 1 - i
        row, lane, after, from_ = _sb_consts()
        strict = lane < row

        @pl.when(i == 0)
        def _():
            dks[...] = jnp.zeros_like(dks)
            dvs[...] = jnp.zeros_like(dvs)

        q = q_ref[...]
        do = do_ref[...]
        dob = do.astype(BF16)
        dd = do * o_ref[...]
        dol = (do - dob.astype(F32)).astype(BF16)
        zero = jnp.zeros_like(q)
        hms = (lane < HEAD_DIM, lane >= HEAD_DIM)
        qms = [jnp.where(hm, q, zero) for hm in hms]
        doms = [jnp.where(hm, dob, zero) for hm in hms]
        dols = [jnp.where(hm, dol, zero) for hm in hms]
        dsums = [jnp.sum(jnp.where(hm, dd, 0.0), axis=1, keepdims=True) for hm in hms]

        def block(kj, mask):
            off = pl.multiple_of(kj * QBLK, QBLK)
            kb = k_ref[pl.ds(off, QBLK), :]
            vb = v_ref[pl.ds(off, QBLK), :]
            top, dq, dk, dv = None, None, None, None
            sc = [_sb_scores(qms[h], kb, mask) for h in range(2)]
            trs = [_xdot(sc[h][0], after) for h in range(2)]
            dws = [_dot(doms[h], vb, NT) + _dot(dols[h], vb, NT) for h in range(2)]
            for h in range(2):
                lb, tr = sc[h][1], trs[h]
                w = jnp.exp(lb + tr[:, :QBLK] + acc[h][...])
                if mask is not None:
                    w = jnp.where(mask, w, 0.0)
                g = w * dws[h]
                tg = _xdot(g, from_)
                before = dsums[h] - (tg[:, :QBLK] + racc[h][...])
                dz = g - jnp.exp(lb) * (g + before)
                if mask is not None:
                    dz = jnp.where(mask, dz, 0.0)
                dzb = (dz * scale).astype(BF16)
                dqh = _dot(dzb, jnp.where(hms[h], kb, zero))
                dkh = _dot(dzb, qms[h], TN)
                dvh = _dot(w.astype(BF16), doms[h], TN)
                dq, dk, dv = (dqh, dkh, dvh) if h == 0 else (dq + dqh, dk + dkh, dv + dvh)
                new = acc[h][...] + tr[:, QBLK:]
                acc[h][...] = new
                racc[h][...] += tg[:, QBLK:]
                top = new if top is None else jnp.maximum(top, new)
            dqs[...] += dq
            dks[pl.ds(off, QBLK), :] += dk
            dvs[pl.ds(off, QBLK), :] += dv
            return jnp.max(top)

        for ref in (dqs, a0, a1, r0, r1):
            ref[...] = jnp.zeros_like(ref)
        top = block(qi, strict)
        lax.while_loop(lambda c: (c[0] >= 0) & (c[1] > -SB_EXIT), lambda c: (c[0] - 1, block(c[0], None)), (qi - 1, top))
        dq_ref[...] = dqs[...]
        fin = pl.multiple_of(qi * QBLK, QBLK)
        dk_ref[...] = dks[pl.ds(fin, QBLK), :]
        dv_ref[...] = dvs[pl.ds(fin, QBLK), :]

    blk = lambda c0: pl.BlockSpec((QBLK, LANES), lambda p, i, c0=c0: (nq - 1 - i, c0 + p))
    return _pcall(
        body, name=name, grid=(2, nq),
        in_specs=[blk(0), pl.BlockSpec((s, LANES), lambda p, i: (0, 2 + p)), pl.BlockSpec((s, LANES), lambda p, i: (0, 4 + p)),
                  blk(0), blk(0)],
        out_specs=[blk(0), blk(0), blk(0)],
        out_shape=[jax.ShapeDtypeStruct((s, SB_W), F32)] * 3,
        scratch_shapes=[pltpu.VMEM((QBLK, LANES), F32)] * 5 + [pltpu.VMEM((s, LANES), F32), pltpu.VMEM((s, LANES), F32)],
        compiler_params=_cparams("arbitrary", "arbitrary"))(qkv, qkv, qkv, o, dmix)


def _seg_consts():
    r = _iota((LANES, LANES), 0)
    c = _iota((LANES, LANES), 1)
    return jnp.where((r >> 6) == (c >> 6), 1.0, 0.0).astype(BF16)


def _rot_half(x, lane):
    half = HEAD_DIM // 2
    return jnp.where((lane & (HEAD_DIM - 1)) < half, pltpu.roll(x, LANES - half, 1), pltpu.roll(x, half, 1))


def _rope_tables(s):
    half = HEAD_DIM // 2
    inv_freq = ROPE_THETA ** (-jnp.arange(half, dtype=F32) * 2.0 / HEAD_DIM)
    ang = jnp.arange(s, dtype=F32)[:, None] * inv_freq[None, :]
    cos, sin = jnp.cos(ang), jnp.sin(ang)
    return jnp.tile(jnp.concatenate([cos, cos], axis=1), (1, 2)), jnp.tile(jnp.concatenate([-sin, sin], axis=1), (1, 2))


def _dil_prep(proj, gq, gk, cos, sin, *, name):
    s = proj.shape[0]
    t = _rows(s, TE)
    c0 = 3 * SB_W // LANES

    def body(q_ref, k_ref, gq_ref, gk_ref, cos_ref, sin_ref, qo_ref, ko_ref):
        seg = _seg_consts()
        lane = _iota((t, LANES), 1)
        cs, sn = cos_ref[...], sin_ref[...]
        for x_ref, g_ref, o_ref, mul in ((q_ref, gq_ref, qo_ref, HEAD_DIM ** -0.5), (k_ref, gk_ref, ko_ref, 1.0)):
            for j in range(2):
                xv = x_ref[:, j * LANES:(j + 1) * LANES]
                ms = _xdot(xv * xv, seg, 3) * (1.0 / HEAD_DIM)
                xn = xv * lax.rsqrt(ms + EPS) * g_ref[...]
                o_ref[:, j * LANES:(j + 1) * LANES] = ((xn * cs + _rot_half(xn, lane) * sn) * mul).astype(BF16)

    return _pcall(
        body, name=name, grid=(s // t,),
        in_specs=[pl.BlockSpec((t, DIL_W), lambda i: (i, c0 // 2)), pl.BlockSpec((t, DIL_W), lambda i: (i, c0 // 2 + 1)),
                  _vec_spec(LANES), _vec_spec(LANES), _row_spec(t, LANES), _row_spec(t, LANES)],
        out_specs=[_row_spec(t, DIL_W), _row_spec(t, DIL_W)],
        out_shape=[jax.ShapeDtypeStruct((s, DIL_W), BF16)] * 2, compiler_params=_cparams("parallel"))(proj, proj, gq, gk, cos, sin)


def _dil_prep_bwd(proj, gq, gk, cos, sin, dqs, dks, dvs, *, name):
    s = proj.shape[0]
    t = _rows(s, TE)
    c0 = 3 * SB_W // LANES

    def body(q_ref, k_ref, gq_ref, gk_ref, cos_ref, sin_ref, a0, a1, a2, b0, b1, b2, c0_ref, c1_ref, c2_ref,
             dq_ref, dk_ref, dv_ref, dgq_ref, dgk_ref):
        @pl.when(pl.program_id(0) == 0)
        def _():
            dgq_ref[...] = jnp.zeros_like(dgq_ref)
            dgk_ref[...] = jnp.zeros_like(dgk_ref)

        dv_ref[...] = c0_ref[...] + c1_ref[...] + c2_ref[...]
        seg = _seg_consts()
        lane = _iota((t, LANES), 1)
        cs, sn = cos_ref[...], sin_ref[...]
        for x_ref, g_ref, parts, o_ref, dg_ref, mul in ((q_ref, gq_ref, (a0, a1, a2), dq_ref, dgq_ref, HEAD_DIM ** -0.5),
                                                          (k_ref, gk_ref, (b0, b1, b2), dk_ref, dgk_ref, 1.0)):
            for j in range(2):
                sl = slice(j * LANES, (j + 1) * LANES)
                dout = (parts[0][:, sl] + parts[1][:, sl] + parts[2][:, sl]) * mul
                dxn = dout * cs + _rot_half(dout * sn, lane)
                xv = x_ref[:, sl]
                r = lax.rsqrt(_xdot(xv * xv, seg, 3) * (1.0 / HEAD_DIM) + EPS)
                xh = xv * r
                dg_ref[...] += jnp.sum(dxn * xh, axis=0, keepdims=True)
                dxh = dxn * g_ref[...]
                o_ref[:, sl] = r * (dxh - xh * (_xdot(dxh * xh, seg, 3) * (1.0 / HEAD_DIM)))

    rs = _row_spec(t, DIL_W)
    return _pcall(
        body, name=name, grid=(s // t,),
        in_specs=[pl.BlockSpec((t, DIL_W), lambda i: (i, c0 // 2)), pl.BlockSpec((t, DIL_W), lambda i: (i, c0 // 2 + 1)),
                  _vec_spec(LANES), _vec_spec(LANES), _row_spec(t, LANES), _row_spec(t, LANES)] + [rs] * 9,
        out_specs=[rs, rs, rs, _vec_spec(LANES), _vec_spec(LANES)],
        out_shape=[jax.ShapeDtypeStruct((s, DIL_W), F32)] * 3 + [jax.ShapeDtypeStruct((1, LANES), F32)] * 2,
        compiler_params=_cparams("arbitrary"))(proj, proj, gq, gk, cos, sin, *dqs, *dks, *dvs)


def _dil_masks(n):
    row = _iota((QBLK, LANES), 0)
    col = _iota((QBLK, LANES), 1)
    return col <= row, (col >= row) & (n > 0)


def _dil_fwd(q, k, v, r, *, name):
    s = q.shape[0]
    rows = s // r
    tb = _tile(rows, DIL_TILE, QBLK)
    nsub = tb // QBLK
    shape = (rows, r * DIL_W)
    q, k, v = (a.reshape(shape) for a in (q, k, v))

    def body(q_ref, kc_ref, kp_ref, vc_ref, vp_ref, num_ref, den_ref, mx_ref):
        n = pl.program_id(1)
        lane = _iota((QBLK, LANES), 1)
        h0 = lane < HEAD_DIM
        for j in range(nsub):
            rs = slice(j * QBLK, (j + 1) * QBLK)
            vc_m, vp_m = _dil_masks(n if j == 0 else 1)
            for p in range(2):
                ls = slice(p * LANES, (p + 1) * LANES)
                qv = q_ref[rs, ls]
                kc, vc = kc_ref[rs, ls], vc_ref[rs, ls]
                if j == 0:
                    kp, vp = kp_ref[:, ls], vp_ref[:, ls]
                else:
                    kp, vp = kc_ref[(j - 1) * QBLK:j * QBLK, ls], vc_ref[(j - 1) * QBLK:j * QBLK, ls]
                res = []
                for h in range(2):
                    qm = jnp.where(h0 if h == 0 else ~h0, qv, jnp.zeros_like(qv))
                    sc = jnp.where(vc_m, _dot(qm, kc, NT), NEG_BIG)
                    sp = jnp.where(vp_m, _dot(qm, kp, NT), NEG_BIG)
                    mx = jnp.maximum(jnp.max(sc, axis=1, keepdims=True), jnp.max(sp, axis=1, keepdims=True))
                    pc = jnp.exp(sc - mx)
                    pp = jnp.exp(sp - mx)
                    den = jnp.sum(pc, axis=1, keepdims=True) + jnp.sum(pp, axis=1, keepdims=True)
                    res.append((_dot(pc.astype(BF16), vc) + _dot(pp.astype(BF16), vp), den, mx))
                num_ref[rs, ls] = jnp.where(h0, res[0][0], res[1][0])
                den_ref[rs, ls] = jnp.where(h0, res[0][1], res[1][1])
                mx_ref[rs, ls] = jnp.where(h0, res[0][2], res[1][2])

    cur = pl.BlockSpec((tb, DIL_W), lambda rho, n: (n, rho))
    prev = pl.BlockSpec((QBLK, DIL_W), lambda rho, n: (jnp.maximum(n * nsub - 1, 0), rho))
    return _pcall(
        body, name=name, grid=(r, rows // tb), in_specs=[cur, cur, prev, cur, prev], out_specs=[cur, cur, cur],
        out_shape=[jax.ShapeDtypeStruct(shape, F32)] * 3,
        compiler_params=_cparams("parallel", "arbitrary"))(q, k, k, v, v)


def _dil_bwd(q, k, v, do, mall, zall, delta, r, *, name):
    s = q.shape[0]
    rows = s // r
    tb = _tile(rows, DIL_TILE, QBLK)
    nsub, nbig = tb // QBLK, rows // tb
    shape = (rows, r * DIL_W)
    q, k, v, do, mall, zall, delta = (a.reshape(shape) for a in (q, k, v, do, mall, zall, delta))

    def body(q_ref, kc_ref, kp_ref, vc_ref, vp_ref, do_ref, m_ref, z_ref, dl_ref, dq_ref, dk_ref, dv_ref, pk, pv):
        n = pl.program_id(1)
        lane = _iota((QBLK, LANES), 1)
        h0 = lane < HEAD_DIM

        @pl.when(n == 0)
        def _():
            pk[...] = jnp.zeros_like(pk)
            pv[...] = jnp.zeros_like(pv)

        @pl.when(n < nbig)
        def _():
            dk_ref[...] = pk[...]
            dv_ref[...] = pv[...]
            for j in range(nsub):
                rs = slice(j * QBLK, (j + 1) * QBLK)
                ps = slice((j - 1) * QBLK, j * QBLK)
                vc_m, vp_m = _dil_masks(n if j == 0 else 1)
                for p in range(2):
                    ls = slice(p * LANES, (p + 1) * LANES)
                    qv = q_ref[rs, ls]
                    dob = do_ref[rs, ls].astype(BF16)
                    zero = jnp.zeros_like(qv)
                    kc, vc = kc_ref[rs, ls], vc_ref[rs, ls]
                    kp, vp = (kp_ref[:, ls], vp_ref[:, ls]) if j == 0 else (kc_ref[ps, ls], vc_ref[ps, ls])
                    dq = None
                    acc = [None] * 4
                    for h in range(2):
                        hm = h0 if h == 0 else ~h0
                        qm = jnp.where(hm, qv, zero)
                        dom = jnp.where(hm, dob, zero)
                        c = p * LANES + h * HEAD_DIM
                        mrow, dlrow = m_ref[rs, c:c + 1], dl_ref[rs, c:c + 1]
                        rz = 1.0 / z_ref[rs, c:c + 1]
                        for kb, vb, valid, o in ((kc, vc, vc_m, 0), (kp, vp, vp_m, 2)):
                            sc = jnp.where(valid, _dot(qm, kb, NT), NEG_BIG)
                            pr = jnp.exp(sc - mrow) * rz
                            ds = (pr * (_dot(dom, vb, NT) - dlrow)).astype(BF16)
                            dqx = _dot(ds, jnp.where(hm, kb, zero))
                            dq = dqx if dq is None else dq + dqx
                            for i, x in ((o, _dot(ds, qm, TN)), (o + 1, _dot(pr.astype(BF16), dom, TN))):
                                acc[i] = x if acc[i] is None else acc[i] + x
                    dq_ref[rs, ls] = dq
                    if j == 0:
                        pk[rs, ls] = acc[0]
                        pv[rs, ls] = acc[1]
                        dk_ref[tb - QBLK:, ls] += acc[2]
                        dv_ref[tb - QBLK:, ls] += acc[3]
                    else:
                        pk[rs, ls] = acc[0]
                        pv[rs, ls] = acc[1]
                        pk[ps, ls] += acc[2]
                        pv[ps, ls] += acc[3]

        @pl.when(n == nbig)
        def _():
            dk_ref[...] = pk[...]
            dv_ref[...] = pv[...]

    last = nbig - 1
    cur = pl.BlockSpec((tb, DIL_W), lambda rho, n: (jnp.minimum(n, last), rho))
    prev = pl.BlockSpec((QBLK, DIL_W), lambda rho, n: (jnp.maximum(jnp.minimum(n, last) * nsub - 1, 0), rho))
    late = pl.BlockSpec((tb, DIL_W), lambda rho, n: (jnp.maximum(n - 1, 0), rho))
    return _pcall(
        body, name=name, grid=(r, nbig + 1), in_specs=[cur, cur, prev, cur, prev, cur, cur, cur, cur],
        out_specs=[cur, late, late], out_shape=[jax.ShapeDtypeStruct(shape, F32)] * 3,
        scratch_shapes=[pltpu.VMEM((tb, DIL_W), F32)] * 2,
        compiler_params=_cparams("parallel", "arbitrary"))(q, k, k, v, v, do, mall, zall, delta)


HG_SHIFT = HG_BLK.bit_length() - 1
HG_Q0, HG_F0, HG_I0 = (3 * SB_W + 3 * DIL_W) // HG_D, (3 * SB_W + 3 * DIL_W + HG_W) // HG_D, (3 * SB_W + 3 * DIL_W + 2 * HG_W) // HG_D


def _hg_inputs(qh, z, v, la, lc, t):
    r = _iota((t, t), 0)
    c = _iota((t, t), 1)
    same = (r >> HG_SHIFT) == (c >> HG_SHIFT)
    tri = jnp.where(same & (c <= r), 1.0, 0.0).astype(BF16)
    blk = jnp.where(same, 1.0, 0.0).astype(BF16)
    lsg = jnp.minimum(z, 0.0) - jnp.log(1.0 + jnp.exp(-jnp.abs(z)))
    b = lc + lsg
    lf = jnp.maximum(la, b) + jnp.log(1.0 + jnp.exp(-jnp.abs(la - b)))
    f = jnp.exp(lf)
    sq = _sigmoid(qh)
    g = _xdot_left(tri, lf)
    gl = _xdot_left(blk, lf)
    half = HG_BLK // 2
    mid = jnp.where(c == ((r >> HG_SHIFT) << HG_SHIFT) + (half - 1), 1.0, 0.0).astype(BF16)
    gm = _xdot_left(mid, g)
    first = (_iota((t, LANES), 0) & (HG_BLK - 1)) < half
    k = 1.0 - f
    qs = qh * sq
    eq = jnp.where(first, 0.0, jnp.exp(jnp.minimum(g - gm, 0.0)))
    ek = jnp.where(first, jnp.exp(jnp.minimum(gm - g, 0.0)), 0.0)
    return dict(lf=lf, b=b, f=f, k=k, sq=sq, qs=qs, g=g, eg=jnp.exp(g), egl=jnp.exp(gl - g), dec=jnp.exp(gl),
                eq=eq, ek=ek, qx=(qs * eq).astype(BF16), kx=(k * ek).astype(BF16), same=same)


def _hgrn_fwd(proj, la, lc, *, name):
    s = proj.shape[0]
    t = _rows(s, HG_TILE)
    nt, nb = s // t, t // HG_BLK

    def body(q_ref, f_ref, i_ref, la_ref, lc_ref, o_ref, st_ref, state, osc):
        @pl.when(pl.program_id(1) == 0)
        def _():
            state[...] = jnp.zeros_like(state)

        v = i_ref[...]
        a = _hg_inputs(q_ref[...], f_ref[...], v, la_ref[...], lc_ref[...], t)
        qs, k, g = a["qs"], a["k"], a["g"]
        vb = v.astype(BF16)
        rb = _iota((t, LANES), 0) & (HG_BLK // 2 - 1)
        o = jnp.sum(qs * k, axis=1, keepdims=True) * v
        e = None
        for d in range(1, HG_BLK // 2):
            m = rb >= d
            fr = a["f"] if d == 1 else pltpu.roll(a["f"], d - 1, 0)
            e = fr if e is None else e * fr
            cd = jnp.sum(qs * pltpu.roll(k, d, 0) * e, axis=1, keepdims=True)
            o = o + jnp.where(m, cd, 0.0) * pltpu.roll(v, d, 0)
        cross = jnp.where(a["same"], _dot(a["qx"], a["kx"], NT), 0.0)
        osc[...] = o + _dot(cross.astype(BF16), vb)
        qt = (qs * a["eg"]).astype(BF16)
        kt = (k * a["egl"]).astype(BF16)
        for blk in range(nb):
            sl = slice(blk * HG_BLK, (blk + 1) * HG_BLK)
            st = state[...]
            stb = st.astype(BF16)
            st_ref[blk * HG_D:(blk + 1) * HG_D, :] = stb
            osc[sl, :] += _dot(qt[sl], stb, NT)
            state[...] = a["dec"][blk * HG_BLK:blk * HG_BLK + 1] * st + _dot(vb[sl], kt[sl], TN)
        o_ref[...] = osc[...]

    col = lambda c0: pl.BlockSpec((t, HG_D), lambda hd, i, c0=c0: (i, c0 + hd))
    vec = pl.BlockSpec((1, HG_D), lambda hd, i: (0, hd))
    return _pcall(
        body, name=name, grid=(4, nt), in_specs=[col(HG_Q0), col(HG_F0), col(HG_I0), vec, vec],
        out_specs=[col(0), pl.BlockSpec((None, nb * HG_D, HG_D), lambda hd, i: (hd, i, 0))],
        out_shape=[jax.ShapeDtypeStruct((s, HG_W), F32), jax.ShapeDtypeStruct((4, s // HG_BLK * HG_D, HG_D), BF16)],
        scratch_shapes=[pltpu.VMEM((HG_D, HG_D), F32), pltpu.VMEM((t, HG_D), F32)],
        compiler_params=_cparams("arbitrary", "arbitrary"))(proj, proj, proj, la, lc)


def _hgrn_bwd(proj, la, lc, st, doh, *, name):
    s = proj.shape[0]
    t = _rows(s, HG_TILE)
    nt, nb = s // t, t // HG_BLK

    def body(q_ref, f_ref, i_ref, la_ref, lc_ref, st_ref, do_ref, dq_ref, df_ref, di_ref, dla_ref, dlc_ref, dstate):
        @pl.when(pl.program_id(1) == 0)
        def _():
            dstate[...] = jnp.zeros_like(dstate)
            dla_ref[...] = jnp.zeros_like(dla_ref)
            dlc_ref[...] = jnp.zeros_like(dlc_ref)

        qh, z, v, do = q_ref[...], f_ref[...], i_ref[...], do_ref[...]
        la = la_ref[...]
        a = _hg_inputs(qh, z, v, la, lc_ref[...], t)
        qs, k, g = a["qs"], a["k"], a["g"]
        vb = v.astype(BF16)
        dob = do.astype(BF16)
        rb = _iota((t, LANES), 0) & (HG_BLK // 2 - 1)
        dc0 = jnp.sum(do * v, axis=1, keepdims=True)
        dq = dc0 * k
        dk = dc0 * qs
        dv = jnp.sum(qs * k, axis=1, keepdims=True) * do
        e = None
        for d in range(1, HG_BLK // 2):
            m = rb >= d
            fr = a["f"] if d == 1 else pltpu.roll(a["f"], d - 1, 0)
            e = fr if e is None else e * fr
            ks = pltpu.roll(k, d, 0)
            qe = qs * e
            cd = jnp.where(m, jnp.sum(qe * ks, axis=1, keepdims=True), 0.0)
            dcd = jnp.where(m, jnp.sum(do * pltpu.roll(v, d, 0), axis=1, keepdims=True), 0.0)
            dq = dq + dcd * ks * e
            dk = dk + pltpu.roll(dcd * qe, t - d, 0)
            dv = dv + pltpu.roll(cd * do, t - d, 0)
        cross = jnp.where(a["same"], _dot(a["qx"], a["kx"], NT), 0.0).astype(BF16)
        dcross = jnp.where(a["same"], _dot(dob, vb, NT), 0.0).astype(BF16)
        dq = dq + _dot(dcross, a["kx"]) * a["eq"]
        dk = dk + _dot(dcross, a["qx"], TN) * a["ek"]
        dv = dv + _dot(cross, dob, TN)
        qt = (qs * a["eg"]).astype(BF16)
        kt = (k * a["egl"]).astype(BF16)
        dqt, dkt, dvi, dgs = [None] * nb, [None] * nb, [None] * nb, [None] * nb
        for blk in reversed(range(nb)):
            sl = slice(blk * HG_BLK, (blk + 1) * HG_BLK)
            stb = st_ref[blk * HG_D:(blk + 1) * HG_D, :]
            ds1 = dstate[...]
            ds1b = ds1.astype(BF16)
            dec = a["dec"][blk * HG_BLK:blk * HG_BLK + 1]
            dqt[blk] = _dot(dob[sl], stb)
            dkt[blk] = _dot(vb[sl], ds1b)
            dvi[blk] = _dot(kt[sl], ds1b, NT)
            dgs[blk] = jnp.broadcast_to(jnp.sum(ds1 * stb.astype(F32), axis=0, keepdims=True) * dec, (HG_BLK, HG_D))
            dstate[...] = dec * ds1 + _dot(dob[sl], qt[sl], TN)
        dki = jnp.concatenate(dkt, axis=0) * a["egl"]
        dq = dq + jnp.concatenate(dqt, axis=0) * a["eg"]
        dk = dk + dki
        dv = dv + jnp.concatenate(dvi, axis=0)
        r = _iota((t, t), 0)
        c = _iota((t, t), 1)
        same = (r >> HG_SHIFT) == (c >> HG_SHIFT)
        later = jnp.where(same & (c >= r), 1.0, 0.0).astype(BF16)
        whole = jnp.where(same, 1.0, 0.0).astype(BF16)
        dlf = (_xdot_left(later, qs * dq - k * dk) + _xdot_left(whole, k * dki) + jnp.concatenate(dgs, axis=0)
               - a["f"] * dk)
        wb = jnp.exp(a["b"] - a["lf"])
        wa = jnp.exp(la - a["lf"])
        sq = a["sq"]
        dq_ref[...] = dq * (sq * (1.0 + qh * (1.0 - sq)))
        df_ref[...] = dlf * wb * (1.0 - _sigmoid(z))
        di_ref[...] = dv
        dla_ref[...] += jnp.sum(dlf * wa, axis=0, keepdims=True)
        dlc_ref[...] += jnp.sum(dlf * wb, axis=0, keepdims=True)

    col = lambda c0: pl.BlockSpec((t, HG_D), lambda hd, i, c0=c0: (nt - 1 - i, c0 + hd))
    vec = pl.BlockSpec((1, HG_D), lambda hd, i: (0, hd))
    return _pcall(
        body, name=name, grid=(4, nt),
        in_specs=[col(HG_Q0), col(HG_F0), col(HG_I0), vec, vec,
                  pl.BlockSpec((None, nb * HG_D, HG_D), lambda hd, i: (hd, nt - 1 - i, 0)), col(0)],
        out_specs=[col(0), col(0), col(0), vec, vec],
        out_shape=[jax.ShapeDtypeStruct((s, HG_W), F32)] * 3 + [jax.ShapeDtypeStruct((1, HG_W), F32)] * 2,
        scratch_shapes=[pltpu.VMEM((HG_D, HG_D), F32)],
        compiler_params=_cparams("arbitrary", "arbitrary"))(proj, proj, proj, la, lc, st, doh)


GH0 = (IN_W - HG_W) // HG_W


def _mix_out(o_a, nums, dens, mxs, oh, proj, hg, *, name):
    s = o_a.shape[0]
    t = _rows(s, TE)

    def body(oa_ref, n0, n1, n2, d0, d1, d2, m0, m1, m2, oh_ref, gh_ref, hg_ref, y_ref, od_ref, mall_ref, z_ref):
        y_ref[:, :SB_W] = oa_ref[...].astype(BF16)
        m = jnp.maximum(jnp.maximum(m0[...], m1[...]), m2[...])
        num = jnp.zeros((t, DIL_W), F32)
        z = jnp.zeros((t, DIL_W), F32)
        for n_ref, d_ref, m_ref in ((n0, d0, m0), (n1, d1, m1), (n2, d2, m2)):
            sc = jnp.exp(m_ref[...] - m)
            num = num + n_ref[...] * sc
            z = z + d_ref[...] * sc
        od = num / z
        od_ref[...] = od
        mall_ref[...] = m
        z_ref[...] = z
        y_ref[:, SB_W:SB_W + DIL_W] = od.astype(BF16)
        for h in range(4):
            sl = slice(h * HG_D, (h + 1) * HG_D)
            ov = oh_ref[:, sl]
            g = gh_ref[:, sl]
            r = lax.rsqrt(jnp.mean(ov * ov, axis=1, keepdims=True) + EPS)
            y_ref[:, SB_W + DIL_W + h * HG_D:SB_W + DIL_W + (h + 1) * HG_D] = (ov * r * hg_ref[...] * (g * _sigmoid(g))).astype(BF16)

    rd = _row_spec(t, DIL_W)
    return _pcall(
        body, name=name, grid=(s // t,),
        in_specs=[rd] * 10 + [_row_spec(t, HG_W), _row_spec(t, HG_W, GH0), _vec_spec(HG_D)],
        out_specs=[_row_spec(t, MIX_W), rd, rd, rd],
        out_shape=[jax.ShapeDtypeStruct((s, MIX_W), BF16)] + [jax.ShapeDtypeStruct((s, DIL_W), F32)] * 3,
        compiler_params=_cparams("parallel"))(o_a, *nums, *dens, *mxs, oh, proj, hg)


def _mix_out_bwd(dmix, oh, proj, hg, od, *, name):
    s = oh.shape[0]
    t = _rows(s, TE)

    def body(dm_ref, oh_ref, gh_ref, hg_ref, od_ref, doh_ref, dgh_ref, dl_ref, dhg_ref):
        @pl.when(pl.program_id(0) == 0)
        def _():
            dhg_ref[...] = jnp.zeros_like(dhg_ref)

        seg = _seg_consts()
        for j in range(2):
            sl = slice(j * LANES, (j + 1) * LANES)
            dl_ref[:, sl] = _xdot(dm_ref[:, SB_W + j * LANES:SB_W + (j + 1) * LANES] * od_ref[:, sl], seg, 3)
        hgv = hg_ref[...]
        for h in range(4):
            sl = slice(h * HG_D, (h + 1) * HG_D)
            dy = dm_ref[:, SB_W + DIL_W + h * HG_D:SB_W + DIL_W + (h + 1) * HG_D]
            ov = oh_ref[:, sl]
            g = gh_ref[:, sl]
            sg = _sigmoid(g)
            silu = g * sg
            r = lax.rsqrt(jnp.mean(ov * ov, axis=1, keepdims=True) + EPS)
            nrm = ov * r
            dhg_ref[...] += jnp.sum(dy * nrm * silu, axis=0, keepdims=True)
            dgh_ref[:, sl] = dy * nrm * hgv * (sg * (1.0 + g * (1.0 - sg)))
            dn = dy * hgv * silu
            doh_ref[:, sl] = r * (dn - nrm * jnp.mean(dn * nrm, axis=1, keepdims=True))

    rh = _row_spec(t, HG_W)
    return _pcall(
        body, name=name, grid=(s // t,),
        in_specs=[_row_spec(t, MIX_W), rh, _row_spec(t, HG_W, GH0), _vec_spec(HG_D), _row_spec(t, DIL_W)],
        out_specs=[rh, rh, _row_spec(t, DIL_W), _vec_spec(HG_D)],
        out_shape=[jax.ShapeDtypeStruct((s, HG_W), F32)] * 2 + [jax.ShapeDtypeStruct((s, DIL_W), F32), jax.ShapeDtypeStruct((1, HG_D), F32)],
        compiler_params=_cparams("arbitrary"))(dmix, oh, proj, hg, od)


def _lb_terms(l):
    l0, l1 = l[0:1], l[1:2]
    m = jnp.maximum(l0, l1)
    e0, e1 = jnp.exp(l0 - m), jnp.exp(l1 - m)
    s0, s1 = e0 / (e0 + e1), e1 / (e0 + e1)
    args = (s0 - s0, (s0 + s1) - s0)
    lbs = tuple(jnp.minimum(jnp.maximum(a, 0.0), 1.0 - EPS) for a in args)
    return s0, s1, args, lbs


def _lb_prep(logits, *, name):
    def body(l_ref, lb_ref, la_ref, lc_ref):
        _, _, _, lbs = _lb_terms(l_ref[...])
        lb = jnp.concatenate(lbs, axis=0)
        lb_ref[...] = lb
        la_ref[...] = jnp.log(jnp.maximum(lb, LB_FLOOR))
        lc_ref[...] = jnp.log1p(-lb)

    return _pcall(body, name=name, out_shape=[jax.ShapeDtypeStruct(logits.shape, F32)] * 3)(logits)


def _lb_bwd(logits, dla, dlc, *, name):
    def half(hi, eq):
        return jnp.where(hi, 1.0, jnp.where(eq, 0.5, 0.0))

    def body(l_ref, dla_ref, dlc_ref, o_ref):
        s0, s1, args, lbs = _lb_terms(l_ref[...])
        da = []
        for i in range(2):
            a, lb = args[i], lbs[i]
            dlb = dla_ref[i:i + 1] * half(lb > LB_FLOOR, lb == LB_FLOOR) / jnp.maximum(lb, LB_FLOOR) - dlc_ref[i:i + 1] / (1.0 - lb)
            t = jnp.maximum(a, 0.0)
            da.append(dlb * half(a > 0.0, a == 0.0) * half(t < 1.0 - EPS, t == 1.0 - EPS))
        ds0 = (da[0] + da[1]) - (da[0] + da[1])
        ds1 = da[1]
        dot = s0 * ds0 + s1 * ds1
        o_ref[...] = jnp.concatenate([s0 * (ds0 - dot), s1 * (ds1 - dot)], axis=0)

    return _pcall(body, name=name, out_shape=jax.ShapeDtypeStruct(logits.shape, F32))(logits, dla, dlc)


def _mod_fwd(c8, w, b, *, name):
    _, d, n = w.shape
    tn = _tile(n, 768)

    def body(c_ref, w_ref, b_ref, o_ref):
        cv = c_ref[...]
        o_ref[...] = _dot((cv * _sigmoid(cv)).astype(BF16), w_ref[...].astype(BF16)) + b_ref[...]

    return _pcall(
        body, name=name, grid=(2, n // tn),
        in_specs=[pl.BlockSpec((8, d), lambda l, j: (0, 0)), pl.BlockSpec((None, d, tn), lambda l, j: (l, 0, j)),
                  pl.BlockSpec((None, 1, tn), lambda l, j: (l, 0, j))],
        out_specs=pl.BlockSpec((None, 8, tn), lambda l, j: (l, 0, j)),
        out_shape=jax.ShapeDtypeStruct((2, 8, n), F32), compiler_params=_cparams("parallel", "parallel"))(c8, w, b)


def _mod_bwd(ct, dm, *, name):
    d = ct.shape[0]
    n = dm.shape[2]
    tn = _tile(n, 768)

    def body(c_ref, dm_ref, o_ref):
        cv = c_ref[...]
        sc = cv * _sigmoid(cv)
        dv = dm_ref[...]
        acc = sc[:, 0:1] * dv[0:1, :]
        for b in range(1, 8):
            acc = acc + sc[:, b:b + 1] * dv[b:b + 1, :]
        o_ref[...] = acc

    return _pcall(
        body, name=name, grid=(2, n // tn),
        in_specs=[pl.BlockSpec((d, 8), lambda l, j: (0, 0)), pl.BlockSpec((None, 8, tn), lambda l, j: (l, 0, j))],
        out_specs=pl.BlockSpec((None, d, tn), lambda l, j: (l, 0, j)),
        out_shape=jax.ShapeDtypeStruct((2, d, n), F32), compiler_params=_cparams("parallel", "parallel"))(ct, dm)


_PEERS = {
    "chips": ((1, 0, 0), (0, 1, 0), (1, 1, 0)),
    "all": tuple((a, b, c) for a in (0, 1) for b in (0, 1) for c in (0, 1) if a + b + c),
    "sib": ((0, 0, 1),),
}
_SLOTS = {"chips": 4, "all": 8, "sib": 2}


def _slot(kind, x, y, c):
    return {"chips": 2 * x + y, "all": 4 * x + 2 * y + c, "sib": c}[kind]


def _exchange(arrs, kind, scatter, *, name):
    n = len(arrs)
    peers = _PEERS[kind]
    ns = _SLOTS[kind]
    np_ = len(peers)

    def pieces(a):
        shape = a.shape[1:] if scatter else a.shape
        if len(shape) == 2:
            for k in (XCHG_CHUNKS, XCHG_CHUNKS // 2, XCHG_CHUNKS // 4):
                if k > 1 and shape[0] % (16 * k) == 0 and shape[0] * shape[1] * a.dtype.itemsize >= k * XCHG_MIN_BYTES:
                    return [(i * (shape[0] // k), shape[0] // k) for i in range(k)]
        return [None]

    chunks = [pieces(a) for a in arrs]
    base = [sum(len(c) for c in chunks[:a]) * np_ for a in range(n)]
    total = sum(len(c) for c in chunks) * np_

    def body(*refs):
        ins, outs = refs[:n], refs[n:2 * n]
        send, recv, loc = refs[2 * n:]
        x, y, c = lax.axis_index("x"), lax.axis_index("y"), lax.axis_index("c")
        me = _slot(kind, x, y, c)
        copies = []
        for a in range(n):
            own = pltpu.make_async_copy(ins[a].at[me] if scatter else ins[a], outs[a].at[me], loc.at[a])
            own.start()
            copies.append(own)
            for j, (dx, dy, dc) in enumerate(peers):
                px, py, pc = (1 - x if dx else x), (1 - y if dy else y), (1 - c if dc else c)
                src = ins[a].at[_slot(kind, px, py, pc)] if scatter else ins[a]
                for i, piece in enumerate(chunks[a]):
                    rows = slice(None) if piece is None else pl.ds(piece[0], piece[1])
                    sem = base[a] + j * len(chunks[a]) + i
                    cp = pltpu.make_async_remote_copy(
                        src_ref=src if piece is None else src.at[rows], dst_ref=outs[a].at[me] if piece is None else outs[a].at[me, rows],
                        send_sem=send.at[sem], recv_sem=recv.at[sem], device_id=(px, py, pc), device_id_type=MESH_ID)
                    cp.start()
                    copies.append(cp)
        for cp in copies:
            cp.wait()

    hbm = pl.BlockSpec(memory_space=pl.ANY)
    shapes = [jax.ShapeDtypeStruct(a.shape if scatter else (ns,) + a.shape, a.dtype) for a in arrs]
    return _pcall(
        body, name=name, in_specs=[hbm] * n, out_specs=[hbm] * n, out_shape=shapes,
        scratch_shapes=[pltpu.SemaphoreType.DMA((total,)), pltpu.SemaphoreType.DMA((total,)), pltpu.SemaphoreType.DMA((n,))],
    )(*arrs)


def _sum_slots(a, *, name, out_dtype=F32):
    ns, r, c = a.shape
    t = _tile(r, max(16, (1 << 18) // c // 16 * 16), 16)

    def body(a_ref, o_ref):
        acc = a_ref[0].astype(F32)
        for i in range(1, ns):
            acc = acc + a_ref[i].astype(F32)
        o_ref[...] = acc.astype(o_ref.dtype)

    return _pcall(body, name=name, grid=(r // t,), in_specs=[pl.BlockSpec((ns, t, c), lambda i: (0, i, 0))],
                  out_specs=pl.BlockSpec((t, c), lambda i: (i, 0)), out_shape=jax.ShapeDtypeStruct((r, c), out_dtype),
                  compiler_params=_cparams("parallel"))(a)


def _adamw(w, gparts, m, v, *, name):
    r, c = w.shape
    t = _tile(r, max(16, (1 << 17) // c // 16 * 16), 16)
    ng = len(gparts)

    def body(*refs):
        w_ref, m_ref, v_ref = refs[0], refs[1 + ng], refs[2 + ng]
        g_ref, d_ref, nm_ref, nv_ref = refs[3 + ng:]
        g = refs[1][...].astype(F32)
        for i in range(1, ng):
            g = g + refs[1 + i][...].astype(F32)
        mn = ADAM_B1 * m_ref[...] + (1.0 - ADAM_B1) * g
        vn = ADAM_B2 * v_ref[...] + (1.0 - ADAM_B2) * (g * g)
        m_hat = mn / (1.0 - ADAM_B1 ** ADAM_STEP)
        v_hat = vn / (1.0 - ADAM_B2 ** ADAM_STEP)
        g_ref[...] = g
        d_ref[...] = -ADAM_LR * (m_hat / (jnp.sqrt(v_hat) + ADAM_EPS) + ADAM_WD * w_ref[...])
        nm_ref[...] = mn
        nv_ref[...] = vn

    spec = pl.BlockSpec((t, c), lambda i: (i, 0))
    return _pcall(body, name=name, grid=(r // t,), in_specs=[spec] * (3 + ng), out_specs=[spec] * 4,
                  out_shape=[jax.ShapeDtypeStruct((r, c), F32)] * 4, compiler_params=_cparams("parallel"))(w, *gparts, m, v)


FFN_TM = 512
FFN_CHUNK = 1408


def _resident(shape):
    return pl.BlockSpec(shape, lambda i: (0,) * len(shape), pipeline_mode=pl.Buffered(1))


def _ffn_up(x, sc, sh, wgu, *, name):
    s, d = x.shape
    f = wgu.shape[1] // 2
    t, fc = _rows(s, FFN_TM), _tile(f, FFN_CHUNK)

    def body(x_ref, sc_ref, sh_ref, w_ref, h_ref, uv_ref, a_ref):
        xv = x_ref[...]
        r = lax.rsqrt(jnp.mean(xv * xv, axis=1, keepdims=True) + EPS)
        hb = ((xv * r) * (1.0 + sc_ref[...]) + sh_ref[...]).astype(BF16)
        h_ref[...] = hb
        for j in range(f // fc):
            u = _dot(hb, w_ref[:, j * fc:(j + 1) * fc])
            v = _dot(hb, w_ref[:, f + j * fc:f + (j + 1) * fc])
            uv_ref[:, j * fc:(j + 1) * fc] = u.astype(BF16)
            uv_ref[:, f + j * fc:f + (j + 1) * fc] = v.astype(BF16)
            a_ref[:, j * fc:(j + 1) * fc] = (u * _sigmoid(u) * v).astype(BF16)

    return _pcall(
        body, name=name, grid=(s // t,), in_specs=[_row_spec(t, d), _vec_spec(d), _vec_spec(d), _resident(wgu.shape)],
        out_specs=[_row_spec(t, d), _row_spec(t, 2 * f), _row_spec(t, f)],
        out_shape=[jax.ShapeDtypeStruct((s, d), BF16), jax.ShapeDtypeStruct((s, 2 * f), BF16), jax.ShapeDtypeStruct((s, f), BF16)],
        compiler_params=_cparams("parallel"))(x, sc, sh, wgu)


def _norm_mm(x, sc, sh, w, nb, *, name):
    s, d = x.shape
    n = w.shape[1]
    t, nc = _rows(s, FFN_TM), _tile(n, 1792)
    assert nb <= nc

    def body(x_ref, sc_ref, sh_ref, w_ref, h_ref, o_ref, ob_ref):
        xv = x_ref[...]
        r = lax.rsqrt(jnp.mean(xv * xv, axis=1, keepdims=True) + EPS)
        hb = ((xv * r) * (1.0 + sc_ref[...]) + sh_ref[...]).astype(BF16)
        h_ref[...] = hb
        for j in range(n // nc):
            part = _dot(hb, w_ref[:, j * nc:(j + 1) * nc])
            o_ref[:, j * nc:(j + 1) * nc] = part
            if j == 0:
                ob_ref[...] = part[:, :nb].astype(BF16)

    return _pcall(
        body, name=name, grid=(s // t,), in_specs=[_row_spec(t, d), _vec_spec(d), _vec_spec(d), _resident(w.shape)],
        out_specs=[_row_spec(t, d), _row_spec(t, n), _row_spec(t, nb)],
        out_shape=[jax.ShapeDtypeStruct((s, d), BF16), jax.ShapeDtypeStruct((s, n), F32), jax.ShapeDtypeStruct((s, nb), BF16)],
        compiler_params=_cparams("parallel"))(x, sc, sh, w)


def _ffn_dact(dxo, y, sg, wd, uv, *, name):
    s, d = y.shape
    f = wd.shape[0]
    t, fc = _rows(s, FFN_TM), _tile(f, FFN_CHUNK)

    def body(dxo_ref, y_ref, sg_ref, w_ref, uv_ref, dy_ref, duv_ref, ds_ref):
        @pl.when(pl.program_id(0) == 0)
        def _():
            ds_ref[...] = jnp.zeros_like(ds_ref)

        dv = dxo_ref[...]
        dyb = (sg_ref[...] * dv).astype(BF16)
        dy_ref[...] = dyb
        ds_ref[...] += jnp.sum(dv * y_ref[...], axis=0, keepdims=True)
        for j in range(f // fc):
            da = _dot(dyb, w_ref[j * fc:(j + 1) * fc, :], NT)
            u = uv_ref[:, j * fc:(j + 1) * fc].astype(F32)
            v = uv_ref[:, f + j * fc:f + (j + 1) * fc].astype(F32)
            sgm = _sigmoid(u)
            duv_ref[:, j * fc:(j + 1) * fc] = (da * v * (sgm * (1.0 + u * (1.0 - sgm)))).astype(BF16)
            duv_ref[:, f + j * fc:f + (j + 1) * fc] = (da * (u * sgm)).astype(BF16)

    return _pcall(
        body, name=name, grid=(s // t,),
        in_specs=[_row_spec(t, d), _row_spec(t, d), _vec_spec(d), _resident(wd.shape), _row_spec(t, 2 * f)],
        out_specs=[_row_spec(t, d), _row_spec(t, 2 * f), _vec_spec(d)],
        out_shape=[jax.ShapeDtypeStruct((s, d), BF16), jax.ShapeDtypeStruct((s, 2 * f), BF16), jax.ShapeDtypeStruct((1, d), F32)],
        compiler_params=_cparams("arbitrary"))(dxo, y, sg, wd, uv)


def _ffn_dh(duv, wgu, x, sc, dxo, *, name):
    s, d = x.shape
    f2 = wgu.shape[1]
    t = _rows(s, FFN_TM)

    def body(duv_ref, w_ref, x_ref, sc_ref, dxo_ref, dx_ref, dsc_ref, dsh_ref):
        @pl.when(pl.program_id(0) == 0)
        def _():
            dsc_ref[...] = jnp.zeros_like(dsc_ref)
            dsh_ref[...] = jnp.zeros_like(dsh_ref)

        dhv = _dot(duv_ref[...], w_ref[...], NT)
        xv = x_ref[...]
        r = lax.rsqrt(jnp.mean(xv * xv, axis=1, keepdims=True) + EPS)
        xn = xv * r
        dxn = dhv * (1.0 + sc_ref[...])
        dx_ref[...] = dxo_ref[...] + r * (dxn - xn * jnp.mean(dxn * xn, axis=1, keepdims=True))
        dsc_ref[...] += jnp.sum(dhv * xn, axis=0, keepdims=True)
        dsh_ref[...] += jnp.sum(dhv, axis=0, keepdims=True)

    return _pcall(
        body, name=name, grid=(s // t,),
        in_specs=[_row_spec(t, f2), _resident(wgu.shape), _row_spec(t, d), _vec_spec(d), _row_spec(t, d)],
        out_specs=[_row_spec(t, d), _vec_spec(d), _vec_spec(d)],
        out_shape=[jax.ShapeDtypeStruct((s, d), F32), jax.ShapeDtypeStruct((1, d), F32), jax.ShapeDtypeStruct((1, d), F32)],
        compiler_params=_cparams("arbitrary"))(duv, wgu, x, sc, dxo)


def _dh_pieces(pieces, w, x, sc, dxo, *, name):
    s, d = x.shape
    t = _rows(s, FFN_TM)
    widths = [p.shape[1] for p in pieces]
    offs = [sum(widths[:i]) for i in range(len(widths))]
    kd = sum(widths)
    npc = len(pieces)

    def body(*refs):
        p_refs = refs[:npc]
        w_ref, x_ref, sc_ref, dxo_ref, dx_ref, dsc_ref, dsh_ref, cat_ref = refs[npc:]

        @pl.when(pl.program_id(0) == 0)
        def _():
            dsc_ref[...] = jnp.zeros_like(dsc_ref)
            dsh_ref[...] = jnp.zeros_like(dsh_ref)

        dhv = None
        for p_ref, off, wd in zip(p_refs, offs, widths):
            pb = p_ref[...].astype(BF16)
            cat_ref[:, off:off + wd] = pb
            part = _dot(pb, w_ref[:, off:off + wd], NT)
            dhv = part if dhv is None else dhv + part
        xv = x_ref[...]
        r = lax.rsqrt(jnp.mean(xv * xv, axis=1, keepdims=True) + EPS)
        xn = xv * r
        dxn = dhv * (1.0 + sc_ref[...])
        dx_ref[...] = dxo_ref[...] + r * (dxn - xn * jnp.mean(dxn * xn, axis=1, keepdims=True))
        dsc_ref[...] += jnp.sum(dhv * xn, axis=0, keepdims=True)
        dsh_ref[...] += jnp.sum(dhv, axis=0, keepdims=True)

    return _pcall(
        body, name=name, grid=(s // t,),
        in_specs=[_row_spec(t, wd) for wd in widths] + [_resident(w.shape), _row_spec(t, d), _vec_spec(d), _row_spec(t, d)],
        out_specs=[_row_spec(t, d), _vec_spec(d), _vec_spec(d), _row_spec(t, kd)],
        out_shape=[jax.ShapeDtypeStruct((s, d), F32), jax.ShapeDtypeStruct((1, d), F32), jax.ShapeDtypeStruct((1, d), F32),
                   jax.ShapeDtypeStruct((s, kd), BF16)],
        compiler_params=_cparams("arbitrary"))(*pieces, w, x, sc, dxo)


def _ffn_fwd(x, sh, sc, g, wgu, wd, tag):
    h, uv, a = _ffn_up(x, sc, sh, wgu, name=f"{tag}_up")
    y, xo = _mm(a, wd, name=f"{tag}_down", tm=512, tn=1024, tk=wd.shape[0], res=x, scale=0.5 * g)
    return xo, (x, h, uv, a, y)


def _ffn_bwd(dxo, saved, sc, g, wgu, wd, tag):
    x, h, uv, a, y = saved
    dyb, duv, dgs = _ffn_dact(dxo, y, 0.5 * g, wd, uv, name=f"{tag}_dact")
    dx, dsc, dsh = _ffn_dh(duv, wgu, x, sc, dxo, name=f"{tag}_dh")
    dwgu = _mm_tn(h, duv, name=f"{tag}_dwgu", tm=1024, tn=1408, tk=512)
    dwd = _mm_tn(a, dyb, name=f"{tag}_dwd", tm=1408, tn=1024, tk=512)
    return dx, dwgu, dwd, dsh, dsc, 0.5 * dgs


def _layer_fwd(x0, mod, w, par, tag):
    s = x0.shape[0]
    sh1, sc1, g1, sh2, sc2, g2, sh3, sc3, g3 = (mod[i:i + 1] for i in range(N_MOD))
    x1, f1 = _ffn_fwd(x0, sh1, sc1, g1, w["gu1"], w["d1"], f"{tag}_ffn1")
    h2, proj, qkv = _norm_mm(x1, sc2, sh2, w["in"], 3 * SB_W + 3 * DIL_W, name=f"{tag}_in")
    vd = qkv[:, 3 * SB_W + 2 * DIL_W:]
    o_a = _sb_fwd(qkv, name=f"{tag}_sb")
    qd, kd = _dil_prep(proj, par["gq"], par["gk"], par["cos"], par["sin"], name=f"{tag}_dil_prep")
    nums, dens, mxs = [], [], []
    for _, r in DIL_PATTERNS:
        nu, de, mx = _dil_fwd(qd, kd, vd, r, name=f"{tag}_dil{r}")
        nums.append(nu.reshape(s, DIL_W))
        dens.append(de.reshape(s, DIL_W))
        mxs.append(mx.reshape(s, DIL_W))
    oh, st = _hgrn_fwd(proj, par["la"], par["lc"], name=f"{tag}_hgrn")
    ymix, od, mall, zall = _mix_out(o_a, nums, dens, mxs, oh, proj, par["hg"], name=f"{tag}_mix_out")
    out, x2 = _mm(ymix, w["out"], name=f"{tag}_out", tm=512, tn=1024, tk=1024, res=x1, scale=g2)
    x3, f2 = _ffn_fwd(x2, sh3, sc3, g3, w["gu2"], w["d2"], f"{tag}_ffn2")
    return x3, dict(f1=f1, f2=f2, x1=x1, h2=h2, proj=proj, qkv=qkv, vd=vd, o_a=o_a, qd=qd, kd=kd, oh=oh, st=st,
                    ymix=ymix, od=od, mall=mall, zall=zall, out=out)


def _layer_bwd(dx3, sv, mod, w, par, tag):
    s = dx3.shape[0]
    sh1, sc1, g1, sh2, sc2, g2, sh3, sc3, g3 = (mod[i:i + 1] for i in range(N_MOD))
    dx2, dwgu2, dwd2, dsh3, dsc3, dg3 = _ffn_bwd(dx3, sv["f2"], sc3, g3, w["gu2"], w["d2"], f"{tag}_ffn2")
    doutb, dg2 = _gate_bwd(dx2, sv["out"], g2, name=f"{tag}_dgate2")
    dmix = _mm(doutb, w["out"], name=f"{tag}_dmix", tb=True, tm=512, tn=1024, tk=1024)
    dwout = _mm_tn(sv["ymix"], doutb, name=f"{tag}_dwout", tm=1024, tn=1024, tk=512)
    proj = sv["proj"]
    doh, dgh, delta, dhg = _mix_out_bwd(dmix, sv["oh"], proj, par["hg"], sv["od"], name=f"{tag}_dmix_out")
    dqa, dka, dva = _sb_bwd(sv["qkv"], sv["o_a"], dmix, name=f"{tag}_dsb")
    do_d = dmix[:, SB_W:SB_W + DIL_W]
    dqs, dks, dvs = [], [], []
    for _, r in DIL_PATTERNS:
        a, b, c = _dil_bwd(sv["qd"], sv["kd"], sv["vd"], do_d, sv["mall"], sv["zall"], delta, r, name=f"{tag}_ddil{r}")
        dqs.append(a.reshape(s, DIL_W))
        dks.append(b.reshape(s, DIL_W))
        dvs.append(c.reshape(s, DIL_W))
    dqd, dkd, dvd, dgq, dgk = _dil_prep_bwd(proj, par["gq"], par["gk"], par["cos"], par["sin"], dqs, dks, dvs,
                                             name=f"{tag}_ddil_prep")
    dqh, dfh, dih, dla, dlc = _hgrn_bwd(proj, par["la"], par["lc"], sv["st"], doh, name=f"{tag}_dhgrn")
    dx1, dsc2, dsh2, dproj = _dh_pieces([dqa, dka, dva, dqd, dkd, dvd, dqh, dfh, dih, dgh], w["in"], sv["x1"], sc2, dx2,
                                         name=f"{tag}_dh2")
    dwin = _mm_tn(sv["h2"], dproj, name=f"{tag}_dwin", tm=1024, tn=1792, tk=512)
    dx0, dwgu1, dwd1, dsh1, dsc1, dg1 = _ffn_bwd(dx1, sv["f1"], sc1, g1, w["gu1"], w["d1"], f"{tag}_ffn1")
    dmod = jnp.concatenate([dsh1, dsc1, dg1, dsh2, dsc2, dg2, dsh3, dsc3, dg3], axis=0)
    fold = lambda v: v.reshape(2, HEAD_DIM).sum(axis=0)
    grads = dict(gu1=dwgu1, d1=dwd1, gu2=dwgu2, d2=dwd2, win=dwin, wout=dwout, dmod=dmod, gq=fold(dgq), gk=fold(dgk),
                 hg=dhg[0], la=dla[0], lc=dlc[0])
    return dx0, grads


def _pack(pieces):
    flat = jnp.concatenate([p.reshape(-1) for p in pieces])
    pad = (-flat.shape[0]) % (8 * LANES)
    return jnp.pad(flat, (0, pad)).reshape(-1, LANES)


def _unpack(flat, like):
    out, off = [], 0
    for p in like:
        out.append(flat[off:off + p.size].reshape(p.shape))
        off += p.size
    return out


def kernel(x, c, w_mod, b_mod, ffn1_w_gate, ffn1_w_up, ffn1_w_down, w_in, w_out, q_norm_g, k_norm_g, hgrn_norm_g, hgrn_lb_logits, ffn2_w_gate, ffn2_w_up, ffn2_w_down, loss_target, m_w_mod, m_b_mod, m_ffn1_w_gate, m_ffn1_w_up, m_ffn1_w_down, m_w_in, m_w_out, m_q_norm_g, m_k_norm_g, m_hgrn_norm_g, m_hgrn_lb_logits, m_ffn2_w_gate, m_ffn2_w_up, m_ffn2_w_down, v_w_mod, v_b_mod, v_ffn1_w_gate, v_ffn1_w_up, v_ffn1_w_down, v_w_in, v_w_out, v_q_norm_g, v_k_norm_g, v_hgrn_norm_g, v_hgrn_lb_logits, v_ffn2_w_gate, v_ffn2_w_up, v_ffn2_w_down):
    names = ["w_mod", "b_mod", "ffn1_w_gate", "ffn1_w_up", "ffn1_w_down", "w_in", "w_out", "q_norm_g", "k_norm_g",
             "hgrn_norm_g", "hgrn_lb_logits", "ffn2_w_gate", "ffn2_w_up", "ffn2_w_down"]
    wts = dict(zip(names, (w_mod, b_mod, ffn1_w_gate, ffn1_w_up, ffn1_w_down, w_in, w_out, q_norm_g, k_norm_g, hgrn_norm_g,
                           hgrn_lb_logits, ffn2_w_gate, ffn2_w_up, ffn2_w_down)))
    mom = dict(zip(names, (m_w_mod, m_b_mod, m_ffn1_w_gate, m_ffn1_w_up, m_ffn1_w_down, m_w_in, m_w_out, m_q_norm_g, m_k_norm_g,
                           m_hgrn_norm_g, m_hgrn_lb_logits, m_ffn2_w_gate, m_ffn2_w_up, m_ffn2_w_down)))
    var = dict(zip(names, (v_w_mod, v_b_mod, v_ffn1_w_gate, v_ffn1_w_up, v_ffn1_w_down, v_w_in, v_w_out, v_q_norm_g, v_k_norm_g,
                           v_hgrn_norm_g, v_hgrn_lb_logits, v_ffn2_w_gate, v_ffn2_w_up, v_ffn2_w_down)))
    depth = w_mod.shape[0]
    assert depth == 2 and x.shape[0] == 1
    s, d = x.shape[1:]
    assert s % (DIL_PATTERNS[-1][1] * QBLK) == 0 and d % LANES == 0
    xi, yi, ci = lax.axis_index("x"), lax.axis_index("y"), lax.axis_index("c")
    chip = 2 * xi + yi
    dev = 2 * chip + ci
    x0, tgt = x[0], loss_target[0]

    c8 = _exchange([c.reshape(d // LANES, LANES)], "all", False, name="gather_c")[0].reshape(8, d)
    ncol = w_mod.shape[2]
    b_loc = lax.dynamic_slice_in_dim(b_mod, chip * ncol, ncol, axis=1)
    m_loc = _mod_fwd(c8, w_mod, b_loc.reshape(depth, 1, ncol), name="mod_fwd")
    m_all = _exchange([m_loc], "chips", False, name="gather_mod")[0]
    mod = jnp.transpose(lax.dynamic_index_in_dim(m_all, dev, axis=2, keepdims=False), (1, 0, 2)).reshape(depth, N_MOD, d)

    col_sharded = ["ffn1_w_gate", "ffn1_w_up", "w_in", "ffn2_w_gate", "ffn2_w_up"]
    row_sharded = ["ffn1_w_down", "w_out", "ffn2_w_down"]
    big = col_sharded + row_sharded
    flat = [wts[n].astype(BF16).reshape(-1, wts[n].shape[-1]) for n in big]
    gathered = {n: g.reshape((4,) + wts[n].shape) for n, g in zip(big, _exchange(flat, "chips", False, name="gather_w"))}
    full = {}
    for n in col_sharded:
        g = gathered[n]
        full[n] = jnp.moveaxis(g, 0, 2).reshape(depth, g.shape[2], -1)
    for n in row_sharded:
        g = gathered[n]
        full[n] = jnp.moveaxis(g, 0, 1).reshape(depth, -1, g.shape[3])
    ws = [dict(gu1=jnp.concatenate([full["ffn1_w_gate"][l], full["ffn1_w_up"][l]], axis=1), d1=full["ffn1_w_down"][l],
               gu2=jnp.concatenate([full["ffn2_w_gate"][l], full["ffn2_w_up"][l]], axis=1), d2=full["ffn2_w_down"][l],
               **{"in": full["w_in"][l], "out": full["w_out"][l]}) for l in range(depth)]

    _, la, lc = _lb_prep(hgrn_lb_logits, name="lb_prep")
    cos, sin = _rope_tables(s)
    pars = [dict(gq=jnp.tile(q_norm_g[l], 2)[None], gk=jnp.tile(k_norm_g[l], 2)[None], hg=hgrn_norm_g[l][None],
                 la=la[l:l + 1], lc=lc[l:l + 1], cos=cos, sin=sin) for l in range(depth)]

    xs, saved = x0, []
    for l in range(depth):
        xs, sv = _layer_fwd(xs, mod[l], ws[l], pars[l], f"l{l}")
        saved.append(sv)
    dx, lpart = _loss_grad(xs, tgt, name="loss")
    grads = [None] * depth
    for l in reversed(range(depth)):
        dx, grads[l] = _layer_bwd(dx, saved[l], mod[l], ws[l], pars[l], f"l{l}")

    stack = lambda k: jnp.stack([grads[l][k] for l in range(depth)])
    small = [stack("dmod"), stack("gq"), stack("gk"), stack("hg"), stack("la"), stack("lc"), lpart[0, :1]]
    packed = _pack(small)
    allp = _exchange([packed], "all", False, name="gather_small")[0]
    tot = _unpack(_sum_slots(allp, name="sum_small").reshape(-1), small)
    g_b_mod = tot[0].reshape(depth, N_MOD * d)
    loss = tot[6][0]
    g_small = {"b_mod": g_b_mod, "q_norm_g": tot[1], "k_norm_g": tot[2], "hgrn_norm_g": tot[3],
               "hgrn_lb_logits": _lb_bwd(hgrn_lb_logits, tot[4], tot[5], name="lb_bwd")}

    dm_all = allp.reshape(8, -1)[:, :depth * N_MOD * d].reshape(8, depth, N_MOD * d)
    dm_loc = jnp.transpose(lax.dynamic_slice_in_dim(dm_all, chip * ncol, ncol, axis=2), (1, 0, 2))
    g_w_mod = _mod_bwd(c8.T, dm_loc, name="mod_bwd")

    fgrad = {
        "ffn1_w_gate": jnp.stack([grads[l]["gu1"][:, :grads[l]["gu1"].shape[1] // 2] for l in range(depth)]),
        "ffn1_w_up": jnp.stack([grads[l]["gu1"][:, grads[l]["gu1"].shape[1] // 2:] for l in range(depth)]),
        "ffn2_w_gate": jnp.stack([grads[l]["gu2"][:, :grads[l]["gu2"].shape[1] // 2] for l in range(depth)]),
        "ffn2_w_up": jnp.stack([grads[l]["gu2"][:, grads[l]["gu2"].shape[1] // 2:] for l in range(depth)]),
        "w_in": stack("win"), "ffn1_w_down": stack("d1"), "ffn2_w_down": stack("d2"), "w_out": stack("wout"),
    }
    by_chip = []
    for n in big:
        g = fgrad[n]
        if n in col_sharded:
            g = jnp.moveaxis(g.reshape(depth, g.shape[1], 4, -1), 2, 0)
        else:
            g = jnp.moveaxis(g.reshape(depth, 4, -1, g.shape[2]), 1, 0)
        by_chip.append(g.reshape(4, -1, g.shape[-1]))
    got = _exchange(by_chip, "chips", True, name="scatter_grads")
    parts = [_sum_slots(g, name=f"sum_{n}", out_dtype=BF16) for n, g in zip(big, got)]
    both = dict(zip(big, _exchange(parts, "sib", False, name="swap_grads")))

    outs = {}
    for n in names:
        w2 = wts[n].reshape(-1, wts[n].shape[-1])
        if n in both:
            gp = [both[n][0], both[n][1]]
        elif n == "w_mod":
            gp = [g_w_mod.reshape(w2.shape)]
        else:
            gp = [g_small[n].reshape(w2.shape)]
        res = _adamw(w2, gp, mom[n].reshape(w2.shape), var[n].reshape(w2.shape), name=f"adamw_{n}")
        outs[n] = [r.reshape(wts[n].shape) for r in res]
    return (loss, dx[None], *[outs[n][0] for n in names], *[outs[n][1] for n in names], *[outs[n][2] for n in names],
            *[outs[n][3] for n in names])
```

```python
import functools
import math

import jax
import jax.numpy as jnp
from jax import lax
from jax.experimental import pallas as pl
from jax.experimental.pallas import tpu as pltpu

F32 = jnp.float32
BF16 = jnp.bfloat16
MESH_ID = pl.DeviceIdType.MESH

HEAD_DIM = 64
SB_W = 256
DIL_W = 256
HG_W = 512
HG_D = 128
IN_W = 3 * SB_W + 3 * DIL_W + 4 * HG_W
MIX_W = SB_W + DIL_W + HG_W
DIL_PATTERNS = ((128, 1), (512, 4), (2048, 16))
ROPE_THETA = 10000.0
EPS = 1e-6
LB_FLOOR = 1e-30
NEG_BIG = -1e30
N_MOD = 9
ADAM_LR = 0.001
ADAM_B1 = 0.9
ADAM_B2 = 0.999
ADAM_EPS = 1e-08
ADAM_WD = 0.01
ADAM_STEP = 10

LANES = 128
QBLK = 128
DIL_TILE = 512
HG_BLK = 16
HG_TILE = 256
SB_EXIT = 88.0
VMEM_LIMIT = 48 * 1024 * 1024
XCHG_CHUNKS = 8
XCHG_MIN_BYTES = 1 << 19

NN = (((1,), (0,)), ((), ()))
NT = (((1,), (1,)), ((), ()))
TN = (((0,), (0,)), ((), ()))


def _pcall(body, **kw):
    return pl.pallas_call(body, **kw)


def _cparams(*sem):
    return pltpu.CompilerParams(dimension_semantics=sem if sem else None, vmem_limit_bytes=VMEM_LIMIT)


def _dot(a, b, dims=NN):
    return lax.dot_general(a, b, dims, preferred_element_type=F32)


def _split(x, n):
    parts = []
    r = x
    for i in range(n):
        p = r.astype(BF16)
        parts.append(p)
        if i + 1 < n:
            r = r - p.astype(F32)
    return parts


def _xdot(x, m, n=2):
    return sum(_dot(p, m) for p in _split(x, n))


def _xdot_left(m, x, n=3):
    return sum(_dot(m, p) for p in _split(x, n))


def _iota(shape, dim):
    return lax.broadcasted_iota(jnp.int32, shape, dim)


def _sigmoid(x):
    return 1.0 / (1.0 + jnp.exp(-x))


def _tile(dim, pref, mult=LANES):
    t = (min(pref, dim) // mult) * mult
    while t >= mult:
        if dim % t == 0:
            return t
        t -= mult
    return dim


def _rows(dim, pref):
    return _tile(dim, pref, 8)


def _mm(a, b, *, name, tb=False, tm=512, tn=1024, tk=1024, out_dtype=F32, res=None, scale=None):
    m, kd = a.shape
    n = b.shape[0] if tb else b.shape[1]
    tm, tn, tk = _rows(m, tm), _tile(n, tn), _tile(kd, tk)
    nk = kd // tk
    epi = res is not None

    def body(*refs):
        if epi:
            a_ref, b_ref, r_ref, s_ref, o_ref, x_ref, acc = refs
        else:
            a_ref, b_ref, o_ref, acc = refs
        k = pl.program_id(2)

        @pl.when(k == 0)
        def _():
            acc[...] = jnp.zeros_like(acc)

        acc[...] += _dot(a_ref[...], b_ref[...], NT if tb else NN)

        @pl.when(k == nk - 1)
        def _():
            o_ref[...] = acc[...].astype(o_ref.dtype)
            if epi:
                x_ref[...] = r_ref[...] + s_ref[...] * acc[...]

    in_specs = [
        pl.BlockSpec((tm, tk), lambda i, j, k: (i, k)),
        pl.BlockSpec((tn, tk), lambda i, j, k: (j, k)) if tb else pl.BlockSpec((tk, tn), lambda i, j, k: (k, j)),
    ]
    out_shape = [jax.ShapeDtypeStruct((m, n), out_dtype)]
    out_specs = [pl.BlockSpec((tm, tn), lambda i, j, k: (i, j))]
    args = [a, b]
    if epi:
        in_specs += [pl.BlockSpec((tm, tn), lambda i, j, k: (i, j)), pl.BlockSpec((1, tn), lambda i, j, k: (0, j))]
        out_shape.append(jax.ShapeDtypeStruct((m, n), F32))
        out_specs.append(pl.BlockSpec((tm, tn), lambda i, j, k: (i, j)))
        args += [res, scale]
    out = _pcall(
        body, name=name, grid=(m // tm, n // tn, nk), in_specs=in_specs, out_specs=out_specs, out_shape=out_shape,
        scratch_shapes=[pltpu.VMEM((tm, tn), F32)], compiler_params=_cparams("parallel", "parallel", "arbitrary"),
    )(*args)
    return out if epi else out[0]


def _mm_tn(a, b, *, name, tm=1024, tn=1408, tk=512, out_dtype=BF16):
    s, m = a.shape
    n = b.shape[1]
    tm, tn, tk = _tile(m, tm), _tile(n, tn), _rows(s, tk)
    nk = s // tk

    def body(a_ref, b_ref, o_ref, acc):
        k = pl.program_id(2)

        @pl.when(k == 0)
        def _():
            acc[...] = jnp.zeros_like(acc)

        acc[...] += _dot(a_ref[...], b_ref[...], TN)

        @pl.when(k == nk - 1)
        def _():
            o_ref[...] = acc[...].astype(o_ref.dtype)

    return _pcall(
        body, name=name, grid=(m // tm, n // tn, nk),
        in_specs=[pl.BlockSpec((tk, tm), lambda i, j, k: (k, i)), pl.BlockSpec((tk, tn), lambda i, j, k: (k, j))],
        out_specs=pl.BlockSpec((tm, tn), lambda i, j, k: (i, j)), out_shape=jax.ShapeDtypeStruct((m, n), out_dtype),
        scratch_shapes=[pltpu.VMEM((tm, tn), F32)], compiler_params=_cparams("parallel", "parallel", "arbitrary"),
    )(a, b)


TE = 512


def _row_spec(t, w, col=0):
    return pl.BlockSpec((t, w), lambda i, col=col: (i, col))


def _vec_spec(w, col=0):
    return pl.BlockSpec((1, w), lambda i, col=col: (0, col))


def _norm_mod(x, sc, sh, *, name):
    s, d = x.shape
    t = _rows(s, TE)

    def body(x_ref, sc_ref, sh_ref, h_ref):
        xv = x_ref[...]
        r = lax.rsqrt(jnp.mean(xv * xv, axis=1, keepdims=True) + EPS)
        h_ref[...] = ((xv * r) * (1.0 + sc_ref[...]) + sh_ref[...]).astype(BF16)

    return _pcall(body, name=name, grid=(s // t,), in_specs=[_row_spec(t, d), _vec_spec(d), _vec_spec(d)],
                  out_specs=_row_spec(t, d), out_shape=jax.ShapeDtypeStruct((s, d), BF16),
                  compiler_params=_cparams("parallel"))(x, sc, sh)


def _norm_mod_bwd(dh, x, sc, dxo, *, name):
    s, d = x.shape
    t = _rows(s, TE)

    def body(dh_ref, x_ref, sc_ref, dxo_ref, dx_ref, dsc_ref, dsh_ref):
        @pl.when(pl.program_id(0) == 0)
        def _():
            dsc_ref[...] = jnp.zeros_like(dsc_ref)
            dsh_ref[...] = jnp.zeros_like(dsh_ref)

        xv = x_ref[...]
        dhv = dh_ref[...]
        r = lax.rsqrt(jnp.mean(xv * xv, axis=1, keepdims=True) + EPS)
        xn = xv * r
        dxn = dhv * (1.0 + sc_ref[...])
        dx_ref[...] = dxo_ref[...] + r * (dxn - xn * jnp.mean(dxn * xn, axis=1, keepdims=True))
        dsc_ref[...] += jnp.sum(dhv * xn, axis=0, keepdims=True)
        dsh_ref[...] += jnp.sum(dhv, axis=0, keepdims=True)

    return _pcall(
        body, name=name, grid=(s // t,),
        in_specs=[_row_spec(t, d), _row_spec(t, d), _vec_spec(d), _row_spec(t, d)],
        out_specs=[_row_spec(t, d), _vec_spec(d), _vec_spec(d)],
        out_shape=[jax.ShapeDtypeStruct((s, d), F32), jax.ShapeDtypeStruct((1, d), F32), jax.ShapeDtypeStruct((1, d), F32)],
        compiler_params=_cparams("arbitrary"))(dh, x, sc, dxo)


def _swiglu(uv, *, name):
    s, f2 = uv.shape
    f = f2 // 2
    t = _rows(s, 256)

    def body(uv_ref, a_ref):
        u = uv_ref[:, :f]
        v = uv_ref[:, f:]
        a_ref[...] = (u * _sigmoid(u) * v).astype(BF16)

    return _pcall(body, name=name, grid=(s // t,), in_specs=[_row_spec(t, f2)], out_specs=_row_spec(t, f),
                  out_shape=jax.ShapeDtypeStruct((s, f), BF16), compiler_params=_cparams("parallel"))(uv)


def _swiglu_bwd(da, uv, *, name):
    s, f2 = uv.shape
    f = f2 // 2
    t = _rows(s, 256)

    def body(da_ref, uv_ref, d_ref):
        u = uv_ref[:, :f]
        v = uv_ref[:, f:]
        dav = da_ref[...]
        sg = _sigmoid(u)
        d_ref[:, :f] = (dav * v * (sg * (1.0 + u * (1.0 - sg)))).astype(BF16)
        d_ref[:, f:] = (dav * (u * sg)).astype(BF16)

    return _pcall(body, name=name, grid=(s // t,), in_specs=[_row_spec(t, f), _row_spec(t, f2)],
                  out_specs=_row_spec(t, f2), out_shape=jax.ShapeDtypeStruct((s, f2), BF16),
                  compiler_params=_cparams("parallel"))(da, uv)


def _gate_bwd(dxo, y, sg, *, name):
    s, d = y.shape
    t = _rows(s, TE)

    def body(dxo_ref, y_ref, sg_ref, dy_ref, ds_ref):
        @pl.when(pl.program_id(0) == 0)
        def _():
            ds_ref[...] = jnp.zeros_like(ds_ref)

        dv = dxo_ref[...]
        dy_ref[...] = (sg_ref[...] * dv).astype(BF16)
        ds_ref[...] += jnp.sum(dv * y_ref[...], axis=0, keepdims=True)

    return _pcall(
        body, name=name, grid=(s // t,), in_specs=[_row_spec(t, d), _row_spec(t, d), _vec_spec(d)],
        out_specs=[_row_spec(t, d), _vec_spec(d)],
        out_shape=[jax.ShapeDtypeStruct((s, d), BF16), jax.ShapeDtypeStruct((1, d), F32)],
        compiler_params=_cparams("arbitrary"))(dxo, y, sg)


def _loss_grad(y, tgt, *, name):
    s, d = y.shape
    t = _rows(s, TE)
    nt = s // t

    def body(y_ref, t_ref, dy_ref, l_ref, acc):
        i = pl.program_id(0)

        @pl.when(i == 0)
        def _():
            acc[...] = jnp.zeros_like(acc)

        e = y_ref[...] - t_ref[...]
        dy_ref[...] = e * (1.0 / d)
        acc[...] += jnp.sum(e * e, axis=0, keepdims=True)

        @pl.when(i == nt - 1)
        def _():
            l_ref[...] = jnp.broadcast_to(jnp.sum(acc[...], axis=1, keepdims=True) * (0.5 / d), l_ref.shape)

    return _pcall(
        body, name=name, grid=(nt,), in_specs=[_row_spec(t, d), _row_spec(t, d)],
        out_specs=[_row_spec(t, d), pl.BlockSpec((1, LANES), lambda i: (0, 0))],
        out_shape=[jax.ShapeDtypeStruct((s, d), F32), jax.ShapeDtypeStruct((1, LANES), F32)],
        scratch_shapes=[pltpu.VMEM((1, d), F32)], compiler_params=_cparams("arbitrary"))(y, tgt)


SB_TQ = 256
SB_NK = SB_TQ // QBLK


def _sb_consts():
    r = _iota((QBLK, LANES), 0)
    c = _iota((QBLK, LANES), 1)
    ones = jnp.ones((QBLK, LANES), BF16)
    after = jnp.concatenate([jnp.where(r > c, 1.0, 0.0).astype(BF16), ones], axis=1)
    from_ = jnp.concatenate([jnp.where(r >= c, 1.0, 0.0).astype(BF16), ones], axis=1)
    return _iota((SB_TQ, LANES), 0), _iota((SB_TQ, LANES), 1), after, from_


def _sb_scores(qm, kb, strict):
    z = _dot(qm, kb, NT) * (HEAD_DIM ** -0.5)
    sp = jnp.log(1.0 + jnp.exp(-jnp.abs(z)))
    lnb = -(jnp.maximum(z, 0.0) + sp)
    lb = jnp.minimum(z, 0.0) - sp
    if strict is not None:
        lnb = jnp.where(strict, lnb, 0.0)
    return lnb, lb


def _sb_fwd(qkv, *, name):
    s = qkv.shape[0]
    nq = s // SB_TQ

    def body(q_ref, k_ref, v_ref, o_ref, *scr):
        acc, osc = scr[:4], scr[4:]
        qi = pl.program_id(0)
        row, lane, after, _ = _sb_consts()
        h0 = lane < HEAD_DIM
        q = q_ref[...]
        qms = []
        for p in range(2):
            qp = q[:, p * LANES:(p + 1) * LANES]
            qms += [jnp.where(h0, qp, jnp.zeros_like(qp)), jnp.where(h0, jnp.zeros_like(qp), qp)]

        def block(kj, mask):
            off = pl.multiple_of(kj * QBLK, QBLK)
            kbs = [k_ref[pl.ds(off, QBLK), p * LANES:(p + 1) * LANES] for p in range(2)]
            vbs = [v_ref[pl.ds(off, QBLK), p * LANES:(p + 1) * LANES] for p in range(2)]
            sc = [_sb_scores(qms[c], kbs[c // 2], mask) for c in range(4)]
            trs = [_xdot(sc[c][0], after) for c in range(4)]
            top = None
            for c in range(4):
                w = jnp.exp(sc[c][1] + trs[c][:, :QBLK] + acc[c][...])
                if mask is not None:
                    w = jnp.where(mask, w, 0.0)
                osc[c][...] += _xdot(w, vbs[c // 2])
                new = acc[c][...] + trs[c][:, QBLK:]
                acc[c][...] = new
                top = new if top is None else jnp.maximum(top, new)
            return jnp.max(top)

        for ref in scr:
            ref[...] = jnp.zeros_like(ref)
        top = None
        for j in reversed(range(SB_NK)):
            top = block(qi * SB_NK + j, (lane + j * QBLK) < row)
        lax.while_loop(lambda c: (c[0] >= 0) & (c[1] > -SB_EXIT), lambda c: (c[0] - 1, block(c[0], None)),
                       (qi * SB_NK - 1, top))
        for p in range(2):
            o_ref[:, p * LANES:(p + 1) * LANES] = jnp.where(h0, osc[2 * p][...], osc[2 * p + 1][...])

    return _pcall(
        body, name=name, grid=(nq,),
        in_specs=[pl.BlockSpec((SB_TQ, SB_W), lambda i: (i, 0)),
                  pl.BlockSpec((s, SB_W), lambda i: (0, 1)),
                  pl.BlockSpec((s, SB_W), lambda i: (0, 2))],
        out_specs=pl.BlockSpec((SB_TQ, SB_W), lambda i: (i, 0)),
        out_shape=jax.ShapeDtypeStruct((s, SB_W), F32),
        scratch_shapes=[pltpu.VMEM((SB_TQ, LANES), F32)] * 8,
        compiler_params=_cparams("arbitrary"))(qkv, qkv, qkv)


def _sb_bwd(qkv, o, dmix, *, name):
    s = qkv.shape[0]
    nq = s // SB_TQ
    scale = HEAD_DIM ** -0.5

    def body(q_ref, k_ref, v_ref, o_ref, do_ref, dq_ref, dk_ref, dv_ref, a0, a1, r0, r1, dqs, dks, dvs):
        acc, racc = (a0, a1), (r0, r1)
        i = pl.program_id(1)
        qi = nq - 1 - i
        row, lane, after, from_ = _sb_consts()
        klane = _iota((QBLK, LANES), 1)
        khms = (klane < HEAD_DIM, klane >= HEAD_DIM)

        @pl.when(i == 0)
        def _():
            dks[...] = jnp.zeros_like(dks)
            dvs[...] = jnp.zeros_like(dvs)

        q = q_ref[...]
        do = do_ref[...]
        dob = do.astype(BF16)
        dd = do * o_ref[...]
        dol = (do - dob.astype(F32)).astype(BF16)
        zero = jnp.zeros_like(q)
        hms = (lane < HEAD_DIM, lane >= HEAD_DIM)
        qms = [jnp.where(hm, q, zero) for hm in hms]
        doms = [jnp.where(hm, dob, zero) for hm in hms]
        dols = [jnp.where(hm, dol, zero) for hm in hms]
        dsums = [jnp.sum(jnp.where(hm, dd, 0.0), axis=1, keepdims=True) for hm in hms]

        def block(kj, mask):
            off = pl.multiple_of(kj * QBLK, QBLK)
            kb = k_ref[pl.ds(off, QBLK), :]
            vb = v_ref[pl.ds(off, QBLK), :]
            top, dq, dk, dv = None, None, None, None
            sc = [_sb_scores(qms[h], kb, mask) for h in range(2)]
            trs = [_xdot(sc[h][0], after) for h in range(2)]
            dws = [_dot(doms[h], vb, NT) + _dot(dols[h], vb, NT) for h in range(2)]
            for h in range(2):
                lb, tr = sc[h][1], trs[h]
                w = jnp.exp(lb + tr[:, :QBLK] + acc[h][...])
                if mask is not None:
                    w = jnp.where(mask, w, 0.0)
                g = w * dws[h]
                tg = _xdot(g, from_)
                before = dsums[h] - (tg[:, :QBLK] + racc[h][...])
                dz = g - jnp.exp(lb) * (g + before)
                if mask is not None:
                    dz = jnp.where(mask, dz, 0.0)
                dzb = (dz * scale).astype(BF16)
                dqh = _dot(dzb, jnp.where(khms[h], kb, jnp.zeros_like(kb)))
                dkh = _dot(dzb, qms[h], TN)
                dvh = _dot(w.astype(BF16), doms[h], TN)
                dq, dk, dv = (dqh, dkh, dvh) if h == 0 else (dq + dqh, dk + dkh, dv + dvh)
                new = acc[h][...] + tr[:, QBLK:]
                acc[h][...] = new
                racc[h][...] += tg[:, QBLK:]
                top = new if top is None else jnp.maximum(top, new)
            dqs[...] += dq
            dks[pl.ds(off, QBLK), :] += dk
            dvs[pl.ds(off, QBLK), :] += dv
            return jnp.max(top)

        for ref in (dqs, a0, a1, r0, r1):
            ref[...] = jnp.zeros_like(ref)
        top = None
        for j in reversed(range(SB_NK)):
            top = block(qi * SB_NK + j, (lane + j * QBLK) < row)
        lax.while_loop(lambda c: (c[0] >= 0) & (c[1] > -SB_EXIT), lambda c: (c[0] - 1, block(c[0], None)),
                       (qi * SB_NK - 1, top))
        dq_ref[...] = dqs[...]
        fin = pl.multiple_of(qi * SB_TQ, SB_TQ)
        dk_ref[...] = dks[pl.ds(fin, SB_TQ), :]
        dv_ref[...] = dvs[pl.ds(fin, SB_TQ), :]

    blk = lambda c0: pl.BlockSpec((SB_TQ, LANES), lambda p, i, c0=c0: (nq - 1 - i, c0 + p))
    return _pcall(
        body, name=name, grid=(2, nq),
        in_specs=[blk(0), pl.BlockSpec((s, LANES), lambda p, i: (0, 2 + p)), pl.BlockSpec((s, LANES), lambda p, i: (0, 4 + p)),
                  blk(0), blk(0)],
        out_specs=[blk(0), blk(0), blk(0)],
        out_shape=[jax.ShapeDtypeStruct((s, SB_W), F32)] * 3,
        scratch_shapes=[pltpu.VMEM((SB_TQ, LANES), F32)] * 5 + [pltpu.VMEM((s, LANES), F32), pltpu.VMEM((s, LANES), F32)],
        compiler_params=_cparams("arbitrary", "arbitrary"))(qkv, qkv, qkv, o, dmix)


def _seg_consts():
    r = _iota((LANES, LANES), 0)
    c = _iota((LANES, LANES), 1)
    return jnp.where((r >> 6) == (c >> 6), 1.0, 0.0).astype(BF16)


def _rot_half(x, lane):
    half = HEAD_DIM // 2
    return jnp.where((lane & (HEAD_DIM - 1)) < half, pltpu.roll(x, LANES - half, 1), pltpu.roll(x, half, 1))


def _rope_tables(s):
    half = HEAD_DIM // 2
    inv_freq = ROPE_THETA ** (-jnp.arange(half, dtype=F32) * 2.0 / HEAD_DIM)
    ang = jnp.arange(s, dtype=F32)[:, None] * inv_freq[None, :]
    cos, sin = jnp.cos(ang), jnp.sin(ang)
    return jnp.tile(jnp.concatenate([cos, cos], axis=1), (1, 2)), jnp.tile(jnp.concatenate([-sin, sin], axis=1), (1, 2))


def _dil_prep(proj, gq, gk, cos, sin, *, name):
    s = proj.shape[0]
    t = _rows(s, TE)
    c0 = 3 * SB_W // LANES

    def body(q_ref, k_ref, gq_ref, gk_ref, cos_ref, sin_ref, qo_ref, ko_ref):
        seg = _seg_consts()
        lane = _iota((t, LANES), 1)
        cs, sn = cos_ref[...], sin_ref[...]
        for x_ref, g_ref, o_ref, mul in ((q_ref, gq_ref, qo_ref, HEAD_DIM ** -0.5), (k_ref, gk_ref, ko_ref, 1.0)):
            for j in range(2):
                xv = x_ref[:, j * LANES:(j + 1) * LANES]
                ms = _xdot(xv * xv, seg, 3) * (1.0 / HEAD_DIM)
                xn = xv * lax.rsqrt(ms + EPS) * g_ref[...]
                o_ref[:, j * LANES:(j + 1) * LANES] = ((xn * cs + _rot_half(xn, lane) * sn) * mul).astype(BF16)

    return _pcall(
        body, name=name, grid=(s // t,),
        in_specs=[pl.BlockSpec((t, DIL_W), lambda i: (i, c0 // 2)), pl.BlockSpec((t, DIL_W), lambda i: (i, c0 // 2 + 1)),
                  _vec_spec(LANES), _vec_spec(LANES), _row_spec(t, LANES), _row_spec(t, LANES)],
        out_specs=[_row_spec(t, DIL_W), _row_spec(t, DIL_W)],
        out_shape=[jax.ShapeDtypeStruct((s, DIL_W), BF16)] * 2, compiler_params=_cparams("parallel"))(proj, proj, gq, gk, cos, sin)


def _dil_prep_bwd(proj, gq, gk, cos, sin, dqs, dks, dvs, *, name):
    s = proj.shape[0]
    t = _rows(s, TE)
    c0 = 3 * SB_W // LANES

    def body(q_ref, k_ref, gq_ref, gk_ref, cos_ref, sin_ref, a0, a1, a2, b0, b1, b2, c0_ref, c1_ref, c2_ref,
             dq_ref, dk_ref, dv_ref, dgq_ref, dgk_ref):
        @pl.when(pl.program_id(0) == 0)
        def _():
            dgq_ref[...] = jnp.zeros_like(dgq_ref)
            dgk_ref[...] = jnp.zeros_like(dgk_ref)

        dv_ref[...] = c0_ref[...] + c1_ref[...] + c2_ref[...]
        seg = _seg_consts()
        lane = _iota((t, LANES), 1)
        cs, sn = cos_ref[...], sin_ref[...]
        for x_ref, g_ref, parts, o_ref, dg_ref, mul in ((q_ref, gq_ref, (a0, a1, a2), dq_ref, dgq_ref, HEAD_DIM ** -0.5),
                                                          (k_ref, gk_ref, (b0, b1, b2), dk_ref, dgk_ref, 1.0)):
            for j in range(2):
                sl = slice(j * LANES, (j + 1) * LANES)
                dout = (parts[0][:, sl] + parts[1][:, sl] + parts[2][:, sl]) * mul
                dxn = dout * cs + _rot_half(dout * sn, lane)
                xv = x_ref[:, sl]
                r = lax.rsqrt(_xdot(xv * xv, seg, 3) * (1.0 / HEAD_DIM) + EPS)
                xh = xv * r
                dg_ref[...] += jnp.sum(dxn * xh, axis=0, keepdims=True)
                dxh = dxn * g_ref[...]
                o_ref[:, sl] = r * (dxh - xh * (_xdot(dxh * xh, seg, 3) * (1.0 / HEAD_DIM)))

    rs = _row_spec(t, DIL_W)
    return _pcall(
        body, name=name, grid=(s // t,),
        in_specs=[pl.BlockSpec((t, DIL_W), lambda i: (i, c0 // 2)), pl.BlockSpec((t, DIL_W), lambda i: (i, c0 // 2 + 1)),
                  _vec_spec(LANES), _vec_spec(LANES), _row_spec(t, LANES), _row_spec(t, LANES)] + [rs] * 9,
        out_specs=[rs, rs, rs, _vec_spec(LANES), _vec_spec(LANES)],
        out_shape=[jax.ShapeDtypeStruct((s, DIL_W), F32)] * 3 + [jax.ShapeDtypeStruct((1, LANES), F32)] * 2,
        compiler_params=_cparams("arbitrary"))(proj, proj, gq, gk, cos, sin, *dqs, *dks, *dvs)


def _dil_masks(n):
    row = _iota((QBLK, LANES), 0)
    col = _iota((QBLK, LANES), 1)
    return col <= row, (col >= row) & (n > 0)


def _dil_fwd(q, k, v, r, *, name):
    s = q.shape[0]
    rows = s // r
    tb = _tile(rows, DIL_TILE, QBLK)
    nsub = tb // QBLK
    shape = (rows, r * DIL_W)
    q, k, v = (a.reshape(shape) for a in (q, k, v))

    def body(q_ref, kc_ref, kp_ref, vc_ref, vp_ref, num_ref, den_ref, mx_ref):
        n = pl.program_id(1)
        lane = _iota((QBLK, LANES), 1)
        h0 = lane < HEAD_DIM
        for j in range(nsub):
            rs = slice(j * QBLK, (j + 1) * QBLK)
            vc_m, vp_m = _dil_masks(n if j == 0 else 1)
            for p in range(2):
                ls = slice(p * LANES, (p + 1) * LANES)
                qv = q_ref[rs, ls]
                kc, vc = kc_ref[rs, ls], vc_ref[rs, ls]
                if j == 0:
                    kp, vp = kp_ref[:, ls], vp_ref[:, ls]
                else:
                    kp, vp = kc_ref[(j - 1) * QBLK:j * QBLK, ls], vc_ref[(j - 1) * QBLK:j * QBLK, ls]
                res = []
                for h in range(2):
                    qm = jnp.where(h0 if h == 0 else ~h0, qv, jnp.zeros_like(qv))
                    sc = jnp.where(vc_m, _dot(qm, kc, NT), NEG_BIG)
                    sp = jnp.where(vp_m, _dot(qm, kp, NT), NEG_BIG)
                    mx = jnp.maximum(jnp.max(sc, axis=1, keepdims=True), jnp.max(sp, axis=1, keepdims=True))
                    pc = jnp.exp(sc - mx)
                    pp = jnp.exp(sp - mx)
                    den = jnp.sum(pc, axis=1, keepdims=True) + jnp.sum(pp, axis=1, keepdims=True)
                    res.append((_dot(pc.astype(BF16), vc) + _dot(pp.astype(BF16), vp), den, mx))
                num_ref[rs, ls] = jnp.where(h0, res[0][0], res[1][0])
                den_ref[rs, ls] = jnp.where(h0, res[0][1], res[1][1])
                mx_ref[rs, ls] = jnp.where(h0, res[0][2], res[1][2])

    cur = pl.BlockSpec((tb, DIL_W), lambda rho, n: (n, rho))
    prev = pl.BlockSpec((QBLK, DIL_W), lambda rho, n: (jnp.maximum(n * nsub - 1, 0), rho))
    return _pcall(
        body, name=name, grid=(r, rows // tb), in_specs=[cur, cur, prev, cur, prev], out_specs=[cur, cur, cur],
        out_shape=[jax.ShapeDtypeStruct(shape, F32)] * 3,
        compiler_params=_cparams("parallel", "arbitrary"))(q, k, k, v, v)


def _dil_bwd(q, k, v, do, mall, zall, delta, r, *, name):
    s = q.shape[0]
    rows = s // r
    tb = _tile(rows, DIL_TILE, QBLK)
    nsub, nbig = tb // QBLK, rows // tb
    shape = (rows, r * DIL_W)
    q, k, v, do, mall, zall, delta = (a.reshape(shape) for a in (q, k, v, do, mall, zall, delta))

    def body(q_ref, kc_ref, kp_ref, vc_ref, vp_ref, do_ref, m_ref, z_ref, dl_ref, dq_ref, dk_ref, dv_ref, pk, pv):
        n = pl.program_id(1)
        lane = _iota((QBLK, LANES), 1)
        h0 = lane < HEAD_DIM

        @pl.when(n == 0)
        def _():
            pk[...] = jnp.zeros_like(pk)
            pv[...] = jnp.zeros_like(pv)

        @pl.when(n < nbig)
        def _():
            dk_ref[...] = pk[...]
            dv_ref[...] = pv[...]
            for j in range(nsub):
                rs = slice(j * QBLK, (j + 1) * QBLK)
                ps = slice((j - 1) * QBLK, j * QBLK)
                vc_m, vp_m = _dil_masks(n if j == 0 else 1)
                for p in range(2):
                    ls = slice(p * LANES, (p + 1) * LANES)
                    qv = q_ref[rs, ls]
                    dob = do_ref[rs, ls].astype(BF16)
                    zero = jnp.zeros_like(qv)
                    kc, vc = kc_ref[rs, ls], vc_ref[rs, ls]
                    kp, vp = (kp_ref[:, ls], vp_ref[:, ls]) if j == 0 else (kc_ref[ps, ls], vc_ref[ps, ls])
                    dq = None
                    acc = [None] * 4
                    for h in range(2):
                        hm = h0 if h == 0 else ~h0
                        qm = jnp.where(hm, qv, zero)
                        dom = jnp.where(hm, dob, zero)
                        c = p * LANES + h * HEAD_DIM
                        mrow, dlrow = m_ref[rs, c:c + 1], dl_ref[rs, c:c + 1]
                        rz = 1.0 / z_ref[rs, c:c + 1]
                        for kb, vb, valid, o in ((kc, vc, vc_m, 0), (kp, vp, vp_m, 2)):
                            sc = jnp.where(valid, _dot(qm, kb, NT), NEG_BIG)
                            pr = jnp.exp(sc - mrow) * rz
                            ds = (pr * (_dot(dom, vb, NT) - dlrow)).astype(BF16)
                            dqx = _dot(ds, jnp.where(hm, kb, zero))
                            dq = dqx if dq is None else dq + dqx
                            for i, x in ((o, _dot(ds, qm, TN)), (o + 1, _dot(pr.astype(BF16), dom, TN))):
                                acc[i] = x if acc[i] is None else acc[i] + x
                    dq_ref[rs, ls] = dq
                    if j == 0:
                        pk[rs, ls] = acc[0]
                        pv[rs, ls] = acc[1]
                        dk_ref[tb - QBLK:, ls] += acc[2]
                        dv_ref[tb - QBLK:, ls] += acc[3]
                    else:
                        pk[rs, ls] = acc[0]
                        pv[rs, ls] = acc[1]
                        pk[ps, ls] += acc[2]
                        pv[ps, ls] += acc[3]

        @pl.when(n == nbig)
        def _():
            dk_ref[...] = pk[...]
            dv_ref[...] = pv[...]

    last = nbig - 1
    cur = pl.BlockSpec((tb, DIL_W), lambda rho, n: (jnp.minimum(n, last), rho))
    prev = pl.BlockSpec((QBLK, DIL_W), lambda rho, n: (jnp.maximum(jnp.minimum(n, last) * nsub - 1, 0), rho))
    late = pl.BlockSpec((tb, DIL_W), lambda rho, n: (jnp.maximum(n - 1, 0), rho))
    return _pcall(
        body, name=name, grid=(r, nbig + 1), in_specs=[cur, cur, prev, cur, prev, cur, cur, cur, cur],
        out_specs=[cur, late, late], out_shape=[jax.ShapeDtypeStruct(shape, F32)] * 3,
        scratch_shapes=[pltpu.VMEM((tb, DIL_W), F32)] * 2,
        compiler_params=_cparams("parallel", "arbitrary"))(q, k, k, v, v, do, mall, zall, delta)


HG_SHIFT = HG_BLK.bit_length() - 1
HG_Q0, HG_F0, HG_I0 = (3 * SB_W + 3 * DIL_W) // HG_D, (3 * SB_W + 3 * DIL_W + HG_W) // HG_D, (3 * SB_W + 3 * DIL_W + 2 * HG_W) // HG_D


def _hg_inputs(qh, z, v, la, lc, t):
    r = _iota((t, t), 0)
    c = _iota((t, t), 1)
    same = (r >> HG_SHIFT) == (c >> HG_SHIFT)
    tri = jnp.where(same & (c <= r), 1.0, 0.0).astype(BF16)
    blk = jnp.where(same, 1.0, 0.0).astype(BF16)
    lsg = jnp.minimum(z, 0.0) - jnp.log(1.0 + jnp.exp(-jnp.abs(z)))
    b = lc + lsg
    lf = jnp.maximum(la, b) + jnp.log(1.0 + jnp.exp(-jnp.abs(la - b)))
    f = jnp.exp(lf)
    sq = _sigmoid(qh)
    g = _xdot_left(tri, lf)
    gl = _xdot_left(blk, lf)
    half = HG_BLK // 2
    mid = jnp.where(c == ((r >> HG_SHIFT) << HG_SHIFT) + (half - 1), 1.0, 0.0).astype(BF16)
    gm = _xdot_left(mid, g)
    first = (_iota((t, LANES), 0) & (HG_BLK - 1)) < half
    k = 1.0 - f
    qs = qh * sq
    eq = jnp.where(first, 0.0, jnp.exp(jnp.minimum(g - gm, 0.0)))
    ek = jnp.where(first, jnp.exp(jnp.minimum(gm - g, 0.0)), 0.0)
    return dict(lf=lf, b=b, f=f, k=k, sq=sq, qs=qs, g=g, eg=jnp.exp(g), egl=jnp.exp(gl - g), dec=jnp.exp(gl),
                eq=eq, ek=ek, qx=(qs * eq).astype(BF16), kx=(k * ek).astype(BF16), same=same)


def _hgrn_fwd(proj, la, lc, *, name):
    s = proj.shape[0]
    t = _rows(s, HG_TILE)
    nt, nb = s // t, t // HG_BLK

    def body(q_ref, f_ref, i_ref, la_ref, lc_ref, o_ref, st_ref, state, osc):
        @pl.when(pl.program_id(1) == 0)
        def _():
            state[...] = jnp.zeros_like(state)

        v = i_ref[...]
        a = _hg_inputs(q_ref[...], f_ref[...], v, la_ref[...], lc_ref[...], t)
        qs, k = a["qs"], a["k"]
        vb = v.astype(BF16)
        rb = _iota((t, LANES), 0) & (HG_BLK // 2 - 1)
        o = jnp.sum(qs * k, axis=1, keepdims=True) * v
        e = None
        for d in range(1, HG_BLK // 2):
            m = rb >= d
            fr = a["f"] if d == 1 else pltpu.roll(a["f"], d - 1, 0)
            e = fr if e is None else e * fr
            cd = jnp.sum(qs * pltpu.roll(k, d, 0) * e, axis=1, keepdims=True)
            o = o + jnp.where(m, cd, 0.0) * pltpu.roll(v, d, 0)
        cross = jnp.where(a["same"], _dot(a["qx"], a["kx"], NT), 0.0)
        osc[...] = o + _dot(cross.astype(BF16), vb)
        qt = (qs * a["eg"]).astype(BF16)
        kt = (k * a["egl"]).astype(BF16)
        for blk in range(nb):
            sl = slice(blk * HG_BLK, (blk + 1) * HG_BLK)
            st = state[...]
            stb = st.astype(BF16)
            st_ref[blk * HG_D:(blk + 1) * HG_D, :] = stb
            osc[sl, :] += _dot(qt[sl], stb, NT)
            state[...] = a["dec"][blk * HG_BLK:blk * HG_BLK + 1] * st + _dot(vb[sl], kt[sl], TN)
        o_ref[...] = osc[...]

    col = lambda c0: pl.BlockSpec((t, HG_D), lambda hd, i, c0=c0: (i, c0 + hd))
    vec = pl.BlockSpec((1, HG_D), lambda hd, i: (0, hd))
    return _pcall(
        body, name=name, grid=(4, nt), in_specs=[col(HG_Q0), col(HG_F0), col(HG_I0), vec, vec],
        out_specs=[col(0), pl.BlockSpec((None, nb * HG_D, HG_D), lambda hd, i: (hd, i, 0))],
        out_shape=[jax.ShapeDtypeStruct((s, HG_W), F32), jax.ShapeDtypeStruct((4, s // HG_BLK * HG_D, HG_D), BF16)],
        scratch_shapes=[pltpu.VMEM((HG_D, HG_D), F32), pltpu.VMEM((t, HG_D), F32)],
        compiler_params=_cparams("arbitrary", "arbitrary"))(proj, proj, proj, la, lc)


def _hgrn_bwd(proj, la, lc, st, doh, *, name):
    s = proj.shape[0]
    t = _rows(s, HG_TILE)
    nt, nb = s // t, t // HG_BLK

    def body(q_ref, f_ref, i_ref, la_ref, lc_ref, st_ref, do_ref, dq_ref, df_ref, di_ref, dla_ref, dlc_ref, dstate):
        @pl.when(pl.program_id(1) == 0)
        def _():
            dstate[...] = jnp.zeros_like(dstate)
            dla_ref[...] = jnp.zeros_like(dla_ref)
            dlc_ref[...] = jnp.zeros_like(dlc_ref)

        qh, z, v, do = q_ref[...], f_ref[...], i_ref[...], do_ref[...]
        la = la_ref[...]
        a = _hg_inputs(qh, z, v, la, lc_ref[...], t)
        qs, k, g = a["qs"], a["k"], a["g"]
        vb = v.astype(BF16)
        dob = do.astype(BF16)
        rb = _iota((t, LANES), 0) & (HG_BLK // 2 - 1)
        dc0 = jnp.sum(do * v, axis=1, keepdims=True)
        dq = dc0 * k
        dk = dc0 * qs
        dv = jnp.sum(qs * k, axis=1, keepdims=True) * do
        e = None
        for d in range(1, HG_BLK // 2):
            m = rb >= d
            fr = a["f"] if d == 1 else pltpu.roll(a["f"], d - 1, 0)
            e = fr if e is None else e * fr
            ks = pltpu.roll(k, d, 0)
            qe = qs * e
            cd = jnp.where(m, jnp.sum(qe * ks, axis=1, keepdims=True), 0.0)
            dcd = jnp.where(m, jnp.sum(do * pltpu.roll(v, d, 0), axis=1, keepdims=True), 0.0)
            dq = dq + dcd * ks * e
            dk = dk + pltpu.roll(dcd * qe, t - d, 0)
            dv = dv + pltpu.roll(cd * do, t - d, 0)
        cross = jnp.where(a["same"], _dot(a["qx"], a["kx"], NT), 0.0).astype(BF16)
        dcross = jnp.where(a["same"], _dot(dob, vb, NT), 0.0).astype(BF16)
        dq = dq + _dot(dcross, a["kx"]) * a["eq"]
        dk = dk + _dot(dcross, a["qx"], TN) * a["ek"]
        dv = dv + _dot(cross, dob, TN)
        qt = (qs * a["eg"]).astype(BF16)
        kt = (k * a["egl"]).astype(BF16)
        dqt, dkt, dvi, dgs = [None] * nb, [None] * nb, [None] * nb, [None] * nb
        for blk in reversed(range(nb)):
            sl = slice(blk * HG_BLK, (blk + 1) * HG_BLK)
            stb = st_ref[blk * HG_D:(blk + 1) * HG_D, :]
            ds1 = dstate[...]
            ds1b = ds1.astype(BF16)
            dec = a["dec"][blk * HG_BLK:blk * HG_BLK + 1]
            dqt[blk] = _dot(dob[sl], stb)
            dkt[blk] = _dot(vb[sl], ds1b)
            dvi[blk] = _dot(kt[sl], ds1b, NT)
            dgs[blk] = jnp.broadcast_to(jnp.sum(ds1 * stb.astype(F32), axis=0, keepdims=True) * dec, (HG_BLK, HG_D))
            dstate[...] = dec * ds1 + _dot(dob[sl], qt[sl], TN)
        dki = jnp.concatenate(dkt, axis=0) * a["egl"]
        dq = dq + jnp.concatenate(dqt, axis=0) * a["eg"]
        dk = dk + dki
        dv = dv + jnp.concatenate(dvi, axis=0)
        r = _iota((t, t), 0)
        c = _iota((t, t), 1)
        same = (r >> HG_SHIFT) == (c >> HG_SHIFT)
        later = jnp.where(same & (c >= r), 1.0, 0.0).astype(BF16)
        whole = jnp.where(same, 1.0, 0.0).astype(BF16)
        dlf = (_xdot_left(later, qs * dq - k * dk) + _xdot_left(whole, k * dki) + jnp.concatenate(dgs, axis=0)
               - a["f"] * dk)
        wb = jnp.exp(a["b"] - a["lf"])
        wa = jnp.exp(la - a["lf"])
        sq = a["sq"]
        dq_ref[...] = dq * (sq * (1.0 + qh * (1.0 - sq)))
        df_ref[...] = dlf * wb * (1.0 - _sigmoid(z))
        di_ref[...] = dv
        dla_ref[...] += jnp.sum(dlf * wa, axis=0, keepdims=True)
        dlc_ref[...] += jnp.sum(dlf * wb, axis=0, keepdims=True)

    col = lambda c0: pl.BlockSpec((t, HG_D), lambda hd, i, c0=c0: (nt - 1 - i, c0 + hd))
    vec = pl.BlockSpec((1, HG_D), lambda hd, i: (0, hd))
    return _pcall(
        body, name=name, grid=(4, nt),
        in_specs=[col(HG_Q0), col(HG_F0), col(HG_I0), vec, vec,
                  pl.BlockSpec((None, nb * HG_D, HG_D), lambda hd, i: (hd, nt - 1 - i, 0)), col(0)],
        out_specs=[col(0), col(0), col(0), vec, vec],
        out_shape=[jax.ShapeDtypeStruct((s, HG_W), F32)] * 3 + [jax.ShapeDtypeStruct((1, HG_W), F32)] * 2,
        scratch_shapes=[pltpu.VMEM((HG_D, HG_D), F32)],
        compiler_params=_cparams("arbitrary", "arbitrary"))(proj, proj, proj, la, lc, st, doh)


GH0 = (IN_W - HG_W) // HG_W


def _mix_out(o_a, nums, dens, mxs, oh, proj, hg, *, name):
    s = o_a.shape[0]
    t = _rows(s, TE)

    def body(oa_ref, n0, n1, n2, d0, d1, d2, m0, m1, m2, oh_ref, gh_ref, hg_ref, y_ref, od_ref, mall_ref, z_ref):
        y_ref[:, :SB_W] = oa_ref[...].astype(BF16)
        m = jnp.maximum(jnp.maximum(m0[...], m1[...]), m2[...])
        num = jnp.zeros((t, DIL_W), F32)
        z = jnp.zeros((t, DIL_W), F32)
        for n_ref, d_ref, m_ref in ((n0, d0, m0), (n1, d1, m1), (n2, d2, m2)):
            sc = jnp.exp(m_ref[...] - m)
            num = num + n_ref[...] * sc
            z = z + d_ref[...] * sc
        od = num / z
        od_ref[...] = od
        mall_ref[...] = m
        z_ref[...] = z
        y_ref[:, SB_W:SB_W + DIL_W] = od.astype(BF16)
        for h in range(4):
            sl = slice(h * HG_D, (h + 1) * HG_D)
            ov = oh_ref[:, sl]
            g = gh_ref[:, sl]
            r = lax.rsqrt(jnp.mean(ov * ov, axis=1, keepdims=True) + EPS)
            y_ref[:, SB_W + DIL_W + h * HG_D:SB_W + DIL_W + (h + 1) * HG_D] = (ov * r * hg_ref[...] * (g * _sigmoid(g))).astype(BF16)

    rd = _row_spec(t, DIL_W)
    return _pcall(
        body, name=name, grid=(s // t,),
        in_specs=[rd] * 10 + [_row_spec(t, HG_W), _row_spec(t, HG_W, GH0), _vec_spec(HG_D)],
        out_specs=[_row_spec(t, MIX_W), rd, rd, rd],
        out_shape=[jax.ShapeDtypeStruct((s, MIX_W), BF16)] + [jax.ShapeDtypeStruct((s, DIL_W), F32)] * 3,
        compiler_params=_cparams("parallel"))(o_a, *nums, *dens, *mxs, oh, proj, hg)


def _mix_out_bwd(dmix, oh, proj, hg, od, *, name):
    s = oh.shape[0]
    t = _rows(s, TE)

    def body(dm_ref, oh_ref, gh_ref, hg_ref, od_ref, doh_ref, dgh_ref, dl_ref, dhg_ref):
        @pl.when(pl.program_id(0) == 0)
        def _():
            dhg_ref[...] = jnp.zeros_like(dhg_ref)

        seg = _seg_consts()
        for j in range(2):
            sl = slice(j * LANES, (j + 1) * LANES)
            dl_ref[:, sl] = _xdot(dm_ref[:, SB_W + j * LANES:SB_W + (j + 1) * LANES] * od_ref[:, sl], seg, 3)
        hgv = hg_ref[...]
        for h in range(4):
            sl = slice(h * HG_D, (h + 1) * HG_D)
            dy = dm_ref[:, SB_W + DIL_W + h * HG_D:SB_W + DIL_W + (h + 1) * HG_D]
            ov = oh_ref[:, sl]
            g = gh_ref[:, sl]
            sg = _sigmoid(g)
            silu = g * sg
            r = lax.rsqrt(jnp.mean(ov * ov, axis=1, keepdims=True) + EPS)
            nrm = ov * r
            dhg_ref[...] += jnp.sum(dy * nrm * silu, axis=0, keepdims=True)
            dgh_ref[:, sl] = dy * nrm * hgv * (sg * (1.0 + g * (1.0 - sg)))
            dn = dy * hgv * silu
            doh_ref[:, sl] = r * (dn - nrm * jnp.mean(dn * nrm, axis=1, keepdims=True))

    rh = _row_spec(t, HG_W)
    return _pcall(
        body, name=name, grid=(s // t,),
        in_specs=[_row_spec(t, MIX_W), rh, _row_spec(t, HG_W, GH0), _vec_spec(HG_D), _row_spec(t, DIL_W)],
        out_specs=[rh, rh, _row_spec(t, DIL_W), _vec_spec(HG_D)],
        out_shape=[jax.ShapeDtypeStruct((s, HG_W), F32)] * 2 + [jax.ShapeDtypeStruct((s, DIL_W), F32), jax.ShapeDtypeStruct((1, HG_D), F32)],
        compiler_params=_cparams("arbitrary"))(dmix, oh, proj, hg, od)


def _lb_terms(l):
    l0, l1 = l[0:1], l[1:2]
    m = jnp.maximum(l0, l1)
    e0, e1 = jnp.exp(l0 - m), jnp.exp(l1 - m)
    s0, s1 = e0 / (e0 + e1), e1 / (e0 + e1)
    args = (s0 - s0, (s0 + s1) - s0)
    lbs = tuple(jnp.minimum(jnp.maximum(a, 0.0), 1.0 - EPS) for a in args)
    return s0, s1, args, lbs


def _lb_prep(logits, *, name):
    def body(l_ref, lb_ref, la_ref, lc_ref):
        _, _, _, lbs = _lb_terms(l_ref[...])
        lb = jnp.concatenate(lbs, axis=0)
        lb_ref[...] = lb
        la_ref[...] = jnp.log(jnp.maximum(lb, LB_FLOOR))
        lc_ref[...] = jnp.log1p(-lb)

    return _pcall(body, name=name, out_shape=[jax.ShapeDtypeStruct(logits.shape, F32)] * 3)(logits)


def _lb_bwd(logits, dla, dlc, *, name):
    def half(hi, eq):
        return jnp.where(hi, 1.0, jnp.where(eq, 0.5, 0.0))

    def body(l_ref, dla_ref, dlc_ref, o_ref):
        s0, s1, args, lbs = _lb_terms(l_ref[...])
        da = []
        for i in range(2):
            a, lb = args[i], lbs[i]
            dlb = dla_ref[i:i + 1] * half(lb > LB_FLOOR, lb == LB_FLOOR) / jnp.maximum(lb, LB_FLOOR) - dlc_ref[i:i + 1] / (1.0 - lb)
            t = jnp.maximum(a, 0.0)
            da.append(dlb * half(a > 0.0, a == 0.0) * half(t < 1.0 - EPS, t == 1.0 - EPS))
        ds0 = (da[0] + da[1]) - (da[0] + da[1])
        ds1 = da[1]
        dot = s0 * ds0 + s1 * ds1
        o_ref[...] = jnp.concatenate([s0 * (ds0 - dot), s1 * (ds1 - dot)], axis=0)

    return _pcall(body, name=name, out_shape=jax.ShapeDtypeStruct(logits.shape, F32))(logits, dla, dlc)


def _mod_fwd(c8, w, b, *, name):
    _, d, n = w.shape
    tn = _tile(n, 768)

    def body(c_ref, w_ref, b_ref, o_ref):
        cv = c_ref[...]
        o_ref[...] = _dot((cv * _sigmoid(cv)).astype(BF16), w_ref[...].astype(BF16)) + b_ref[...]

    return _pcall(
        body, name=name, grid=(2, n // tn),
        in_specs=[pl.BlockSpec((8, d), lambda l, j: (0, 0)), pl.BlockSpec((None, d, tn), lambda l, j: (l, 0, j)),
                  pl.BlockSpec((None, 1, tn), lambda l, j: (l, 0, j))],
        out_specs=pl.BlockSpec((None, 8, tn), lambda l, j: (l, 0, j)),
        out_shape=jax.ShapeDtypeStruct((2, 8, n), F32), compiler_params=_cparams("parallel", "parallel"))(c8, w, b)


def _mod_bwd(ct, dm, *, name):
    d = ct.shape[0]
    n = dm.shape[2]
    tn = _tile(n, 768)

    def body(c_ref, dm_ref, o_ref):
        cv = c_ref[...]
        sc = cv * _sigmoid(cv)
        dv = dm_ref[...]
        acc = sc[:, 0:1] * dv[0:1, :]
        for b in range(1, 8):
            acc = acc + sc[:, b:b + 1] * dv[b:b + 1, :]
        o_ref[...] = acc

    return _pcall(
        body, name=name, grid=(2, n // tn),
        in_specs=[pl.BlockSpec((d, 8), lambda l, j: (0, 0)), pl.BlockSpec((None, 8, tn), lambda l, j: (l, 0, j))],
        out_specs=pl.BlockSpec((None, d, tn), lambda l, j: (l, 0, j)),
        out_shape=jax.ShapeDtypeStruct((2, d, n), F32), compiler_params=_cparams("parallel", "parallel"))(ct, dm)


_PEERS = {
    "chips": ((1, 0, 0), (0, 1, 0), (1, 1, 0)),
    "all": tuple((a, b, c) for a in (0, 1) for b in (0, 1) for c in (0, 1) if a + b + c),
    "sib": ((0, 0, 1),),
}
_SLOTS = {"chips": 4, "all": 8, "sib": 2}


def _slot(kind, x, y, c):
    return {"chips": 2 * x + y, "all": 4 * x + 2 * y + c, "sib": c}[kind]


def _exchange(arrs, kind, scatter, *, name):
    n = len(arrs)
    peers = _PEERS[kind]
    ns = _SLOTS[kind]
    np_ = len(peers)

    def pieces(a):
        shape = a.shape[1:] if scatter else a.shape
        if len(shape) == 2:
            for k in (XCHG_CHUNKS, XCHG_CHUNKS // 2, XCHG_CHUNKS // 4):
                if k > 1 and shape[0] % (16 * k) == 0 and shape[0] * shape[1] * a.dtype.itemsize >= k * XCHG_MIN_BYTES:
                    return [(i * (shape[0] // k), shape[0] // k) for i in range(k)]
        return [None]

    chunks = [pieces(a) for a in arrs]
    base = [sum(len(c) for c in chunks[:a]) * np_ for a in range(n)]
    total = sum(len(c) for c in chunks) * np_

    def body(*refs):
        ins, outs = refs[:n], refs[n:2 * n]
        send, recv, loc = refs[2 * n:]
        x, y, c = lax.axis_index("x"), lax.axis_index("y"), lax.axis_index("c")
        me = _slot(kind, x, y, c)
        copies = []
        for a in range(n):
            own = pltpu.make_async_copy(ins[a].at[me] if scatter else ins[a], outs[a].at[me], loc.at[a])
            own.start()
            copies.append(own)
            for j, (dx, dy, dc) in enumerate(peers):
                px, py, pc = (1 - x if dx else x), (1 - y if dy else y), (1 - c if dc else c)
                src = ins[a].at[_slot(kind, px, py, pc)] if scatter else ins[a]
                for i, piece in enumerate(chunks[a]):
                    rows = slice(None) if piece is None else pl.ds(piece[0], piece[1])
                    sem = base[a] + j * len(chunks[a]) + i
                    cp = pltpu.make_async_remote_copy(
                        src_ref=src if piece is None else src.at[rows], dst_ref=outs[a].at[me] if piece is None else outs[a].at[me, rows],
                        send_sem=send.at[sem], recv_sem=recv.at[sem], device_id=(px, py, pc), device_id_type=MESH_ID)
                    cp.start()
                    copies.append(cp)
        for cp in copies:
            cp.wait()

    hbm = pl.BlockSpec(memory_space=pl.ANY)
    shapes = [jax.ShapeDtypeStruct(a.shape if scatter else (ns,) + a.shape, a.dtype) for a in arrs]
    return _pcall(
        body, name=name, in_specs=[hbm] * n, out_specs=[hbm] * n, out_shape=shapes,
        scratch_shapes=[pltpu.SemaphoreType.DMA((total,)), pltpu.SemaphoreType.DMA((total,)), pltpu.SemaphoreType.DMA((n,))],
    )(*arrs)


def _sum_slots(a, *, name, out_dtype=F32):
    ns, r, c = a.shape
    t = _tile(r, max(16, (1 << 18) // c // 16 * 16), 16)

    def body(a_ref, o_ref):
        acc = a_ref[0].astype(F32)
        for i in range(1, ns):
            acc = acc + a_ref[i].astype(F32)
        o_ref[...] = acc.astype(o_ref.dtype)

    return _pcall(body, name=name, grid=(r // t,), in_specs=[pl.BlockSpec((ns, t, c), lambda i: (0, i, 0))],
                  out_specs=pl.BlockSpec((t, c), lambda i: (i, 0)), out_shape=jax.ShapeDtypeStruct((r, c), out_dtype),
                  compiler_params=_cparams("parallel"))(a)


def _adamw(w, gparts, m, v, *, name):
    r, c = w.shape
    t = _tile(r, max(16, (1 << 17) // c // 16 * 16), 16)
    ng = len(gparts)

    def body(*refs):
        w_ref, m_ref, v_ref = refs[0], refs[1 + ng], refs[2 + ng]
        g_ref, d_ref, nm_ref, nv_ref = refs[3 + ng:]
        g = refs[1][...].astype(F32)
        for i in range(1, ng):
            g = g + refs[1 + i][...].astype(F32)
        mn = ADAM_B1 * m_ref[...] + (1.0 - ADAM_B1) * g
        vn = ADAM_B2 * v_ref[...] + (1.0 - ADAM_B2) * (g * g)
        m_hat = mn / (1.0 - ADAM_B1 ** ADAM_STEP)
        v_hat = vn / (1.0 - ADAM_B2 ** ADAM_STEP)
        g_ref[...] = g
        d_ref[...] = -ADAM_LR * (m_hat / (jnp.sqrt(v_hat) + ADAM_EPS) + ADAM_WD * w_ref[...])
        nm_ref[...] = mn
        nv_ref[...] = vn

    spec = pl.BlockSpec((t, c), lambda i: (i, 0))
    return _pcall(body, name=name, grid=(r // t,), in_specs=[spec] * (3 + ng), out_specs=[spec] * 4,
                  out_shape=[jax.ShapeDtypeStruct((r, c), F32)] * 4, compiler_params=_cparams("parallel"))(w, *gparts, m, v)


FFN_TM = 512
FFN_CHUNK = 1408


def _resident(shape):
    return pl.BlockSpec(shape, lambda i: (0,) * len(shape), pipeline_mode=pl.Buffered(1))


def _ffn_up(x, sc, sh, wgu, *, name):
    s, d = x.shape
    f = wgu.shape[1] // 2
    t, fc = _rows(s, FFN_TM), _tile(f, FFN_CHUNK)

    def body(x_ref, sc_ref, sh_ref, w_ref, h_ref, uv_ref, a_ref):
        xv = x_ref[...]
        r = lax.rsqrt(jnp.mean(xv * xv, axis=1, keepdims=True) + EPS)
        hb = ((xv * r) * (1.0 + sc_ref[...]) + sh_ref[...]).astype(BF16)
        h_ref[...] = hb
        for j in range(f // fc):
            u = _dot(hb, w_ref[:, j * fc:(j + 1) * fc])
            v = _dot(hb, w_ref[:, f + j * fc:f + (j + 1) * fc])
            sg = _sigmoid(u)
            silu = u * sg
            uv_ref[:, j * fc:(j + 1) * fc] = (v * (sg * (1.0 + u * (1.0 - sg)))).astype(BF16)
            uv_ref[:, f + j * fc:f + (j + 1) * fc] = silu.astype(BF16)
            a_ref[:, j * fc:(j + 1) * fc] = (silu * v).astype(BF16)

    return _pcall(
        body, name=name, grid=(s // t,), in_specs=[_row_spec(t, d), _vec_spec(d), _vec_spec(d), _resident(wgu.shape)],
        out_specs=[_row_spec(t, d), _row_spec(t, 2 * f), _row_spec(t, f)],
        out_shape=[jax.ShapeDtypeStruct((s, d), BF16), jax.ShapeDtypeStruct((s, 2 * f), BF16), jax.ShapeDtypeStruct((s, f), BF16)],
        compiler_params=_cparams("parallel"))(x, sc, sh, wgu)


def _norm_mm(x, sc, sh, w, nb, *, name):
    s, d = x.shape
    n = w.shape[1]
    t, nc = _rows(s, FFN_TM), _tile(n, 1792)
    assert nb <= nc

    def body(x_ref, sc_ref, sh_ref, w_ref, h_ref, o_ref, ob_ref):
        xv = x_ref[...]
        r = lax.rsqrt(jnp.mean(xv * xv, axis=1, keepdims=True) + EPS)
        hb = ((xv * r) * (1.0 + sc_ref[...]) + sh_ref[...]).astype(BF16)
        h_ref[...] = hb
        for j in range(n // nc):
            part = _dot(hb, w_ref[:, j * nc:(j + 1) * nc])
            o_ref[:, j * nc:(j + 1) * nc] = part
            if j == 0:
                ob_ref[...] = part[:, :nb].astype(BF16)

    return _pcall(
        body, name=name, grid=(s // t,), in_specs=[_row_spec(t, d), _vec_spec(d), _vec_spec(d), _resident(w.shape)],
        out_specs=[_row_spec(t, d), _row_spec(t, n), _row_spec(t, nb)],
        out_shape=[jax.ShapeDtypeStruct((s, d), BF16), jax.ShapeDtypeStruct((s, n), F32), jax.ShapeDtypeStruct((s, nb), BF16)],
        compiler_params=_cparams("parallel"))(x, sc, sh, w)


def _ffn_dact(dxo, y, sg, wd, uv, *, name):
    s, d = y.shape
    f = wd.shape[0]
    t, fc = _rows(s, FFN_TM), _tile(f, FFN_CHUNK)

    def body(dxo_ref, y_ref, sg_ref, w_ref, uv_ref, dy_ref, duv_ref, ds_ref):
        @pl.when(pl.program_id(0) == 0)
        def _():
            ds_ref[...] = jnp.zeros_like(ds_ref)

        dv = dxo_ref[...]
        dyb = (sg_ref[...] * dv).astype(BF16)
        dy_ref[...] = dyb
        ds_ref[...] += jnp.sum(dv * y_ref[...], axis=0, keepdims=True)
        for j in range(f // fc):
            da = _dot(dyb, w_ref[j * fc:(j + 1) * fc, :], NT)
            duv_ref[:, j * fc:(j + 1) * fc] = (da * uv_ref[:, j * fc:(j + 1) * fc].astype(F32)).astype(BF16)
            duv_ref[:, f + j * fc:f + (j + 1) * fc] = (da * uv_ref[:, f + j * fc:f + (j + 1) * fc].astype(F32)).astype(BF16)

    return _pcall(
        body, name=name, grid=(s // t,),
        in_specs=[_row_spec(t, d), _row_spec(t, d), _vec_spec(d), _resident(wd.shape), _row_spec(t, 2 * f)],
        out_specs=[_row_spec(t, d), _row_spec(t, 2 * f), _vec_spec(d)],
        out_shape=[jax.ShapeDtypeStruct((s, d), BF16), jax.ShapeDtypeStruct((s, 2 * f), BF16), jax.ShapeDtypeStruct((1, d), F32)],
        compiler_params=_cparams("arbitrary"))(dxo, y, sg, wd, uv)


def _ffn_dh(duv, wgu, x, sc, dxo, *, name):
    s, d = x.shape
    f2 = wgu.shape[1]
    t = _rows(s, FFN_TM)

    def body(duv_ref, w_ref, x_ref, sc_ref, dxo_ref, dx_ref, dsc_ref, dsh_ref):
        @pl.when(pl.program_id(0) == 0)
        def _():
            dsc_ref[...] = jnp.zeros_like(dsc_ref)
            dsh_ref[...] = jnp.zeros_like(dsh_ref)

        dhv = _dot(duv_ref[...], w_ref[...], NT)
        xv = x_ref[...]
        r = lax.rsqrt(jnp.mean(xv * xv, axis=1, keepdims=True) + EPS)
        xn = xv * r
        dxn = dhv * (1.0 + sc_ref[...])
        dx_ref[...] = dxo_ref[...] + r * (dxn - xn * jnp.mean(dxn * xn, axis=1, keepdims=True))
        dsc_ref[...] += jnp.sum(dhv * xn, axis=0, keepdims=True)
        dsh_ref[...] += jnp.sum(dhv, axis=0, keepdims=True)

    return _pcall(
        body, name=name, grid=(s // t,),
        in_specs=[_row_spec(t, f2), _resident(wgu.shape), _row_spec(t, d), _vec_spec(d), _row_spec(t, d)],
        out_specs=[_row_spec(t, d), _vec_spec(d), _vec_spec(d)],
        out_shape=[jax.ShapeDtypeStruct((s, d), F32), jax.ShapeDtypeStruct((1, d), F32), jax.ShapeDtypeStruct((1, d), F32)],
        compiler_params=_cparams("arbitrary"))(duv, wgu, x, sc, dxo)


def _dh_pieces(pieces, w, x, sc, dxo, *, name):
    s, d = x.shape
    t = _rows(s, FFN_TM)
    widths = [p.shape[1] for p in pieces]
    offs = [sum(widths[:i]) for i in range(len(widths))]
    kd = sum(widths)
    npc = len(pieces)

    def body(*refs):
        p_refs = refs[:npc]
        w_ref, x_ref, sc_ref, dxo_ref, dx_ref, dsc_ref, dsh_ref, cat_ref = refs[npc:]

        @pl.when(pl.program_id(0) == 0)
        def _():
            dsc_ref[...] = jnp.zeros_like(dsc_ref)
            dsh_ref[...] = jnp.zeros_like(dsh_ref)

        dhv = None
        for p_ref, off, wd in zip(p_refs, offs, widths):
            pb = p_ref[...].astype(BF16)
            cat_ref[:, off:off + wd] = pb
            part = _dot(pb, w_ref[:, off:off + wd], NT)
            dhv = part if dhv is None else dhv + part
        xv = x_ref[...]
        r = lax.rsqrt(jnp.mean(xv * xv, axis=1, keepdims=True) + EPS)
        xn = xv * r
        dxn = dhv * (1.0 + sc_ref[...])
        dx_ref[...] = dxo_ref[...] + r * (dxn - xn * jnp.mean(dxn * xn, axis=1, keepdims=True))
        dsc_ref[...] += jnp.sum(dhv * xn, axis=0, keepdims=True)
        dsh_ref[...] += jnp.sum(dhv, axis=0, keepdims=True)

    return _pcall(
        body, name=name, grid=(s // t,),
        in_specs=[_row_spec(t, wd) for wd in widths] + [_resident(w.shape), _row_spec(t, d), _vec_spec(d), _row_spec(t, d)],
        out_specs=[_row_spec(t, d), _vec_spec(d), _vec_spec(d), _row_spec(t, kd)],
        out_shape=[jax.ShapeDtypeStruct((s, d), F32), jax.ShapeDtypeStruct((1, d), F32), jax.ShapeDtypeStruct((1, d), F32),
                   jax.ShapeDtypeStruct((s, kd), BF16)],
        compiler_params=_cparams("arbitrary"))(*pieces, w, x, sc, dxo)


def _ffn_fwd(x, sh, sc, g, wgu, wd, tag):
    h, uv, a = _ffn_up(x, sc, sh, wgu, name=f"{tag}_up")
    y, xo = _mm(a, wd, name=f"{tag}_down", tm=512, tn=1024, tk=wd.shape[0], res=x, scale=0.5 * g)
    return xo, (x, h, uv, a, y)


def _ffn_bwd(dxo, saved, sc, g, wgu, wd, tag):
    x, h, uv, a, y = saved
    dyb, duv, dgs = _ffn_dact(dxo, y, 0.5 * g, wd, uv, name=f"{tag}_dact")
    dx, dsc, dsh = _ffn_dh(duv, wgu, x, sc, dxo, name=f"{tag}_dh")
    dwgu = _mm_tn(h, duv, name=f"{tag}_dwgu", tm=1024, tn=1408, tk=512)
    dwd = _mm_tn(a, dyb, name=f"{tag}_dwd", tm=1408, tn=1024, tk=512)
    return dx, dwgu, dwd, dsh, dsc, 0.5 * dgs


def _layer_fwd(x0, mod, w, par, tag):
    s = x0.shape[0]
    sh1, sc1, g1, sh2, sc2, g2, sh3, sc3, g3 = (mod[i:i + 1] for i in range(N_MOD))
    x1, f1 = _ffn_fwd(x0, sh1, sc1, g1, w["gu1"], w["d1"], f"{tag}_ffn1")
    h2, proj, qkv = _norm_mm(x1, sc2, sh2, w["in"], 3 * SB_W + 3 * DIL_W, name=f"{tag}_in")
    vd = qkv[:, 3 * SB_W + 2 * DIL_W:]
    o_a = _sb_fwd(qkv, name=f"{tag}_sb")
    qd, kd = _dil_prep(proj, par["gq"], par["gk"], par["cos"], par["sin"], name=f"{tag}_dil_prep")
    nums, dens, mxs = [], [], []
    for _, r in DIL_PATTERNS:
        nu, de, mx = _dil_fwd(qd, kd, vd, r, name=f"{tag}_dil{r}")
        nums.append(nu.reshape(s, DIL_W))
        dens.append(de.reshape(s, DIL_W))
        mxs.append(mx.reshape(s, DIL_W))
    oh, st = _hgrn_fwd(proj, par["la"], par["lc"], name=f"{tag}_hgrn")
    ymix, od, mall, zall = _mix_out(o_a, nums, dens, mxs, oh, proj, par["hg"], name=f"{tag}_mix_out")
    out, x2 = _mm(ymix, w["out"], name=f"{tag}_out", tm=512, tn=1024, tk=1024, res=x1, scale=g2)
    x3, f2 = _ffn_fwd(x2, sh3, sc3, g3, w["gu2"], w["d2"], f"{tag}_ffn2")
    return x3, dict(f1=f1, f2=f2, x1=x1, h2=h2, proj=proj, qkv=qkv, vd=vd, o_a=o_a, qd=qd, kd=kd, oh=oh, st=st,
                    ymix=ymix, od=od, mall=mall, zall=zall, out=out)


def _layer_bwd(dx3, sv, mod, w, par, tag):
    s = dx3.shape[0]
    sh1, sc1, g1, sh2, sc2, g2, sh3, sc3, g3 = (mod[i:i + 1] for i in range(N_MOD))
    dx2, dwgu2, dwd2, dsh3, dsc3, dg3 = _ffn_bwd(dx3, sv["f2"], sc3, g3, w["gu2"], w["d2"], f"{tag}_ffn2")
    doutb, dg2 = _gate_bwd(dx2, sv["out"], g2, name=f"{tag}_dgate2")
    dmix = _mm(doutb, w["out"], name=f"{tag}_dmix", tb=True, tm=512, tn=1024, tk=1024)
    dwout = _mm_tn(sv["ymix"], doutb, name=f"{tag}_dwout", tm=1024, tn=1024, tk=512)
    proj = sv["proj"]
    doh, dgh, delta, dhg = _mix_out_bwd(dmix, sv["oh"], proj, par["hg"], sv["od"], name=f"{tag}_dmix_out")
    dqa, dka, dva = _sb_bwd(sv["qkv"], sv["o_a"], dmix, name=f"{tag}_dsb")
    do_d = dmix[:, SB_W:SB_W + DIL_W]
    dqs, dks, dvs = [], [], []
    for _, r in DIL_PATTERNS:
        a, b, c = _dil_bwd(sv["qd"], sv["kd"], sv["vd"], do_d, sv["mall"], sv["zall"], delta, r, name=f"{tag}_ddil{r}")
        dqs.append(a.reshape(s, DIL_W))
        dks.append(b.reshape(s, DIL_W))
        dvs.append(c.reshape(s, DIL_W))
    dqd, dkd, dvd, dgq, dgk = _dil_prep_bwd(proj, par["gq"], par["gk"], par["cos"], par["sin"], dqs, dks, dvs,
                                             name=f"{tag}_ddil_prep")
    dqh, dfh, dih, dla, dlc = _hgrn_bwd(proj, par["la"], par["lc"], sv["st"], doh, name=f"{tag}_dhgrn")
    dx1, dsc2, dsh2, dproj = _dh_pieces([dqa, dka, dva, dqd, dkd, dvd, dqh, dfh, dih, dgh], w["in"], sv["x1"], sc2, dx2,
                                         name=f"{tag}_dh2")
    dwin = _mm_tn(sv["h2"], dproj, name=f"{tag}_dwin", tm=1024, tn=1792, tk=512)
    dx0, dwgu1, dwd1, dsh1, dsc1, dg1 = _ffn_bwd(dx1, sv["f1"], sc1, g1, w["gu1"], w["d1"], f"{tag}_ffn1")
    dmod = jnp.concatenate([dsh1, dsc1, dg1, dsh2, dsc2, dg2, dsh3, dsc3, dg3], axis=0)
    fold = lambda v: v.reshape(2, HEAD_DIM).sum(axis=0)
    grads = dict(gu1=dwgu1, d1=dwd1, gu2=dwgu2, d2=dwd2, win=dwin, wout=dwout, dmod=dmod, gq=fold(dgq), gk=fold(dgk),
                 hg=dhg[0], la=dla[0], lc=dlc[0])
    return dx0, grads


def _pack(pieces):
    flat = jnp.concatenate([p.reshape(-1) for p in pieces])
    pad = (-flat.shape[0]) % (8 * LANES)
    return jnp.pad(flat, (0, pad)).reshape(-1, LANES)


def _unpack(flat, like):
    out, off = [], 0
    for p in like:
        out.append(flat[off:off + p.size].reshape(p.shape))
        off += p.size
    return out


def kernel(x, c, w_mod, b_mod, ffn1_w_gate, ffn1_w_up, ffn1_w_down, w_in, w_out, q_norm_g, k_norm_g, hgrn_norm_g, hgrn_lb_logits, ffn2_w_gate, ffn2_w_up, ffn2_w_down, loss_target, m_w_mod, m_b_mod, m_ffn1_w_gate, m_ffn1_w_up, m_ffn1_w_down, m_w_in, m_w_out, m_q_norm_g, m_k_norm_g, m_hgrn_norm_g, m_hgrn_lb_logits, m_ffn2_w_gate, m_ffn2_w_up, m_ffn2_w_down, v_w_mod, v_b_mod, v_ffn1_w_gate, v_ffn1_w_up, v_ffn1_w_down, v_w_in, v_w_out, v_q_norm_g, v_k_norm_g, v_hgrn_norm_g, v_hgrn_lb_logits, v_ffn2_w_gate, v_ffn2_w_up, v_ffn2_w_down):
    names = ["w_mod", "b_mod", "ffn1_w_gate", "ffn1_w_up", "ffn1_w_down", "w_in", "w_out", "q_norm_g", "k_norm_g",
             "hgrn_norm_g", "hgrn_lb_logits", "ffn2_w_gate", "ffn2_w_up", "ffn2_w_down"]
    wts = dict(zip(names, (w_mod, b_mod, ffn1_w_gate, ffn1_w_up, ffn1_w_down, w_in, w_out, q_norm_g, k_norm_g, hgrn_norm_g,
                           hgrn_lb_logits, ffn2_w_gate, ffn2_w_up, ffn2_w_down)))
    mom = dict(zip(names, (m_w_mod, m_b_mod, m_ffn1_w_gate, m_ffn1_w_up, m_ffn1_w_down, m_w_in, m_w_out, m_q_norm_g, m_k_norm_g,
                           m_hgrn_norm_g, m_hgrn_lb_logits, m_ffn2_w_gate, m_ffn2_w_up, m_ffn2_w_down)))
    var = dict(zip(names, (v_w_mod, v_b_mod, v_ffn1_w_gate, v_ffn1_w_up, v_ffn1_w_down, v_w_in, v_w_out, v_q_norm_g, v_k_norm_g,
                           v_hgrn_norm_g, v_hgrn_lb_logits, v_ffn2_w_gate, v_ffn2_w_up, v_ffn2_w_down)))
    depth = w_mod.shape[0]
    assert depth == 2 and x.shape[0] == 1
    s, d = x.shape[1:]
    assert s % (DIL_PATTERNS[-1][1] * QBLK) == 0 and d % LANES == 0
    xi, yi, ci = lax.axis_index("x"), lax.axis_index("y"), lax.axis_index("c")
    chip = 2 * xi + yi
    dev = 2 * chip + ci
    x0, tgt = x[0], loss_target[0]

    c8 = _exchange([c.reshape(d // LANES, LANES)], "all", False, name="gather_c")[0].reshape(8, d)
    ncol = w_mod.shape[2]
    b_loc = lax.dynamic_slice_in_dim(b_mod, chip * ncol, ncol, axis=1)
    m_loc = _mod_fwd(c8, w_mod, b_loc.reshape(depth, 1, ncol), name="mod_fwd")
    m_all = _exchange([m_loc], "chips", False, name="gather_mod")[0]
    mod = jnp.transpose(lax.dynamic_index_in_dim(m_all, dev, axis=2, keepdims=False), (1, 0, 2)).reshape(depth, N_MOD, d)

    col_sharded = ["ffn1_w_gate", "ffn1_w_up", "w_in", "ffn2_w_gate", "ffn2_w_up"]
    row_sharded = ["ffn1_w_down", "w_out", "ffn2_w_down"]
    big = col_sharded + row_sharded
    flat = [wts[n].astype(BF16).reshape(-1, wts[n].shape[-1]) for n in big]
    gathered = {n: g.reshape((4,) + wts[n].shape) for n, g in zip(big, _exchange(flat, "chips", False, name="gather_w"))}
    full = {}
    for n in col_sharded:
        g = gathered[n]
        full[n] = jnp.moveaxis(g, 0, 2).reshape(depth, g.shape[2], -1)
    for n in row_sharded:
        g = gathered[n]
        full[n] = jnp.moveaxis(g, 0, 1).reshape(depth, -1, g.shape[3])
    ws = [dict(gu1=jnp.concatenate([full["ffn1_w_gate"][l], full["ffn1_w_up"][l]], axis=1), d1=full["ffn1_w_down"][l],
               gu2=jnp.concatenate([full["ffn2_w_gate"][l], full["ffn2_w_up"][l]], axis=1), d2=full["ffn2_w_down"][l],
               **{"in": full["w_in"][l], "out": full["w_out"][l]}) for l in range(depth)]

    _, la, lc = _lb_prep(hgrn_lb_logits, name="lb_prep")
    cos, sin = _rope_tables(s)
    pars = [dict(gq=jnp.tile(q_norm_g[l], 2)[None], gk=jnp.tile(k_norm_g[l], 2)[None], hg=hgrn_norm_g[l][None],
                 la=la[l:l + 1], lc=lc[l:l + 1], cos=cos, sin=sin) for l in range(depth)]

    xs, saved = x0, []
    for l in range(depth):
        xs, sv = _layer_fwd(xs, mod[l], ws[l], pars[l], f"l{l}")
        saved.append(sv)
    dx, lpart = _loss_grad(xs, tgt, name="loss")
    grads = [None] * depth
    for l in reversed(range(depth)):
        dx, grads[l] = _layer_bwd(dx, saved[l], mod[l], ws[l], pars[l], f"l{l}")

    stack = lambda k: jnp.stack([grads[l][k] for l in range(depth)])
    small = [stack("dmod"), stack("gq"), stack("gk"), stack("hg"), stack("la"), stack("lc"), lpart[0, :1]]
    packed = _pack(small)
    allp = _exchange([packed], "all", False, name="gather_small")[0]
    tot = _unpack(_sum_slots(allp, name="sum_small").reshape(-1), small)
    g_b_mod = tot[0].reshape(depth, N_MOD * d)
    loss = tot[6][0]
    g_small = {"b_mod": g_b_mod, "q_norm_g": tot[1], "k_norm_g": tot[2], "hgrn_norm_g": tot[3],
               "hgrn_lb_logits": _lb_bwd(hgrn_lb_logits, tot[4], tot[5], name="lb_bwd")}

    dm_all = allp.reshape(8, -1)[:, :depth * N_MOD * d].reshape(8, depth, N_MOD * d)
    dm_loc = jnp.transpose(lax.dynamic_slice_in_dim(dm_all, chip * ncol, ncol, axis=2), (1, 0, 2))
    g_w_mod = _mod_bwd(c8.T, dm_loc, name="mod_bwd")

    fgrad = {
        "ffn1_w_gate": jnp.stack([grads[l]["gu1"][:, :grads[l]["gu1"].shape[1] // 2] for l in range(depth)]),
        "ffn1_w_up": jnp.stack([grads[l]["gu1"][:, grads[l]["gu1"].shape[1] // 2:] for l in range(depth)]),
        "ffn2_w_gate": jnp.stack([grads[l]["gu2"][:, :grads[l]["gu2"].shape[1] // 2] for l in range(depth)]),
        "ffn2_w_up": jnp.stack([grads[l]["gu2"][:, grads[l]["gu2"].shape[1] // 2:] for l in range(depth)]),
        "w_in": stack("win"), "ffn1_w_down": stack("d1"), "ffn2_w_down": stack("d2"), "w_out": stack("wout"),
    }
    by_chip = []
    for n in big:
        g = fgrad[n]
        if n in col_sharded:
            g = jnp.moveaxis(g.reshape(depth, g.shape[1], 4, -1), 2, 0)
        else:
            g = jnp.moveaxis(g.reshape(depth, 4, -1, g.shape[2]), 1, 0)
        by_chip.append(g.reshape(4, -1, g.shape[-1]))
    got = _exchange(by_chip, "chips", True, name="scatter_grads")
    parts = [_sum_slots(g, name=f"sum_{n}", out_dtype=BF16) for n, g in zip(big, got)]
    both = dict(zip(big, _exchange(parts, "sib", False, name="swap_grads")))

    outs = {}
    for n in names:
        w2 = wts[n].reshape(-1, wts[n].shape[-1])
        if n in both:
            gp = [both[n][0], both[n][1]]
        elif n == "w_mod":
            gp = [g_w_mod.reshape(w2.shape)]
        else:
            gp = [g_small[n].reshape(w2.shape)]
        res = _adamw(w2, gp, mom[n].reshape(w2.shape), var[n].reshape(w2.shape), name=f"adamw_{n}")
        outs[n] = [r.reshape(wts[n].shape) for r in res]
    return (loss, dx[None], *[outs[n][0] for n in names], *[outs[n][1] for n in names], *[outs[n][2] for n in names],
            *[outs[n][3] for n in names])
```

```python
import functools
import math

import jax
import jax.numpy as jnp
from jax import lax
from jax.experimental import pallas as pl
from jax.experimental.pallas import tpu as pltpu

F32 = jnp.float32
BF16 = jnp.bfloat16
MESH_ID = pl.DeviceIdType.MESH

HEAD_DIM = 64
SB_W = 256
DIL_W = 256
HG_W = 512
HG_D = 128
IN_W = 3 * SB_W + 3 * DIL_W + 4 * HG_W
MIX_W = SB_W + DIL_W + HG_W
DIL_PATTERNS = ((128, 1), (512, 4), (2048, 16))
ROPE_THETA = 10000.0
EPS = 1e-6
LB_FLOOR = 1e-30
NEG_BIG = -1e30
N_MOD = 9
ADAM_LR = 0.001
ADAM_B1 = 0.9
ADAM_B2 = 0.999
ADAM_EPS = 1e-08
ADAM_WD = 0.01
ADAM_STEP = 10

LANES = 128
QBLK = 128
DIL_TILE = 512
HG_BLK = 16
HG_TILE = 256
SB_EXIT = 88.0
VMEM_LIMIT = 48 * 1024 * 1024
XCHG_CHUNKS = 8
XCHG_MIN_BYTES = 1 << 19

NN = (((1,), (0,)), ((), ()))
NT = (((1,), (1,)), ((), ()))
TN = (((0,), (0,)), ((), ()))


def _pcall(body, **kw):
    return pl.pallas_call(body, **kw)


def _cparams(*sem):
    return pltpu.CompilerParams(dimension_semantics=sem if sem else None, vmem_limit_bytes=VMEM_LIMIT)


def _dot(a, b, dims=NN):
    return lax.dot_general(a, b, dims, preferred_element_type=F32)


def _split(x, n):
    parts = []
    r = x
    for i in range(n):
        p = r.astype(BF16)
        parts.append(p)
        if i + 1 < n:
            r = r - p.astype(F32)
    return parts


def _xdot(x, m, n=2):
    return sum(_dot(p, m) for p in _split(x, n))


def _xdot_left(m, x, n=3):
    return sum(_dot(m, p) for p in _split(x, n))


def _iota(shape, dim):
    return lax.broadcasted_iota(jnp.int32, shape, dim)


def _sigmoid(x):
    return 1.0 / (1.0 + jnp.exp(-x))


def _tile(dim, pref, mult=LANES):
    t = (min(pref, dim) // mult) * mult
    while t >= mult:
        if dim % t == 0:
            return t
        t -= mult
    return dim


def _rows(dim, pref):
    return _tile(dim, pref, 8)


def _mm(a, b, *, name, tb=False, tm=512, tn=1024, tk=1024, out_dtype=F32, res=None, scale=None):
    m, kd = a.shape
    n = b.shape[0] if tb else b.shape[1]
    tm, tn, tk = _rows(m, tm), _tile(n, tn), _tile(kd, tk)
    nk = kd // tk
    epi = res is not None

    def body(*refs):
        if epi:
            a_ref, b_ref, r_ref, s_ref, o_ref, x_ref, acc = refs
        else:
            a_ref, b_ref, o_ref, acc = refs
        k = pl.program_id(2)

        @pl.when(k == 0)
        def _():
            acc[...] = jnp.zeros_like(acc)

        acc[...] += _dot(a_ref[...], b_ref[...], NT if tb else NN)

        @pl.when(k == nk - 1)
        def _():
            o_ref[...] = acc[...].astype(o_ref.dtype)
            if epi:
                x_ref[...] = r_ref[...] + s_ref[...] * acc[...]

    in_specs = [
        pl.BlockSpec((tm, tk), lambda i, j, k: (i, k)),
        pl.BlockSpec((tn, tk), lambda i, j, k: (j, k)) if tb else pl.BlockSpec((tk, tn), lambda i, j, k: (k, j)),
    ]
    out_shape = [jax.ShapeDtypeStruct((m, n), out_dtype)]
    out_specs = [pl.BlockSpec((tm, tn), lambda i, j, k: (i, j))]
    args = [a, b]
    if epi:
        in_specs += [pl.BlockSpec((tm, tn), lambda i, j, k: (i, j)), pl.BlockSpec((1, tn), lambda i, j, k: (0, j))]
        out_shape.append(jax.ShapeDtypeStruct((m, n), F32))
        out_specs.append(pl.BlockSpec((tm, tn), lambda i, j, k: (i, j)))
        args += [res, scale]
    out = _pcall(
        body, name=name, grid=(m // tm, n // tn, nk), in_specs=in_specs, out_specs=out_specs, out_shape=out_shape,
        scratch_shapes=[pltpu.VMEM((tm, tn), F32)], compiler_params=_cparams("parallel", "parallel", "arbitrary"),
    )(*args)
    return out if epi else out[0]


def _mm_tn(a, b, *, name, tm=1024, tn=1408, tk=512, out_dtype=BF16):
    s, m = a.shape
    n = b.shape[1]
    tm, tn, tk = _tile(m, tm), _tile(n, tn), _rows(s, tk)
    nk = s // tk

    def body(a_ref, b_ref, o_ref, acc):
        k = pl.program_id(2)

        @pl.when(k == 0)
        def _():
            acc[...] = jnp.zeros_like(acc)

        acc[...] += _dot(a_ref[...], b_ref[...], TN)

        @pl.when(k == nk - 1)
        def _():
            o_ref[...] = acc[...].astype(o_ref.dtype)

    return _pcall(
        body, name=name, grid=(m // tm, n // tn, nk),
        in_specs=[pl.BlockSpec((tk, tm), lambda i, j, k: (k, i)), pl.BlockSpec((tk, tn), lambda i, j, k: (k, j))],
        out_specs=pl.BlockSpec((tm, tn), lambda i, j, k: (i, j)), out_shape=jax.ShapeDtypeStruct((m, n), out_dtype),
        scratch_shapes=[pltpu.VMEM((tm, tn), F32)], compiler_params=_cparams("parallel", "parallel", "arbitrary"),
    )(a, b)


TE = 512


def _row_spec(t, w, col=0):
    return pl.BlockSpec((t, w), lambda i, col=col: (i, col))


def _vec_spec(w, col=0):
    return pl.BlockSpec((1, w), lambda i, col=col: (0, col))


def _norm_mod(x, sc, sh, *, name):
    s, d = x.shape
    t = _rows(s, TE)

    def body(x_ref, sc_ref, sh_ref, h_ref):
        xv = x_ref[...]
        r = lax.rsqrt(jnp.mean(xv * xv, axis=1, keepdims=True) + EPS)
        h_ref[...] = ((xv * r) * (1.0 + sc_ref[...]) + sh_ref[...]).astype(BF16)

    return _pcall(body, name=name, grid=(s // t,), in_specs=[_row_spec(t, d), _vec_spec(d), _vec_spec(d)],
                  out_specs=_row_spec(t, d), out_shape=jax.ShapeDtypeStruct((s, d), BF16),
                  compiler_params=_cparams("parallel"))(x, sc, sh)


def _norm_mod_bwd(dh, x, sc, dxo, *, name):
    s, d = x.shape
    t = _rows(s, TE)

    def body(dh_ref, x_ref, sc_ref, dxo_ref, dx_ref, dsc_ref, dsh_ref):
        @pl.when(pl.program_id(0) == 0)
        def _():
            dsc_ref[...] = jnp.zeros_like(dsc_ref)
            dsh_ref[...] = jnp.zeros_like(dsh_ref)

        xv = x_ref[...]
        dhv = dh_ref[...]
        r = lax.rsqrt(jnp.mean(xv * xv, axis=1, keepdims=True) + EPS)
        xn = xv * r
        dxn = dhv * (1.0 + sc_ref[...])
        dx_ref[...] = dxo_ref[...] + r * (dxn - xn * jnp.mean(dxn * xn, axis=1, keepdims=True))
        dsc_ref[...] += jnp.sum(dhv * xn, axis=0, keepdims=True)
        dsh_ref[...] += jnp.sum(dhv, axis=0, keepdims=True)

    return _pcall(
        body, name=name, grid=(s // t,),
        in_specs=[_row_spec(t, d), _row_spec(t, d), _vec_spec(d), _row_spec(t, d)],
        out_specs=[_row_spec(t, d), _vec_spec(d), _vec_spec(d)],
        out_shape=[jax.ShapeDtypeStruct((s, d), F32), jax.ShapeDtypeStruct((1, d), F32), jax.ShapeDtypeStruct((1, d), F32)],
        compiler_params=_cparams("arbitrary"))(dh, x, sc, dxo)


def _swiglu(uv, *, name):
    s, f2 = uv.shape
    f = f2 // 2
    t = _rows(s, 256)

    def body(uv_ref, a_ref):
        u = uv_ref[:, :f]
        v = uv_ref[:, f:]
        a_ref[...] = (u * _sigmoid(u) * v).astype(BF16)

    return _pcall(body, name=name, grid=(s // t,), in_specs=[_row_spec(t, f2)], out_specs=_row_spec(t, f),
                  out_shape=jax.ShapeDtypeStruct((s, f), BF16), compiler_params=_cparams("parallel"))(uv)


def _swiglu_bwd(da, uv, *, name):
    s, f2 = uv.shape
    f = f2 // 2
    t = _rows(s, 256)

    def body(da_ref, uv_ref, d_ref):
        u = uv_ref[:, :f]
        v = uv_ref[:, f:]
        dav = da_ref[...]
        sg = _sigmoid(u)
        d_ref[:, :f] = (dav * v * (sg * (1.0 + u * (1.0 - sg)))).astype(BF16)
        d_ref[:, f:] = (dav * (u * sg)).astype(BF16)

    return _pcall(body, name=name, grid=(s // t,), in_specs=[_row_spec(t, f), _row_spec(t, f2)],
                  out_specs=_row_spec(t, f2), out_shape=jax.ShapeDtypeStruct((s, f2), BF16),
                  compiler_params=_cparams("parallel"))(da, uv)


def _gate_bwd(dxo, y, sg, *, name):
    s, d = y.shape
    t = _rows(s, TE)

    def body(dxo_ref, y_ref, sg_ref, dy_ref, ds_ref):
        @pl.when(pl.program_id(0) == 0)
        def _():
            ds_ref[...] = jnp.zeros_like(ds_ref)

        dv = dxo_ref[...]
        dy_ref[...] = (sg_ref[...] * dv).astype(BF16)
        ds_ref[...] += jnp.sum(dv * y_ref[...], axis=0, keepdims=True)

    return _pcall(
        body, name=name, grid=(s // t,), in_specs=[_row_spec(t, d), _row_spec(t, d), _vec_spec(d)],
        out_specs=[_row_spec(t, d), _vec_spec(d)],
        out_shape=[jax.ShapeDtypeStruct((s, d), BF16), jax.ShapeDtypeStruct((1, d), F32)],
        compiler_params=_cparams("arbitrary"))(dxo, y, sg)


def _loss_grad(y, tgt, *, name):
    s, d = y.shape
    t = _rows(s, TE)
    nt = s // t

    def body(y_ref, t_ref, dy_ref, l_ref, acc):
        i = pl.program_id(0)

        @pl.when(i == 0)
        def _():
            acc[...] = jnp.zeros_like(acc)

        e = y_ref[...] - t_ref[...]
        dy_ref[...] = e * (1.0 / d)
        acc[...] += jnp.sum(e * e, axis=0, keepdims=True)

        @pl.when(i == nt - 1)
        def _():
            l_ref[...] = jnp.broadcast_to(jnp.sum(acc[...], axis=1, keepdims=True) * (0.5 / d), l_ref.shape)

    return _pcall(
        body, name=name, grid=(nt,), in_specs=[_row_spec(t, d), _row_spec(t, d)],
        out_specs=[_row_spec(t, d), pl.BlockSpec((1, LANES), lambda i: (0, 0))],
        out_shape=[jax.ShapeDtypeStruct((s, d), F32), jax.ShapeDtypeStruct((1, LANES), F32)],
        scratch_shapes=[pltpu.VMEM((1, d), F32)], compiler_params=_cparams("arbitrary"))(y, tgt)


SB_TQ = 256
SB_NK = SB_TQ // QBLK


def _sb_consts():
    r = _iota((QBLK, LANES), 0)
    c = _iota((QBLK, LANES), 1)
    ones = jnp.ones((QBLK, LANES), BF16)
    after = jnp.concatenate([jnp.where(r > c, 1.0, 0.0).astype(BF16), ones], axis=1)
    from_ = jnp.concatenate([jnp.where(r >= c, 1.0, 0.0).astype(BF16), ones], axis=1)
    return _iota((SB_TQ, LANES), 0), _iota((SB_TQ, LANES), 1), after, from_


def _sb_scores(qm, kb, strict):
    z = _dot(qm, kb, NT) * (HEAD_DIM ** -0.5)
    sp = jnp.log(1.0 + jnp.exp(-jnp.abs(z)))
    lnb = -(jnp.maximum(z, 0.0) + sp)
    lb = jnp.minimum(z, 0.0) - sp
    if strict is not None:
        lnb = jnp.where(strict, lnb, 0.0)
    return lnb, lb


def _sb_fwd(qkv, *, name):
    s = qkv.shape[0]
    nq = s // SB_TQ

    def body(q_ref, k_ref, v_ref, o_ref, *scr):
        acc, osc = scr[:4], scr[4:]
        qi = pl.program_id(0)
        row, lane, after, _ = _sb_consts()
        h0 = lane < HEAD_DIM
        q = q_ref[...]
        qms = []
        for p in range(2):
            qp = q[:, p * LANES:(p + 1) * LANES]
            qms += [jnp.where(h0, qp, jnp.zeros_like(qp)), jnp.where(h0, jnp.zeros_like(qp), qp)]

        def block(kj, mask):
            off = pl.multiple_of(kj * QBLK, QBLK)
            kbs = [k_ref[pl.ds(off, QBLK), p * LANES:(p + 1) * LANES] for p in range(2)]
            vbs = [v_ref[pl.ds(off, QBLK), p * LANES:(p + 1) * LANES] for p in range(2)]
            sc = [_sb_scores(qms[c], kbs[c // 2], mask) for c in range(4)]
            trs = [_xdot(sc[c][0], after) for c in range(4)]
            top = None
            for c in range(4):
                w = jnp.exp(sc[c][1] + trs[c][:, :QBLK] + acc[c][...])
                if mask is not None:
                    w = jnp.where(mask, w, 0.0)
                osc[c][...] += _xdot(w, vbs[c // 2])
                new = acc[c][...] + trs[c][:, QBLK:]
                acc[c][...] = new
                top = new if top is None else jnp.maximum(top, new)
            return jnp.max(top)

        for ref in scr:
            ref[...] = jnp.zeros_like(ref)
        top = None
        for j in reversed(range(SB_NK)):
            top = block(qi * SB_NK + j, (lane + j * QBLK) < row)
        lax.while_loop(lambda c: (c[0] >= 0) & (c[1] > -SB_EXIT), lambda c: (c[0] - 1, block(c[0], None)),
                       (qi * SB_NK - 1, top))
        for p in range(2):
            o_ref[:, p * LANES:(p + 1) * LANES] = jnp.where(h0, osc[2 * p][...], osc[2 * p + 1][...])

    return _pcall(
        body, name=name, grid=(nq,),
        in_specs=[pl.BlockSpec((SB_TQ, SB_W), lambda i: (i, 0)),
                  pl.BlockSpec((s, SB_W), lambda i: (0, 1)),
                  pl.BlockSpec((s, SB_W), lambda i: (0, 2))],
        out_specs=pl.BlockSpec((SB_TQ, SB_W), lambda i: (i, 0)),
        out_shape=jax.ShapeDtypeStruct((s, SB_W), F32),
        scratch_shapes=[pltpu.VMEM((SB_TQ, LANES), F32)] * 8,
        compiler_params=_cparams("arbitrary"))(qkv, qkv, qkv)


def _sb_bwd(qkv, o, dmix, *, name):
    s = qkv.shape[0]
    nq = s // SB_TQ
    scale = HEAD_DIM ** -0.5

    def body(q_ref, k_ref, v_ref, o_ref, do_ref, dq_ref, dk_ref, dv_ref, a0, a1, r0, r1, dqs, dks, dvs):
        acc, racc = (a0, a1), (r0, r1)
        i = pl.program_id(1)
        qi = nq - 1 - i
        row, lane, after, from_ = _sb_consts()
        klane = _iota((QBLK, LANES), 1)
        khms = (klane < HEAD_DIM, klane >= HEAD_DIM)

        @pl.when(i == 0)
        def _():
            dks[...] = jnp.zeros_like(dks)
            dvs[...] = jnp.zeros_like(dvs)

        q = q_ref[...]
        do = do_ref[...]
        dob = do.astype(BF16)
        dd = do * o_ref[...]
        dol = (do - dob.astype(F32)).astype(BF16)
        zero = jnp.zeros_like(q)
        hms = (lane < HEAD_DIM, lane >= HEAD_DIM)
        qms = [jnp.where(hm, q, zero) for hm in hms]
        doms = [jnp.where(hm, dob, zero) for hm in hms]
        dols = [jnp.where(hm, dol, zero) for hm in hms]
        dsums = [jnp.sum(jnp.where(hm, dd, 0.0), axis=1, keepdims=True) for hm in hms]

        def block(kj, mask):
            off = pl.multiple_of(kj * QBLK, QBLK)
            kb = k_ref[pl.ds(off, QBLK), :]
            vb = v_ref[pl.ds(off, QBLK), :]
            top, dq, dk, dv = None, None, None, None
            sc = [_sb_scores(qms[h], kb, mask) for h in range(2)]
            trs = [_xdot(sc[h][0], after) for h in range(2)]
            dws = [_dot(doms[h], vb, NT) + _dot(dols[h], vb, NT) for h in range(2)]
            for h in range(2):
                lb, tr = sc[h][1], trs[h]
                w = jnp.exp(lb + tr[:, :QBLK] + acc[h][...])
                if mask is not None:
                    w = jnp.where(mask, w, 0.0)
                g = w * dws[h]
                tg = _xdot(g, from_)
                before = dsums[h] - (tg[:, :QBLK] + racc[h][...])
                dz = g - jnp.exp(lb) * (g + before)
                if mask is not None:
                    dz = jnp.where(mask, dz, 0.0)
                dzb = (dz * scale).astype(BF16)
                dqh = _dot(dzb, jnp.where(khms[h], kb, jnp.zeros_like(kb)))
                dkh = _dot(dzb, qms[h], TN)
                dvh = _dot(w.astype(BF16), doms[h], TN)
                dq, dk, dv = (dqh, dkh, dvh) if h == 0 else (dq + dqh, dk + dkh, dv + dvh)
                new = acc[h][...] + tr[:, QBLK:]
                acc[h][...] = new
                racc[h][...] += tg[:, QBLK:]
                top = new if top is None else jnp.maximum(top, new)
            dqs[...] += dq
            dks[pl.ds(off, QBLK), :] += dk
            dvs[pl.ds(off, QBLK), :] += dv
            return jnp.max(top)

        for ref in (dqs, a0, a1, r0, r1):
            ref[...] = jnp.zeros_like(ref)
        top = None
        for j in reversed(range(SB_NK)):
            top = block(qi * SB_NK + j, (lane + j * QBLK) < row)
        lax.while_loop(lambda c: (c[0] >= 0) & (c[1] > -SB_EXIT), lambda c: (c[0] - 1, block(c[0], None)),
                       (qi * SB_NK - 1, top))
        dq_ref[...] = dqs[...]
        fin = pl.multiple_of(qi * SB_TQ, SB_TQ)
        dk_ref[...] = dks[pl.ds(fin, SB_TQ), :]
        dv_ref[...] = dvs[pl.ds(fin, SB_TQ), :]

    blk = lambda c0: pl.BlockSpec((SB_TQ, LANES), lambda p, i, c0=c0: (nq - 1 - i, c0 + p))
    return _pcall(
        body, name=name, grid=(2, nq),
        in_specs=[blk(0), pl.BlockSpec((s, LANES), lambda p, i: (0, 2 + p)), pl.BlockSpec((s, LANES), lambda p, i: (0, 4 + p)),
                  blk(0), blk(0)],
        out_specs=[blk(0), blk(0), blk(0)],
        out_shape=[jax.ShapeDtypeStruct((s, SB_W), F32)] * 3,
        scratch_shapes=[pltpu.VMEM((SB_TQ, LANES), F32)] * 5 + [pltpu.VMEM((s, LANES), F32), pltpu.VMEM((s, LANES), F32)],
        compiler_params=_cparams("arbitrary", "arbitrary"))(qkv, qkv, qkv, o, dmix)


def _seg_consts():
    r = _iota((LANES, LANES), 0)
    c = _iota((LANES, LANES), 1)
    return jnp.where((r >> 6) == (c >> 6), 1.0, 0.0).astype(BF16)


def _rot_half(x, lane):
    half = HEAD_DIM // 2
    return jnp.where((lane & (HEAD_DIM - 1)) < half, pltpu.roll(x, LANES - half, 1), pltpu.roll(x, half, 1))


def _rope_tables(s):
    half = HEAD_DIM // 2
    inv_freq = ROPE_THETA ** (-jnp.arange(half, dtype=F32) * 2.0 / HEAD_DIM)
    ang = jnp.arange(s, dtype=F32)[:, None] * inv_freq[None, :]
    cos, sin = jnp.cos(ang), jnp.sin(ang)
    return jnp.tile(jnp.concatenate([cos, cos], axis=1), (1, 2)), jnp.tile(jnp.concatenate([-sin, sin], axis=1), (1, 2))


def _dil_prep(proj, gq, gk, cos, sin, *, name):
    s = proj.shape[0]
    t = _rows(s, TE)
    c0 = 3 * SB_W // LANES

    def body(q_ref, k_ref, gq_ref, gk_ref, cos_ref, sin_ref, qo_ref, ko_ref):
        seg = _seg_consts()
        lane = _iota((t, LANES), 1)
        cs, sn = cos_ref[...], sin_ref[...]
        for x_ref, g_ref, o_ref, mul in ((q_ref, gq_ref, qo_ref, HEAD_DIM ** -0.5), (k_ref, gk_ref, ko_ref, 1.0)):
            for j in range(2):
                xv = x_ref[:, j * LANES:(j + 1) * LANES]
                ms = _xdot(xv * xv, seg, 3) * (1.0 / HEAD_DIM)
                xn = xv * lax.rsqrt(ms + EPS) * g_ref[...]
                o_ref[:, j * LANES:(j + 1) * LANES] = ((xn * cs + _rot_half(xn, lane) * sn) * mul).astype(BF16)

    return _pcall(
        body, name=name, grid=(s // t,),
        in_specs=[pl.BlockSpec((t, DIL_W), lambda i: (i, c0 // 2)), pl.BlockSpec((t, DIL_W), lambda i: (i, c0 // 2 + 1)),
                  _vec_spec(LANES), _vec_spec(LANES), _row_spec(t, LANES), _row_spec(t, LANES)],
        out_specs=[_row_spec(t, DIL_W), _row_spec(t, DIL_W)],
        out_shape=[jax.ShapeDtypeStruct((s, DIL_W), BF16)] * 2, compiler_params=_cparams("parallel"))(proj, proj, gq, gk, cos, sin)


def _dil_prep_bwd(proj, gq, gk, cos, sin, dqs, dks, dvs, *, name):
    s = proj.shape[0]
    t = _rows(s, TE)
    c0 = 3 * SB_W // LANES

    def body(q_ref, k_ref, gq_ref, gk_ref, cos_ref, sin_ref, a0, a1, a2, b0, b1, b2, c0_ref, c1_ref, c2_ref,
             dq_ref, dk_ref, dv_ref, dgq_ref, dgk_ref):
        @pl.when(pl.program_id(0) == 0)
        def _():
            dgq_ref[...] = jnp.zeros_like(dgq_ref)
            dgk_ref[...] = jnp.zeros_like(dgk_ref)

        dv_ref[...] = c0_ref[...] + c1_ref[...] + c2_ref[...]
        seg = _seg_consts()
        lane = _iota((t, LANES), 1)
        cs, sn = cos_ref[...], sin_ref[...]
        for x_ref, g_ref, parts, o_ref, dg_ref, mul in ((q_ref, gq_ref, (a0, a1, a2), dq_ref, dgq_ref, HEAD_DIM ** -0.5),
                                                          (k_ref, gk_ref, (b0, b1, b2), dk_ref, dgk_ref, 1.0)):
            for j in range(2):
                sl = slice(j * LANES, (j + 1) * LANES)
                dout = (parts[0][:, sl] + parts[1][:, sl] + parts[2][:, sl]) * mul
                dxn = dout * cs + _rot_half(dout * sn, lane)
                xv = x_ref[:, sl]
                r = lax.rsqrt(_xdot(xv * xv, seg, 3) * (1.0 / HEAD_DIM) + EPS)
                xh = xv * r
                dg_ref[...] += jnp.sum(dxn * xh, axis=0, keepdims=True)
                dxh = dxn * g_ref[...]
                o_ref[:, sl] = r * (dxh - xh * (_xdot(dxh * xh, seg, 3) * (1.0 / HEAD_DIM)))

    rs = _row_spec(t, DIL_W)
    return _pcall(
        body, name=name, grid=(s // t,),
        in_specs=[pl.BlockSpec((t, DIL_W), lambda i: (i, c0 // 2)), pl.BlockSpec((t, DIL_W), lambda i: (i, c0 // 2 + 1)),
                  _vec_spec(LANES), _vec_spec(LANES), _row_spec(t, LANES), _row_spec(t, LANES)] + [rs] * 9,
        out_specs=[rs, rs, rs, _vec_spec(LANES), _vec_spec(LANES)],
        out_shape=[jax.ShapeDtypeStruct((s, DIL_W), F32)] * 3 + [jax.ShapeDtypeStruct((1, LANES), F32)] * 2,
        compiler_params=_cparams("arbitrary"))(proj, proj, gq, gk, cos, sin, *dqs, *dks, *dvs)


def _dil_masks(n):
    row = _iota((QBLK, LANES), 0)
    col = _iota((QBLK, LANES), 1)
    return col <= row, (col >= row) & (n > 0)


def _dil_fwd(q, k, v, r, *, name):
    s = q.shape[0]
    rows = s // r
    tb = _tile(rows, DIL_TILE, QBLK)
    nsub = tb // QBLK
    shape = (rows, r * DIL_W)
    q, k, v = (a.reshape(shape) for a in (q, k, v))

    def body(q_ref, kc_ref, kp_ref, vc_ref, vp_ref, num_ref, den_ref, mx_ref):
        n = pl.program_id(1)
        lane = _iota((QBLK, LANES), 1)
        h0 = lane < HEAD_DIM
        for j in range(nsub):
            rs = slice(j * QBLK, (j + 1) * QBLK)
            vc_m, vp_m = _dil_masks(n if j == 0 else 1)
            for p in range(2):
                ls = slice(p * LANES, (p + 1) * LANES)
                qv = q_ref[rs, ls]
                kc, vc = kc_ref[rs, ls], vc_ref[rs, ls]
                if j == 0:
                    kp, vp = kp_ref[:, ls], vp_ref[:, ls]
                else:
                    kp, vp = kc_ref[(j - 1) * QBLK:j * QBLK, ls], vc_ref[(j - 1) * QBLK:j * QBLK, ls]
                res = []
                for h in range(2):
                    qm = jnp.where(h0 if h == 0 else ~h0, qv, jnp.zeros_like(qv))
                    sc = jnp.where(vc_m, _dot(qm, kc, NT), NEG_BIG)
                    sp = jnp.where(vp_m, _dot(qm, kp, NT), NEG_BIG)
                    mx = jnp.maximum(jnp.max(sc, axis=1, keepdims=True), jnp.max(sp, axis=1, keepdims=True))
                    pc = jnp.exp(sc - mx)
                    pp = jnp.exp(sp - mx)
                    den = jnp.sum(pc, axis=1, keepdims=True) + jnp.sum(pp, axis=1, keepdims=True)
                    res.append((_dot(pc.astype(BF16), vc) + _dot(pp.astype(BF16), vp), den, mx))
                num_ref[rs, ls] = jnp.where(h0, res[0][0], res[1][0])
                den_ref[rs, ls] = jnp.where(h0, res[0][1], res[1][1])
                mx_ref[rs, ls] = jnp.where(h0, res[0][2], res[1][2])

    cur = pl.BlockSpec((tb, DIL_W), lambda rho, n: (n, rho))
    prev = pl.BlockSpec((QBLK, DIL_W), lambda rho, n: (jnp.maximum(n * nsub - 1, 0), rho))
    return _pcall(
        body, name=name, grid=(r, rows // tb), in_specs=[cur, cur, prev, cur, prev], out_specs=[cur, cur, cur],
        out_shape=[jax.ShapeDtypeStruct(shape, F32)] * 3,
        compiler_params=_cparams("parallel", "arbitrary"))(q, k, k, v, v)


def _dil_bwd(q, k, v, do, mall, zall, delta, r, *, name):
    s = q.shape[0]
    rows = s // r
    tb = _tile(rows, DIL_TILE, QBLK)
    nsub, nbig = tb // QBLK, rows // tb
    shape = (rows, r * DIL_W)
    q, k, v, do, mall, zall, delta = (a.reshape(shape) for a in (q, k, v, do, mall, zall, delta))

    def body(q_ref, kc_ref, kp_ref, vc_ref, vp_ref, do_ref, m_ref, z_ref, dl_ref, dq_ref, dk_ref, dv_ref, pk, pv):
        n = pl.program_id(1)
        lane = _iota((QBLK, LANES), 1)
        h0 = lane < HEAD_DIM

        @pl.when(n == 0)
        def _():
            pk[...] = jnp.zeros_like(pk)
            pv[...] = jnp.zeros_like(pv)

        @pl.when(n < nbig)
        def _():
            dk_ref[...] = pk[...]
            dv_ref[...] = pv[...]
            for j in range(nsub):
                rs = slice(j * QBLK, (j + 1) * QBLK)
                ps = slice((j - 1) * QBLK, j * QBLK)
                vc_m, vp_m = _dil_masks(n if j == 0 else 1)
                for p in range(2):
                    ls = slice(p * LANES, (p + 1) * LANES)
                    qv = q_ref[rs, ls]
                    dob = do_ref[rs, ls].astype(BF16)
                    zero = jnp.zeros_like(qv)
                    kc, vc = kc_ref[rs, ls], vc_ref[rs, ls]
                    kp, vp = (kp_ref[:, ls], vp_ref[:, ls]) if j == 0 else (kc_ref[ps, ls], vc_ref[ps, ls])
                    dq = None
                    acc = [None] * 4
                    for h in range(2):
                        hm = h0 if h == 0 else ~h0
                        qm = jnp.where(hm, qv, zero)
                        dom = jnp.where(hm, dob, zero)
                        c = p * LANES + h * HEAD_DIM
                        mrow, dlrow = m_ref[rs, c:c + 1], dl_ref[rs, c:c + 1]
                        rz = 1.0 / z_ref[rs, c:c + 1]
                        for kb, vb, valid, o in ((kc, vc, vc_m, 0), (kp, vp, vp_m, 2)):
                            sc = jnp.where(valid, _dot(qm, kb, NT), NEG_BIG)
                            pr = jnp.exp(sc - mrow) * rz
                            ds = (pr * (_dot(dom, vb, NT) - dlrow)).astype(BF16)
                            dqx = _dot(ds, jnp.where(hm, kb, zero))
                            dq = dqx if dq is None else dq + dqx
                            for i, x in ((o, _dot(ds, qm, TN)), (o + 1, _dot(pr.astype(BF16), dom, TN))):
                                acc[i] = x if acc[i] is None else acc[i] + x
                    dq_ref[rs, ls] = dq
                    if j == 0:
                        pk[rs, ls] = acc[0]
                        pv[rs, ls] = acc[1]
                        dk_ref[tb - QBLK:, ls] += acc[2]
                        dv_ref[tb - QBLK:, ls] += acc[3]
                    else:
                        pk[rs, ls] = acc[0]
                        pv[rs, ls] = acc[1]
                        pk[ps, ls] += acc[2]
                        pv[ps, ls] += acc[3]

        @pl.when(n == nbig)
        def _():
            dk_ref[...] = pk[...]
            dv_ref[...] = pv[...]

    last = nbig - 1
    cur = pl.BlockSpec((tb, DIL_W), lambda rho, n: (jnp.minimum(n, last), rho))
    prev = pl.BlockSpec((QBLK, DIL_W), lambda rho, n: (jnp.maximum(jnp.minimum(n, last) * nsub - 1, 0), rho))
    late = pl.BlockSpec((tb, DIL_W), lambda rho, n: (jnp.maximum(n - 1, 0), rho))
    return _pcall(
        body, name=name, grid=(r, nbig + 1), in_specs=[cur, cur, prev, cur, prev, cur, cur, cur, cur],
        out_specs=[cur, late, late], out_shape=[jax.ShapeDtypeStruct(shape, F32)] * 3,
        scratch_shapes=[pltpu.VMEM((tb, DIL_W), F32)] * 2,
        compiler_params=_cparams("parallel", "arbitrary"))(q, k, k, v, v, do, mall, zall, delta)


HG_SHIFT = HG_BLK.bit_length() - 1
HG_Q0, HG_F0, HG_I0 = (3 * SB_W + 3 * DIL_W) // HG_D, (3 * SB_W + 3 * DIL_W + HG_W) // HG_D, (3 * SB_W + 3 * DIL_W + 2 * HG_W) // HG_D


def _hg_scan(x):
    t = x.shape[0]
    half = HG_BLK // 2
    rb = _iota((t, LANES), 0) & (HG_BLK - 1)
    rh = rb & (half - 1)
    p = x
    for s in (1, 2, 4):
        p = p + jnp.where(rh >= s, pltpu.roll(p, s, 0), 0.0)
    h = jnp.where(rh == half - 1, p, 0.0)
    for s in (1, 2, 4):
        h = h + jnp.where(rh + s < half, pltpu.roll(h, t - s, 0), 0.0)
    first = rb < half
    pref = jnp.where(first, p, p + pltpu.roll(h, half, 0))
    total = h + jnp.where(first, pltpu.roll(h, t - half, 0), pltpu.roll(h, half, 0))
    return p, h, pref, total, first


def _hg_same(t):
    i = jnp.arange(t) >> HG_SHIFT
    return (i[:, None] == i[None, :]).astype(F32)


def _hg_own(t):
    i = jnp.arange(t) >> HG_SHIFT
    j = jnp.arange(t // HG_BLK * HG_D) // HG_D
    return (i[:, None] == j[None, :]).astype(BF16)


def _hg_diag(x, nb):
    return jnp.concatenate([x[b * HG_BLK:(b + 1) * HG_BLK, b * HG_D:(b + 1) * HG_D] for b in range(nb)], axis=0)


def _hg_inputs(qh, z, v, la, lc, t):
    lsg = jnp.minimum(z, 0.0) - jnp.log(1.0 + jnp.exp(-jnp.abs(z)))
    b = lc + lsg
    lf = jnp.maximum(la, b) + jnp.log(1.0 + jnp.exp(-jnp.abs(la - b)))
    f = jnp.exp(lf)
    sq = _sigmoid(qh)
    p, h, g, gl, first = _hg_scan(lf)
    k = 1.0 - f
    qs = qh * sq
    eq = jnp.where(first, 0.0, jnp.exp(jnp.minimum(p, 0.0)))
    ek = jnp.where(first, jnp.exp(jnp.minimum(h - p, 0.0)), 0.0)
    return dict(lf=lf, b=b, f=f, k=k, sq=sq, qs=qs, g=g, eg=jnp.exp(g), egl=jnp.exp(gl - g), dec=jnp.exp(gl),
                eq=eq, ek=ek, qx=(qs * eq).astype(BF16), kx=(k * ek).astype(BF16))


def _hgrn_fwd(proj, la, lc, *, name):
    s = proj.shape[0]
    t = _rows(s, HG_TILE)
    nt, nb = s // t, t // HG_BLK

    def body(q_ref, f_ref, i_ref, la_ref, lc_ref, same_ref, own_ref, o_ref, st_ref, state):
        @pl.when(pl.program_id(1) == 0)
        def _():
            state[...] = jnp.zeros_like(state)

        v = i_ref[...]
        a = _hg_inputs(q_ref[...], f_ref[...], v, la_ref[...], lc_ref[...], t)
        qs, k = a["qs"], a["k"]
        vb = v.astype(BF16)
        rb = _iota((t, LANES), 0) & (HG_BLK // 2 - 1)
        o = jnp.sum(qs * k, axis=1, keepdims=True) * v
        e = None
        for d in range(1, HG_BLK // 2):
            m = rb >= d
            fr = a["f"] if d == 1 else pltpu.roll(a["f"], d - 1, 0)
            e = fr if e is None else e * fr
            cd = jnp.sum(qs * pltpu.roll(k, d, 0) * e, axis=1, keepdims=True)
            o = o + jnp.where(m, cd, 0.0) * pltpu.roll(v, d, 0)
        cross = _dot(a["qx"], a["kx"], NT) * same_ref[...]
        o = o + _dot(cross.astype(BF16), vb)
        qt = (qs * a["eg"]).astype(BF16)
        kt = (k * a["egl"]).astype(BF16)
        upd = _dot(vb, jnp.tile(kt, (1, nb)) * own_ref[...], TN)
        st = state[...]
        for blk in range(nb):
            st_ref[blk * HG_D:(blk + 1) * HG_D, :] = st.astype(BF16)
            st = a["dec"][blk * HG_BLK:blk * HG_BLK + 1] * st + upd[:, blk * HG_D:(blk + 1) * HG_D]
        state[...] = st
        o_ref[...] = o + _hg_diag(_dot(qt, st_ref[...], NT), nb)

    col = lambda c0: pl.BlockSpec((t, HG_D), lambda hd, i, c0=c0: (i, c0 + hd))
    vec = pl.BlockSpec((1, HG_D), lambda hd, i: (0, hd))
    return _pcall(
        body, name=name, grid=(4, nt),
        in_specs=[col(HG_Q0), col(HG_F0), col(HG_I0), vec, vec, pl.BlockSpec((t, t), lambda hd, i: (0, 0)),
                  pl.BlockSpec((t, nb * HG_D), lambda hd, i: (0, 0))],
        out_specs=[col(0), pl.BlockSpec((None, nb * HG_D, HG_D), lambda hd, i: (hd, i, 0))],
        out_shape=[jax.ShapeDtypeStruct((s, HG_W), F32), jax.ShapeDtypeStruct((4, s // HG_BLK * HG_D, HG_D), BF16)],
        scratch_shapes=[pltpu.VMEM((HG_D, HG_D), F32)],
        compiler_params=_cparams("arbitrary", "arbitrary"))(proj, proj, proj, la, lc, _hg_same(t), _hg_own(t))


def _hgrn_bwd(proj, la, lc, st, doh, *, name):
    s = proj.shape[0]
    t = _rows(s, HG_TILE)
    nt, nb = s // t, t // HG_BLK

    def body(q_ref, f_ref, i_ref, la_ref, lc_ref, st_ref, do_ref, same_ref, own_ref, dq_ref, df_ref, di_ref, dla_ref, dlc_ref,
             dstate, dsb):
        @pl.when(pl.program_id(1) == 0)
        def _():
            dstate[...] = jnp.zeros_like(dstate)
            dla_ref[...] = jnp.zeros_like(dla_ref)
            dlc_ref[...] = jnp.zeros_like(dlc_ref)

        qh, z, v, do = q_ref[...], f_ref[...], i_ref[...], do_ref[...]
        la = la_ref[...]
        a = _hg_inputs(qh, z, v, la, lc_ref[...], t)
        qs, k, g = a["qs"], a["k"], a["g"]
        vb = v.astype(BF16)
        dob = do.astype(BF16)
        rb = _iota((t, LANES), 0) & (HG_BLK // 2 - 1)
        dc0 = jnp.sum(do * v, axis=1, keepdims=True)
        dq = dc0 * k
        dk = dc0 * qs
        dv = jnp.sum(qs * k, axis=1, keepdims=True) * do
        e = None
        for d in range(1, HG_BLK // 2):
            m = rb >= d
            fr = a["f"] if d == 1 else pltpu.roll(a["f"], d - 1, 0)
            e = fr if e is None else e * fr
            ks = pltpu.roll(k, d, 0)
            qe = qs * e
            cd = jnp.where(m, jnp.sum(qe * ks, axis=1, keepdims=True), 0.0)
            dcd = jnp.where(m, jnp.sum(do * pltpu.roll(v, d, 0), axis=1, keepdims=True), 0.0)
            dq = dq + dcd * ks * e
            dk = dk + pltpu.roll(dcd * qe, t - d, 0)
            dv = dv + pltpu.roll(cd * do, t - d, 0)
        same = same_ref[...]
        cross = (_dot(a["qx"], a["kx"], NT) * same).astype(BF16)
        dcross = (_dot(dob, vb, NT) * same).astype(BF16)
        dq = dq + _dot(dcross, a["kx"]) * a["eq"]
        dk = dk + _dot(dcross, a["qx"], TN) * a["ek"]
        dv = dv + _dot(cross, dob, TN)
        qt = (qs * a["eg"]).astype(BF16)
        kt = (k * a["egl"]).astype(BF16)
        own = own_ref[...]
        upd = _dot(dob, jnp.tile(qt, (1, nb)) * own, TN)
        ds = dstate[...]
        dgs = [None] * nb
        for blk in reversed(range(nb)):
            rows = slice(blk * HG_D, (blk + 1) * HG_D)
            dec = a["dec"][blk * HG_BLK:blk * HG_BLK + 1]
            dsb[rows, :] = ds.astype(BF16)
            dgs[blk] = jnp.broadcast_to(jnp.sum(ds * st_ref[rows, :].astype(F32), axis=0, keepdims=True) * dec, (HG_BLK, HG_D))
            ds = dec * ds + upd[:, rows]
        dstate[...] = ds
        dki = _dot(jnp.tile(vb, (1, nb)) * own, dsb[...]) * a["egl"]
        dq = dq + _dot(jnp.tile(dob, (1, nb)) * own, st_ref[...]) * a["eg"]
        dk = dk + dki
        dv = dv + _hg_diag(_dot(kt, dsb[...], NT), nb)
        x = qs * dq - k * dk
        _, _, xpre, xtot, _ = _hg_scan(x)
        _, _, _, ktot, _ = _hg_scan(k * dki)
        dlf = (xtot - xpre + x) + ktot + jnp.concatenate(dgs, axis=0) - a["f"] * dk
        wb = jnp.exp(a["b"] - a["lf"])
        wa = jnp.exp(la - a["lf"])
        sq = a["sq"]
        dq_ref[...] = dq * (sq * (1.0 + qh * (1.0 - sq)))
        df_ref[...] = dlf * wb * (1.0 - _sigmoid(z))
        di_ref[...] = dv
        dla_ref[...] += jnp.sum(dlf * wa, axis=0, keepdims=True)
        dlc_ref[...] += jnp.sum(dlf * wb, axis=0, keepdims=True)

    col = lambda c0: pl.BlockSpec((t, HG_D), lambda hd, i, c0=c0: (nt - 1 - i, c0 + hd))
    vec = pl.BlockSpec((1, HG_D), lambda hd, i: (0, hd))
    return _pcall(
        body, name=name, grid=(4, nt),
        in_specs=[col(HG_Q0), col(HG_F0), col(HG_I0), vec, vec,
                  pl.BlockSpec((None, nb * HG_D, HG_D), lambda hd, i: (hd, nt - 1 - i, 0)), col(0),
                  pl.BlockSpec((t, t), lambda hd, i: (0, 0)), pl.BlockSpec((t, nb * HG_D), lambda hd, i: (0, 0))],
        out_specs=[col(0), col(0), col(0), vec, vec],
        out_shape=[jax.ShapeDtypeStruct((s, HG_W), F32)] * 3 + [jax.ShapeDtypeStruct((1, HG_W), F32)] * 2,
        scratch_shapes=[pltpu.VMEM((HG_D, HG_D), F32), pltpu.VMEM((nb * HG_D, HG_D), BF16)],
        compiler_params=_cparams("arbitrary", "arbitrary"))(proj, proj, proj, la, lc, st, doh, _hg_same(t), _hg_own(t))


GH0 = (IN_W - HG_W) // HG_W


def _mix_out(o_a, nums, dens, mxs, oh, proj, hg, *, name):
    s = o_a.shape[0]
    t = _rows(s, TE)

    def body(oa_ref, n0, n1, n2, d0, d1, d2, m0, m1, m2, oh_ref, gh_ref, hg_ref, y_ref, od_ref, mall_ref, z_ref):
        y_ref[:, :SB_W] = oa_ref[...].astype(BF16)
        m = jnp.maximum(jnp.maximum(m0[...], m1[...]), m2[...])
        num = jnp.zeros((t, DIL_W), F32)
        z = jnp.zeros((t, DIL_W), F32)
        for n_ref, d_ref, m_ref in ((n0, d0, m0), (n1, d1, m1), (n2, d2, m2)):
            sc = jnp.exp(m_ref[...] - m)
            num = num + n_ref[...] * sc
            z = z + d_ref[...] * sc
        od = num / z
        od_ref[...] = od
        mall_ref[...] = m
        z_ref[...] = z
        y_ref[:, SB_W:SB_W + DIL_W] = od.astype(BF16)
        for h in range(4):
            sl = slice(h * HG_D, (h + 1) * HG_D)
            ov = oh_ref[:, sl]
            g = gh_ref[:, sl]
            r = lax.rsqrt(jnp.mean(ov * ov, axis=1, keepdims=True) + EPS)
            y_ref[:, SB_W + DIL_W + h * HG_D:SB_W + DIL_W + (h + 1) * HG_D] = (ov * r * hg_ref[...] * (g * _sigmoid(g))).astype(BF16)

    rd = _row_spec(t, DIL_W)
    return _pcall(
        body, name=name, grid=(s // t,),
        in_specs=[rd] * 10 + [_row_spec(t, HG_W), _row_spec(t, HG_W, GH0), _vec_spec(HG_D)],
        out_specs=[_row_spec(t, MIX_W), rd, rd, rd],
        out_shape=[jax.ShapeDtypeStruct((s, MIX_W), BF16)] + [jax.ShapeDtypeStruct((s, DIL_W), F32)] * 3,
        compiler_params=_cparams("parallel"))(o_a, *nums, *dens, *mxs, oh, proj, hg)


def _mix_out_bwd(dmix, oh, proj, hg, od, *, name):
    s = oh.shape[0]
    t = _rows(s, TE)

    def body(dm_ref, oh_ref, gh_ref, hg_ref, od_ref, doh_ref, dgh_ref, dl_ref, dhg_ref):
        @pl.when(pl.program_id(0) == 0)
        def _():
            dhg_ref[...] = jnp.zeros_like(dhg_ref)

        seg = _seg_consts()
        for j in range(2):
            sl = slice(j * LANES, (j + 1) * LANES)
            dl_ref[:, sl] = _xdot(dm_ref[:, SB_W + j * LANES:SB_W + (j + 1) * LANES] * od_ref[:, sl], seg, 3)
        hgv = hg_ref[...]
        for h in range(4):
            sl = slice(h * HG_D, (h + 1) * HG_D)
            dy = dm_ref[:, SB_W + DIL_W + h * HG_D:SB_W + DIL_W + (h + 1) * HG_D]
            ov = oh_ref[:, sl]
            g = gh_ref[:, sl]
            sg = _sigmoid(g)
            silu = g * sg
            r = lax.rsqrt(jnp.mean(ov * ov, axis=1, keepdims=True) + EPS)
            nrm = ov * r
            dhg_ref[...] += jnp.sum(dy * nrm * silu, axis=0, keepdims=True)
            dgh_ref[:, sl] = dy * nrm * hgv * (sg * (1.0 + g * (1.0 - sg)))
            dn = dy * hgv * silu
            doh_ref[:, sl] = r * (dn - nrm * jnp.mean(dn * nrm, axis=1, keepdims=True))

    rh = _row_spec(t, HG_W)
    return _pcall(
        body, name=name, grid=(s // t,),
        in_specs=[_row_spec(t, MIX_W), rh, _row_spec(t, HG_W, GH0), _vec_spec(HG_D), _row_spec(t, DIL_W)],
        out_specs=[rh, rh, _row_spec(t, DIL_W), _vec_spec(HG_D)],
        out_shape=[jax.ShapeDtypeStruct((s, HG_W), F32)] * 2 + [jax.ShapeDtypeStruct((s, DIL_W), F32), jax.ShapeDtypeStruct((1, HG_D), F32)],
        compiler_params=_cparams("arbitrary"))(dmix, oh, proj, hg, od)


def _lb_terms(l):
    l0, l1 = l[0:1], l[1:2]
    m = jnp.maximum(l0, l1)
    e0, e1 = jnp.exp(l0 - m), jnp.exp(l1 - m)
    s0, s1 = e0 / (e0 + e1), e1 / (e0 + e1)
    args = (s0 - s0, (s0 + s1) - s0)
    lbs = tuple(jnp.minimum(jnp.maximum(a, 0.0), 1.0 - EPS) for a in args)
    return s0, s1, args, lbs


def _lb_prep(logits, *, name):
    def body(l_ref, lb_ref, la_ref, lc_ref):
        _, _, _, lbs = _lb_terms(l_ref[...])
        lb = jnp.concatenate(lbs, axis=0)
        lb_ref[...] = lb
        la_ref[...] = jnp.log(jnp.maximum(lb, LB_FLOOR))
        lc_ref[...] = jnp.log1p(-lb)

    return _pcall(body, name=name, out_shape=[jax.ShapeDtypeStruct(logits.shape, F32)] * 3)(logits)


def _lb_bwd(logits, dla, dlc, *, name):
    def half(hi, eq):
        return jnp.where(hi, 1.0, jnp.where(eq, 0.5, 0.0))

    def body(l_ref, dla_ref, dlc_ref, o_ref):
        s0, s1, args, lbs = _lb_terms(l_ref[...])
        da = []
        for i in range(2):
            a, lb = args[i], lbs[i]
            dlb = dla_ref[i:i + 1] * half(lb > LB_FLOOR, lb == LB_FLOOR) / jnp.maximum(lb, LB_FLOOR) - dlc_ref[i:i + 1] / (1.0 - lb)
            t = jnp.maximum(a, 0.0)
            da.append(dlb * half(a > 0.0, a == 0.0) * half(t < 1.0 - EPS, t == 1.0 - EPS))
        ds0 = (da[0] + da[1]) - (da[0] + da[1])
        ds1 = da[1]
        dot = s0 * ds0 + s1 * ds1
        o_ref[...] = jnp.concatenate([s0 * (ds0 - dot), s1 * (ds1 - dot)], axis=0)

    return _pcall(body, name=name, out_shape=jax.ShapeDtypeStruct(logits.shape, F32))(logits, dla, dlc)


def _mod_fwd(c8, w, b, *, name):
    _, d, n = w.shape
    tn = _tile(n, 768)

    def body(c_ref, w_ref, b_ref, o_ref):
        cv = c_ref[...]
        o_ref[...] = _dot((cv * _sigmoid(cv)).astype(BF16), w_ref[...].astype(BF16)) + b_ref[...]

    return _pcall(
        body, name=name, grid=(2, n // tn),
        in_specs=[pl.BlockSpec((8, d), lambda l, j: (0, 0)), pl.BlockSpec((None, d, tn), lambda l, j: (l, 0, j)),
                  pl.BlockSpec((None, 1, tn), lambda l, j: (l, 0, j))],
        out_specs=pl.BlockSpec((None, 8, tn), lambda l, j: (l, 0, j)),
        out_shape=jax.ShapeDtypeStruct((2, 8, n), F32), compiler_params=_cparams("parallel", "parallel"))(c8, w, b)


def _mod_bwd(ct, dm, *, name):
    d = ct.shape[0]
    n = dm.shape[2]
    tn = _tile(n, 768)

    def body(c_ref, dm_ref, o_ref):
        cv = c_ref[...]
        sc = cv * _sigmoid(cv)
        dv = dm_ref[...]
        acc = sc[:, 0:1] * dv[0:1, :]
        for b in range(1, 8):
            acc = acc + sc[:, b:b + 1] * dv[b:b + 1, :]
        o_ref[...] = acc

    return _pcall(
        body, name=name, grid=(2, n // tn),
        in_specs=[pl.BlockSpec((d, 8), lambda l, j: (0, 0)), pl.BlockSpec((None, 8, tn), lambda l, j: (l, 0, j))],
        out_specs=pl.BlockSpec((None, d, tn), lambda l, j: (l, 0, j)),
        out_shape=jax.ShapeDtypeStruct((2, d, n), F32), compiler_params=_cparams("parallel", "parallel"))(ct, dm)


_PEERS = {
    "chips": ((1, 0, 0), (0, 1, 0), (1, 1, 0)),
    "all": tuple((a, b, c) for a in (0, 1) for b in (0, 1) for c in (0, 1) if a + b + c),
    "sib": ((0, 0, 1),),
}
_SLOTS = {"chips": 4, "all": 8, "sib": 2}


def _slot(kind, x, y, c):
    return {"chips": 2 * x + y, "all": 4 * x + 2 * y + c, "sib": c}[kind]


def _exchange(arrs, kind, scatter, *, name):
    n = len(arrs)
    peers = _PEERS[kind]
    ns = _SLOTS[kind]
    np_ = len(peers)

    def pieces(a):
        shape = a.shape[1:] if scatter else a.shape
        if len(shape) == 2:
            for k in (XCHG_CHUNKS, XCHG_CHUNKS // 2, XCHG_CHUNKS // 4):
                if k > 1 and shape[0] % (16 * k) == 0 and shape[0] * shape[1] * a.dtype.itemsize >= k * XCHG_MIN_BYTES:
                    return [(i * (shape[0] // k), shape[0] // k) for i in range(k)]
        return [None]

    chunks = [pieces(a) for a in arrs]
    base = [sum(len(c) for c in chunks[:a]) * np_ for a in range(n)]
    total = sum(len(c) for c in chunks) * np_

    def body(*refs):
        ins, outs = refs[:n], refs[n:2 * n]
        send, recv, loc = refs[2 * n:]
        x, y, c = lax.axis_index("x"), lax.axis_index("y"), lax.axis_index("c")
        me = _slot(kind, x, y, c)
        copies = []
        for a in range(n):
            own = pltpu.make_async_copy(ins[a].at[me] if scatter else ins[a], outs[a].at[me], loc.at[a])
            own.start()
            copies.append(own)
            for j, (dx, dy, dc) in enumerate(peers):
                px, py, pc = (1 - x if dx else x), (1 - y if dy else y), (1 - c if dc else c)
                src = ins[a].at[_slot(kind, px, py, pc)] if scatter else ins[a]
                for i, piece in enumerate(chunks[a]):
                    rows = slice(None) if piece is None else pl.ds(piece[0], piece[1])
                    sem = base[a] + j * len(chunks[a]) + i
                    cp = pltpu.make_async_remote_copy(
                        src_ref=src if piece is None else src.at[rows], dst_ref=outs[a].at[me] if piece is None else outs[a].at[me, rows],
                        send_sem=send.at[sem], recv_sem=recv.at[sem], device_id=(px, py, pc), device_id_type=MESH_ID)
                    cp.start()
                    copies.append(cp)
        for cp in copies:
            cp.wait()

    hbm = pl.BlockSpec(memory_space=pl.ANY)
    shapes = [jax.ShapeDtypeStruct(a.shape if scatter else (ns,) + a.shape, a.dtype) for a in arrs]
    return _pcall(
        body, name=name, in_specs=[hbm] * n, out_specs=[hbm] * n, out_shape=shapes,
        scratch_shapes=[pltpu.SemaphoreType.DMA((total,)), pltpu.SemaphoreType.DMA((total,)), pltpu.SemaphoreType.DMA((n,))],
    )(*arrs)


def _sum_slots(a, *, name, out_dtype=F32):
    ns, r, c = a.shape
    t = _tile(r, max(16, (1 << 18) // c // 16 * 16), 16)

    def body(a_ref, o_ref):
        acc = a_ref[0].astype(F32)
        for i in range(1, ns):
            acc = acc + a_ref[i].astype(F32)
        o_ref[...] = acc.astype(o_ref.dtype)

    return _pcall(body, name=name, grid=(r // t,), in_specs=[pl.BlockSpec((ns, t, c), lambda i: (0, i, 0))],
                  out_specs=pl.BlockSpec((t, c), lambda i: (i, 0)), out_shape=jax.ShapeDtypeStruct((r, c), out_dtype),
                  compiler_params=_cparams("parallel"))(a)


def _adamw(w, gparts, m, v, *, name):
    r, c = w.shape
    t = _tile(r, max(16, (1 << 17) // c // 16 * 16), 16)
    ng = len(gparts)

    def body(*refs):
        w_ref, m_ref, v_ref = refs[0], refs[1 + ng], refs[2 + ng]
        g_ref, d_ref, nm_ref, nv_ref = refs[3 + ng:]
        g = refs[1][...].astype(F32)
        for i in range(1, ng):
            g = g + refs[1 + i][...].astype(F32)
        mn = ADAM_B1 * m_ref[...] + (1.0 - ADAM_B1) * g
        vn = ADAM_B2 * v_ref[...] + (1.0 - ADAM_B2) * (g * g)
        m_hat = mn / (1.0 - ADAM_B1 ** ADAM_STEP)
        v_hat = vn / (1.0 - ADAM_B2 ** ADAM_STEP)
        g_ref[...] = g
        d_ref[...] = -ADAM_LR * (m_hat / (jnp.sqrt(v_hat) + ADAM_EPS) + ADAM_WD * w_ref[...])
        nm_ref[...] = mn
        nv_ref[...] = vn

    spec = pl.BlockSpec((t, c), lambda i: (i, 0))
    return _pcall(body, name=name, grid=(r // t,), in_specs=[spec] * (3 + ng), out_specs=[spec] * 4,
                  out_shape=[jax.ShapeDtypeStruct((r, c), F32)] * 4, compiler_params=_cparams("parallel"))(w, *gparts, m, v)


FFN_TM = 512
FFN_CHUNK = 1408


def _resident(shape):
    return pl.BlockSpec(shape, lambda i: (0,) * len(shape), pipeline_mode=pl.Buffered(1))


def _ffn_up(x, sc, sh, wgu, *, name):
    s, d = x.shape
    f = wgu.shape[1] // 2
    t, fc = _rows(s, FFN_TM), _tile(f, FFN_CHUNK)

    def body(x_ref, sc_ref, sh_ref, w_ref, h_ref, uv_ref, a_ref):
        xv = x_ref[...]
        r = lax.rsqrt(jnp.mean(xv * xv, axis=1, keepdims=True) + EPS)
        hb = ((xv * r) * (1.0 + sc_ref[...]) + sh_ref[...]).astype(BF16)
        h_ref[...] = hb
        for j in range(f // fc):
            u = _dot(hb, w_ref[:, j * fc:(j + 1) * fc])
            v = _dot(hb, w_ref[:, f + j * fc:f + (j + 1) * fc])
            sg = _sigmoid(u)
            silu = u * sg
            uv_ref[:, j * fc:(j + 1) * fc] = (v * (sg * (1.0 + u * (1.0 - sg)))).astype(BF16)
            uv_ref[:, f + j * fc:f + (j + 1) * fc] = silu.astype(BF16)
            a_ref[:, j * fc:(j + 1) * fc] = (silu * v).astype(BF16)

    return _pcall(
        body, name=name, grid=(s // t,), in_specs=[_row_spec(t, d), _vec_spec(d), _vec_spec(d), _resident(wgu.shape)],
        out_specs=[_row_spec(t, d), _row_spec(t, 2 * f), _row_spec(t, f)],
        out_shape=[jax.ShapeDtypeStruct((s, d), BF16), jax.ShapeDtypeStruct((s, 2 * f), BF16), jax.ShapeDtypeStruct((s, f), BF16)],
        compiler_params=_cparams("parallel"))(x, sc, sh, wgu)


def _norm_mm(x, sc, sh, w, nb, *, name):
    s, d = x.shape
    n = w.shape[1]
    t, nc = _rows(s, FFN_TM), _tile(n, 1792)
    assert nb <= nc

    def body(x_ref, sc_ref, sh_ref, w_ref, h_ref, o_ref, ob_ref):
        xv = x_ref[...]
        r = lax.rsqrt(jnp.mean(xv * xv, axis=1, keepdims=True) + EPS)
        hb = ((xv * r) * (1.0 + sc_ref[...]) + sh_ref[...]).astype(BF16)
        h_ref[...] = hb
        for j in range(n // nc):
            part = _dot(hb, w_ref[:, j * nc:(j + 1) * nc])
            o_ref[:, j * nc:(j + 1) * nc] = part
            if j == 0:
                ob_ref[...] = part[:, :nb].astype(BF16)

    return _pcall(
        body, name=name, grid=(s // t,), in_specs=[_row_spec(t, d), _vec_spec(d), _vec_spec(d), _resident(w.shape)],
        out_specs=[_row_spec(t, d), _row_spec(t, n), _row_spec(t, nb)],
        out_shape=[jax.ShapeDtypeStruct((s, d), BF16), jax.ShapeDtypeStruct((s, n), F32), jax.ShapeDtypeStruct((s, nb), BF16)],
        compiler_params=_cparams("parallel"))(x, sc, sh, w)


def _ffn_dact(dxo, y, sg, wd, uv, *, name):
    s, d = y.shape
    f = wd.shape[0]
    t, fc = _rows(s, FFN_TM), _tile(f, FFN_CHUNK)

    def body(dxo_ref, y_ref, sg_ref, w_ref, uv_ref, dy_ref, duv_ref, ds_ref):
        @pl.when(pl.program_id(0) == 0)
        def _():
            ds_ref[...] = jnp.zeros_like(ds_ref)

        dv = dxo_ref[...]
        dyb = (sg_ref[...] * dv).astype(BF16)
        dy_ref[...] = dyb
        ds_ref[...] += jnp.sum(dv * y_ref[...], axis=0, keepdims=True)
        for j in range(f // fc):
            da = _dot(dyb, w_ref[j * fc:(j + 1) * fc, :], NT)
            duv_ref[:, j * fc:(j + 1) * fc] = (da * uv_ref[:, j * fc:(j + 1) * fc].astype(F32)).astype(BF16)
            duv_ref[:, f + j * fc:f + (j + 1) * fc] = (da * uv_ref[:, f + j * fc:f + (j + 1) * fc].astype(F32)).astype(BF16)

    return _pcall(
        body, name=name, grid=(s // t,),
        in_specs=[_row_spec(t, d), _row_spec(t, d), _vec_spec(d), _resident(wd.shape), _row_spec(t, 2 * f)],
        out_specs=[_row_spec(t, d), _row_spec(t, 2 * f), _vec_spec(d)],
        out_shape=[jax.ShapeDtypeStruct((s, d), BF16), jax.ShapeDtypeStruct((s, 2 * f), BF16), jax.ShapeDtypeStruct((1, d), F32)],
        compiler_params=_cparams("arbitrary"))(dxo, y, sg, wd, uv)


def _ffn_dh(duv, wgu, x, sc, dxo, *, name):
    s, d = x.shape
    f2 = wgu.shape[1]
    t = _rows(s, FFN_TM)

    def body(duv_ref, w_ref, x_ref, sc_ref, dxo_ref, dx_ref, dsc_ref, dsh_ref):
        @pl.when(pl.program_id(0) == 0)
        def _():
            dsc_ref[...] = jnp.zeros_like(dsc_ref)
            dsh_ref[...] = jnp.zeros_like(dsh_ref)

        dhv = _dot(duv_ref[...], w_ref[...], NT)
        xv = x_ref[...]
        r = lax.rsqrt(jnp.mean(xv * xv, axis=1, keepdims=True) + EPS)
        xn = xv * r
        dxn = dhv * (1.0 + sc_ref[...])
        dx_ref[...] = dxo_ref[...] + r * (dxn - xn * jnp.mean(dxn * xn, axis=1, keepdims=True))
        dsc_ref[...] += jnp.sum(dhv * xn, axis=0, keepdims=True)
        dsh_ref[...] += jnp.sum(dhv, axis=0, keepdims=True)

    return _pcall(
        body, name=name, grid=(s // t,),
        in_specs=[_row_spec(t, f2), _resident(wgu.shape), _row_spec(t, d), _vec_spec(d), _row_spec(t, d)],
        out_specs=[_row_spec(t, d), _vec_spec(d), _vec_spec(d)],
        out_shape=[jax.ShapeDtypeStruct((s, d), F32), jax.ShapeDtypeStruct((1, d), F32), jax.ShapeDtypeStruct((1, d), F32)],
        compiler_params=_cparams("arbitrary"))(duv, wgu, x, sc, dxo)


def _dh_pieces(pieces, w, x, sc, dxo, *, name):
    s, d = x.shape
    t = _rows(s, FFN_TM)
    widths = [p.shape[1] for p in pieces]
    offs = [sum(widths[:i]) for i in range(len(widths))]
    kd = sum(widths)
    npc = len(pieces)

    def body(*refs):
        p_refs = refs[:npc]
        w_ref, x_ref, sc_ref, dxo_ref, dx_ref, dsc_ref, dsh_ref, cat_ref = refs[npc:]

        @pl.when(pl.program_id(0) == 0)
        def _():
            dsc_ref[...] = jnp.zeros_like(dsc_ref)
            dsh_ref[...] = jnp.zeros_like(dsh_ref)

        dhv = None
        for p_ref, off, wd in zip(p_refs, offs, widths):
            pb = p_ref[...].astype(BF16)
            cat_ref[:, off:off + wd] = pb
            part = _dot(pb, w_ref[:, off:off + wd], NT)
            dhv = part if dhv is None else dhv + part
        xv = x_ref[...]
        r = lax.rsqrt(jnp.mean(xv * xv, axis=1, keepdims=True) + EPS)
        xn = xv * r
        dxn = dhv * (1.0 + sc_ref[...])
        dx_ref[...] = dxo_ref[...] + r * (dxn - xn * jnp.mean(dxn * xn, axis=1, keepdims=True))
        dsc_ref[...] += jnp.sum(dhv * xn, axis=0, keepdims=True)
        dsh_ref[...] += jnp.sum(dhv, axis=0, keepdims=True)

    return _pcall(
        body, name=name, grid=(s // t,),
        in_specs=[_row_spec(t, wd) for wd in widths] + [_resident(w.shape), _row_spec(t, d), _vec_spec(d), _row_spec(t, d)],
        out_specs=[_row_spec(t, d), _vec_spec(d), _vec_spec(d), _row_spec(t, kd)],
        out_shape=[jax.ShapeDtypeStruct((s, d), F32), jax.ShapeDtypeStruct((1, d), F32), jax.ShapeDtypeStruct((1, d), F32),
                   jax.ShapeDtypeStruct((s, kd), BF16)],
        compiler_params=_cparams("arbitrary"))(*pieces, w, x, sc, dxo)


def _ffn_fwd(x, sh, sc, g, wgu, wd, tag):
    h, uv, a = _ffn_up(x, sc, sh, wgu, name=f"{tag}_up")
    y, xo = _mm(a, wd, name=f"{tag}_down", tm=512, tn=1024, tk=wd.shape[0], res=x, scale=0.5 * g)
    return xo, (x, h, uv, a, y)


def _ffn_bwd(dxo, saved, sc, g, wgu, wd, tag):
    x, h, uv, a, y = saved
    dyb, duv, dgs = _ffn_dact(dxo, y, 0.5 * g, wd, uv, name=f"{tag}_dact")
    dx, dsc, dsh = _ffn_dh(duv, wgu, x, sc, dxo, name=f"{tag}_dh")
    dwgu = _mm_tn(h, duv, name=f"{tag}_dwgu", tm=1024, tn=1408, tk=512)
    dwd = _mm_tn(a, dyb, name=f"{tag}_dwd", tm=1408, tn=1024, tk=512)
    return dx, dwgu, dwd, dsh, dsc, 0.5 * dgs


def _layer_fwd(x0, mod, w, par, tag):
    s = x0.shape[0]
    sh1, sc1, g1, sh2, sc2, g2, sh3, sc3, g3 = (mod[i:i + 1] for i in range(N_MOD))
    x1, f1 = _ffn_fwd(x0, sh1, sc1, g1, w["gu1"], w["d1"], f"{tag}_ffn1")
    h2, proj, qkv = _norm_mm(x1, sc2, sh2, w["in"], 3 * SB_W + 3 * DIL_W, name=f"{tag}_in")
    vd = qkv[:, 3 * SB_W + 2 * DIL_W:]
    o_a = _sb_fwd(qkv, name=f"{tag}_sb")
    qd, kd = _dil_prep(proj, par["gq"], par["gk"], par["cos"], par["sin"], name=f"{tag}_dil_prep")
    nums, dens, mxs = [], [], []
    for _, r in DIL_PATTERNS:
        nu, de, mx = _dil_fwd(qd, kd, vd, r, name=f"{tag}_dil{r}")
        nums.append(nu.reshape(s, DIL_W))
        dens.append(de.reshape(s, DIL_W))
        mxs.append(mx.reshape(s, DIL_W))
    oh, st = _hgrn_fwd(proj, par["la"], par["lc"], name=f"{tag}_hgrn")
    ymix, od, mall, zall = _mix_out(o_a, nums, dens, mxs, oh, proj, par["hg"], name=f"{tag}_mix_out")
    out, x2 = _mm(ymix, w["out"], name=f"{tag}_out", tm=512, tn=1024, tk=1024, res=x1, scale=g2)
    x3, f2 = _ffn_fwd(x2, sh3, sc3, g3, w["gu2"], w["d2"], f"{tag}_ffn2")
    return x3, dict(f1=f1, f2=f2, x1=x1, h2=h2, proj=proj, qkv=qkv, vd=vd, o_a=o_a, qd=qd, kd=kd, oh=oh, st=st,
                    ymix=ymix, od=od, mall=mall, zall=zall, out=out)


def _layer_bwd(dx3, sv, mod, w, par, tag):
    s = dx3.shape[0]
    sh1, sc1, g1, sh2, sc2, g2, sh3, sc3, g3 = (mod[i:i + 1] for i in range(N_MOD))
    dx2, dwgu2, dwd2, dsh3, dsc3, dg3 = _ffn_bwd(dx3, sv["f2"], sc3, g3, w["gu2"], w["d2"], f"{tag}_ffn2")
    doutb, dg2 = _gate_bwd(dx2, sv["out"], g2, name=f"{tag}_dgate2")
    dmix = _mm(doutb, w["out"], name=f"{tag}_dmix", tb=True, tm=512, tn=1024, tk=1024)
    dwout = _mm_tn(sv["ymix"], doutb, name=f"{tag}_dwout", tm=1024, tn=1024, tk=512)
    proj = sv["proj"]
    doh, dgh, delta, dhg = _mix_out_bwd(dmix, sv["oh"], proj, par["hg"], sv["od"], name=f"{tag}_dmix_out")
    dqa, dka, dva = _sb_bwd(sv["qkv"], sv["o_a"], dmix, name=f"{tag}_dsb")
    do_d = dmix[:, SB_W:SB_W + DIL_W]
    dqs, dks, dvs = [], [], []
    for _, r in DIL_PATTERNS:
        a, b, c = _dil_bwd(sv["qd"], sv["kd"], sv["vd"], do_d, sv["mall"], sv["zall"], delta, r, name=f"{tag}_ddil{r}")
        dqs.append(a.reshape(s, DIL_W))
        dks.append(b.reshape(s, DIL_W))
        dvs.append(c.reshape(s, DIL_W))
    dqd, dkd, dvd, dgq, dgk = _dil_prep_bwd(proj, par["gq"], par["gk"], par["cos"], par["sin"], dqs, dks, dvs,
                                             name=f"{tag}_ddil_prep")
    dqh, dfh, dih, dla, dlc = _hgrn_bwd(proj, par["la"], par["lc"], sv["st"], doh, name=f"{tag}_dhgrn")
    dx1, dsc2, dsh2, dproj = _dh_pieces([dqa, dka, dva, dqd, dkd, dvd, dqh, dfh, dih, dgh], w["in"], sv["x1"], sc2, dx2,
                                         name=f"{tag}_dh2")
    dwin = _mm_tn(sv["h2"], dproj, name=f"{tag}_dwin", tm=1024, tn=1792, tk=512)
    dx0, dwgu1, dwd1, dsh1, dsc1, dg1 = _ffn_bwd(dx1, sv["f1"], sc1, g1, w["gu1"], w["d1"], f"{tag}_ffn1")
    dmod = jnp.concatenate([dsh1, dsc1, dg1, dsh2, dsc2, dg2, dsh3, dsc3, dg3], axis=0)
    fold = lambda v: v.reshape(2, HEAD_DIM).sum(axis=0)
    grads = dict(gu1=dwgu1, d1=dwd1, gu2=dwgu2, d2=dwd2, win=dwin, wout=dwout, dmod=dmod, gq=fold(dgq), gk=fold(dgk),
                 hg=dhg[0], la=dla[0], lc=dlc[0])
    return dx0, grads


def _pack(pieces):
    flat = jnp.concatenate([p.reshape(-1) for p in pieces])
    pad = (-flat.shape[0]) % (8 * LANES)
    return jnp.pad(flat, (0, pad)).reshape(-1, LANES)


def _unpack(flat, like):
    out, off = [], 0
    for p in like:
        out.append(flat[off:off + p.size].reshape(p.shape))
        off += p.size
    return out


def kernel(x, c, w_mod, b_mod, ffn1_w_gate, ffn1_w_up, ffn1_w_down, w_in, w_out, q_norm_g, k_norm_g, hgrn_norm_g, hgrn_lb_logits, ffn2_w_gate, ffn2_w_up, ffn2_w_down, loss_target, m_w_mod, m_b_mod, m_ffn1_w_gate, m_ffn1_w_up, m_ffn1_w_down, m_w_in, m_w_out, m_q_norm_g, m_k_norm_g, m_hgrn_norm_g, m_hgrn_lb_logits, m_ffn2_w_gate, m_ffn2_w_up, m_ffn2_w_down, v_w_mod, v_b_mod, v_ffn1_w_gate, v_ffn1_w_up, v_ffn1_w_down, v_w_in, v_w_out, v_q_norm_g, v_k_norm_g, v_hgrn_norm_g, v_hgrn_lb_logits, v_ffn2_w_gate, v_ffn2_w_up, v_ffn2_w_down):
    names = ["w_mod", "b_mod", "ffn1_w_gate", "ffn1_w_up", "ffn1_w_down", "w_in", "w_out", "q_norm_g", "k_norm_g",
             "hgrn_norm_g", "hgrn_lb_logits", "ffn2_w_gate", "ffn2_w_up", "ffn2_w_down"]
    wts = dict(zip(names, (w_mod, b_mod, ffn1_w_gate, ffn1_w_up, ffn1_w_down, w_in, w_out, q_norm_g, k_norm_g, hgrn_norm_g,
                           hgrn_lb_logits, ffn2_w_gate, ffn2_w_up, ffn2_w_down)))
    mom = dict(zip(names, (m_w_mod, m_b_mod, m_ffn1_w_gate, m_ffn1_w_up, m_ffn1_w_down, m_w_in, m_w_out, m_q_norm_g, m_k_norm_g,
                           m_hgrn_norm_g, m_hgrn_lb_logits, m_ffn2_w_gate, m_ffn2_w_up, m_ffn2_w_down)))
    var = dict(zip(names, (v_w_mod, v_b_mod, v_ffn1_w_gate, v_ffn1_w_up, v_ffn1_w_down, v_w_in, v_w_out, v_q_norm_g, v_k_norm_g,
                           v_hgrn_norm_g, v_hgrn_lb_logits, v_ffn2_w_gate, v_ffn2_w_up, v_ffn2_w_down)))
    depth = w_mod.shape[0]
    assert depth == 2 and x.shape[0] == 1
    s, d = x.shape[1:]
    assert s % (DIL_PATTERNS[-1][1] * QBLK) == 0 and d % LANES == 0
    xi, yi, ci = lax.axis_index("x"), lax.axis_index("y"), lax.axis_index("c")
    chip = 2 * xi + yi
    dev = 2 * chip + ci
    x0, tgt = x[0], loss_target[0]

    c8 = _exchange([c.reshape(d // LANES, LANES)], "all", False, name="gather_c")[0].reshape(8, d)
    ncol = w_mod.shape[2]
    b_loc = lax.dynamic_slice_in_dim(b_mod, chip * ncol, ncol, axis=1)
    m_loc = _mod_fwd(c8, w_mod, b_loc.reshape(depth, 1, ncol), name="mod_fwd")
    m_all = _exchange([m_loc], "chips", False, name="gather_mod")[0]
    mod = jnp.transpose(lax.dynamic_index_in_dim(m_all, dev, axis=2, keepdims=False), (1, 0, 2)).reshape(depth, N_MOD, d)

    col_sharded = ["ffn1_w_gate", "ffn1_w_up", "w_in", "ffn2_w_gate", "ffn2_w_up"]
    row_sharded = ["ffn1_w_down", "w_out", "ffn2_w_down"]
    big = col_sharded + row_sharded
    flat = [wts[n].astype(BF16).reshape(-1, wts[n].shape[-1]) for n in big]
    gathered = {n: g.reshape((4,) + wts[n].shape) for n, g in zip(big, _exchange(flat, "chips", False, name="gather_w"))}
    full = {}
    for n in col_sharded:
        g = gathered[n]
        full[n] = jnp.moveaxis(g, 0, 2).reshape(depth, g.shape[2], -1)
    for n in row_sharded:
        g = gathered[n]
        full[n] = jnp.moveaxis(g, 0, 1).reshape(depth, -1, g.shape[3])
    ws = [dict(gu1=jnp.concatenate([full["ffn1_w_gate"][l], full["ffn1_w_up"][l]], axis=1), d1=full["ffn1_w_down"][l],
               gu2=jnp.concatenate([full["ffn2_w_gate"][l], full["ffn2_w_up"][l]], axis=1), d2=full["ffn2_w_down"][l],
               **{"in": full["w_in"][l], "out": full["w_out"][l]}) for l in range(depth)]

    _, la, lc = _lb_prep(hgrn_lb_logits, name="lb_prep")
    cos, sin = _rope_tables(s)
    pars = [dict(gq=jnp.tile(q_norm_g[l], 2)[None], gk=jnp.tile(k_norm_g[l], 2)[None], hg=hgrn_norm_g[l][None],
                 la=la[l:l + 1], lc=lc[l:l + 1], cos=cos, sin=sin) for l in range(depth)]

    xs, saved = x0, []
    for l in range(depth):
        xs, sv = _layer_fwd(xs, mod[l], ws[l], pars[l], f"l{l}")
        saved.append(sv)
    dx, lpart = _loss_grad(xs, tgt, name="loss")
    grads = [None] * depth
    for l in reversed(range(depth)):
        dx, grads[l] = _layer_bwd(dx, saved[l], mod[l], ws[l], pars[l], f"l{l}")

    stack = lambda k: jnp.stack([grads[l][k] for l in range(depth)])
    small = [stack("dmod"), stack("gq"), stack("gk"), stack("hg"), stack("la"), stack("lc"), lpart[0, :1]]
    packed = _pack(small)
    allp = _exchange([packed], "all", False, name="gather_small")[0]
    tot = _unpack(_sum_slots(allp, name="sum_small").reshape(-1), small)
    g_b_mod = tot[0].reshape(depth, N_MOD * d)
    loss = tot[6][0]
    g_small = {"b_mod": g_b_mod, "q_norm_g": tot[1], "k_norm_g": tot[2], "hgrn_norm_g": tot[3],
               "hgrn_lb_logits": _lb_bwd(hgrn_lb_logits, tot[4], tot[5], name="lb_bwd")}

    dm_all = allp.reshape(8, -1)[:, :depth * N_MOD * d].reshape(8, depth, N_MOD * d)
    dm_loc = jnp.transpose(lax.dynamic_slice_in_dim(dm_all, chip * ncol, ncol, axis=2), (1, 0, 2))
    g_w_mod = _mod_bwd(c8.T, dm_loc, name="mod_bwd")

    fgrad = {
        "ffn1_w_gate": jnp.stack([grads[l]["gu1"][:, :grads[l]["gu1"].shape[1] // 2] for l in range(depth)]),
        "ffn1_w_up": jnp.stack([grads[l]["gu1"][:, grads[l]["gu1"].shape[1] // 2:] for l in range(depth)]),
        "ffn2_w_gate": jnp.stack([grads[l]["gu2"][:, :grads[l]["gu2"].shape[1] // 2] for l in range(depth)]),
        "ffn2_w_up": jnp.stack([grads[l]["gu2"][:, grads[l]["gu2"].shape[1] // 2:] for l in range(depth)]),
        "w_in": stack("win"), "ffn1_w_down": stack("d1"), "ffn2_w_down": stack("d2"), "w_out": stack("wout"),
    }
    by_chip = []
    for n in big:
        g = fgrad[n]
        if n in col_sharded:
            g = jnp.moveaxis(g.reshape(depth, g.shape[1], 4, -1), 2, 0)
        else:
            g = jnp.moveaxis(g.reshape(depth, 4, -1, g.shape[2]), 1, 0)
        by_chip.append(g.reshape(4, -1, g.shape[-1]))
    got = _exchange(by_chip, "chips", True, name="scatter_grads")
    parts = [_sum_slots(g, name=f"sum_{n}", out_dtype=BF16) for n, g in zip(big, got)]
    both = dict(zip(big, _exchange(parts, "sib", False, name="swap_grads")))

    outs = {}
    for n in names:
        w2 = wts[n].reshape(-1, wts[n].shape[-1])
        if n in both:
            gp = [both[n][0], both[n][1]]
        elif n == "w_mod":
            gp = [g_w_mod.reshape(w2.shape)]
        else:
            gp = [g_small[n].reshape(w2.shape)]
        res = _adamw(w2, gp, mom[n].reshape(w2.shape), var[n].reshape(w2.shape), name=f"adamw_{n}")
        outs[n] = [r.reshape(wts[n].shape) for r in res]
    return (loss, dx[None], *[outs[n][0] for n in names], *[outs[n][1] for n in names], *[outs[n][2] for n in names],
            *[outs[n][3] for n in names])
```

```python
import functools
import math

import jax
import jax.numpy as jnp
from jax import lax
from jax.experimental import pallas as pl
from jax.experimental.pallas import tpu as pltpu

F32 = jnp.float32
BF16 = jnp.bfloat16
MESH_ID = pl.DeviceIdType.MESH

HEAD_DIM = 64
SB_W = 256
DIL_W = 256
HG_W = 512
HG_D = 128
IN_W = 3 * SB_W + 3 * DIL_W + 4 * HG_W
MIX_W = SB_W + DIL_W + HG_W
DIL_PATTERNS = ((128, 1), (512, 4), (2048, 16))
ROPE_THETA = 10000.0
EPS = 1e-6
LB_FLOOR = 1e-30
NEG_BIG = -1e30
N_MOD = 9
ADAM_LR = 0.001
ADAM_B1 = 0.9
ADAM_B2 = 0.999
ADAM_EPS = 1e-08
ADAM_WD = 0.01
ADAM_STEP = 10

LANES = 128
QBLK = 128
DIL_TILE = 1024
HG_BLK = 16
HG_TILE = 256
SB_EXIT = 88.0
VMEM_LIMIT = 48 * 1024 * 1024
XCHG_CHUNKS = 8
XCHG_MIN_BYTES = 1 << 19

NN = (((1,), (0,)), ((), ()))
NT = (((1,), (1,)), ((), ()))
TN = (((0,), (0,)), ((), ()))


def _pcall(body, **kw):
    return pl.pallas_call(body, **kw)


def _cparams(*sem):
    return pltpu.CompilerParams(dimension_semantics=sem if sem else None, vmem_limit_bytes=VMEM_LIMIT)


def _dot(a, b, dims=NN):
    return lax.dot_general(a, b, dims, preferred_element_type=F32)


def _split(x, n):
    parts = []
    r = x
    for i in range(n):
        p = r.astype(BF16)
        parts.append(p)
        if i + 1 < n:
            r = r - p.astype(F32)
    return parts


def _xdot(x, m, n=2):
    return sum(_dot(p, m) for p in _split(x, n))


def _xdot_left(m, x, n=3):
    return sum(_dot(m, p) for p in _split(x, n))


def _iota(shape, dim):
    return lax.broadcasted_iota(jnp.int32, shape, dim)


def _sigmoid(x):
    return 1.0 / (1.0 + jnp.exp(-x))


def _tile(dim, pref, mult=LANES):
    t = (min(pref, dim) // mult) * mult
    while t >= mult:
        if dim % t == 0:
            return t
        t -= mult
    return dim


def _rows(dim, pref):
    return _tile(dim, pref, 8)


def _mm(a, b, *, name, tb=False, tm=512, tn=1024, tk=1024, out_dtype=F32, res=None, scale=None):
    m, kd = a.shape
    n = b.shape[0] if tb else b.shape[1]
    tm, tn, tk = _rows(m, tm), _tile(n, tn), _tile(kd, tk)
    nk = kd // tk
    epi = res is not None

    def body(*refs):
        if epi:
            a_ref, b_ref, r_ref, s_ref, o_ref, x_ref, acc = refs
        else:
            a_ref, b_ref, o_ref, acc = refs
        k = pl.program_id(2)

        @pl.when(k == 0)
        def _():
            acc[...] = jnp.zeros_like(acc)

        acc[...] += _dot(a_ref[...], b_ref[...], NT if tb else NN)

        @pl.when(k == nk - 1)
        def _():
            o_ref[...] = acc[...].astype(o_ref.dtype)
            if epi:
                x_ref[...] = r_ref[...] + s_ref[...] * acc[...]

    in_specs = [
        pl.BlockSpec((tm, tk), lambda i, j, k: (i, k)),
        pl.BlockSpec((tn, tk), lambda i, j, k: (j, k)) if tb else pl.BlockSpec((tk, tn), lambda i, j, k: (k, j)),
    ]
    out_shape = [jax.ShapeDtypeStruct((m, n), out_dtype)]
    out_specs = [pl.BlockSpec((tm, tn), lambda i, j, k: (i, j))]
    args = [a, b]
    if epi:
        in_specs += [pl.BlockSpec((tm, tn), lambda i, j, k: (i, j)), pl.BlockSpec((1, tn), lambda i, j, k: (0, j))]
        out_shape.append(jax.ShapeDtypeStruct((m, n), F32))
        out_specs.append(pl.BlockSpec((tm, tn), lambda i, j, k: (i, j)))
        args += [res, scale]
    out = _pcall(
        body, name=name, grid=(m // tm, n // tn, nk), in_specs=in_specs, out_specs=out_specs, out_shape=out_shape,
        scratch_shapes=[pltpu.VMEM((tm, tn), F32)], compiler_params=_cparams("parallel", "parallel", "arbitrary"),
    )(*args)
    return out if epi else out[0]


def _mm_tn(a, b, *, name, tm=1024, tn=1408, tk=512, out_dtype=BF16):
    s, m = a.shape
    n = b.shape[1]
    tm, tn, tk = _tile(m, tm), _tile(n, tn), _rows(s, tk)
    nk = s // tk

    def body(a_ref, b_ref, o_ref, acc):
        k = pl.program_id(2)

        @pl.when(k == 0)
        def _():
            acc[...] = jnp.zeros_like(acc)

        acc[...] += _dot(a_ref[...], b_ref[...], TN)

        @pl.when(k == nk - 1)
        def _():
            o_ref[...] = acc[...].astype(o_ref.dtype)

    return _pcall(
        body, name=name, grid=(m // tm, n // tn, nk),
        in_specs=[pl.BlockSpec((tk, tm), lambda i, j, k: (k, i)), pl.BlockSpec((tk, tn), lambda i, j, k: (k, j))],
        out_specs=pl.BlockSpec((tm, tn), lambda i, j, k: (i, j)), out_shape=jax.ShapeDtypeStruct((m, n), out_dtype),
        scratch_shapes=[pltpu.VMEM((tm, tn), F32)], compiler_params=_cparams("parallel", "parallel", "arbitrary"),
    )(a, b)


TE = 512


def _row_spec(t, w, col=0):
    return pl.BlockSpec((t, w), lambda i, col=col: (i, col))


def _vec_spec(w, col=0):
    return pl.BlockSpec((1, w), lambda i, col=col: (0, col))


def _norm_mod(x, sc, sh, *, name):
    s, d = x.shape
    t = _rows(s, TE)

    def body(x_ref, sc_ref, sh_ref, h_ref):
        xv = x_ref[...]
        r = lax.rsqrt(jnp.mean(xv * xv, axis=1, keepdims=True) + EPS)
        h_ref[...] = ((xv * r) * (1.0 + sc_ref[...]) + sh_ref[...]).astype(BF16)

    return _pcall(body, name=name, grid=(s // t,), in_specs=[_row_spec(t, d), _vec_spec(d), _vec_spec(d)],
                  out_specs=_row_spec(t, d), out_shape=jax.ShapeDtypeStruct((s, d), BF16),
                  compiler_params=_cparams("parallel"))(x, sc, sh)


def _norm_mod_bwd(dh, x, sc, dxo, *, name):
    s, d = x.shape
    t = _rows(s, TE)

    def body(dh_ref, x_ref, sc_ref, dxo_ref, dx_ref, dsc_ref, dsh_ref):
        @pl.when(pl.program_id(0) == 0)
        def _():
            dsc_ref[...] = jnp.zeros_like(dsc_ref)
            dsh_ref[...] = jnp.zeros_like(dsh_ref)

        xv = x_ref[...]
        dhv = dh_ref[...]
        r = lax.rsqrt(jnp.mean(xv * xv, axis=1, keepdims=True) + EPS)
        xn = xv * r
        dxn = dhv * (1.0 + sc_ref[...])
        dx_ref[...] = dxo_ref[...] + r * (dxn - xn * jnp.mean(dxn * xn, axis=1, keepdims=True))
        dsc_ref[...] += jnp.sum(dhv * xn, axis=0, keepdims=True)
        dsh_ref[...] += jnp.sum(dhv, axis=0, keepdims=True)

    return _pcall(
        body, name=name, grid=(s // t,),
        in_specs=[_row_spec(t, d), _row_spec(t, d), _vec_spec(d), _row_spec(t, d)],
        out_specs=[_row_spec(t, d), _vec_spec(d), _vec_spec(d)],
        out_shape=[jax.ShapeDtypeStruct((s, d), F32), jax.ShapeDtypeStruct((1, d), F32), jax.ShapeDtypeStruct((1, d), F32)],
        compiler_params=_cparams("arbitrary"))(dh, x, sc, dxo)


def _swiglu(uv, *, name):
    s, f2 = uv.shape
    f = f2 // 2
    t = _rows(s, 256)

    def body(uv_ref, a_ref):
        u = uv_ref[:, :f]
        v = uv_ref[:, f:]
        a_ref[...] = (u * _sigmoid(u) * v).astype(BF16)

    return _pcall(body, name=name, grid=(s // t,), in_specs=[_row_spec(t, f2)], out_specs=_row_spec(t, f),
                  out_shape=jax.ShapeDtypeStruct((s, f), BF16), compiler_params=_cparams("parallel"))(uv)


def _swiglu_bwd(da, uv, *, name):
    s, f2 = uv.shape
    f = f2 // 2
    t = _rows(s, 256)

    def body(da_ref, uv_ref, d_ref):
        u = uv_ref[:, :f]
        v = uv_ref[:, f:]
        dav = da_ref[...]
        sg = _sigmoid(u)
        d_ref[:, :f] = (dav * v * (sg * (1.0 + u * (1.0 - sg)))).astype(BF16)
        d_ref[:, f:] = (dav * (u * sg)).astype(BF16)

    return _pcall(body, name=name, grid=(s // t,), in_specs=[_row_spec(t, f), _row_spec(t, f2)],
                  out_specs=_row_spec(t, f2), out_shape=jax.ShapeDtypeStruct((s, f2), BF16),
                  compiler_params=_cparams("parallel"))(da, uv)


def _gate_bwd(dxo, y, sg, *, name):
    s, d = y.shape
    t = _rows(s, TE)

    def body(dxo_ref, y_ref, sg_ref, dy_ref, ds_ref):
        @pl.when(pl.program_id(0) == 0)
        def _():
            ds_ref[...] = jnp.zeros_like(ds_ref)

        dv = dxo_ref[...]
        dy_ref[...] = (sg_ref[...] * dv).astype(BF16)
        ds_ref[...] += jnp.sum(dv * y_ref[...], axis=0, keepdims=True)

    return _pcall(
        body, name=name, grid=(s // t,), in_specs=[_row_spec(t, d), _row_spec(t, d), _vec_spec(d)],
        out_specs=[_row_spec(t, d), _vec_spec(d)],
        out_shape=[jax.ShapeDtypeStruct((s, d), BF16), jax.ShapeDtypeStruct((1, d), F32)],
        compiler_params=_cparams("arbitrary"))(dxo, y, sg)


def _loss_grad(y, tgt, *, name):
    s, d = y.shape
    t = _rows(s, TE)
    nt = s // t

    def body(y_ref, t_ref, dy_ref, l_ref, acc):
        i = pl.program_id(0)

        @pl.when(i == 0)
        def _():
            acc[...] = jnp.zeros_like(acc)

        e = y_ref[...] - t_ref[...]
        dy_ref[...] = e * (1.0 / d)
        acc[...] += jnp.sum(e * e, axis=0, keepdims=True)

        @pl.when(i == nt - 1)
        def _():
            l_ref[...] = jnp.broadcast_to(jnp.sum(acc[...], axis=1, keepdims=True) * (0.5 / d), l_ref.shape)

    return _pcall(
        body, name=name, grid=(nt,), in_specs=[_row_spec(t, d), _row_spec(t, d)],
        out_specs=[_row_spec(t, d), pl.BlockSpec((1, LANES), lambda i: (0, 0))],
        out_shape=[jax.ShapeDtypeStruct((s, d), F32), jax.ShapeDtypeStruct((1, LANES), F32)],
        scratch_shapes=[pltpu.VMEM((1, d), F32)], compiler_params=_cparams("arbitrary"))(y, tgt)


SB_TQ = 256
SB_NK = SB_TQ // QBLK


def _sb_consts():
    r = _iota((QBLK, LANES), 0)
    c = _iota((QBLK, LANES), 1)
    ones = jnp.ones((QBLK, LANES), BF16)
    after = jnp.concatenate([jnp.where(r > c, 1.0, 0.0).astype(BF16), ones], axis=1)
    from_ = jnp.concatenate([jnp.where(r >= c, 1.0, 0.0).astype(BF16), ones], axis=1)
    return _iota((SB_TQ, LANES), 0), _iota((SB_TQ, LANES), 1), after, from_


def _sb_scores(qm, kb, strict):
    z = _dot(qm, kb, NT) * (HEAD_DIM ** -0.5)
    sp = jnp.log(1.0 + jnp.exp(-jnp.abs(z)))
    lnb = -(jnp.maximum(z, 0.0) + sp)
    lb = jnp.minimum(z, 0.0) - sp
    if strict is not None:
        lnb = jnp.where(strict, lnb, 0.0)
    return lnb, lb


def _sb_fwd(qkv, *, name):
    s = qkv.shape[0]
    nq = s // SB_TQ

    def body(q_ref, k_ref, v_ref, o_ref, *scr):
        acc, osc = scr[:4], scr[4:]
        qi = pl.program_id(0)
        row, lane, after, _ = _sb_consts()
        h0 = lane < HEAD_DIM
        q = q_ref[...]
        qms = []
        for p in range(2):
            qp = q[:, p * LANES:(p + 1) * LANES]
            qms += [jnp.where(h0, qp, jnp.zeros_like(qp)), jnp.where(h0, jnp.zeros_like(qp), qp)]

        def block(kj, mask):
            off = pl.multiple_of(kj * QBLK, QBLK)
            kbs = [k_ref[pl.ds(off, QBLK), p * LANES:(p + 1) * LANES] for p in range(2)]
            vbs = [v_ref[pl.ds(off, QBLK), p * LANES:(p + 1) * LANES] for p in range(2)]
            sc = [_sb_scores(qms[c], kbs[c // 2], mask) for c in range(4)]
            trs = [_xdot(sc[c][0], after) for c in range(4)]
            top = None
            for c in range(4):
                w = jnp.exp(sc[c][1] + trs[c][:, :QBLK] + acc[c][...])
                if mask is not None:
                    w = jnp.where(mask, w, 0.0)
                osc[c][...] += _xdot(w, vbs[c // 2])
                new = acc[c][...] + trs[c][:, QBLK:]
                acc[c][...] = new
                top = new if top is None else jnp.maximum(top, new)
            return jnp.max(top)

        for ref in scr:
            ref[...] = jnp.zeros_like(ref)
        top = None
        for j in reversed(range(SB_NK)):
            top = block(qi * SB_NK + j, (lane + j * QBLK) < row)
        lax.while_loop(lambda c: (c[0] >= 0) & (c[1] > -SB_EXIT), lambda c: (c[0] - 1, block(c[0], None)),
                       (qi * SB_NK - 1, top))
        for p in range(2):
            o_ref[:, p * LANES:(p + 1) * LANES] = jnp.where(h0, osc[2 * p][...], osc[2 * p + 1][...])

    return _pcall(
        body, name=name, grid=(nq,),
        in_specs=[pl.BlockSpec((SB_TQ, SB_W), lambda i: (i, 0)),
                  pl.BlockSpec((s, SB_W), lambda i: (0, 1)),
                  pl.BlockSpec((s, SB_W), lambda i: (0, 2))],
        out_specs=pl.BlockSpec((SB_TQ, SB_W), lambda i: (i, 0)),
        out_shape=jax.ShapeDtypeStruct((s, SB_W), F32),
        scratch_shapes=[pltpu.VMEM((SB_TQ, LANES), F32)] * 8,
        compiler_params=_cparams("arbitrary"))(qkv, qkv, qkv)


def _sb_bwd(qkv, o, dmix, *, name):
    s = qkv.shape[0]
    nq = s // SB_TQ
    scale = HEAD_DIM ** -0.5

    def body(q_ref, k_ref, v_ref, o_ref, do_ref, dq_ref, dk_ref, dv_ref, a0, a1, r0, r1, dqs, dks, dvs):
        acc, racc = (a0, a1), (r0, r1)
        i = pl.program_id(1)
        qi = nq - 1 - i
        row, lane, after, from_ = _sb_consts()
        klane = _iota((QBLK, LANES), 1)
        khms = (klane < HEAD_DIM, klane >= HEAD_DIM)

        @pl.when(i == 0)
        def _():
            dks[...] = jnp.zeros_like(dks)
            dvs[...] = jnp.zeros_like(dvs)

        q = q_ref[...]
        do = do_ref[...]
        dob = do.astype(BF16)
        dd = do * o_ref[...]
        dol = (do - dob.astype(F32)).astype(BF16)
        zero = jnp.zeros_like(q)
        hms = (lane < HEAD_DIM, lane >= HEAD_DIM)
        qms = [jnp.where(hm, q, zero) for hm in hms]
        doms = [jnp.where(hm, dob, zero) for hm in hms]
        dols = [jnp.where(hm, dol, zero) for hm in hms]
        dsums = [jnp.sum(jnp.where(hm, dd, 0.0), axis=1, keepdims=True) for hm in hms]

        def block(kj, mask):
            off = pl.multiple_of(kj * QBLK, QBLK)
            kb = k_ref[pl.ds(off, QBLK), :]
            vb = v_ref[pl.ds(off, QBLK), :]
            top, dq, dk, dv = None, None, None, None
            sc = [_sb_scores(qms[h], kb, mask) for h in range(2)]
            trs = [_xdot(sc[h][0], after) for h in range(2)]
            dws = [_dot(doms[h], vb, NT) + _dot(dols[h], vb, NT) for h in range(2)]
            for h in range(2):
                lb, tr = sc[h][1], trs[h]
                w = jnp.exp(lb + tr[:, :QBLK] + acc[h][...])
                if mask is not None:
                    w = jnp.where(mask, w, 0.0)
                g = w * dws[h]
                tg = _xdot(g, from_)
                before = dsums[h] - (tg[:, :QBLK] + racc[h][...])
                dz = g - jnp.exp(lb) * (g + before)
                if mask is not None:
                    dz = jnp.where(mask, dz, 0.0)
                dzb = (dz * scale).astype(BF16)
                dqh = _dot(dzb, jnp.where(khms[h], kb, jnp.zeros_like(kb)))
                dkh = _dot(dzb, qms[h], TN)
                dvh = _dot(w.astype(BF16), doms[h], TN)
                dq, dk, dv = (dqh, dkh, dvh) if h == 0 else (dq + dqh, dk + dkh, dv + dvh)
                new = acc[h][...] + tr[:, QBLK:]
                acc[h][...] = new
                racc[h][...] += tg[:, QBLK:]
                top = new if top is None else jnp.maximum(top, new)
            dqs[...] += dq
            dks[pl.ds(off, QBLK), :] += dk
            dvs[pl.ds(off, QBLK), :] += dv
            return jnp.max(top)

        for ref in (dqs, a0, a1, r0, r1):
            ref[...] = jnp.zeros_like(ref)
        top = None
        for j in reversed(range(SB_NK)):
            top = block(qi * SB_NK + j, (lane + j * QBLK) < row)
        lax.while_loop(lambda c: (c[0] >= 0) & (c[1] > -SB_EXIT), lambda c: (c[0] - 1, block(c[0], None)),
                       (qi * SB_NK - 1, top))
        dq_ref[...] = dqs[...]
        fin = pl.multiple_of(qi * SB_TQ, SB_TQ)
        dk_ref[...] = dks[pl.ds(fin, SB_TQ), :]
        dv_ref[...] = dvs[pl.ds(fin, SB_TQ), :]

    blk = lambda c0: pl.BlockSpec((SB_TQ, LANES), lambda p, i, c0=c0: (nq - 1 - i, c0 + p))
    return _pcall(
        body, name=name, grid=(2, nq),
        in_specs=[blk(0), pl.BlockSpec((s, LANES), lambda p, i: (0, 2 + p)), pl.BlockSpec((s, LANES), lambda p, i: (0, 4 + p)),
                  blk(0), blk(0)],
        out_specs=[blk(0), blk(0), blk(0)],
        out_shape=[jax.ShapeDtypeStruct((s, SB_W), F32)] * 3,
        scratch_shapes=[pltpu.VMEM((SB_TQ, LANES), F32)] * 5 + [pltpu.VMEM((s, LANES), F32), pltpu.VMEM((s, LANES), F32)],
        compiler_params=_cparams("arbitrary", "arbitrary"))(qkv, qkv, qkv, o, dmix)


def _seg_consts():
    r = _iota((LANES, LANES), 0)
    c = _iota((LANES, LANES), 1)
    return jnp.where((r >> 6) == (c >> 6), 1.0, 0.0).astype(BF16)


def _rot_half(x, lane):
    half = HEAD_DIM // 2
    return jnp.where((lane & (HEAD_DIM - 1)) < half, pltpu.roll(x, LANES - half, 1), pltpu.roll(x, half, 1))


def _rope_tables(s):
    half = HEAD_DIM // 2
    inv_freq = ROPE_THETA ** (-jnp.arange(half, dtype=F32) * 2.0 / HEAD_DIM)
    ang = jnp.arange(s, dtype=F32)[:, None] * inv_freq[None, :]
    cos, sin = jnp.cos(ang), jnp.sin(ang)
    return jnp.tile(jnp.concatenate([cos, cos], axis=1), (1, 2)), jnp.tile(jnp.concatenate([-sin, sin], axis=1), (1, 2))


def _dil_prep(proj, gq, gk, cos, sin, *, name):
    s = proj.shape[0]
    t = _rows(s, TE)
    c0 = 3 * SB_W // LANES

    def body(q_ref, k_ref, gq_ref, gk_ref, cos_ref, sin_ref, qo_ref, ko_ref):
        seg = _seg_consts()
        lane = _iota((t, LANES), 1)
        cs, sn = cos_ref[...], sin_ref[...]
        for x_ref, g_ref, o_ref, mul in ((q_ref, gq_ref, qo_ref, HEAD_DIM ** -0.5), (k_ref, gk_ref, ko_ref, 1.0)):
            for j in range(2):
                xv = x_ref[:, j * LANES:(j + 1) * LANES]
                ms = _xdot(xv * xv, seg, 3) * (1.0 / HEAD_DIM)
                xn = xv * lax.rsqrt(ms + EPS) * g_ref[...]
                o_ref[:, j * LANES:(j + 1) * LANES] = (xn * cs + _rot_half(xn, lane) * sn) * mul

    return _pcall(
        body, name=name, grid=(s // t,),
        in_specs=[pl.BlockSpec((t, DIL_W), lambda i: (i, c0 // 2)), pl.BlockSpec((t, DIL_W), lambda i: (i, c0 // 2 + 1)),
                  _vec_spec(LANES), _vec_spec(LANES), _row_spec(t, LANES), _row_spec(t, LANES)],
        out_specs=[_row_spec(t, DIL_W), _row_spec(t, DIL_W)],
        out_shape=[jax.ShapeDtypeStruct((s, DIL_W), F32)] * 2, compiler_params=_cparams("parallel"))(proj, proj, gq, gk, cos, sin)


def _dil_prep_bwd(proj, gq, gk, cos, sin, dqs, dks, dvs, *, name):
    s = proj.shape[0]
    t = _rows(s, TE)
    c0 = 3 * SB_W // LANES

    def body(q_ref, k_ref, gq_ref, gk_ref, cos_ref, sin_ref, a0, a1, a2, b0, b1, b2, c0_ref, c1_ref, c2_ref,
             dq_ref, dk_ref, dv_ref, dgq_ref, dgk_ref):
        @pl.when(pl.program_id(0) == 0)
        def _():
            dgq_ref[...] = jnp.zeros_like(dgq_ref)
            dgk_ref[...] = jnp.zeros_like(dgk_ref)

        dv_ref[...] = c0_ref[...] + c1_ref[...] + c2_ref[...]
        seg = _seg_consts()
        lane = _iota((t, LANES), 1)
        cs, sn = cos_ref[...], sin_ref[...]
        for x_ref, g_ref, parts, o_ref, dg_ref, mul in ((q_ref, gq_ref, (a0, a1, a2), dq_ref, dgq_ref, HEAD_DIM ** -0.5),
                                                          (k_ref, gk_ref, (b0, b1, b2), dk_ref, dgk_ref, 1.0)):
            for j in range(2):
                sl = slice(j * LANES, (j + 1) * LANES)
                dout = (parts[0][:, sl] + parts[1][:, sl] + parts[2][:, sl]) * mul
                dxn = dout * cs + _rot_half(dout * sn, lane)
                xv = x_ref[:, sl]
                r = lax.rsqrt(_xdot(xv * xv, seg, 3) * (1.0 / HEAD_DIM) + EPS)
                xh = xv * r
                dg_ref[...] += jnp.sum(dxn * xh, axis=0, keepdims=True)
                dxh = dxn * g_ref[...]
                o_ref[:, sl] = r * (dxh - xh * (_xdot(dxh * xh, seg, 3) * (1.0 / HEAD_DIM)))

    rs = _row_spec(t, DIL_W)
    return _pcall(
        body, name=name, grid=(s // t,),
        in_specs=[pl.BlockSpec((t, DIL_W), lambda i: (i, c0 // 2)), pl.BlockSpec((t, DIL_W), lambda i: (i, c0 // 2 + 1)),
                  _vec_spec(LANES), _vec_spec(LANES), _row_spec(t, LANES), _row_spec(t, LANES)] + [rs] * 9,
        out_specs=[rs, rs, rs, _vec_spec(LANES), _vec_spec(LANES)],
        out_shape=[jax.ShapeDtypeStruct((s, DIL_W), F32)] * 3 + [jax.ShapeDtypeStruct((1, LANES), F32)] * 2,
        compiler_params=_cparams("arbitrary"))(proj, proj, gq, gk, cos, sin, *dqs, *dks, *dvs)


def _dil_masks(n):
    row = _iota((QBLK, 2 * LANES), 0)
    col = _iota((QBLK, 2 * LANES), 1)
    return ((col < LANES) & (col >= row) & (n > 0)) | ((col >= LANES) & (col - LANES <= row))


DIL_V0 = (3 * SB_W + 2 * DIL_W) // LANES
DIL_DO0 = SB_W // LANES


def _dil_tiles(s, r):
    span = QBLK * r
    nsub = max(1, DIL_TILE // span)
    while s % (nsub * span):
        nsub -= 1
    return span, nsub


def _dil_rows(j, rho, span, r):
    return pl.ds(j * span + rho, QBLK, stride=r) if r > 1 else pl.ds(j * span, QBLK)


def _dil_fwd(q, k, proj, r, *, name):
    s = q.shape[0]
    span, nsub = _dil_tiles(s, r)
    tr = nsub * span

    def body(q_ref, kc_ref, kp_ref, vc_ref, vp_ref, num_ref, den_ref, mx_ref):
        n = pl.program_id(1)
        lane = _iota((QBLK, LANES), 1)
        h0 = lane < HEAD_DIM
        ones = jnp.ones((2 * QBLK, LANES), BF16)
        for j in range(nsub):
            valid = _dil_masks(n if j == 0 else 1)
            for rho in range(r):
                rows = _dil_rows(j, rho, span, r)
                before = _dil_rows(max(j - 1, 0), rho, span, r)
                k_prev, v_prev = (kp_ref, vp_ref) if j == 0 else (kc_ref, vc_ref)
                qv = q_ref[rows, :].astype(BF16)
                kk = jnp.concatenate([k_prev[before, :], kc_ref[rows, :]], axis=0).astype(BF16)
                vv = jnp.concatenate([jnp.concatenate([v_prev[before, :], vc_ref[rows, :]], axis=0).astype(BF16), ones], axis=1)
                res = []
                for h in range(2):
                    qm = jnp.where(h0 if h == 0 else ~h0, qv, jnp.zeros_like(qv))
                    sc = jnp.where(valid, _dot(qm, kk, NT), NEG_BIG)
                    mx = jnp.max(sc, axis=1, keepdims=True)
                    nd = _dot(jnp.exp(sc - mx).astype(BF16), vv)
                    res.append((nd[:, :LANES], nd[:, LANES:], mx))
                num_ref[rows, :] = jnp.where(h0, res[0][0], res[1][0])
                den_ref[rows, :] = jnp.where(h0, res[0][1], res[1][1])
                mx_ref[rows, :] = jnp.where(h0, res[0][2], res[1][2])

    cur = lambda c0: pl.BlockSpec((tr, LANES), lambda p, n, c0=c0: (n, c0 + p))
    prev = lambda c0: pl.BlockSpec((span, LANES), lambda p, n, c0=c0: (jnp.maximum(n * nsub - 1, 0), c0 + p))
    return _pcall(
        body, name=name, grid=(2, s // tr), in_specs=[cur(0), cur(0), prev(0), cur(DIL_V0), prev(DIL_V0)],
        out_specs=[cur(0), cur(0), cur(0)], out_shape=[jax.ShapeDtypeStruct((s, DIL_W), F32)] * 3,
        compiler_params=_cparams("parallel", "arbitrary"))(q, k, k, proj, proj)


def _dil_bwd(q, k, proj, dmix, mall, zall, delta, r, *, name):
    s = q.shape[0]
    span, nsub = _dil_tiles(s, r)
    tr = nsub * span
    nbig = s // tr

    def body(q_ref, kc_ref, kp_ref, vc_ref, vp_ref, do_ref, m_ref, z_ref, dl_ref, dq_ref, dk_ref, dv_ref, pk, pv):
        n = pl.program_id(1)
        lane = _iota((QBLK, LANES), 1)
        h0 = lane < HEAD_DIM

        @pl.when(n == 0)
        def _():
            pk[...] = jnp.zeros_like(pk)
            pv[...] = jnp.zeros_like(pv)

        @pl.when(n < nbig)
        def _():
            dk_ref[...] = pk[...]
            dv_ref[...] = pv[...]
            for j in range(nsub):
                valid = _dil_masks(n if j == 0 else 1)
                for rho in range(r):
                    rows = _dil_rows(j, rho, span, r)
                    before = _dil_rows(max(j - 1, 0), rho, span, r)
                    k_prev, v_prev = (kp_ref, vp_ref) if j == 0 else (kc_ref, vc_ref)
                    qv = q_ref[rows, :].astype(BF16)
                    dob = do_ref[rows, :].astype(BF16)
                    zero = jnp.zeros_like(qv)
                    kk = jnp.concatenate([k_prev[before, :], kc_ref[rows, :]], axis=0).astype(BF16)
                    vv = jnp.concatenate([v_prev[before, :], vc_ref[rows, :]], axis=0).astype(BF16)
                    mall_v, z_v, dl_v = m_ref[rows, :], z_ref[rows, :], dl_ref[rows, :]
                    dq, dk, dv = None, None, None
                    for h in range(2):
                        hm = h0 if h == 0 else ~h0
                        qm = jnp.where(hm, qv, zero)
                        dom = jnp.where(hm, dob, zero)
                        c = h * HEAD_DIM
                        sc = jnp.where(valid, _dot(qm, kk, NT), NEG_BIG)
                        pr = jnp.exp(sc - mall_v[:, c:c + 1]) * (1.0 / z_v[:, c:c + 1])
                        ds = (pr * (_dot(dom, vv, NT) - dl_v[:, c:c + 1])).astype(BF16)
                        parts = (_dot(ds, jnp.where(jnp.concatenate([hm, hm], axis=0), kk, jnp.zeros_like(kk))),
                                 _dot(ds, qm, TN), _dot(pr.astype(BF16), dom, TN))
                        dq, dk, dv = parts if h == 0 else (dq + parts[0], dk + parts[1], dv + parts[2])
                    dq_ref[rows, :] = dq
                    pk[rows, :] = dk[QBLK:]
                    pv[rows, :] = dv[QBLK:]
                    if j == 0:
                        last_span = _dil_rows(nsub - 1, rho, span, r)
                        dk_ref[last_span, :] += dk[:QBLK]
                        dv_ref[last_span, :] += dv[:QBLK]
                    else:
                        pk[before, :] += dk[:QBLK]
                        pv[before, :] += dv[:QBLK]

        @pl.when(n == nbig)
        def _():
            dk_ref[...] = pk[...]
            dv_ref[...] = pv[...]

    last = nbig - 1
    cur = lambda c0: pl.BlockSpec((tr, LANES), lambda p, n, c0=c0: (jnp.minimum(n, last), c0 + p))
    prev = lambda c0: pl.BlockSpec((span, LANES), lambda p, n, c0=c0: (jnp.maximum(jnp.minimum(n, last) * nsub - 1, 0), c0 + p))
    late = pl.BlockSpec((tr, LANES), lambda p, n: (jnp.maximum(n - 1, 0), p))
    return _pcall(
        body, name=name, grid=(2, nbig + 1),
        in_specs=[cur(0), cur(0), prev(0), cur(DIL_V0), prev(DIL_V0), cur(DIL_DO0), cur(0), cur(0), cur(0)],
        out_specs=[cur(0), late, late], out_shape=[jax.ShapeDtypeStruct((s, DIL_W), F32)] * 3,
        scratch_shapes=[pltpu.VMEM((tr, LANES), F32)] * 2,
        compiler_params=_cparams("parallel", "arbitrary"))(q, k, k, proj, proj, dmix, mall, zall, delta)


HG_SHIFT = HG_BLK.bit_length() - 1
HG_Q0, HG_F0, HG_I0 = (3 * SB_W + 3 * DIL_W) // HG_D, (3 * SB_W + 3 * DIL_W + HG_W) // HG_D, (3 * SB_W + 3 * DIL_W + 2 * HG_W) // HG_D


def _hg_scan(x):
    t = x.shape[0]
    half = HG_BLK // 2
    rb = _iota((t, LANES), 0) & (HG_BLK - 1)
    rh = rb & (half - 1)
    p = x
    for s in (1, 2, 4):
        p = p + jnp.where(rh >= s, pltpu.roll(p, s, 0), 0.0)
    h = jnp.where(rh == half - 1, p, 0.0)
    for s in (1, 2, 4):
        h = h + jnp.where(rh + s < half, pltpu.roll(h, t - s, 0), 0.0)
    first = rb < half
    pref = jnp.where(first, p, p + pltpu.roll(h, half, 0))
    total = h + jnp.where(first, pltpu.roll(h, t - half, 0), pltpu.roll(h, half, 0))
    return p, h, pref, total, first


def _hg_same(t):
    i = jnp.arange(t) >> HG_SHIFT
    return (i[:, None] == i[None, :]).astype(F32)


def _hg_own(t):
    i = jnp.arange(t) >> HG_SHIFT
    j = jnp.arange(t // HG_BLK * HG_D) // HG_D
    return (i[:, None] == j[None, :]).astype(BF16)


def _hg_diag(x, nb):
    return jnp.concatenate([x[b * HG_BLK:(b + 1) * HG_BLK, b * HG_D:(b + 1) * HG_D] for b in range(nb)], axis=0)


def _hg_inputs(qh, z, v, la, lc, t):
    lsg = jnp.minimum(z, 0.0) - jnp.log(1.0 + jnp.exp(-jnp.abs(z)))
    b = lc + lsg
    lf = jnp.maximum(la, b) + jnp.log(1.0 + jnp.exp(-jnp.abs(la - b)))
    f = jnp.exp(lf)
    sq = _sigmoid(qh)
    p, h, g, gl, first = _hg_scan(lf)
    k = 1.0 - f
    qs = qh * sq
    eq = jnp.where(first, 0.0, jnp.exp(jnp.minimum(p, 0.0)))
    ek = jnp.where(first, jnp.exp(jnp.minimum(h - p, 0.0)), 0.0)
    return dict(lf=lf, b=b, f=f, k=k, sq=sq, qs=qs, g=g, eg=jnp.exp(g), egl=jnp.exp(gl - g), dec=jnp.exp(gl),
                eq=eq, ek=ek, qx=(qs * eq).astype(BF16), kx=(k * ek).astype(BF16))


def _hgrn_fwd(proj, la, lc, *, name):
    s = proj.shape[0]
    t = _rows(s, HG_TILE)
    nt, nb = s // t, t // HG_BLK

    def body(q_ref, f_ref, i_ref, la_ref, lc_ref, same_ref, own_ref, o_ref, st_ref, state):
        @pl.when(pl.program_id(1) == 0)
        def _():
            state[...] = jnp.zeros_like(state)

        v = i_ref[...]
        a = _hg_inputs(q_ref[...], f_ref[...], v, la_ref[...], lc_ref[...], t)
        qs, k = a["qs"], a["k"]
        vb = v.astype(BF16)
        rb = _iota((t, LANES), 0) & (HG_BLK // 2 - 1)
        o = jnp.sum(qs * k, axis=1, keepdims=True) * v
        e = None
        for d in range(1, HG_BLK // 2):
            m = rb >= d
            fr = a["f"] if d == 1 else pltpu.roll(a["f"], d - 1, 0)
            e = fr if e is None else e * fr
            cd = jnp.sum(qs * pltpu.roll(k, d, 0) * e, axis=1, keepdims=True)
            o = o + jnp.where(m, cd, 0.0) * pltpu.roll(v, d, 0)
        cross = _dot(a["qx"], a["kx"], NT) * same_ref[...]
        o = o + _dot(cross.astype(BF16), vb)
        qt = (qs * a["eg"]).astype(BF16)
        kt = (k * a["egl"]).astype(BF16)
        upd = _dot(vb, jnp.tile(kt, (1, nb)) * own_ref[...], TN)
        st = state[...]
        for blk in range(nb):
            st_ref[blk * HG_D:(blk + 1) * HG_D, :] = st.astype(BF16)
            st = a["dec"][blk * HG_BLK:blk * HG_BLK + 1] * st + upd[:, blk * HG_D:(blk + 1) * HG_D]
        state[...] = st
        o_ref[...] = o + _hg_diag(_dot(qt, st_ref[...], NT), nb)

    col = lambda c0: pl.BlockSpec((t, HG_D), lambda hd, i, c0=c0: (i, c0 + hd))
    vec = pl.BlockSpec((1, HG_D), lambda hd, i: (0, hd))
    return _pcall(
        body, name=name, grid=(4, nt),
        in_specs=[col(HG_Q0), col(HG_F0), col(HG_I0), vec, vec, pl.BlockSpec((t, t), lambda hd, i: (0, 0)),
                  pl.BlockSpec((t, nb * HG_D), lambda hd, i: (0, 0))],
        out_specs=[col(0), pl.BlockSpec((None, nb * HG_D, HG_D), lambda hd, i: (hd, i, 0))],
        out_shape=[jax.ShapeDtypeStruct((s, HG_W), F32), jax.ShapeDtypeStruct((4, s // HG_BLK * HG_D, HG_D), BF16)],
        scratch_shapes=[pltpu.VMEM((HG_D, HG_D), F32)],
        compiler_params=_cparams("arbitrary", "arbitrary"))(proj, proj, proj, la, lc, _hg_same(t), _hg_own(t))


def _hgrn_bwd(proj, la, lc, st, doh, *, name):
    s = proj.shape[0]
    t = _rows(s, HG_TILE)
    nt, nb = s // t, t // HG_BLK

    def body(q_ref, f_ref, i_ref, la_ref, lc_ref, st_ref, do_ref, same_ref, own_ref, dq_ref, df_ref, di_ref, dla_ref, dlc_ref,
             dstate, dsb):
        @pl.when(pl.program_id(1) == 0)
        def _():
            dstate[...] = jnp.zeros_like(dstate)
            dla_ref[...] = jnp.zeros_like(dla_ref)
            dlc_ref[...] = jnp.zeros_like(dlc_ref)

        qh, z, v, do = q_ref[...], f_ref[...], i_ref[...], do_ref[...]
        la = la_ref[...]
        a = _hg_inputs(qh, z, v, la, lc_ref[...], t)
        qs, k, g = a["qs"], a["k"], a["g"]
        vb = v.astype(BF16)
        dob = do.astype(BF16)
        rb = _iota((t, LANES), 0) & (HG_BLK // 2 - 1)
        dc0 = jnp.sum(do * v, axis=1, keepdims=True)
        dq = dc0 * k
        dk = dc0 * qs
        dv = jnp.sum(qs * k, axis=1, keepdims=True) * do
        e = None
        for d in range(1, HG_BLK // 2):
            m = rb >= d
            fr = a["f"] if d == 1 else pltpu.roll(a["f"], d - 1, 0)
            e = fr if e is None else e * fr
            ks = pltpu.roll(k, d, 0)
            qe = qs * e
            cd = jnp.where(m, jnp.sum(qe * ks, axis=1, keepdims=True), 0.0)
            dcd = jnp.where(m, jnp.sum(do * pltpu.roll(v, d, 0), axis=1, keepdims=True), 0.0)
            dq = dq + dcd * ks * e
            dk = dk + pltpu.roll(dcd * qe, t - d, 0)
            dv = dv + pltpu.roll(cd * do, t - d, 0)
        same = same_ref[...]
        cross = (_dot(a["qx"], a["kx"], NT) * same).astype(BF16)
        dcross = (_dot(dob, vb, NT) * same).astype(BF16)
        dq = dq + _dot(dcross, a["kx"]) * a["eq"]
        dk = dk + _dot(dcross, a["qx"], TN) * a["ek"]
        dv = dv + _dot(cross, dob, TN)
        qt = (qs * a["eg"]).astype(BF16)
        kt = (k * a["egl"]).astype(BF16)
        own = own_ref[...]
        upd = _dot(dob, jnp.tile(qt, (1, nb)) * own, TN)
        ds = dstate[...]
        dgs = [None] * nb
        for blk in reversed(range(nb)):
            rows = slice(blk * HG_D, (blk + 1) * HG_D)
            dec = a["dec"][blk * HG_BLK:blk * HG_BLK + 1]
            dsb[rows, :] = ds.astype(BF16)
            dgs[blk] = jnp.broadcast_to(jnp.sum(ds * st_ref[rows, :].astype(F32), axis=0, keepdims=True) * dec, (HG_BLK, HG_D))
            ds = dec * ds + upd[:, rows]
        dstate[...] = ds
        dki = _dot(jnp.tile(vb, (1, nb)) * own, dsb[...]) * a["egl"]
        dq = dq + _dot(jnp.tile(dob, (1, nb)) * own, st_ref[...]) * a["eg"]
        dk = dk + dki
        dv = dv + _hg_diag(_dot(kt, dsb[...], NT), nb)
        x = qs * dq - k * dk
        _, _, xpre, xtot, _ = _hg_scan(x)
        _, _, _, ktot, _ = _hg_scan(k * dki)
        dlf = (xtot - xpre + x) + ktot + jnp.concatenate(dgs, axis=0) - a["f"] * dk
        wb = jnp.exp(a["b"] - a["lf"])
        wa = jnp.exp(la - a["lf"])
        sq = a["sq"]
        dq_ref[...] = dq * (sq * (1.0 + qh * (1.0 - sq)))
        df_ref[...] = dlf * wb * (1.0 - _sigmoid(z))
        di_ref[...] = dv
        dla_ref[...] += jnp.sum(dlf * wa, axis=0, keepdims=True)
        dlc_ref[...] += jnp.sum(dlf * wb, axis=0, keepdims=True)

    col = lambda c0: pl.BlockSpec((t, HG_D), lambda hd, i, c0=c0: (nt - 1 - i, c0 + hd))
    vec = pl.BlockSpec((1, HG_D), lambda hd, i: (0, hd))
    return _pcall(
        body, name=name, grid=(4, nt),
        in_specs=[col(HG_Q0), col(HG_F0), col(HG_I0), vec, vec,
                  pl.BlockSpec((None, nb * HG_D, HG_D), lambda hd, i: (hd, nt - 1 - i, 0)), col(0),
                  pl.BlockSpec((t, t), lambda hd, i: (0, 0)), pl.BlockSpec((t, nb * HG_D), lambda hd, i: (0, 0))],
        out_specs=[col(0), col(0), col(0), vec, vec],
        out_shape=[jax.ShapeDtypeStruct((s, HG_W), F32)] * 3 + [jax.ShapeDtypeStruct((1, HG_W), F32)] * 2,
        scratch_shapes=[pltpu.VMEM((HG_D, HG_D), F32), pltpu.VMEM((nb * HG_D, HG_D), BF16)],
        compiler_params=_cparams("arbitrary", "arbitrary"))(proj, proj, proj, la, lc, st, doh, _hg_same(t), _hg_own(t))


GH0 = (IN_W - HG_W) // HG_W


def _mix_out(o_a, nums, dens, mxs, oh, proj, hg, *, name):
    s = o_a.shape[0]
    t = _rows(s, TE)

    def body(oa_ref, n0, n1, n2, d0, d1, d2, m0, m1, m2, oh_ref, gh_ref, hg_ref, y_ref, od_ref, mall_ref, z_ref):
        y_ref[:, :SB_W] = oa_ref[...].astype(BF16)
        m = jnp.maximum(jnp.maximum(m0[...], m1[...]), m2[...])
        num = jnp.zeros((t, DIL_W), F32)
        z = jnp.zeros((t, DIL_W), F32)
        for n_ref, d_ref, m_ref in ((n0, d0, m0), (n1, d1, m1), (n2, d2, m2)):
            sc = jnp.exp(m_ref[...] - m)
            num = num + n_ref[...] * sc
            z = z + d_ref[...] * sc
        od = num / z
        od_ref[...] = od
        mall_ref[...] = m
        z_ref[...] = z
        y_ref[:, SB_W:SB_W + DIL_W] = od.astype(BF16)
        for h in range(4):
            sl = slice(h * HG_D, (h + 1) * HG_D)
            ov = oh_ref[:, sl]
            g = gh_ref[:, sl]
            r = lax.rsqrt(jnp.mean(ov * ov, axis=1, keepdims=True) + EPS)
            y_ref[:, SB_W + DIL_W + h * HG_D:SB_W + DIL_W + (h + 1) * HG_D] = (ov * r * hg_ref[...] * (g * _sigmoid(g))).astype(BF16)

    rd = _row_spec(t, DIL_W)
    return _pcall(
        body, name=name, grid=(s // t,),
        in_specs=[rd] * 10 + [_row_spec(t, HG_W), _row_spec(t, HG_W, GH0), _vec_spec(HG_D)],
        out_specs=[_row_spec(t, MIX_W), rd, rd, rd],
        out_shape=[jax.ShapeDtypeStruct((s, MIX_W), BF16)] + [jax.ShapeDtypeStruct((s, DIL_W), F32)] * 3,
        compiler_params=_cparams("parallel"))(o_a, *nums, *dens, *mxs, oh, proj, hg)


def _mix_out_bwd(dmix, oh, proj, hg, od, *, name):
    s = oh.shape[0]
    t = _rows(s, TE)

    def body(dm_ref, oh_ref, gh_ref, hg_ref, od_ref, doh_ref, dgh_ref, dl_ref, dhg_ref):
        @pl.when(pl.program_id(0) == 0)
        def _():
            dhg_ref[...] = jnp.zeros_like(dhg_ref)

        seg = _seg_consts()
        for j in range(2):
            sl = slice(j * LANES, (j + 1) * LANES)
            dl_ref[:, sl] = _xdot(dm_ref[:, SB_W + j * LANES:SB_W + (j + 1) * LANES] * od_ref[:, sl], seg, 3)
        hgv = hg_ref[...]
        for h in range(4):
            sl = slice(h * HG_D, (h + 1) * HG_D)
            dy = dm_ref[:, SB_W + DIL_W + h * HG_D:SB_W + DIL_W + (h + 1) * HG_D]
            ov = oh_ref[:, sl]
            g = gh_ref[:, sl]
            sg = _sigmoid(g)
            silu = g * sg
            r = lax.rsqrt(jnp.mean(ov * ov, axis=1, keepdims=True) + EPS)
            nrm = ov * r
            dhg_ref[...] += jnp.sum(dy * nrm * silu, axis=0, keepdims=True)
            dgh_ref[:, sl] = dy * nrm * hgv * (sg * (1.0 + g * (1.0 - sg)))
            dn = dy * hgv * silu
            doh_ref[:, sl] = r * (dn - nrm * jnp.mean(dn * nrm, axis=1, keepdims=True))

    rh = _row_spec(t, HG_W)
    return _pcall(
        body, name=name, grid=(s // t,),
        in_specs=[_row_spec(t, MIX_W), rh, _row_spec(t, HG_W, GH0), _vec_spec(HG_D), _row_spec(t, DIL_W)],
        out_specs=[rh, rh, _row_spec(t, DIL_W), _vec_spec(HG_D)],
        out_shape=[jax.ShapeDtypeStruct((s, HG_W), F32)] * 2 + [jax.ShapeDtypeStruct((s, DIL_W), F32), jax.ShapeDtypeStruct((1, HG_D), F32)],
        compiler_params=_cparams("arbitrary"))(dmix, oh, proj, hg, od)


def _lb_terms(l):
    l0, l1 = l[0:1], l[1:2]
    m = jnp.maximum(l0, l1)
    e0, e1 = jnp.exp(l0 - m), jnp.exp(l1 - m)
    s0, s1 = e0 / (e0 + e1), e1 / (e0 + e1)
    args = (s0 - s0, (s0 + s1) - s0)
    lbs = tuple(jnp.minimum(jnp.maximum(a, 0.0), 1.0 - EPS) for a in args)
    return s0, s1, args, lbs


def _lb_prep(logits, *, name):
    def body(l_ref, lb_ref, la_ref, lc_ref):
        _, _, _, lbs = _lb_terms(l_ref[...])
        lb = jnp.concatenate(lbs, axis=0)
        lb_ref[...] = lb
        la_ref[...] = jnp.log(jnp.maximum(lb, LB_FLOOR))
        lc_ref[...] = jnp.log1p(-lb)

    return _pcall(body, name=name, out_shape=[jax.ShapeDtypeStruct(logits.shape, F32)] * 3)(logits)


def _lb_bwd(logits, dla, dlc, *, name):
    def half(hi, eq):
        return jnp.where(hi, 1.0, jnp.where(eq, 0.5, 0.0))

    def body(l_ref, dla_ref, dlc_ref, o_ref):
        s0, s1, args, lbs = _lb_terms(l_ref[...])
        da = []
        for i in range(2):
            a, lb = args[i], lbs[i]
            dlb = dla_ref[i:i + 1] * half(lb > LB_FLOOR, lb == LB_FLOOR) / jnp.maximum(lb, LB_FLOOR) - dlc_ref[i:i + 1] / (1.0 - lb)
            t = jnp.maximum(a, 0.0)
            da.append(dlb * half(a > 0.0, a == 0.0) * half(t < 1.0 - EPS, t == 1.0 - EPS))
        ds0 = (da[0] + da[1]) - (da[0] + da[1])
        ds1 = da[1]
        dot = s0 * ds0 + s1 * ds1
        o_ref[...] = jnp.concatenate([s0 * (ds0 - dot), s1 * (ds1 - dot)], axis=0)

    return _pcall(body, name=name, out_shape=jax.ShapeDtypeStruct(logits.shape, F32))(logits, dla, dlc)


def _mod_fwd(c8, w, b, *, name):
    _, d, n = w.shape
    tn = _tile(n, 768)

    def body(c_ref, w_ref, b_ref, o_ref):
        cv = c_ref[...]
        o_ref[...] = _dot((cv * _sigmoid(cv)).astype(BF16), w_ref[...].astype(BF16)) + b_ref[...]

    return _pcall(
        body, name=name, grid=(2, n // tn),
        in_specs=[pl.BlockSpec((8, d), lambda l, j: (0, 0)), pl.BlockSpec((None, d, tn), lambda l, j: (l, 0, j)),
                  pl.BlockSpec((None, 1, tn), lambda l, j: (l, 0, j))],
        out_specs=pl.BlockSpec((None, 8, tn), lambda l, j: (l, 0, j)),
        out_shape=jax.ShapeDtypeStruct((2, 8, n), F32), compiler_params=_cparams("parallel", "parallel"))(c8, w, b)


def _mod_bwd(ct, dm, *, name):
    d = ct.shape[0]
    n = dm.shape[2]
    tn = _tile(n, 768)

    def body(c_ref, dm_ref, o_ref):
        cv = c_ref[...]
        sc = cv * _sigmoid(cv)
        dv = dm_ref[...]
        acc = sc[:, 0:1] * dv[0:1, :]
        for b in range(1, 8):
            acc = acc + sc[:, b:b + 1] * dv[b:b + 1, :]
        o_ref[...] = acc

    return _pcall(
        body, name=name, grid=(2, n // tn),
        in_specs=[pl.BlockSpec((d, 8), lambda l, j: (0, 0)), pl.BlockSpec((None, 8, tn), lambda l, j: (l, 0, j))],
        out_specs=pl.BlockSpec((None, d, tn), lambda l, j: (l, 0, j)),
        out_shape=jax.ShapeDtypeStruct((2, d, n), F32), compiler_params=_cparams("parallel", "parallel"))(ct, dm)


_PEERS = {
    "chips": ((1, 0, 0), (0, 1, 0), (1, 1, 0)),
    "all": tuple((a, b, c) for a in (0, 1) for b in (0, 1) for c in (0, 1) if a + b + c),
    "sib": ((0, 0, 1),),
}
_SLOTS = {"chips": 4, "all": 8, "sib": 2}


def _slot(kind, x, y, c):
    return {"chips": 2 * x + y, "all": 4 * x + 2 * y + c, "sib": c}[kind]


def _exchange(arrs, kind, scatter, *, name):
    n = len(arrs)
    peers = _PEERS[kind]
    ns = _SLOTS[kind]
    np_ = len(peers)

    def pieces(a):
        shape = a.shape[1:] if scatter else a.shape
        if len(shape) == 2:
            for k in (XCHG_CHUNKS, XCHG_CHUNKS // 2, XCHG_CHUNKS // 4):
                if k > 1 and shape[0] % (16 * k) == 0 and shape[0] * shape[1] * a.dtype.itemsize >= k * XCHG_MIN_BYTES:
                    return [(i * (shape[0] // k), shape[0] // k) for i in range(k)]
        return [None]

    chunks = [pieces(a) for a in arrs]
    base = [sum(len(c) for c in chunks[:a]) * np_ for a in range(n)]
    total = sum(len(c) for c in chunks) * np_

    def body(*refs):
        ins, outs = refs[:n], refs[n:2 * n]
        send, recv, loc = refs[2 * n:]
        x, y, c = lax.axis_index("x"), lax.axis_index("y"), lax.axis_index("c")
        me = _slot(kind, x, y, c)
        copies = []
        for a in range(n):
            own = pltpu.make_async_copy(ins[a].at[me] if scatter else ins[a], outs[a].at[me], loc.at[a])
            own.start()
            copies.append(own)
            for j, (dx, dy, dc) in enumerate(peers):
                px, py, pc = (1 - x if dx else x), (1 - y if dy else y), (1 - c if dc else c)
                src = ins[a].at[_slot(kind, px, py, pc)] if scatter else ins[a]
                for i, piece in enumerate(chunks[a]):
                    rows = slice(None) if piece is None else pl.ds(piece[0], piece[1])
                    sem = base[a] + j * len(chunks[a]) + i
                    cp = pltpu.make_async_remote_copy(
                        src_ref=src if piece is None else src.at[rows], dst_ref=outs[a].at[me] if piece is None else outs[a].at[me, rows],
                        send_sem=send.at[sem], recv_sem=recv.at[sem], device_id=(px, py, pc), device_id_type=MESH_ID)
                    cp.start()
                    copies.append(cp)
        for cp in copies:
            cp.wait()

    hbm = pl.BlockSpec(memory_space=pl.ANY)
    shapes = [jax.ShapeDtypeStruct(a.shape if scatter else (ns,) + a.shape, a.dtype) for a in arrs]
    return _pcall(
        body, name=name, in_specs=[hbm] * n, out_specs=[hbm] * n, out_shape=shapes,
        scratch_shapes=[pltpu.SemaphoreType.DMA((total,)), pltpu.SemaphoreType.DMA((total,)), pltpu.SemaphoreType.DMA((n,))],
    )(*arrs)


def _sum_slots(a, *, name, out_dtype=F32):
    ns, r, c = a.shape
    t = _tile(r, max(16, (1 << 18) // c // 16 * 16), 16)

    def body(a_ref, o_ref):
        acc = a_ref[0].astype(F32)
        for i in range(1, ns):
            acc = acc + a_ref[i].astype(F32)
        o_ref[...] = acc.astype(o_ref.dtype)

    return _pcall(body, name=name, grid=(r // t,), in_specs=[pl.BlockSpec((ns, t, c), lambda i: (0, i, 0))],
                  out_specs=pl.BlockSpec((t, c), lambda i: (i, 0)), out_shape=jax.ShapeDtypeStruct((r, c), out_dtype),
                  compiler_params=_cparams("parallel"))(a)


def _adamw(w, gparts, m, v, *, name):
    r, c = w.shape
    t = _tile(r, max(16, (1 << 17) // c // 16 * 16), 16)
    ng = len(gparts)

    def body(*refs):
        w_ref, m_ref, v_ref = refs[0], refs[1 + ng], refs[2 + ng]
        g_ref, d_ref, nm_ref, nv_ref = refs[3 + ng:]
        g = refs[1][...].astype(F32)
        for i in range(1, ng):
            g = g + refs[1 + i][...].astype(F32)
        mn = ADAM_B1 * m_ref[...] + (1.0 - ADAM_B1) * g
        vn = ADAM_B2 * v_ref[...] + (1.0 - ADAM_B2) * (g * g)
        m_hat = mn / (1.0 - ADAM_B1 ** ADAM_STEP)
        v_hat = vn / (1.0 - ADAM_B2 ** ADAM_STEP)
        g_ref[...] = g
        d_ref[...] = -ADAM_LR * (m_hat / (jnp.sqrt(v_hat) + ADAM_EPS) + ADAM_WD * w_ref[...])
        nm_ref[...] = mn
        nv_ref[...] = vn

    spec = pl.BlockSpec((t, c), lambda i: (i, 0))
    return _pcall(body, name=name, grid=(r // t,), in_specs=[spec] * (3 + ng), out_specs=[spec] * 4,
                  out_shape=[jax.ShapeDtypeStruct((r, c), F32)] * 4, compiler_params=_cparams("parallel"))(w, *gparts, m, v)


FFN_TM = 512
FFN_CHUNK = 1408


def _resident(shape):
    return pl.BlockSpec(shape, lambda i: (0,) * len(shape), pipeline_mode=pl.Buffered(1))


def _ffn_up(x, sc, sh, wgu, *, name):
    s, d = x.shape
    f = wgu.shape[1] // 2
    t, fc = _rows(s, FFN_TM), _tile(f, FFN_CHUNK)

    def body(x_ref, sc_ref, sh_ref, w_ref, h_ref, uv_ref, a_ref):
        xv = x_ref[...]
        r = lax.rsqrt(jnp.mean(xv * xv, axis=1, keepdims=True) + EPS)
        hb = ((xv * r) * (1.0 + sc_ref[...]) + sh_ref[...]).astype(BF16)
        h_ref[...] = hb
        for j in range(f // fc):
            u = _dot(hb, w_ref[:, j * fc:(j + 1) * fc])
            v = _dot(hb, w_ref[:, f + j * fc:f + (j + 1) * fc])
            sg = _sigmoid(u)
            silu = u * sg
            uv_ref[:, j * fc:(j + 1) * fc] = (v * (sg * (1.0 + u * (1.0 - sg)))).astype(BF16)
            uv_ref[:, f + j * fc:f + (j + 1) * fc] = silu.astype(BF16)
            a_ref[:, j * fc:(j + 1) * fc] = (silu * v).astype(BF16)

    return _pcall(
        body, name=name, grid=(s // t,), in_specs=[_row_spec(t, d), _vec_spec(d), _vec_spec(d), _resident(wgu.shape)],
        out_specs=[_row_spec(t, d), _row_spec(t, 2 * f), _row_spec(t, f)],
        out_shape=[jax.ShapeDtypeStruct((s, d), BF16), jax.ShapeDtypeStruct((s, 2 * f), BF16), jax.ShapeDtypeStruct((s, f), BF16)],
        compiler_params=_cparams("parallel"))(x, sc, sh, wgu)


def _norm_mm(x, sc, sh, w, nb, *, name):
    s, d = x.shape
    n = w.shape[1]
    t, nc = _rows(s, FFN_TM), _tile(n, 1792)
    assert nb <= nc

    def body(x_ref, sc_ref, sh_ref, w_ref, h_ref, o_ref, ob_ref):
        xv = x_ref[...]
        r = lax.rsqrt(jnp.mean(xv * xv, axis=1, keepdims=True) + EPS)
        hb = ((xv * r) * (1.0 + sc_ref[...]) + sh_ref[...]).astype(BF16)
        h_ref[...] = hb
        for j in range(n // nc):
            part = _dot(hb, w_ref[:, j * nc:(j + 1) * nc])
            o_ref[:, j * nc:(j + 1) * nc] = part
            if j == 0:
                ob_ref[...] = part[:, :nb].astype(BF16)

    return _pcall(
        body, name=name, grid=(s // t,), in_specs=[_row_spec(t, d), _vec_spec(d), _vec_spec(d), _resident(w.shape)],
        out_specs=[_row_spec(t, d), _row_spec(t, n), _row_spec(t, nb)],
        out_shape=[jax.ShapeDtypeStruct((s, d), BF16), jax.ShapeDtypeStruct((s, n), F32), jax.ShapeDtypeStruct((s, nb), BF16)],
        compiler_params=_cparams("parallel"))(x, sc, sh, w)


def _ffn_dact(dxo, y, sg, wd, uv, *, name):
    s, d = y.shape
    f = wd.shape[0]
    t, fc = _rows(s, FFN_TM), _tile(f, FFN_CHUNK)

    def body(dxo_ref, y_ref, sg_ref, w_ref, uv_ref, dy_ref, duv_ref, ds_ref):
        @pl.when(pl.program_id(0) == 0)
        def _():
            ds_ref[...] = jnp.zeros_like(ds_ref)

        dv = dxo_ref[...]
        dyb = (sg_ref[...] * dv).astype(BF16)
        dy_ref[...] = dyb
        ds_ref[...] += jnp.sum(dv * y_ref[...], axis=0, keepdims=True)
        for j in range(f // fc):
            da = _dot(dyb, w_ref[j * fc:(j + 1) * fc, :], NT)
            duv_ref[:, j * fc:(j + 1) * fc] = (da * uv_ref[:, j * fc:(j + 1) * fc].astype(F32)).astype(BF16)
            duv_ref[:, f + j * fc:f + (j + 1) * fc] = (da * uv_ref[:, f + j * fc:f + (j + 1) * fc].astype(F32)).astype(BF16)

    return _pcall(
        body, name=name, grid=(s // t,),
        in_specs=[_row_spec(t, d), _row_spec(t, d), _vec_spec(d), _resident(wd.shape), _row_spec(t, 2 * f)],
        out_specs=[_row_spec(t, d), _row_spec(t, 2 * f), _vec_spec(d)],
        out_shape=[jax.ShapeDtypeStruct((s, d), BF16), jax.ShapeDtypeStruct((s, 2 * f), BF16), jax.ShapeDtypeStruct((1, d), F32)],
        compiler_params=_cparams("arbitrary"))(dxo, y, sg, wd, uv)


def _ffn_dh(duv, wgu, x, sc, dxo, *, name):
    s, d = x.shape
    f2 = wgu.shape[1]
    t = _rows(s, FFN_TM)

    def body(duv_ref, w_ref, x_ref, sc_ref, dxo_ref, dx_ref, dsc_ref, dsh_ref):
        @pl.when(pl.program_id(0) == 0)
        def _():
            dsc_ref[...] = jnp.zeros_like(dsc_ref)
            dsh_ref[...] = jnp.zeros_like(dsh_ref)

        dhv = _dot(duv_ref[...], w_ref[...], NT)
        xv = x_ref[...]
        r = lax.rsqrt(jnp.mean(xv * xv, axis=1, keepdims=True) + EPS)
        xn = xv * r
        dxn = dhv * (1.0 + sc_ref[...])
        dx_ref[...] = dxo_ref[...] + r * (dxn - xn * jnp.mean(dxn * xn, axis=1, keepdims=True))
        dsc_ref[...] += jnp.sum(dhv * xn, axis=0, keepdims=True)
        dsh_ref[...] += jnp.sum(dhv, axis=0, keepdims=True)

    return _pcall(
        body, name=name, grid=(s // t,),
        in_specs=[_row_spec(t, f2), _resident(wgu.shape), _row_spec(t, d), _vec_spec(d), _row_spec(t, d)],
        out_specs=[_row_spec(t, d), _vec_spec(d), _vec_spec(d)],
        out_shape=[jax.ShapeDtypeStruct((s, d), F32), jax.ShapeDtypeStruct((1, d), F32), jax.ShapeDtypeStruct((1, d), F32)],
        compiler_params=_cparams("arbitrary"))(duv, wgu, x, sc, dxo)


def _dh_pieces(pieces, w, x, sc, dxo, *, name):
    s, d = x.shape
    t = _rows(s, FFN_TM)
    widths = [p.shape[1] for p in pieces]
    offs = [sum(widths[:i]) for i in range(len(widths))]
    kd = sum(widths)
    npc = len(pieces)

    def body(*refs):
        p_refs = refs[:npc]
        w_ref, x_ref, sc_ref, dxo_ref, dx_ref, dsc_ref, dsh_ref, cat_ref = refs[npc:]

        @pl.when(pl.program_id(0) == 0)
        def _():
            dsc_ref[...] = jnp.zeros_like(dsc_ref)
            dsh_ref[...] = jnp.zeros_like(dsh_ref)

        dhv = None
        for p_ref, off, wd in zip(p_refs, offs, widths):
            pb = p_ref[...].astype(BF16)
            cat_ref[:, off:off + wd] = pb
            part = _dot(pb, w_ref[:, off:off + wd], NT)
            dhv = part if dhv is None else dhv + part
        xv = x_ref[...]
        r = lax.rsqrt(jnp.mean(xv * xv, axis=1, keepdims=True) + EPS)
        xn = xv * r
        dxn = dhv * (1.0 + sc_ref[...])
        dx_ref[...] = dxo_ref[...] + r * (dxn - xn * jnp.mean(dxn * xn, axis=1, keepdims=True))
        dsc_ref[...] += jnp.sum(dhv * xn, axis=0, keepdims=True)
        dsh_ref[...] += jnp.sum(dhv, axis=0, keepdims=True)

    return _pcall(
        body, name=name, grid=(s // t,),
        in_specs=[_row_spec(t, wd) for wd in widths] + [_resident(w.shape), _row_spec(t, d), _vec_spec(d), _row_spec(t, d)],
        out_specs=[_row_spec(t, d), _vec_spec(d), _vec_spec(d), _row_spec(t, kd)],
        out_shape=[jax.ShapeDtypeStruct((s, d), F32), jax.ShapeDtypeStruct((1, d), F32), jax.ShapeDtypeStruct((1, d), F32),
                   jax.ShapeDtypeStruct((s, kd), BF16)],
        compiler_params=_cparams("arbitrary"))(*pieces, w, x, sc, dxo)


def _ffn_fwd(x, sh, sc, g, wgu, wd, tag):
    h, uv, a = _ffn_up(x, sc, sh, wgu, name=f"{tag}_up")
    y, xo = _mm(a, wd, name=f"{tag}_down", tm=512, tn=1024, tk=wd.shape[0], res=x, scale=0.5 * g)
    return xo, (x, h, uv, a, y)


def _ffn_bwd(dxo, saved, sc, g, wgu, wd, tag):
    x, h, uv, a, y = saved
    dyb, duv, dgs = _ffn_dact(dxo, y, 0.5 * g, wd, uv, name=f"{tag}_dact")
    dx, dsc, dsh = _ffn_dh(duv, wgu, x, sc, dxo, name=f"{tag}_dh")
    dwgu = _mm_tn(h, duv, name=f"{tag}_dwgu", tm=1024, tn=1408, tk=512)
    dwd = _mm_tn(a, dyb, name=f"{tag}_dwd", tm=1408, tn=1024, tk=512)
    return dx, dwgu, dwd, dsh, dsc, 0.5 * dgs


def _layer_fwd(x0, mod, w, par, tag):
    s = x0.shape[0]
    sh1, sc1, g1, sh2, sc2, g2, sh3, sc3, g3 = (mod[i:i + 1] for i in range(N_MOD))
    x1, f1 = _ffn_fwd(x0, sh1, sc1, g1, w["gu1"], w["d1"], f"{tag}_ffn1")
    h2, proj, qkv = _norm_mm(x1, sc2, sh2, w["in"], 3 * SB_W, name=f"{tag}_in")
    o_a = _sb_fwd(qkv, name=f"{tag}_sb")
    qd, kd = _dil_prep(proj, par["gq"], par["gk"], par["cos"], par["sin"], name=f"{tag}_dil_prep")
    nums, dens, mxs = [], [], []
    for _, r in DIL_PATTERNS:
        nu, de, mx = _dil_fwd(qd, kd, proj, r, name=f"{tag}_dil{r}")
        nums.append(nu)
        dens.append(de)
        mxs.append(mx)
    oh, st = _hgrn_fwd(proj, par["la"], par["lc"], name=f"{tag}_hgrn")
    ymix, od, mall, zall = _mix_out(o_a, nums, dens, mxs, oh, proj, par["hg"], name=f"{tag}_mix_out")
    out, x2 = _mm(ymix, w["out"], name=f"{tag}_out", tm=512, tn=1024, tk=1024, res=x1, scale=g2)
    x3, f2 = _ffn_fwd(x2, sh3, sc3, g3, w["gu2"], w["d2"], f"{tag}_ffn2")
    return x3, dict(f1=f1, f2=f2, x1=x1, h2=h2, proj=proj, qkv=qkv, o_a=o_a, qd=qd, kd=kd, oh=oh, st=st,
                    ymix=ymix, od=od, mall=mall, zall=zall, out=out)


def _layer_bwd(dx3, sv, mod, w, par, tag):
    s = dx3.shape[0]
    sh1, sc1, g1, sh2, sc2, g2, sh3, sc3, g3 = (mod[i:i + 1] for i in range(N_MOD))
    dx2, dwgu2, dwd2, dsh3, dsc3, dg3 = _ffn_bwd(dx3, sv["f2"], sc3, g3, w["gu2"], w["d2"], f"{tag}_ffn2")
    doutb, dg2 = _gate_bwd(dx2, sv["out"], g2, name=f"{tag}_dgate2")
    dmix = _mm(doutb, w["out"], name=f"{tag}_dmix", tb=True, tm=512, tn=1024, tk=1024)
    dwout = _mm_tn(sv["ymix"], doutb, name=f"{tag}_dwout", tm=1024, tn=1024, tk=512)
    proj = sv["proj"]
    doh, dgh, delta, dhg = _mix_out_bwd(dmix, sv["oh"], proj, par["hg"], sv["od"], name=f"{tag}_dmix_out")
    dqa, dka, dva = _sb_bwd(sv["qkv"], sv["o_a"], dmix, name=f"{tag}_dsb")
    dqs, dks, dvs = [], [], []
    for _, r in DIL_PATTERNS:
        a, b, c = _dil_bwd(sv["qd"], sv["kd"], proj, dmix, sv["mall"], sv["zall"], delta, r, name=f"{tag}_ddil{r}")
        dqs.append(a)
        dks.append(b)
        dvs.append(c)
    dqd, dkd, dvd, dgq, dgk = _dil_prep_bwd(proj, par["gq"], par["gk"], par["cos"], par["sin"], dqs, dks, dvs,
                                             name=f"{tag}_ddil_prep")
    dqh, dfh, dih, dla, dlc = _hgrn_bwd(proj, par["la"], par["lc"], sv["st"], doh, name=f"{tag}_dhgrn")
    dx1, dsc2, dsh2, dproj = _dh_pieces([dqa, dka, dva, dqd, dkd, dvd, dqh, dfh, dih, dgh], w["in"], sv["x1"], sc2, dx2,
                                         name=f"{tag}_dh2")
    dwin = _mm_tn(sv["h2"], dproj, name=f"{tag}_dwin", tm=1024, tn=1792, tk=512)
    dx0, dwgu1, dwd1, dsh1, dsc1, dg1 = _ffn_bwd(dx1, sv["f1"], sc1, g1, w["gu1"], w["d1"], f"{tag}_ffn1")
    dmod = jnp.concatenate([dsh1, dsc1, dg1, dsh2, dsc2, dg2, dsh3, dsc3, dg3], axis=0)
    fold = lambda v: v.reshape(2, HEAD_DIM).sum(axis=0)
    grads = dict(gu1=dwgu1, d1=dwd1, gu2=dwgu2, d2=dwd2, win=dwin, wout=dwout, dmod=dmod, gq=fold(dgq), gk=fold(dgk),
                 hg=dhg[0], la=dla[0], lc=dlc[0])
    return dx0, grads


def _pack(pieces):
    flat = jnp.concatenate([p.reshape(-1) for p in pieces])
    pad = (-flat.shape[0]) % (8 * LANES)
    return jnp.pad(flat, (0, pad)).reshape(-1, LANES)


def _unpack(flat, like):
    out, off = [], 0
    for p in like:
        out.append(flat[off:off + p.size].reshape(p.shape))
        off += p.size
    return out


def kernel(x, c, w_mod, b_mod, ffn1_w_gate, ffn1_w_up, ffn1_w_down, w_in, w_out, q_norm_g, k_norm_g, hgrn_norm_g, hgrn_lb_logits, ffn2_w_gate, ffn2_w_up, ffn2_w_down, loss_target, m_w_mod, m_b_mod, m_ffn1_w_gate, m_ffn1_w_up, m_ffn1_w_down, m_w_in, m_w_out, m_q_norm_g, m_k_norm_g, m_hgrn_norm_g, m_hgrn_lb_logits, m_ffn2_w_gate, m_ffn2_w_up, m_ffn2_w_down, v_w_mod, v_b_mod, v_ffn1_w_gate, v_ffn1_w_up, v_ffn1_w_down, v_w_in, v_w_out, v_q_norm_g, v_k_norm_g, v_hgrn_norm_g, v_hgrn_lb_logits, v_ffn2_w_gate, v_ffn2_w_up, v_ffn2_w_down):
    names = ["w_mod", "b_mod", "ffn1_w_gate", "ffn1_w_up", "ffn1_w_down", "w_in", "w_out", "q_norm_g", "k_norm_g",
             "hgrn_norm_g", "hgrn_lb_logits", "ffn2_w_gate", "ffn2_w_up", "ffn2_w_down"]
    wts = dict(zip(names, (w_mod, b_mod, ffn1_w_gate, ffn1_w_up, ffn1_w_down, w_in, w_out, q_norm_g, k_norm_g, hgrn_norm_g,
                           hgrn_lb_logits, ffn2_w_gate, ffn2_w_up, ffn2_w_down)))
    mom = dict(zip(names, (m_w_mod, m_b_mod, m_ffn1_w_gate, m_ffn1_w_up, m_ffn1_w_down, m_w_in, m_w_out, m_q_norm_g, m_k_norm_g,
                           m_hgrn_norm_g, m_hgrn_lb_logits, m_ffn2_w_gate, m_ffn2_w_up, m_ffn2_w_down)))
    var = dict(zip(names, (v_w_mod, v_b_mod, v_ffn1_w_gate, v_ffn1_w_up, v_ffn1_w_down, v_w_in, v_w_out, v_q_norm_g, v_k_norm_g,
                           v_hgrn_norm_g, v_hgrn_lb_logits, v_ffn2_w_gate, v_ffn2_w_up, v_ffn2_w_down)))
    depth = w_mod.shape[0]
    assert depth == 2 and x.shape[0] == 1
    s, d = x.shape[1:]
    assert s % (DIL_PATTERNS[-1][1] * QBLK) == 0 and d % LANES == 0
    xi, yi, ci = lax.axis_index("x"), lax.axis_index("y"), lax.axis_index("c")
    chip = 2 * xi + yi
    dev = 2 * chip + ci
    x0, tgt = x[0], loss_target[0]

    c8 = _exchange([c.reshape(d // LANES, LANES)], "all", False, name="gather_c")[0].reshape(8, d)
    ncol = w_mod.shape[2]
    b_loc = lax.dynamic_slice_in_dim(b_mod, chip * ncol, ncol, axis=1)
    m_loc = _mod_fwd(c8, w_mod, b_loc.reshape(depth, 1, ncol), name="mod_fwd")
    m_all = _exchange([m_loc], "chips", False, name="gather_mod")[0]
    mod = jnp.transpose(lax.dynamic_index_in_dim(m_all, dev, axis=2, keepdims=False), (1, 0, 2)).reshape(depth, N_MOD, d)

    col_sharded = ["ffn1_w_gate", "ffn1_w_up", "w_in", "ffn2_w_gate", "ffn2_w_up"]
    row_sharded = ["ffn1_w_down", "w_out", "ffn2_w_down"]
    big = col_sharded + row_sharded
    flat = [wts[n].astype(BF16).reshape(-1, wts[n].shape[-1]) for n in big]
    gathered = {n: g.reshape((4,) + wts[n].shape) for n, g in zip(big, _exchange(flat, "chips", False, name="gather_w"))}
    full = {}
    for n in col_sharded:
        g = gathered[n]
        full[n] = jnp.moveaxis(g, 0, 2).reshape(depth, g.shape[2], -1)
    for n in row_sharded:
        g = gathered[n]
        full[n] = jnp.moveaxis(g, 0, 1).reshape(depth, -1, g.shape[3])
    ws = [dict(gu1=jnp.concatenate([full["ffn1_w_gate"][l], full["ffn1_w_up"][l]], axis=1), d1=full["ffn1_w_down"][l],
               gu2=jnp.concatenate([full["ffn2_w_gate"][l], full["ffn2_w_up"][l]], axis=1), d2=full["ffn2_w_down"][l],
               **{"in": full["w_in"][l], "out": full["w_out"][l]}) for l in range(depth)]

    _, la, lc = _lb_prep(hgrn_lb_logits, name="lb_prep")
    cos, sin = _rope_tables(s)
    pars = [dict(gq=jnp.tile(q_norm_g[l], 2)[None], gk=jnp.tile(k_norm_g[l], 2)[None], hg=hgrn_norm_g[l][None],
                 la=la[l:l + 1], lc=lc[l:l + 1], cos=cos, sin=sin) for l in range(depth)]

    xs, saved = x0, []
    for l in range(depth):
        xs, sv = _layer_fwd(xs, mod[l], ws[l], pars[l], f"l{l}")
        saved.append(sv)
    dx, lpart = _loss_grad(xs, tgt, name="loss")
    grads = [None] * depth
    for l in reversed(range(depth)):
        dx, grads[l] = _layer_bwd(dx, saved[l], mod[l], ws[l], pars[l], f"l{l}")

    stack = lambda k: jnp.stack([grads[l][k] for l in range(depth)])
    small = [stack("dmod"), stack("gq"), stack("gk"), stack("hg"), stack("la"), stack("lc"), lpart[0, :1]]
    packed = _pack(small)
    allp = _exchange([packed], "all", False, name="gather_small")[0]
    tot = _unpack(_sum_slots(allp, name="sum_small").reshape(-1), small)
    g_b_mod = tot[0].reshape(depth, N_MOD * d)
    loss = tot[6][0]
    g_small = {"b_mod": g_b_mod, "q_norm_g": tot[1], "k_norm_g": tot[2], "hgrn_norm_g": tot[3],
               "hgrn_lb_logits": _lb_bwd(hgrn_lb_logits, tot[4], tot[5], name="lb_bwd")}

    dm_all = allp.reshape(8, -1)[:, :depth * N_MOD * d].reshape(8, depth, N_MOD * d)
    dm_loc = jnp.transpose(lax.dynamic_slice_in_dim(dm_all, chip * ncol, ncol, axis=2), (1, 0, 2))
    g_w_mod = _mod_bwd(c8.T, dm_loc, name="mod_bwd")

    fgrad = {
        "ffn1_w_gate": jnp.stack([grads[l]["gu1"][:, :grads[l]["gu1"].shape[1] // 2] for l in range(depth)]),
        "ffn1_w_up": jnp.stack([grads[l]["gu1"][:, grads[l]["gu1"].shape[1] // 2:] for l in range(depth)]),
        "ffn2_w_gate": jnp.stack([grads[l]["gu2"][:, :grads[l]["gu2"].shape[1] // 2] for l in range(depth)]),
        "ffn2_w_up": jnp.stack([grads[l]["gu2"][:, grads[l]["gu2"].shape[1] // 2:] for l in range(depth)]),
        "w_in": stack("win"), "ffn1_w_down": stack("d1"), "ffn2_w_down": stack("d2"), "w_out": stack("wout"),
    }
    by_chip = []
    for n in big:
        g = fgrad[n]
        if n in col_sharded:
            g = jnp.moveaxis(g.reshape(depth, g.shape[1], 4, -1), 2, 0)
        else:
            g = jnp.moveaxis(g.reshape(depth, 4, -1, g.shape[2]), 1, 0)
        by_chip.append(g.reshape(4, -1, g.shape[-1]))
    got = _exchange(by_chip, "chips", True, name="scatter_grads")
    parts = [_sum_slots(g, name=f"sum_{n}", out_dtype=BF16) for n, g in zip(big, got)]
    both = dict(zip(big, _exchange(parts, "sib", False, name="swap_grads")))

    outs = {}
    for n in names:
        w2 = wts[n].reshape(-1, wts[n].shape[-1])
        if n in both:
            gp = [both[n][0], both[n][1]]
        elif n == "w_mod":
            gp = [g_w_mod.reshape(w2.shape)]
        else:
            gp = [g_small[n].reshape(w2.shape)]
        res = _adamw(w2, gp, mom[n].reshape(w2.shape), var[n].reshape(w2.shape), name=f"adamw_{n}")
        outs[n] = [r.reshape(wts[n].shape) for r in res]
    return (loss, dx[None], *[outs[n][0] for n in names], *[outs[n][1] for n in names], *[outs[n][2] for n in names],
            *[outs[n][3] for n in names])
```

```python
import functools
import math

import jax
import jax.numpy as jnp
from jax import lax
from jax.experimental import pallas as pl
from jax.experimental.pallas import tpu as pltpu

F32 = jnp.float32
BF16 = jnp.bfloat16
MESH_ID = pl.DeviceIdType.MESH

HEAD_DIM = 64
SB_W = 256
DIL_W = 256
HG_W = 512
HG_D = 128
IN_W = 3 * SB_W + 3 * DIL_W + 4 * HG_W
MIX_W = SB_W + DIL_W + HG_W
DIL_PATTERNS = ((128, 1), (512, 4), (2048, 16))
ROPE_THETA = 10000.0
EPS = 1e-6
LB_FLOOR = 1e-30
NEG_BIG = -1e30
N_MOD = 9
ADAM_LR = 0.001
ADAM_B1 = 0.9
ADAM_B2 = 0.999
ADAM_EPS = 1e-08
ADAM_WD = 0.01
ADAM_STEP = 10

LANES = 128
QBLK = 128
DIL_TILE = 1024
HG_BLK = 16
HG_TILE = 256
SB_EXIT = 88.0
VMEM_LIMIT = 48 * 1024 * 1024
XCHG_CHUNKS = 8
XCHG_MIN_BYTES = 1 << 19

NN = (((1,), (0,)), ((), ()))
NT = (((1,), (1,)), ((), ()))
TN = (((0,), (0,)), ((), ()))


def _pcall(body, **kw):
    return pl.pallas_call(body, **kw)


def _cparams(*sem):
    return pltpu.CompilerParams(dimension_semantics=sem if sem else None, vmem_limit_bytes=VMEM_LIMIT)


def _dot(a, b, dims=NN):
    return lax.dot_general(a, b, dims, preferred_element_type=F32)


def _split(x, n):
    parts = []
    r = x
    for i in range(n):
        p = r.astype(BF16)
        parts.append(p)
        if i + 1 < n:
            r = r - p.astype(F32)
    return parts


def _xdot(x, m, n=2):
    return sum(_dot(p, m) for p in _split(x, n))


def _xdot_left(m, x, n=3):
    return sum(_dot(m, p) for p in _split(x, n))


def _iota(shape, dim):
    return lax.broadcasted_iota(jnp.int32, shape, dim)


def _sigmoid(x):
    return 1.0 / (1.0 + jnp.exp(-x))


def _tile(dim, pref, mult=LANES):
    t = (min(pref, dim) // mult) * mult
    while t >= mult:
        if dim % t == 0:
            return t
        t -= mult
    return dim


def _rows(dim, pref):
    return _tile(dim, pref, 8)


def _mm(a, b, *, name, tb=False, tm=512, tn=1024, tk=1024, out_dtype=F32, res=None, scale=None):
    m, kd = a.shape
    n = b.shape[0] if tb else b.shape[1]
    tm, tn, tk = _rows(m, tm), _tile(n, tn), _tile(kd, tk)
    nk = kd // tk
    epi = res is not None

    def body(*refs):
        if epi:
            a_ref, b_ref, r_ref, s_ref, o_ref, x_ref, acc = refs
        else:
            a_ref, b_ref, o_ref, acc = refs
        k = pl.program_id(2)

        @pl.when(k == 0)
        def _():
            acc[...] = jnp.zeros_like(acc)

        acc[...] += _dot(a_ref[...], b_ref[...], NT if tb else NN)

        @pl.when(k == nk - 1)
        def _():
            o_ref[...] = acc[...].astype(o_ref.dtype)
            if epi:
                x_ref[...] = r_ref[...] + s_ref[...] * acc[...]

    in_specs = [
        pl.BlockSpec((tm, tk), lambda i, j, k: (i, k)),
        pl.BlockSpec((tn, tk), lambda i, j, k: (j, k)) if tb else pl.BlockSpec((tk, tn), lambda i, j, k: (k, j)),
    ]
    out_shape = [jax.ShapeDtypeStruct((m, n), out_dtype)]
    out_specs = [pl.BlockSpec((tm, tn), lambda i, j, k: (i, j))]
    args = [a, b]
    if epi:
        in_specs += [pl.BlockSpec((tm, tn), lambda i, j, k: (i, j)), pl.BlockSpec((1, tn), lambda i, j, k: (0, j))]
        out_shape.append(jax.ShapeDtypeStruct((m, n), F32))
        out_specs.append(pl.BlockSpec((tm, tn), lambda i, j, k: (i, j)))
        args += [res, scale]
    out = _pcall(
        body, name=name, grid=(m // tm, n // tn, nk), in_specs=in_specs, out_specs=out_specs, out_shape=out_shape,
        scratch_shapes=[pltpu.VMEM((tm, tn), F32)], compiler_params=_cparams("parallel", "parallel", "arbitrary"),
    )(*args)
    return out if epi else out[0]


def _mm_tn(a, b, *, name, tm=1024, tn=1408, tk=512, out_dtype=BF16):
    s, m = a.shape
    n = b.shape[1]
    tm, tn, tk = _tile(m, tm), _tile(n, tn), _rows(s, tk)
    nk = s // tk

    def body(a_ref, b_ref, o_ref, acc):
        k = pl.program_id(2)

        @pl.when(k == 0)
        def _():
            acc[...] = jnp.zeros_like(acc)

        acc[...] += _dot(a_ref[...], b_ref[...], TN)

        @pl.when(k == nk - 1)
        def _():
            o_ref[...] = acc[...].astype(o_ref.dtype)

    return _pcall(
        body, name=name, grid=(m // tm, n // tn, nk),
        in_specs=[pl.BlockSpec((tk, tm), lambda i, j, k: (k, i)), pl.BlockSpec((tk, tn), lambda i, j, k: (k, j))],
        out_specs=pl.BlockSpec((tm, tn), lambda i, j, k: (i, j)), out_shape=jax.ShapeDtypeStruct((m, n), out_dtype),
        scratch_shapes=[pltpu.VMEM((tm, tn), F32)], compiler_params=_cparams("parallel", "parallel", "arbitrary"),
    )(a, b)


TE = 512


def _row_spec(t, w, col=0):
    return pl.BlockSpec((t, w), lambda i, col=col: (i, col))


def _vec_spec(w, col=0):
    return pl.BlockSpec((1, w), lambda i, col=col: (0, col))


def _norm_mod(x, sc, sh, *, name):
    s, d = x.shape
    t = _rows(s, TE)

    def body(x_ref, sc_ref, sh_ref, h_ref):
        xv = x_ref[...]
        r = lax.rsqrt(jnp.mean(xv * xv, axis=1, keepdims=True) + EPS)
        h_ref[...] = ((xv * r) * (1.0 + sc_ref[...]) + sh_ref[...]).astype(BF16)

    return _pcall(body, name=name, grid=(s // t,), in_specs=[_row_spec(t, d), _vec_spec(d), _vec_spec(d)],
                  out_specs=_row_spec(t, d), out_shape=jax.ShapeDtypeStruct((s, d), BF16),
                  compiler_params=_cparams("parallel"))(x, sc, sh)


def _norm_mod_bwd(dh, x, sc, dxo, *, name):
    s, d = x.shape
    t = _rows(s, TE)

    def body(dh_ref, x_ref, sc_ref, dxo_ref, dx_ref, dsc_ref, dsh_ref):
        @pl.when(pl.program_id(0) == 0)
        def _():
            dsc_ref[...] = jnp.zeros_like(dsc_ref)
            dsh_ref[...] = jnp.zeros_like(dsh_ref)

        xv = x_ref[...]
        dhv = dh_ref[...]
        r = lax.rsqrt(jnp.mean(xv * xv, axis=1, keepdims=True) + EPS)
        xn = xv * r
        dxn = dhv * (1.0 + sc_ref[...])
        dx_ref[...] = dxo_ref[...] + r * (dxn - xn * jnp.mean(dxn * xn, axis=1, keepdims=True))
        dsc_ref[...] += jnp.sum(dhv * xn, axis=0, keepdims=True)
        dsh_ref[...] += jnp.sum(dhv, axis=0, keepdims=True)

    return _pcall(
        body, name=name, grid=(s // t,),
        in_specs=[_row_spec(t, d), _row_spec(t, d), _vec_spec(d), _row_spec(t, d)],
        out_specs=[_row_spec(t, d), _vec_spec(d), _vec_spec(d)],
        out_shape=[jax.ShapeDtypeStruct((s, d), F32), jax.ShapeDtypeStruct((1, d), F32), jax.ShapeDtypeStruct((1, d), F32)],
        compiler_params=_cparams("arbitrary"))(dh, x, sc, dxo)


def _swiglu(uv, *, name):
    s, f2 = uv.shape
    f = f2 // 2
    t = _rows(s, 256)

    def body(uv_ref, a_ref):
        u = uv_ref[:, :f]
        v = uv_ref[:, f:]
        a_ref[...] = (u * _sigmoid(u) * v).astype(BF16)

    return _pcall(body, name=name, grid=(s // t,), in_specs=[_row_spec(t, f2)], out_specs=_row_spec(t, f),
                  out_shape=jax.ShapeDtypeStruct((s, f), BF16), compiler_params=_cparams("parallel"))(uv)


def _swiglu_bwd(da, uv, *, name):
    s, f2 = uv.shape
    f = f2 // 2
    t = _rows(s, 256)

    def body(da_ref, uv_ref, d_ref):
        u = uv_ref[:, :f]
        v = uv_ref[:, f:]
        dav = da_ref[...]
        sg = _sigmoid(u)
        d_ref[:, :f] = (dav * v * (sg * (1.0 + u * (1.0 - sg)))).astype(BF16)
        d_ref[:, f:] = (dav * (u * sg)).astype(BF16)

    return _pcall(body, name=name, grid=(s // t,), in_specs=[_row_spec(t, f), _row_spec(t, f2)],
                  out_specs=_row_spec(t, f2), out_shape=jax.ShapeDtypeStruct((s, f2), BF16),
                  compiler_params=_cparams("parallel"))(da, uv)


def _gate_bwd(dxo, y, sg, *, name):
    s, d = y.shape
    t = _rows(s, TE)

    def body(dxo_ref, y_ref, sg_ref, dy_ref, ds_ref):
        @pl.when(pl.program_id(0) == 0)
        def _():
            ds_ref[...] = jnp.zeros_like(ds_ref)

        dv = dxo_ref[...]
        dy_ref[...] = (sg_ref[...] * dv).astype(BF16)
        ds_ref[...] += jnp.sum(dv * y_ref[...], axis=0, keepdims=True)

    return _pcall(
        body, name=name, grid=(s // t,), in_specs=[_row_spec(t, d), _row_spec(t, d), _vec_spec(d)],
        out_specs=[_row_spec(t, d), _vec_spec(d)],
        out_shape=[jax.ShapeDtypeStruct((s, d), BF16), jax.ShapeDtypeStruct((1, d), F32)],
        compiler_params=_cparams("arbitrary"))(dxo, y, sg)


def _loss_grad(y, tgt, *, name):
    s, d = y.shape
    t = _rows(s, TE)
    nt = s // t

    def body(y_ref, t_ref, dy_ref, l_ref, acc):
        i = pl.program_id(0)

        @pl.when(i == 0)
        def _():
            acc[...] = jnp.zeros_like(acc)

        e = y_ref[...] - t_ref[...]
        dy_ref[...] = e * (1.0 / d)
        acc[...] += jnp.sum(e * e, axis=0, keepdims=True)

        @pl.when(i == nt - 1)
        def _():
            l_ref[...] = jnp.broadcast_to(jnp.sum(acc[...], axis=1, keepdims=True) * (0.5 / d), l_ref.shape)

    return _pcall(
        body, name=name, grid=(nt,), in_specs=[_row_spec(t, d), _row_spec(t, d)],
        out_specs=[_row_spec(t, d), pl.BlockSpec((1, LANES), lambda i: (0, 0))],
        out_shape=[jax.ShapeDtypeStruct((s, d), F32), jax.ShapeDtypeStruct((1, LANES), F32)],
        scratch_shapes=[pltpu.VMEM((1, d), F32)], compiler_params=_cparams("arbitrary"))(y, tgt)


SB_TQ = 256
SB_NK = SB_TQ // QBLK


def _sb_consts():
    r = _iota((QBLK, LANES), 0)
    c = _iota((QBLK, LANES), 1)
    ones = jnp.ones((QBLK, LANES), BF16)
    after = jnp.concatenate([jnp.where(r > c, 1.0, 0.0).astype(BF16), ones], axis=1)
    from_ = jnp.concatenate([jnp.where(r >= c, 1.0, 0.0).astype(BF16), ones], axis=1)
    return _iota((SB_TQ, LANES), 0), _iota((SB_TQ, LANES), 1), after, from_


def _sb_scores(qm, kb, strict):
    z = _dot(qm, kb, NT) * (HEAD_DIM ** -0.5)
    sp = jnp.log(1.0 + jnp.exp(-jnp.abs(z)))
    lnb = -(jnp.maximum(z, 0.0) + sp)
    lb = jnp.minimum(z, 0.0) - sp
    if strict is not None:
        lnb = jnp.where(strict, lnb, 0.0)
    return lnb, lb


def _sb_fwd(qkv, *, name):
    s = qkv.shape[0]
    nq = s // SB_TQ

    def body(q_ref, k_ref, v_ref, o_ref, *scr):
        acc, osc = scr[:4], scr[4:]
        qi = pl.program_id(0)
        row, lane, after, _ = _sb_consts()
        h0 = lane < HEAD_DIM
        q = q_ref[...]
        qms = []
        for p in range(2):
            qp = q[:, p * LANES:(p + 1) * LANES]
            qms += [jnp.where(h0, qp, jnp.zeros_like(qp)), jnp.where(h0, jnp.zeros_like(qp), qp)]

        def block(kj, mask):
            off = pl.multiple_of(kj * QBLK, QBLK)
            kbs = [k_ref[pl.ds(off, QBLK), p * LANES:(p + 1) * LANES] for p in range(2)]
            vbs = [v_ref[pl.ds(off, QBLK), p * LANES:(p + 1) * LANES] for p in range(2)]
            sc = [_sb_scores(qms[c], kbs[c // 2], mask) for c in range(4)]
            trs = [_xdot(sc[c][0], after) for c in range(4)]
            top = None
            for c in range(4):
                w = jnp.exp(sc[c][1] + trs[c][:, :QBLK] + acc[c][...])
                if mask is not None:
                    w = jnp.where(mask, w, 0.0)
                osc[c][...] += _xdot(w, vbs[c // 2])
                new = acc[c][...] + trs[c][:, QBLK:]
                acc[c][...] = new
                top = new if top is None else jnp.maximum(top, new)
            return jnp.max(top)

        for ref in scr:
            ref[...] = jnp.zeros_like(ref)
        top = None
        for j in reversed(range(SB_NK)):
            top = block(qi * SB_NK + j, (lane + j * QBLK) < row)
        lax.while_loop(lambda c: (c[0] >= 0) & (c[1] > -SB_EXIT), lambda c: (c[0] - 1, block(c[0], None)),
                       (qi * SB_NK - 1, top))
        for p in range(2):
            o_ref[:, p * LANES:(p + 1) * LANES] = jnp.where(h0, osc[2 * p][...], osc[2 * p + 1][...])

    return _pcall(
        body, name=name, grid=(nq,),
        in_specs=[pl.BlockSpec((SB_TQ, SB_W), lambda i: (i, 0)),
                  pl.BlockSpec((s, SB_W), lambda i: (0, 1)),
                  pl.BlockSpec((s, SB_W), lambda i: (0, 2))],
        out_specs=pl.BlockSpec((SB_TQ, SB_W), lambda i: (i, 0)),
        out_shape=jax.ShapeDtypeStruct((s, SB_W), F32),
        scratch_shapes=[pltpu.VMEM((SB_TQ, LANES), F32)] * 8,
        compiler_params=_cparams("arbitrary"))(qkv, qkv, qkv)


def _sb_bwd(qkv, o, dmix, *, name, side=None):
    s = qkv.shape[0]
    nq = s // SB_TQ
    scale = HEAD_DIM ** -0.5

    def body(q_ref, k_ref, v_ref, o_ref, do_ref, dq_ref, dk_ref, dv_ref, a0, a1, r0, r1, dqs, dks, dvs):
        acc, racc = (a0, a1), (r0, r1)
        i = pl.program_id(1)
        qi = nq - 1 - i
        row, lane, after, from_ = _sb_consts()
        klane = _iota((QBLK, LANES), 1)
        khms = (klane < HEAD_DIM, klane >= HEAD_DIM)

        @pl.when(i == 0)
        def _():
            dks[...] = jnp.zeros_like(dks)
            dvs[...] = jnp.zeros_like(dvs)

        q = q_ref[...]
        do = do_ref[...]
        dob = do.astype(BF16)
        dd = do * o_ref[...]
        dol = (do - dob.astype(F32)).astype(BF16)
        zero = jnp.zeros_like(q)
        hms = (lane < HEAD_DIM, lane >= HEAD_DIM)
        qms = [jnp.where(hm, q, zero) for hm in hms]
        doms = [jnp.where(hm, dob, zero) for hm in hms]
        dols = [jnp.where(hm, dol, zero) for hm in hms]
        dsums = [jnp.sum(jnp.where(hm, dd, 0.0), axis=1, keepdims=True) for hm in hms]

        def block(kj, mask):
            off = pl.multiple_of(kj * QBLK, QBLK)
            kb = k_ref[pl.ds(off, QBLK), :]
            vb = v_ref[pl.ds(off, QBLK), :]
            top, dq, dk, dv = None, None, None, None
            sc = [_sb_scores(qms[h], kb, mask) for h in range(2)]
            trs = [_xdot(sc[h][0], after) for h in range(2)]
            dws = [_dot(doms[h], vb, NT) + _dot(dols[h], vb, NT) for h in range(2)]
            for h in range(2):
                lb, tr = sc[h][1], trs[h]
                w = jnp.exp(lb + tr[:, :QBLK] + acc[h][...])
                if mask is not None:
                    w = jnp.where(mask, w, 0.0)
                g = w * dws[h]
                tg = _xdot(g, from_)
                before = dsums[h] - (tg[:, :QBLK] + racc[h][...])
                dz = g - jnp.exp(lb) * (g + before)
                if mask is not None:
                    dz = jnp.where(mask, dz, 0.0)
                dzb = (dz * scale).astype(BF16)
                dqh = _dot(dzb, jnp.where(khms[h], kb, jnp.zeros_like(kb)))
                dkh = _dot(dzb, qms[h], TN)
                dvh = _dot(w.astype(BF16), doms[h], TN)
                dq, dk, dv = (dqh, dkh, dvh) if h == 0 else (dq + dqh, dk + dkh, dv + dvh)
                new = acc[h][...] + tr[:, QBLK:]
                acc[h][...] = new
                racc[h][...] += tg[:, QBLK:]
                top = new if top is None else jnp.maximum(top, new)
            dqs[...] += dq
            dks[pl.ds(off, QBLK), :] += dk
            dvs[pl.ds(off, QBLK), :] += dv
            return jnp.max(top)

        for ref in (dqs, a0, a1, r0, r1):
            ref[...] = jnp.zeros_like(ref)
        top = None
        for j in reversed(range(SB_NK)):
            top = block(qi * SB_NK + j, (lane + j * QBLK) < row)
        lax.while_loop(lambda c: (c[0] >= 0) & (c[1] > -SB_EXIT), lambda c: (c[0] - 1, block(c[0], None)),
                       (qi * SB_NK - 1, top))
        dq_ref[...] = dqs[...]
        fin = pl.multiple_of(qi * SB_TQ, SB_TQ)
        dk_ref[...] = dks[pl.ds(fin, SB_TQ), :]
        dv_ref[...] = dvs[pl.ds(fin, SB_TQ), :]

    blk = lambda c0: pl.BlockSpec((SB_TQ, LANES), lambda p, i, c0=c0: (nq - 1 - i, c0 + p))
    return _call_with_exchange(
        body, side, lambda: (pl.program_id(0) == 0) & (pl.program_id(1) == 0),
        lambda: (pl.program_id(0) == 1) & (pl.program_id(1) == nq - 1), (qkv, qkv, qkv, o, dmix), name=name, grid=(2, nq),
        in_specs=[blk(0), pl.BlockSpec((s, LANES), lambda p, i: (0, 2 + p)), pl.BlockSpec((s, LANES), lambda p, i: (0, 4 + p)),
                  blk(0), blk(0)],
        out_specs=[blk(0), blk(0), blk(0)],
        out_shape=[jax.ShapeDtypeStruct((s, SB_W), F32)] * 3,
        scratch_shapes=[pltpu.VMEM((SB_TQ, LANES), F32)] * 5 + [pltpu.VMEM((s, LANES), F32), pltpu.VMEM((s, LANES), F32)],
        compiler_params=_cparams("arbitrary", "arbitrary"))


def _seg_consts():
    r = _iota((LANES, LANES), 0)
    c = _iota((LANES, LANES), 1)
    return jnp.where((r >> 6) == (c >> 6), 1.0, 0.0).astype(BF16)


def _rot_half(x, lane):
    half = HEAD_DIM // 2
    return jnp.where((lane & (HEAD_DIM - 1)) < half, pltpu.roll(x, LANES - half, 1), pltpu.roll(x, half, 1))


def _rope_tables(s):
    half = HEAD_DIM // 2
    inv_freq = ROPE_THETA ** (-jnp.arange(half, dtype=F32) * 2.0 / HEAD_DIM)
    ang = jnp.arange(s, dtype=F32)[:, None] * inv_freq[None, :]
    cos, sin = jnp.cos(ang), jnp.sin(ang)
    return jnp.tile(jnp.concatenate([cos, cos], axis=1), (1, 2)), jnp.tile(jnp.concatenate([-sin, sin], axis=1), (1, 2))


def _dil_prep(proj, gq, gk, cos, sin, *, name):
    s = proj.shape[0]
    t = _rows(s, TE)
    c0 = 3 * SB_W // LANES

    def body(q_ref, k_ref, gq_ref, gk_ref, cos_ref, sin_ref, qo_ref, ko_ref):
        seg = _seg_consts()
        lane = _iota((t, LANES), 1)
        cs, sn = cos_ref[...], sin_ref[...]
        for x_ref, g_ref, o_ref, mul in ((q_ref, gq_ref, qo_ref, HEAD_DIM ** -0.5), (k_ref, gk_ref, ko_ref, 1.0)):
            for j in range(2):
                xv = x_ref[:, j * LANES:(j + 1) * LANES]
                ms = _xdot(xv * xv, seg, 3) * (1.0 / HEAD_DIM)
                xn = xv * lax.rsqrt(ms + EPS) * g_ref[...]
                o_ref[:, j * LANES:(j + 1) * LANES] = (xn * cs + _rot_half(xn, lane) * sn) * mul

    return _pcall(
        body, name=name, grid=(s // t,),
        in_specs=[pl.BlockSpec((t, DIL_W), lambda i: (i, c0 // 2)), pl.BlockSpec((t, DIL_W), lambda i: (i, c0 // 2 + 1)),
                  _vec_spec(LANES), _vec_spec(LANES), _row_spec(t, LANES), _row_spec(t, LANES)],
        out_specs=[_row_spec(t, DIL_W), _row_spec(t, DIL_W)],
        out_shape=[jax.ShapeDtypeStruct((s, DIL_W), F32)] * 2, compiler_params=_cparams("parallel"))(proj, proj, gq, gk, cos, sin)


def _dil_prep_bwd(proj, gq, gk, cos, sin, dqs, dks, dvs, *, name):
    s = proj.shape[0]
    t = _rows(s, TE)
    c0 = 3 * SB_W // LANES

    def body(q_ref, k_ref, gq_ref, gk_ref, cos_ref, sin_ref, a0, a1, a2, b0, b1, b2, c0_ref, c1_ref, c2_ref,
             dq_ref, dk_ref, dv_ref, dgq_ref, dgk_ref):
        @pl.when(pl.program_id(0) == 0)
        def _():
            dgq_ref[...] = jnp.zeros_like(dgq_ref)
            dgk_ref[...] = jnp.zeros_like(dgk_ref)

        dv_ref[...] = c0_ref[...] + c1_ref[...] + c2_ref[...]
        seg = _seg_consts()
        lane = _iota((t, LANES), 1)
        cs, sn = cos_ref[...], sin_ref[...]
        for x_ref, g_ref, parts, o_ref, dg_ref, mul in ((q_ref, gq_ref, (a0, a1, a2), dq_ref, dgq_ref, HEAD_DIM ** -0.5),
                                                          (k_ref, gk_ref, (b0, b1, b2), dk_ref, dgk_ref, 1.0)):
            for j in range(2):
                sl = slice(j * LANES, (j + 1) * LANES)
                dout = (parts[0][:, sl] + parts[1][:, sl] + parts[2][:, sl]) * mul
                dxn = dout * cs + _rot_half(dout * sn, lane)
                xv = x_ref[:, sl]
                r = lax.rsqrt(_xdot(xv * xv, seg, 3) * (1.0 / HEAD_DIM) + EPS)
                xh = xv * r
                dg_ref[...] += jnp.sum(dxn * xh, axis=0, keepdims=True)
                dxh = dxn * g_ref[...]
                o_ref[:, sl] = r * (dxh - xh * (_xdot(dxh * xh, seg, 3) * (1.0 / HEAD_DIM)))

    rs = _row_spec(t, DIL_W)
    return _pcall(
        body, name=name, grid=(s // t,),
        in_specs=[pl.BlockSpec((t, DIL_W), lambda i: (i, c0 // 2)), pl.BlockSpec((t, DIL_W), lambda i: (i, c0 // 2 + 1)),
                  _vec_spec(LANES), _vec_spec(LANES), _row_spec(t, LANES), _row_spec(t, LANES)] + [rs] * 9,
        out_specs=[rs, rs, rs, _vec_spec(LANES), _vec_spec(LANES)],
        out_shape=[jax.ShapeDtypeStruct((s, DIL_W), F32)] * 3 + [jax.ShapeDtypeStruct((1, LANES), F32)] * 2,
        compiler_params=_cparams("arbitrary"))(proj, proj, gq, gk, cos, sin, *dqs, *dks, *dvs)


def _dil_masks(n):
    row = _iota((QBLK, 2 * LANES), 0)
    col = _iota((QBLK, 2 * LANES), 1)
    return ((col < LANES) & (col >= row) & (n > 0)) | ((col >= LANES) & (col - LANES <= row))


DIL_V0 = (3 * SB_W + 2 * DIL_W) // LANES
DIL_DO0 = SB_W // LANES


def _dil_tiles(s, r):
    span = QBLK * r
    nsub = max(1, DIL_TILE // span)
    while s % (nsub * span):
        nsub -= 1
    return span, nsub


def _dil_rows(j, rho, span, r):
    return pl.ds(j * span + rho, QBLK, stride=r) if r > 1 else pl.ds(j * span, QBLK)


def _dil_fwd(q, k, proj, r, *, name):
    s = q.shape[0]
    span, nsub = _dil_tiles(s, r)
    tr = nsub * span

    def body(q_ref, kc_ref, kp_ref, vc_ref, vp_ref, num_ref, den_ref, mx_ref):
        n = pl.program_id(1)
        lane = _iota((QBLK, LANES), 1)
        h0 = lane < HEAD_DIM
        ones = jnp.ones((2 * QBLK, LANES), BF16)
        for j in range(nsub):
            valid = _dil_masks(n if j == 0 else 1)
            for rho in range(r):
                rows = _dil_rows(j, rho, span, r)
                before = _dil_rows(max(j - 1, 0), rho, span, r)
                k_prev, v_prev = (kp_ref, vp_ref) if j == 0 else (kc_ref, vc_ref)
                qv = q_ref[rows, :].astype(BF16)
                kk = jnp.concatenate([k_prev[before, :], kc_ref[rows, :]], axis=0).astype(BF16)
                vv = jnp.concatenate([jnp.concatenate([v_prev[before, :], vc_ref[rows, :]], axis=0).astype(BF16), ones], axis=1)
                res = []
                for h in range(2):
                    qm = jnp.where(h0 if h == 0 else ~h0, qv, jnp.zeros_like(qv))
                    sc = jnp.where(valid, _dot(qm, kk, NT), NEG_BIG)
                    mx = jnp.max(sc, axis=1, keepdims=True)
                    nd = _dot(jnp.exp(sc - mx).astype(BF16), vv)
                    res.append((nd[:, :LANES], nd[:, LANES:], mx))
                num_ref[rows, :] = jnp.where(h0, res[0][0], res[1][0])
                den_ref[rows, :] = jnp.where(h0, res[0][1], res[1][1])
                mx_ref[rows, :] = jnp.where(h0, res[0][2], res[1][2])

    cur = lambda c0: pl.BlockSpec((tr, LANES), lambda p, n, c0=c0: (n, c0 + p))
    prev = lambda c0: pl.BlockSpec((span, LANES), lambda p, n, c0=c0: (jnp.maximum(n * nsub - 1, 0), c0 + p))
    return _pcall(
        body, name=name, grid=(2, s // tr), in_specs=[cur(0), cur(0), prev(0), cur(DIL_V0), prev(DIL_V0)],
        out_specs=[cur(0), cur(0), cur(0)], out_shape=[jax.ShapeDtypeStruct((s, DIL_W), F32)] * 3,
        compiler_params=_cparams("parallel", "arbitrary"))(q, k, k, proj, proj)


def _dil_bwd(q, k, proj, dmix, mall, zall, delta, r, *, name):
    s = q.shape[0]
    span, nsub = _dil_tiles(s, r)
    tr = nsub * span
    nbig = s // tr

    def body(q_ref, kc_ref, kp_ref, vc_ref, vp_ref, do_ref, m_ref, z_ref, dl_ref, dq_ref, dk_ref, dv_ref, pk, pv):
        n = pl.program_id(1)
        lane = _iota((QBLK, LANES), 1)
        h0 = lane < HEAD_DIM

        @pl.when(n == 0)
        def _():
            pk[...] = jnp.zeros_like(pk)
            pv[...] = jnp.zeros_like(pv)

        @pl.when(n < nbig)
        def _():
            dk_ref[...] = pk[...]
            dv_ref[...] = pv[...]
            for j in range(nsub):
                valid = _dil_masks(n if j == 0 else 1)
                for rho in range(r):
                    rows = _dil_rows(j, rho, span, r)
                    before = _dil_rows(max(j - 1, 0), rho, span, r)
                    k_prev, v_prev = (kp_ref, vp_ref) if j == 0 else (kc_ref, vc_ref)
                    qv = q_ref[rows, :].astype(BF16)
                    dob = do_ref[rows, :].astype(BF16)
                    zero = jnp.zeros_like(qv)
                    kk = jnp.concatenate([k_prev[before, :], kc_ref[rows, :]], axis=0).astype(BF16)
                    vv = jnp.concatenate([v_prev[before, :], vc_ref[rows, :]], axis=0).astype(BF16)
                    mall_v, z_v, dl_v = m_ref[rows, :], z_ref[rows, :], dl_ref[rows, :]
                    dq, dk, dv = None, None, None
                    for h in range(2):
                        hm = h0 if h == 0 else ~h0
                        qm = jnp.where(hm, qv, zero)
                        dom = jnp.where(hm, dob, zero)
                        c = h * HEAD_DIM
                        sc = jnp.where(valid, _dot(qm, kk, NT), NEG_BIG)
                        pr = jnp.exp(sc - mall_v[:, c:c + 1]) * (1.0 / z_v[:, c:c + 1])
                        ds = (pr * (_dot(dom, vv, NT) - dl_v[:, c:c + 1])).astype(BF16)
                        parts = (_dot(ds, jnp.where(jnp.concatenate([hm, hm], axis=0), kk, jnp.zeros_like(kk))),
                                 _dot(ds, qm, TN), _dot(pr.astype(BF16), dom, TN))
                        dq, dk, dv = parts if h == 0 else (dq + parts[0], dk + parts[1], dv + parts[2])
                    dq_ref[rows, :] = dq
                    pk[rows, :] = dk[QBLK:]
                    pv[rows, :] = dv[QBLK:]
                    if j == 0:
                        last_span = _dil_rows(nsub - 1, rho, span, r)
                        dk_ref[last_span, :] += dk[:QBLK]
                        dv_ref[last_span, :] += dv[:QBLK]
                    else:
                        pk[before, :] += dk[:QBLK]
                        pv[before, :] += dv[:QBLK]

        @pl.when(n == nbig)
        def _():
            dk_ref[...] = pk[...]
            dv_ref[...] = pv[...]

    last = nbig - 1
    cur = lambda c0: pl.BlockSpec((tr, LANES), lambda p, n, c0=c0: (jnp.minimum(n, last), c0 + p))
    prev = lambda c0: pl.BlockSpec((span, LANES), lambda p, n, c0=c0: (jnp.maximum(jnp.minimum(n, last) * nsub - 1, 0), c0 + p))
    late = pl.BlockSpec((tr, LANES), lambda p, n: (jnp.maximum(n - 1, 0), p))
    return _pcall(
        body, name=name, grid=(2, nbig + 1),
        in_specs=[cur(0), cur(0), prev(0), cur(DIL_V0), prev(DIL_V0), cur(DIL_DO0), cur(0), cur(0), cur(0)],
        out_specs=[cur(0), late, late], out_shape=[jax.ShapeDtypeStruct((s, DIL_W), F32)] * 3,
        scratch_shapes=[pltpu.VMEM((tr, LANES), F32)] * 2,
        compiler_params=_cparams("parallel", "arbitrary"))(q, k, k, proj, proj, dmix, mall, zall, delta)


HG_SHIFT = HG_BLK.bit_length() - 1
HG_Q0, HG_F0, HG_I0 = (3 * SB_W + 3 * DIL_W) // HG_D, (3 * SB_W + 3 * DIL_W + HG_W) // HG_D, (3 * SB_W + 3 * DIL_W + 2 * HG_W) // HG_D


def _hg_scan(x):
    t = x.shape[0]
    half = HG_BLK // 2
    rb = _iota((t, LANES), 0) & (HG_BLK - 1)
    rh = rb & (half - 1)
    p = x
    for s in (1, 2, 4):
        p = p + jnp.where(rh >= s, pltpu.roll(p, s, 0), 0.0)
    h = jnp.where(rh == half - 1, p, 0.0)
    for s in (1, 2, 4):
        h = h + jnp.where(rh + s < half, pltpu.roll(h, t - s, 0), 0.0)
    first = rb < half
    pref = jnp.where(first, p, p + pltpu.roll(h, half, 0))
    total = h + jnp.where(first, pltpu.roll(h, t - half, 0), pltpu.roll(h, half, 0))
    return p, h, pref, total, first


def _hg_same(t):
    i = jnp.arange(t) >> HG_SHIFT
    return (i[:, None] == i[None, :]).astype(F32)


def _hg_own(t):
    i = jnp.arange(t) >> HG_SHIFT
    j = jnp.arange(t // HG_BLK * HG_D) // HG_D
    return (i[:, None] == j[None, :]).astype(BF16)


def _hg_diag(x, nb):
    return jnp.concatenate([x[b * HG_BLK:(b + 1) * HG_BLK, b * HG_D:(b + 1) * HG_D] for b in range(nb)], axis=0)


def _hg_inputs(qh, z, v, la, lc, t):
    lsg = jnp.minimum(z, 0.0) - jnp.log(1.0 + jnp.exp(-jnp.abs(z)))
    b = lc + lsg
    lf = jnp.maximum(la, b) + jnp.log(1.0 + jnp.exp(-jnp.abs(la - b)))
    f = jnp.exp(lf)
    sq = _sigmoid(qh)
    p, h, g, gl, first = _hg_scan(lf)
    k = 1.0 - f
    qs = qh * sq
    eq = jnp.where(first, 0.0, jnp.exp(jnp.minimum(p, 0.0)))
    ek = jnp.where(first, jnp.exp(jnp.minimum(h - p, 0.0)), 0.0)
    return dict(lf=lf, b=b, f=f, k=k, sq=sq, qs=qs, g=g, eg=jnp.exp(g), egl=jnp.exp(gl - g), dec=jnp.exp(gl),
                eq=eq, ek=ek, qx=(qs * eq).astype(BF16), kx=(k * ek).astype(BF16))


def _hgrn_fwd(proj, la, lc, *, name):
    s = proj.shape[0]
    t = _rows(s, HG_TILE)
    nt, nb = s // t, t // HG_BLK

    def body(q_ref, f_ref, i_ref, la_ref, lc_ref, same_ref, own_ref, o_ref, st_ref, state):
        @pl.when(pl.program_id(1) == 0)
        def _():
            state[...] = jnp.zeros_like(state)

        v = i_ref[...]
        a = _hg_inputs(q_ref[...], f_ref[...], v, la_ref[...], lc_ref[...], t)
        qs, k = a["qs"], a["k"]
        vb = v.astype(BF16)
        rb = _iota((t, LANES), 0) & (HG_BLK // 2 - 1)
        o = jnp.sum(qs * k, axis=1, keepdims=True) * v
        e = None
        for d in range(1, HG_BLK // 2):
            m = rb >= d
            fr = a["f"] if d == 1 else pltpu.roll(a["f"], d - 1, 0)
            e = fr if e is None else e * fr
            cd = jnp.sum(qs * pltpu.roll(k, d, 0) * e, axis=1, keepdims=True)
            o = o + jnp.where(m, cd, 0.0) * pltpu.roll(v, d, 0)
        cross = _dot(a["qx"], a["kx"], NT) * same_ref[...]
        o = o + _dot(cross.astype(BF16), vb)
        qt = (qs * a["eg"]).astype(BF16)
        kt = (k * a["egl"]).astype(BF16)
        upd = _dot(vb, jnp.tile(kt, (1, nb)) * own_ref[...], TN)
        st = state[...]
        for blk in range(nb):
            st_ref[blk * HG_D:(blk + 1) * HG_D, :] = st.astype(BF16)
            st = a["dec"][blk * HG_BLK:blk * HG_BLK + 1] * st + upd[:, blk * HG_D:(blk + 1) * HG_D]
        state[...] = st
        o_ref[...] = o + _hg_diag(_dot(qt, st_ref[...], NT), nb)

    col = lambda c0: pl.BlockSpec((t, HG_D), lambda hd, i, c0=c0: (i, c0 + hd))
    vec = pl.BlockSpec((1, HG_D), lambda hd, i: (0, hd))
    return _pcall(
        body, name=name, grid=(4, nt),
        in_specs=[col(HG_Q0), col(HG_F0), col(HG_I0), vec, vec, pl.BlockSpec((t, t), lambda hd, i: (0, 0)),
                  pl.BlockSpec((t, nb * HG_D), lambda hd, i: (0, 0))],
        out_specs=[col(0), pl.BlockSpec((None, nb * HG_D, HG_D), lambda hd, i: (hd, i, 0))],
        out_shape=[jax.ShapeDtypeStruct((s, HG_W), F32), jax.ShapeDtypeStruct((4, s // HG_BLK * HG_D, HG_D), BF16)],
        scratch_shapes=[pltpu.VMEM((HG_D, HG_D), F32)],
        compiler_params=_cparams("arbitrary", "arbitrary"))(proj, proj, proj, la, lc, _hg_same(t), _hg_own(t))


def _hgrn_bwd(proj, la, lc, st, doh, *, name, side=None):
    s = proj.shape[0]
    t = _rows(s, HG_TILE)
    nt, nb = s // t, t // HG_BLK

    def body(q_ref, f_ref, i_ref, la_ref, lc_ref, st_ref, do_ref, same_ref, own_ref, dq_ref, df_ref, di_ref, dla_ref, dlc_ref,
             dstate, dsb):
        @pl.when(pl.program_id(1) == 0)
        def _():
            dstate[...] = jnp.zeros_like(dstate)
            dla_ref[...] = jnp.zeros_like(dla_ref)
            dlc_ref[...] = jnp.zeros_like(dlc_ref)

        qh, z, v, do = q_ref[...], f_ref[...], i_ref[...], do_ref[...]
        la = la_ref[...]
        a = _hg_inputs(qh, z, v, la, lc_ref[...], t)
        qs, k, g = a["qs"], a["k"], a["g"]
        vb = v.astype(BF16)
        dob = do.astype(BF16)
        rb = _iota((t, LANES), 0) & (HG_BLK // 2 - 1)
        dc0 = jnp.sum(do * v, axis=1, keepdims=True)
        dq = dc0 * k
        dk = dc0 * qs
        dv = jnp.sum(qs * k, axis=1, keepdims=True) * do
        e = None
        for d in range(1, HG_BLK // 2):
            m = rb >= d
            fr = a["f"] if d == 1 else pltpu.roll(a["f"], d - 1, 0)
            e = fr if e is None else e * fr
            ks = pltpu.roll(k, d, 0)
            qe = qs * e
            cd = jnp.where(m, jnp.sum(qe * ks, axis=1, keepdims=True), 0.0)
            dcd = jnp.where(m, jnp.sum(do * pltpu.roll(v, d, 0), axis=1, keepdims=True), 0.0)
            dq = dq + dcd * ks * e
            dk = dk + pltpu.roll(dcd * qe, t - d, 0)
            dv = dv + pltpu.roll(cd * do, t - d, 0)
        same = same_ref[...]
        cross = (_dot(a["qx"], a["kx"], NT) * same).astype(BF16)
        dcross = (_dot(dob, vb, NT) * same).astype(BF16)
        dq = dq + _dot(dcross, a["kx"]) * a["eq"]
        dk = dk + _dot(dcross, a["qx"], TN) * a["ek"]
        dv = dv + _dot(cross, dob, TN)
        qt = (qs * a["eg"]).astype(BF16)
        kt = (k * a["egl"]).astype(BF16)
        own = own_ref[...]
        upd = _dot(dob, jnp.tile(qt, (1, nb)) * own, TN)
        ds = dstate[...]
        dgs = [None] * nb
        for blk in reversed(range(nb)):
            rows = slice(blk * HG_D, (blk + 1) * HG_D)
            dec = a["dec"][blk * HG_BLK:blk * HG_BLK + 1]
            dsb[rows, :] = ds.astype(BF16)
            dgs[blk] = jnp.broadcast_to(jnp.sum(ds * st_ref[rows, :].astype(F32), axis=0, keepdims=True) * dec, (HG_BLK, HG_D))
            ds = dec * ds + upd[:, rows]
        dstate[...] = ds
        dki = _dot(jnp.tile(vb, (1, nb)) * own, dsb[...]) * a["egl"]
        dq = dq + _dot(jnp.tile(dob, (1, nb)) * own, st_ref[...]) * a["eg"]
        dk = dk + dki
        dv = dv + _hg_diag(_dot(kt, dsb[...], NT), nb)
        x = qs * dq - k * dk
        _, _, xpre, xtot, _ = _hg_scan(x)
        _, _, _, ktot, _ = _hg_scan(k * dki)
        dlf = (xtot - xpre + x) + ktot + jnp.concatenate(dgs, axis=0) - a["f"] * dk
        wb = jnp.exp(a["b"] - a["lf"])
        wa = jnp.exp(la - a["lf"])
        sq = a["sq"]
        dq_ref[...] = dq * (sq * (1.0 + qh * (1.0 - sq)))
        df_ref[...] = dlf * wb * (1.0 - _sigmoid(z))
        di_ref[...] = dv
        dla_ref[...] += jnp.sum(dlf * wa, axis=0, keepdims=True)
        dlc_ref[...] += jnp.sum(dlf * wb, axis=0, keepdims=True)

    col = lambda c0: pl.BlockSpec((t, HG_D), lambda hd, i, c0=c0: (nt - 1 - i, c0 + hd))
    vec = pl.BlockSpec((1, HG_D), lambda hd, i: (0, hd))
    return _call_with_exchange(
        body, side, lambda: (pl.program_id(0) == 0) & (pl.program_id(1) == 0),
        lambda: (pl.program_id(0) == 3) & (pl.program_id(1) == nt - 1),
        (proj, proj, proj, la, lc, st, doh, _hg_same(t), _hg_own(t)), name=name, grid=(4, nt),
        in_specs=[col(HG_Q0), col(HG_F0), col(HG_I0), vec, vec,
                  pl.BlockSpec((None, nb * HG_D, HG_D), lambda hd, i: (hd, nt - 1 - i, 0)), col(0),
                  pl.BlockSpec((t, t), lambda hd, i: (0, 0)), pl.BlockSpec((t, nb * HG_D), lambda hd, i: (0, 0))],
        out_specs=[col(0), col(0), col(0), vec, vec],
        out_shape=[jax.ShapeDtypeStruct((s, HG_W), F32)] * 3 + [jax.ShapeDtypeStruct((1, HG_W), F32)] * 2,
        scratch_shapes=[pltpu.VMEM((HG_D, HG_D), F32), pltpu.VMEM((nb * HG_D, HG_D), BF16)],
        compiler_params=_cparams("arbitrary", "arbitrary"))


GH0 = (IN_W - HG_W) // HG_W


def _mix_out(o_a, nums, dens, mxs, oh, proj, hg, *, name):
    s = o_a.shape[0]
    t = _rows(s, TE)

    def body(oa_ref, n0, n1, n2, d0, d1, d2, m0, m1, m2, oh_ref, gh_ref, hg_ref, y_ref, od_ref, mall_ref, z_ref):
        y_ref[:, :SB_W] = oa_ref[...].astype(BF16)
        m = jnp.maximum(jnp.maximum(m0[...], m1[...]), m2[...])
        num = jnp.zeros((t, DIL_W), F32)
        z = jnp.zeros((t, DIL_W), F32)
        for n_ref, d_ref, m_ref in ((n0, d0, m0), (n1, d1, m1), (n2, d2, m2)):
            sc = jnp.exp(m_ref[...] - m)
            num = num + n_ref[...] * sc
            z = z + d_ref[...] * sc
        od = num / z
        od_ref[...] = od
        mall_ref[...] = m
        z_ref[...] = z
        y_ref[:, SB_W:SB_W + DIL_W] = od.astype(BF16)
        for h in range(4):
            sl = slice(h * HG_D, (h + 1) * HG_D)
            ov = oh_ref[:, sl]
            g = gh_ref[:, sl]
            r = lax.rsqrt(jnp.mean(ov * ov, axis=1, keepdims=True) + EPS)
            y_ref[:, SB_W + DIL_W + h * HG_D:SB_W + DIL_W + (h + 1) * HG_D] = (ov * r * hg_ref[...] * (g * _sigmoid(g))).astype(BF16)

    rd = _row_spec(t, DIL_W)
    return _pcall(
        body, name=name, grid=(s // t,),
        in_specs=[rd] * 10 + [_row_spec(t, HG_W), _row_spec(t, HG_W, GH0), _vec_spec(HG_D)],
        out_specs=[_row_spec(t, MIX_W), rd, rd, rd],
        out_shape=[jax.ShapeDtypeStruct((s, MIX_W), BF16)] + [jax.ShapeDtypeStruct((s, DIL_W), F32)] * 3,
        compiler_params=_cparams("parallel"))(o_a, *nums, *dens, *mxs, oh, proj, hg)


def _mix_out_bwd(dmix, oh, proj, hg, od, *, name):
    s = oh.shape[0]
    t = _rows(s, TE)

    def body(dm_ref, oh_ref, gh_ref, hg_ref, od_ref, doh_ref, dgh_ref, dl_ref, dhg_ref):
        @pl.when(pl.program_id(0) == 0)
        def _():
            dhg_ref[...] = jnp.zeros_like(dhg_ref)

        seg = _seg_consts()
        for j in range(2):
            sl = slice(j * LANES, (j + 1) * LANES)
            dl_ref[:, sl] = _xdot(dm_ref[:, SB_W + j * LANES:SB_W + (j + 1) * LANES] * od_ref[:, sl], seg, 3)
        hgv = hg_ref[...]
        for h in range(4):
            sl = slice(h * HG_D, (h + 1) * HG_D)
            dy = dm_ref[:, SB_W + DIL_W + h * HG_D:SB_W + DIL_W + (h + 1) * HG_D]
            ov = oh_ref[:, sl]
            g = gh_ref[:, sl]
            sg = _sigmoid(g)
            silu = g * sg
            r = lax.rsqrt(jnp.mean(ov * ov, axis=1, keepdims=True) + EPS)
            nrm = ov * r
            dhg_ref[...] += jnp.sum(dy * nrm * silu, axis=0, keepdims=True)
            dgh_ref[:, sl] = dy * nrm * hgv * (sg * (1.0 + g * (1.0 - sg)))
            dn = dy * hgv * silu
            doh_ref[:, sl] = r * (dn - nrm * jnp.mean(dn * nrm, axis=1, keepdims=True))

    rh = _row_spec(t, HG_W)
    return _pcall(
        body, name=name, grid=(s // t,),
        in_specs=[_row_spec(t, MIX_W), rh, _row_spec(t, HG_W, GH0), _vec_spec(HG_D), _row_spec(t, DIL_W)],
        out_specs=[rh, rh, _row_spec(t, DIL_W), _vec_spec(HG_D)],
        out_shape=[jax.ShapeDtypeStruct((s, HG_W), F32)] * 2 + [jax.ShapeDtypeStruct((s, DIL_W), F32), jax.ShapeDtypeStruct((1, HG_D), F32)],
        compiler_params=_cparams("arbitrary"))(dmix, oh, proj, hg, od)


def _lb_terms(l):
    l0, l1 = l[0:1], l[1:2]
    m = jnp.maximum(l0, l1)
    e0, e1 = jnp.exp(l0 - m), jnp.exp(l1 - m)
    s0, s1 = e0 / (e0 + e1), e1 / (e0 + e1)
    args = (s0 - s0, (s0 + s1) - s0)
    lbs = tuple(jnp.minimum(jnp.maximum(a, 0.0), 1.0 - EPS) for a in args)
    return s0, s1, args, lbs


def _lb_prep(logits, *, name):
    def body(l_ref, lb_ref, la_ref, lc_ref):
        _, _, _, lbs = _lb_terms(l_ref[...])
        lb = jnp.concatenate(lbs, axis=0)
        lb_ref[...] = lb
        la_ref[...] = jnp.log(jnp.maximum(lb, LB_FLOOR))
        lc_ref[...] = jnp.log1p(-lb)

    return _pcall(body, name=name, out_shape=[jax.ShapeDtypeStruct(logits.shape, F32)] * 3)(logits)


def _lb_bwd(logits, dla, dlc, *, name):
    def half(hi, eq):
        return jnp.where(hi, 1.0, jnp.where(eq, 0.5, 0.0))

    def body(l_ref, dla_ref, dlc_ref, o_ref):
        s0, s1, args, lbs = _lb_terms(l_ref[...])
        da = []
        for i in range(2):
            a, lb = args[i], lbs[i]
            dlb = dla_ref[i:i + 1] * half(lb > LB_FLOOR, lb == LB_FLOOR) / jnp.maximum(lb, LB_FLOOR) - dlc_ref[i:i + 1] / (1.0 - lb)
            t = jnp.maximum(a, 0.0)
            da.append(dlb * half(a > 0.0, a == 0.0) * half(t < 1.0 - EPS, t == 1.0 - EPS))
        ds0 = (da[0] + da[1]) - (da[0] + da[1])
        ds1 = da[1]
        dot = s0 * ds0 + s1 * ds1
        o_ref[...] = jnp.concatenate([s0 * (ds0 - dot), s1 * (ds1 - dot)], axis=0)

    return _pcall(body, name=name, out_shape=jax.ShapeDtypeStruct(logits.shape, F32))(logits, dla, dlc)


def _mod_fwd(c8, w, b, *, name):
    _, d, n = w.shape
    tn = _tile(n, 768)

    def body(c_ref, w_ref, b_ref, o_ref):
        cv = c_ref[...]
        o_ref[...] = _dot((cv * _sigmoid(cv)).astype(BF16), w_ref[...].astype(BF16)) + b_ref[...]

    return _pcall(
        body, name=name, grid=(2, n // tn),
        in_specs=[pl.BlockSpec((8, d), lambda l, j: (0, 0)), pl.BlockSpec((None, d, tn), lambda l, j: (l, 0, j)),
                  pl.BlockSpec((None, 1, tn), lambda l, j: (l, 0, j))],
        out_specs=pl.BlockSpec((None, 8, tn), lambda l, j: (l, 0, j)),
        out_shape=jax.ShapeDtypeStruct((2, 8, n), F32), compiler_params=_cparams("parallel", "parallel"))(c8, w, b)


def _mod_bwd(ct, dm, *, name):
    d = ct.shape[0]
    n = dm.shape[2]
    tn = _tile(n, 768)

    def body(c_ref, dm_ref, o_ref):
        cv = c_ref[...]
        sc = cv * _sigmoid(cv)
        dv = dm_ref[...]
        acc = sc[:, 0:1] * dv[0:1, :]
        for b in range(1, 8):
            acc = acc + sc[:, b:b + 1] * dv[b:b + 1, :]
        o_ref[...] = acc

    return _pcall(
        body, name=name, grid=(2, n // tn),
        in_specs=[pl.BlockSpec((d, 8), lambda l, j: (0, 0)), pl.BlockSpec((None, 8, tn), lambda l, j: (l, 0, j))],
        out_specs=pl.BlockSpec((None, d, tn), lambda l, j: (l, 0, j)),
        out_shape=jax.ShapeDtypeStruct((2, d, n), F32), compiler_params=_cparams("parallel", "parallel"))(ct, dm)


_PEERS = {
    "chips": ((1, 0, 0), (0, 1, 0), (1, 1, 0)),
    "all": tuple((a, b, c) for a in (0, 1) for b in (0, 1) for c in (0, 1) if a + b + c),
    "sib": ((0, 0, 1),),
}
_SLOTS = {"chips": 4, "all": 8, "sib": 2}


def _slot(kind, x, y, c):
    return {"chips": 2 * x + y, "all": 4 * x + 2 * y + c, "sib": c}[kind]


class _Xchg:
    def __init__(self, arrs, kind, scatter):
        self.arrs, self.kind, self.scatter = list(arrs), kind, scatter
        self.n = len(self.arrs)
        self.peers = _PEERS[kind]
        self.chunks = [self._pieces(a) for a in self.arrs]
        npeer = len(self.peers)
        self.base = [sum(len(c) for c in self.chunks[:a]) * npeer for a in range(self.n)]
        total = sum(len(c) for c in self.chunks) * npeer
        self.specs = [pl.BlockSpec(memory_space=pl.ANY)] * self.n
        self.out_shape = [jax.ShapeDtypeStruct(a.shape if scatter else (_SLOTS[kind],) + a.shape, a.dtype) for a in self.arrs]
        self.scratch = [pltpu.SemaphoreType.DMA((total,)), pltpu.SemaphoreType.DMA((total,)), pltpu.SemaphoreType.DMA((self.n,))]

    def _pieces(self, a):
        shape = a.shape[1:] if self.scatter else a.shape
        if len(shape) == 2:
            for k in (XCHG_CHUNKS, XCHG_CHUNKS // 2, XCHG_CHUNKS // 4):
                if k > 1 and shape[0] % (16 * k) == 0 and shape[0] * shape[1] * a.dtype.itemsize >= k * XCHG_MIN_BYTES:
                    return [(i * (shape[0] // k), shape[0] // k) for i in range(k)]
        return [None]

    def copies(self, ins, outs, send, recv, loc):
        kind, scatter = self.kind, self.scatter
        x, y, c = lax.axis_index("x"), lax.axis_index("y"), lax.axis_index("c")
        me = _slot(kind, x, y, c)
        out = []
        for a in range(self.n):
            out.append(pltpu.make_async_copy(ins[a].at[me] if scatter else ins[a], outs[a].at[me], loc.at[a]))
            for j, (dx, dy, dc) in enumerate(self.peers):
                px, py, pc = (1 - x if dx else x), (1 - y if dy else y), (1 - c if dc else c)
                src = ins[a].at[_slot(kind, px, py, pc)] if scatter else ins[a]
                for i, piece in enumerate(self.chunks[a]):
                    rows = slice(None) if piece is None else pl.ds(piece[0], piece[1])
                    sem = self.base[a] + j * len(self.chunks[a]) + i
                    out.append(pltpu.make_async_remote_copy(
                        src_ref=src if piece is None else src.at[rows], dst_ref=outs[a].at[me] if piece is None else outs[a].at[me, rows],
                        send_sem=send.at[sem], recv_sem=recv.at[sem], device_id=(px, py, pc), device_id_type=MESH_ID))
        return out


def _exchange(arrs, kind, scatter, *, name):
    xc = _Xchg(arrs, kind, scatter)

    def body(*refs):
        copies = xc.copies(refs[:xc.n], refs[xc.n:2 * xc.n], *refs[2 * xc.n:])
        for cp in copies:
            cp.start()
        for cp in copies:
            cp.wait()

    return _pcall(body, name=name, in_specs=xc.specs, out_specs=xc.specs, out_shape=xc.out_shape, scratch_shapes=xc.scratch)(*arrs)


def _with_exchange(body, n_in, n_out, n_scr, xc, first, last):
    def wrapped(*refs):
        ins, side_in = refs[:n_in], refs[n_in:n_in + xc.n]
        o0 = n_in + xc.n
        outs, side_out = refs[o0:o0 + n_out], refs[o0 + n_out:o0 + n_out + xc.n]
        s0 = o0 + n_out + xc.n
        scr, sems = refs[s0:s0 + n_scr], refs[s0 + n_scr:]

        @pl.when(first())
        def _():
            for cp in xc.copies(side_in, side_out, *sems):
                cp.start()

        body(*ins, *outs, *scr)

        @pl.when(last())
        def _():
            for cp in xc.copies(side_in, side_out, *sems):
                cp.wait()

    return wrapped


def _call_with_exchange(body, xc, first, last, args, *, in_specs, out_specs, out_shape, scratch_shapes, **kw):
    if xc is None:
        return _pcall(body, in_specs=in_specs, out_specs=out_specs, out_shape=out_shape, scratch_shapes=scratch_shapes, **kw)(*args), None
    wrapped = _with_exchange(body, len(in_specs), len(out_specs), len(scratch_shapes), xc, first, last)
    res = _pcall(wrapped, in_specs=list(in_specs) + xc.specs, out_specs=list(out_specs) + xc.specs,
                 out_shape=list(out_shape) + xc.out_shape, scratch_shapes=list(scratch_shapes) + xc.scratch, **kw)(*args, *xc.arrs)
    return res[:len(out_specs)], res[len(out_specs):]


def _sum_slots(a, *, name, out_dtype=F32):
    ns, r, c = a.shape
    t = _tile(r, max(16, (1 << 18) // c // 16 * 16), 16)

    def body(a_ref, o_ref):
        acc = a_ref[0].astype(F32)
        for i in range(1, ns):
            acc = acc + a_ref[i].astype(F32)
        o_ref[...] = acc.astype(o_ref.dtype)

    return _pcall(body, name=name, grid=(r // t,), in_specs=[pl.BlockSpec((ns, t, c), lambda i: (0, i, 0))],
                  out_specs=pl.BlockSpec((t, c), lambda i: (i, 0)), out_shape=jax.ShapeDtypeStruct((r, c), out_dtype),
                  compiler_params=_cparams("parallel"))(a)


def _adamw(w, gparts, m, v, *, name):
    r, c = w.shape
    t = _tile(r, max(16, (1 << 17) // c // 16 * 16), 16)
    ng = len(gparts)

    def body(*refs):
        w_ref, m_ref, v_ref = refs[0], refs[1 + ng], refs[2 + ng]
        g_ref, d_ref, nm_ref, nv_ref = refs[3 + ng:]
        g = refs[1][...].astype(F32)
        for i in range(1, ng):
            g = g + refs[1 + i][...].astype(F32)
        mn = ADAM_B1 * m_ref[...] + (1.0 - ADAM_B1) * g
        vn = ADAM_B2 * v_ref[...] + (1.0 - ADAM_B2) * (g * g)
        m_hat = mn / (1.0 - ADAM_B1 ** ADAM_STEP)
        v_hat = vn / (1.0 - ADAM_B2 ** ADAM_STEP)
        g_ref[...] = g
        d_ref[...] = -ADAM_LR * (m_hat / (jnp.sqrt(v_hat) + ADAM_EPS) + ADAM_WD * w_ref[...])
        nm_ref[...] = mn
        nv_ref[...] = vn

    spec = pl.BlockSpec((t, c), lambda i: (i, 0))
    return _pcall(body, name=name, grid=(r // t,), in_specs=[spec] * (3 + ng), out_specs=[spec] * 4,
                  out_shape=[jax.ShapeDtypeStruct((r, c), F32)] * 4, compiler_params=_cparams("parallel"))(w, *gparts, m, v)


def _adamw_layers(w, halves, m, v, *, name):
    r, c = w.shape
    nl = len(halves)
    rl = r // nl
    t = _tile(rl, max(16, (1 << 17) // c // 16 * 16), 16)
    nbl = rl // t

    def body(*refs):
        w_ref, m_ref, v_ref = refs[0], refs[1 + 2 * nl], refs[2 + 2 * nl]
        g_ref, d_ref, nm_ref, nv_ref = refs[3 + 2 * nl:]
        g = None
        for l in range(nl):
            gl = refs[1 + 2 * l][...].astype(F32) + refs[2 + 2 * l][...].astype(F32)
            g = gl if g is None else jnp.where(pl.program_id(0) >= l * nbl, gl, g)
        mn = ADAM_B1 * m_ref[...] + (1.0 - ADAM_B1) * g
        vn = ADAM_B2 * v_ref[...] + (1.0 - ADAM_B2) * (g * g)
        m_hat = mn / (1.0 - ADAM_B1 ** ADAM_STEP)
        v_hat = vn / (1.0 - ADAM_B2 ** ADAM_STEP)
        g_ref[...] = g
        d_ref[...] = -ADAM_LR * (m_hat / (jnp.sqrt(v_hat) + ADAM_EPS) + ADAM_WD * w_ref[...])
        nm_ref[...] = mn
        nv_ref[...] = vn

    spec = pl.BlockSpec((t, c), lambda i: (i, 0))
    part = lambda l, core: pl.BlockSpec((None, t, c), lambda i, l=l, core=core: (core, jnp.clip(i - l * nbl, 0, nbl - 1), 0))
    gspecs = [part(l, core) for l in range(nl) for core in range(2)]
    gargs = [halves[l] for l in range(nl) for _ in range(2)]
    return _pcall(body, name=name, grid=(r // t,), in_specs=[spec] + gspecs + [spec, spec], out_specs=[spec] * 4,
                  out_shape=[jax.ShapeDtypeStruct((r, c), F32)] * 4, compiler_params=_cparams("parallel"))(w, *gargs, m, v)


FFN_TM = 512
FFN_CHUNK = 1408


def _resident(shape):
    return pl.BlockSpec(shape, lambda i: (0,) * len(shape), pipeline_mode=pl.Buffered(1))


def _ffn_up(x, sc, sh, wgu, *, name):
    s, d = x.shape
    f = wgu.shape[1] // 2
    t, fc = _rows(s, FFN_TM), _tile(f, FFN_CHUNK)

    def body(x_ref, sc_ref, sh_ref, w_ref, h_ref, uv_ref, a_ref):
        xv = x_ref[...]
        r = lax.rsqrt(jnp.mean(xv * xv, axis=1, keepdims=True) + EPS)
        hb = ((xv * r) * (1.0 + sc_ref[...]) + sh_ref[...]).astype(BF16)
        h_ref[...] = hb
        for j in range(f // fc):
            u = _dot(hb, w_ref[:, j * fc:(j + 1) * fc])
            v = _dot(hb, w_ref[:, f + j * fc:f + (j + 1) * fc])
            sg = _sigmoid(u)
            silu = u * sg
            uv_ref[:, j * fc:(j + 1) * fc] = (v * (sg * (1.0 + u * (1.0 - sg)))).astype(BF16)
            uv_ref[:, f + j * fc:f + (j + 1) * fc] = silu.astype(BF16)
            a_ref[:, j * fc:(j + 1) * fc] = (silu * v).astype(BF16)

    return _pcall(
        body, name=name, grid=(s // t,), in_specs=[_row_spec(t, d), _vec_spec(d), _vec_spec(d), _resident(wgu.shape)],
        out_specs=[_row_spec(t, d), _row_spec(t, 2 * f), _row_spec(t, f)],
        out_shape=[jax.ShapeDtypeStruct((s, d), BF16), jax.ShapeDtypeStruct((s, 2 * f), BF16), jax.ShapeDtypeStruct((s, f), BF16)],
        compiler_params=_cparams("parallel"))(x, sc, sh, wgu)


def _norm_mm(x, sc, sh, w, nb, *, name):
    s, d = x.shape
    n = w.shape[1]
    t, nc = _rows(s, FFN_TM), _tile(n, 1792)
    assert nb <= nc

    def body(x_ref, sc_ref, sh_ref, w_ref, h_ref, o_ref, ob_ref):
        xv = x_ref[...]
        r = lax.rsqrt(jnp.mean(xv * xv, axis=1, keepdims=True) + EPS)
        hb = ((xv * r) * (1.0 + sc_ref[...]) + sh_ref[...]).astype(BF16)
        h_ref[...] = hb
        for j in range(n // nc):
            part = _dot(hb, w_ref[:, j * nc:(j + 1) * nc])
            o_ref[:, j * nc:(j + 1) * nc] = part
            if j == 0:
                ob_ref[...] = part[:, :nb].astype(BF16)

    return _pcall(
        body, name=name, grid=(s // t,), in_specs=[_row_spec(t, d), _vec_spec(d), _vec_spec(d), _resident(w.shape)],
        out_specs=[_row_spec(t, d), _row_spec(t, n), _row_spec(t, nb)],
        out_shape=[jax.ShapeDtypeStruct((s, d), BF16), jax.ShapeDtypeStruct((s, n), F32), jax.ShapeDtypeStruct((s, nb), BF16)],
        compiler_params=_cparams("parallel"))(x, sc, sh, w)


def _ffn_dact(dxo, y, sg, wd, uv, *, name):
    s, d = y.shape
    f = wd.shape[0]
    t, fc = _rows(s, FFN_TM), _tile(f, FFN_CHUNK)

    def body(dxo_ref, y_ref, sg_ref, w_ref, uv_ref, dy_ref, duv_ref, ds_ref):
        @pl.when(pl.program_id(0) == 0)
        def _():
            ds_ref[...] = jnp.zeros_like(ds_ref)

        dv = dxo_ref[...]
        dyb = (sg_ref[...] * dv).astype(BF16)
        dy_ref[...] = dyb
        ds_ref[...] += jnp.sum(dv * y_ref[...], axis=0, keepdims=True)
        for j in range(f // fc):
            da = _dot(dyb, w_ref[j * fc:(j + 1) * fc, :], NT)
            duv_ref[:, j * fc:(j + 1) * fc] = (da * uv_ref[:, j * fc:(j + 1) * fc].astype(F32)).astype(BF16)
            duv_ref[:, f + j * fc:f + (j + 1) * fc] = (da * uv_ref[:, f + j * fc:f + (j + 1) * fc].astype(F32)).astype(BF16)

    return _pcall(
        body, name=name, grid=(s // t,),
        in_specs=[_row_spec(t, d), _row_spec(t, d), _vec_spec(d), _resident(wd.shape), _row_spec(t, 2 * f)],
        out_specs=[_row_spec(t, d), _row_spec(t, 2 * f), _vec_spec(d)],
        out_shape=[jax.ShapeDtypeStruct((s, d), BF16), jax.ShapeDtypeStruct((s, 2 * f), BF16), jax.ShapeDtypeStruct((1, d), F32)],
        compiler_params=_cparams("arbitrary"))(dxo, y, sg, wd, uv)


def _ffn_dh(duv, wgu, x, sc, dxo, *, name):
    s, d = x.shape
    f2 = wgu.shape[1]
    t = _rows(s, FFN_TM)

    def body(duv_ref, w_ref, x_ref, sc_ref, dxo_ref, dx_ref, dsc_ref, dsh_ref):
        @pl.when(pl.program_id(0) == 0)
        def _():
            dsc_ref[...] = jnp.zeros_like(dsc_ref)
            dsh_ref[...] = jnp.zeros_like(dsh_ref)

        dhv = _dot(duv_ref[...], w_ref[...], NT)
        xv = x_ref[...]
        r = lax.rsqrt(jnp.mean(xv * xv, axis=1, keepdims=True) + EPS)
        xn = xv * r
        dxn = dhv * (1.0 + sc_ref[...])
        dx_ref[...] = dxo_ref[...] + r * (dxn - xn * jnp.mean(dxn * xn, axis=1, keepdims=True))
        dsc_ref[...] += jnp.sum(dhv * xn, axis=0, keepdims=True)
        dsh_ref[...] += jnp.sum(dhv, axis=0, keepdims=True)

    return _pcall(
        body, name=name, grid=(s // t,),
        in_specs=[_row_spec(t, f2), _resident(wgu.shape), _row_spec(t, d), _vec_spec(d), _row_spec(t, d)],
        out_specs=[_row_spec(t, d), _vec_spec(d), _vec_spec(d)],
        out_shape=[jax.ShapeDtypeStruct((s, d), F32), jax.ShapeDtypeStruct((1, d), F32), jax.ShapeDtypeStruct((1, d), F32)],
        compiler_params=_cparams("arbitrary"))(duv, wgu, x, sc, dxo)


def _dh_pieces(pieces, w, x, sc, dxo, *, name):
    s, d = x.shape
    t = _rows(s, FFN_TM)
    widths = [p.shape[1] for p in pieces]
    offs = [sum(widths[:i]) for i in range(len(widths))]
    kd = sum(widths)
    npc = len(pieces)

    def body(*refs):
        p_refs = refs[:npc]
        w_ref, x_ref, sc_ref, dxo_ref, dx_ref, dsc_ref, dsh_ref, cat_ref = refs[npc:]

        @pl.when(pl.program_id(0) == 0)
        def _():
            dsc_ref[...] = jnp.zeros_like(dsc_ref)
            dsh_ref[...] = jnp.zeros_like(dsh_ref)

        dhv = None
        for p_ref, off, wd in zip(p_refs, offs, widths):
            pb = p_ref[...].astype(BF16)
            cat_ref[:, off:off + wd] = pb
            part = _dot(pb, w_ref[:, off:off + wd], NT)
            dhv = part if dhv is None else dhv + part
        xv = x_ref[...]
        r = lax.rsqrt(jnp.mean(xv * xv, axis=1, keepdims=True) + EPS)
        xn = xv * r
        dxn = dhv * (1.0 + sc_ref[...])
        dx_ref[...] = dxo_ref[...] + r * (dxn - xn * jnp.mean(dxn * xn, axis=1, keepdims=True))
        dsc_ref[...] += jnp.sum(dhv * xn, axis=0, keepdims=True)
        dsh_ref[...] += jnp.sum(dhv, axis=0, keepdims=True)

    return _pcall(
        body, name=name, grid=(s // t,),
        in_specs=[_row_spec(t, wd) for wd in widths] + [_resident(w.shape), _row_spec(t, d), _vec_spec(d), _row_spec(t, d)],
        out_specs=[_row_spec(t, d), _vec_spec(d), _vec_spec(d), _row_spec(t, kd)],
        out_shape=[jax.ShapeDtypeStruct((s, d), F32), jax.ShapeDtypeStruct((1, d), F32), jax.ShapeDtypeStruct((1, d), F32),
                   jax.ShapeDtypeStruct((s, kd), BF16)],
        compiler_params=_cparams("arbitrary"))(*pieces, w, x, sc, dxo)


def _ffn_fwd(x, sh, sc, g, wgu, wd, tag):
    h, uv, a = _ffn_up(x, sc, sh, wgu, name=f"{tag}_up")
    y, xo = _mm(a, wd, name=f"{tag}_down", tm=512, tn=1024, tk=wd.shape[0], res=x, scale=0.5 * g)
    return xo, (x, h, uv, a, y)


def _ffn_bwd(dxo, saved, sc, g, wgu, wd, tag):
    x, h, uv, a, y = saved
    dyb, duv, dgs = _ffn_dact(dxo, y, 0.5 * g, wd, uv, name=f"{tag}_dact")
    dx, dsc, dsh = _ffn_dh(duv, wgu, x, sc, dxo, name=f"{tag}_dh")
    dwgu = _mm_tn(h, duv, name=f"{tag}_dwgu", tm=1024, tn=1408, tk=512)
    dwd = _mm_tn(a, dyb, name=f"{tag}_dwd", tm=1408, tn=1024, tk=512)
    return dx, dwgu, dwd, dsh, dsc, 0.5 * dgs


def _layer_fwd(x0, mod, w, par, tag):
    s = x0.shape[0]
    sh1, sc1, g1, sh2, sc2, g2, sh3, sc3, g3 = (mod[i:i + 1] for i in range(N_MOD))
    x1, f1 = _ffn_fwd(x0, sh1, sc1, g1, w["gu1"], w["d1"], f"{tag}_ffn1")
    h2, proj, qkv = _norm_mm(x1, sc2, sh2, w["in"], 3 * SB_W, name=f"{tag}_in")
    o_a = _sb_fwd(qkv, name=f"{tag}_sb")
    qd, kd = _dil_prep(proj, par["gq"], par["gk"], par["cos"], par["sin"], name=f"{tag}_dil_prep")
    nums, dens, mxs = [], [], []
    for _, r in DIL_PATTERNS:
        nu, de, mx = _dil_fwd(qd, kd, proj, r, name=f"{tag}_dil{r}")
        nums.append(nu)
        dens.append(de)
        mxs.append(mx)
    oh, st = _hgrn_fwd(proj, par["la"], par["lc"], name=f"{tag}_hgrn")
    ymix, od, mall, zall = _mix_out(o_a, nums, dens, mxs, oh, proj, par["hg"], name=f"{tag}_mix_out")
    out, x2 = _mm(ymix, w["out"], name=f"{tag}_out", tm=512, tn=1024, tk=1024, res=x1, scale=g2)
    x3, f2 = _ffn_fwd(x2, sh3, sc3, g3, w["gu2"], w["d2"], f"{tag}_ffn2")
    return x3, dict(f1=f1, f2=f2, x1=x1, h2=h2, proj=proj, qkv=qkv, o_a=o_a, qd=qd, kd=kd, oh=oh, st=st,
                    ymix=ymix, od=od, mall=mall, zall=zall, out=out)


def _layer_bwd(dx3, sv, mod, w, par, tag, pending=None):
    sh1, sc1, g1, sh2, sc2, g2, sh3, sc3, g3 = (mod[i:i + 1] for i in range(N_MOD))
    dx2, dwgu2, dwd2, dsh3, dsc3, dg3 = _ffn_bwd(dx3, sv["f2"], sc3, g3, w["gu2"], w["d2"], f"{tag}_ffn2")
    doutb, dg2 = _gate_bwd(dx2, sv["out"], g2, name=f"{tag}_dgate2")
    dmix = _mm(doutb, w["out"], name=f"{tag}_dmix", tb=True, tm=512, tn=1024, tk=1024)
    dwout = _mm_tn(sv["ymix"], doutb, name=f"{tag}_dwout", tm=1024, tn=1024, tk=512)
    proj = sv["proj"]
    doh, dgh, delta, dhg = _mix_out_bwd(dmix, sv["oh"], proj, par["hg"], sv["od"], name=f"{tag}_dmix_out")
    (dqa, dka, dva), got = _sb_bwd(sv["qkv"], sv["o_a"], dmix, name=f"{tag}_dsb",
                                   side=None if pending is None else _Xchg(pending, "chips", True))
    parts = None if got is None else [_sum_slots(g, name=f"{tag}_sum{i}", out_dtype=BF16) for i, g in enumerate(got)]
    dqs, dks, dvs = [], [], []
    for _, r in DIL_PATTERNS:
        a, b, c = _dil_bwd(sv["qd"], sv["kd"], proj, dmix, sv["mall"], sv["zall"], delta, r, name=f"{tag}_ddil{r}")
        dqs.append(a)
        dks.append(b)
        dvs.append(c)
    dqd, dkd, dvd, dgq, dgk = _dil_prep_bwd(proj, par["gq"], par["gk"], par["cos"], par["sin"], dqs, dks, dvs,
                                             name=f"{tag}_ddil_prep")
    (dqh, dfh, dih, dla, dlc), swapped = _hgrn_bwd(proj, par["la"], par["lc"], sv["st"], doh, name=f"{tag}_dhgrn",
                                                   side=None if parts is None else _Xchg(parts, "sib", False))
    dx1, dsc2, dsh2, dproj = _dh_pieces([dqa, dka, dva, dqd, dkd, dvd, dqh, dfh, dih, dgh], w["in"], sv["x1"], sc2, dx2,
                                         name=f"{tag}_dh2")
    dwin = _mm_tn(sv["h2"], dproj, name=f"{tag}_dwin", tm=1024, tn=1792, tk=512)
    dx0, dwgu1, dwd1, dsh1, dsc1, dg1 = _ffn_bwd(dx1, sv["f1"], sc1, g1, w["gu1"], w["d1"], f"{tag}_ffn1")
    dmod = jnp.concatenate([dsh1, dsc1, dg1, dsh2, dsc2, dg2, dsh3, dsc3, dg3], axis=0)
    fold = lambda v: v.reshape(2, HEAD_DIM).sum(axis=0)
    grads = dict(gu1=dwgu1, d1=dwd1, gu2=dwgu2, d2=dwd2, win=dwin, wout=dwout, dmod=dmod, gq=fold(dgq), gk=fold(dgk),
                 hg=dhg[0], la=dla[0], lc=dlc[0])
    return dx0, grads, swapped


def _pack(pieces):
    flat = jnp.concatenate([p.reshape(-1) for p in pieces])
    pad = (-flat.shape[0]) % (8 * LANES)
    return jnp.pad(flat, (0, pad)).reshape(-1, LANES)


def _unpack(flat, like):
    out, off = [], 0
    for p in like:
        out.append(flat[off:off + p.size].reshape(p.shape))
        off += p.size
    return out


def kernel(x, c, w_mod, b_mod, ffn1_w_gate, ffn1_w_up, ffn1_w_down, w_in, w_out, q_norm_g, k_norm_g, hgrn_norm_g, hgrn_lb_logits, ffn2_w_gate, ffn2_w_up, ffn2_w_down, loss_target, m_w_mod, m_b_mod, m_ffn1_w_gate, m_ffn1_w_up, m_ffn1_w_down, m_w_in, m_w_out, m_q_norm_g, m_k_norm_g, m_hgrn_norm_g, m_hgrn_lb_logits, m_ffn2_w_gate, m_ffn2_w_up, m_ffn2_w_down, v_w_mod, v_b_mod, v_ffn1_w_gate, v_ffn1_w_up, v_ffn1_w_down, v_w_in, v_w_out, v_q_norm_g, v_k_norm_g, v_hgrn_norm_g, v_hgrn_lb_logits, v_ffn2_w_gate, v_ffn2_w_up, v_ffn2_w_down):
    names = ["w_mod", "b_mod", "ffn1_w_gate", "ffn1_w_up", "ffn1_w_down", "w_in", "w_out", "q_norm_g", "k_norm_g",
             "hgrn_norm_g", "hgrn_lb_logits", "ffn2_w_gate", "ffn2_w_up", "ffn2_w_down"]
    wts = dict(zip(names, (w_mod, b_mod, ffn1_w_gate, ffn1_w_up, ffn1_w_down, w_in, w_out, q_norm_g, k_norm_g, hgrn_norm_g,
                           hgrn_lb_logits, ffn2_w_gate, ffn2_w_up, ffn2_w_down)))
    mom = dict(zip(names, (m_w_mod, m_b_mod, m_ffn1_w_gate, m_ffn1_w_up, m_ffn1_w_down, m_w_in, m_w_out, m_q_norm_g, m_k_norm_g,
                           m_hgrn_norm_g, m_hgrn_lb_logits, m_ffn2_w_gate, m_ffn2_w_up, m_ffn2_w_down)))
    var = dict(zip(names, (v_w_mod, v_b_mod, v_ffn1_w_gate, v_ffn1_w_up, v_ffn1_w_down, v_w_in, v_w_out, v_q_norm_g, v_k_norm_g,
                           v_hgrn_norm_g, v_hgrn_lb_logits, v_ffn2_w_gate, v_ffn2_w_up, v_ffn2_w_down)))
    depth = w_mod.shape[0]
    assert depth == 2 and x.shape[0] == 1
    s, d = x.shape[1:]
    assert s % (DIL_PATTERNS[-1][1] * QBLK) == 0 and d % LANES == 0
    xi, yi, ci = lax.axis_index("x"), lax.axis_index("y"), lax.axis_index("c")
    chip = 2 * xi + yi
    dev = 2 * chip + ci
    x0, tgt = x[0], loss_target[0]

    c8 = _exchange([c.reshape(d // LANES, LANES)], "all", False, name="gather_c")[0].reshape(8, d)
    ncol = w_mod.shape[2]
    b_loc = lax.dynamic_slice_in_dim(b_mod, chip * ncol, ncol, axis=1)
    m_loc = _mod_fwd(c8, w_mod, b_loc.reshape(depth, 1, ncol), name="mod_fwd")
    m_all = _exchange([m_loc], "chips", False, name="gather_mod")[0]
    mod = jnp.transpose(lax.dynamic_index_in_dim(m_all, dev, axis=2, keepdims=False), (1, 0, 2)).reshape(depth, N_MOD, d)

    col_sharded = ["ffn1_w_gate", "ffn1_w_up", "w_in", "ffn2_w_gate", "ffn2_w_up"]
    row_sharded = ["ffn1_w_down", "w_out", "ffn2_w_down"]
    big = col_sharded + row_sharded
    flat = [wts[n].astype(BF16).reshape(-1, wts[n].shape[-1]) for n in big]
    gathered = {n: g.reshape((4,) + wts[n].shape) for n, g in zip(big, _exchange(flat, "chips", False, name="gather_w"))}
    full = {}
    for n in col_sharded:
        g = gathered[n]
        full[n] = jnp.moveaxis(g, 0, 2).reshape(depth, g.shape[2], -1)
    for n in row_sharded:
        g = gathered[n]
        full[n] = jnp.moveaxis(g, 0, 1).reshape(depth, -1, g.shape[3])
    ws = [dict(gu1=jnp.concatenate([full["ffn1_w_gate"][l], full["ffn1_w_up"][l]], axis=1), d1=full["ffn1_w_down"][l],
               gu2=jnp.concatenate([full["ffn2_w_gate"][l], full["ffn2_w_up"][l]], axis=1), d2=full["ffn2_w_down"][l],
               **{"in": full["w_in"][l], "out": full["w_out"][l]}) for l in range(depth)]

    _, la, lc = _lb_prep(hgrn_lb_logits, name="lb_prep")
    cos, sin = _rope_tables(s)
    pars = [dict(gq=jnp.tile(q_norm_g[l], 2)[None], gk=jnp.tile(k_norm_g[l], 2)[None], hg=hgrn_norm_g[l][None],
                 la=la[l:l + 1], lc=lc[l:l + 1], cos=cos, sin=sin) for l in range(depth)]

    xs, saved = x0, []
    for l in range(depth):
        xs, sv = _layer_fwd(xs, mod[l], ws[l], pars[l], f"l{l}")
        saved.append(sv)
    dx, lpart = _loss_grad(xs, tgt, name="loss")

    col_grad = {"ffn1_w_gate": ("gu1", 0), "ffn1_w_up": ("gu1", 1), "ffn2_w_gate": ("gu2", 0), "ffn2_w_up": ("gu2", 1), "w_in": ("win", None)}
    row_grad = {"ffn1_w_down": "d1", "ffn2_w_down": "d2", "w_out": "wout"}

    def by_chip(gr):
        out = []
        for n in big:
            if n in col_grad:
                key, half = col_grad[n]
                g = gr[key]
                if half is not None:
                    g = g[:, half * (g.shape[1] // 2):(half + 1) * (g.shape[1] // 2)]
                out.append(jnp.moveaxis(g.reshape(g.shape[0], 4, -1), 1, 0))
            else:
                g = gr[row_grad[n]]
                out.append(g.reshape(4, -1, g.shape[1]))
        return out

    grads, halves, pending = [None] * depth, [None] * depth, None
    for l in reversed(range(depth)):
        dx, grads[l], swapped = _layer_bwd(dx, saved[l], mod[l], ws[l], pars[l], f"l{l}", pending)
        if pending is not None:
            halves[l + 1] = swapped
        pending = by_chip(grads[l])
    got = _exchange(pending, "chips", True, name="scatter_grads")
    parts = [_sum_slots(g, name=f"sum_{n}", out_dtype=BF16) for n, g in zip(big, got)]
    halves[0] = _exchange(parts, "sib", False, name="swap_grads")

    stack = lambda k: jnp.stack([grads[l][k] for l in range(depth)])
    small = [stack("dmod"), stack("gq"), stack("gk"), stack("hg"), stack("la"), stack("lc"), lpart[0, :1]]
    packed = _pack(small)
    allp = _exchange([packed], "all", False, name="gather_small")[0]
    tot = _unpack(_sum_slots(allp, name="sum_small").reshape(-1), small)
    g_b_mod = tot[0].reshape(depth, N_MOD * d)
    loss = tot[6][0]
    g_small = {"b_mod": g_b_mod, "q_norm_g": tot[1], "k_norm_g": tot[2], "hgrn_norm_g": tot[3],
               "hgrn_lb_logits": _lb_bwd(hgrn_lb_logits, tot[4], tot[5], name="lb_bwd")}

    dm_all = allp.reshape(8, -1)[:, :depth * N_MOD * d].reshape(8, depth, N_MOD * d)
    dm_loc = jnp.transpose(lax.dynamic_slice_in_dim(dm_all, chip * ncol, ncol, axis=2), (1, 0, 2))
    g_w_mod = _mod_bwd(c8.T, dm_loc, name="mod_bwd")

    outs = {}
    for n in names:
        w2 = wts[n].reshape(-1, wts[n].shape[-1])
        m2, v2 = mom[n].reshape(w2.shape), var[n].reshape(w2.shape)
        if n in big:
            res = _adamw_layers(w2, [halves[l][big.index(n)] for l in range(depth)], m2, v2, name=f"adamw_{n}")
        else:
            g = g_w_mod if n == "w_mod" else g_small[n]
            res = _adamw(w2, [g.reshape(w2.shape)], m2, v2, name=f"adamw_{n}")
        outs[n] = [r.reshape(wts[n].shape) for r in res]
    return (loss, dx[None], *[outs[n][0] for n in names], *[outs[n][1] for n in names], *[outs[n][2] for n in names],
            *[outs[n][3] for n in names])
```

```python
import functools
import math

import jax
import jax.numpy as jnp
from jax import lax
from jax.experimental import pallas as pl
from jax.experimental.pallas import tpu as pltpu

F32 = jnp.float32
BF16 = jnp.bfloat16
MESH_ID = pl.DeviceIdType.MESH

HEAD_DIM = 64
SB_W = 256
DIL_W = 256
HG_W = 512
HG_D = 128
IN_W = 3 * SB_W + 3 * DIL_W + 4 * HG_W
MIX_W = SB_W + DIL_W + HG_W
DIL_PATTERNS = ((128, 1), (512, 4), (2048, 16))
ROPE_THETA = 10000.0
EPS = 1e-6
LB_FLOOR = 1e-30
NEG_BIG = -1e30
N_MOD = 9
ADAM_LR = 0.001
ADAM_B1 = 0.9
ADAM_B2 = 0.999
ADAM_EPS = 1e-08
ADAM_WD = 0.01
ADAM_STEP = 10

LANES = 128
QBLK = 128
DIL_TILE = 1024
HG_BLK = 16
HG_TILE = 256
SB_EXIT = 88.0
VMEM_LIMIT = 48 * 1024 * 1024
XCHG_CHUNKS = 8
XCHG_MIN_BYTES = 1 << 19

NN = (((1,), (0,)), ((), ()))
NT = (((1,), (1,)), ((), ()))
TN = (((0,), (0,)), ((), ()))


def _pcall(body, **kw):
    return pl.pallas_call(body, **kw)


def _cparams(*sem):
    return pltpu.CompilerParams(dimension_semantics=sem if sem else None, vmem_limit_bytes=VMEM_LIMIT)


def _dot(a, b, dims=NN):
    return lax.dot_general(a, b, dims, preferred_element_type=F32)


def _split(x, n):
    parts = []
    r = x
    for i in range(n):
        p = r.astype(BF16)
        parts.append(p)
        if i + 1 < n:
            r = r - p.astype(F32)
    return parts


def _xdot(x, m, n=2):
    return sum(_dot(p, m) for p in _split(x, n))


def _xdot_left(m, x, n=3):
    return sum(_dot(m, p) for p in _split(x, n))


def _iota(shape, dim):
    return lax.broadcasted_iota(jnp.int32, shape, dim)


def _sigmoid(x):
    return 1.0 / (1.0 + jnp.exp(-x))


def _tile(dim, pref, mult=LANES):
    t = (min(pref, dim) // mult) * mult
    while t >= mult:
        if dim % t == 0:
            return t
        t -= mult
    return dim


def _rows(dim, pref):
    return _tile(dim, pref, 8)


def _mm(a, b, *, name, tb=False, tm=512, tn=1024, tk=1024, out_dtype=F32, res=None, scale=None):
    m, kd = a.shape
    n = b.shape[0] if tb else b.shape[1]
    tm, tn, tk = _rows(m, tm), _tile(n, tn), _tile(kd, tk)
    nk = kd // tk
    epi = res is not None

    def body(*refs):
        if epi:
            a_ref, b_ref, r_ref, s_ref, o_ref, x_ref, acc = refs
        else:
            a_ref, b_ref, o_ref, acc = refs
        k = pl.program_id(2)

        @pl.when(k == 0)
        def _():
            acc[...] = jnp.zeros_like(acc)

        acc[...] += _dot(a_ref[...], b_ref[...], NT if tb else NN)

        @pl.when(k == nk - 1)
        def _():
            o_ref[...] = acc[...].astype(o_ref.dtype)
            if epi:
                x_ref[...] = r_ref[...] + s_ref[...] * acc[...]

    in_specs = [
        pl.BlockSpec((tm, tk), lambda i, j, k: (i, k)),
        pl.BlockSpec((tn, tk), lambda i, j, k: (j, k)) if tb else pl.BlockSpec((tk, tn), lambda i, j, k: (k, j)),
    ]
    out_shape = [jax.ShapeDtypeStruct((m, n), out_dtype)]
    out_specs = [pl.BlockSpec((tm, tn), lambda i, j, k: (i, j))]
    args = [a, b]
    if epi:
        in_specs += [pl.BlockSpec((tm, tn), lambda i, j, k: (i, j)), pl.BlockSpec((1, tn), lambda i, j, k: (0, j))]
        out_shape.append(jax.ShapeDtypeStruct((m, n), F32))
        out_specs.append(pl.BlockSpec((tm, tn), lambda i, j, k: (i, j)))
        args += [res, scale]
    out = _pcall(
        body, name=name, grid=(m // tm, n // tn, nk), in_specs=in_specs, out_specs=out_specs, out_shape=out_shape,
        scratch_shapes=[pltpu.VMEM((tm, tn), F32)], compiler_params=_cparams("parallel", "parallel", "arbitrary"),
    )(*args)
    return out if epi else out[0]


def _mm_tn(a, b, *, name, tm=1024, tn=1408, tk=512, out_dtype=BF16):
    s, m = a.shape
    n = b.shape[1]
    tm, tn, tk = _tile(m, tm), _tile(n, tn), _rows(s, tk)
    nk = s // tk

    def body(a_ref, b_ref, o_ref, acc):
        k = pl.program_id(2)

        @pl.when(k == 0)
        def _():
            acc[...] = jnp.zeros_like(acc)

        acc[...] += _dot(a_ref[...], b_ref[...], TN)

        @pl.when(k == nk - 1)
        def _():
            o_ref[...] = acc[...].astype(o_ref.dtype)

    return _pcall(
        body, name=name, grid=(m // tm, n // tn, nk),
        in_specs=[pl.BlockSpec((tk, tm), lambda i, j, k: (k, i)), pl.BlockSpec((tk, tn), lambda i, j, k: (k, j))],
        out_specs=pl.BlockSpec((tm, tn), lambda i, j, k: (i, j)), out_shape=jax.ShapeDtypeStruct((m, n), out_dtype),
        scratch_shapes=[pltpu.VMEM((tm, tn), F32)], compiler_params=_cparams("parallel", "parallel", "arbitrary"),
    )(a, b)


TE = 512


def _row_spec(t, w, col=0):
    return pl.BlockSpec((t, w), lambda i, col=col: (i, col))


def _vec_spec(w, col=0):
    return pl.BlockSpec((1, w), lambda i, col=col: (0, col))


def _norm_mod(x, sc, sh, *, name):
    s, d = x.shape
    t = _rows(s, TE)

    def body(x_ref, sc_ref, sh_ref, h_ref):
        xv = x_ref[...]
        r = lax.rsqrt(jnp.mean(xv * xv, axis=1, keepdims=True) + EPS)
        h_ref[...] = ((xv * r) * (1.0 + sc_ref[...]) + sh_ref[...]).astype(BF16)

    return _pcall(body, name=name, grid=(s // t,), in_specs=[_row_spec(t, d), _vec_spec(d), _vec_spec(d)],
                  out_specs=_row_spec(t, d), out_shape=jax.ShapeDtypeStruct((s, d), BF16),
                  compiler_params=_cparams("parallel"))(x, sc, sh)


def _norm_mod_bwd(dh, x, sc, dxo, *, name):
    s, d = x.shape
    t = _rows(s, TE)

    def body(dh_ref, x_ref, sc_ref, dxo_ref, dx_ref, dsc_ref, dsh_ref):
        @pl.when(pl.program_id(0) == 0)
        def _():
            dsc_ref[...] = jnp.zeros_like(dsc_ref)
            dsh_ref[...] = jnp.zeros_like(dsh_ref)

        xv = x_ref[...]
        dhv = dh_ref[...]
        r = lax.rsqrt(jnp.mean(xv * xv, axis=1, keepdims=True) + EPS)
        xn = xv * r
        dxn = dhv * (1.0 + sc_ref[...])
        dx_ref[...] = dxo_ref[...] + r * (dxn - xn * jnp.mean(dxn * xn, axis=1, keepdims=True))
        dsc_ref[...] += jnp.sum(dhv * xn, axis=0, keepdims=True)
        dsh_ref[...] += jnp.sum(dhv, axis=0, keepdims=True)

    return _pcall(
        body, name=name, grid=(s // t,),
        in_specs=[_row_spec(t, d), _row_spec(t, d), _vec_spec(d), _row_spec(t, d)],
        out_specs=[_row_spec(t, d), _vec_spec(d), _vec_spec(d)],
        out_shape=[jax.ShapeDtypeStruct((s, d), F32), jax.ShapeDtypeStruct((1, d), F32), jax.ShapeDtypeStruct((1, d), F32)],
        compiler_params=_cparams("arbitrary"))(dh, x, sc, dxo)


def _swiglu(uv, *, name):
    s, f2 = uv.shape
    f = f2 // 2
    t = _rows(s, 256)

    def body(uv_ref, a_ref):
        u = uv_ref[:, :f]
        v = uv_ref[:, f:]
        a_ref[...] = (u * _sigmoid(u) * v).astype(BF16)

    return _pcall(body, name=name, grid=(s // t,), in_specs=[_row_spec(t, f2)], out_specs=_row_spec(t, f),
                  out_shape=jax.ShapeDtypeStruct((s, f), BF16), compiler_params=_cparams("parallel"))(uv)


def _swiglu_bwd(da, uv, *, name):
    s, f2 = uv.shape
    f = f2 // 2
    t = _rows(s, 256)

    def body(da_ref, uv_ref, d_ref):
        u = uv_ref[:, :f]
        v = uv_ref[:, f:]
        dav = da_ref[...]
        sg = _sigmoid(u)
        d_ref[:, :f] = (dav * v * (sg * (1.0 + u * (1.0 - sg)))).astype(BF16)
        d_ref[:, f:] = (dav * (u * sg)).astype(BF16)

    return _pcall(body, name=name, grid=(s // t,), in_specs=[_row_spec(t, f), _row_spec(t, f2)],
                  out_specs=_row_spec(t, f2), out_shape=jax.ShapeDtypeStruct((s, f2), BF16),
                  compiler_params=_cparams("parallel"))(da, uv)


def _gate_bwd(dxo, y, sg, *, name):
    s, d = y.shape
    t = _rows(s, TE)

    def body(dxo_ref, y_ref, sg_ref, dy_ref, ds_ref):
        @pl.when(pl.program_id(0) == 0)
        def _():
            ds_ref[...] = jnp.zeros_like(ds_ref)

        dv = dxo_ref[...]
        dy_ref[...] = (sg_ref[...] * dv).astype(BF16)
        ds_ref[...] += jnp.sum(dv * y_ref[...], axis=0, keepdims=True)

    return _pcall(
        body, name=name, grid=(s // t,), in_specs=[_row_spec(t, d), _row_spec(t, d), _vec_spec(d)],
        out_specs=[_row_spec(t, d), _vec_spec(d)],
        out_shape=[jax.ShapeDtypeStruct((s, d), BF16), jax.ShapeDtypeStruct((1, d), F32)],
        compiler_params=_cparams("arbitrary"))(dxo, y, sg)


def _loss_grad(y, tgt, *, name):
    s, d = y.shape
    t = _rows(s, TE)
    nt = s // t

    def body(y_ref, t_ref, dy_ref, l_ref, acc):
        i = pl.program_id(0)

        @pl.when(i == 0)
        def _():
            acc[...] = jnp.zeros_like(acc)

        e = y_ref[...] - t_ref[...]
        dy_ref[...] = e * (1.0 / d)
        acc[...] += jnp.sum(e * e, axis=0, keepdims=True)

        @pl.when(i == nt - 1)
        def _():
            l_ref[...] = jnp.broadcast_to(jnp.sum(acc[...], axis=1, keepdims=True) * (0.5 / d), l_ref.shape)

    return _pcall(
        body, name=name, grid=(nt,), in_specs=[_row_spec(t, d), _row_spec(t, d)],
        out_specs=[_row_spec(t, d), pl.BlockSpec((1, LANES), lambda i: (0, 0))],
        out_shape=[jax.ShapeDtypeStruct((s, d), F32), jax.ShapeDtypeStruct((1, LANES), F32)],
        scratch_shapes=[pltpu.VMEM((1, d), F32)], compiler_params=_cparams("arbitrary"))(y, tgt)


SB_TQ = 256
SB_NK = SB_TQ // QBLK


def _sb_consts():
    r = _iota((QBLK, LANES), 0)
    c = _iota((QBLK, LANES), 1)
    ones = jnp.ones((QBLK, LANES), BF16)
    after = jnp.concatenate([jnp.where(r > c, 1.0, 0.0).astype(BF16), ones], axis=1)
    from_ = jnp.concatenate([jnp.where(r >= c, 1.0, 0.0).astype(BF16), ones], axis=1)
    return _iota((SB_TQ, LANES), 0), _iota((SB_TQ, LANES), 1), after, from_


def _sb_scores(qm, kb, strict):
    z = _dot(qm, kb, NT) * (HEAD_DIM ** -0.5)
    sp = jnp.log(1.0 + jnp.exp(-jnp.abs(z)))
    lnb = -(jnp.maximum(z, 0.0) + sp)
    lb = jnp.minimum(z, 0.0) - sp
    if strict is not None:
        lnb = jnp.where(strict, lnb, 0.0)
    return lnb, lb


def _sb_fwd(qkv, *, name, side=None):
    s = qkv.shape[0]
    nq = s // SB_TQ

    def body(q_ref, k_ref, v_ref, o_ref, *scr):
        acc, osc = scr[:4], scr[4:]
        qi = pl.program_id(0)
        row, lane, after, _ = _sb_consts()
        h0 = lane < HEAD_DIM
        q = q_ref[...]
        qms = []
        for p in range(2):
            qp = q[:, p * LANES:(p + 1) * LANES]
            qms += [jnp.where(h0, qp, jnp.zeros_like(qp)), jnp.where(h0, jnp.zeros_like(qp), qp)]

        def block(kj, mask):
            off = pl.multiple_of(kj * QBLK, QBLK)
            kbs = [k_ref[pl.ds(off, QBLK), p * LANES:(p + 1) * LANES] for p in range(2)]
            vbs = [v_ref[pl.ds(off, QBLK), p * LANES:(p + 1) * LANES] for p in range(2)]
            sc = [_sb_scores(qms[c], kbs[c // 2], mask) for c in range(4)]
            trs = [_xdot(sc[c][0], after) for c in range(4)]
            top = None
            for c in range(4):
                w = jnp.exp(sc[c][1] + trs[c][:, :QBLK] + acc[c][...])
                if mask is not None:
                    w = jnp.where(mask, w, 0.0)
                osc[c][...] += _xdot(w, vbs[c // 2])
                new = acc[c][...] + trs[c][:, QBLK:]
                acc[c][...] = new
                top = new if top is None else jnp.maximum(top, new)
            return jnp.max(top)

        for ref in scr:
            ref[...] = jnp.zeros_like(ref)
        top = None
        for j in reversed(range(SB_NK)):
            top = block(qi * SB_NK + j, (lane + j * QBLK) < row)
        lax.while_loop(lambda c: (c[0] >= 0) & (c[1] > -SB_EXIT), lambda c: (c[0] - 1, block(c[0], None)),
                       (qi * SB_NK - 1, top))
        for p in range(2):
            o_ref[:, p * LANES:(p + 1) * LANES] = jnp.where(h0, osc[2 * p][...], osc[2 * p + 1][...])

    (o,), got = _call_with_exchange(
        body, side, lambda: pl.program_id(0) == 0, lambda: pl.program_id(0) == nq - 1, (qkv, qkv, qkv), name=name, grid=(nq,),
        in_specs=[pl.BlockSpec((SB_TQ, SB_W), lambda i: (i, 0)),
                  pl.BlockSpec((s, SB_W), lambda i: (0, 1)),
                  pl.BlockSpec((s, SB_W), lambda i: (0, 2))],
        out_specs=[pl.BlockSpec((SB_TQ, SB_W), lambda i: (i, 0))],
        out_shape=[jax.ShapeDtypeStruct((s, SB_W), F32)],
        scratch_shapes=[pltpu.VMEM((SB_TQ, LANES), F32)] * 8,
        compiler_params=_cparams("arbitrary"))
    return o, got


def _sb_bwd(qkv, o, dmix, *, name, side=None):
    s = qkv.shape[0]
    nq = s // SB_TQ
    scale = HEAD_DIM ** -0.5

    def body(q_ref, k_ref, v_ref, o_ref, do_ref, dq_ref, dk_ref, dv_ref, a0, a1, r0, r1, dqs, dks, dvs):
        acc, racc = (a0, a1), (r0, r1)
        i = pl.program_id(1)
        qi = nq - 1 - i
        row, lane, after, from_ = _sb_consts()
        klane = _iota((QBLK, LANES), 1)
        khms = (klane < HEAD_DIM, klane >= HEAD_DIM)

        @pl.when(i == 0)
        def _():
            dks[...] = jnp.zeros_like(dks)
            dvs[...] = jnp.zeros_like(dvs)

        q = q_ref[...]
        do = do_ref[...]
        dob = do.astype(BF16)
        dd = do * o_ref[...]
        dol = (do - dob.astype(F32)).astype(BF16)
        zero = jnp.zeros_like(q)
        hms = (lane < HEAD_DIM, lane >= HEAD_DIM)
        qms = [jnp.where(hm, q, zero) for hm in hms]
        doms = [jnp.where(hm, dob, zero) for hm in hms]
        dols = [jnp.where(hm, dol, zero) for hm in hms]
        dsums = [jnp.sum(jnp.where(hm, dd, 0.0), axis=1, keepdims=True) for hm in hms]

        def block(kj, mask):
            off = pl.multiple_of(kj * QBLK, QBLK)
            kb = k_ref[pl.ds(off, QBLK), :]
            vb = v_ref[pl.ds(off, QBLK), :]
            top, dq, dk, dv = None, None, None, None
            sc = [_sb_scores(qms[h], kb, mask) for h in range(2)]
            trs = [_xdot(sc[h][0], after) for h in range(2)]
            dws = [_dot(doms[h], vb, NT) + _dot(dols[h], vb, NT) for h in range(2)]
            for h in range(2):
                lb, tr = sc[h][1], trs[h]
                w = jnp.exp(lb + tr[:, :QBLK] + acc[h][...])
                if mask is not None:
                    w = jnp.where(mask, w, 0.0)
                g = w * dws[h]
                tg = _xdot(g, from_)
                before = dsums[h] - (tg[:, :QBLK] + racc[h][...])
                dz = g - jnp.exp(lb) * (g + before)
                if mask is not None:
                    dz = jnp.where(mask, dz, 0.0)
                dzb = (dz * scale).astype(BF16)
                dqh = _dot(dzb, jnp.where(khms[h], kb, jnp.zeros_like(kb)))
                dkh = _dot(dzb, qms[h], TN)
                dvh = _dot(w.astype(BF16), doms[h], TN)
                dq, dk, dv = (dqh, dkh, dvh) if h == 0 else (dq + dqh, dk + dkh, dv + dvh)
                new = acc[h][...] + tr[:, QBLK:]
                acc[h][...] = new
                racc[h][...] += tg[:, QBLK:]
                top = new if top is None else jnp.maximum(top, new)
            dqs[...] += dq
            dks[pl.ds(off, QBLK), :] += dk
            dvs[pl.ds(off, QBLK), :] += dv
            return jnp.max(top)

        for ref in (dqs, a0, a1, r0, r1):
            ref[...] = jnp.zeros_like(ref)
        top = None
        for j in reversed(range(SB_NK)):
            top = block(qi * SB_NK + j, (lane + j * QBLK) < row)
        lax.while_loop(lambda c: (c[0] >= 0) & (c[1] > -SB_EXIT), lambda c: (c[0] - 1, block(c[0], None)),
                       (qi * SB_NK - 1, top))
        dq_ref[...] = dqs[...]
        fin = pl.multiple_of(qi * SB_TQ, SB_TQ)
        dk_ref[...] = dks[pl.ds(fin, SB_TQ), :]
        dv_ref[...] = dvs[pl.ds(fin, SB_TQ), :]

    blk = lambda c0: pl.BlockSpec((SB_TQ, LANES), lambda p, i, c0=c0: (nq - 1 - i, c0 + p))
    return _call_with_exchange(
        body, side, lambda: (pl.program_id(0) == 0) & (pl.program_id(1) == 0),
        lambda: (pl.program_id(0) == 1) & (pl.program_id(1) == nq - 1), (qkv, qkv, qkv, o, dmix), name=name, grid=(2, nq),
        in_specs=[blk(0), pl.BlockSpec((s, LANES), lambda p, i: (0, 2 + p)), pl.BlockSpec((s, LANES), lambda p, i: (0, 4 + p)),
                  blk(0), blk(0)],
        out_specs=[blk(0), blk(0), blk(0)],
        out_shape=[jax.ShapeDtypeStruct((s, SB_W), F32)] * 3,
        scratch_shapes=[pltpu.VMEM((SB_TQ, LANES), F32)] * 5 + [pltpu.VMEM((s, LANES), F32), pltpu.VMEM((s, LANES), F32)],
        compiler_params=_cparams("arbitrary", "arbitrary"))


def _seg_consts():
    r = _iota((LANES, LANES), 0)
    c = _iota((LANES, LANES), 1)
    return jnp.where((r >> 6) == (c >> 6), 1.0, 0.0).astype(BF16)


def _rot_half(x, lane):
    half = HEAD_DIM // 2
    return jnp.where((lane & (HEAD_DIM - 1)) < half, pltpu.roll(x, LANES - half, 1), pltpu.roll(x, half, 1))


def _rope_tables(s):
    half = HEAD_DIM // 2
    inv_freq = ROPE_THETA ** (-jnp.arange(half, dtype=F32) * 2.0 / HEAD_DIM)
    ang = jnp.arange(s, dtype=F32)[:, None] * inv_freq[None, :]
    cos, sin = jnp.cos(ang), jnp.sin(ang)
    return jnp.tile(jnp.concatenate([cos, cos], axis=1), (1, 2)), jnp.tile(jnp.concatenate([-sin, sin], axis=1), (1, 2))


def _dil_prep(proj, gq, gk, cos, sin, *, name):
    s = proj.shape[0]
    t = _rows(s, TE)
    c0 = 3 * SB_W // LANES

    def body(q_ref, k_ref, gq_ref, gk_ref, cos_ref, sin_ref, qo_ref, ko_ref):
        seg = _seg_consts()
        lane = _iota((t, LANES), 1)
        cs, sn = cos_ref[...], sin_ref[...]
        for x_ref, g_ref, o_ref, mul in ((q_ref, gq_ref, qo_ref, HEAD_DIM ** -0.5), (k_ref, gk_ref, ko_ref, 1.0)):
            for j in range(2):
                xv = x_ref[:, j * LANES:(j + 1) * LANES]
                ms = _xdot(xv * xv, seg, 3) * (1.0 / HEAD_DIM)
                xn = xv * lax.rsqrt(ms + EPS) * g_ref[...]
                o_ref[:, j * LANES:(j + 1) * LANES] = (xn * cs + _rot_half(xn, lane) * sn) * mul

    return _pcall(
        body, name=name, grid=(s // t,),
        in_specs=[pl.BlockSpec((t, DIL_W), lambda i: (i, c0 // 2)), pl.BlockSpec((t, DIL_W), lambda i: (i, c0 // 2 + 1)),
                  _vec_spec(LANES), _vec_spec(LANES), _row_spec(t, LANES), _row_spec(t, LANES)],
        out_specs=[_row_spec(t, DIL_W), _row_spec(t, DIL_W)],
        out_shape=[jax.ShapeDtypeStruct((s, DIL_W), F32)] * 2, compiler_params=_cparams("parallel"))(proj, proj, gq, gk, cos, sin)


def _dil_prep_bwd(proj, gq, gk, cos, sin, dqs, dks, dvs, *, name):
    s = proj.shape[0]
    t = _rows(s, TE)
    c0 = 3 * SB_W // LANES

    def body(q_ref, k_ref, gq_ref, gk_ref, cos_ref, sin_ref, a0, a1, a2, b0, b1, b2, c0_ref, c1_ref, c2_ref,
             dq_ref, dk_ref, dv_ref, dgq_ref, dgk_ref):
        @pl.when(pl.program_id(0) == 0)
        def _():
            dgq_ref[...] = jnp.zeros_like(dgq_ref)
            dgk_ref[...] = jnp.zeros_like(dgk_ref)

        dv_ref[...] = c0_ref[...] + c1_ref[...] + c2_ref[...]
        seg = _seg_consts()
        lane = _iota((t, LANES), 1)
        cs, sn = cos_ref[...], sin_ref[...]
        for x_ref, g_ref, parts, o_ref, dg_ref, mul in ((q_ref, gq_ref, (a0, a1, a2), dq_ref, dgq_ref, HEAD_DIM ** -0.5),
                                                          (k_ref, gk_ref, (b0, b1, b2), dk_ref, dgk_ref, 1.0)):
            for j in range(2):
                sl = slice(j * LANES, (j + 1) * LANES)
                dout = (parts[0][:, sl] + parts[1][:, sl] + parts[2][:, sl]) * mul
                dxn = dout * cs + _rot_half(dout * sn, lane)
                xv = x_ref[:, sl]
                r = lax.rsqrt(_xdot(xv * xv, seg, 3) * (1.0 / HEAD_DIM) + EPS)
                xh = xv * r
                dg_ref[...] += jnp.sum(dxn * xh, axis=0, keepdims=True)
                dxh = dxn * g_ref[...]
                o_ref[:, sl] = r * (dxh - xh * (_xdot(dxh * xh, seg, 3) * (1.0 / HEAD_DIM)))

    rs = _row_spec(t, DIL_W)
    return _pcall(
        body, name=name, grid=(s // t,),
        in_specs=[pl.BlockSpec((t, DIL_W), lambda i: (i, c0 // 2)), pl.BlockSpec((t, DIL_W), lambda i: (i, c0 // 2 + 1)),
                  _vec_spec(LANES), _vec_spec(LANES), _row_spec(t, LANES), _row_spec(t, LANES)] + [rs] * 9,
        out_specs=[rs, rs, rs, _vec_spec(LANES), _vec_spec(LANES)],
        out_shape=[jax.ShapeDtypeStruct((s, DIL_W), F32)] * 3 + [jax.ShapeDtypeStruct((1, LANES), F32)] * 2,
        compiler_params=_cparams("arbitrary"))(proj, proj, gq, gk, cos, sin, *dqs, *dks, *dvs)


def _dil_masks(n):
    row = _iota((QBLK, 2 * LANES), 0)
    col = _iota((QBLK, 2 * LANES), 1)
    return ((col < LANES) & (col >= row) & (n > 0)) | ((col >= LANES) & (col - LANES <= row))


DIL_V0 = (3 * SB_W + 2 * DIL_W) // LANES
DIL_DO0 = SB_W // LANES


def _dil_tiles(s, r):
    span = QBLK * r
    nsub = max(1, DIL_TILE // span)
    while s % (nsub * span):
        nsub -= 1
    return span, nsub


def _dil_rows(j, rho, span, r):
    return pl.ds(j * span + rho, QBLK, stride=r) if r > 1 else pl.ds(j * span, QBLK)


def _dil_fwd(q, k, proj, r, *, name):
    s = q.shape[0]
    span, nsub = _dil_tiles(s, r)
    tr = nsub * span

    def body(q_ref, kc_ref, kp_ref, vc_ref, vp_ref, num_ref, den_ref, mx_ref):
        n = pl.program_id(1)
        lane = _iota((QBLK, LANES), 1)
        h0 = lane < HEAD_DIM
        ones = jnp.ones((2 * QBLK, LANES), BF16)
        for j in range(nsub):
            valid = _dil_masks(n if j == 0 else 1)
            for rho in range(r):
                rows = _dil_rows(j, rho, span, r)
                before = _dil_rows(max(j - 1, 0), rho, span, r)
                k_prev, v_prev = (kp_ref, vp_ref) if j == 0 else (kc_ref, vc_ref)
                qv = q_ref[rows, :].astype(BF16)
                kk = jnp.concatenate([k_prev[before, :], kc_ref[rows, :]], axis=0).astype(BF16)
                vv = jnp.concatenate([jnp.concatenate([v_prev[before, :], vc_ref[rows, :]], axis=0).astype(BF16), ones], axis=1)
                res = []
                for h in range(2):
                    qm = jnp.where(h0 if h == 0 else ~h0, qv, jnp.zeros_like(qv))
                    sc = jnp.where(valid, _dot(qm, kk, NT), NEG_BIG)
                    mx = jnp.max(sc, axis=1, keepdims=True)
                    nd = _dot(jnp.exp(sc - mx).astype(BF16), vv)
                    res.append((nd[:, :LANES], nd[:, LANES:], mx))
                num_ref[rows, :] = jnp.where(h0, res[0][0], res[1][0])
                den_ref[rows, :] = jnp.where(h0, res[0][1], res[1][1])
                mx_ref[rows, :] = jnp.where(h0, res[0][2], res[1][2])

    cur = lambda c0: pl.BlockSpec((tr, LANES), lambda p, n, c0=c0: (n, c0 + p))
    prev = lambda c0: pl.BlockSpec((span, LANES), lambda p, n, c0=c0: (jnp.maximum(n * nsub - 1, 0), c0 + p))
    return _pcall(
        body, name=name, grid=(2, s // tr), in_specs=[cur(0), cur(0), prev(0), cur(DIL_V0), prev(DIL_V0)],
        out_specs=[cur(0), cur(0), cur(0)], out_shape=[jax.ShapeDtypeStruct((s, DIL_W), F32)] * 3,
        compiler_params=_cparams("parallel", "arbitrary"))(q, k, k, proj, proj)


def _dil_bwd(q, k, proj, dmix, mall, zall, delta, r, *, name):
    s = q.shape[0]
    span, nsub = _dil_tiles(s, r)
    tr = nsub * span
    nbig = s // tr

    def body(q_ref, kc_ref, kp_ref, vc_ref, vp_ref, do_ref, m_ref, z_ref, dl_ref, dq_ref, dk_ref, dv_ref, pk, pv):
        n = pl.program_id(1)
        lane = _iota((QBLK, LANES), 1)
        h0 = lane < HEAD_DIM

        @pl.when(n == 0)
        def _():
            pk[...] = jnp.zeros_like(pk)
            pv[...] = jnp.zeros_like(pv)

        @pl.when(n < nbig)
        def _():
            dk_ref[...] = pk[...]
            dv_ref[...] = pv[...]
            for j in range(nsub):
                valid = _dil_masks(n if j == 0 else 1)
                for rho in range(r):
                    rows = _dil_rows(j, rho, span, r)
                    before = _dil_rows(max(j - 1, 0), rho, span, r)
                    k_prev, v_prev = (kp_ref, vp_ref) if j == 0 else (kc_ref, vc_ref)
                    qv = q_ref[rows, :].astype(BF16)
                    dob = do_ref[rows, :].astype(BF16)
                    zero = jnp.zeros_like(qv)
                    kk = jnp.concatenate([k_prev[before, :], kc_ref[rows, :]], axis=0).astype(BF16)
                    vv = jnp.concatenate([v_prev[before, :], vc_ref[rows, :]], axis=0).astype(BF16)
                    mall_v, z_v, dl_v = m_ref[rows, :], z_ref[rows, :], dl_ref[rows, :]
                    dq, dk, dv = None, None, None
                    for h in range(2):
                        hm = h0 if h == 0 else ~h0
                        qm = jnp.where(hm, qv, zero)
                        dom = jnp.where(hm, dob, zero)
                        c = h * HEAD_DIM
                        sc = jnp.where(valid, _dot(qm, kk, NT), NEG_BIG)
                        pr = jnp.exp(sc - mall_v[:, c:c + 1]) * (1.0 / z_v[:, c:c + 1])
                        ds = (pr * (_dot(dom, vv, NT) - dl_v[:, c:c + 1])).astype(BF16)
                        parts = (_dot(ds, jnp.where(jnp.concatenate([hm, hm], axis=0), kk, jnp.zeros_like(kk))),
                                 _dot(ds, qm, TN), _dot(pr.astype(BF16), dom, TN))
                        dq, dk, dv = parts if h == 0 else (dq + parts[0], dk + parts[1], dv + parts[2])
                    dq_ref[rows, :] = dq
                    pk[rows, :] = dk[QBLK:]
                    pv[rows, :] = dv[QBLK:]
                    if j == 0:
                        last_span = _dil_rows(nsub - 1, rho, span, r)
                        dk_ref[last_span, :] += dk[:QBLK]
                        dv_ref[last_span, :] += dv[:QBLK]
                    else:
                        pk[before, :] += dk[:QBLK]
                        pv[before, :] += dv[:QBLK]

        @pl.when(n == nbig)
        def _():
            dk_ref[...] = pk[...]
            dv_ref[...] = pv[...]

    last = nbig - 1
    cur = lambda c0: pl.BlockSpec((tr, LANES), lambda p, n, c0=c0: (jnp.minimum(n, last), c0 + p))
    prev = lambda c0: pl.BlockSpec((span, LANES), lambda p, n, c0=c0: (jnp.maximum(jnp.minimum(n, last) * nsub - 1, 0), c0 + p))
    late = pl.BlockSpec((tr, LANES), lambda p, n: (jnp.maximum(n - 1, 0), p))
    return _pcall(
        body, name=name, grid=(2, nbig + 1),
        in_specs=[cur(0), cur(0), prev(0), cur(DIL_V0), prev(DIL_V0), cur(DIL_DO0), cur(0), cur(0), cur(0)],
        out_specs=[cur(0), late, late], out_shape=[jax.ShapeDtypeStruct((s, DIL_W), F32)] * 3,
        scratch_shapes=[pltpu.VMEM((tr, LANES), F32)] * 2,
        compiler_params=_cparams("parallel", "arbitrary"))(q, k, k, proj, proj, dmix, mall, zall, delta)


HG_SHIFT = HG_BLK.bit_length() - 1
HG_Q0, HG_F0, HG_I0 = (3 * SB_W + 3 * DIL_W) // HG_D, (3 * SB_W + 3 * DIL_W + HG_W) // HG_D, (3 * SB_W + 3 * DIL_W + 2 * HG_W) // HG_D


def _hg_scan(x):
    t = x.shape[0]
    half = HG_BLK // 2
    rb = _iota((t, LANES), 0) & (HG_BLK - 1)
    rh = rb & (half - 1)
    p = x
    for s in (1, 2, 4):
        p = p + jnp.where(rh >= s, pltpu.roll(p, s, 0), 0.0)
    h = jnp.where(rh == half - 1, p, 0.0)
    for s in (1, 2, 4):
        h = h + jnp.where(rh + s < half, pltpu.roll(h, t - s, 0), 0.0)
    first = rb < half
    pref = jnp.where(first, p, p + pltpu.roll(h, half, 0))
    total = h + jnp.where(first, pltpu.roll(h, t - half, 0), pltpu.roll(h, half, 0))
    return p, h, pref, total, first


def _hg_same(t):
    i = jnp.arange(t) >> HG_SHIFT
    return (i[:, None] == i[None, :]).astype(F32)


def _hg_own(t):
    i = jnp.arange(t) >> HG_SHIFT
    j = jnp.arange(t // HG_BLK * HG_D) // HG_D
    return (i[:, None] == j[None, :]).astype(BF16)


def _hg_diag(x, nb):
    return jnp.concatenate([x[b * HG_BLK:(b + 1) * HG_BLK, b * HG_D:(b + 1) * HG_D] for b in range(nb)], axis=0)


def _hg_inputs(qh, z, v, la, lc, t):
    lsg = jnp.minimum(z, 0.0) - jnp.log(1.0 + jnp.exp(-jnp.abs(z)))
    b = lc + lsg
    lf = jnp.maximum(la, b) + jnp.log(1.0 + jnp.exp(-jnp.abs(la - b)))
    f = jnp.exp(lf)
    sq = _sigmoid(qh)
    p, h, g, gl, first = _hg_scan(lf)
    k = 1.0 - f
    qs = qh * sq
    eq = jnp.where(first, 0.0, jnp.exp(jnp.minimum(p, 0.0)))
    ek = jnp.where(first, jnp.exp(jnp.minimum(h - p, 0.0)), 0.0)
    return dict(lf=lf, b=b, f=f, k=k, sq=sq, qs=qs, g=g, eg=jnp.exp(g), egl=jnp.exp(gl - g), dec=jnp.exp(gl),
                eq=eq, ek=ek, qx=(qs * eq).astype(BF16), kx=(k * ek).astype(BF16))


def _hgrn_fwd(proj, la, lc, *, name, side=None):
    s = proj.shape[0]
    t = _rows(s, HG_TILE)
    nt, nb = s // t, t // HG_BLK

    def body(q_ref, f_ref, i_ref, la_ref, lc_ref, same_ref, own_ref, o_ref, st_ref, state):
        @pl.when(pl.program_id(1) == 0)
        def _():
            state[...] = jnp.zeros_like(state)

        v = i_ref[...]
        a = _hg_inputs(q_ref[...], f_ref[...], v, la_ref[...], lc_ref[...], t)
        qs, k = a["qs"], a["k"]
        vb = v.astype(BF16)
        rb = _iota((t, LANES), 0) & (HG_BLK // 2 - 1)
        o = jnp.sum(qs * k, axis=1, keepdims=True) * v
        e = None
        for d in range(1, HG_BLK // 2):
            m = rb >= d
            fr = a["f"] if d == 1 else pltpu.roll(a["f"], d - 1, 0)
            e = fr if e is None else e * fr
            cd = jnp.sum(qs * pltpu.roll(k, d, 0) * e, axis=1, keepdims=True)
            o = o + jnp.where(m, cd, 0.0) * pltpu.roll(v, d, 0)
        cross = _dot(a["qx"], a["kx"], NT) * same_ref[...]
        o = o + _dot(cross.astype(BF16), vb)
        qt = (qs * a["eg"]).astype(BF16)
        kt = (k * a["egl"]).astype(BF16)
        upd = _dot(vb, jnp.tile(kt, (1, nb)) * own_ref[...], TN)
        st = state[...]
        for blk in range(nb):
            st_ref[blk * HG_D:(blk + 1) * HG_D, :] = st.astype(BF16)
            st = a["dec"][blk * HG_BLK:blk * HG_BLK + 1] * st + upd[:, blk * HG_D:(blk + 1) * HG_D]
        state[...] = st
        o_ref[...] = o + _hg_diag(_dot(qt, st_ref[...], NT), nb)

    col = lambda c0: pl.BlockSpec((t, HG_D), lambda hd, i, c0=c0: (i, c0 + hd))
    vec = pl.BlockSpec((1, HG_D), lambda hd, i: (0, hd))
    return _call_with_exchange(
        body, side, lambda: (pl.program_id(0) == 0) & (pl.program_id(1) == 0),
        lambda: (pl.program_id(0) == 3) & (pl.program_id(1) == nt - 1),
        (proj, proj, proj, la, lc, _hg_same(t), _hg_own(t)), name=name, grid=(4, nt),
        in_specs=[col(HG_Q0), col(HG_F0), col(HG_I0), vec, vec, pl.BlockSpec((t, t), lambda hd, i: (0, 0)),
                  pl.BlockSpec((t, nb * HG_D), lambda hd, i: (0, 0))],
        out_specs=[col(0), pl.BlockSpec((None, nb * HG_D, HG_D), lambda hd, i: (hd, i, 0))],
        out_shape=[jax.ShapeDtypeStruct((s, HG_W), F32), jax.ShapeDtypeStruct((4, s // HG_BLK * HG_D, HG_D), BF16)],
        scratch_shapes=[pltpu.VMEM((HG_D, HG_D), F32)],
        compiler_params=_cparams("arbitrary", "arbitrary"))


def _hgrn_bwd(proj, la, lc, st, doh, *, name, side=None):
    s = proj.shape[0]
    t = _rows(s, HG_TILE)
    nt, nb = s // t, t // HG_BLK

    def body(q_ref, f_ref, i_ref, la_ref, lc_ref, st_ref, do_ref, same_ref, own_ref, dq_ref, df_ref, di_ref, dla_ref, dlc_ref,
             dstate, dsb):
        @pl.when(pl.program_id(1) == 0)
        def _():
            dstate[...] = jnp.zeros_like(dstate)
            dla_ref[...] = jnp.zeros_like(dla_ref)
            dlc_ref[...] = jnp.zeros_like(dlc_ref)

        qh, z, v, do = q_ref[...], f_ref[...], i_ref[...], do_ref[...]
        la = la_ref[...]
        a = _hg_inputs(qh, z, v, la, lc_ref[...], t)
        qs, k, g = a["qs"], a["k"], a["g"]
        vb = v.astype(BF16)
        dob = do.astype(BF16)
        rb = _iota((t, LANES), 0) & (HG_BLK // 2 - 1)
        dc0 = jnp.sum(do * v, axis=1, keepdims=True)
        dq = dc0 * k
        dk = dc0 * qs
        dv = jnp.sum(qs * k, axis=1, keepdims=True) * do
        e = None
        for d in range(1, HG_BLK // 2):
            m = rb >= d
            fr = a["f"] if d == 1 else pltpu.roll(a["f"], d - 1, 0)
            e = fr if e is None else e * fr
            ks = pltpu.roll(k, d, 0)
            qe = qs * e
            cd = jnp.where(m, jnp.sum(qe * ks, axis=1, keepdims=True), 0.0)
            dcd = jnp.where(m, jnp.sum(do * pltpu.roll(v, d, 0), axis=1, keepdims=True), 0.0)
            dq = dq + dcd * ks * e
            dk = dk + pltpu.roll(dcd * qe, t - d, 0)
            dv = dv + pltpu.roll(cd * do, t - d, 0)
        same = same_ref[...]
        cross = (_dot(a["qx"], a["kx"], NT) * same).astype(BF16)
        dcross = (_dot(dob, vb, NT) * same).astype(BF16)
        dq = dq + _dot(dcross, a["kx"]) * a["eq"]
        dk = dk + _dot(dcross, a["qx"], TN) * a["ek"]
        dv = dv + _dot(cross, dob, TN)
        qt = (qs * a["eg"]).astype(BF16)
        kt = (k * a["egl"]).astype(BF16)
        own = own_ref[...]
        upd = _dot(dob, jnp.tile(qt, (1, nb)) * own, TN)
        ds = dstate[...]
        dgs = [None] * nb
        for blk in reversed(range(nb)):
            rows = slice(blk * HG_D, (blk + 1) * HG_D)
            dec = a["dec"][blk * HG_BLK:blk * HG_BLK + 1]
            dsb[rows, :] = ds.astype(BF16)
            dgs[blk] = jnp.broadcast_to(jnp.sum(ds * st_ref[rows, :].astype(F32), axis=0, keepdims=True) * dec, (HG_BLK, HG_D))
            ds = dec * ds + upd[:, rows]
        dstate[...] = ds
        dki = _dot(jnp.tile(vb, (1, nb)) * own, dsb[...]) * a["egl"]
        dq = dq + _dot(jnp.tile(dob, (1, nb)) * own, st_ref[...]) * a["eg"]
        dk = dk + dki
        dv = dv + _hg_diag(_dot(kt, dsb[...], NT), nb)
        x = qs * dq - k * dk
        _, _, xpre, xtot, _ = _hg_scan(x)
        _, _, _, ktot, _ = _hg_scan(k * dki)
        dlf = (xtot - xpre + x) + ktot + jnp.concatenate(dgs, axis=0) - a["f"] * dk
        wb = jnp.exp(a["b"] - a["lf"])
        wa = jnp.exp(la - a["lf"])
        sq = a["sq"]
        dq_ref[...] = dq * (sq * (1.0 + qh * (1.0 - sq)))
        df_ref[...] = dlf * wb * (1.0 - _sigmoid(z))
        di_ref[...] = dv
        dla_ref[...] += jnp.sum(dlf * wa, axis=0, keepdims=True)
        dlc_ref[...] += jnp.sum(dlf * wb, axis=0, keepdims=True)

    col = lambda c0: pl.BlockSpec((t, HG_D), lambda hd, i, c0=c0: (nt - 1 - i, c0 + hd))
    vec = pl.BlockSpec((1, HG_D), lambda hd, i: (0, hd))
    return _call_with_exchange(
        body, side, lambda: (pl.program_id(0) == 0) & (pl.program_id(1) == 0),
        lambda: (pl.program_id(0) == 3) & (pl.program_id(1) == nt - 1),
        (proj, proj, proj, la, lc, st, doh, _hg_same(t), _hg_own(t)), name=name, grid=(4, nt),
        in_specs=[col(HG_Q0), col(HG_F0), col(HG_I0), vec, vec,
                  pl.BlockSpec((None, nb * HG_D, HG_D), lambda hd, i: (hd, nt - 1 - i, 0)), col(0),
                  pl.BlockSpec((t, t), lambda hd, i: (0, 0)), pl.BlockSpec((t, nb * HG_D), lambda hd, i: (0, 0))],
        out_specs=[col(0), col(0), col(0), vec, vec],
        out_shape=[jax.ShapeDtypeStruct((s, HG_W), F32)] * 3 + [jax.ShapeDtypeStruct((1, HG_W), F32)] * 2,
        scratch_shapes=[pltpu.VMEM((HG_D, HG_D), F32), pltpu.VMEM((nb * HG_D, HG_D), BF16)],
        compiler_params=_cparams("arbitrary", "arbitrary"))


GH0 = (IN_W - HG_W) // HG_W


def _mix_out(o_a, nums, dens, mxs, oh, proj, hg, *, name):
    s = o_a.shape[0]
    t = _rows(s, TE)

    def body(oa_ref, n0, n1, n2, d0, d1, d2, m0, m1, m2, oh_ref, gh_ref, hg_ref, y_ref, od_ref, mall_ref, z_ref):
        y_ref[:, :SB_W] = oa_ref[...].astype(BF16)
        m = jnp.maximum(jnp.maximum(m0[...], m1[...]), m2[...])
        num = jnp.zeros((t, DIL_W), F32)
        z = jnp.zeros((t, DIL_W), F32)
        for n_ref, d_ref, m_ref in ((n0, d0, m0), (n1, d1, m1), (n2, d2, m2)):
            sc = jnp.exp(m_ref[...] - m)
            num = num + n_ref[...] * sc
            z = z + d_ref[...] * sc
        od = num / z
        od_ref[...] = od
        mall_ref[...] = m
        z_ref[...] = z
        y_ref[:, SB_W:SB_W + DIL_W] = od.astype(BF16)
        for h in range(4):
            sl = slice(h * HG_D, (h + 1) * HG_D)
            ov = oh_ref[:, sl]
            g = gh_ref[:, sl]
            r = lax.rsqrt(jnp.mean(ov * ov, axis=1, keepdims=True) + EPS)
            y_ref[:, SB_W + DIL_W + h * HG_D:SB_W + DIL_W + (h + 1) * HG_D] = (ov * r * hg_ref[...] * (g * _sigmoid(g))).astype(BF16)

    rd = _row_spec(t, DIL_W)
    return _pcall(
        body, name=name, grid=(s // t,),
        in_specs=[rd] * 10 + [_row_spec(t, HG_W), _row_spec(t, HG_W, GH0), _vec_spec(HG_D)],
        out_specs=[_row_spec(t, MIX_W), rd, rd, rd],
        out_shape=[jax.ShapeDtypeStruct((s, MIX_W), BF16)] + [jax.ShapeDtypeStruct((s, DIL_W), F32)] * 3,
        compiler_params=_cparams("parallel"))(o_a, *nums, *dens, *mxs, oh, proj, hg)


def _mix_out_bwd(dmix, oh, proj, hg, od, *, name):
    s = oh.shape[0]
    t = _rows(s, TE)

    def body(dm_ref, oh_ref, gh_ref, hg_ref, od_ref, doh_ref, dgh_ref, dl_ref, dhg_ref):
        @pl.when(pl.program_id(0) == 0)
        def _():
            dhg_ref[...] = jnp.zeros_like(dhg_ref)

        seg = _seg_consts()
        for j in range(2):
            sl = slice(j * LANES, (j + 1) * LANES)
            dl_ref[:, sl] = _xdot(dm_ref[:, SB_W + j * LANES:SB_W + (j + 1) * LANES] * od_ref[:, sl], seg, 3)
        hgv = hg_ref[...]
        for h in range(4):
            sl = slice(h * HG_D, (h + 1) * HG_D)
            dy = dm_ref[:, SB_W + DIL_W + h * HG_D:SB_W + DIL_W + (h + 1) * HG_D]
            ov = oh_ref[:, sl]
            g = gh_ref[:, sl]
            sg = _sigmoid(g)
            silu = g * sg
            r = lax.rsqrt(jnp.mean(ov * ov, axis=1, keepdims=True) + EPS)
            nrm = ov * r
            dhg_ref[...] += jnp.sum(dy * nrm * silu, axis=0, keepdims=True)
            dgh_ref[:, sl] = dy * nrm * hgv * (sg * (1.0 + g * (1.0 - sg)))
            dn = dy * hgv * silu
            doh_ref[:, sl] = r * (dn - nrm * jnp.mean(dn * nrm, axis=1, keepdims=True))

    rh = _row_spec(t, HG_W)
    return _pcall(
        body, name=name, grid=(s // t,),
        in_specs=[_row_spec(t, MIX_W), rh, _row_spec(t, HG_W, GH0), _vec_spec(HG_D), _row_spec(t, DIL_W)],
        out_specs=[rh, rh, _row_spec(t, DIL_W), _vec_spec(HG_D)],
        out_shape=[jax.ShapeDtypeStruct((s, HG_W), F32)] * 2 + [jax.ShapeDtypeStruct((s, DIL_W), F32), jax.ShapeDtypeStruct((1, HG_D), F32)],
        compiler_params=_cparams("arbitrary"))(dmix, oh, proj, hg, od)


def _lb_terms(l):
    l0, l1 = l[0:1], l[1:2]
    m = jnp.maximum(l0, l1)
    e0, e1 = jnp.exp(l0 - m), jnp.exp(l1 - m)
    s0, s1 = e0 / (e0 + e1), e1 / (e0 + e1)
    args = (s0 - s0, (s0 + s1) - s0)
    lbs = tuple(jnp.minimum(jnp.maximum(a, 0.0), 1.0 - EPS) for a in args)
    return s0, s1, args, lbs


def _lb_prep(logits, *, name):
    def body(l_ref, lb_ref, la_ref, lc_ref):
        _, _, _, lbs = _lb_terms(l_ref[...])
        lb = jnp.concatenate(lbs, axis=0)
        lb_ref[...] = lb
        la_ref[...] = jnp.log(jnp.maximum(lb, LB_FLOOR))
        lc_ref[...] = jnp.log1p(-lb)

    return _pcall(body, name=name, out_shape=[jax.ShapeDtypeStruct(logits.shape, F32)] * 3)(logits)


def _lb_bwd(logits, dla, dlc, *, name):
    def half(hi, eq):
        return jnp.where(hi, 1.0, jnp.where(eq, 0.5, 0.0))

    def body(l_ref, dla_ref, dlc_ref, o_ref):
        s0, s1, args, lbs = _lb_terms(l_ref[...])
        da = []
        for i in range(2):
            a, lb = args[i], lbs[i]
            dlb = dla_ref[i:i + 1] * half(lb > LB_FLOOR, lb == LB_FLOOR) / jnp.maximum(lb, LB_FLOOR) - dlc_ref[i:i + 1] / (1.0 - lb)
            t = jnp.maximum(a, 0.0)
            da.append(dlb * half(a > 0.0, a == 0.0) * half(t < 1.0 - EPS, t == 1.0 - EPS))
        ds0 = (da[0] + da[1]) - (da[0] + da[1])
        ds1 = da[1]
        dot = s0 * ds0 + s1 * ds1
        o_ref[...] = jnp.concatenate([s0 * (ds0 - dot), s1 * (ds1 - dot)], axis=0)

    return _pcall(body, name=name, out_shape=jax.ShapeDtypeStruct(logits.shape, F32))(logits, dla, dlc)


def _mod_fwd(c8, w, b, *, name):
    _, d, n = w.shape
    tn = _tile(n, 768)

    def body(c_ref, w_ref, b_ref, o_ref):
        cv = c_ref[...]
        o_ref[...] = _dot((cv * _sigmoid(cv)).astype(BF16), w_ref[...].astype(BF16)) + b_ref[...]

    return _pcall(
        body, name=name, grid=(2, n // tn),
        in_specs=[pl.BlockSpec((8, d), lambda l, j: (0, 0)), pl.BlockSpec((None, d, tn), lambda l, j: (l, 0, j)),
                  pl.BlockSpec((None, 1, tn), lambda l, j: (l, 0, j))],
        out_specs=pl.BlockSpec((None, 8, tn), lambda l, j: (l, 0, j)),
        out_shape=jax.ShapeDtypeStruct((2, 8, n), F32), compiler_params=_cparams("parallel", "parallel"))(c8, w, b)


def _mod_bwd(ct, dm, *, name):
    d = ct.shape[0]
    n = dm.shape[2]
    tn = _tile(n, 768)

    def body(c_ref, dm_ref, o_ref):
        cv = c_ref[...]
        sc = cv * _sigmoid(cv)
        dv = dm_ref[...]
        acc = sc[:, 0:1] * dv[0:1, :]
        for b in range(1, 8):
            acc = acc + sc[:, b:b + 1] * dv[b:b + 1, :]
        o_ref[...] = acc

    return _pcall(
        body, name=name, grid=(2, n // tn),
        in_specs=[pl.BlockSpec((d, 8), lambda l, j: (0, 0)), pl.BlockSpec((None, 8, tn), lambda l, j: (l, 0, j))],
        out_specs=pl.BlockSpec((None, d, tn), lambda l, j: (l, 0, j)),
        out_shape=jax.ShapeDtypeStruct((2, d, n), F32), compiler_params=_cparams("parallel", "parallel"))(ct, dm)


_PEERS = {
    "chips": ((1, 0, 0), (0, 1, 0), (1, 1, 0)),
    "all": tuple((a, b, c) for a in (0, 1) for b in (0, 1) for c in (0, 1) if a + b + c),
    "sib": ((0, 0, 1),),
}
_SLOTS = {"chips": 4, "all": 8, "sib": 2}


def _slot(kind, x, y, c):
    return {"chips": 2 * x + y, "all": 4 * x + 2 * y + c, "sib": c}[kind]


class _Xchg:
    def __init__(self, arrs, kind, scatter):
        self.arrs, self.kind, self.scatter = list(arrs), kind, scatter
        self.n = len(self.arrs)
        self.peers = _PEERS[kind]
        self.chunks = [self._pieces(a) for a in self.arrs]
        npeer = len(self.peers)
        self.base = [sum(len(c) for c in self.chunks[:a]) * npeer for a in range(self.n)]
        total = sum(len(c) for c in self.chunks) * npeer
        self.specs = [pl.BlockSpec(memory_space=pl.ANY)] * self.n
        self.out_shape = [jax.ShapeDtypeStruct(a.shape if scatter else (_SLOTS[kind],) + a.shape, a.dtype) for a in self.arrs]
        self.scratch = [pltpu.SemaphoreType.DMA((total,)), pltpu.SemaphoreType.DMA((total,)), pltpu.SemaphoreType.DMA((self.n,))]

    def _pieces(self, a):
        shape = a.shape[1:] if self.scatter else a.shape
        if len(shape) == 2:
            for k in (XCHG_CHUNKS, XCHG_CHUNKS // 2, XCHG_CHUNKS // 4):
                if k > 1 and shape[0] % (16 * k) == 0 and shape[0] * shape[1] * a.dtype.itemsize >= k * XCHG_MIN_BYTES:
                    return [(i * (shape[0] // k), shape[0] // k) for i in range(k)]
        return [None]

    def copies(self, ins, outs, send, recv, loc):
        kind, scatter = self.kind, self.scatter
        x, y, c = lax.axis_index("x"), lax.axis_index("y"), lax.axis_index("c")
        me = _slot(kind, x, y, c)
        out = []
        for a in range(self.n):
            out.append(pltpu.make_async_copy(ins[a].at[me] if scatter else ins[a], outs[a].at[me], loc.at[a]))
            for j, (dx, dy, dc) in enumerate(self.peers):
                px, py, pc = (1 - x if dx else x), (1 - y if dy else y), (1 - c if dc else c)
                src = ins[a].at[_slot(kind, px, py, pc)] if scatter else ins[a]
                for i, piece in enumerate(self.chunks[a]):
                    rows = slice(None) if piece is None else pl.ds(piece[0], piece[1])
                    sem = self.base[a] + j * len(self.chunks[a]) + i
                    out.append(pltpu.make_async_remote_copy(
                        src_ref=src if piece is None else src.at[rows], dst_ref=outs[a].at[me] if piece is None else outs[a].at[me, rows],
                        send_sem=send.at[sem], recv_sem=recv.at[sem], device_id=(px, py, pc), device_id_type=MESH_ID))
        return out


def _exchange(arrs, kind, scatter, *, name):
    xc = _Xchg(arrs, kind, scatter)

    def body(*refs):
        copies = xc.copies(refs[:xc.n], refs[xc.n:2 * xc.n], *refs[2 * xc.n:])
        for cp in copies:
            cp.start()
        for cp in copies:
            cp.wait()

    return _pcall(body, name=name, in_specs=xc.specs, out_specs=xc.specs, out_shape=xc.out_shape, scratch_shapes=xc.scratch)(*arrs)


def _with_exchange(body, n_in, n_out, n_scr, xc, first, last):
    def wrapped(*refs):
        ins, side_in = refs[:n_in], refs[n_in:n_in + xc.n]
        o0 = n_in + xc.n
        outs, side_out = refs[o0:o0 + n_out], refs[o0 + n_out:o0 + n_out + xc.n]
        s0 = o0 + n_out + xc.n
        scr, sems = refs[s0:s0 + n_scr], refs[s0 + n_scr:]

        @pl.when(first())
        def _():
            for cp in xc.copies(side_in, side_out, *sems):
                cp.start()

        body(*ins, *outs, *scr)

        @pl.when(last())
        def _():
            for cp in xc.copies(side_in, side_out, *sems):
                cp.wait()

    return wrapped


def _call_with_exchange(body, xc, first, last, args, *, in_specs, out_specs, out_shape, scratch_shapes, **kw):
    if xc is None:
        return _pcall(body, in_specs=in_specs, out_specs=out_specs, out_shape=out_shape, scratch_shapes=scratch_shapes, **kw)(*args), None
    wrapped = _with_exchange(body, len(in_specs), len(out_specs), len(scratch_shapes), xc, first, last)
    res = _pcall(wrapped, in_specs=list(in_specs) + xc.specs, out_specs=list(out_specs) + xc.specs,
                 out_shape=list(out_shape) + xc.out_shape, scratch_shapes=list(scratch_shapes) + xc.scratch, **kw)(*args, *xc.arrs)
    return res[:len(out_specs)], res[len(out_specs):]


def _sum_slots(a, *, name, out_dtype=F32):
    ns, r, c = a.shape
    t = _tile(r, max(16, (1 << 18) // c // 16 * 16), 16)

    def body(a_ref, o_ref):
        acc = a_ref[0].astype(F32)
        for i in range(1, ns):
            acc = acc + a_ref[i].astype(F32)
        o_ref[...] = acc.astype(o_ref.dtype)

    return _pcall(body, name=name, grid=(r // t,), in_specs=[pl.BlockSpec((ns, t, c), lambda i: (0, i, 0))],
                  out_specs=pl.BlockSpec((t, c), lambda i: (i, 0)), out_shape=jax.ShapeDtypeStruct((r, c), out_dtype),
                  compiler_params=_cparams("parallel"))(a)


def _adamw(w, gparts, m, v, *, name):
    r, c = w.shape
    t = _tile(r, max(16, (1 << 17) // c // 16 * 16), 16)
    ng = len(gparts)

    def body(*refs):
        w_ref, m_ref, v_ref = refs[0], refs[1 + ng], refs[2 + ng]
        g_ref, d_ref, nm_ref, nv_ref = refs[3 + ng:]
        g = refs[1][...].astype(F32)
        for i in range(1, ng):
            g = g + refs[1 + i][...].astype(F32)
        mn = ADAM_B1 * m_ref[...] + (1.0 - ADAM_B1) * g
        vn = ADAM_B2 * v_ref[...] + (1.0 - ADAM_B2) * (g * g)
        m_hat = mn / (1.0 - ADAM_B1 ** ADAM_STEP)
        v_hat = vn / (1.0 - ADAM_B2 ** ADAM_STEP)
        g_ref[...] = g
        d_ref[...] = -ADAM_LR * (m_hat / (jnp.sqrt(v_hat) + ADAM_EPS) + ADAM_WD * w_ref[...])
        nm_ref[...] = mn
        nv_ref[...] = vn

    spec = pl.BlockSpec((t, c), lambda i: (i, 0))
    return _pcall(body, name=name, grid=(r // t,), in_specs=[spec] * (3 + ng), out_specs=[spec] * 4,
                  out_shape=[jax.ShapeDtypeStruct((r, c), F32)] * 4, compiler_params=_cparams("parallel"))(w, *gparts, m, v)


def _adamw_layers(w, halves, m, v, *, name):
    r, c = w.shape
    nl = len(halves)
    rl = r // nl
    t = _tile(rl, max(16, (1 << 17) // c // 16 * 16), 16)
    nbl = rl // t

    def body(*refs):
        w_ref, m_ref, v_ref = refs[0], refs[1 + 2 * nl], refs[2 + 2 * nl]
        g_ref, d_ref, nm_ref, nv_ref = refs[3 + 2 * nl:]
        g = None
        for l in range(nl):
            gl = refs[1 + 2 * l][...].astype(F32) + refs[2 + 2 * l][...].astype(F32)
            g = gl if g is None else jnp.where(pl.program_id(0) >= l * nbl, gl, g)
        mn = ADAM_B1 * m_ref[...] + (1.0 - ADAM_B1) * g
        vn = ADAM_B2 * v_ref[...] + (1.0 - ADAM_B2) * (g * g)
        m_hat = mn / (1.0 - ADAM_B1 ** ADAM_STEP)
        v_hat = vn / (1.0 - ADAM_B2 ** ADAM_STEP)
        g_ref[...] = g
        d_ref[...] = -ADAM_LR * (m_hat / (jnp.sqrt(v_hat) + ADAM_EPS) + ADAM_WD * w_ref[...])
        nm_ref[...] = mn
        nv_ref[...] = vn

    spec = pl.BlockSpec((t, c), lambda i: (i, 0))
    part = lambda l, core: pl.BlockSpec((None, t, c), lambda i, l=l, core=core: (core, jnp.clip(i - l * nbl, 0, nbl - 1), 0))
    gspecs = [part(l, core) for l in range(nl) for core in range(2)]
    gargs = [halves[l] for l in range(nl) for _ in range(2)]
    return _pcall(body, name=name, grid=(r // t,), in_specs=[spec] + gspecs + [spec, spec], out_specs=[spec] * 4,
                  out_shape=[jax.ShapeDtypeStruct((r, c), F32)] * 4, compiler_params=_cparams("parallel"))(w, *gargs, m, v)


FFN_TM = 512
FFN_CHUNK = 1408


def _resident(shape):
    return pl.BlockSpec(shape, lambda i: (0,) * len(shape), pipeline_mode=pl.Buffered(1))


def _ffn_up(x, sc, sh, wgu, *, name):
    s, d = x.shape
    f = wgu.shape[1] // 2
    t, fc = _rows(s, FFN_TM), _tile(f, FFN_CHUNK)

    def body(x_ref, sc_ref, sh_ref, w_ref, h_ref, uv_ref, a_ref):
        xv = x_ref[...]
        r = lax.rsqrt(jnp.mean(xv * xv, axis=1, keepdims=True) + EPS)
        hb = ((xv * r) * (1.0 + sc_ref[...]) + sh_ref[...]).astype(BF16)
        h_ref[...] = hb
        for j in range(f // fc):
            u = _dot(hb, w_ref[:, j * fc:(j + 1) * fc])
            v = _dot(hb, w_ref[:, f + j * fc:f + (j + 1) * fc])
            sg = _sigmoid(u)
            silu = u * sg
            uv_ref[:, j * fc:(j + 1) * fc] = (v * (sg * (1.0 + u * (1.0 - sg)))).astype(BF16)
            uv_ref[:, f + j * fc:f + (j + 1) * fc] = silu.astype(BF16)
            a_ref[:, j * fc:(j + 1) * fc] = (silu * v).astype(BF16)

    return _pcall(
        body, name=name, grid=(s // t,), in_specs=[_row_spec(t, d), _vec_spec(d), _vec_spec(d), _resident(wgu.shape)],
        out_specs=[_row_spec(t, d), _row_spec(t, 2 * f), _row_spec(t, f)],
        out_shape=[jax.ShapeDtypeStruct((s, d), BF16), jax.ShapeDtypeStruct((s, 2 * f), BF16), jax.ShapeDtypeStruct((s, f), BF16)],
        compiler_params=_cparams("parallel"))(x, sc, sh, wgu)


def _norm_mm(x, sc, sh, w, nb, *, name):
    s, d = x.shape
    n = w.shape[1]
    t, nc = _rows(s, FFN_TM), _tile(n, 1792)
    assert nb <= nc

    def body(x_ref, sc_ref, sh_ref, w_ref, h_ref, o_ref, ob_ref):
        xv = x_ref[...]
        r = lax.rsqrt(jnp.mean(xv * xv, axis=1, keepdims=True) + EPS)
        hb = ((xv * r) * (1.0 + sc_ref[...]) + sh_ref[...]).astype(BF16)
        h_ref[...] = hb
        for j in range(n // nc):
            part = _dot(hb, w_ref[:, j * nc:(j + 1) * nc])
            o_ref[:, j * nc:(j + 1) * nc] = part
            if j == 0:
                ob_ref[...] = part[:, :nb].astype(BF16)

    return _pcall(
        body, name=name, grid=(s // t,), in_specs=[_row_spec(t, d), _vec_spec(d), _vec_spec(d), _resident(w.shape)],
        out_specs=[_row_spec(t, d), _row_spec(t, n), _row_spec(t, nb)],
        out_shape=[jax.ShapeDtypeStruct((s, d), BF16), jax.ShapeDtypeStruct((s, n), F32), jax.ShapeDtypeStruct((s, nb), BF16)],
        compiler_params=_cparams("parallel"))(x, sc, sh, w)


def _ffn_dact(dxo, y, sg, wd, uv, *, name):
    s, d = y.shape
    f = wd.shape[0]
    t, fc = _rows(s, FFN_TM), _tile(f, FFN_CHUNK)

    def body(dxo_ref, y_ref, sg_ref, w_ref, uv_ref, dy_ref, duv_ref, ds_ref):
        @pl.when(pl.program_id(0) == 0)
        def _():
            ds_ref[...] = jnp.zeros_like(ds_ref)

        dv = dxo_ref[...]
        dyb = (sg_ref[...] * dv).astype(BF16)
        dy_ref[...] = dyb
        ds_ref[...] += jnp.sum(dv * y_ref[...], axis=0, keepdims=True)
        for j in range(f // fc):
            da = _dot(dyb, w_ref[j * fc:(j + 1) * fc, :], NT)
            duv_ref[:, j * fc:(j + 1) * fc] = (da * uv_ref[:, j * fc:(j + 1) * fc].astype(F32)).astype(BF16)
            duv_ref[:, f + j * fc:f + (j + 1) * fc] = (da * uv_ref[:, f + j * fc:f + (j + 1) * fc].astype(F32)).astype(BF16)

    return _pcall(
        body, name=name, grid=(s // t,),
        in_specs=[_row_spec(t, d), _row_spec(t, d), _vec_spec(d), _resident(wd.shape), _row_spec(t, 2 * f)],
        out_specs=[_row_spec(t, d), _row_spec(t, 2 * f), _vec_spec(d)],
        out_shape=[jax.ShapeDtypeStruct((s, d), BF16), jax.ShapeDtypeStruct((s, 2 * f), BF16), jax.ShapeDtypeStruct((1, d), F32)],
        compiler_params=_cparams("arbitrary"))(dxo, y, sg, wd, uv)


def _ffn_dh(duv, wgu, x, sc, dxo, *, name):
    s, d = x.shape
    f2 = wgu.shape[1]
    t = _rows(s, FFN_TM)

    def body(duv_ref, w_ref, x_ref, sc_ref, dxo_ref, dx_ref, dsc_ref, dsh_ref):
        @pl.when(pl.program_id(0) == 0)
        def _():
            dsc_ref[...] = jnp.zeros_like(dsc_ref)
            dsh_ref[...] = jnp.zeros_like(dsh_ref)

        dhv = _dot(duv_ref[...], w_ref[...], NT)
        xv = x_ref[...]
        r = lax.rsqrt(jnp.mean(xv * xv, axis=1, keepdims=True) + EPS)
        xn = xv * r
        dxn = dhv * (1.0 + sc_ref[...])
        dx_ref[...] = dxo_ref[...] + r * (dxn - xn * jnp.mean(dxn * xn, axis=1, keepdims=True))
        dsc_ref[...] += jnp.sum(dhv * xn, axis=0, keepdims=True)
        dsh_ref[...] += jnp.sum(dhv, axis=0, keepdims=True)

    return _pcall(
        body, name=name, grid=(s // t,),
        in_specs=[_row_spec(t, f2), _resident(wgu.shape), _row_spec(t, d), _vec_spec(d), _row_spec(t, d)],
        out_specs=[_row_spec(t, d), _vec_spec(d), _vec_spec(d)],
        out_shape=[jax.ShapeDtypeStruct((s, d), F32), jax.ShapeDtypeStruct((1, d), F32), jax.ShapeDtypeStruct((1, d), F32)],
        compiler_params=_cparams("arbitrary"))(duv, wgu, x, sc, dxo)


def _dh_pieces(pieces, w, x, sc, dxo, *, name):
    s, d = x.shape
    t = _rows(s, FFN_TM)
    widths = [p.shape[1] for p in pieces]
    offs = [sum(widths[:i]) for i in range(len(widths))]
    kd = sum(widths)
    npc = len(pieces)

    def body(*refs):
        p_refs = refs[:npc]
        w_ref, x_ref, sc_ref, dxo_ref, dx_ref, dsc_ref, dsh_ref, cat_ref = refs[npc:]

        @pl.when(pl.program_id(0) == 0)
        def _():
            dsc_ref[...] = jnp.zeros_like(dsc_ref)
            dsh_ref[...] = jnp.zeros_like(dsh_ref)

        dhv = None
        for p_ref, off, wd in zip(p_refs, offs, widths):
            pb = p_ref[...].astype(BF16)
            cat_ref[:, off:off + wd] = pb
            part = _dot(pb, w_ref[:, off:off + wd], NT)
            dhv = part if dhv is None else dhv + part
        xv = x_ref[...]
        r = lax.rsqrt(jnp.mean(xv * xv, axis=1, keepdims=True) + EPS)
        xn = xv * r
        dxn = dhv * (1.0 + sc_ref[...])
        dx_ref[...] = dxo_ref[...] + r * (dxn - xn * jnp.mean(dxn * xn, axis=1, keepdims=True))
        dsc_ref[...] += jnp.sum(dhv * xn, axis=0, keepdims=True)
        dsh_ref[...] += jnp.sum(dhv, axis=0, keepdims=True)

    return _pcall(
        body, name=name, grid=(s // t,),
        in_specs=[_row_spec(t, wd) for wd in widths] + [_resident(w.shape), _row_spec(t, d), _vec_spec(d), _row_spec(t, d)],
        out_specs=[_row_spec(t, d), _vec_spec(d), _vec_spec(d), _row_spec(t, kd)],
        out_shape=[jax.ShapeDtypeStruct((s, d), F32), jax.ShapeDtypeStruct((1, d), F32), jax.ShapeDtypeStruct((1, d), F32),
                   jax.ShapeDtypeStruct((s, kd), BF16)],
        compiler_params=_cparams("arbitrary"))(*pieces, w, x, sc, dxo)


def _ffn_fwd(x, sh, sc, g, wgu, wd, tag):
    h, uv, a = _ffn_up(x, sc, sh, wgu, name=f"{tag}_up")
    y, xo = _mm(a, wd, name=f"{tag}_down", tm=512, tn=1024, tk=wd.shape[0], res=x, scale=0.5 * g)
    return xo, (x, h, uv, a, y)


def _ffn_bwd(dxo, saved, sc, g, wgu, wd, tag):
    x, h, uv, a, y = saved
    dyb, duv, dgs = _ffn_dact(dxo, y, 0.5 * g, wd, uv, name=f"{tag}_dact")
    dx, dsc, dsh = _ffn_dh(duv, wgu, x, sc, dxo, name=f"{tag}_dh")
    dwgu = _mm_tn(h, duv, name=f"{tag}_dwgu", tm=1024, tn=1408, tk=512)
    dwd = _mm_tn(a, dyb, name=f"{tag}_dwd", tm=1408, tn=1024, tk=512)
    return dx, dwgu, dwd, dsh, dsc, 0.5 * dgs


def _layer_fwd(x0, mod, w, par, tag, carry=None):
    s = x0.shape[0]
    sh1, sc1, g1, sh2, sc2, g2, sh3, sc3, g3 = (mod[i:i + 1] for i in range(N_MOD))
    x1, f1 = _ffn_fwd(x0, sh1, sc1, g1, w["gu1"], w["d1"], f"{tag}_ffn1")
    h2, proj, qkv = _norm_mm(x1, sc2, sh2, w["in"], 3 * SB_W, name=f"{tag}_in")
    o_a, got_a = _sb_fwd(qkv, name=f"{tag}_sb", side=None if carry is None else _Xchg(carry[0], "chips", False))
    qd, kd = _dil_prep(proj, par["gq"], par["gk"], par["cos"], par["sin"], name=f"{tag}_dil_prep")
    nums, dens, mxs = [], [], []
    for _, r in DIL_PATTERNS:
        nu, de, mx = _dil_fwd(qd, kd, proj, r, name=f"{tag}_dil{r}")
        nums.append(nu)
        dens.append(de)
        mxs.append(mx)
    (oh, st), got_b = _hgrn_fwd(proj, par["la"], par["lc"], name=f"{tag}_hgrn",
                                side=None if carry is None else _Xchg(carry[1], "chips", False))
    ymix, od, mall, zall = _mix_out(o_a, nums, dens, mxs, oh, proj, par["hg"], name=f"{tag}_mix_out")
    out, x2 = _mm(ymix, w["out"], name=f"{tag}_out", tm=512, tn=1024, tk=1024, res=x1, scale=g2)
    x3, f2 = _ffn_fwd(x2, sh3, sc3, g3, w["gu2"], w["d2"], f"{tag}_ffn2")
    return x3, dict(f1=f1, f2=f2, x1=x1, h2=h2, proj=proj, qkv=qkv, o_a=o_a, qd=qd, kd=kd, oh=oh, st=st,
                    ymix=ymix, od=od, mall=mall, zall=zall, out=out), (None if carry is None else (got_a, got_b))


def _layer_bwd(dx3, sv, mod, w, par, tag, pending=None):
    sh1, sc1, g1, sh2, sc2, g2, sh3, sc3, g3 = (mod[i:i + 1] for i in range(N_MOD))
    dx2, dwgu2, dwd2, dsh3, dsc3, dg3 = _ffn_bwd(dx3, sv["f2"], sc3, g3, w["gu2"], w["d2"], f"{tag}_ffn2")
    doutb, dg2 = _gate_bwd(dx2, sv["out"], g2, name=f"{tag}_dgate2")
    dmix = _mm(doutb, w["out"], name=f"{tag}_dmix", tb=True, tm=512, tn=1024, tk=1024)
    dwout = _mm_tn(sv["ymix"], doutb, name=f"{tag}_dwout", tm=1024, tn=1024, tk=512)
    proj = sv["proj"]
    doh, dgh, delta, dhg = _mix_out_bwd(dmix, sv["oh"], proj, par["hg"], sv["od"], name=f"{tag}_dmix_out")
    (dqa, dka, dva), got = _sb_bwd(sv["qkv"], sv["o_a"], dmix, name=f"{tag}_dsb",
                                   side=None if pending is None else _Xchg(pending, "chips", True))
    parts = None if got is None else [_sum_slots(g, name=f"{tag}_sum{i}", out_dtype=BF16) for i, g in enumerate(got)]
    dqs, dks, dvs = [], [], []
    for _, r in DIL_PATTERNS:
        a, b, c = _dil_bwd(sv["qd"], sv["kd"], proj, dmix, sv["mall"], sv["zall"], delta, r, name=f"{tag}_ddil{r}")
        dqs.append(a)
        dks.append(b)
        dvs.append(c)
    dqd, dkd, dvd, dgq, dgk = _dil_prep_bwd(proj, par["gq"], par["gk"], par["cos"], par["sin"], dqs, dks, dvs,
                                             name=f"{tag}_ddil_prep")
    (dqh, dfh, dih, dla, dlc), swapped = _hgrn_bwd(proj, par["la"], par["lc"], sv["st"], doh, name=f"{tag}_dhgrn",
                                                   side=None if parts is None else _Xchg(parts, "sib", False))
    dx1, dsc2, dsh2, dproj = _dh_pieces([dqa, dka, dva, dqd, dkd, dvd, dqh, dfh, dih, dgh], w["in"], sv["x1"], sc2, dx2,
                                         name=f"{tag}_dh2")
    dwin = _mm_tn(sv["h2"], dproj, name=f"{tag}_dwin", tm=1024, tn=1792, tk=512)
    dx0, dwgu1, dwd1, dsh1, dsc1, dg1 = _ffn_bwd(dx1, sv["f1"], sc1, g1, w["gu1"], w["d1"], f"{tag}_ffn1")
    dmod = jnp.concatenate([dsh1, dsc1, dg1, dsh2, dsc2, dg2, dsh3, dsc3, dg3], axis=0)
    fold = lambda v: v.reshape(2, HEAD_DIM).sum(axis=0)
    grads = dict(gu1=dwgu1, d1=dwd1, gu2=dwgu2, d2=dwd2, win=dwin, wout=dwout, dmod=dmod, gq=fold(dgq), gk=fold(dgk),
                 hg=dhg[0], la=dla[0], lc=dlc[0])
    return dx0, grads, swapped


def _pack(pieces):
    flat = jnp.concatenate([p.reshape(-1) for p in pieces])
    pad = (-flat.shape[0]) % (8 * LANES)
    return jnp.pad(flat, (0, pad)).reshape(-1, LANES)


def _unpack(flat, like):
    out, off = [], 0
    for p in like:
        out.append(flat[off:off + p.size].reshape(p.shape))
        off += p.size
    return out


def kernel(x, c, w_mod, b_mod, ffn1_w_gate, ffn1_w_up, ffn1_w_down, w_in, w_out, q_norm_g, k_norm_g, hgrn_norm_g, hgrn_lb_logits, ffn2_w_gate, ffn2_w_up, ffn2_w_down, loss_target, m_w_mod, m_b_mod, m_ffn1_w_gate, m_ffn1_w_up, m_ffn1_w_down, m_w_in, m_w_out, m_q_norm_g, m_k_norm_g, m_hgrn_norm_g, m_hgrn_lb_logits, m_ffn2_w_gate, m_ffn2_w_up, m_ffn2_w_down, v_w_mod, v_b_mod, v_ffn1_w_gate, v_ffn1_w_up, v_ffn1_w_down, v_w_in, v_w_out, v_q_norm_g, v_k_norm_g, v_hgrn_norm_g, v_hgrn_lb_logits, v_ffn2_w_gate, v_ffn2_w_up, v_ffn2_w_down):
    names = ["w_mod", "b_mod", "ffn1_w_gate", "ffn1_w_up", "ffn1_w_down", "w_in", "w_out", "q_norm_g", "k_norm_g",
             "hgrn_norm_g", "hgrn_lb_logits", "ffn2_w_gate", "ffn2_w_up", "ffn2_w_down"]
    wts = dict(zip(names, (w_mod, b_mod, ffn1_w_gate, ffn1_w_up, ffn1_w_down, w_in, w_out, q_norm_g, k_norm_g, hgrn_norm_g,
                           hgrn_lb_logits, ffn2_w_gate, ffn2_w_up, ffn2_w_down)))
    mom = dict(zip(names, (m_w_mod, m_b_mod, m_ffn1_w_gate, m_ffn1_w_up, m_ffn1_w_down, m_w_in, m_w_out, m_q_norm_g, m_k_norm_g,
                           m_hgrn_norm_g, m_hgrn_lb_logits, m_ffn2_w_gate, m_ffn2_w_up, m_ffn2_w_down)))
    var = dict(zip(names, (v_w_mod, v_b_mod, v_ffn1_w_gate, v_ffn1_w_up, v_ffn1_w_down, v_w_in, v_w_out, v_q_norm_g, v_k_norm_g,
                           v_hgrn_norm_g, v_hgrn_lb_logits, v_ffn2_w_gate, v_ffn2_w_up, v_ffn2_w_down)))
    depth = w_mod.shape[0]
    assert depth == 2 and x.shape[0] == 1
    s, d = x.shape[1:]
    assert s % (DIL_PATTERNS[-1][1] * QBLK) == 0 and d % LANES == 0
    xi, yi, ci = lax.axis_index("x"), lax.axis_index("y"), lax.axis_index("c")
    chip = 2 * xi + yi
    dev = 2 * chip + ci
    x0, tgt = x[0], loss_target[0]

    c8 = _exchange([c.reshape(d // LANES, LANES)], "all", False, name="gather_c")[0].reshape(8, d)
    ncol = w_mod.shape[2]
    b_loc = lax.dynamic_slice_in_dim(b_mod, chip * ncol, ncol, axis=1)
    m_loc = _mod_fwd(c8, w_mod, b_loc.reshape(depth, 1, ncol), name="mod_fwd")
    m_all = _exchange([m_loc], "chips", False, name="gather_mod")[0]
    mod = jnp.transpose(lax.dynamic_index_in_dim(m_all, dev, axis=2, keepdims=False), (1, 0, 2)).reshape(depth, N_MOD, d)

    col_sharded = ["ffn1_w_gate", "ffn1_w_up", "w_in", "ffn2_w_gate", "ffn2_w_up"]
    row_sharded = ["ffn1_w_down", "w_out", "ffn2_w_down"]
    big = col_sharded + row_sharded
    early = ["ffn1_w_gate", "ffn1_w_up", "ffn1_w_down", "w_in", "w_out"]
    late = ["ffn2_w_gate", "ffn2_w_up", "ffn2_w_down"]

    def shards(l, group):
        return [wts[n][l].astype(BF16) for n in group]

    def whole(group, gathered):
        out = {}
        for n, g in zip(group, gathered):
            out[n] = jnp.moveaxis(g, 0, 1).reshape(g.shape[1], -1) if n in col_sharded else g.reshape(-1, g.shape[2])
        return out

    def layer_weights(full):
        return dict(gu1=jnp.concatenate([full["ffn1_w_gate"], full["ffn1_w_up"]], axis=1), d1=full["ffn1_w_down"],
                    gu2=jnp.concatenate([full["ffn2_w_gate"], full["ffn2_w_up"]], axis=1), d2=full["ffn2_w_down"],
                    **{"in": full["w_in"], "out": full["w_out"]})

    ws = [layer_weights(whole(early + late, _exchange(shards(0, early + late), "chips", False, name="gather_w")))]

    _, la, lc = _lb_prep(hgrn_lb_logits, name="lb_prep")
    cos, sin = _rope_tables(s)
    pars = [dict(gq=jnp.tile(q_norm_g[l], 2)[None], gk=jnp.tile(k_norm_g[l], 2)[None], hg=hgrn_norm_g[l][None],
                 la=la[l:l + 1], lc=lc[l:l + 1], cos=cos, sin=sin) for l in range(depth)]

    xs, saved = x0, []
    for l in range(depth):
        carry = (shards(l + 1, early), shards(l + 1, late)) if l + 1 < depth else None
        xs, sv, got = _layer_fwd(xs, mod[l], ws[l], pars[l], f"l{l}", carry)
        saved.append(sv)
        if got is not None:
            ws.append(layer_weights({**whole(early, got[0]), **whole(late, got[1])}))
    dx, lpart = _loss_grad(xs, tgt, name="loss")

    col_grad = {"ffn1_w_gate": ("gu1", 0), "ffn1_w_up": ("gu1", 1), "ffn2_w_gate": ("gu2", 0), "ffn2_w_up": ("gu2", 1), "w_in": ("win", None)}
    row_grad = {"ffn1_w_down": "d1", "ffn2_w_down": "d2", "w_out": "wout"}

    def by_chip(gr):
        out = []
        for n in big:
            if n in col_grad:
                key, half = col_grad[n]
                g = gr[key]
                if half is not None:
                    g = g[:, half * (g.shape[1] // 2):(half + 1) * (g.shape[1] // 2)]
                out.append(jnp.moveaxis(g.reshape(g.shape[0], 4, -1), 1, 0))
            else:
                g = gr[row_grad[n]]
                out.append(g.reshape(4, -1, g.shape[1]))
        return out

    grads, halves, pending = [None] * depth, [None] * depth, None
    for l in reversed(range(depth)):
        dx, grads[l], swapped = _layer_bwd(dx, saved[l], mod[l], ws[l], pars[l], f"l{l}", pending)
        if pending is not None:
            halves[l + 1] = swapped
        pending = by_chip(grads[l])
    got = _exchange(pending, "chips", True, name="scatter_grads")
    parts = [_sum_slots(g, name=f"sum_{n}", out_dtype=BF16) for n, g in zip(big, got)]
    halves[0] = _exchange(parts, "sib", False, name="swap_grads")

    stack = lambda k: jnp.stack([grads[l][k] for l in range(depth)])
    small = [stack("dmod"), stack("gq"), stack("gk"), stack("hg"), stack("la"), stack("lc"), lpart[0, :1]]
    packed = _pack(small)
    allp = _exchange([packed], "all", False, name="gather_small")[0]
    tot = _unpack(_sum_slots(allp, name="sum_small").reshape(-1), small)
    g_b_mod = tot[0].reshape(depth, N_MOD * d)
    loss = tot[6][0]
    g_small = {"b_mod": g_b_mod, "q_norm_g": tot[1], "k_norm_g": tot[2], "hgrn_norm_g": tot[3],
               "hgrn_lb_logits": _lb_bwd(hgrn_lb_logits, tot[4], tot[5], name="lb_bwd")}

    dm_all = allp.reshape(8, -1)[:, :depth * N_MOD * d].reshape(8, depth, N_MOD * d)
    dm_loc = jnp.transpose(lax.dynamic_slice_in_dim(dm_all, chip * ncol, ncol, axis=2), (1, 0, 2))
    g_w_mod = _mod_bwd(c8.T, dm_loc, name="mod_bwd")

    outs = {}
    for n in names:
        w2 = wts[n].reshape(-1, wts[n].shape[-1])
        m2, v2 = mom[n].reshape(w2.shape), var[n].reshape(w2.shape)
        if n in big:
            res = _adamw_layers(w2, [halves[l][big.index(n)] for l in range(depth)], m2, v2, name=f"adamw_{n}")
        else:
            g = g_w_mod if n == "w_mod" else g_small[n]
            res = _adamw(w2, [g.reshape(w2.shape)], m2, v2, name=f"adamw_{n}")
        outs[n] = [r.reshape(wts[n].shape) for r in res]
    return (loss, dx[None], *[outs[n][0] for n in names], *[outs[n][1] for n in names], *[outs[n][2] for n in names],
            *[outs[n][3] for n in names])
```

```python
import jax
import jax.numpy as jnp
from jax import lax
from jax.experimental import pallas as pl
from jax.experimental.pallas import tpu as pltpu

F32 = jnp.float32
BF16 = jnp.bfloat16
MESH_ID = pl.DeviceIdType.MESH

HEAD_DIM = 64
SB_W = 256
DIL_W = 256
HG_W = 512
HG_D = 128
IN_W = 3 * SB_W + 3 * DIL_W + 4 * HG_W
MIX_W = SB_W + DIL_W + HG_W
DIL_PATTERNS = ((128, 1), (512, 4), (2048, 16))
ROPE_THETA = 10000.0
EPS = 1e-6
LB_FLOOR = 1e-30
NEG_BIG = -1e30
N_MOD = 9
ADAM_LR = 0.001
ADAM_B1 = 0.9
ADAM_B2 = 0.999
ADAM_EPS = 1e-08
ADAM_WD = 0.01
ADAM_STEP = 10

LANES = 128
QBLK = 128
DIL_TILE = 1024
HG_BLK = 16
HG_TILE = 256
SB_EXIT = 88.0
VMEM_LIMIT = 48 * 1024 * 1024
XCHG_CHUNKS = 8
XCHG_MIN_BYTES = 1 << 19

NN = (((1,), (0,)), ((), ()))
NT = (((1,), (1,)), ((), ()))
TN = (((0,), (0,)), ((), ()))


def _pcall(body, **kw):
    return pl.pallas_call(body, **kw)


def _cparams(*sem):
    return pltpu.CompilerParams(dimension_semantics=sem if sem else None, vmem_limit_bytes=VMEM_LIMIT)


def _dot(a, b, dims=NN):
    return lax.dot_general(a, b, dims, preferred_element_type=F32)


def _split(x, n):
    parts = []
    r = x
    for i in range(n):
        p = r.astype(BF16)
        parts.append(p)
        if i + 1 < n:
            r = r - p.astype(F32)
    return parts


def _xdot(x, m, n=2):
    return sum(_dot(p, m) for p in _split(x, n))


def _iota(shape, dim):
    return lax.broadcasted_iota(jnp.int32, shape, dim)


def _sigmoid(x):
    return 1.0 / (1.0 + jnp.exp(-x))


def _tile(dim, pref, mult=LANES):
    t = (min(pref, dim) // mult) * mult
    while t >= mult:
        if dim % t == 0:
            return t
        t -= mult
    return dim


def _rows(dim, pref):
    return _tile(dim, pref, 8)


def _mm(a, b, *, name, tb=False, tm=512, tn=1024, tk=1024, out_dtype=F32, res=None, scale=None):
    m, kd = a.shape
    n = b.shape[0] if tb else b.shape[1]
    tm, tn, tk = _rows(m, tm), _tile(n, tn), _tile(kd, tk)
    nk = kd // tk
    epi = res is not None

    def body(*refs):
        if epi:
            a_ref, b_ref, r_ref, s_ref, o_ref, x_ref, acc = refs
        else:
            a_ref, b_ref, o_ref, acc = refs
        k = pl.program_id(2)

        @pl.when(k == 0)
        def _():
            acc[...] = jnp.zeros_like(acc)

        acc[...] += _dot(a_ref[...], b_ref[...], NT if tb else NN)

        @pl.when(k == nk - 1)
        def _():
            o_ref[...] = acc[...].astype(o_ref.dtype)
            if epi:
                x_ref[...] = r_ref[...] + s_ref[...] * acc[...]

    in_specs = [
        pl.BlockSpec((tm, tk), lambda i, j, k: (i, k)),
        pl.BlockSpec((tn, tk), lambda i, j, k: (j, k)) if tb else pl.BlockSpec((tk, tn), lambda i, j, k: (k, j)),
    ]
    out_shape = [jax.ShapeDtypeStruct((m, n), out_dtype)]
    out_specs = [pl.BlockSpec((tm, tn), lambda i, j, k: (i, j))]
    args = [a, b]
    if epi:
        in_specs += [pl.BlockSpec((tm, tn), lambda i, j, k: (i, j)), pl.BlockSpec((1, tn), lambda i, j, k: (0, j))]
        out_shape.append(jax.ShapeDtypeStruct((m, n), F32))
        out_specs.append(pl.BlockSpec((tm, tn), lambda i, j, k: (i, j)))
        args += [res, scale]
    out = _pcall(
        body, name=name, grid=(m // tm, n // tn, nk), in_specs=in_specs, out_specs=out_specs, out_shape=out_shape,
        scratch_shapes=[pltpu.VMEM((tm, tn), F32)], compiler_params=_cparams("parallel", "parallel", "arbitrary"),
    )(*args)
    return out if epi else out[0]


def _mm_tn(a, b, *, name, tm=1024, tn=1408, tk=512, out_dtype=BF16):
    s, m = a.shape
    n = b.shape[1]
    tm, tn, tk = _tile(m, tm), _tile(n, tn), _rows(s, tk)
    nk = s // tk

    def body(a_ref, b_ref, o_ref, acc):
        k = pl.program_id(2)

        @pl.when(k == 0)
        def _():
            acc[...] = jnp.zeros_like(acc)

        acc[...] += _dot(a_ref[...], b_ref[...], TN)

        @pl.when(k == nk - 1)
        def _():
            o_ref[...] = acc[...].astype(o_ref.dtype)

    return _pcall(
        body, name=name, grid=(m // tm, n // tn, nk),
        in_specs=[pl.BlockSpec((tk, tm), lambda i, j, k: (k, i)), pl.BlockSpec((tk, tn), lambda i, j, k: (k, j))],
        out_specs=pl.BlockSpec((tm, tn), lambda i, j, k: (i, j)), out_shape=jax.ShapeDtypeStruct((m, n), out_dtype),
        scratch_shapes=[pltpu.VMEM((tm, tn), F32)], compiler_params=_cparams("parallel", "parallel", "arbitrary"),
    )(a, b)


TE = 512


def _row_spec(t, w, col=0):
    return pl.BlockSpec((t, w), lambda i, col=col: (i, col))


def _vec_spec(w, col=0):
    return pl.BlockSpec((1, w), lambda i, col=col: (0, col))


def _gate_bwd(dxo, y, sg, *, name):
    s, d = y.shape
    t = _rows(s, TE)

    def body(dxo_ref, y_ref, sg_ref, dy_ref, ds_ref):
        @pl.when(pl.program_id(0) == 0)
        def _():
            ds_ref[...] = jnp.zeros_like(ds_ref)

        dv = dxo_ref[...]
        dy_ref[...] = (sg_ref[...] * dv).astype(BF16)
        ds_ref[...] += jnp.sum(dv * y_ref[...], axis=0, keepdims=True)

    return _pcall(
        body, name=name, grid=(s // t,), in_specs=[_row_spec(t, d), _row_spec(t, d), _vec_spec(d)],
        out_specs=[_row_spec(t, d), _vec_spec(d)],
        out_shape=[jax.ShapeDtypeStruct((s, d), BF16), jax.ShapeDtypeStruct((1, d), F32)],
        compiler_params=_cparams("arbitrary"))(dxo, y, sg)


def _loss_grad(y, tgt, *, name):
    s, d = y.shape
    t = _rows(s, TE)
    nt = s // t

    def body(y_ref, t_ref, dy_ref, l_ref, acc):
        i = pl.program_id(0)

        @pl.when(i == 0)
        def _():
            acc[...] = jnp.zeros_like(acc)

        e = y_ref[...] - t_ref[...]
        dy_ref[...] = e * (1.0 / d)
        acc[...] += jnp.sum(e * e, axis=0, keepdims=True)

        @pl.when(i == nt - 1)
        def _():
            l_ref[...] = jnp.broadcast_to(jnp.sum(acc[...], axis=1, keepdims=True) * (0.5 / d), l_ref.shape)

    return _pcall(
        body, name=name, grid=(nt,), in_specs=[_row_spec(t, d), _row_spec(t, d)],
        out_specs=[_row_spec(t, d), pl.BlockSpec((1, LANES), lambda i: (0, 0))],
        out_shape=[jax.ShapeDtypeStruct((s, d), F32), jax.ShapeDtypeStruct((1, LANES), F32)],
        scratch_shapes=[pltpu.VMEM((1, d), F32)], compiler_params=_cparams("arbitrary"))(y, tgt)


SB_TQ = 256
SB_NK = SB_TQ // QBLK


def _sb_consts():
    r = _iota((QBLK, LANES), 0)
    c = _iota((QBLK, LANES), 1)
    ones = jnp.ones((QBLK, LANES), BF16)
    after = jnp.concatenate([jnp.where(r > c, 1.0, 0.0).astype(BF16), ones], axis=1)
    from_ = jnp.concatenate([jnp.where(r >= c, 1.0, 0.0).astype(BF16), ones], axis=1)
    return _iota((SB_TQ, LANES), 0), _iota((SB_TQ, LANES), 1), after, from_


def _sb_scores(qm, kb, strict):
    z = _dot(qm, kb, NT) * (HEAD_DIM ** -0.5)
    sp = jnp.log(1.0 + jnp.exp(-jnp.abs(z)))
    lnb = -(jnp.maximum(z, 0.0) + sp)
    lb = jnp.minimum(z, 0.0) - sp
    if strict is not None:
        lnb = jnp.where(strict, lnb, 0.0)
    return lnb, lb


def _sb_fwd(qkv, *, name, side=None):
    s = qkv.shape[0]
    nq = s // SB_TQ

    def body(q_ref, k_ref, v_ref, o_ref, *scr):
        acc, osc = scr[:4], scr[4:]
        qi = pl.program_id(0)
        row, lane, after, _ = _sb_consts()
        h0 = lane < HEAD_DIM
        q = q_ref[...]
        qms = []
        for p in range(2):
            qp = q[:, p * LANES:(p + 1) * LANES]
            qms += [jnp.where(h0, qp, jnp.zeros_like(qp)), jnp.where(h0, jnp.zeros_like(qp), qp)]

        def block(kj, mask):
            off = pl.multiple_of(kj * QBLK, QBLK)
            kbs = [k_ref[pl.ds(off, QBLK), p * LANES:(p + 1) * LANES] for p in range(2)]
            vbs = [v_ref[pl.ds(off, QBLK), p * LANES:(p + 1) * LANES] for p in range(2)]
            sc = [_sb_scores(qms[c], kbs[c // 2], mask) for c in range(4)]
            trs = [_xdot(sc[c][0], after) for c in range(4)]
            top = None
            for c in range(4):
                w = jnp.exp(sc[c][1] + trs[c][:, :QBLK] + acc[c][...])
                if mask is not None:
                    w = jnp.where(mask, w, 0.0)
                osc[c][...] += _xdot(w, vbs[c // 2])
                new = acc[c][...] + trs[c][:, QBLK:]
                acc[c][...] = new
                top = new if top is None else jnp.maximum(top, new)
            return jnp.max(top)

        for ref in scr:
            ref[...] = jnp.zeros_like(ref)
        top = None
        for j in reversed(range(SB_NK)):
            top = block(qi * SB_NK + j, (lane + j * QBLK) < row)
        lax.while_loop(lambda c: (c[0] >= 0) & (c[1] > -SB_EXIT), lambda c: (c[0] - 1, block(c[0], None)),
                       (qi * SB_NK - 1, top))
        for p in range(2):
            o_ref[:, p * LANES:(p + 1) * LANES] = jnp.where(h0, osc[2 * p][...], osc[2 * p + 1][...])

    (o,), got = _call_with_exchange(
        body, side, lambda: pl.program_id(0) == 0, lambda: pl.program_id(0) == nq - 1, (qkv, qkv, qkv), name=name, grid=(nq,),
        in_specs=[pl.BlockSpec((SB_TQ, SB_W), lambda i: (i, 0)),
                  pl.BlockSpec((s, SB_W), lambda i: (0, 1)),
                  pl.BlockSpec((s, SB_W), lambda i: (0, 2))],
        out_specs=[pl.BlockSpec((SB_TQ, SB_W), lambda i: (i, 0))],
        out_shape=[jax.ShapeDtypeStruct((s, SB_W), F32)],
        scratch_shapes=[pltpu.VMEM((SB_TQ, LANES), F32)] * 8,
        compiler_params=_cparams("arbitrary"))
    return o, got


def _sb_bwd(qkv, o, dmix, *, name, side=None):
    s = qkv.shape[0]
    nq = s // SB_TQ
    scale = HEAD_DIM ** -0.5

    def body(q_ref, k_ref, v_ref, o_ref, do_ref, dq_ref, dk_ref, dv_ref, a0, a1, r0, r1, dqs, dks, dvs):
        acc, racc = (a0, a1), (r0, r1)
        i = pl.program_id(1)
        qi = nq - 1 - i
        row, lane, after, from_ = _sb_consts()
        klane = _iota((QBLK, LANES), 1)
        khms = (klane < HEAD_DIM, klane >= HEAD_DIM)

        @pl.when(i == 0)
        def _():
            dks[...] = jnp.zeros_like(dks)
            dvs[...] = jnp.zeros_like(dvs)

        q = q_ref[...]
        do = do_ref[...]
        dob = do.astype(BF16)
        dd = do * o_ref[...]
        dol = (do - dob.astype(F32)).astype(BF16)
        zero = jnp.zeros_like(q)
        hms = (lane < HEAD_DIM, lane >= HEAD_DIM)
        qms = [jnp.where(hm, q, zero) for hm in hms]
        doms = [jnp.where(hm, dob, zero) for hm in hms]
        dols = [jnp.where(hm, dol, zero) for hm in hms]
        dsums = [jnp.sum(jnp.where(hm, dd, 0.0), axis=1, keepdims=True) for hm in hms]

        def block(kj, mask):
            off = pl.multiple_of(kj * QBLK, QBLK)
            kb = k_ref[pl.ds(off, QBLK), :]
            vb = v_ref[pl.ds(off, QBLK), :]
            top, dq, dk, dv = None, None, None, None
            sc = [_sb_scores(qms[h], kb, mask) for h in range(2)]
            trs = [_xdot(sc[h][0], after) for h in range(2)]
            dws = [_dot(doms[h], vb, NT) + _dot(dols[h], vb, NT) for h in range(2)]
            for h in range(2):
                lb, tr = sc[h][1], trs[h]
                w = jnp.exp(lb + tr[:, :QBLK] + acc[h][...])
                if mask is not None:
                    w = jnp.where(mask, w, 0.0)
                g = w * dws[h]
                tg = _xdot(g, from_)
                before = dsums[h] - (tg[:, :QBLK] + racc[h][...])
                dz = g - jnp.exp(lb) * (g + before)
                if mask is not None:
                    dz = jnp.where(mask, dz, 0.0)
                dzb = (dz * scale).astype(BF16)
                dqh = _dot(dzb, jnp.where(khms[h], kb, jnp.zeros_like(kb)))
                dkh = _dot(dzb, qms[h], TN)
                dvh = _dot(w.astype(BF16), doms[h], TN)
                dq, dk, dv = (dqh, dkh, dvh) if h == 0 else (dq + dqh, dk + dkh, dv + dvh)
                new = acc[h][...] + tr[:, QBLK:]
                acc[h][...] = new
                racc[h][...] += tg[:, QBLK:]
                top = new if top is None else jnp.maximum(top, new)
            dqs[...] += dq
            dks[pl.ds(off, QBLK), :] += dk
            dvs[pl.ds(off, QBLK), :] += dv
            return jnp.max(top)

        for ref in (dqs, a0, a1, r0, r1):
            ref[...] = jnp.zeros_like(ref)
        top = None
        for j in reversed(range(SB_NK)):
            top = block(qi * SB_NK + j, (lane + j * QBLK) < row)
        lax.while_loop(lambda c: (c[0] >= 0) & (c[1] > -SB_EXIT), lambda c: (c[0] - 1, block(c[0], None)),
                       (qi * SB_NK - 1, top))
        dq_ref[...] = dqs[...]
        fin = pl.multiple_of(qi * SB_TQ, SB_TQ)
        dk_ref[...] = dks[pl.ds(fin, SB_TQ), :]
        dv_ref[...] = dvs[pl.ds(fin, SB_TQ), :]

    blk = lambda c0: pl.BlockSpec((SB_TQ, LANES), lambda p, i, c0=c0: (nq - 1 - i, c0 + p))
    return _call_with_exchange(
        body, side, lambda: (pl.program_id(0) == 0) & (pl.program_id(1) == 0),
        lambda: (pl.program_id(0) == 1) & (pl.program_id(1) == nq - 1), (qkv, qkv, qkv, o, dmix), name=name, grid=(2, nq),
        in_specs=[blk(0), pl.BlockSpec((s, LANES), lambda p, i: (0, 2 + p)), pl.BlockSpec((s, LANES), lambda p, i: (0, 4 + p)),
                  blk(0), blk(0)],
        out_specs=[blk(0), blk(0), blk(0)],
        out_shape=[jax.ShapeDtypeStruct((s, SB_W), F32)] * 3,
        scratch_shapes=[pltpu.VMEM((SB_TQ, LANES), F32)] * 5 + [pltpu.VMEM((s, LANES), F32), pltpu.VMEM((s, LANES), F32)],
        compiler_params=_cparams("arbitrary", "arbitrary"))


def _seg_consts():
    r = _iota((LANES, LANES), 0)
    c = _iota((LANES, LANES), 1)
    return jnp.where((r >> 6) == (c >> 6), 1.0, 0.0).astype(BF16)


def _rot_half(x, lane):
    half = HEAD_DIM // 2
    return jnp.where((lane & (HEAD_DIM - 1)) < half, pltpu.roll(x, LANES - half, 1), pltpu.roll(x, half, 1))


def _rope_tables(s):
    half = HEAD_DIM // 2
    inv_freq = ROPE_THETA ** (-jnp.arange(half, dtype=F32) * 2.0 / HEAD_DIM)
    ang = jnp.arange(s, dtype=F32)[:, None] * inv_freq[None, :]
    cos, sin = jnp.cos(ang), jnp.sin(ang)
    return jnp.tile(jnp.concatenate([cos, cos], axis=1), (1, 2)), jnp.tile(jnp.concatenate([-sin, sin], axis=1), (1, 2))


def _dil_prep(proj, gq, gk, cos, sin, *, name):
    s = proj.shape[0]
    t = _rows(s, TE)
    c0 = 3 * SB_W // LANES

    def body(q_ref, k_ref, gq_ref, gk_ref, cos_ref, sin_ref, qo_ref, ko_ref):
        seg = _seg_consts()
        lane = _iota((t, LANES), 1)
        cs, sn = cos_ref[...], sin_ref[...]
        for x_ref, g_ref, o_ref, mul in ((q_ref, gq_ref, qo_ref, HEAD_DIM ** -0.5), (k_ref, gk_ref, ko_ref, 1.0)):
            for j in range(2):
                xv = x_ref[:, j * LANES:(j + 1) * LANES]
                ms = _xdot(xv * xv, seg, 3) * (1.0 / HEAD_DIM)
                xn = xv * lax.rsqrt(ms + EPS) * g_ref[...]
                o_ref[:, j * LANES:(j + 1) * LANES] = (xn * cs + _rot_half(xn, lane) * sn) * mul

    return _pcall(
        body, name=name, grid=(s // t,),
        in_specs=[pl.BlockSpec((t, DIL_W), lambda i: (i, c0 // 2)), pl.BlockSpec((t, DIL_W), lambda i: (i, c0 // 2 + 1)),
                  _vec_spec(LANES), _vec_spec(LANES), _row_spec(t, LANES), _row_spec(t, LANES)],
        out_specs=[_row_spec(t, DIL_W), _row_spec(t, DIL_W)],
        out_shape=[jax.ShapeDtypeStruct((s, DIL_W), F32)] * 2, compiler_params=_cparams("parallel"))(proj, proj, gq, gk, cos, sin)


def _dil_prep_bwd(proj, gq, gk, cos, sin, dqs, dks, dvs, *, name):
    s = proj.shape[0]
    t = _rows(s, TE)
    c0 = 3 * SB_W // LANES

    def body(q_ref, k_ref, gq_ref, gk_ref, cos_ref, sin_ref, a0, a1, a2, b0, b1, b2, c0_ref, c1_ref, c2_ref,
             dq_ref, dk_ref, dv_ref, dgq_ref, dgk_ref):
        @pl.when(pl.program_id(0) == 0)
        def _():
            dgq_ref[...] = jnp.zeros_like(dgq_ref)
            dgk_ref[...] = jnp.zeros_like(dgk_ref)

        dv_ref[...] = c0_ref[...] + c1_ref[...] + c2_ref[...]
        seg = _seg_consts()
        lane = _iota((t, LANES), 1)
        cs, sn = cos_ref[...], sin_ref[...]
        for x_ref, g_ref, parts, o_ref, dg_ref, mul in ((q_ref, gq_ref, (a0, a1, a2), dq_ref, dgq_ref, HEAD_DIM ** -0.5),
                                                          (k_ref, gk_ref, (b0, b1, b2), dk_ref, dgk_ref, 1.0)):
            for j in range(2):
                sl = slice(j * LANES, (j + 1) * LANES)
                dout = (parts[0][:, sl] + parts[1][:, sl] + parts[2][:, sl]) * mul
                dxn = dout * cs + _rot_half(dout * sn, lane)
                xv = x_ref[:, sl]
                r = lax.rsqrt(_xdot(xv * xv, seg, 3) * (1.0 / HEAD_DIM) + EPS)
                xh = xv * r
                dg_ref[...] += jnp.sum(dxn * xh, axis=0, keepdims=True)
                dxh = dxn * g_ref[...]
                o_ref[:, sl] = r * (dxh - xh * (_xdot(dxh * xh, seg, 3) * (1.0 / HEAD_DIM)))

    rs = _row_spec(t, DIL_W)
    return _pcall(
        body, name=name, grid=(s // t,),
        in_specs=[pl.BlockSpec((t, DIL_W), lambda i: (i, c0 // 2)), pl.BlockSpec((t, DIL_W), lambda i: (i, c0 // 2 + 1)),
                  _vec_spec(LANES), _vec_spec(LANES), _row_spec(t, LANES), _row_spec(t, LANES)] + [rs] * 9,
        out_specs=[rs, rs, rs, _vec_spec(LANES), _vec_spec(LANES)],
        out_shape=[jax.ShapeDtypeStruct((s, DIL_W), F32)] * 3 + [jax.ShapeDtypeStruct((1, LANES), F32)] * 2,
        compiler_params=_cparams("arbitrary"))(proj, proj, gq, gk, cos, sin, *dqs, *dks, *dvs)


def _dil_masks(n):
    row = _iota((QBLK, 2 * LANES), 0)
    col = _iota((QBLK, 2 * LANES), 1)
    return ((col < LANES) & (col >= row) & (n > 0)) | ((col >= LANES) & (col - LANES <= row))


DIL_V0 = (3 * SB_W + 2 * DIL_W) // LANES
DIL_DO0 = SB_W // LANES


def _dil_tiles(s, r):
    span = QBLK * r
    nsub = max(1, DIL_TILE // span)
    while s % (nsub * span):
        nsub -= 1
    return span, nsub


def _dil_rows(j, rho, span, r):
    return pl.ds(j * span + rho, QBLK, stride=r) if r > 1 else pl.ds(j * span, QBLK)


def _dil_fwd(q, k, proj, r, *, name):
    s = q.shape[0]
    span, nsub = _dil_tiles(s, r)
    tr = nsub * span

    def body(q_ref, kc_ref, kp_ref, vc_ref, vp_ref, num_ref, den_ref, mx_ref):
        n = pl.program_id(1)
        lane = _iota((QBLK, LANES), 1)
        h0 = lane < HEAD_DIM
        ones = jnp.ones((2 * QBLK, LANES), BF16)
        for j in range(nsub):
            valid = _dil_masks(n if j == 0 else 1)
            for rho in range(r):
                rows = _dil_rows(j, rho, span, r)
                before = _dil_rows(max(j - 1, 0), rho, span, r)
                k_prev, v_prev = (kp_ref, vp_ref) if j == 0 else (kc_ref, vc_ref)
                qv = q_ref[rows, :].astype(BF16)
                kk = jnp.concatenate([k_prev[before, :], kc_ref[rows, :]], axis=0).astype(BF16)
                vv = jnp.concatenate([jnp.concatenate([v_prev[before, :], vc_ref[rows, :]], axis=0).astype(BF16), ones], axis=1)
                res = []
                for h in range(2):
                    qm = jnp.where(h0 if h == 0 else ~h0, qv, jnp.zeros_like(qv))
                    sc = jnp.where(valid, _dot(qm, kk, NT), NEG_BIG)
                    mx = jnp.max(sc, axis=1, keepdims=True)
                    nd = _dot(jnp.exp(sc - mx).astype(BF16), vv)
                    res.append((nd[:, :LANES], nd[:, LANES:], mx))
                num_ref[rows, :] = jnp.where(h0, res[0][0], res[1][0])
                den_ref[rows, :] = jnp.where(h0, res[0][1], res[1][1])
                mx_ref[rows, :] = jnp.where(h0, res[0][2], res[1][2])

    cur = lambda c0: pl.BlockSpec((tr, LANES), lambda p, n, c0=c0: (n, c0 + p))
    prev = lambda c0: pl.BlockSpec((span, LANES), lambda p, n, c0=c0: (jnp.maximum(n * nsub - 1, 0), c0 + p))
    return _pcall(
        body, name=name, grid=(2, s // tr), in_specs=[cur(0), cur(0), prev(0), cur(DIL_V0), prev(DIL_V0)],
        out_specs=[cur(0), cur(0), cur(0)], out_shape=[jax.ShapeDtypeStruct((s, DIL_W), F32)] * 3,
        compiler_params=_cparams("parallel", "arbitrary"))(q, k, k, proj, proj)


def _dil_bwd(q, k, proj, dmix, mall, zall, delta, r, *, name):
    s = q.shape[0]
    span, nsub = _dil_tiles(s, r)
    tr = nsub * span
    nbig = s // tr

    def body(q_ref, kc_ref, kp_ref, vc_ref, vp_ref, do_ref, m_ref, z_ref, dl_ref, dq_ref, dk_ref, dv_ref, pk, pv):
        n = pl.program_id(1)
        lane = _iota((QBLK, LANES), 1)
        h0 = lane < HEAD_DIM

        @pl.when(n == 0)
        def _():
            pk[...] = jnp.zeros_like(pk)
            pv[...] = jnp.zeros_like(pv)

        @pl.when(n < nbig)
        def _():
            dk_ref[...] = pk[...]
            dv_ref[...] = pv[...]
            for j in range(nsub):
                valid = _dil_masks(n if j == 0 else 1)
                for rho in range(r):
                    rows = _dil_rows(j, rho, span, r)
                    before = _dil_rows(max(j - 1, 0), rho, span, r)
                    k_prev, v_prev = (kp_ref, vp_ref) if j == 0 else (kc_ref, vc_ref)
                    qv = q_ref[rows, :].astype(BF16)
                    dob = do_ref[rows, :].astype(BF16)
                    zero = jnp.zeros_like(qv)
                    kk = jnp.concatenate([k_prev[before, :], kc_ref[rows, :]], axis=0).astype(BF16)
                    vv = jnp.concatenate([v_prev[before, :], vc_ref[rows, :]], axis=0).astype(BF16)
                    mall_v, z_v, dl_v = m_ref[rows, :], z_ref[rows, :], dl_ref[rows, :]
                    dq, dk, dv = None, None, None
                    for h in range(2):
                        hm = h0 if h == 0 else ~h0
                        qm = jnp.where(hm, qv, zero)
                        dom = jnp.where(hm, dob, zero)
                        c = h * HEAD_DIM
                        sc = jnp.where(valid, _dot(qm, kk, NT), NEG_BIG)
                        pr = jnp.exp(sc - mall_v[:, c:c + 1]) * (1.0 / z_v[:, c:c + 1])
                        ds = (pr * (_dot(dom, vv, NT) - dl_v[:, c:c + 1])).astype(BF16)
                        parts = (_dot(ds, jnp.where(jnp.concatenate([hm, hm], axis=0), kk, jnp.zeros_like(kk))),
                                 _dot(ds, qm, TN), _dot(pr.astype(BF16), dom, TN))
                        dq, dk, dv = parts if h == 0 else (dq + parts[0], dk + parts[1], dv + parts[2])
                    dq_ref[rows, :] = dq
                    pk[rows, :] = dk[QBLK:]
                    pv[rows, :] = dv[QBLK:]
                    if j == 0:
                        last_span = _dil_rows(nsub - 1, rho, span, r)
                        dk_ref[last_span, :] += dk[:QBLK]
                        dv_ref[last_span, :] += dv[:QBLK]
                    else:
                        pk[before, :] += dk[:QBLK]
                        pv[before, :] += dv[:QBLK]

        @pl.when(n == nbig)
        def _():
            dk_ref[...] = pk[...]
            dv_ref[...] = pv[...]

    last = nbig - 1
    cur = lambda c0: pl.BlockSpec((tr, LANES), lambda p, n, c0=c0: (jnp.minimum(n, last), c0 + p))
    prev = lambda c0: pl.BlockSpec((span, LANES), lambda p, n, c0=c0: (jnp.maximum(jnp.minimum(n, last) * nsub - 1, 0), c0 + p))
    late = pl.BlockSpec((tr, LANES), lambda p, n: (jnp.maximum(n - 1, 0), p))
    return _pcall(
        body, name=name, grid=(2, nbig + 1),
        in_specs=[cur(0), cur(0), prev(0), cur(DIL_V0), prev(DIL_V0), cur(DIL_DO0), cur(0), cur(0), cur(0)],
        out_specs=[cur(0), late, late], out_shape=[jax.ShapeDtypeStruct((s, DIL_W), F32)] * 3,
        scratch_shapes=[pltpu.VMEM((tr, LANES), F32)] * 2,
        compiler_params=_cparams("parallel", "arbitrary"))(q, k, k, proj, proj, dmix, mall, zall, delta)


HG_SHIFT = HG_BLK.bit_length() - 1
HG_Q0, HG_F0, HG_I0 = (3 * SB_W + 3 * DIL_W) // HG_D, (3 * SB_W + 3 * DIL_W + HG_W) // HG_D, (3 * SB_W + 3 * DIL_W + 2 * HG_W) // HG_D


def _hg_scan(x):
    t = x.shape[0]
    half = HG_BLK // 2
    rb = _iota((t, LANES), 0) & (HG_BLK - 1)
    rh = rb & (half - 1)
    p = x
    for s in (1, 2, 4):
        p = p + jnp.where(rh >= s, pltpu.roll(p, s, 0), 0.0)
    h = jnp.where(rh == half - 1, p, 0.0)
    for s in (1, 2, 4):
        h = h + jnp.where(rh + s < half, pltpu.roll(h, t - s, 0), 0.0)
    first = rb < half
    pref = jnp.where(first, p, p + pltpu.roll(h, half, 0))
    total = h + jnp.where(first, pltpu.roll(h, t - half, 0), pltpu.roll(h, half, 0))
    return p, h, pref, total, first


def _hg_same(t):
    i = jnp.arange(t) >> HG_SHIFT
    return (i[:, None] == i[None, :]).astype(F32)


def _hg_own(t):
    i = jnp.arange(t) >> HG_SHIFT
    j = jnp.arange(t // HG_BLK * HG_D) // HG_D
    return (i[:, None] == j[None, :]).astype(BF16)


def _hg_diag(x, nb):
    return jnp.concatenate([x[b * HG_BLK:(b + 1) * HG_BLK, b * HG_D:(b + 1) * HG_D] for b in range(nb)], axis=0)


def _hg_inputs(qh, z, v, la, lc, t):
    lsg = jnp.minimum(z, 0.0) - jnp.log(1.0 + jnp.exp(-jnp.abs(z)))
    b = lc + lsg
    lf = jnp.maximum(la, b) + jnp.log(1.0 + jnp.exp(-jnp.abs(la - b)))
    f = jnp.exp(lf)
    sq = _sigmoid(qh)
    p, h, g, gl, first = _hg_scan(lf)
    k = 1.0 - f
    qs = qh * sq
    eq = jnp.where(first, 0.0, jnp.exp(jnp.minimum(p, 0.0)))
    ek = jnp.where(first, jnp.exp(jnp.minimum(h - p, 0.0)), 0.0)
    return dict(lf=lf, b=b, f=f, k=k, sq=sq, qs=qs, g=g, eg=jnp.exp(g), egl=jnp.exp(gl - g), dec=jnp.exp(gl),
                eq=eq, ek=ek, qx=(qs * eq).astype(BF16), kx=(k * ek).astype(BF16))


def _hgrn_fwd(proj, la, lc, *, name, side=None):
    s = proj.shape[0]
    t = _rows(s, HG_TILE)
    nt, nb = s // t, t // HG_BLK

    def body(q_ref, f_ref, i_ref, la_ref, lc_ref, same_ref, own_ref, o_ref, st_ref, state):
        @pl.when(pl.program_id(1) == 0)
        def _():
            state[...] = jnp.zeros_like(state)

        v = i_ref[...]
        a = _hg_inputs(q_ref[...], f_ref[...], v, la_ref[...], lc_ref[...], t)
        qs, k = a["qs"], a["k"]
        vb = v.astype(BF16)
        rb = _iota((t, LANES), 0) & (HG_BLK // 2 - 1)
        o = jnp.sum(qs * k, axis=1, keepdims=True) * v
        e = None
        for d in range(1, HG_BLK // 2):
            m = rb >= d
            fr = a["f"] if d == 1 else pltpu.roll(a["f"], d - 1, 0)
            e = fr if e is None else e * fr
            cd = jnp.sum(qs * pltpu.roll(k, d, 0) * e, axis=1, keepdims=True)
            o = o + jnp.where(m, cd, 0.0) * pltpu.roll(v, d, 0)
        cross = _dot(a["qx"], a["kx"], NT) * same_ref[...]
        o = o + _dot(cross.astype(BF16), vb)
        qt = (qs * a["eg"]).astype(BF16)
        kt = (k * a["egl"]).astype(BF16)
        upd = _dot(vb, jnp.tile(kt, (1, nb)) * own_ref[...], TN)
        st = state[...]
        for blk in range(nb):
            st_ref[blk * HG_D:(blk + 1) * HG_D, :] = st.astype(BF16)
            st = a["dec"][blk * HG_BLK:blk * HG_BLK + 1] * st + upd[:, blk * HG_D:(blk + 1) * HG_D]
        state[...] = st
        o_ref[...] = o + _hg_diag(_dot(qt, st_ref[...], NT), nb)

    col = lambda c0: pl.BlockSpec((t, HG_D), lambda hd, i, c0=c0: (i, c0 + hd))
    vec = pl.BlockSpec((1, HG_D), lambda hd, i: (0, hd))
    return _call_with_exchange(
        body, side, lambda: (pl.program_id(0) == 0) & (pl.program_id(1) == 0),
        lambda: (pl.program_id(0) == 3) & (pl.program_id(1) == nt - 1),
        (proj, proj, proj, la, lc, _hg_same(t), _hg_own(t)), name=name, grid=(4, nt),
        in_specs=[col(HG_Q0), col(HG_F0), col(HG_I0), vec, vec, pl.BlockSpec((t, t), lambda hd, i: (0, 0)),
                  pl.BlockSpec((t, nb * HG_D), lambda hd, i: (0, 0))],
        out_specs=[col(0), pl.BlockSpec((None, nb * HG_D, HG_D), lambda hd, i: (hd, i, 0))],
        out_shape=[jax.ShapeDtypeStruct((s, HG_W), F32), jax.ShapeDtypeStruct((4, s // HG_BLK * HG_D, HG_D), BF16)],
        scratch_shapes=[pltpu.VMEM((HG_D, HG_D), F32)],
        compiler_params=_cparams("arbitrary", "arbitrary"))


def _hgrn_bwd(proj, la, lc, st, doh, *, name, side=None):
    s = proj.shape[0]
    t = _rows(s, HG_TILE)
    nt, nb = s // t, t // HG_BLK

    def body(q_ref, f_ref, i_ref, la_ref, lc_ref, st_ref, do_ref, same_ref, own_ref, dq_ref, df_ref, di_ref, dla_ref, dlc_ref,
             dstate, dsb):
        @pl.when(pl.program_id(1) == 0)
        def _():
            dstate[...] = jnp.zeros_like(dstate)
            dla_ref[...] = jnp.zeros_like(dla_ref)
            dlc_ref[...] = jnp.zeros_like(dlc_ref)

        qh, z, v, do = q_ref[...], f_ref[...], i_ref[...], do_ref[...]
        la = la_ref[...]
        a = _hg_inputs(qh, z, v, la, lc_ref[...], t)
        qs, k, g = a["qs"], a["k"], a["g"]
        vb = v.astype(BF16)
        dob = do.astype(BF16)
        rb = _iota((t, LANES), 0) & (HG_BLK // 2 - 1)
        dc0 = jnp.sum(do * v, axis=1, keepdims=True)
        dq = dc0 * k
        dk = dc0 * qs
        dv = jnp.sum(qs * k, axis=1, keepdims=True) * do
        e = None
        for d in range(1, HG_BLK // 2):
            m = rb >= d
            fr = a["f"] if d == 1 else pltpu.roll(a["f"], d - 1, 0)
            e = fr if e is None else e * fr
            ks = pltpu.roll(k, d, 0)
            qe = qs * e
            cd = jnp.where(m, jnp.sum(qe * ks, axis=1, keepdims=True), 0.0)
            dcd = jnp.where(m, jnp.sum(do * pltpu.roll(v, d, 0), axis=1, keepdims=True), 0.0)
            dq = dq + dcd * ks * e
            dk = dk + pltpu.roll(dcd * qe, t - d, 0)
            dv = dv + pltpu.roll(cd * do, t - d, 0)
        same = same_ref[...]
        cross = (_dot(a["qx"], a["kx"], NT) * same).astype(BF16)
        dcross = (_dot(dob, vb, NT) * same).astype(BF16)
        dq = dq + _dot(dcross, a["kx"]) * a["eq"]
        dk = dk + _dot(dcross, a["qx"], TN) * a["ek"]
        dv = dv + _dot(cross, dob, TN)
        qt = (qs * a["eg"]).astype(BF16)
        kt = (k * a["egl"]).astype(BF16)
        own = own_ref[...]
        upd = _dot(dob, jnp.tile(qt, (1, nb)) * own, TN)
        ds = dstate[...]
        dgs = [None] * nb
        for blk in reversed(range(nb)):
            rows = slice(blk * HG_D, (blk + 1) * HG_D)
            dec = a["dec"][blk * HG_BLK:blk * HG_BLK + 1]
            dsb[rows, :] = ds.astype(BF16)
            dgs[blk] = jnp.broadcast_to(jnp.sum(ds * st_ref[rows, :].astype(F32), axis=0, keepdims=True) * dec, (HG_BLK, HG_D))
            ds = dec * ds + upd[:, rows]
        dstate[...] = ds
        dki = _dot(jnp.tile(vb, (1, nb)) * own, dsb[...]) * a["egl"]
        dq = dq + _dot(jnp.tile(dob, (1, nb)) * own, st_ref[...]) * a["eg"]
        dk = dk + dki
        dv = dv + _hg_diag(_dot(kt, dsb[...], NT), nb)
        x = qs * dq - k * dk
        _, _, xpre, xtot, _ = _hg_scan(x)
        _, _, _, ktot, _ = _hg_scan(k * dki)
        dlf = (xtot - xpre + x) + ktot + jnp.concatenate(dgs, axis=0) - a["f"] * dk
        wb = jnp.exp(a["b"] - a["lf"])
        wa = jnp.exp(la - a["lf"])
        sq = a["sq"]
        dq_ref[...] = dq * (sq * (1.0 + qh * (1.0 - sq)))
        df_ref[...] = dlf * wb * (1.0 - _sigmoid(z))
        di_ref[...] = dv
        dla_ref[...] += jnp.sum(dlf * wa, axis=0, keepdims=True)
        dlc_ref[...] += jnp.sum(dlf * wb, axis=0, keepdims=True)

    col = lambda c0: pl.BlockSpec((t, HG_D), lambda hd, i, c0=c0: (nt - 1 - i, c0 + hd))
    vec = pl.BlockSpec((1, HG_D), lambda hd, i: (0, hd))
    return _call_with_exchange(
        body, side, lambda: (pl.program_id(0) == 0) & (pl.program_id(1) == 0),
        lambda: (pl.program_id(0) == 3) & (pl.program_id(1) == nt - 1),
        (proj, proj, proj, la, lc, st, doh, _hg_same(t), _hg_own(t)), name=name, grid=(4, nt),
        in_specs=[col(HG_Q0), col(HG_F0), col(HG_I0), vec, vec,
                  pl.BlockSpec((None, nb * HG_D, HG_D), lambda hd, i: (hd, nt - 1 - i, 0)), col(0),
                  pl.BlockSpec((t, t), lambda hd, i: (0, 0)), pl.BlockSpec((t, nb * HG_D), lambda hd, i: (0, 0))],
        out_specs=[col(0), col(0), col(0), vec, vec],
        out_shape=[jax.ShapeDtypeStruct((s, HG_W), F32)] * 3 + [jax.ShapeDtypeStruct((1, HG_W), F32)] * 2,
        scratch_shapes=[pltpu.VMEM((HG_D, HG_D), F32), pltpu.VMEM((nb * HG_D, HG_D), BF16)],
        compiler_params=_cparams("arbitrary", "arbitrary"))


GH0 = (IN_W - HG_W) // HG_W


def _mix_out(o_a, nums, dens, mxs, oh, proj, hg, *, name):
    s = o_a.shape[0]
    t = _rows(s, TE)

    def body(oa_ref, n0, n1, n2, d0, d1, d2, m0, m1, m2, oh_ref, gh_ref, hg_ref, y_ref, od_ref, mall_ref, z_ref):
        y_ref[:, :SB_W] = oa_ref[...].astype(BF16)
        m = jnp.maximum(jnp.maximum(m0[...], m1[...]), m2[...])
        num = jnp.zeros((t, DIL_W), F32)
        z = jnp.zeros((t, DIL_W), F32)
        for n_ref, d_ref, m_ref in ((n0, d0, m0), (n1, d1, m1), (n2, d2, m2)):
            sc = jnp.exp(m_ref[...] - m)
            num = num + n_ref[...] * sc
            z = z + d_ref[...] * sc
        od = num / z
        od_ref[...] = od
        mall_ref[...] = m
        z_ref[...] = z
        y_ref[:, SB_W:SB_W + DIL_W] = od.astype(BF16)
        for h in range(4):
            sl = slice(h * HG_D, (h + 1) * HG_D)
            ov = oh_ref[:, sl]
            g = gh_ref[:, sl]
            r = lax.rsqrt(jnp.mean(ov * ov, axis=1, keepdims=True) + EPS)
            y_ref[:, SB_W + DIL_W + h * HG_D:SB_W + DIL_W + (h + 1) * HG_D] = (ov * r * hg_ref[...] * (g * _sigmoid(g))).astype(BF16)

    rd = _row_spec(t, DIL_W)
    return _pcall(
        body, name=name, grid=(s // t,),
        in_specs=[rd] * 10 + [_row_spec(t, HG_W), _row_spec(t, HG_W, GH0), _vec_spec(HG_D)],
        out_specs=[_row_spec(t, MIX_W), rd, rd, rd],
        out_shape=[jax.ShapeDtypeStruct((s, MIX_W), BF16)] + [jax.ShapeDtypeStruct((s, DIL_W), F32)] * 3,
        compiler_params=_cparams("parallel"))(o_a, *nums, *dens, *mxs, oh, proj, hg)


def _mix_out_bwd(dmix, oh, proj, hg, od, *, name):
    s = oh.shape[0]
    t = _rows(s, TE)

    def body(dm_ref, oh_ref, gh_ref, hg_ref, od_ref, doh_ref, dgh_ref, dl_ref, dhg_ref):
        @pl.when(pl.program_id(0) == 0)
        def _():
            dhg_ref[...] = jnp.zeros_like(dhg_ref)

        seg = _seg_consts()
        for j in range(2):
            sl = slice(j * LANES, (j + 1) * LANES)
            dl_ref[:, sl] = _xdot(dm_ref[:, SB_W + j * LANES:SB_W + (j + 1) * LANES] * od_ref[:, sl], seg, 3)
        hgv = hg_ref[...]
        for h in range(4):
            sl = slice(h * HG_D, (h + 1) * HG_D)
            dy = dm_ref[:, SB_W + DIL_W + h * HG_D:SB_W + DIL_W + (h + 1) * HG_D]
            ov = oh_ref[:, sl]
            g = gh_ref[:, sl]
            sg = _sigmoid(g)
            silu = g * sg
            r = lax.rsqrt(jnp.mean(ov * ov, axis=1, keepdims=True) + EPS)
            nrm = ov * r
            dhg_ref[...] += jnp.sum(dy * nrm * silu, axis=0, keepdims=True)
            dgh_ref[:, sl] = dy * nrm * hgv * (sg * (1.0 + g * (1.0 - sg)))
            dn = dy * hgv * silu
            doh_ref[:, sl] = r * (dn - nrm * jnp.mean(dn * nrm, axis=1, keepdims=True))

    rh = _row_spec(t, HG_W)
    return _pcall(
        body, name=name, grid=(s // t,),
        in_specs=[_row_spec(t, MIX_W), rh, _row_spec(t, HG_W, GH0), _vec_spec(HG_D), _row_spec(t, DIL_W)],
        out_specs=[rh, rh, _row_spec(t, DIL_W), _vec_spec(HG_D)],
        out_shape=[jax.ShapeDtypeStruct((s, HG_W), F32)] * 2 + [jax.ShapeDtypeStruct((s, DIL_W), F32), jax.ShapeDtypeStruct((1, HG_D), F32)],
        compiler_params=_cparams("arbitrary"))(dmix, oh, proj, hg, od)


def _lb_terms(l):
    l0, l1 = l[0:1], l[1:2]
    m = jnp.maximum(l0, l1)
    e0, e1 = jnp.exp(l0 - m), jnp.exp(l1 - m)
    s0, s1 = e0 / (e0 + e1), e1 / (e0 + e1)
    args = (s0 - s0, (s0 + s1) - s0)
    lbs = tuple(jnp.minimum(jnp.maximum(a, 0.0), 1.0 - EPS) for a in args)
    return s0, s1, args, lbs


def _lb_prep(logits, *, name):
    def body(l_ref, lb_ref, la_ref, lc_ref):
        _, _, _, lbs = _lb_terms(l_ref[...])
        lb = jnp.concatenate(lbs, axis=0)
        lb_ref[...] = lb
        la_ref[...] = jnp.log(jnp.maximum(lb, LB_FLOOR))
        lc_ref[...] = jnp.log1p(-lb)

    return _pcall(body, name=name, out_shape=[jax.ShapeDtypeStruct(logits.shape, F32)] * 3)(logits)


def _lb_bwd(logits, dla, dlc, *, name):
    def half(hi, eq):
        return jnp.where(hi, 1.0, jnp.where(eq, 0.5, 0.0))

    def body(l_ref, dla_ref, dlc_ref, o_ref):
        s0, s1, args, lbs = _lb_terms(l_ref[...])
        da = []
        for i in range(2):
            a, lb = args[i], lbs[i]
            dlb = dla_ref[i:i + 1] * half(lb > LB_FLOOR, lb == LB_FLOOR) / jnp.maximum(lb, LB_FLOOR) - dlc_ref[i:i + 1] / (1.0 - lb)
            t = jnp.maximum(a, 0.0)
            da.append(dlb * half(a > 0.0, a == 0.0) * half(t < 1.0 - EPS, t == 1.0 - EPS))
        ds0 = (da[0] + da[1]) - (da[0] + da[1])
        ds1 = da[1]
        dot = s0 * ds0 + s1 * ds1
        o_ref[...] = jnp.concatenate([s0 * (ds0 - dot), s1 * (ds1 - dot)], axis=0)

    return _pcall(body, name=name, out_shape=jax.ShapeDtypeStruct(logits.shape, F32))(logits, dla, dlc)


def _mod_fwd(c8, w, b, *, name):
    _, d, n = w.shape
    tn = _tile(n, 768)

    def body(c_ref, w_ref, b_ref, o_ref):
        cv = c_ref[...]
        o_ref[...] = _dot((cv * _sigmoid(cv)).astype(BF16), w_ref[...].astype(BF16)) + b_ref[...]

    return _pcall(
        body, name=name, grid=(2, n // tn),
        in_specs=[pl.BlockSpec((8, d), lambda l, j: (0, 0)), pl.BlockSpec((None, d, tn), lambda l, j: (l, 0, j)),
                  pl.BlockSpec((None, 1, tn), lambda l, j: (l, 0, j))],
        out_specs=pl.BlockSpec((None, 8, tn), lambda l, j: (l, 0, j)),
        out_shape=jax.ShapeDtypeStruct((2, 8, n), F32), compiler_params=_cparams("parallel", "parallel"))(c8, w, b)


def _mod_bwd(ct, dm, *, name):
    d = ct.shape[0]
    n = dm.shape[2]
    tn = _tile(n, 768)

    def body(c_ref, dm_ref, o_ref):
        cv = c_ref[...]
        sc = cv * _sigmoid(cv)
        dv = dm_ref[...]
        acc = sc[:, 0:1] * dv[0:1, :]
        for b in range(1, 8):
            acc = acc + sc[:, b:b + 1] * dv[b:b + 1, :]
        o_ref[...] = acc

    return _pcall(
        body, name=name, grid=(2, n // tn),
        in_specs=[pl.BlockSpec((d, 8), lambda l, j: (0, 0)), pl.BlockSpec((None, 8, tn), lambda l, j: (l, 0, j))],
        out_specs=pl.BlockSpec((None, d, tn), lambda l, j: (l, 0, j)),
        out_shape=jax.ShapeDtypeStruct((2, d, n), F32), compiler_params=_cparams("parallel", "parallel"))(ct, dm)


_PEERS = {
    "chips": ((1, 0, 0), (0, 1, 0), (1, 1, 0)),
    "all": tuple((a, b, c) for a in (0, 1) for b in (0, 1) for c in (0, 1) if a + b + c),
    "sib": ((0, 0, 1),),
}
_SLOTS = {"chips": 4, "all": 8, "sib": 2}


def _slot(kind, x, y, c):
    return {"chips": 2 * x + y, "all": 4 * x + 2 * y + c, "sib": c}[kind]


class _Xchg:
    def __init__(self, arrs, kind, scatter):
        self.arrs, self.kind, self.scatter = list(arrs), kind, scatter
        self.n = len(self.arrs)
        self.peers = _PEERS[kind]
        self.chunks = [self._pieces(a) for a in self.arrs]
        npeer = len(self.peers)
        self.base = [sum(len(c) for c in self.chunks[:a]) * npeer for a in range(self.n)]
        total = sum(len(c) for c in self.chunks) * npeer
        self.specs = [pl.BlockSpec(memory_space=pl.ANY)] * self.n
        self.out_shape = [jax.ShapeDtypeStruct(a.shape if scatter else (_SLOTS[kind],) + a.shape, a.dtype) for a in self.arrs]
        self.scratch = [pltpu.SemaphoreType.DMA((total,)), pltpu.SemaphoreType.DMA((total,)), pltpu.SemaphoreType.DMA((self.n,))]

    def _pieces(self, a):
        shape = a.shape[1:] if self.scatter else a.shape
        if len(shape) == 2:
            for k in (XCHG_CHUNKS, XCHG_CHUNKS // 2, XCHG_CHUNKS // 4):
                if k > 1 and shape[0] % (16 * k) == 0 and shape[0] * shape[1] * a.dtype.itemsize >= k * XCHG_MIN_BYTES:
                    return [(i * (shape[0] // k), shape[0] // k) for i in range(k)]
        return [None]

    def copies(self, ins, outs, send, recv, loc):
        kind, scatter = self.kind, self.scatter
        x, y, c = lax.axis_index("x"), lax.axis_index("y"), lax.axis_index("c")
        me = _slot(kind, x, y, c)
        out = []
        for a in range(self.n):
            out.append(pltpu.make_async_copy(ins[a].at[me] if scatter else ins[a], outs[a].at[me], loc.at[a]))
            for j, (dx, dy, dc) in enumerate(self.peers):
                px, py, pc = (1 - x if dx else x), (1 - y if dy else y), (1 - c if dc else c)
                src = ins[a].at[_slot(kind, px, py, pc)] if scatter else ins[a]
                for i, piece in enumerate(self.chunks[a]):
                    rows = slice(None) if piece is None else pl.ds(piece[0], piece[1])
                    sem = self.base[a] + j * len(self.chunks[a]) + i
                    out.append(pltpu.make_async_remote_copy(
                        src_ref=src if piece is None else src.at[rows], dst_ref=outs[a].at[me] if piece is None else outs[a].at[me, rows],
                        send_sem=send.at[sem], recv_sem=recv.at[sem], device_id=(px, py, pc), device_id_type=MESH_ID))
        return out


def _exchange(arrs, kind, scatter, *, name):
    xc = _Xchg(arrs, kind, scatter)

    def body(*refs):
        copies = xc.copies(refs[:xc.n], refs[xc.n:2 * xc.n], *refs[2 * xc.n:])
        for cp in copies:
            cp.start()
        for cp in copies:
            cp.wait()

    return _pcall(body, name=name, in_specs=xc.specs, out_specs=xc.specs, out_shape=xc.out_shape, scratch_shapes=xc.scratch)(*arrs)


def _with_exchange(body, n_in, n_out, n_scr, xc, first, last):
    def wrapped(*refs):
        ins, side_in = refs[:n_in], refs[n_in:n_in + xc.n]
        o0 = n_in + xc.n
        outs, side_out = refs[o0:o0 + n_out], refs[o0 + n_out:o0 + n_out + xc.n]
        s0 = o0 + n_out + xc.n
        scr, sems = refs[s0:s0 + n_scr], refs[s0 + n_scr:]

        @pl.when(first())
        def _():
            for cp in xc.copies(side_in, side_out, *sems):
                cp.start()

        body(*ins, *outs, *scr)

        @pl.when(last())
        def _():
            for cp in xc.copies(side_in, side_out, *sems):
                cp.wait()

    return wrapped


def _call_with_exchange(body, xc, first, last, args, *, in_specs, out_specs, out_shape, scratch_shapes, **kw):
    if xc is None:
        return _pcall(body, in_specs=in_specs, out_specs=out_specs, out_shape=out_shape, scratch_shapes=scratch_shapes, **kw)(*args), None
    wrapped = _with_exchange(body, len(in_specs), len(out_specs), len(scratch_shapes), xc, first, last)
    res = _pcall(wrapped, in_specs=list(in_specs) + xc.specs, out_specs=list(out_specs) + xc.specs,
                 out_shape=list(out_shape) + xc.out_shape, scratch_shapes=list(scratch_shapes) + xc.scratch, **kw)(*args, *xc.arrs)
    return res[:len(out_specs)], res[len(out_specs):]


def _sum_slots(a, *, name, out_dtype=F32):
    ns, r, c = a.shape
    t = _tile(r, max(16, (1 << 18) // c // 16 * 16), 16)

    def body(a_ref, o_ref):
        acc = a_ref[0].astype(F32)
        for i in range(1, ns):
            acc = acc + a_ref[i].astype(F32)
        o_ref[...] = acc.astype(o_ref.dtype)

    return _pcall(body, name=name, grid=(r // t,), in_specs=[pl.BlockSpec((ns, t, c), lambda i: (0, i, 0))],
                  out_specs=pl.BlockSpec((t, c), lambda i: (i, 0)), out_shape=jax.ShapeDtypeStruct((r, c), out_dtype),
                  compiler_params=_cparams("parallel"))(a)


def _adamw(w, gparts, m, v, *, name):
    r, c = w.shape
    t = _tile(r, max(16, (1 << 17) // c // 16 * 16), 16)
    ng = len(gparts)

    def body(*refs):
        w_ref, m_ref, v_ref = refs[0], refs[1 + ng], refs[2 + ng]
        g_ref, d_ref, nm_ref, nv_ref = refs[3 + ng:]
        g = refs[1][...].astype(F32)
        for i in range(1, ng):
            g = g + refs[1 + i][...].astype(F32)
        mn = ADAM_B1 * m_ref[...] + (1.0 - ADAM_B1) * g
        vn = ADAM_B2 * v_ref[...] + (1.0 - ADAM_B2) * (g * g)
        m_hat = mn / (1.0 - ADAM_B1 ** ADAM_STEP)
        v_hat = vn / (1.0 - ADAM_B2 ** ADAM_STEP)
        g_ref[...] = g
        d_ref[...] = -ADAM_LR * (m_hat / (jnp.sqrt(v_hat) + ADAM_EPS) + ADAM_WD * w_ref[...])
        nm_ref[...] = mn
        nv_ref[...] = vn

    spec = pl.BlockSpec((t, c), lambda i: (i, 0))
    return _pcall(body, name=name, grid=(r // t,), in_specs=[spec] * (3 + ng), out_specs=[spec] * 4,
                  out_shape=[jax.ShapeDtypeStruct((r, c), F32)] * 4, compiler_params=_cparams("parallel"))(w, *gparts, m, v)


def _adamw_layers(w, halves, m, v, *, name):
    r, c = w.shape
    nl = len(halves)
    rl = r // nl
    t = _tile(rl, max(16, (1 << 17) // c // 16 * 16), 16)
    nbl = rl // t

    def body(*refs):
        w_ref, m_ref, v_ref = refs[0], refs[1 + 2 * nl], refs[2 + 2 * nl]
        g_ref, d_ref, nm_ref, nv_ref = refs[3 + 2 * nl:]
        g = None
        for l in range(nl):
            gl = refs[1 + 2 * l][...].astype(F32) + refs[2 + 2 * l][...].astype(F32)
            g = gl if g is None else jnp.where(pl.program_id(0) >= l * nbl, gl, g)
        mn = ADAM_B1 * m_ref[...] + (1.0 - ADAM_B1) * g
        vn = ADAM_B2 * v_ref[...] + (1.0 - ADAM_B2) * (g * g)
        m_hat = mn / (1.0 - ADAM_B1 ** ADAM_STEP)
        v_hat = vn / (1.0 - ADAM_B2 ** ADAM_STEP)
        g_ref[...] = g
        d_ref[...] = -ADAM_LR * (m_hat / (jnp.sqrt(v_hat) + ADAM_EPS) + ADAM_WD * w_ref[...])
        nm_ref[...] = mn
        nv_ref[...] = vn

    spec = pl.BlockSpec((t, c), lambda i: (i, 0))
    part = lambda l, core: pl.BlockSpec((None, t, c), lambda i, l=l, core=core: (core, jnp.clip(i - l * nbl, 0, nbl - 1), 0))
    gspecs = [part(l, core) for l in range(nl) for core in range(2)]
    gargs = [halves[l] for l in range(nl) for _ in range(2)]
    return _pcall(body, name=name, grid=(r // t,), in_specs=[spec] + gspecs + [spec, spec], out_specs=[spec] * 4,
                  out_shape=[jax.ShapeDtypeStruct((r, c), F32)] * 4, compiler_params=_cparams("parallel"))(w, *gargs, m, v)


FFN_TM = 512
FFN_CHUNK = 1408


def _resident(shape):
    return pl.BlockSpec(shape, lambda i: (0,) * len(shape), pipeline_mode=pl.Buffered(1))


def _ffn_up(x, sc, sh, wgu, *, name):
    s, d = x.shape
    f = wgu.shape[1] // 2
    t, fc = _rows(s, FFN_TM), _tile(f, FFN_CHUNK)

    def body(x_ref, sc_ref, sh_ref, w_ref, h_ref, uv_ref, a_ref):
        xv = x_ref[...]
        r = lax.rsqrt(jnp.mean(xv * xv, axis=1, keepdims=True) + EPS)
        hb = ((xv * r) * (1.0 + sc_ref[...]) + sh_ref[...]).astype(BF16)
        h_ref[...] = hb
        for j in range(f // fc):
            u = _dot(hb, w_ref[:, j * fc:(j + 1) * fc])
            v = _dot(hb, w_ref[:, f + j * fc:f + (j + 1) * fc])
            sg = _sigmoid(u)
            silu = u * sg
            uv_ref[:, j * fc:(j + 1) * fc] = (v * (sg * (1.0 + u * (1.0 - sg)))).astype(BF16)
            uv_ref[:, f + j * fc:f + (j + 1) * fc] = silu.astype(BF16)
            a_ref[:, j * fc:(j + 1) * fc] = (silu * v).astype(BF16)

    return _pcall(
        body, name=name, grid=(s // t,), in_specs=[_row_spec(t, d), _vec_spec(d), _vec_spec(d), _resident(wgu.shape)],
        out_specs=[_row_spec(t, d), _row_spec(t, 2 * f), _row_spec(t, f)],
        out_shape=[jax.ShapeDtypeStruct((s, d), BF16), jax.ShapeDtypeStruct((s, 2 * f), BF16), jax.ShapeDtypeStruct((s, f), BF16)],
        compiler_params=_cparams("parallel"))(x, sc, sh, wgu)


def _norm_mm(x, sc, sh, w, nb, *, name):
    s, d = x.shape
    n = w.shape[1]
    t, nc = _rows(s, FFN_TM), _tile(n, 1792)
    assert nb <= nc

    def body(x_ref, sc_ref, sh_ref, w_ref, h_ref, o_ref, ob_ref):
        xv = x_ref[...]
        r = lax.rsqrt(jnp.mean(xv * xv, axis=1, keepdims=True) + EPS)
        hb = ((xv * r) * (1.0 + sc_ref[...]) + sh_ref[...]).astype(BF16)
        h_ref[...] = hb
        for j in range(n // nc):
            part = _dot(hb, w_ref[:, j * nc:(j + 1) * nc])
            o_ref[:, j * nc:(j + 1) * nc] = part
            if j == 0:
                ob_ref[...] = part[:, :nb].astype(BF16)

    return _pcall(
        body, name=name, grid=(s // t,), in_specs=[_row_spec(t, d), _vec_spec(d), _vec_spec(d), _resident(w.shape)],
        out_specs=[_row_spec(t, d), _row_spec(t, n), _row_spec(t, nb)],
        out_shape=[jax.ShapeDtypeStruct((s, d), BF16), jax.ShapeDtypeStruct((s, n), F32), jax.ShapeDtypeStruct((s, nb), BF16)],
        compiler_params=_cparams("parallel"))(x, sc, sh, w)


def _ffn_dact(dxo, y, sg, wd, uv, *, name):
    s, d = y.shape
    f = wd.shape[0]
    t, fc = _rows(s, FFN_TM), _tile(f, FFN_CHUNK)

    def body(dxo_ref, y_ref, sg_ref, w_ref, uv_ref, dy_ref, duv_ref, ds_ref):
        @pl.when(pl.program_id(0) == 0)
        def _():
            ds_ref[...] = jnp.zeros_like(ds_ref)

        dv = dxo_ref[...]
        dyb = (sg_ref[...] * dv).astype(BF16)
        dy_ref[...] = dyb
        ds_ref[...] += jnp.sum(dv * y_ref[...], axis=0, keepdims=True)
        for j in range(f // fc):
            da = _dot(dyb, w_ref[j * fc:(j + 1) * fc, :], NT)
            duv_ref[:, j * fc:(j + 1) * fc] = (da * uv_ref[:, j * fc:(j + 1) * fc].astype(F32)).astype(BF16)
            duv_ref[:, f + j * fc:f + (j + 1) * fc] = (da * uv_ref[:, f + j * fc:f + (j + 1) * fc].astype(F32)).astype(BF16)

    return _pcall(
        body, name=name, grid=(s // t,),
        in_specs=[_row_spec(t, d), _row_spec(t, d), _vec_spec(d), _resident(wd.shape), _row_spec(t, 2 * f)],
        out_specs=[_row_spec(t, d), _row_spec(t, 2 * f), _vec_spec(d)],
        out_shape=[jax.ShapeDtypeStruct((s, d), BF16), jax.ShapeDtypeStruct((s, 2 * f), BF16), jax.ShapeDtypeStruct((1, d), F32)],
        compiler_params=_cparams("arbitrary"))(dxo, y, sg, wd, uv)


def _ffn_dh(duv, wgu, x, sc, dxo, *, name):
    s, d = x.shape
    f2 = wgu.shape[1]
    t = _rows(s, FFN_TM)

    def body(duv_ref, w_ref, x_ref, sc_ref, dxo_ref, dx_ref, dsc_ref, dsh_ref):
        @pl.when(pl.program_id(0) == 0)
        def _():
            dsc_ref[...] = jnp.zeros_like(dsc_ref)
            dsh_ref[...] = jnp.zeros_like(dsh_ref)

        dhv = _dot(duv_ref[...], w_ref[...], NT)
        xv = x_ref[...]
        r = lax.rsqrt(jnp.mean(xv * xv, axis=1, keepdims=True) + EPS)
        xn = xv * r
        dxn = dhv * (1.0 + sc_ref[...])
        dx_ref[...] = dxo_ref[...] + r * (dxn - xn * jnp.mean(dxn * xn, axis=1, keepdims=True))
        dsc_ref[...] += jnp.sum(dhv * xn, axis=0, keepdims=True)
        dsh_ref[...] += jnp.sum(dhv, axis=0, keepdims=True)

    return _pcall(
        body, name=name, grid=(s // t,),
        in_specs=[_row_spec(t, f2), _resident(wgu.shape), _row_spec(t, d), _vec_spec(d), _row_spec(t, d)],
        out_specs=[_row_spec(t, d), _vec_spec(d), _vec_spec(d)],
        out_shape=[jax.ShapeDtypeStruct((s, d), F32), jax.ShapeDtypeStruct((1, d), F32), jax.ShapeDtypeStruct((1, d), F32)],
        compiler_params=_cparams("arbitrary"))(duv, wgu, x, sc, dxo)


def _dh_pieces(pieces, w, x, sc, dxo, *, name):
    s, d = x.shape
    t = _rows(s, FFN_TM)
    widths = [p.shape[1] for p in pieces]
    offs = [sum(widths[:i]) for i in range(len(widths))]
    kd = sum(widths)
    npc = len(pieces)

    def body(*refs):
        p_refs = refs[:npc]
        w_ref, x_ref, sc_ref, dxo_ref, dx_ref, dsc_ref, dsh_ref, cat_ref = refs[npc:]

        @pl.when(pl.program_id(0) == 0)
        def _():
            dsc_ref[...] = jnp.zeros_like(dsc_ref)
            dsh_ref[...] = jnp.zeros_like(dsh_ref)

        dhv = None
        for p_ref, off, wd in zip(p_refs, offs, widths):
            pb = p_ref[...].astype(BF16)
            cat_ref[:, off:off + wd] = pb
            part = _dot(pb, w_ref[:, off:off + wd], NT)
            dhv = part if dhv is None else dhv + part
        xv = x_ref[...]
        r = lax.rsqrt(jnp.mean(xv * xv, axis=1, keepdims=True) + EPS)
        xn = xv * r
        dxn = dhv * (1.0 + sc_ref[...])
        dx_ref[...] = dxo_ref[...] + r * (dxn - xn * jnp.mean(dxn * xn, axis=1, keepdims=True))
        dsc_ref[...] += jnp.sum(dhv * xn, axis=0, keepdims=True)
        dsh_ref[...] += jnp.sum(dhv, axis=0, keepdims=True)

    return _pcall(
        body, name=name, grid=(s // t,),
        in_specs=[_row_spec(t, wd) for wd in widths] + [_resident(w.shape), _row_spec(t, d), _vec_spec(d), _row_spec(t, d)],
        out_specs=[_row_spec(t, d), _vec_spec(d), _vec_spec(d), _row_spec(t, kd)],
        out_shape=[jax.ShapeDtypeStruct((s, d), F32), jax.ShapeDtypeStruct((1, d), F32), jax.ShapeDtypeStruct((1, d), F32),
                   jax.ShapeDtypeStruct((s, kd), BF16)],
        compiler_params=_cparams("arbitrary"))(*pieces, w, x, sc, dxo)


def _ffn_fwd(x, sh, sc, g, wgu, wd, tag):
    h, uv, a = _ffn_up(x, sc, sh, wgu, name=f"{tag}_up")
    y, xo = _mm(a, wd, name=f"{tag}_down", tm=512, tn=1024, tk=wd.shape[0], res=x, scale=0.5 * g)
    return xo, (x, h, uv, a, y)


def _ffn_bwd(dxo, saved, sc, g, wgu, wd, tag):
    x, h, uv, a, y = saved
    dyb, duv, dgs = _ffn_dact(dxo, y, 0.5 * g, wd, uv, name=f"{tag}_dact")
    dx, dsc, dsh = _ffn_dh(duv, wgu, x, sc, dxo, name=f"{tag}_dh")
    dwgu = _mm_tn(h, duv, name=f"{tag}_dwgu", tm=1024, tn=1408, tk=512)
    dwd = _mm_tn(a, dyb, name=f"{tag}_dwd", tm=1408, tn=1024, tk=512)
    return dx, dwgu, dwd, dsh, dsc, 0.5 * dgs


def _layer_fwd(x0, mod, w, par, tag, carry=None):
    s = x0.shape[0]
    sh1, sc1, g1, sh2, sc2, g2, sh3, sc3, g3 = (mod[i:i + 1] for i in range(N_MOD))
    x1, f1 = _ffn_fwd(x0, sh1, sc1, g1, w["gu1"], w["d1"], f"{tag}_ffn1")
    h2, proj, qkv = _norm_mm(x1, sc2, sh2, w["in"], 3 * SB_W, name=f"{tag}_in")
    o_a, got_a = _sb_fwd(qkv, name=f"{tag}_sb", side=None if carry is None else _Xchg(carry[0], "chips", False))
    qd, kd = _dil_prep(proj, par["gq"], par["gk"], par["cos"], par["sin"], name=f"{tag}_dil_prep")
    nums, dens, mxs = [], [], []
    for _, r in DIL_PATTERNS:
        nu, de, mx = _dil_fwd(qd, kd, proj, r, name=f"{tag}_dil{r}")
        nums.append(nu)
        dens.append(de)
        mxs.append(mx)
    (oh, st), got_b = _hgrn_fwd(proj, par["la"], par["lc"], name=f"{tag}_hgrn",
                                side=None if carry is None else _Xchg(carry[1], "chips", False))
    ymix, od, mall, zall = _mix_out(o_a, nums, dens, mxs, oh, proj, par["hg"], name=f"{tag}_mix_out")
    out, x2 = _mm(ymix, w["out"], name=f"{tag}_out", tm=512, tn=1024, tk=1024, res=x1, scale=g2)
    x3, f2 = _ffn_fwd(x2, sh3, sc3, g3, w["gu2"], w["d2"], f"{tag}_ffn2")
    return x3, dict(f1=f1, f2=f2, x1=x1, h2=h2, proj=proj, qkv=qkv, o_a=o_a, qd=qd, kd=kd, oh=oh, st=st,
                    ymix=ymix, od=od, mall=mall, zall=zall, out=out), (None if carry is None else (got_a, got_b))


COL_SHARDED = ("ffn1_w_gate", "ffn1_w_up", "w_in", "ffn2_w_gate", "ffn2_w_up")
ROW_SHARDED = ("ffn1_w_down", "w_out", "ffn2_w_down")


def _by_chip(name, g):
    if name in COL_SHARDED:
        return jnp.moveaxis(g.reshape(g.shape[0], 4, -1), 1, 0)
    return g.reshape(4, -1, g.shape[1])


def _halves(x):
    f = x.shape[1] // 2
    return x[:, :f], x[:, f:]


def _layer_bwd(dx3, sv, mod, w, par, tag, pending):
    sh1, sc1, g1, sh2, sc2, g2, sh3, sc3, g3 = (mod[i:i + 1] for i in range(N_MOD))
    dx2, dwgu2, dwd2, dsh3, dsc3, dg3 = _ffn_bwd(dx3, sv["f2"], sc3, g3, w["gu2"], w["d2"], f"{tag}_ffn2")
    doutb, dg2 = _gate_bwd(dx2, sv["out"], g2, name=f"{tag}_dgate2")
    dmix = _mm(doutb, w["out"], name=f"{tag}_dmix", tb=True, tm=512, tn=1024, tk=1024)
    dwout = _mm_tn(sv["ymix"], doutb, name=f"{tag}_dwout", tm=1024, tn=1024, tk=512)
    proj = sv["proj"]
    doh, dgh, delta, dhg = _mix_out_bwd(dmix, sv["oh"], proj, par["hg"], sv["od"], name=f"{tag}_dmix_out")
    dgate2, dup2 = _halves(dwgu2)
    ride = dict(pending)
    for n, g in (("ffn2_w_gate", dgate2), ("ffn2_w_up", dup2), ("ffn2_w_down", dwd2), ("w_out", dwout)):
        ride[(tag, n)] = _by_chip(n, g)
    keys = list(ride)
    (dqa, dka, dva), got = _sb_bwd(sv["qkv"], sv["o_a"], dmix, name=f"{tag}_dsb", side=_Xchg([ride[k] for k in keys], "chips", True))
    parts = [_sum_slots(g, name=f"{tag}_sum{i}", out_dtype=BF16) for i, g in enumerate(got)]
    dqs, dks, dvs = [], [], []
    for _, r in DIL_PATTERNS:
        a, b, c = _dil_bwd(sv["qd"], sv["kd"], proj, dmix, sv["mall"], sv["zall"], delta, r, name=f"{tag}_ddil{r}")
        dqs.append(a)
        dks.append(b)
        dvs.append(c)
    dqd, dkd, dvd, dgq, dgk = _dil_prep_bwd(proj, par["gq"], par["gk"], par["cos"], par["sin"], dqs, dks, dvs,
                                             name=f"{tag}_ddil_prep")
    (dqh, dfh, dih, dla, dlc), swapped = _hgrn_bwd(proj, par["la"], par["lc"], sv["st"], doh, name=f"{tag}_dhgrn",
                                                   side=_Xchg(parts, "sib", False))
    dx1, dsc2, dsh2, dproj = _dh_pieces([dqa, dka, dva, dqd, dkd, dvd, dqh, dfh, dih, dgh], w["in"], sv["x1"], sc2, dx2,
                                         name=f"{tag}_dh2")
    dwin = _mm_tn(sv["h2"], dproj, name=f"{tag}_dwin", tm=1024, tn=1792, tk=512)
    dx0, dwgu1, dwd1, dsh1, dsc1, dg1 = _ffn_bwd(dx1, sv["f1"], sc1, g1, w["gu1"], w["d1"], f"{tag}_ffn1")
    dmod = jnp.concatenate([dsh1, dsc1, dg1, dsh2, dsc2, dg2, dsh3, dsc3, dg3], axis=0)
    fold = lambda v: v.reshape(2, HEAD_DIM).sum(axis=0)
    dgate1, dup1 = _halves(dwgu1)
    late = {(tag, n): _by_chip(n, g) for n, g in (("ffn1_w_gate", dgate1), ("ffn1_w_up", dup1), ("ffn1_w_down", dwd1), ("w_in", dwin))}
    grads = dict(late=late, dmod=dmod, gq=fold(dgq), gk=fold(dgk), hg=dhg[0], la=dla[0], lc=dlc[0])
    return dx0, grads, dict(zip(keys, swapped))


def _pack(pieces):
    flat = jnp.concatenate([p.reshape(-1) for p in pieces])
    pad = (-flat.shape[0]) % (8 * LANES)
    return jnp.pad(flat, (0, pad)).reshape(-1, LANES)


def _unpack(flat, like):
    out, off = [], 0
    for p in like:
        out.append(flat[off:off + p.size].reshape(p.shape))
        off += p.size
    return out


def kernel(x, c, w_mod, b_mod, ffn1_w_gate, ffn1_w_up, ffn1_w_down, w_in, w_out, q_norm_g, k_norm_g, hgrn_norm_g, hgrn_lb_logits, ffn2_w_gate, ffn2_w_up, ffn2_w_down, loss_target, m_w_mod, m_b_mod, m_ffn1_w_gate, m_ffn1_w_up, m_ffn1_w_down, m_w_in, m_w_out, m_q_norm_g, m_k_norm_g, m_hgrn_norm_g, m_hgrn_lb_logits, m_ffn2_w_gate, m_ffn2_w_up, m_ffn2_w_down, v_w_mod, v_b_mod, v_ffn1_w_gate, v_ffn1_w_up, v_ffn1_w_down, v_w_in, v_w_out, v_q_norm_g, v_k_norm_g, v_hgrn_norm_g, v_hgrn_lb_logits, v_ffn2_w_gate, v_ffn2_w_up, v_ffn2_w_down):
    names = ["w_mod", "b_mod", "ffn1_w_gate", "ffn1_w_up", "ffn1_w_down", "w_in", "w_out", "q_norm_g", "k_norm_g",
             "hgrn_norm_g", "hgrn_lb_logits", "ffn2_w_gate", "ffn2_w_up", "ffn2_w_down"]
    wts = dict(zip(names, (w_mod, b_mod, ffn1_w_gate, ffn1_w_up, ffn1_w_down, w_in, w_out, q_norm_g, k_norm_g, hgrn_norm_g,
                           hgrn_lb_logits, ffn2_w_gate, ffn2_w_up, ffn2_w_down)))
    mom = dict(zip(names, (m_w_mod, m_b_mod, m_ffn1_w_gate, m_ffn1_w_up, m_ffn1_w_down, m_w_in, m_w_out, m_q_norm_g, m_k_norm_g,
                           m_hgrn_norm_g, m_hgrn_lb_logits, m_ffn2_w_gate, m_ffn2_w_up, m_ffn2_w_down)))
    var = dict(zip(names, (v_w_mod, v_b_mod, v_ffn1_w_gate, v_ffn1_w_up, v_ffn1_w_down, v_w_in, v_w_out, v_q_norm_g, v_k_norm_g,
                           v_hgrn_norm_g, v_hgrn_lb_logits, v_ffn2_w_gate, v_ffn2_w_up, v_ffn2_w_down)))
    depth = w_mod.shape[0]
    assert depth == 2 and x.shape[0] == 1
    s, d = x.shape[1:]
    assert s % (DIL_PATTERNS[-1][1] * QBLK) == 0 and d % LANES == 0
    xi, yi, ci = lax.axis_index("x"), lax.axis_index("y"), lax.axis_index("c")
    chip = 2 * xi + yi
    dev = 2 * chip + ci
    x0, tgt = x[0], loss_target[0]

    c8 = _exchange([c.reshape(d // LANES, LANES)], "all", False, name="gather_c")[0].reshape(8, d)
    ncol = w_mod.shape[2]
    b_loc = lax.dynamic_slice_in_dim(b_mod, chip * ncol, ncol, axis=1)
    m_loc = _mod_fwd(c8, w_mod, b_loc.reshape(depth, 1, ncol), name="mod_fwd")
    m_all = _exchange([m_loc], "chips", False, name="gather_mod")[0]
    mod = jnp.transpose(lax.dynamic_index_in_dim(m_all, dev, axis=2, keepdims=False), (1, 0, 2)).reshape(depth, N_MOD, d)

    col_sharded = ["ffn1_w_gate", "ffn1_w_up", "w_in", "ffn2_w_gate", "ffn2_w_up"]
    row_sharded = ["ffn1_w_down", "w_out", "ffn2_w_down"]
    big = col_sharded + row_sharded
    early = ["ffn1_w_gate", "ffn1_w_up", "ffn1_w_down", "w_in", "w_out"]
    late = ["ffn2_w_gate", "ffn2_w_up", "ffn2_w_down"]

    def shards(l, group):
        return [wts[n][l].astype(BF16) for n in group]

    def whole(group, gathered):
        out = {}
        for n, g in zip(group, gathered):
            out[n] = jnp.moveaxis(g, 0, 1).reshape(g.shape[1], -1) if n in col_sharded else g.reshape(-1, g.shape[2])
        return out

    def layer_weights(full):
        return dict(gu1=jnp.concatenate([full["ffn1_w_gate"], full["ffn1_w_up"]], axis=1), d1=full["ffn1_w_down"],
                    gu2=jnp.concatenate([full["ffn2_w_gate"], full["ffn2_w_up"]], axis=1), d2=full["ffn2_w_down"],
                    **{"in": full["w_in"], "out": full["w_out"]})

    ws = [layer_weights(whole(early + late, _exchange(shards(0, early + late), "chips", False, name="gather_w")))]

    _, la, lc = _lb_prep(hgrn_lb_logits, name="lb_prep")
    cos, sin = _rope_tables(s)
    pars = [dict(gq=jnp.tile(q_norm_g[l], 2)[None], gk=jnp.tile(k_norm_g[l], 2)[None], hg=hgrn_norm_g[l][None],
                 la=la[l:l + 1], lc=lc[l:l + 1], cos=cos, sin=sin) for l in range(depth)]

    xs, saved = x0, []
    for l in range(depth):
        carry = (shards(l + 1, early), shards(l + 1, late)) if l + 1 < depth else None
        xs, sv, got = _layer_fwd(xs, mod[l], ws[l], pars[l], f"l{l}", carry)
        saved.append(sv)
        if got is not None:
            ws.append(layer_weights({**whole(early, got[0]), **whole(late, got[1])}))
    dx, lpart = _loss_grad(xs, tgt, name="loss")

    grads, halves, pending = [None] * depth, {}, {}
    for l in reversed(range(depth)):
        dx, grads[l], swapped = _layer_bwd(dx, saved[l], mod[l], ws[l], pars[l], f"l{l}", pending)
        halves.update(swapped)
        pending = grads[l]["late"]
    keys = list(pending)
    got = _exchange([pending[k] for k in keys], "chips", True, name="scatter_grads")
    parts = [_sum_slots(g, name=f"sum_{k[1]}", out_dtype=BF16) for k, g in zip(keys, got)]
    halves.update(zip(keys, _exchange(parts, "sib", False, name="swap_grads")))

    stack = lambda k: jnp.stack([grads[l][k] for l in range(depth)])
    small = [stack("dmod"), stack("gq"), stack("gk"), stack("hg"), stack("la"), stack("lc"), lpart[0, :1]]
    packed = _pack(small)
    allp = _exchange([packed], "all", False, name="gather_small")[0]
    tot = _unpack(_sum_slots(allp, name="sum_small").reshape(-1), small)
    g_b_mod = tot[0].reshape(depth, N_MOD * d)
    loss = tot[6][0]
    g_small = {"b_mod": g_b_mod, "q_norm_g": tot[1], "k_norm_g": tot[2], "hgrn_norm_g": tot[3],
               "hgrn_lb_logits": _lb_bwd(hgrn_lb_logits, tot[4], tot[5], name="lb_bwd")}

    dm_all = allp.reshape(8, -1)[:, :depth * N_MOD * d].reshape(8, depth, N_MOD * d)
    dm_loc = jnp.transpose(lax.dynamic_slice_in_dim(dm_all, chip * ncol, ncol, axis=2), (1, 0, 2))
    g_w_mod = _mod_bwd(c8.T, dm_loc, name="mod_bwd")

    outs = {}
    for n in names:
        w2 = wts[n].reshape(-1, wts[n].shape[-1])
        m2, v2 = mom[n].reshape(w2.shape), var[n].reshape(w2.shape)
        if n in big:
            res = _adamw_layers(w2, [halves[(f"l{l}", n)] for l in range(depth)], m2, v2, name=f"adamw_{n}")
        else:
            g = g_w_mod if n == "w_mod" else g_small[n]
            res = _adamw(w2, [g.reshape(w2.shape)], m2, v2, name=f"adamw_{n}")
        outs[n] = [r.reshape(wts[n].shape) for r in res]
    return (loss, dx[None], *[outs[n][0] for n in names], *[outs[n][1] for n in names], *[outs[n][2] for n in names],
            *[outs[n][3] for n in names])
```

```python
import jax
import jax.numpy as jnp
from jax import lax
from jax.experimental import pallas as pl
from jax.experimental.pallas import tpu as pltpu

F32 = jnp.float32
BF16 = jnp.bfloat16
MESH_ID = pl.DeviceIdType.MESH

HEAD_DIM = 64
SB_W = 256
DIL_W = 256
HG_W = 512
HG_D = 128
IN_W = 3 * SB_W + 3 * DIL_W + 4 * HG_W
MIX_W = SB_W + DIL_W + HG_W
DIL_PATTERNS = ((128, 1), (512, 4), (2048, 16))
ROPE_THETA = 10000.0
EPS = 1e-6
LB_FLOOR = 1e-30
NEG_BIG = -1e30
N_MOD = 9
ADAM_LR = 0.001
ADAM_B1 = 0.9
ADAM_B2 = 0.999
ADAM_EPS = 1e-08
ADAM_WD = 0.01
ADAM_STEP = 10

LANES = 128
QBLK = 128
DIL_TILE = 1024
HG_BLK = 16
HG_TILE = 256
SB_EXIT = 88.0
VMEM_LIMIT = 48 * 1024 * 1024
XCHG_CHUNKS = 8
XCHG_MIN_BYTES = 1 << 19

NN = (((1,), (0,)), ((), ()))
NT = (((1,), (1,)), ((), ()))
TN = (((0,), (0,)), ((), ()))


def _pcall(body, **kw):
    return pl.pallas_call(body, **kw)


def _cparams(*sem):
    return pltpu.CompilerParams(dimension_semantics=sem if sem else None, vmem_limit_bytes=VMEM_LIMIT)


def _dot(a, b, dims=NN):
    return lax.dot_general(a, b, dims, preferred_element_type=F32)


def _split(x, n):
    parts = []
    r = x
    for i in range(n):
        p = r.astype(BF16)
        parts.append(p)
        if i + 1 < n:
            r = r - p.astype(F32)
    return parts


def _xdot(x, m, n=2):
    return sum(_dot(p, m) for p in _split(x, n))


def _iota(shape, dim):
    return lax.broadcasted_iota(jnp.int32, shape, dim)


def _sigmoid(x):
    return 1.0 / (1.0 + jnp.exp(-x))


def _tile(dim, pref, mult=LANES):
    t = (min(pref, dim) // mult) * mult
    while t >= mult:
        if dim % t == 0:
            return t
        t -= mult
    return dim


def _rows(dim, pref):
    return _tile(dim, pref, 8)


def _mm(a, b, *, name, tb=False, tm=512, tn=1024, tk=1024, out_dtype=F32, res=None, scale=None):
    m, kd = a.shape
    n = b.shape[0] if tb else b.shape[1]
    tm, tn, tk = _rows(m, tm), _tile(n, tn), _tile(kd, tk)
    nk = kd // tk
    epi = res is not None

    def body(*refs):
        if epi:
            a_ref, b_ref, r_ref, s_ref, o_ref, x_ref, acc = refs
        else:
            a_ref, b_ref, o_ref, acc = refs
        k = pl.program_id(2)

        @pl.when(k == 0)
        def _():
            acc[...] = jnp.zeros_like(acc)

        acc[...] += _dot(a_ref[...], b_ref[...], NT if tb else NN)

        @pl.when(k == nk - 1)
        def _():
            o_ref[...] = acc[...].astype(o_ref.dtype)
            if epi:
                x_ref[...] = r_ref[...] + s_ref[...] * acc[...]

    in_specs = [
        pl.BlockSpec((tm, tk), lambda i, j, k: (i, k)),
        pl.BlockSpec((tn, tk), lambda i, j, k: (j, k)) if tb else pl.BlockSpec((tk, tn), lambda i, j, k: (k, j)),
    ]
    out_shape = [jax.ShapeDtypeStruct((m, n), out_dtype)]
    out_specs = [pl.BlockSpec((tm, tn), lambda i, j, k: (i, j))]
    args = [a, b]
    if epi:
        in_specs += [pl.BlockSpec((tm, tn), lambda i, j, k: (i, j)), pl.BlockSpec((1, tn), lambda i, j, k: (0, j))]
        out_shape.append(jax.ShapeDtypeStruct((m, n), F32))
        out_specs.append(pl.BlockSpec((tm, tn), lambda i, j, k: (i, j)))
        args += [res, scale]
    out = _pcall(
        body, name=name, grid=(m // tm, n // tn, nk), in_specs=in_specs, out_specs=out_specs, out_shape=out_shape,
        scratch_shapes=[pltpu.VMEM((tm, tn), F32)], compiler_params=_cparams("parallel", "parallel", "arbitrary"),
    )(*args)
    return out if epi else out[0]


def _mm_tn(a, b, *, name, tm=1024, tn=1408, tk=512, out_dtype=BF16, side=None):
    s, m = a.shape
    n = b.shape[1]
    tm, tn, tk = _tile(m, tm), _tile(n, tn), _rows(s, tk)
    nk = s // tk
    ni, nj = m // tm, n // tn

    def body(a_ref, b_ref, o_ref, acc):
        k = pl.program_id(2)

        @pl.when(k == 0)
        def _():
            acc[...] = jnp.zeros_like(acc)

        acc[...] += _dot(a_ref[...], b_ref[...], TN)

        @pl.when(k == nk - 1)
        def _():
            o_ref[...] = acc[...].astype(o_ref.dtype)

    at = lambda i, j, k: (pl.program_id(0) == i) & (pl.program_id(1) == j) & (pl.program_id(2) == k)
    (out,), got = _call_with_exchange(
        body, side, lambda: at(0, 0, 0), lambda: at(ni - 1, nj - 1, nk - 1), (a, b), name=name, grid=(ni, nj, nk),
        in_specs=[pl.BlockSpec((tk, tm), lambda i, j, k: (k, i)), pl.BlockSpec((tk, tn), lambda i, j, k: (k, j))],
        out_specs=[pl.BlockSpec((tm, tn), lambda i, j, k: (i, j))], out_shape=[jax.ShapeDtypeStruct((m, n), out_dtype)],
        scratch_shapes=[pltpu.VMEM((tm, tn), F32)],
        compiler_params=_cparams(*(("parallel", "parallel", "arbitrary") if side is None else ("arbitrary",) * 3)))
    return out if side is None else (out, got)


TE = 512


def _row_spec(t, w, col=0):
    return pl.BlockSpec((t, w), lambda i, col=col: (i, col))


def _vec_spec(w, col=0):
    return pl.BlockSpec((1, w), lambda i, col=col: (0, col))


def _gate_bwd(dxo, y, sg, *, name):
    s, d = y.shape
    t = _rows(s, TE)

    def body(dxo_ref, y_ref, sg_ref, dy_ref, ds_ref):
        @pl.when(pl.program_id(0) == 0)
        def _():
            ds_ref[...] = jnp.zeros_like(ds_ref)

        dv = dxo_ref[...]
        dy_ref[...] = (sg_ref[...] * dv).astype(BF16)
        ds_ref[...] += jnp.sum(dv * y_ref[...], axis=0, keepdims=True)

    return _pcall(
        body, name=name, grid=(s // t,), in_specs=[_row_spec(t, d), _row_spec(t, d), _vec_spec(d)],
        out_specs=[_row_spec(t, d), _vec_spec(d)],
        out_shape=[jax.ShapeDtypeStruct((s, d), BF16), jax.ShapeDtypeStruct((1, d), F32)],
        compiler_params=_cparams("arbitrary"))(dxo, y, sg)


def _loss_grad(y, tgt, *, name):
    s, d = y.shape
    t = _rows(s, TE)
    nt = s // t

    def body(y_ref, t_ref, dy_ref, l_ref, acc):
        i = pl.program_id(0)

        @pl.when(i == 0)
        def _():
            acc[...] = jnp.zeros_like(acc)

        e = y_ref[...] - t_ref[...]
        dy_ref[...] = e * (1.0 / d)
        acc[...] += jnp.sum(e * e, axis=0, keepdims=True)

        @pl.when(i == nt - 1)
        def _():
            l_ref[...] = jnp.broadcast_to(jnp.sum(acc[...], axis=1, keepdims=True) * (0.5 / d), l_ref.shape)

    return _pcall(
        body, name=name, grid=(nt,), in_specs=[_row_spec(t, d), _row_spec(t, d)],
        out_specs=[_row_spec(t, d), pl.BlockSpec((1, LANES), lambda i: (0, 0))],
        out_shape=[jax.ShapeDtypeStruct((s, d), F32), jax.ShapeDtypeStruct((1, LANES), F32)],
        scratch_shapes=[pltpu.VMEM((1, d), F32)], compiler_params=_cparams("arbitrary"))(y, tgt)


SB_TQ = 256
SB_NK = SB_TQ // QBLK


def _sb_consts():
    r = _iota((QBLK, LANES), 0)
    c = _iota((QBLK, LANES), 1)
    ones = jnp.ones((QBLK, LANES), BF16)
    after = jnp.concatenate([jnp.where(r > c, 1.0, 0.0).astype(BF16), ones], axis=1)
    from_ = jnp.concatenate([jnp.where(r >= c, 1.0, 0.0).astype(BF16), ones], axis=1)
    return _iota((SB_TQ, LANES), 0), _iota((SB_TQ, LANES), 1), after, from_


def _sb_scores(qm, kb, strict):
    z = _dot(qm, kb, NT) * (HEAD_DIM ** -0.5)
    sp = jnp.log(1.0 + jnp.exp(-jnp.abs(z)))
    lnb = -(jnp.maximum(z, 0.0) + sp)
    lb = jnp.minimum(z, 0.0) - sp
    if strict is not None:
        lnb = jnp.where(strict, lnb, 0.0)
    return lnb, lb


def _sb_fwd(qkv, *, name, side=None):
    s = qkv.shape[0]
    nq = s // SB_TQ

    def body(q_ref, k_ref, v_ref, o_ref, *scr):
        acc, osc = scr[:4], scr[4:]
        qi = pl.program_id(0)
        row, lane, after, _ = _sb_consts()
        h0 = lane < HEAD_DIM
        q = q_ref[...]
        qms = []
        for p in range(2):
            qp = q[:, p * LANES:(p + 1) * LANES]
            qms += [jnp.where(h0, qp, jnp.zeros_like(qp)), jnp.where(h0, jnp.zeros_like(qp), qp)]

        def block(kj, mask):
            off = pl.multiple_of(kj * QBLK, QBLK)
            kbs = [k_ref[pl.ds(off, QBLK), p * LANES:(p + 1) * LANES] for p in range(2)]
            vbs = [v_ref[pl.ds(off, QBLK), p * LANES:(p + 1) * LANES] for p in range(2)]
            sc = [_sb_scores(qms[c], kbs[c // 2], mask) for c in range(4)]
            trs = [_xdot(sc[c][0], after) for c in range(4)]
            top = None
            for c in range(4):
                w = jnp.exp(sc[c][1] + trs[c][:, :QBLK] + acc[c][...])
                if mask is not None:
                    w = jnp.where(mask, w, 0.0)
                osc[c][...] += _xdot(w, vbs[c // 2])
                new = acc[c][...] + trs[c][:, QBLK:]
                acc[c][...] = new
                top = new if top is None else jnp.maximum(top, new)
            return jnp.max(top)

        for ref in scr:
            ref[...] = jnp.zeros_like(ref)
        top = None
        for j in reversed(range(SB_NK)):
            top = block(qi * SB_NK + j, (lane + j * QBLK) < row)
        lax.while_loop(lambda c: (c[0] >= 0) & (c[1] > -SB_EXIT), lambda c: (c[0] - 1, block(c[0], None)),
                       (qi * SB_NK - 1, top))
        for p in range(2):
            o_ref[:, p * LANES:(p + 1) * LANES] = jnp.where(h0, osc[2 * p][...], osc[2 * p + 1][...])

    (o,), got = _call_with_exchange(
        body, side, lambda: pl.program_id(0) == 0, lambda: pl.program_id(0) == nq - 1, (qkv, qkv, qkv), name=name, grid=(nq,),
        in_specs=[pl.BlockSpec((SB_TQ, SB_W), lambda i: (i, 0)),
                  pl.BlockSpec((s, SB_W), lambda i: (0, 1)),
                  pl.BlockSpec((s, SB_W), lambda i: (0, 2))],
        out_specs=[pl.BlockSpec((SB_TQ, SB_W), lambda i: (i, 0))],
        out_shape=[jax.ShapeDtypeStruct((s, SB_W), F32)],
        scratch_shapes=[pltpu.VMEM((SB_TQ, LANES), F32)] * 8,
        compiler_params=_cparams("arbitrary"))
    return o, got


def _sb_bwd(qkv, o, dmix, *, name, side=None):
    s = qkv.shape[0]
    nq = s // SB_TQ
    scale = HEAD_DIM ** -0.5

    def body(q_ref, k_ref, v_ref, o_ref, do_ref, dq_ref, dk_ref, dv_ref, a0, a1, r0, r1, dqs, dks, dvs):
        acc, racc = (a0, a1), (r0, r1)
        i = pl.program_id(1)
        qi = nq - 1 - i
        row, lane, after, from_ = _sb_consts()
        klane = _iota((QBLK, LANES), 1)
        khms = (klane < HEAD_DIM, klane >= HEAD_DIM)

        @pl.when(i == 0)
        def _():
            dks[...] = jnp.zeros_like(dks)
            dvs[...] = jnp.zeros_like(dvs)

        q = q_ref[...]
        do = do_ref[...]
        dob = do.astype(BF16)
        dd = do * o_ref[...]
        dol = (do - dob.astype(F32)).astype(BF16)
        zero = jnp.zeros_like(q)
        hms = (lane < HEAD_DIM, lane >= HEAD_DIM)
        qms = [jnp.where(hm, q, zero) for hm in hms]
        doms = [jnp.where(hm, dob, zero) for hm in hms]
        dols = [jnp.where(hm, dol, zero) for hm in hms]
        dsums = [jnp.sum(jnp.where(hm, dd, 0.0), axis=1, keepdims=True) for hm in hms]

        def block(kj, mask):
            off = pl.multiple_of(kj * QBLK, QBLK)
            kb = k_ref[pl.ds(off, QBLK), :]
            vb = v_ref[pl.ds(off, QBLK), :]
            top, dq, dk, dv = None, None, None, None
            sc = [_sb_scores(qms[h], kb, mask) for h in range(2)]
            trs = [_xdot(sc[h][0], after) for h in range(2)]
            dws = [_dot(doms[h], vb, NT) + _dot(dols[h], vb, NT) for h in range(2)]
            for h in range(2):
                lb, tr = sc[h][1], trs[h]
                w = jnp.exp(lb + tr[:, :QBLK] + acc[h][...])
                if mask is not None:
                    w = jnp.where(mask, w, 0.0)
                g = w * dws[h]
                tg = _xdot(g, from_)
                before = dsums[h] - (tg[:, :QBLK] + racc[h][...])
                dz = g - jnp.exp(lb) * (g + before)
                if mask is not None:
                    dz = jnp.where(mask, dz, 0.0)
                dzb = (dz * scale).astype(BF16)
                dqh = _dot(dzb, jnp.where(khms[h], kb, jnp.zeros_like(kb)))
                dkh = _dot(dzb, qms[h], TN)
                dvh = _dot(w.astype(BF16), doms[h], TN)
                dq, dk, dv = (dqh, dkh, dvh) if h == 0 else (dq + dqh, dk + dkh, dv + dvh)
                new = acc[h][...] + tr[:, QBLK:]
                acc[h][...] = new
                racc[h][...] += tg[:, QBLK:]
                top = new if top is None else jnp.maximum(top, new)
            dqs[...] += dq
            dks[pl.ds(off, QBLK), :] += dk
            dvs[pl.ds(off, QBLK), :] += dv
            return jnp.max(top)

        for ref in (dqs, a0, a1, r0, r1):
            ref[...] = jnp.zeros_like(ref)
        top = None
        for j in reversed(range(SB_NK)):
            top = block(qi * SB_NK + j, (lane + j * QBLK) < row)
        lax.while_loop(lambda c: (c[0] >= 0) & (c[1] > -SB_EXIT), lambda c: (c[0] - 1, block(c[0], None)),
                       (qi * SB_NK - 1, top))
        dq_ref[...] = dqs[...]
        fin = pl.multiple_of(qi * SB_TQ, SB_TQ)
        dk_ref[...] = dks[pl.ds(fin, SB_TQ), :]
        dv_ref[...] = dvs[pl.ds(fin, SB_TQ), :]

    blk = lambda c0: pl.BlockSpec((SB_TQ, LANES), lambda p, i, c0=c0: (nq - 1 - i, c0 + p))
    return _call_with_exchange(
        body, side, lambda: (pl.program_id(0) == 0) & (pl.program_id(1) == 0),
        lambda: (pl.program_id(0) == 1) & (pl.program_id(1) == nq - 1), (qkv, qkv, qkv, o, dmix), name=name, grid=(2, nq),
        in_specs=[blk(0), pl.BlockSpec((s, LANES), lambda p, i: (0, 2 + p)), pl.BlockSpec((s, LANES), lambda p, i: (0, 4 + p)),
                  blk(0), blk(0)],
        out_specs=[blk(0), blk(0), blk(0)],
        out_shape=[jax.ShapeDtypeStruct((s, SB_W), F32)] * 3,
        scratch_shapes=[pltpu.VMEM((SB_TQ, LANES), F32)] * 5 + [pltpu.VMEM((s, LANES), F32), pltpu.VMEM((s, LANES), F32)],
        compiler_params=_cparams("arbitrary", "arbitrary"))


def _seg_consts():
    r = _iota((LANES, LANES), 0)
    c = _iota((LANES, LANES), 1)
    return jnp.where((r >> 6) == (c >> 6), 1.0, 0.0).astype(BF16)


def _rot_half(x, lane):
    half = HEAD_DIM // 2
    return jnp.where((lane & (HEAD_DIM - 1)) < half, pltpu.roll(x, LANES - half, 1), pltpu.roll(x, half, 1))


def _rope_tables(s):
    half = HEAD_DIM // 2
    inv_freq = ROPE_THETA ** (-jnp.arange(half, dtype=F32) * 2.0 / HEAD_DIM)
    ang = jnp.arange(s, dtype=F32)[:, None] * inv_freq[None, :]
    cos, sin = jnp.cos(ang), jnp.sin(ang)
    return jnp.tile(jnp.concatenate([cos, cos], axis=1), (1, 2)), jnp.tile(jnp.concatenate([-sin, sin], axis=1), (1, 2))


def _dil_prep(proj, gq, gk, cos, sin, *, name):
    s = proj.shape[0]
    t = _rows(s, TE)
    c0 = 3 * SB_W // LANES

    def body(q_ref, k_ref, gq_ref, gk_ref, cos_ref, sin_ref, qo_ref, ko_ref):
        seg = _seg_consts()
        lane = _iota((t, LANES), 1)
        cs, sn = cos_ref[...], sin_ref[...]
        for x_ref, g_ref, o_ref, mul in ((q_ref, gq_ref, qo_ref, HEAD_DIM ** -0.5), (k_ref, gk_ref, ko_ref, 1.0)):
            for j in range(2):
                xv = x_ref[:, j * LANES:(j + 1) * LANES]
                ms = _xdot(xv * xv, seg, 3) * (1.0 / HEAD_DIM)
                xn = xv * lax.rsqrt(ms + EPS) * g_ref[...]
                o_ref[:, j * LANES:(j + 1) * LANES] = (xn * cs + _rot_half(xn, lane) * sn) * mul

    return _pcall(
        body, name=name, grid=(s // t,),
        in_specs=[pl.BlockSpec((t, DIL_W), lambda i: (i, c0 // 2)), pl.BlockSpec((t, DIL_W), lambda i: (i, c0 // 2 + 1)),
                  _vec_spec(LANES), _vec_spec(LANES), _row_spec(t, LANES), _row_spec(t, LANES)],
        out_specs=[_row_spec(t, DIL_W), _row_spec(t, DIL_W)],
        out_shape=[jax.ShapeDtypeStruct((s, DIL_W), F32)] * 2, compiler_params=_cparams("parallel"))(proj, proj, gq, gk, cos, sin)


def _dil_prep_bwd(proj, gq, gk, cos, sin, dqs, dks, dvs, *, name):
    s = proj.shape[0]
    t = _rows(s, TE)
    c0 = 3 * SB_W // LANES

    def body(q_ref, k_ref, gq_ref, gk_ref, cos_ref, sin_ref, a0, a1, a2, b0, b1, b2, c0_ref, c1_ref, c2_ref,
             dq_ref, dk_ref, dv_ref, dgq_ref, dgk_ref):
        @pl.when(pl.program_id(0) == 0)
        def _():
            dgq_ref[...] = jnp.zeros_like(dgq_ref)
            dgk_ref[...] = jnp.zeros_like(dgk_ref)

        dv_ref[...] = c0_ref[...] + c1_ref[...] + c2_ref[...]
        seg = _seg_consts()
        lane = _iota((t, LANES), 1)
        cs, sn = cos_ref[...], sin_ref[...]
        for x_ref, g_ref, parts, o_ref, dg_ref, mul in ((q_ref, gq_ref, (a0, a1, a2), dq_ref, dgq_ref, HEAD_DIM ** -0.5),
                                                          (k_ref, gk_ref, (b0, b1, b2), dk_ref, dgk_ref, 1.0)):
            for j in range(2):
                sl = slice(j * LANES, (j + 1) * LANES)
                dout = (parts[0][:, sl] + parts[1][:, sl] + parts[2][:, sl]) * mul
                dxn = dout * cs + _rot_half(dout * sn, lane)
                xv = x_ref[:, sl]
                r = lax.rsqrt(_xdot(xv * xv, seg, 3) * (1.0 / HEAD_DIM) + EPS)
                xh = xv * r
                dg_ref[...] += jnp.sum(dxn * xh, axis=0, keepdims=True)
                dxh = dxn * g_ref[...]
                o_ref[:, sl] = r * (dxh - xh * (_xdot(dxh * xh, seg, 3) * (1.0 / HEAD_DIM)))

    rs = _row_spec(t, DIL_W)
    return _pcall(
        body, name=name, grid=(s // t,),
        in_specs=[pl.BlockSpec((t, DIL_W), lambda i: (i, c0 // 2)), pl.BlockSpec((t, DIL_W), lambda i: (i, c0 // 2 + 1)),
                  _vec_spec(LANES), _vec_spec(LANES), _row_spec(t, LANES), _row_spec(t, LANES)] + [rs] * 9,
        out_specs=[rs, rs, rs, _vec_spec(LANES), _vec_spec(LANES)],
        out_shape=[jax.ShapeDtypeStruct((s, DIL_W), F32)] * 3 + [jax.ShapeDtypeStruct((1, LANES), F32)] * 2,
        compiler_params=_cparams("arbitrary"))(proj, proj, gq, gk, cos, sin, *dqs, *dks, *dvs)


def _dil_masks(n):
    row = _iota((QBLK, 2 * LANES), 0)
    col = _iota((QBLK, 2 * LANES), 1)
    return ((col < LANES) & (col >= row) & (n > 0)) | ((col >= LANES) & (col - LANES <= row))


DIL_V0 = (3 * SB_W + 2 * DIL_W) // LANES
DIL_DO0 = SB_W // LANES


def _dil_tiles(s, r):
    span = QBLK * r
    nsub = max(1, DIL_TILE // span)
    while s % (nsub * span):
        nsub -= 1
    return span, nsub


def _dil_rows(j, rho, span, r):
    return pl.ds(j * span + rho, QBLK, stride=r) if r > 1 else pl.ds(j * span, QBLK)


def _dil_fwd(q, k, proj, r, *, name):
    s = q.shape[0]
    span, nsub = _dil_tiles(s, r)
    tr = nsub * span

    def body(q_ref, kc_ref, kp_ref, vc_ref, vp_ref, num_ref, den_ref, mx_ref):
        n = pl.program_id(1)
        lane = _iota((QBLK, LANES), 1)
        h0 = lane < HEAD_DIM
        ones = jnp.ones((2 * QBLK, LANES), BF16)
        for j in range(nsub):
            valid = _dil_masks(n if j == 0 else 1)
            for rho in range(r):
                rows = _dil_rows(j, rho, span, r)
                before = _dil_rows(max(j - 1, 0), rho, span, r)
                k_prev, v_prev = (kp_ref, vp_ref) if j == 0 else (kc_ref, vc_ref)
                qv = q_ref[rows, :].astype(BF16)
                kk = jnp.concatenate([k_prev[before, :], kc_ref[rows, :]], axis=0).astype(BF16)
                vv = jnp.concatenate([jnp.concatenate([v_prev[before, :], vc_ref[rows, :]], axis=0).astype(BF16), ones], axis=1)
                res = []
                for h in range(2):
                    qm = jnp.where(h0 if h == 0 else ~h0, qv, jnp.zeros_like(qv))
                    sc = jnp.where(valid, _dot(qm, kk, NT), NEG_BIG)
                    mx = jnp.max(sc, axis=1, keepdims=True)
                    nd = _dot(jnp.exp(sc - mx).astype(BF16), vv)
                    res.append((nd[:, :LANES], nd[:, LANES:], mx))
                num_ref[rows, :] = jnp.where(h0, res[0][0], res[1][0])
                den_ref[rows, :] = jnp.where(h0, res[0][1], res[1][1])
                mx_ref[rows, :] = jnp.where(h0, res[0][2], res[1][2])

    cur = lambda c0: pl.BlockSpec((tr, LANES), lambda p, n, c0=c0: (n, c0 + p))
    prev = lambda c0: pl.BlockSpec((span, LANES), lambda p, n, c0=c0: (jnp.maximum(n * nsub - 1, 0), c0 + p))
    return _pcall(
        body, name=name, grid=(2, s // tr), in_specs=[cur(0), cur(0), prev(0), cur(DIL_V0), prev(DIL_V0)],
        out_specs=[cur(0), cur(0), cur(0)], out_shape=[jax.ShapeDtypeStruct((s, DIL_W), F32)] * 3,
        compiler_params=_cparams("parallel", "arbitrary"))(q, k, k, proj, proj)


def _dil_bwd(q, k, proj, dmix, mall, zall, delta, r, *, name):
    s = q.shape[0]
    span, nsub = _dil_tiles(s, r)
    tr = nsub * span
    nbig = s // tr

    def body(q_ref, kc_ref, kp_ref, vc_ref, vp_ref, do_ref, m_ref, z_ref, dl_ref, dq_ref, dk_ref, dv_ref, pk, pv):
        n = pl.program_id(1)
        lane = _iota((QBLK, LANES), 1)
        h0 = lane < HEAD_DIM

        @pl.when(n == 0)
        def _():
            pk[...] = jnp.zeros_like(pk)
            pv[...] = jnp.zeros_like(pv)

        @pl.when(n < nbig)
        def _():
            dk_ref[...] = pk[...]
            dv_ref[...] = pv[...]
            for j in range(nsub):
                valid = _dil_masks(n if j == 0 else 1)
                for rho in range(r):
                    rows = _dil_rows(j, rho, span, r)
                    before = _dil_rows(max(j - 1, 0), rho, span, r)
                    k_prev, v_prev = (kp_ref, vp_ref) if j == 0 else (kc_ref, vc_ref)
                    qv = q_ref[rows, :].astype(BF16)
                    dob = do_ref[rows, :].astype(BF16)
                    zero = jnp.zeros_like(qv)
                    kk = jnp.concatenate([k_prev[before, :], kc_ref[rows, :]], axis=0).astype(BF16)
                    vv = jnp.concatenate([v_prev[before, :], vc_ref[rows, :]], axis=0).astype(BF16)
                    mall_v, z_v, dl_v = m_ref[rows, :], z_ref[rows, :], dl_ref[rows, :]
                    dq, dk, dv = None, None, None
                    for h in range(2):
                        hm = h0 if h == 0 else ~h0
                        qm = jnp.where(hm, qv, zero)
                        dom = jnp.where(hm, dob, zero)
                        c = h * HEAD_DIM
                        sc = jnp.where(valid, _dot(qm, kk, NT), NEG_BIG)
                        pr = jnp.exp(sc - mall_v[:, c:c + 1]) * (1.0 / z_v[:, c:c + 1])
                        ds = (pr * (_dot(dom, vv, NT) - dl_v[:, c:c + 1])).astype(BF16)
                        parts = (_dot(ds, jnp.where(jnp.concatenate([hm, hm], axis=0), kk, jnp.zeros_like(kk))),
                                 _dot(ds, qm, TN), _dot(pr.astype(BF16), dom, TN))
                        dq, dk, dv = parts if h == 0 else (dq + parts[0], dk + parts[1], dv + parts[2])
                    dq_ref[rows, :] = dq
                    pk[rows, :] = dk[QBLK:]
                    pv[rows, :] = dv[QBLK:]
                    if j == 0:
                        last_span = _dil_rows(nsub - 1, rho, span, r)
                        dk_ref[last_span, :] += dk[:QBLK]
                        dv_ref[last_span, :] += dv[:QBLK]
                    else:
                        pk[before, :] += dk[:QBLK]
                        pv[before, :] += dv[:QBLK]

        @pl.when(n == nbig)
        def _():
            dk_ref[...] = pk[...]
            dv_ref[...] = pv[...]

    last = nbig - 1
    cur = lambda c0: pl.BlockSpec((tr, LANES), lambda p, n, c0=c0: (jnp.minimum(n, last), c0 + p))
    prev = lambda c0: pl.BlockSpec((span, LANES), lambda p, n, c0=c0: (jnp.maximum(jnp.minimum(n, last) * nsub - 1, 0), c0 + p))
    late = pl.BlockSpec((tr, LANES), lambda p, n: (jnp.maximum(n - 1, 0), p))
    return _pcall(
        body, name=name, grid=(2, nbig + 1),
        in_specs=[cur(0), cur(0), prev(0), cur(DIL_V0), prev(DIL_V0), cur(DIL_DO0), cur(0), cur(0), cur(0)],
        out_specs=[cur(0), late, late], out_shape=[jax.ShapeDtypeStruct((s, DIL_W), F32)] * 3,
        scratch_shapes=[pltpu.VMEM((tr, LANES), F32)] * 2,
        compiler_params=_cparams("parallel", "arbitrary"))(q, k, k, proj, proj, dmix, mall, zall, delta)


HG_SHIFT = HG_BLK.bit_length() - 1
HG_Q0, HG_F0, HG_I0 = (3 * SB_W + 3 * DIL_W) // HG_D, (3 * SB_W + 3 * DIL_W + HG_W) // HG_D, (3 * SB_W + 3 * DIL_W + 2 * HG_W) // HG_D


def _hg_scan(x):
    t = x.shape[0]
    half = HG_BLK // 2
    rb = _iota((t, LANES), 0) & (HG_BLK - 1)
    rh = rb & (half - 1)
    p = x
    for s in (1, 2, 4):
        p = p + jnp.where(rh >= s, pltpu.roll(p, s, 0), 0.0)
    h = jnp.where(rh == half - 1, p, 0.0)
    for s in (1, 2, 4):
        h = h + jnp.where(rh + s < half, pltpu.roll(h, t - s, 0), 0.0)
    first = rb < half
    pref = jnp.where(first, p, p + pltpu.roll(h, half, 0))
    total = h + jnp.where(first, pltpu.roll(h, t - half, 0), pltpu.roll(h, half, 0))
    return p, h, pref, total, first


def _hg_same(t):
    i = jnp.arange(t) >> HG_SHIFT
    return (i[:, None] == i[None, :]).astype(F32)


def _hg_own(t):
    i = jnp.arange(t) >> HG_SHIFT
    j = jnp.arange(t // HG_BLK * HG_D) // HG_D
    return (i[:, None] == j[None, :]).astype(BF16)


def _hg_diag(x, nb):
    return jnp.concatenate([x[b * HG_BLK:(b + 1) * HG_BLK, b * HG_D:(b + 1) * HG_D] for b in range(nb)], axis=0)


def _hg_inputs(qh, z, v, la, lc, t):
    lsg = jnp.minimum(z, 0.0) - jnp.log(1.0 + jnp.exp(-jnp.abs(z)))
    b = lc + lsg
    lf = jnp.maximum(la, b) + jnp.log(1.0 + jnp.exp(-jnp.abs(la - b)))
    f = jnp.exp(lf)
    sq = _sigmoid(qh)
    p, h, g, gl, first = _hg_scan(lf)
    k = 1.0 - f
    qs = qh * sq
    eq = jnp.where(first, 0.0, jnp.exp(jnp.minimum(p, 0.0)))
    ek = jnp.where(first, jnp.exp(jnp.minimum(h - p, 0.0)), 0.0)
    return dict(lf=lf, b=b, f=f, k=k, sq=sq, qs=qs, g=g, eg=jnp.exp(g), egl=jnp.exp(gl - g), dec=jnp.exp(gl),
                eq=eq, ek=ek, qx=(qs * eq).astype(BF16), kx=(k * ek).astype(BF16))


def _hgrn_fwd(proj, la, lc, *, name, side=None):
    s = proj.shape[0]
    t = _rows(s, HG_TILE)
    nt, nb = s // t, t // HG_BLK

    def body(q_ref, f_ref, i_ref, la_ref, lc_ref, same_ref, own_ref, o_ref, st_ref, state):
        @pl.when(pl.program_id(1) == 0)
        def _():
            state[...] = jnp.zeros_like(state)

        v = i_ref[...]
        a = _hg_inputs(q_ref[...], f_ref[...], v, la_ref[...], lc_ref[...], t)
        qs, k = a["qs"], a["k"]
        vb = v.astype(BF16)
        rb = _iota((t, LANES), 0) & (HG_BLK // 2 - 1)
        o = jnp.sum(qs * k, axis=1, keepdims=True) * v
        e = None
        for d in range(1, HG_BLK // 2):
            m = rb >= d
            fr = a["f"] if d == 1 else pltpu.roll(a["f"], d - 1, 0)
            e = fr if e is None else e * fr
            cd = jnp.sum(qs * pltpu.roll(k, d, 0) * e, axis=1, keepdims=True)
            o = o + jnp.where(m, cd, 0.0) * pltpu.roll(v, d, 0)
        cross = _dot(a["qx"], a["kx"], NT) * same_ref[...]
        o = o + _dot(cross.astype(BF16), vb)
        qt = (qs * a["eg"]).astype(BF16)
        kt = (k * a["egl"]).astype(BF16)
        upd = _dot(vb, jnp.tile(kt, (1, nb)) * own_ref[...], TN)
        st = state[...]
        for blk in range(nb):
            st_ref[blk * HG_D:(blk + 1) * HG_D, :] = st.astype(BF16)
            st = a["dec"][blk * HG_BLK:blk * HG_BLK + 1] * st + upd[:, blk * HG_D:(blk + 1) * HG_D]
        state[...] = st
        o_ref[...] = o + _hg_diag(_dot(qt, st_ref[...], NT), nb)

    col = lambda c0: pl.BlockSpec((t, HG_D), lambda hd, i, c0=c0: (i, c0 + hd))
    vec = pl.BlockSpec((1, HG_D), lambda hd, i: (0, hd))
    return _call_with_exchange(
        body, side, lambda: (pl.program_id(0) == 0) & (pl.program_id(1) == 0),
        lambda: (pl.program_id(0) == 3) & (pl.program_id(1) == nt - 1),
        (proj, proj, proj, la, lc, _hg_same(t), _hg_own(t)), name=name, grid=(4, nt),
        in_specs=[col(HG_Q0), col(HG_F0), col(HG_I0), vec, vec, pl.BlockSpec((t, t), lambda hd, i: (0, 0)),
                  pl.BlockSpec((t, nb * HG_D), lambda hd, i: (0, 0))],
        out_specs=[col(0), pl.BlockSpec((None, nb * HG_D, HG_D), lambda hd, i: (hd, i, 0))],
        out_shape=[jax.ShapeDtypeStruct((s, HG_W), F32), jax.ShapeDtypeStruct((4, s // HG_BLK * HG_D, HG_D), BF16)],
        scratch_shapes=[pltpu.VMEM((HG_D, HG_D), F32)],
        compiler_params=_cparams("arbitrary", "arbitrary"))


def _hgrn_bwd(proj, la, lc, st, doh, *, name, side=None):
    s = proj.shape[0]
    t = _rows(s, HG_TILE)
    nt, nb = s // t, t // HG_BLK

    def body(q_ref, f_ref, i_ref, la_ref, lc_ref, st_ref, do_ref, same_ref, own_ref, dq_ref, df_ref, di_ref, dla_ref, dlc_ref,
             dstate, dsb):
        @pl.when(pl.program_id(1) == 0)
        def _():
            dstate[...] = jnp.zeros_like(dstate)
            dla_ref[...] = jnp.zeros_like(dla_ref)
            dlc_ref[...] = jnp.zeros_like(dlc_ref)

        qh, z, v, do = q_ref[...], f_ref[...], i_ref[...], do_ref[...]
        la = la_ref[...]
        a = _hg_inputs(qh, z, v, la, lc_ref[...], t)
        qs, k, g = a["qs"], a["k"], a["g"]
        vb = v.astype(BF16)
        dob = do.astype(BF16)
        rb = _iota((t, LANES), 0) & (HG_BLK // 2 - 1)
        dc0 = jnp.sum(do * v, axis=1, keepdims=True)
        dq = dc0 * k
        dk = dc0 * qs
        dv = jnp.sum(qs * k, axis=1, keepdims=True) * do
        e = None
        for d in range(1, HG_BLK // 2):
            m = rb >= d
            fr = a["f"] if d == 1 else pltpu.roll(a["f"], d - 1, 0)
            e = fr if e is None else e * fr
            ks = pltpu.roll(k, d, 0)
            qe = qs * e
            cd = jnp.where(m, jnp.sum(qe * ks, axis=1, keepdims=True), 0.0)
            dcd = jnp.where(m, jnp.sum(do * pltpu.roll(v, d, 0), axis=1, keepdims=True), 0.0)
            dq = dq + dcd * ks * e
            dk = dk + pltpu.roll(dcd * qe, t - d, 0)
            dv = dv + pltpu.roll(cd * do, t - d, 0)
        same = same_ref[...]
        cross = (_dot(a["qx"], a["kx"], NT) * same).astype(BF16)
        dcross = (_dot(dob, vb, NT) * same).astype(BF16)
        dq = dq + _dot(dcross, a["kx"]) * a["eq"]
        dk = dk + _dot(dcross, a["qx"], TN) * a["ek"]
        dv = dv + _dot(cross, dob, TN)
        qt = (qs * a["eg"]).astype(BF16)
        kt = (k * a["egl"]).astype(BF16)
        own = own_ref[...]
        upd = _dot(dob, jnp.tile(qt, (1, nb)) * own, TN)
        ds = dstate[...]
        dgs = [None] * nb
        for blk in reversed(range(nb)):
            rows = slice(blk * HG_D, (blk + 1) * HG_D)
            dec = a["dec"][blk * HG_BLK:blk * HG_BLK + 1]
            dsb[rows, :] = ds.astype(BF16)
            dgs[blk] = jnp.broadcast_to(jnp.sum(ds * st_ref[rows, :].astype(F32), axis=0, keepdims=True) * dec, (HG_BLK, HG_D))
            ds = dec * ds + upd[:, rows]
        dstate[...] = ds
        dki = _dot(jnp.tile(vb, (1, nb)) * own, dsb[...]) * a["egl"]
        dq = dq + _dot(jnp.tile(dob, (1, nb)) * own, st_ref[...]) * a["eg"]
        dk = dk + dki
        dv = dv + _hg_diag(_dot(kt, dsb[...], NT), nb)
        x = qs * dq - k * dk
        _, _, xpre, xtot, _ = _hg_scan(x)
        _, _, _, ktot, _ = _hg_scan(k * dki)
        dlf = (xtot - xpre + x) + ktot + jnp.concatenate(dgs, axis=0) - a["f"] * dk
        wb = jnp.exp(a["b"] - a["lf"])
        wa = jnp.exp(la - a["lf"])
        sq = a["sq"]
        dq_ref[...] = dq * (sq * (1.0 + qh * (1.0 - sq)))
        df_ref[...] = dlf * wb * (1.0 - _sigmoid(z))
        di_ref[...] = dv
        dla_ref[...] += jnp.sum(dlf * wa, axis=0, keepdims=True)
        dlc_ref[...] += jnp.sum(dlf * wb, axis=0, keepdims=True)

    col = lambda c0: pl.BlockSpec((t, HG_D), lambda hd, i, c0=c0: (nt - 1 - i, c0 + hd))
    vec = pl.BlockSpec((1, HG_D), lambda hd, i: (0, hd))
    return _call_with_exchange(
        body, side, lambda: (pl.program_id(0) == 0) & (pl.program_id(1) == 0),
        lambda: (pl.program_id(0) == 3) & (pl.program_id(1) == nt - 1),
        (proj, proj, proj, la, lc, st, doh, _hg_same(t), _hg_own(t)), name=name, grid=(4, nt),
        in_specs=[col(HG_Q0), col(HG_F0), col(HG_I0), vec, vec,
                  pl.BlockSpec((None, nb * HG_D, HG_D), lambda hd, i: (hd, nt - 1 - i, 0)), col(0),
                  pl.BlockSpec((t, t), lambda hd, i: (0, 0)), pl.BlockSpec((t, nb * HG_D), lambda hd, i: (0, 0))],
        out_specs=[col(0), col(0), col(0), vec, vec],
        out_shape=[jax.ShapeDtypeStruct((s, HG_W), F32)] * 3 + [jax.ShapeDtypeStruct((1, HG_W), F32)] * 2,
        scratch_shapes=[pltpu.VMEM((HG_D, HG_D), F32), pltpu.VMEM((nb * HG_D, HG_D), BF16)],
        compiler_params=_cparams("arbitrary", "arbitrary"))


GH0 = (IN_W - HG_W) // HG_W


def _mix_out(o_a, nums, dens, mxs, oh, proj, hg, *, name):
    s = o_a.shape[0]
    t = _rows(s, TE)

    def body(oa_ref, n0, n1, n2, d0, d1, d2, m0, m1, m2, oh_ref, gh_ref, hg_ref, y_ref, od_ref, mall_ref, z_ref):
        y_ref[:, :SB_W] = oa_ref[...].astype(BF16)
        m = jnp.maximum(jnp.maximum(m0[...], m1[...]), m2[...])
        num = jnp.zeros((t, DIL_W), F32)
        z = jnp.zeros((t, DIL_W), F32)
        for n_ref, d_ref, m_ref in ((n0, d0, m0), (n1, d1, m1), (n2, d2, m2)):
            sc = jnp.exp(m_ref[...] - m)
            num = num + n_ref[...] * sc
            z = z + d_ref[...] * sc
        od = num / z
        od_ref[...] = od
        mall_ref[...] = m
        z_ref[...] = z
        y_ref[:, SB_W:SB_W + DIL_W] = od.astype(BF16)
        for h in range(4):
            sl = slice(h * HG_D, (h + 1) * HG_D)
            ov = oh_ref[:, sl]
            g = gh_ref[:, sl]
            r = lax.rsqrt(jnp.mean(ov * ov, axis=1, keepdims=True) + EPS)
            y_ref[:, SB_W + DIL_W + h * HG_D:SB_W + DIL_W + (h + 1) * HG_D] = (ov * r * hg_ref[...] * (g * _sigmoid(g))).astype(BF16)

    rd = _row_spec(t, DIL_W)
    return _pcall(
        body, name=name, grid=(s // t,),
        in_specs=[rd] * 10 + [_row_spec(t, HG_W), _row_spec(t, HG_W, GH0), _vec_spec(HG_D)],
        out_specs=[_row_spec(t, MIX_W), rd, rd, rd],
        out_shape=[jax.ShapeDtypeStruct((s, MIX_W), BF16)] + [jax.ShapeDtypeStruct((s, DIL_W), F32)] * 3,
        compiler_params=_cparams("parallel"))(o_a, *nums, *dens, *mxs, oh, proj, hg)


def _mix_out_bwd(dmix, oh, proj, hg, od, *, name):
    s = oh.shape[0]
    t = _rows(s, TE)

    def body(dm_ref, oh_ref, gh_ref, hg_ref, od_ref, doh_ref, dgh_ref, dl_ref, dhg_ref):
        @pl.when(pl.program_id(0) == 0)
        def _():
            dhg_ref[...] = jnp.zeros_like(dhg_ref)

        seg = _seg_consts()
        for j in range(2):
            sl = slice(j * LANES, (j + 1) * LANES)
            dl_ref[:, sl] = _xdot(dm_ref[:, SB_W + j * LANES:SB_W + (j + 1) * LANES] * od_ref[:, sl], seg, 3)
        hgv = hg_ref[...]
        for h in range(4):
            sl = slice(h * HG_D, (h + 1) * HG_D)
            dy = dm_ref[:, SB_W + DIL_W + h * HG_D:SB_W + DIL_W + (h + 1) * HG_D]
            ov = oh_ref[:, sl]
            g = gh_ref[:, sl]
            sg = _sigmoid(g)
            silu = g * sg
            r = lax.rsqrt(jnp.mean(ov * ov, axis=1, keepdims=True) + EPS)
            nrm = ov * r
            dhg_ref[...] += jnp.sum(dy * nrm * silu, axis=0, keepdims=True)
            dgh_ref[:, sl] = dy * nrm * hgv * (sg * (1.0 + g * (1.0 - sg)))
            dn = dy * hgv * silu
            doh_ref[:, sl] = r * (dn - nrm * jnp.mean(dn * nrm, axis=1, keepdims=True))

    rh = _row_spec(t, HG_W)
    return _pcall(
        body, name=name, grid=(s // t,),
        in_specs=[_row_spec(t, MIX_W), rh, _row_spec(t, HG_W, GH0), _vec_spec(HG_D), _row_spec(t, DIL_W)],
        out_specs=[rh, rh, _row_spec(t, DIL_W), _vec_spec(HG_D)],
        out_shape=[jax.ShapeDtypeStruct((s, HG_W), F32)] * 2 + [jax.ShapeDtypeStruct((s, DIL_W), F32), jax.ShapeDtypeStruct((1, HG_D), F32)],
        compiler_params=_cparams("arbitrary"))(dmix, oh, proj, hg, od)


def _lb_terms(l):
    l0, l1 = l[0:1], l[1:2]
    m = jnp.maximum(l0, l1)
    e0, e1 = jnp.exp(l0 - m), jnp.exp(l1 - m)
    s0, s1 = e0 / (e0 + e1), e1 / (e0 + e1)
    args = (s0 - s0, (s0 + s1) - s0)
    lbs = tuple(jnp.minimum(jnp.maximum(a, 0.0), 1.0 - EPS) for a in args)
    return s0, s1, args, lbs


def _lb_prep(logits, *, name):
    def body(l_ref, lb_ref, la_ref, lc_ref):
        _, _, _, lbs = _lb_terms(l_ref[...])
        lb = jnp.concatenate(lbs, axis=0)
        lb_ref[...] = lb
        la_ref[...] = jnp.log(jnp.maximum(lb, LB_FLOOR))
        lc_ref[...] = jnp.log1p(-lb)

    return _pcall(body, name=name, out_shape=[jax.ShapeDtypeStruct(logits.shape, F32)] * 3)(logits)


def _lb_bwd(logits, dla, dlc, *, name):
    def half(hi, eq):
        return jnp.where(hi, 1.0, jnp.where(eq, 0.5, 0.0))

    def body(l_ref, dla_ref, dlc_ref, o_ref):
        s0, s1, args, lbs = _lb_terms(l_ref[...])
        da = []
        for i in range(2):
            a, lb = args[i], lbs[i]
            dlb = dla_ref[i:i + 1] * half(lb > LB_FLOOR, lb == LB_FLOOR) / jnp.maximum(lb, LB_FLOOR) - dlc_ref[i:i + 1] / (1.0 - lb)
            t = jnp.maximum(a, 0.0)
            da.append(dlb * half(a > 0.0, a == 0.0) * half(t < 1.0 - EPS, t == 1.0 - EPS))
        ds0 = (da[0] + da[1]) - (da[0] + da[1])
        ds1 = da[1]
        dot = s0 * ds0 + s1 * ds1
        o_ref[...] = jnp.concatenate([s0 * (ds0 - dot), s1 * (ds1 - dot)], axis=0)

    return _pcall(body, name=name, out_shape=jax.ShapeDtypeStruct(logits.shape, F32))(logits, dla, dlc)


def _mod_fwd(c8, w, b, *, name):
    _, d, n = w.shape
    tn = _tile(n, 768)

    def body(c_ref, w_ref, b_ref, o_ref):
        cv = c_ref[...]
        o_ref[...] = _dot((cv * _sigmoid(cv)).astype(BF16), w_ref[...].astype(BF16)) + b_ref[...]

    return _pcall(
        body, name=name, grid=(2, n // tn),
        in_specs=[pl.BlockSpec((8, d), lambda l, j: (0, 0)), pl.BlockSpec((None, d, tn), lambda l, j: (l, 0, j)),
                  pl.BlockSpec((None, 1, tn), lambda l, j: (l, 0, j))],
        out_specs=pl.BlockSpec((None, 8, tn), lambda l, j: (l, 0, j)),
        out_shape=jax.ShapeDtypeStruct((2, 8, n), F32), compiler_params=_cparams("parallel", "parallel"))(c8, w, b)


def _mod_bwd(ct, dm, *, name):
    d = ct.shape[0]
    n = dm.shape[2]
    tn = _tile(n, 768)

    def body(c_ref, dm_ref, o_ref):
        cv = c_ref[...]
        sc = cv * _sigmoid(cv)
        dv = dm_ref[...]
        acc = sc[:, 0:1] * dv[0:1, :]
        for b in range(1, 8):
            acc = acc + sc[:, b:b + 1] * dv[b:b + 1, :]
        o_ref[...] = acc

    return _pcall(
        body, name=name, grid=(2, n // tn),
        in_specs=[pl.BlockSpec((d, 8), lambda l, j: (0, 0)), pl.BlockSpec((None, 8, tn), lambda l, j: (l, 0, j))],
        out_specs=pl.BlockSpec((None, d, tn), lambda l, j: (l, 0, j)),
        out_shape=jax.ShapeDtypeStruct((2, d, n), F32), compiler_params=_cparams("parallel", "parallel"))(ct, dm)


_PEERS = {
    "chips": ((1, 0, 0), (0, 1, 0), (1, 1, 0)),
    "all": tuple((a, b, c) for a in (0, 1) for b in (0, 1) for c in (0, 1) if a + b + c),
    "sib": ((0, 0, 1),),
}
_SLOTS = {"chips": 4, "all": 8, "sib": 2}


def _slot(kind, x, y, c):
    return {"chips": 2 * x + y, "all": 4 * x + 2 * y + c, "sib": c}[kind]


class _Xchg:
    def __init__(self, arrs, kind, scatter):
        self.arrs, self.kind, self.scatter = list(arrs), kind, scatter
        self.n = len(self.arrs)
        self.peers = _PEERS[kind]
        self.chunks = [self._pieces(a) for a in self.arrs]
        npeer = len(self.peers)
        self.base = [sum(len(c) for c in self.chunks[:a]) * npeer for a in range(self.n)]
        total = sum(len(c) for c in self.chunks) * npeer
        self.specs = [pl.BlockSpec(memory_space=pl.ANY)] * self.n
        self.out_shape = [jax.ShapeDtypeStruct(a.shape if scatter else (_SLOTS[kind],) + a.shape, a.dtype) for a in self.arrs]
        self.scratch = [pltpu.SemaphoreType.DMA((total,)), pltpu.SemaphoreType.DMA((total,)), pltpu.SemaphoreType.DMA((self.n,))]

    def _pieces(self, a):
        shape = a.shape[1:] if self.scatter else a.shape
        if len(shape) == 2:
            for k in (XCHG_CHUNKS, XCHG_CHUNKS // 2, XCHG_CHUNKS // 4):
                if k > 1 and shape[0] % (16 * k) == 0 and shape[0] * shape[1] * a.dtype.itemsize >= k * XCHG_MIN_BYTES:
                    return [(i * (shape[0] // k), shape[0] // k) for i in range(k)]
        return [None]

    def copies(self, ins, outs, send, recv, loc):
        kind, scatter = self.kind, self.scatter
        x, y, c = lax.axis_index("x"), lax.axis_index("y"), lax.axis_index("c")
        me = _slot(kind, x, y, c)
        out = []
        for a in range(self.n):
            out.append(pltpu.make_async_copy(ins[a].at[me] if scatter else ins[a], outs[a].at[me], loc.at[a]))
            for j, (dx, dy, dc) in enumerate(self.peers):
                px, py, pc = (1 - x if dx else x), (1 - y if dy else y), (1 - c if dc else c)
                src = ins[a].at[_slot(kind, px, py, pc)] if scatter else ins[a]
                for i, piece in enumerate(self.chunks[a]):
                    rows = slice(None) if piece is None else pl.ds(piece[0], piece[1])
                    sem = self.base[a] + j * len(self.chunks[a]) + i
                    out.append(pltpu.make_async_remote_copy(
                        src_ref=src if piece is None else src.at[rows], dst_ref=outs[a].at[me] if piece is None else outs[a].at[me, rows],
                        send_sem=send.at[sem], recv_sem=recv.at[sem], device_id=(px, py, pc), device_id_type=MESH_ID))
        return out


def _exchange(arrs, kind, scatter, *, name):
    xc = _Xchg(arrs, kind, scatter)

    def body(*refs):
        copies = xc.copies(refs[:xc.n], refs[xc.n:2 * xc.n], *refs[2 * xc.n:])
        for cp in copies:
            cp.start()
        for cp in copies:
            cp.wait()

    return _pcall(body, name=name, in_specs=xc.specs, out_specs=xc.specs, out_shape=xc.out_shape, scratch_shapes=xc.scratch)(*arrs)


def _with_exchange(body, n_in, n_out, n_scr, xc, first, last):
    def wrapped(*refs):
        ins, side_in = refs[:n_in], refs[n_in:n_in + xc.n]
        o0 = n_in + xc.n
        outs, side_out = refs[o0:o0 + n_out], refs[o0 + n_out:o0 + n_out + xc.n]
        s0 = o0 + n_out + xc.n
        scr, sems = refs[s0:s0 + n_scr], refs[s0 + n_scr:]

        @pl.when(first())
        def _():
            for cp in xc.copies(side_in, side_out, *sems):
                cp.start()

        body(*ins, *outs, *scr)

        @pl.when(last())
        def _():
            for cp in xc.copies(side_in, side_out, *sems):
                cp.wait()

    return wrapped


def _call_with_exchange(body, xc, first, last, args, *, in_specs, out_specs, out_shape, scratch_shapes=(), **kw):
    if xc is None:
        return _pcall(body, in_specs=in_specs, out_specs=out_specs, out_shape=out_shape, scratch_shapes=scratch_shapes, **kw)(*args), None
    wrapped = _with_exchange(body, len(in_specs), len(out_specs), len(scratch_shapes), xc, first, last)
    res = _pcall(wrapped, in_specs=list(in_specs) + xc.specs, out_specs=list(out_specs) + xc.specs,
                 out_shape=list(out_shape) + xc.out_shape, scratch_shapes=list(scratch_shapes) + xc.scratch, **kw)(*args, *xc.arrs)
    return res[:len(out_specs)], res[len(out_specs):]


def _sum_slots(a, *, name, out_dtype=F32):
    ns, r, c = a.shape
    t = _tile(r, max(16, (1 << 18) // c // 16 * 16), 16)

    def body(a_ref, o_ref):
        acc = a_ref[0].astype(F32)
        for i in range(1, ns):
            acc = acc + a_ref[i].astype(F32)
        o_ref[...] = acc.astype(o_ref.dtype)

    return _pcall(body, name=name, grid=(r // t,), in_specs=[pl.BlockSpec((ns, t, c), lambda i: (0, i, 0))],
                  out_specs=pl.BlockSpec((t, c), lambda i: (i, 0)), out_shape=jax.ShapeDtypeStruct((r, c), out_dtype),
                  compiler_params=_cparams("parallel"))(a)


def _adamw(w, gparts, m, v, *, name):
    r, c = w.shape
    t = _tile(r, max(16, (1 << 17) // c // 16 * 16), 16)
    ng = len(gparts)

    def body(*refs):
        w_ref, m_ref, v_ref = refs[0], refs[1 + ng], refs[2 + ng]
        g_ref, d_ref, nm_ref, nv_ref = refs[3 + ng:]
        g = refs[1][...].astype(F32)
        for i in range(1, ng):
            g = g + refs[1 + i][...].astype(F32)
        mn = ADAM_B1 * m_ref[...] + (1.0 - ADAM_B1) * g
        vn = ADAM_B2 * v_ref[...] + (1.0 - ADAM_B2) * (g * g)
        m_hat = mn / (1.0 - ADAM_B1 ** ADAM_STEP)
        v_hat = vn / (1.0 - ADAM_B2 ** ADAM_STEP)
        g_ref[...] = g
        d_ref[...] = -ADAM_LR * (m_hat / (jnp.sqrt(v_hat) + ADAM_EPS) + ADAM_WD * w_ref[...])
        nm_ref[...] = mn
        nv_ref[...] = vn

    spec = pl.BlockSpec((t, c), lambda i: (i, 0))
    return _pcall(body, name=name, grid=(r // t,), in_specs=[spec] * (3 + ng), out_specs=[spec] * 4,
                  out_shape=[jax.ShapeDtypeStruct((r, c), F32)] * 4, compiler_params=_cparams("parallel"))(w, *gparts, m, v)


def _adamw_layers(w, halves, m, v, *, name):
    r, c = w.shape
    nl = len(halves)
    rl = r // nl
    t = _tile(rl, max(16, (1 << 17) // c // 16 * 16), 16)
    nbl = rl // t

    def body(*refs):
        w_ref, m_ref, v_ref = refs[0], refs[1 + 2 * nl], refs[2 + 2 * nl]
        g_ref, d_ref, nm_ref, nv_ref = refs[3 + 2 * nl:]
        g = None
        for l in range(nl):
            gl = refs[1 + 2 * l][...].astype(F32) + refs[2 + 2 * l][...].astype(F32)
            g = gl if g is None else jnp.where(pl.program_id(0) >= l * nbl, gl, g)
        mn = ADAM_B1 * m_ref[...] + (1.0 - ADAM_B1) * g
        vn = ADAM_B2 * v_ref[...] + (1.0 - ADAM_B2) * (g * g)
        m_hat = mn / (1.0 - ADAM_B1 ** ADAM_STEP)
        v_hat = vn / (1.0 - ADAM_B2 ** ADAM_STEP)
        g_ref[...] = g
        d_ref[...] = -ADAM_LR * (m_hat / (jnp.sqrt(v_hat) + ADAM_EPS) + ADAM_WD * w_ref[...])
        nm_ref[...] = mn
        nv_ref[...] = vn

    spec = pl.BlockSpec((t, c), lambda i: (i, 0))
    part = lambda l, core: pl.BlockSpec((None, t, c), lambda i, l=l, core=core: (core, jnp.clip(i - l * nbl, 0, nbl - 1), 0))
    gspecs = [part(l, core) for l in range(nl) for core in range(2)]
    gargs = [halves[l] for l in range(nl) for _ in range(2)]
    return _pcall(body, name=name, grid=(r // t,), in_specs=[spec] + gspecs + [spec, spec], out_specs=[spec] * 4,
                  out_shape=[jax.ShapeDtypeStruct((r, c), F32)] * 4, compiler_params=_cparams("parallel"))(w, *gargs, m, v)


FFN_TM = 512
FFN_CHUNK = 1408


def _resident(shape):
    return pl.BlockSpec(shape, lambda i: (0,) * len(shape), pipeline_mode=pl.Buffered(1))


def _ffn_up(x, sc, sh, wgu, *, name, side=None):
    s, d = x.shape
    f = wgu.shape[1] // 2
    t, fc = _rows(s, FFN_TM), _tile(f, FFN_CHUNK)

    def body(x_ref, sc_ref, sh_ref, w_ref, h_ref, uv_ref, a_ref):
        xv = x_ref[...]
        r = lax.rsqrt(jnp.mean(xv * xv, axis=1, keepdims=True) + EPS)
        hb = ((xv * r) * (1.0 + sc_ref[...]) + sh_ref[...]).astype(BF16)
        h_ref[...] = hb
        for j in range(f // fc):
            u = _dot(hb, w_ref[:, j * fc:(j + 1) * fc])
            v = _dot(hb, w_ref[:, f + j * fc:f + (j + 1) * fc])
            sg = _sigmoid(u)
            silu = u * sg
            uv_ref[:, j * fc:(j + 1) * fc] = (v * (sg * (1.0 + u * (1.0 - sg)))).astype(BF16)
            uv_ref[:, f + j * fc:f + (j + 1) * fc] = silu.astype(BF16)
            a_ref[:, j * fc:(j + 1) * fc] = (silu * v).astype(BF16)

    return _call_with_exchange(
        body, side, lambda: pl.program_id(0) == 0, lambda: pl.program_id(0) == s // t - 1, (x, sc, sh, wgu),
        name=name, grid=(s // t,), in_specs=[_row_spec(t, d), _vec_spec(d), _vec_spec(d), _resident(wgu.shape)],
        out_specs=[_row_spec(t, d), _row_spec(t, 2 * f), _row_spec(t, f)],
        out_shape=[jax.ShapeDtypeStruct((s, d), BF16), jax.ShapeDtypeStruct((s, 2 * f), BF16), jax.ShapeDtypeStruct((s, f), BF16)],
        compiler_params=_cparams("parallel" if side is None else "arbitrary"))


def _norm_mm(x, sc, sh, w, nb, *, name):
    s, d = x.shape
    n = w.shape[1]
    t, nc = _rows(s, FFN_TM), _tile(n, 1792)
    assert nb <= nc

    def body(x_ref, sc_ref, sh_ref, w_ref, h_ref, o_ref, ob_ref):
        xv = x_ref[...]
        r = lax.rsqrt(jnp.mean(xv * xv, axis=1, keepdims=True) + EPS)
        hb = ((xv * r) * (1.0 + sc_ref[...]) + sh_ref[...]).astype(BF16)
        h_ref[...] = hb
        for j in range(n // nc):
            part = _dot(hb, w_ref[:, j * nc:(j + 1) * nc])
            o_ref[:, j * nc:(j + 1) * nc] = part
            if j == 0:
                ob_ref[...] = part[:, :nb].astype(BF16)

    return _pcall(
        body, name=name, grid=(s // t,), in_specs=[_row_spec(t, d), _vec_spec(d), _vec_spec(d), _resident(w.shape)],
        out_specs=[_row_spec(t, d), _row_spec(t, n), _row_spec(t, nb)],
        out_shape=[jax.ShapeDtypeStruct((s, d), BF16), jax.ShapeDtypeStruct((s, n), F32), jax.ShapeDtypeStruct((s, nb), BF16)],
        compiler_params=_cparams("parallel"))(x, sc, sh, w)


def _ffn_dact(dxo, y, sg, wd, uv, *, name, side=None):
    s, d = y.shape
    f = wd.shape[0]
    t, fc = _rows(s, FFN_TM), _tile(f, FFN_CHUNK)

    def body(dxo_ref, y_ref, sg_ref, w_ref, uv_ref, dy_ref, duv_ref, ds_ref):
        @pl.when(pl.program_id(0) == 0)
        def _():
            ds_ref[...] = jnp.zeros_like(ds_ref)

        dv = dxo_ref[...]
        dyb = (sg_ref[...] * dv).astype(BF16)
        dy_ref[...] = dyb
        ds_ref[...] += jnp.sum(dv * y_ref[...], axis=0, keepdims=True)
        for j in range(f // fc):
            da = _dot(dyb, w_ref[j * fc:(j + 1) * fc, :], NT)
            duv_ref[:, j * fc:(j + 1) * fc] = (da * uv_ref[:, j * fc:(j + 1) * fc].astype(F32)).astype(BF16)
            duv_ref[:, f + j * fc:f + (j + 1) * fc] = (da * uv_ref[:, f + j * fc:f + (j + 1) * fc].astype(F32)).astype(BF16)

    return _call_with_exchange(
        body, side, lambda: pl.program_id(0) == 0, lambda: pl.program_id(0) == s // t - 1, (dxo, y, sg, wd, uv),
        name=name, grid=(s // t,),
        in_specs=[_row_spec(t, d), _row_spec(t, d), _vec_spec(d), _resident(wd.shape), _row_spec(t, 2 * f)],
        out_specs=[_row_spec(t, d), _row_spec(t, 2 * f), _vec_spec(d)],
        out_shape=[jax.ShapeDtypeStruct((s, d), BF16), jax.ShapeDtypeStruct((s, 2 * f), BF16), jax.ShapeDtypeStruct((1, d), F32)],
        compiler_params=_cparams("arbitrary"))


def _ffn_dh(duv, wgu, x, sc, dxo, *, name, side=None):
    s, d = x.shape
    f2 = wgu.shape[1]
    t = _rows(s, FFN_TM)

    def body(duv_ref, w_ref, x_ref, sc_ref, dxo_ref, dx_ref, dsc_ref, dsh_ref):
        @pl.when(pl.program_id(0) == 0)
        def _():
            dsc_ref[...] = jnp.zeros_like(dsc_ref)
            dsh_ref[...] = jnp.zeros_like(dsh_ref)

        dhv = _dot(duv_ref[...], w_ref[...], NT)
        xv = x_ref[...]
        r = lax.rsqrt(jnp.mean(xv * xv, axis=1, keepdims=True) + EPS)
        xn = xv * r
        dxn = dhv * (1.0 + sc_ref[...])
        dx_ref[...] = dxo_ref[...] + r * (dxn - xn * jnp.mean(dxn * xn, axis=1, keepdims=True))
        dsc_ref[...] += jnp.sum(dhv * xn, axis=0, keepdims=True)
        dsh_ref[...] += jnp.sum(dhv, axis=0, keepdims=True)

    return _call_with_exchange(
        body, side, lambda: pl.program_id(0) == 0, lambda: pl.program_id(0) == s // t - 1, (duv, wgu, x, sc, dxo),
        name=name, grid=(s // t,),
        in_specs=[_row_spec(t, f2), _resident(wgu.shape), _row_spec(t, d), _vec_spec(d), _row_spec(t, d)],
        out_specs=[_row_spec(t, d), _vec_spec(d), _vec_spec(d)],
        out_shape=[jax.ShapeDtypeStruct((s, d), F32), jax.ShapeDtypeStruct((1, d), F32), jax.ShapeDtypeStruct((1, d), F32)],
        compiler_params=_cparams("arbitrary"))


def _dh_pieces(pieces, w, x, sc, dxo, *, name):
    s, d = x.shape
    t = _rows(s, FFN_TM)
    widths = [p.shape[1] for p in pieces]
    offs = [sum(widths[:i]) for i in range(len(widths))]
    kd = sum(widths)
    npc = len(pieces)

    def body(*refs):
        p_refs = refs[:npc]
        w_ref, x_ref, sc_ref, dxo_ref, dx_ref, dsc_ref, dsh_ref, cat_ref = refs[npc:]

        @pl.when(pl.program_id(0) == 0)
        def _():
            dsc_ref[...] = jnp.zeros_like(dsc_ref)
            dsh_ref[...] = jnp.zeros_like(dsh_ref)

        dhv = None
        for p_ref, off, wd in zip(p_refs, offs, widths):
            pb = p_ref[...].astype(BF16)
            cat_ref[:, off:off + wd] = pb
            part = _dot(pb, w_ref[:, off:off + wd], NT)
            dhv = part if dhv is None else dhv + part
        xv = x_ref[...]
        r = lax.rsqrt(jnp.mean(xv * xv, axis=1, keepdims=True) + EPS)
        xn = xv * r
        dxn = dhv * (1.0 + sc_ref[...])
        dx_ref[...] = dxo_ref[...] + r * (dxn - xn * jnp.mean(dxn * xn, axis=1, keepdims=True))
        dsc_ref[...] += jnp.sum(dhv * xn, axis=0, keepdims=True)
        dsh_ref[...] += jnp.sum(dhv, axis=0, keepdims=True)

    return _pcall(
        body, name=name, grid=(s // t,),
        in_specs=[_row_spec(t, wd) for wd in widths] + [_resident(w.shape), _row_spec(t, d), _vec_spec(d), _row_spec(t, d)],
        out_specs=[_row_spec(t, d), _vec_spec(d), _vec_spec(d), _row_spec(t, kd)],
        out_shape=[jax.ShapeDtypeStruct((s, d), F32), jax.ShapeDtypeStruct((1, d), F32), jax.ShapeDtypeStruct((1, d), F32),
                   jax.ShapeDtypeStruct((s, kd), BF16)],
        compiler_params=_cparams("arbitrary"))(*pieces, w, x, sc, dxo)


def _ffn_fwd(x, sh, sc, g, wgu, wd, tag, side=None):
    (h, uv, a), got = _ffn_up(x, sc, sh, wgu, name=f"{tag}_up", side=side)
    y, xo = _mm(a, wd, name=f"{tag}_down", tm=512, tn=1024, tk=wd.shape[0], res=x, scale=0.5 * g)
    return xo, (x, h, uv, a, y), got


def _ffn_bwd(dxo, saved, sc, g, wgu, wd, tag, ride=None):
    x, h, uv, a, y = saved
    if ride is None:
        (dyb, duv, dgs), _ = _ffn_dact(dxo, y, 0.5 * g, wd, uv, name=f"{tag}_dact")
        (dx, dsc, dsh), _ = _ffn_dh(duv, wgu, x, sc, dxo, name=f"{tag}_dh")
        dwgu = _mm_tn(h, duv, name=f"{tag}_dwgu", tm=1024, tn=1408, tk=512)
        dwd = _mm_tn(a, dyb, name=f"{tag}_dwd", tm=1408, tn=1024, tk=512)
        return dx, dwgu, dwd, dsh, dsc, 0.5 * dgs, {}
    (dyb, duv, dgs), got = _ffn_dact(dxo, y, 0.5 * g, wd, uv, name=f"{tag}_dact", side=_Xchg([ride[1]], "chips", True))
    part = _sum_slots(got[0], name=f"{tag}_sum_a", out_dtype=BF16)
    dwd, swap_a = _mm_tn(a, dyb, name=f"{tag}_dwd", tm=1408, tn=1024, tk=512, side=_Xchg([part], "sib", False))
    (dx, dsc, dsh), got = _ffn_dh(duv, wgu, x, sc, dxo, name=f"{tag}_dh", side=_Xchg([_by_chip("ffn1_w_down", dwd)], "chips", True))
    part = _sum_slots(got[0], name=f"{tag}_sum_b", out_dtype=BF16)
    dwgu, swap_b = _mm_tn(h, duv, name=f"{tag}_dwgu", tm=1024, tn=1408, tk=512, side=_Xchg([part], "sib", False))
    return dx, dwgu, dwd, dsh, dsc, 0.5 * dgs, {ride[0]: swap_a[0], "ffn1_w_down": swap_b[0]}


def _layer_fwd(x0, mod, w, par, tag, carry=None, own=None):
    sh1, sc1, g1, sh2, sc2, g2, sh3, sc3, g3 = (mod[i:i + 1] for i in range(N_MOD))
    x1, f1, got = _ffn_fwd(x0, sh1, sc1, g1, w["gu1"], w["d1"], f"{tag}_ffn1",
                           side=None if own is None else _Xchg(own[0], "chips", False))
    if own is not None:
        w = {**w, **own[1](got)}
    h2, proj, qkv = _norm_mm(x1, sc2, sh2, w["in"], 3 * SB_W, name=f"{tag}_in")
    o_a, got_a = _sb_fwd(qkv, name=f"{tag}_sb", side=None if carry is None else _Xchg(carry[0], "chips", False))
    qd, kd = _dil_prep(proj, par["gq"], par["gk"], par["cos"], par["sin"], name=f"{tag}_dil_prep")
    nums, dens, mxs = [], [], []
    for _, r in DIL_PATTERNS:
        nu, de, mx = _dil_fwd(qd, kd, proj, r, name=f"{tag}_dil{r}")
        nums.append(nu)
        dens.append(de)
        mxs.append(mx)
    (oh, st), got_b = _hgrn_fwd(proj, par["la"], par["lc"], name=f"{tag}_hgrn",
                                side=None if carry is None else _Xchg(carry[1], "chips", False))
    ymix, od, mall, zall = _mix_out(o_a, nums, dens, mxs, oh, proj, par["hg"], name=f"{tag}_mix_out")
    out, x2 = _mm(ymix, w["out"], name=f"{tag}_out", tm=512, tn=1024, tk=1024, res=x1, scale=g2)
    x3, f2, _ = _ffn_fwd(x2, sh3, sc3, g3, w["gu2"], w["d2"], f"{tag}_ffn2")
    return x3, dict(f1=f1, f2=f2, x1=x1, h2=h2, proj=proj, qkv=qkv, o_a=o_a, qd=qd, kd=kd, oh=oh, st=st,
                    ymix=ymix, od=od, mall=mall, zall=zall, out=out), (None if carry is None else (got_a, got_b)), w


COL_SHARDED = ("ffn1_w_gate", "ffn1_w_up", "w_in", "ffn2_w_gate", "ffn2_w_up")
ROW_SHARDED = ("ffn1_w_down", "w_out", "ffn2_w_down")


def _by_chip(name, g):
    if name in COL_SHARDED:
        return jnp.moveaxis(g.reshape(g.shape[0], 4, -1), 1, 0)
    return g.reshape(4, -1, g.shape[1])


def _halves(x):
    f = x.shape[1] // 2
    return x[:, :f], x[:, f:]


def _layer_bwd(dx3, sv, mod, w, par, tag, pending):
    sh1, sc1, g1, sh2, sc2, g2, sh3, sc3, g3 = (mod[i:i + 1] for i in range(N_MOD))
    dx2, dwgu2, dwd2, dsh3, dsc3, dg3, _ = _ffn_bwd(dx3, sv["f2"], sc3, g3, w["gu2"], w["d2"], f"{tag}_ffn2")
    doutb, dg2 = _gate_bwd(dx2, sv["out"], g2, name=f"{tag}_dgate2")
    dmix = _mm(doutb, w["out"], name=f"{tag}_dmix", tb=True, tm=512, tn=1024, tk=1024)
    dwout = _mm_tn(sv["ymix"], doutb, name=f"{tag}_dwout", tm=1024, tn=1024, tk=512)
    proj = sv["proj"]
    doh, dgh, delta, dhg = _mix_out_bwd(dmix, sv["oh"], proj, par["hg"], sv["od"], name=f"{tag}_dmix_out")
    dgate2, dup2 = _halves(dwgu2)
    ride = dict(pending)
    for n, g in (("ffn2_w_gate", dgate2), ("ffn2_w_up", dup2), ("ffn2_w_down", dwd2), ("w_out", dwout)):
        ride[(tag, n)] = _by_chip(n, g)
    keys = list(ride)
    (dqa, dka, dva), got = _sb_bwd(sv["qkv"], sv["o_a"], dmix, name=f"{tag}_dsb", side=_Xchg([ride[k] for k in keys], "chips", True))
    parts = [_sum_slots(g, name=f"{tag}_sum{i}", out_dtype=BF16) for i, g in enumerate(got)]
    dqs, dks, dvs = [], [], []
    for _, r in DIL_PATTERNS:
        a, b, c = _dil_bwd(sv["qd"], sv["kd"], proj, dmix, sv["mall"], sv["zall"], delta, r, name=f"{tag}_ddil{r}")
        dqs.append(a)
        dks.append(b)
        dvs.append(c)
    dqd, dkd, dvd, dgq, dgk = _dil_prep_bwd(proj, par["gq"], par["gk"], par["cos"], par["sin"], dqs, dks, dvs,
                                             name=f"{tag}_ddil_prep")
    (dqh, dfh, dih, dla, dlc), swapped = _hgrn_bwd(proj, par["la"], par["lc"], sv["st"], doh, name=f"{tag}_dhgrn",
                                                   side=_Xchg(parts, "sib", False))
    dx1, dsc2, dsh2, dproj = _dh_pieces([dqa, dka, dva, dqd, dkd, dvd, dqh, dfh, dih, dgh], w["in"], sv["x1"], sc2, dx2,
                                         name=f"{tag}_dh2")
    dwin = _mm_tn(sv["h2"], dproj, name=f"{tag}_dwin", tm=1024, tn=1792, tk=512)
    dx0, dwgu1, _, dsh1, dsc1, dg1, rode = _ffn_bwd(dx1, sv["f1"], sc1, g1, w["gu1"], w["d1"], f"{tag}_ffn1",
                                                    ride=("w_in", _by_chip("w_in", dwin)))
    dmod = jnp.concatenate([dsh1, dsc1, dg1, dsh2, dsc2, dg2, dsh3, dsc3, dg3], axis=0)
    fold = lambda v: v.reshape(2, HEAD_DIM).sum(axis=0)
    dgate1, dup1 = _halves(dwgu1)
    late = {(tag, n): _by_chip(n, g) for n, g in (("ffn1_w_gate", dgate1), ("ffn1_w_up", dup1))}
    grads = dict(late=late, dmod=dmod, gq=fold(dgq), gk=fold(dgk), hg=dhg[0], la=dla[0], lc=dlc[0])
    return dx0, grads, {**dict(zip(keys, swapped)), **{(tag, n): v for n, v in rode.items()}}


def _pack(pieces):
    flat = jnp.concatenate([p.reshape(-1) for p in pieces])
    pad = (-flat.shape[0]) % (8 * LANES)
    return jnp.pad(flat, (0, pad)).reshape(-1, LANES)


def _unpack(flat, like):
    out, off = [], 0
    for p in like:
        out.append(flat[off:off + p.size].reshape(p.shape))
        off += p.size
    return out


def kernel(x, c, w_mod, b_mod, ffn1_w_gate, ffn1_w_up, ffn1_w_down, w_in, w_out, q_norm_g, k_norm_g, hgrn_norm_g, hgrn_lb_logits, ffn2_w_gate, ffn2_w_up, ffn2_w_down, loss_target, m_w_mod, m_b_mod, m_ffn1_w_gate, m_ffn1_w_up, m_ffn1_w_down, m_w_in, m_w_out, m_q_norm_g, m_k_norm_g, m_hgrn_norm_g, m_hgrn_lb_logits, m_ffn2_w_gate, m_ffn2_w_up, m_ffn2_w_down, v_w_mod, v_b_mod, v_ffn1_w_gate, v_ffn1_w_up, v_ffn1_w_down, v_w_in, v_w_out, v_q_norm_g, v_k_norm_g, v_hgrn_norm_g, v_hgrn_lb_logits, v_ffn2_w_gate, v_ffn2_w_up, v_ffn2_w_down):
    names = ["w_mod", "b_mod", "ffn1_w_gate", "ffn1_w_up", "ffn1_w_down", "w_in", "w_out", "q_norm_g", "k_norm_g",
             "hgrn_norm_g", "hgrn_lb_logits", "ffn2_w_gate", "ffn2_w_up", "ffn2_w_down"]
    wts = dict(zip(names, (w_mod, b_mod, ffn1_w_gate, ffn1_w_up, ffn1_w_down, w_in, w_out, q_norm_g, k_norm_g, hgrn_norm_g,
                           hgrn_lb_logits, ffn2_w_gate, ffn2_w_up, ffn2_w_down)))
    mom = dict(zip(names, (m_w_mod, m_b_mod, m_ffn1_w_gate, m_ffn1_w_up, m_ffn1_w_down, m_w_in, m_w_out, m_q_norm_g, m_k_norm_g,
                           m_hgrn_norm_g, m_hgrn_lb_logits, m_ffn2_w_gate, m_ffn2_w_up, m_ffn2_w_down)))
    var = dict(zip(names, (v_w_mod, v_b_mod, v_ffn1_w_gate, v_ffn1_w_up, v_ffn1_w_down, v_w_in, v_w_out, v_q_norm_g, v_k_norm_g,
                           v_hgrn_norm_g, v_hgrn_lb_logits, v_ffn2_w_gate, v_ffn2_w_up, v_ffn2_w_down)))
    depth = w_mod.shape[0]
    assert depth == 2 and x.shape[0] == 1
    s, d = x.shape[1:]
    assert s % (DIL_PATTERNS[-1][1] * QBLK) == 0 and d % LANES == 0
    xi, yi, ci = lax.axis_index("x"), lax.axis_index("y"), lax.axis_index("c")
    chip = 2 * xi + yi
    dev = 2 * chip + ci
    x0, tgt = x[0], loss_target[0]

    c8 = _exchange([c.reshape(d // LANES, LANES)], "all", False, name="gather_c")[0].reshape(8, d)
    ncol = w_mod.shape[2]
    b_loc = lax.dynamic_slice_in_dim(b_mod, chip * ncol, ncol, axis=1)
    m_loc = _mod_fwd(c8, w_mod, b_loc.reshape(depth, 1, ncol), name="mod_fwd")
    m_all = _exchange([m_loc], "chips", False, name="gather_mod")[0]
    mod = jnp.transpose(lax.dynamic_index_in_dim(m_all, dev, axis=2, keepdims=False), (1, 0, 2)).reshape(depth, N_MOD, d)

    col_sharded = ["ffn1_w_gate", "ffn1_w_up", "w_in", "ffn2_w_gate", "ffn2_w_up"]
    row_sharded = ["ffn1_w_down", "w_out", "ffn2_w_down"]
    big = col_sharded + row_sharded
    early = ["ffn1_w_gate", "ffn1_w_up", "ffn1_w_down", "w_in", "w_out"]
    late = ["ffn2_w_gate", "ffn2_w_up", "ffn2_w_down"]

    def shards(l, group):
        return [wts[n][l].astype(BF16) for n in group]

    def whole(group, gathered):
        out = {}
        for n, g in zip(group, gathered):
            out[n] = jnp.moveaxis(g, 0, 1).reshape(g.shape[1], -1) if n in col_sharded else g.reshape(-1, g.shape[2])
        return out

    def early_weights(gathered):
        full = whole(early, gathered)
        return {"gu1": jnp.concatenate([full["ffn1_w_gate"], full["ffn1_w_up"]], axis=1), "d1": full["ffn1_w_down"],
                "in": full["w_in"], "out": full["w_out"]}

    def late_weights(gathered):
        full = whole(late, gathered)
        return {"gu2": jnp.concatenate([full["ffn2_w_gate"], full["ffn2_w_up"]], axis=1), "d2": full["ffn2_w_down"]}

    ws = [early_weights(_exchange(shards(0, early), "chips", False, name="gather_w"))]

    _, la, lc = _lb_prep(hgrn_lb_logits, name="lb_prep")
    cos, sin = _rope_tables(s)
    pars = [dict(gq=jnp.tile(q_norm_g[l], 2)[None], gk=jnp.tile(k_norm_g[l], 2)[None], hg=hgrn_norm_g[l][None],
                 la=la[l:l + 1], lc=lc[l:l + 1], cos=cos, sin=sin) for l in range(depth)]

    xs, saved = x0, []
    for l in range(depth):
        carry = (shards(l + 1, early), shards(l + 1, late)) if l + 1 < depth else None
        own = (shards(l, late), late_weights) if l == 0 else None
        xs, sv, got, ws[l] = _layer_fwd(xs, mod[l], ws[l], pars[l], f"l{l}", carry, own)
        saved.append(sv)
        if got is not None:
            ws.append({**early_weights(got[0]), **late_weights(got[1])})
    dx, lpart = _loss_grad(xs, tgt, name="loss")

    grads, halves, pending = [None] * depth, {}, {}
    for l in reversed(range(depth)):
        dx, grads[l], swapped = _layer_bwd(dx, saved[l], mod[l], ws[l], pars[l], f"l{l}", pending)
        halves.update(swapped)
        pending = grads[l]["late"]
    keys = list(pending)
    got = _exchange([pending[k] for k in keys], "chips", True, name="scatter_grads")
    parts = [_sum_slots(g, name=f"sum_{k[1]}", out_dtype=BF16) for k, g in zip(keys, got)]
    halves.update(zip(keys, _exchange(parts, "sib", False, name="swap_grads")))

    stack = lambda k: jnp.stack([grads[l][k] for l in range(depth)])
    small = [stack("dmod"), stack("gq"), stack("gk"), stack("hg"), stack("la"), stack("lc"), lpart[0, :1]]
    packed = _pack(small)
    allp = _exchange([packed], "all", False, name="gather_small")[0]
    tot = _unpack(_sum_slots(allp, name="sum_small").reshape(-1), small)
    g_b_mod = tot[0].reshape(depth, N_MOD * d)
    loss = tot[6][0]
    g_small = {"b_mod": g_b_mod, "q_norm_g": tot[1], "k_norm_g": tot[2], "hgrn_norm_g": tot[3],
               "hgrn_lb_logits": _lb_bwd(hgrn_lb_logits, tot[4], tot[5], name="lb_bwd")}

    dm_all = allp.reshape(8, -1)[:, :depth * N_MOD * d].reshape(8, depth, N_MOD * d)
    dm_loc = jnp.transpose(lax.dynamic_slice_in_dim(dm_all, chip * ncol, ncol, axis=2), (1, 0, 2))
    g_w_mod = _mod_bwd(c8.T, dm_loc, name="mod_bwd")

    outs = {}
    for n in names:
        w2 = wts[n].reshape(-1, wts[n].shape[-1])
        m2, v2 = mom[n].reshape(w2.shape), var[n].reshape(w2.shape)
        if n in big:
            res = _adamw_layers(w2, [halves[(f"l{l}", n)] for l in range(depth)], m2, v2, name=f"adamw_{n}")
        else:
            g = g_w_mod if n == "w_mod" else g_small[n]
            res = _adamw(w2, [g.reshape(w2.shape)], m2, v2, name=f"adamw_{n}")
        outs[n] = [r.reshape(wts[n].shape) for r in res]
    return (loss, dx[None], *[outs[n][0] for n in names], *[outs[n][1] for n in names], *[outs[n][2] for n in names],
            *[outs[n][3] for n in names])
```

```python
import jax
import jax.numpy as jnp
from jax import lax
from jax.experimental import pallas as pl
from jax.experimental.pallas import tpu as pltpu

F32 = jnp.float32
BF16 = jnp.bfloat16
MESH_ID = pl.DeviceIdType.MESH

HEAD_DIM = 64
SB_W = 256
DIL_W = 256
HG_W = 512
HG_D = 128
IN_W = 3 * SB_W + 3 * DIL_W + 4 * HG_W
MIX_W = SB_W + DIL_W + HG_W
DIL_PATTERNS = ((128, 1), (512, 4), (2048, 16))
ROPE_THETA = 10000.0
EPS = 1e-6
LB_FLOOR = 1e-30
NEG_BIG = -1e30
N_MOD = 9
ADAM_LR = 0.001
ADAM_B1 = 0.9
ADAM_B2 = 0.999
ADAM_EPS = 1e-08
ADAM_WD = 0.01
ADAM_STEP = 10

LANES = 128
QBLK = 128
DIL_TILE = 1024
HG_BLK = 16
HG_TILE = 256
SB_EXIT = 88.0
VMEM_LIMIT = 48 * 1024 * 1024
XCHG_CHUNKS = 8
XCHG_MIN_BYTES = 1 << 19

NN = (((1,), (0,)), ((), ()))
NT = (((1,), (1,)), ((), ()))
TN = (((0,), (0,)), ((), ()))


def _pcall(body, **kw):
    return pl.pallas_call(body, **kw)


def _cparams(*sem):
    return pltpu.CompilerParams(dimension_semantics=sem if sem else None, vmem_limit_bytes=VMEM_LIMIT)


def _dot(a, b, dims=NN):
    return lax.dot_general(a, b, dims, preferred_element_type=F32)


def _split(x, n):
    parts = []
    r = x
    for i in range(n):
        p = r.astype(BF16)
        parts.append(p)
        if i + 1 < n:
            r = r - p.astype(F32)
    return parts


def _xdot(x, m, n=2):
    return sum(_dot(p, m) for p in _split(x, n))


def _iota(shape, dim):
    return lax.broadcasted_iota(jnp.int32, shape, dim)


def _sigmoid(x):
    return 1.0 / (1.0 + jnp.exp(-x))


def _tile(dim, pref, mult=LANES):
    t = (min(pref, dim) // mult) * mult
    while t >= mult:
        if dim % t == 0:
            return t
        t -= mult
    return dim


def _rows(dim, pref):
    return _tile(dim, pref, 8)


def _mm(a, b, *, name, tb=False, tm=512, tn=1024, tk=1024, out_dtype=F32, res=None, scale=None):
    m, kd = a.shape
    n = b.shape[0] if tb else b.shape[1]
    tm, tn, tk = _rows(m, tm), _tile(n, tn), _tile(kd, tk)
    nk = kd // tk
    epi = res is not None

    def body(*refs):
        if epi:
            a_ref, b_ref, r_ref, s_ref, o_ref, x_ref, acc = refs
        else:
            a_ref, b_ref, o_ref, acc = refs
        k = pl.program_id(2)

        @pl.when(k == 0)
        def _():
            acc[...] = jnp.zeros_like(acc)

        acc[...] += _dot(a_ref[...], b_ref[...], NT if tb else NN)

        @pl.when(k == nk - 1)
        def _():
            o_ref[...] = acc[...].astype(o_ref.dtype)
            if epi:
                x_ref[...] = r_ref[...] + s_ref[...] * acc[...]

    in_specs = [
        pl.BlockSpec((tm, tk), lambda i, j, k: (i, k)),
        pl.BlockSpec((tn, tk), lambda i, j, k: (j, k)) if tb else pl.BlockSpec((tk, tn), lambda i, j, k: (k, j)),
    ]
    out_shape = [jax.ShapeDtypeStruct((m, n), out_dtype)]
    out_specs = [pl.BlockSpec((tm, tn), lambda i, j, k: (i, j))]
    args = [a, b]
    if epi:
        in_specs += [pl.BlockSpec((tm, tn), lambda i, j, k: (i, j)), pl.BlockSpec((1, tn), lambda i, j, k: (0, j))]
        out_shape.append(jax.ShapeDtypeStruct((m, n), F32))
        out_specs.append(pl.BlockSpec((tm, tn), lambda i, j, k: (i, j)))
        args += [res, scale]
    out = _pcall(
        body, name=name, grid=(m // tm, n // tn, nk), in_specs=in_specs, out_specs=out_specs, out_shape=out_shape,
        scratch_shapes=[pltpu.VMEM((tm, tn), F32)], compiler_params=_cparams("parallel", "parallel", "arbitrary"),
    )(*args)
    return out if epi else out[0]


def _mm_tn(a, b, *, name, tm=1024, tn=1408, tk=512, out_dtype=BF16, side=None):
    s, m = a.shape
    n = b.shape[1]
    tm, tn, tk = _tile(m, tm), _tile(n, tn), _rows(s, tk)
    nk = s // tk
    ni, nj = m // tm, n // tn

    def body(a_ref, b_ref, o_ref, acc):
        k = pl.program_id(2)

        @pl.when(k == 0)
        def _():
            acc[...] = jnp.zeros_like(acc)

        acc[...] += _dot(a_ref[...], b_ref[...], TN)

        @pl.when(k == nk - 1)
        def _():
            o_ref[...] = acc[...].astype(o_ref.dtype)

    at = lambda i, j, k: (pl.program_id(0) == i) & (pl.program_id(1) == j) & (pl.program_id(2) == k)
    (out,), got = _call_with_exchange(
        body, side, lambda: at(0, 0, 0), lambda: at(ni - 1, nj - 1, nk - 1), (a, b), name=name, grid=(ni, nj, nk),
        in_specs=[pl.BlockSpec((tk, tm), lambda i, j, k: (k, i)), pl.BlockSpec((tk, tn), lambda i, j, k: (k, j))],
        out_specs=[pl.BlockSpec((tm, tn), lambda i, j, k: (i, j))], out_shape=[jax.ShapeDtypeStruct((m, n), out_dtype)],
        scratch_shapes=[pltpu.VMEM((tm, tn), F32)],
        compiler_params=_cparams(*(("parallel", "parallel", "arbitrary") if side is None else ("arbitrary",) * 3)))
    return out if side is None else (out, got)


TE = 512


def _row_spec(t, w, col=0):
    return pl.BlockSpec((t, w), lambda i, col=col: (i, col))


def _vec_spec(w, col=0):
    return pl.BlockSpec((1, w), lambda i, col=col: (0, col))


def _gate_bwd(dxo, y, sg, *, name):
    s, d = y.shape
    t = _rows(s, TE)

    def body(dxo_ref, y_ref, sg_ref, dy_ref, ds_ref):
        @pl.when(pl.program_id(0) == 0)
        def _():
            ds_ref[...] = jnp.zeros_like(ds_ref)

        dv = dxo_ref[...]
        dy_ref[...] = (sg_ref[...] * dv).astype(BF16)
        ds_ref[...] += jnp.sum(dv * y_ref[...], axis=0, keepdims=True)

    return _pcall(
        body, name=name, grid=(s // t,), in_specs=[_row_spec(t, d), _row_spec(t, d), _vec_spec(d)],
        out_specs=[_row_spec(t, d), _vec_spec(d)],
        out_shape=[jax.ShapeDtypeStruct((s, d), BF16), jax.ShapeDtypeStruct((1, d), F32)],
        compiler_params=_cparams("arbitrary"))(dxo, y, sg)


def _loss_grad(y, tgt, *, name):
    s, d = y.shape
    t = _rows(s, TE)
    nt = s // t

    def body(y_ref, t_ref, dy_ref, l_ref, acc):
        i = pl.program_id(0)

        @pl.when(i == 0)
        def _():
            acc[...] = jnp.zeros_like(acc)

        e = y_ref[...] - t_ref[...]
        dy_ref[...] = e * (1.0 / d)
        acc[...] += jnp.sum(e * e, axis=0, keepdims=True)

        @pl.when(i == nt - 1)
        def _():
            l_ref[...] = jnp.broadcast_to(jnp.sum(acc[...], axis=1, keepdims=True) * (0.5 / d), l_ref.shape)

    return _pcall(
        body, name=name, grid=(nt,), in_specs=[_row_spec(t, d), _row_spec(t, d)],
        out_specs=[_row_spec(t, d), pl.BlockSpec((1, LANES), lambda i: (0, 0))],
        out_shape=[jax.ShapeDtypeStruct((s, d), F32), jax.ShapeDtypeStruct((1, LANES), F32)],
        scratch_shapes=[pltpu.VMEM((1, d), F32)], compiler_params=_cparams("arbitrary"))(y, tgt)


SB_TQ = 256
SB_NK = SB_TQ // QBLK


def _sb_consts():
    r = _iota((QBLK, LANES), 0)
    c = _iota((QBLK, LANES), 1)
    ones = jnp.ones((QBLK, LANES), BF16)
    after = jnp.concatenate([jnp.where(r > c, 1.0, 0.0).astype(BF16), ones], axis=1)
    from_ = jnp.concatenate([jnp.where(r >= c, 1.0, 0.0).astype(BF16), ones], axis=1)
    return _iota((SB_TQ, LANES), 0), _iota((SB_TQ, LANES), 1), after, from_


def _sb_scores(qm, kb, strict):
    z = _dot(qm, kb, NT) * (HEAD_DIM ** -0.5)
    sp = jnp.log(1.0 + jnp.exp(-jnp.abs(z)))
    lnb = -(jnp.maximum(z, 0.0) + sp)
    lb = jnp.minimum(z, 0.0) - sp
    if strict is not None:
        lnb = jnp.where(strict, lnb, 0.0)
    return lnb, lb


def _sb_fwd(qkv, *, name, side=None):
    s = qkv.shape[0]
    nq = s // SB_TQ

    def body(q_ref, k_ref, v_ref, o_ref, *scr):
        acc, osc = scr[:4], scr[4:]
        qi = pl.program_id(0)
        row, lane, after, _ = _sb_consts()
        h0 = lane < HEAD_DIM
        q = q_ref[...]
        qms = []
        for p in range(2):
            qp = q[:, p * LANES:(p + 1) * LANES]
            qms += [jnp.where(h0, qp, jnp.zeros_like(qp)), jnp.where(h0, jnp.zeros_like(qp), qp)]

        def block(kj, mask, r0=0):
            rows = slice(r0, SB_TQ)
            off = pl.multiple_of(kj * QBLK, QBLK)
            kbs = [k_ref[pl.ds(off, QBLK), p * LANES:(p + 1) * LANES] for p in range(2)]
            vbs = [v_ref[pl.ds(off, QBLK), p * LANES:(p + 1) * LANES] for p in range(2)]
            mask = None if mask is None else mask[rows]
            sc = [_sb_scores(qms[c][rows], kbs[c // 2], mask) for c in range(4)]
            trs = [_xdot(sc[c][0], after) for c in range(4)]
            top = None
            for c in range(4):
                w = jnp.exp(sc[c][1] + trs[c][:, :QBLK] + acc[c][rows, :])
                if mask is not None:
                    w = jnp.where(mask, w, 0.0)
                osc[c][rows, :] += _xdot(w, vbs[c // 2])
                new = acc[c][rows, :] + trs[c][:, QBLK:]
                acc[c][rows, :] = new
                top = new if top is None else jnp.maximum(top, new)
            return jnp.max(top)

        for ref in scr:
            ref[...] = jnp.zeros_like(ref)
        top = None
        for j in reversed(range(SB_NK)):
            top = block(qi * SB_NK + j, (lane + j * QBLK) < row, j * QBLK)
        lax.while_loop(lambda c: (c[0] >= 0) & (c[1] > -SB_EXIT), lambda c: (c[0] - 1, block(c[0], None)),
                       (qi * SB_NK - 1, top))
        for p in range(2):
            o_ref[:, p * LANES:(p + 1) * LANES] = jnp.where(h0, osc[2 * p][...], osc[2 * p + 1][...])

    (o,), got = _call_with_exchange(
        body, side, lambda: pl.program_id(0) == 0, lambda: pl.program_id(0) == nq - 1, (qkv, qkv, qkv), name=name, grid=(nq,),
        in_specs=[pl.BlockSpec((SB_TQ, SB_W), lambda i: (i, 0)),
                  pl.BlockSpec((s, SB_W), lambda i: (0, 1)),
                  pl.BlockSpec((s, SB_W), lambda i: (0, 2))],
        out_specs=[pl.BlockSpec((SB_TQ, SB_W), lambda i: (i, 0))],
        out_shape=[jax.ShapeDtypeStruct((s, SB_W), F32)],
        scratch_shapes=[pltpu.VMEM((SB_TQ, LANES), F32)] * 8,
        compiler_params=_cparams("arbitrary"))
    return o, got


def _sb_bwd(qkv, o, dmix, *, name, side=None):
    s = qkv.shape[0]
    nq = s // SB_TQ
    scale = HEAD_DIM ** -0.5

    def body(q_ref, k_ref, v_ref, o_ref, do_ref, dq_ref, dk_ref, dv_ref, a0, a1, r0, r1, dqs, dks, dvs):
        acc, racc = (a0, a1), (r0, r1)
        i = pl.program_id(1)
        qi = nq - 1 - i
        row, lane, after, from_ = _sb_consts()
        klane = _iota((QBLK, LANES), 1)
        khms = (klane < HEAD_DIM, klane >= HEAD_DIM)

        @pl.when(i == 0)
        def _():
            dks[...] = jnp.zeros_like(dks)
            dvs[...] = jnp.zeros_like(dvs)

        q = q_ref[...]
        do = do_ref[...]
        dob = do.astype(BF16)
        dd = do * o_ref[...]
        dol = (do - dob.astype(F32)).astype(BF16)
        zero = jnp.zeros_like(q)
        hms = (lane < HEAD_DIM, lane >= HEAD_DIM)
        qms = [jnp.where(hm, q, zero) for hm in hms]
        doms = [jnp.where(hm, dob, zero) for hm in hms]
        dols = [jnp.where(hm, dol, zero) for hm in hms]
        dsums = [jnp.sum(jnp.where(hm, dd, 0.0), axis=1, keepdims=True) for hm in hms]

        def block(kj, mask, start=0):
            rows = slice(start, SB_TQ)
            off = pl.multiple_of(kj * QBLK, QBLK)
            kb = k_ref[pl.ds(off, QBLK), :]
            vb = v_ref[pl.ds(off, QBLK), :]
            mask = None if mask is None else mask[rows]
            top, dq, dk, dv = None, None, None, None
            sc = [_sb_scores(qms[h][rows], kb, mask) for h in range(2)]
            trs = [_xdot(sc[h][0], after) for h in range(2)]
            dws = [_dot(doms[h][rows], vb, NT) + _dot(dols[h][rows], vb, NT) for h in range(2)]
            for h in range(2):
                lb, tr = sc[h][1], trs[h]
                w = jnp.exp(lb + tr[:, :QBLK] + acc[h][rows, :])
                if mask is not None:
                    w = jnp.where(mask, w, 0.0)
                g = w * dws[h]
                tg = _xdot(g, from_)
                before = dsums[h][rows] - (tg[:, :QBLK] + racc[h][rows, :])
                dz = g - jnp.exp(lb) * (g + before)
                if mask is not None:
                    dz = jnp.where(mask, dz, 0.0)
                dzb = (dz * scale).astype(BF16)
                dqh = _dot(dzb, jnp.where(khms[h], kb, jnp.zeros_like(kb)))
                dkh = _dot(dzb, qms[h][rows], TN)
                dvh = _dot(w.astype(BF16), doms[h][rows], TN)
                dq, dk, dv = (dqh, dkh, dvh) if h == 0 else (dq + dqh, dk + dkh, dv + dvh)
                new = acc[h][rows, :] + tr[:, QBLK:]
                acc[h][rows, :] = new
                racc[h][rows, :] += tg[:, QBLK:]
                top = new if top is None else jnp.maximum(top, new)
            dqs[rows, :] += dq
            dks[pl.ds(off, QBLK), :] += dk
            dvs[pl.ds(off, QBLK), :] += dv
            return jnp.max(top)

        for ref in (dqs, a0, a1, r0, r1):
            ref[...] = jnp.zeros_like(ref)
        top = None
        for j in reversed(range(SB_NK)):
            top = block(qi * SB_NK + j, (lane + j * QBLK) < row, j * QBLK)
        lax.while_loop(lambda c: (c[0] >= 0) & (c[1] > -SB_EXIT), lambda c: (c[0] - 1, block(c[0], None)),
                       (qi * SB_NK - 1, top))
        dq_ref[...] = dqs[...]
        fin = pl.multiple_of(qi * SB_TQ, SB_TQ)
        dk_ref[...] = dks[pl.ds(fin, SB_TQ), :]
        dv_ref[...] = dvs[pl.ds(fin, SB_TQ), :]

    blk = lambda c0: pl.BlockSpec((SB_TQ, LANES), lambda p, i, c0=c0: (nq - 1 - i, c0 + p))
    return _call_with_exchange(
        body, side, lambda: (pl.program_id(0) == 0) & (pl.program_id(1) == 0),
        lambda: (pl.program_id(0) == 1) & (pl.program_id(1) == nq - 1), (qkv, qkv, qkv, o, dmix), name=name, grid=(2, nq),
        in_specs=[blk(0), pl.BlockSpec((s, LANES), lambda p, i: (0, 2 + p)), pl.BlockSpec((s, LANES), lambda p, i: (0, 4 + p)),
                  blk(0), blk(0)],
        out_specs=[blk(0), blk(0), blk(0)],
        out_shape=[jax.ShapeDtypeStruct((s, SB_W), F32)] * 3,
        scratch_shapes=[pltpu.VMEM((SB_TQ, LANES), F32)] * 5 + [pltpu.VMEM((s, LANES), F32), pltpu.VMEM((s, LANES), F32)],
        compiler_params=_cparams("arbitrary", "arbitrary"))


def _seg_consts():
    r = _iota((LANES, LANES), 0)
    c = _iota((LANES, LANES), 1)
    return jnp.where((r >> 6) == (c >> 6), 1.0, 0.0).astype(BF16)


def _rot_half(x, lane):
    half = HEAD_DIM // 2
    return jnp.where((lane & (HEAD_DIM - 1)) < half, pltpu.roll(x, LANES - half, 1), pltpu.roll(x, half, 1))


def _rope_tables(s):
    half = HEAD_DIM // 2
    inv_freq = ROPE_THETA ** (-jnp.arange(half, dtype=F32) * 2.0 / HEAD_DIM)
    ang = jnp.arange(s, dtype=F32)[:, None] * inv_freq[None, :]
    cos, sin = jnp.cos(ang), jnp.sin(ang)
    return jnp.tile(jnp.concatenate([cos, cos], axis=1), (1, 2)), jnp.tile(jnp.concatenate([-sin, sin], axis=1), (1, 2))


def _dil_prep(proj, gq, gk, cos, sin, *, name):
    s = proj.shape[0]
    t = _rows(s, TE)
    c0 = 3 * SB_W // LANES

    def body(q_ref, k_ref, gq_ref, gk_ref, cos_ref, sin_ref, qo_ref, ko_ref):
        seg = _seg_consts()
        lane = _iota((t, LANES), 1)
        cs, sn = cos_ref[...], sin_ref[...]
        for x_ref, g_ref, o_ref, mul in ((q_ref, gq_ref, qo_ref, HEAD_DIM ** -0.5), (k_ref, gk_ref, ko_ref, 1.0)):
            for j in range(2):
                xv = x_ref[:, j * LANES:(j + 1) * LANES]
                ms = _xdot(xv * xv, seg, 3) * (1.0 / HEAD_DIM)
                xn = xv * lax.rsqrt(ms + EPS) * g_ref[...]
                o_ref[:, j * LANES:(j + 1) * LANES] = (xn * cs + _rot_half(xn, lane) * sn) * mul

    return _pcall(
        body, name=name, grid=(s // t,),
        in_specs=[pl.BlockSpec((t, DIL_W), lambda i: (i, c0 // 2)), pl.BlockSpec((t, DIL_W), lambda i: (i, c0 // 2 + 1)),
                  _vec_spec(LANES), _vec_spec(LANES), _row_spec(t, LANES), _row_spec(t, LANES)],
        out_specs=[_row_spec(t, DIL_W), _row_spec(t, DIL_W)],
        out_shape=[jax.ShapeDtypeStruct((s, DIL_W), F32)] * 2, compiler_params=_cparams("parallel"))(proj, proj, gq, gk, cos, sin)


def _dil_prep_bwd(proj, gq, gk, cos, sin, dqs, dks, dvs, *, name):
    s = proj.shape[0]
    t = _rows(s, TE)
    c0 = 3 * SB_W // LANES

    def body(q_ref, k_ref, gq_ref, gk_ref, cos_ref, sin_ref, a0, a1, a2, b0, b1, b2, c0_ref, c1_ref, c2_ref,
             dq_ref, dk_ref, dv_ref, dgq_ref, dgk_ref):
        @pl.when(pl.program_id(0) == 0)
        def _():
            dgq_ref[...] = jnp.zeros_like(dgq_ref)
            dgk_ref[...] = jnp.zeros_like(dgk_ref)

        dv_ref[...] = c0_ref[...] + c1_ref[...] + c2_ref[...]
        seg = _seg_consts()
        lane = _iota((t, LANES), 1)
        cs, sn = cos_ref[...], sin_ref[...]
        for x_ref, g_ref, parts, o_ref, dg_ref, mul in ((q_ref, gq_ref, (a0, a1, a2), dq_ref, dgq_ref, HEAD_DIM ** -0.5),
                                                          (k_ref, gk_ref, (b0, b1, b2), dk_ref, dgk_ref, 1.0)):
            for j in range(2):
                sl = slice(j * LANES, (j + 1) * LANES)
                dout = (parts[0][:, sl] + parts[1][:, sl] + parts[2][:, sl]) * mul
                dxn = dout * cs + _rot_half(dout * sn, lane)
                xv = x_ref[:, sl]
                r = lax.rsqrt(_xdot(xv * xv, seg, 3) * (1.0 / HEAD_DIM) + EPS)
                xh = xv * r
                dg_ref[...] += jnp.sum(dxn * xh, axis=0, keepdims=True)
                dxh = dxn * g_ref[...]
                o_ref[:, sl] = r * (dxh - xh * (_xdot(dxh * xh, seg, 3) * (1.0 / HEAD_DIM)))

    rs = _row_spec(t, DIL_W)
    return _pcall(
        body, name=name, grid=(s // t,),
        in_specs=[pl.BlockSpec((t, DIL_W), lambda i: (i, c0 // 2)), pl.BlockSpec((t, DIL_W), lambda i: (i, c0 // 2 + 1)),
                  _vec_spec(LANES), _vec_spec(LANES), _row_spec(t, LANES), _row_spec(t, LANES)] + [rs] * 9,
        out_specs=[rs, rs, rs, _vec_spec(LANES), _vec_spec(LANES)],
        out_shape=[jax.ShapeDtypeStruct((s, DIL_W), F32)] * 3 + [jax.ShapeDtypeStruct((1, LANES), F32)] * 2,
        compiler_params=_cparams("arbitrary"))(proj, proj, gq, gk, cos, sin, *dqs, *dks, *dvs)


def _dil_masks(n):
    row = _iota((QBLK, 2 * LANES), 0)
    col = _iota((QBLK, 2 * LANES), 1)
    return ((col < LANES) & (col >= row) & (n > 0)) | ((col >= LANES) & (col - LANES <= row))


DIL_V0 = (3 * SB_W + 2 * DIL_W) // LANES
DIL_DO0 = SB_W // LANES


def _dil_tiles(s, r):
    span = QBLK * r
    nsub = max(1, DIL_TILE // span)
    while s % (nsub * span):
        nsub -= 1
    return span, nsub


def _dil_rows(j, rho, span, r):
    return pl.ds(j * span + rho, QBLK, stride=r) if r > 1 else pl.ds(j * span, QBLK)


def _dil_fwd(q, k, proj, r, *, name):
    s = q.shape[0]
    span, nsub = _dil_tiles(s, r)
    tr = nsub * span

    def body(q_ref, kc_ref, kp_ref, vc_ref, vp_ref, num_ref, den_ref, mx_ref):
        n = pl.program_id(1)
        lane = _iota((QBLK, LANES), 1)
        h0 = lane < HEAD_DIM
        ones = jnp.ones((2 * QBLK, LANES), BF16)
        for j in range(nsub):
            valid = _dil_masks(n if j == 0 else 1)
            for rho in range(r):
                rows = _dil_rows(j, rho, span, r)
                before = _dil_rows(max(j - 1, 0), rho, span, r)
                k_prev, v_prev = (kp_ref, vp_ref) if j == 0 else (kc_ref, vc_ref)
                qv = q_ref[rows, :].astype(BF16)
                kk = jnp.concatenate([k_prev[before, :], kc_ref[rows, :]], axis=0).astype(BF16)
                vv = jnp.concatenate([jnp.concatenate([v_prev[before, :], vc_ref[rows, :]], axis=0).astype(BF16), ones], axis=1)
                res = []
                for h in range(2):
                    qm = jnp.where(h0 if h == 0 else ~h0, qv, jnp.zeros_like(qv))
                    sc = jnp.where(valid, _dot(qm, kk, NT), NEG_BIG)
                    mx = jnp.max(sc, axis=1, keepdims=True)
                    nd = _dot(jnp.exp(sc - mx).astype(BF16), vv)
                    res.append((nd[:, :LANES], nd[:, LANES:], mx))
                num_ref[rows, :] = jnp.where(h0, res[0][0], res[1][0])
                den_ref[rows, :] = jnp.where(h0, res[0][1], res[1][1])
                mx_ref[rows, :] = jnp.where(h0, res[0][2], res[1][2])

    cur = lambda c0: pl.BlockSpec((tr, LANES), lambda p, n, c0=c0: (n, c0 + p))
    prev = lambda c0: pl.BlockSpec((span, LANES), lambda p, n, c0=c0: (jnp.maximum(n * nsub - 1, 0), c0 + p))
    return _pcall(
        body, name=name, grid=(2, s // tr), in_specs=[cur(0), cur(0), prev(0), cur(DIL_V0), prev(DIL_V0)],
        out_specs=[cur(0), cur(0), cur(0)], out_shape=[jax.ShapeDtypeStruct((s, DIL_W), F32)] * 3,
        compiler_params=_cparams("parallel", "arbitrary"))(q, k, k, proj, proj)


def _dil_bwd(q, k, proj, dmix, mall, zall, delta, r, *, name):
    s = q.shape[0]
    span, nsub = _dil_tiles(s, r)
    tr = nsub * span
    nbig = s // tr

    def body(q_ref, kc_ref, kp_ref, vc_ref, vp_ref, do_ref, m_ref, z_ref, dl_ref, dq_ref, dk_ref, dv_ref, pk, pv):
        n = pl.program_id(1)
        lane = _iota((QBLK, LANES), 1)
        h0 = lane < HEAD_DIM

        @pl.when(n == 0)
        def _():
            pk[...] = jnp.zeros_like(pk)
            pv[...] = jnp.zeros_like(pv)

        @pl.when(n < nbig)
        def _():
            dk_ref[...] = pk[...]
            dv_ref[...] = pv[...]
            for j in range(nsub):
                valid = _dil_masks(n if j == 0 else 1)
                for rho in range(r):
                    rows = _dil_rows(j, rho, span, r)
                    before = _dil_rows(max(j - 1, 0), rho, span, r)
                    k_prev, v_prev = (kp_ref, vp_ref) if j == 0 else (kc_ref, vc_ref)
                    qv = q_ref[rows, :].astype(BF16)
                    dob = do_ref[rows, :].astype(BF16)
                    zero = jnp.zeros_like(qv)
                    kk = jnp.concatenate([k_prev[before, :], kc_ref[rows, :]], axis=0).astype(BF16)
                    vv = jnp.concatenate([v_prev[before, :], vc_ref[rows, :]], axis=0).astype(BF16)
                    mall_v, z_v, dl_v = m_ref[rows, :], z_ref[rows, :], dl_ref[rows, :]
                    dq, dk, dv = None, None, None
                    for h in range(2):
                        hm = h0 if h == 0 else ~h0
                        qm = jnp.where(hm, qv, zero)
                        dom = jnp.where(hm, dob, zero)
                        c = h * HEAD_DIM
                        sc = jnp.where(valid, _dot(qm, kk, NT), NEG_BIG)
                        pr = jnp.exp(sc - mall_v[:, c:c + 1]) * (1.0 / z_v[:, c:c + 1])
                        ds = (pr * (_dot(dom, vv, NT) - dl_v[:, c:c + 1])).astype(BF16)
                        parts = (_dot(ds, jnp.where(jnp.concatenate([hm, hm], axis=0), kk, jnp.zeros_like(kk))),
                                 _dot(ds, qm, TN), _dot(pr.astype(BF16), dom, TN))
                        dq, dk, dv = parts if h == 0 else (dq + parts[0], dk + parts[1], dv + parts[2])
                    dq_ref[rows, :] = dq
                    pk[rows, :] = dk[QBLK:]
                    pv[rows, :] = dv[QBLK:]
                    if j == 0:
                        last_span = _dil_rows(nsub - 1, rho, span, r)
                        dk_ref[last_span, :] += dk[:QBLK]
                        dv_ref[last_span, :] += dv[:QBLK]
                    else:
                        pk[before, :] += dk[:QBLK]
                        pv[before, :] += dv[:QBLK]

        @pl.when(n == nbig)
        def _():
            dk_ref[...] = pk[...]
            dv_ref[...] = pv[...]

    last = nbig - 1
    cur = lambda c0: pl.BlockSpec((tr, LANES), lambda p, n, c0=c0: (jnp.minimum(n, last), c0 + p))
    prev = lambda c0: pl.BlockSpec((span, LANES), lambda p, n, c0=c0: (jnp.maximum(jnp.minimum(n, last) * nsub - 1, 0), c0 + p))
    late = pl.BlockSpec((tr, LANES), lambda p, n: (jnp.maximum(n - 1, 0), p))
    return _pcall(
        body, name=name, grid=(2, nbig + 1),
        in_specs=[cur(0), cur(0), prev(0), cur(DIL_V0), prev(DIL_V0), cur(DIL_DO0), cur(0), cur(0), cur(0)],
        out_specs=[cur(0), late, late], out_shape=[jax.ShapeDtypeStruct((s, DIL_W), F32)] * 3,
        scratch_shapes=[pltpu.VMEM((tr, LANES), F32)] * 2,
        compiler_params=_cparams("parallel", "arbitrary"))(q, k, k, proj, proj, dmix, mall, zall, delta)


HG_SHIFT = HG_BLK.bit_length() - 1
HG_Q0, HG_F0, HG_I0 = (3 * SB_W + 3 * DIL_W) // HG_D, (3 * SB_W + 3 * DIL_W + HG_W) // HG_D, (3 * SB_W + 3 * DIL_W + 2 * HG_W) // HG_D


def _hg_scan(x):
    t = x.shape[0]
    half = HG_BLK // 2
    rb = _iota((t, LANES), 0) & (HG_BLK - 1)
    rh = rb & (half - 1)
    p = x
    for s in (1, 2, 4):
        p = p + jnp.where(rh >= s, pltpu.roll(p, s, 0), 0.0)
    h = jnp.where(rh == half - 1, p, 0.0)
    for s in (1, 2, 4):
        h = h + jnp.where(rh + s < half, pltpu.roll(h, t - s, 0), 0.0)
    first = rb < half
    pref = jnp.where(first, p, p + pltpu.roll(h, half, 0))
    total = h + jnp.where(first, pltpu.roll(h, t - half, 0), pltpu.roll(h, half, 0))
    return p, h, pref, total, first


def _hg_same(t):
    i = jnp.arange(t) >> HG_SHIFT
    return (i[:, None] == i[None, :]).astype(F32)


def _hg_own(t):
    i = jnp.arange(t) >> HG_SHIFT
    j = jnp.arange(t // HG_BLK * HG_D) // HG_D
    return (i[:, None] == j[None, :]).astype(BF16)


def _hg_diag(x, nb):
    return jnp.concatenate([x[b * HG_BLK:(b + 1) * HG_BLK, b * HG_D:(b + 1) * HG_D] for b in range(nb)], axis=0)


def _hg_inputs(qh, z, v, la, lc, t):
    lsg = jnp.minimum(z, 0.0) - jnp.log(1.0 + jnp.exp(-jnp.abs(z)))
    b = lc + lsg
    lf = jnp.maximum(la, b) + jnp.log(1.0 + jnp.exp(-jnp.abs(la - b)))
    f = jnp.exp(lf)
    sq = _sigmoid(qh)
    p, h, g, gl, first = _hg_scan(lf)
    k = 1.0 - f
    qs = qh * sq
    eq = jnp.where(first, 0.0, jnp.exp(jnp.minimum(p, 0.0)))
    ek = jnp.where(first, jnp.exp(jnp.minimum(h - p, 0.0)), 0.0)
    return dict(lf=lf, b=b, f=f, k=k, sq=sq, qs=qs, g=g, eg=jnp.exp(g), egl=jnp.exp(gl - g), dec=jnp.exp(gl),
                eq=eq, ek=ek, qx=(qs * eq).astype(BF16), kx=(k * ek).astype(BF16))


def _hgrn_fwd(proj, la, lc, *, name, side=None):
    s = proj.shape[0]
    t = _rows(s, HG_TILE)
    nt, nb = s // t, t // HG_BLK

    def body(q_ref, f_ref, i_ref, la_ref, lc_ref, same_ref, own_ref, o_ref, st_ref, state):
        @pl.when(pl.program_id(1) == 0)
        def _():
            state[...] = jnp.zeros_like(state)

        v = i_ref[...]
        a = _hg_inputs(q_ref[...], f_ref[...], v, la_ref[...], lc_ref[...], t)
        qs, k = a["qs"], a["k"]
        vb = v.astype(BF16)
        rb = _iota((t, LANES), 0) & (HG_BLK // 2 - 1)
        o = jnp.sum(qs * k, axis=1, keepdims=True) * v
        e = None
        for d in range(1, HG_BLK // 2):
            m = rb >= d
            fr = a["f"] if d == 1 else pltpu.roll(a["f"], d - 1, 0)
            e = fr if e is None else e * fr
            cd = jnp.sum(qs * pltpu.roll(k, d, 0) * e, axis=1, keepdims=True)
            o = o + jnp.where(m, cd, 0.0) * pltpu.roll(v, d, 0)
        cross = _dot(a["qx"], a["kx"], NT) * same_ref[...]
        o = o + _dot(cross.astype(BF16), vb)
        qt = (qs * a["eg"]).astype(BF16)
        kt = (k * a["egl"]).astype(BF16)
        upd = _dot(vb, jnp.tile(kt, (1, nb)) * own_ref[...], TN)
        st = state[...]
        for blk in range(nb):
            st_ref[blk * HG_D:(blk + 1) * HG_D, :] = st.astype(BF16)
            st = a["dec"][blk * HG_BLK:blk * HG_BLK + 1] * st + upd[:, blk * HG_D:(blk + 1) * HG_D]
        state[...] = st
        o_ref[...] = o + _hg_diag(_dot(qt, st_ref[...], NT), nb)

    col = lambda c0: pl.BlockSpec((t, HG_D), lambda hd, i, c0=c0: (i, c0 + hd))
    vec = pl.BlockSpec((1, HG_D), lambda hd, i: (0, hd))
    return _call_with_exchange(
        body, side, lambda: (pl.program_id(0) == 0) & (pl.program_id(1) == 0),
        lambda: (pl.program_id(0) == 3) & (pl.program_id(1) == nt - 1),
        (proj, proj, proj, la, lc, _hg_same(t), _hg_own(t)), name=name, grid=(4, nt),
        in_specs=[col(HG_Q0), col(HG_F0), col(HG_I0), vec, vec, pl.BlockSpec((t, t), lambda hd, i: (0, 0)),
                  pl.BlockSpec((t, nb * HG_D), lambda hd, i: (0, 0))],
        out_specs=[col(0), pl.BlockSpec((None, nb * HG_D, HG_D), lambda hd, i: (hd, i, 0))],
        out_shape=[jax.ShapeDtypeStruct((s, HG_W), F32), jax.ShapeDtypeStruct((4, s // HG_BLK * HG_D, HG_D), BF16)],
        scratch_shapes=[pltpu.VMEM((HG_D, HG_D), F32)],
        compiler_params=_cparams("arbitrary", "arbitrary"))


def _hgrn_bwd(proj, la, lc, st, doh, *, name, side=None):
    s = proj.shape[0]
    t = _rows(s, HG_TILE)
    nt, nb = s // t, t // HG_BLK

    def body(q_ref, f_ref, i_ref, la_ref, lc_ref, st_ref, do_ref, same_ref, own_ref, dq_ref, df_ref, di_ref, dla_ref, dlc_ref,
             dstate, dsb):
        @pl.when(pl.program_id(1) == 0)
        def _():
            dstate[...] = jnp.zeros_like(dstate)
            dla_ref[...] = jnp.zeros_like(dla_ref)
            dlc_ref[...] = jnp.zeros_like(dlc_ref)

        qh, z, v, do = q_ref[...], f_ref[...], i_ref[...], do_ref[...]
        la = la_ref[...]
        a = _hg_inputs(qh, z, v, la, lc_ref[...], t)
        qs, k, g = a["qs"], a["k"], a["g"]
        vb = v.astype(BF16)
        dob = do.astype(BF16)
        rb = _iota((t, LANES), 0) & (HG_BLK // 2 - 1)
        dc0 = jnp.sum(do * v, axis=1, keepdims=True)
        dq = dc0 * k
        dk = dc0 * qs
        dv = jnp.sum(qs * k, axis=1, keepdims=True) * do
        e = None
        for d in range(1, HG_BLK // 2):
            m = rb >= d
            fr = a["f"] if d == 1 else pltpu.roll(a["f"], d - 1, 0)
            e = fr if e is None else e * fr
            ks = pltpu.roll(k, d, 0)
            qe = qs * e
            cd = jnp.where(m, jnp.sum(qe * ks, axis=1, keepdims=True), 0.0)
            dcd = jnp.where(m, jnp.sum(do * pltpu.roll(v, d, 0), axis=1, keepdims=True), 0.0)
            dq = dq + dcd * ks * e
            dk = dk + pltpu.roll(dcd * qe, t - d, 0)
            dv = dv + pltpu.roll(cd * do, t - d, 0)
        same = same_ref[...]
        cross = (_dot(a["qx"], a["kx"], NT) * same).astype(BF16)
        dcross = (_dot(dob, vb, NT) * same).astype(BF16)
        dq = dq + _dot(dcross, a["kx"]) * a["eq"]
        dk = dk + _dot(dcross, a["qx"], TN) * a["ek"]
        dv = dv + _dot(cross, dob, TN)
        qt = (qs * a["eg"]).astype(BF16)
        kt = (k * a["egl"]).astype(BF16)
        own = own_ref[...]
        upd = _dot(dob, jnp.tile(qt, (1, nb)) * own, TN)
        ds = dstate[...]
        dgs = [None] * nb
        for blk in reversed(range(nb)):
            rows = slice(blk * HG_D, (blk + 1) * HG_D)
            dec = a["dec"][blk * HG_BLK:blk * HG_BLK + 1]
            dsb[rows, :] = ds.astype(BF16)
            dgs[blk] = jnp.broadcast_to(jnp.sum(ds * st_ref[rows, :].astype(F32), axis=0, keepdims=True) * dec, (HG_BLK, HG_D))
            ds = dec * ds + upd[:, rows]
        dstate[...] = ds
        dki = _dot(jnp.tile(vb, (1, nb)) * own, dsb[...]) * a["egl"]
        dq = dq + _dot(jnp.tile(dob, (1, nb)) * own, st_ref[...]) * a["eg"]
        dk = dk + dki
        dv = dv + _hg_diag(_dot(kt, dsb[...], NT), nb)
        x = qs * dq - k * dk
        _, _, xpre, xtot, _ = _hg_scan(x)
        _, _, _, ktot, _ = _hg_scan(k * dki)
        dlf = (xtot - xpre + x) + ktot + jnp.concatenate(dgs, axis=0) - a["f"] * dk
        wb = jnp.exp(a["b"] - a["lf"])
        wa = jnp.exp(la - a["lf"])
        sq = a["sq"]
        dq_ref[...] = dq * (sq * (1.0 + qh * (1.0 - sq)))
        df_ref[...] = dlf * wb * (1.0 - _sigmoid(z))
        di_ref[...] = dv
        dla_ref[...] += jnp.sum(dlf * wa, axis=0, keepdims=True)
        dlc_ref[...] += jnp.sum(dlf * wb, axis=0, keepdims=True)

    col = lambda c0: pl.BlockSpec((t, HG_D), lambda hd, i, c0=c0: (nt - 1 - i, c0 + hd))
    vec = pl.BlockSpec((1, HG_D), lambda hd, i: (0, hd))
    return _call_with_exchange(
        body, side, lambda: (pl.program_id(0) == 0) & (pl.program_id(1) == 0),
        lambda: (pl.program_id(0) == 3) & (pl.program_id(1) == nt - 1),
        (proj, proj, proj, la, lc, st, doh, _hg_same(t), _hg_own(t)), name=name, grid=(4, nt),
        in_specs=[col(HG_Q0), col(HG_F0), col(HG_I0), vec, vec,
                  pl.BlockSpec((None, nb * HG_D, HG_D), lambda hd, i: (hd, nt - 1 - i, 0)), col(0),
                  pl.BlockSpec((t, t), lambda hd, i: (0, 0)), pl.BlockSpec((t, nb * HG_D), lambda hd, i: (0, 0))],
        out_specs=[col(0), col(0), col(0), vec, vec],
        out_shape=[jax.ShapeDtypeStruct((s, HG_W), F32)] * 3 + [jax.ShapeDtypeStruct((1, HG_W), F32)] * 2,
        scratch_shapes=[pltpu.VMEM((HG_D, HG_D), F32), pltpu.VMEM((nb * HG_D, HG_D), BF16)],
        compiler_params=_cparams("arbitrary", "arbitrary"))


GH0 = (IN_W - HG_W) // HG_W


def _mix_out(o_a, nums, dens, mxs, oh, proj, hg, *, name):
    s = o_a.shape[0]
    t = _rows(s, TE)

    def body(oa_ref, n0, n1, n2, d0, d1, d2, m0, m1, m2, oh_ref, gh_ref, hg_ref, y_ref, od_ref, mall_ref, z_ref):
        y_ref[:, :SB_W] = oa_ref[...].astype(BF16)
        m = jnp.maximum(jnp.maximum(m0[...], m1[...]), m2[...])
        num = jnp.zeros((t, DIL_W), F32)
        z = jnp.zeros((t, DIL_W), F32)
        for n_ref, d_ref, m_ref in ((n0, d0, m0), (n1, d1, m1), (n2, d2, m2)):
            sc = jnp.exp(m_ref[...] - m)
            num = num + n_ref[...] * sc
            z = z + d_ref[...] * sc
        od = num / z
        od_ref[...] = od
        mall_ref[...] = m
        z_ref[...] = z
        y_ref[:, SB_W:SB_W + DIL_W] = od.astype(BF16)
        for h in range(4):
            sl = slice(h * HG_D, (h + 1) * HG_D)
            ov = oh_ref[:, sl]
            g = gh_ref[:, sl]
            r = lax.rsqrt(jnp.mean(ov * ov, axis=1, keepdims=True) + EPS)
            y_ref[:, SB_W + DIL_W + h * HG_D:SB_W + DIL_W + (h + 1) * HG_D] = (ov * r * hg_ref[...] * (g * _sigmoid(g))).astype(BF16)

    rd = _row_spec(t, DIL_W)
    return _pcall(
        body, name=name, grid=(s // t,),
        in_specs=[rd] * 10 + [_row_spec(t, HG_W), _row_spec(t, HG_W, GH0), _vec_spec(HG_D)],
        out_specs=[_row_spec(t, MIX_W), rd, rd, rd],
        out_shape=[jax.ShapeDtypeStruct((s, MIX_W), BF16)] + [jax.ShapeDtypeStruct((s, DIL_W), F32)] * 3,
        compiler_params=_cparams("parallel"))(o_a, *nums, *dens, *mxs, oh, proj, hg)


def _mix_out_bwd(dmix, oh, proj, hg, od, *, name):
    s = oh.shape[0]
    t = _rows(s, TE)

    def body(dm_ref, oh_ref, gh_ref, hg_ref, od_ref, doh_ref, dgh_ref, dl_ref, dhg_ref):
        @pl.when(pl.program_id(0) == 0)
        def _():
            dhg_ref[...] = jnp.zeros_like(dhg_ref)

        seg = _seg_consts()
        for j in range(2):
            sl = slice(j * LANES, (j + 1) * LANES)
            dl_ref[:, sl] = _xdot(dm_ref[:, SB_W + j * LANES:SB_W + (j + 1) * LANES] * od_ref[:, sl], seg, 3)
        hgv = hg_ref[...]
        for h in range(4):
            sl = slice(h * HG_D, (h + 1) * HG_D)
            dy = dm_ref[:, SB_W + DIL_W + h * HG_D:SB_W + DIL_W + (h + 1) * HG_D]
            ov = oh_ref[:, sl]
            g = gh_ref[:, sl]
            sg = _sigmoid(g)
            silu = g * sg
            r = lax.rsqrt(jnp.mean(ov * ov, axis=1, keepdims=True) + EPS)
            nrm = ov * r
            dhg_ref[...] += jnp.sum(dy * nrm * silu, axis=0, keepdims=True)
            dgh_ref[:, sl] = dy * nrm * hgv * (sg * (1.0 + g * (1.0 - sg)))
            dn = dy * hgv * silu
            doh_ref[:, sl] = r * (dn - nrm * jnp.mean(dn * nrm, axis=1, keepdims=True))

    rh = _row_spec(t, HG_W)
    return _pcall(
        body, name=name, grid=(s // t,),
        in_specs=[_row_spec(t, MIX_W), rh, _row_spec(t, HG_W, GH0), _vec_spec(HG_D), _row_spec(t, DIL_W)],
        out_specs=[rh, rh, _row_spec(t, DIL_W), _vec_spec(HG_D)],
        out_shape=[jax.ShapeDtypeStruct((s, HG_W), F32)] * 2 + [jax.ShapeDtypeStruct((s, DIL_W), F32), jax.ShapeDtypeStruct((1, HG_D), F32)],
        compiler_params=_cparams("arbitrary"))(dmix, oh, proj, hg, od)


def _lb_terms(l):
    l0, l1 = l[0:1], l[1:2]
    m = jnp.maximum(l0, l1)
    e0, e1 = jnp.exp(l0 - m), jnp.exp(l1 - m)
    s0, s1 = e0 / (e0 + e1), e1 / (e0 + e1)
    args = (s0 - s0, (s0 + s1) - s0)
    lbs = tuple(jnp.minimum(jnp.maximum(a, 0.0), 1.0 - EPS) for a in args)
    return s0, s1, args, lbs


def _lb_prep(logits, *, name):
    def body(l_ref, lb_ref, la_ref, lc_ref):
        _, _, _, lbs = _lb_terms(l_ref[...])
        lb = jnp.concatenate(lbs, axis=0)
        lb_ref[...] = lb
        la_ref[...] = jnp.log(jnp.maximum(lb, LB_FLOOR))
        lc_ref[...] = jnp.log1p(-lb)

    return _pcall(body, name=name, out_shape=[jax.ShapeDtypeStruct(logits.shape, F32)] * 3)(logits)


def _lb_bwd(logits, dla, dlc, *, name):
    def half(hi, eq):
        return jnp.where(hi, 1.0, jnp.where(eq, 0.5, 0.0))

    def body(l_ref, dla_ref, dlc_ref, o_ref):
        s0, s1, args, lbs = _lb_terms(l_ref[...])
        da = []
        for i in range(2):
            a, lb = args[i], lbs[i]
            dlb = dla_ref[i:i + 1] * half(lb > LB_FLOOR, lb == LB_FLOOR) / jnp.maximum(lb, LB_FLOOR) - dlc_ref[i:i + 1] / (1.0 - lb)
            t = jnp.maximum(a, 0.0)
            da.append(dlb * half(a > 0.0, a == 0.0) * half(t < 1.0 - EPS, t == 1.0 - EPS))
        ds0 = (da[0] + da[1]) - (da[0] + da[1])
        ds1 = da[1]
        dot = s0 * ds0 + s1 * ds1
        o_ref[...] = jnp.concatenate([s0 * (ds0 - dot), s1 * (ds1 - dot)], axis=0)

    return _pcall(body, name=name, out_shape=jax.ShapeDtypeStruct(logits.shape, F32))(logits, dla, dlc)


def _mod_fwd(c8, w, b, *, name):
    _, d, n = w.shape
    tn = _tile(n, 768)

    def body(c_ref, w_ref, b_ref, o_ref):
        cv = c_ref[...]
        o_ref[...] = _dot((cv * _sigmoid(cv)).astype(BF16), w_ref[...].astype(BF16)) + b_ref[...]

    return _pcall(
        body, name=name, grid=(2, n // tn),
        in_specs=[pl.BlockSpec((8, d), lambda l, j: (0, 0)), pl.BlockSpec((None, d, tn), lambda l, j: (l, 0, j)),
                  pl.BlockSpec((None, 1, tn), lambda l, j: (l, 0, j))],
        out_specs=pl.BlockSpec((None, 8, tn), lambda l, j: (l, 0, j)),
        out_shape=jax.ShapeDtypeStruct((2, 8, n), F32), compiler_params=_cparams("parallel", "parallel"))(c8, w, b)


def _mod_bwd(ct, dm, *, name):
    d = ct.shape[0]
    n = dm.shape[2]
    tn = _tile(n, 768)

    def body(c_ref, dm_ref, o_ref):
        cv = c_ref[...]
        sc = cv * _sigmoid(cv)
        dv = dm_ref[...]
        acc = sc[:, 0:1] * dv[0:1, :]
        for b in range(1, 8):
            acc = acc + sc[:, b:b + 1] * dv[b:b + 1, :]
        o_ref[...] = acc

    return _pcall(
        body, name=name, grid=(2, n // tn),
        in_specs=[pl.BlockSpec((d, 8), lambda l, j: (0, 0)), pl.BlockSpec((None, 8, tn), lambda l, j: (l, 0, j))],
        out_specs=pl.BlockSpec((None, d, tn), lambda l, j: (l, 0, j)),
        out_shape=jax.ShapeDtypeStruct((2, d, n), F32), compiler_params=_cparams("parallel", "parallel"))(ct, dm)


_PEERS = {
    "chips": ((1, 0, 0), (0, 1, 0), (1, 1, 0)),
    "all": tuple((a, b, c) for a in (0, 1) for b in (0, 1) for c in (0, 1) if a + b + c),
    "sib": ((0, 0, 1),),
}
_SLOTS = {"chips": 4, "all": 8, "sib": 2}


def _slot(kind, x, y, c):
    return {"chips": 2 * x + y, "all": 4 * x + 2 * y + c, "sib": c}[kind]


class _Xchg:
    def __init__(self, arrs, kind, scatter):
        self.arrs, self.kind, self.scatter = list(arrs), kind, scatter
        self.n = len(self.arrs)
        self.peers = _PEERS[kind]
        self.chunks = [self._pieces(a) for a in self.arrs]
        npeer = len(self.peers)
        self.base = [sum(len(c) for c in self.chunks[:a]) * npeer for a in range(self.n)]
        total = sum(len(c) for c in self.chunks) * npeer
        self.specs = [pl.BlockSpec(memory_space=pl.ANY)] * self.n
        self.out_shape = [jax.ShapeDtypeStruct(a.shape if scatter else (_SLOTS[kind],) + a.shape, a.dtype) for a in self.arrs]
        self.scratch = [pltpu.SemaphoreType.DMA((total,)), pltpu.SemaphoreType.DMA((total,)), pltpu.SemaphoreType.DMA((self.n,))]

    def _pieces(self, a):
        shape = a.shape[1:] if self.scatter else a.shape
        if len(shape) == 2:
            for k in (XCHG_CHUNKS, XCHG_CHUNKS // 2, XCHG_CHUNKS // 4):
                if k > 1 and shape[0] % (16 * k) == 0 and shape[0] * shape[1] * a.dtype.itemsize >= k * XCHG_MIN_BYTES:
                    return [(i * (shape[0] // k), shape[0] // k) for i in range(k)]
        return [None]

    def copies(self, ins, outs, send, recv, loc):
        kind, scatter = self.kind, self.scatter
        x, y, c = lax.axis_index("x"), lax.axis_index("y"), lax.axis_index("c")
        me = _slot(kind, x, y, c)
        out = []
        for a in range(self.n):
            out.append(pltpu.make_async_copy(ins[a].at[me] if scatter else ins[a], outs[a].at[me], loc.at[a]))
            for j, (dx, dy, dc) in enumerate(self.peers):
                px, py, pc = (1 - x if dx else x), (1 - y if dy else y), (1 - c if dc else c)
                src = ins[a].at[_slot(kind, px, py, pc)] if scatter else ins[a]
                for i, piece in enumerate(self.chunks[a]):
                    rows = slice(None) if piece is None else pl.ds(piece[0], piece[1])
                    sem = self.base[a] + j * len(self.chunks[a]) + i
                    out.append(pltpu.make_async_remote_copy(
                        src_ref=src if piece is None else src.at[rows], dst_ref=outs[a].at[me] if piece is None else outs[a].at[me, rows],
                        send_sem=send.at[sem], recv_sem=recv.at[sem], device_id=(px, py, pc), device_id_type=MESH_ID))
        return out


def _exchange(arrs, kind, scatter, *, name):
    xc = _Xchg(arrs, kind, scatter)

    def body(*refs):
        copies = xc.copies(refs[:xc.n], refs[xc.n:2 * xc.n], *refs[2 * xc.n:])
        for cp in copies:
            cp.start()
        for cp in copies:
            cp.wait()

    return _pcall(body, name=name, in_specs=xc.specs, out_specs=xc.specs, out_shape=xc.out_shape, scratch_shapes=xc.scratch)(*arrs)


def _with_exchange(body, n_in, n_out, n_scr, xc, first, last):
    def wrapped(*refs):
        ins, side_in = refs[:n_in], refs[n_in:n_in + xc.n]
        o0 = n_in + xc.n
        outs, side_out = refs[o0:o0 + n_out], refs[o0 + n_out:o0 + n_out + xc.n]
        s0 = o0 + n_out + xc.n
        scr, sems = refs[s0:s0 + n_scr], refs[s0 + n_scr:]

        @pl.when(first())
        def _():
            for cp in xc.copies(side_in, side_out, *sems):
                cp.start()

        body(*ins, *outs, *scr)

        @pl.when(last())
        def _():
            for cp in xc.copies(side_in, side_out, *sems):
                cp.wait()

    return wrapped


def _call_with_exchange(body, xc, first, last, args, *, in_specs, out_specs, out_shape, scratch_shapes=(), **kw):
    if xc is None:
        return _pcall(body, in_specs=in_specs, out_specs=out_specs, out_shape=out_shape, scratch_shapes=scratch_shapes, **kw)(*args), None
    wrapped = _with_exchange(body, len(in_specs), len(out_specs), len(scratch_shapes), xc, first, last)
    res = _pcall(wrapped, in_specs=list(in_specs) + xc.specs, out_specs=list(out_specs) + xc.specs,
                 out_shape=list(out_shape) + xc.out_shape, scratch_shapes=list(scratch_shapes) + xc.scratch, **kw)(*args, *xc.arrs)
    return res[:len(out_specs)], res[len(out_specs):]


def _sum_slots(a, *, name, out_dtype=F32):
    ns, r, c = a.shape
    t = _tile(r, max(16, (1 << 18) // c // 16 * 16), 16)

    def body(a_ref, o_ref):
        acc = a_ref[0].astype(F32)
        for i in range(1, ns):
            acc = acc + a_ref[i].astype(F32)
        o_ref[...] = acc.astype(o_ref.dtype)

    return _pcall(body, name=name, grid=(r // t,), in_specs=[pl.BlockSpec((ns, t, c), lambda i: (0, i, 0))],
                  out_specs=pl.BlockSpec((t, c), lambda i: (i, 0)), out_shape=jax.ShapeDtypeStruct((r, c), out_dtype),
                  compiler_params=_cparams("parallel"))(a)


def _adamw(w, gparts, m, v, *, name):
    r, c = w.shape
    t = _tile(r, max(16, (1 << 17) // c // 16 * 16), 16)
    ng = len(gparts)

    def body(*refs):
        w_ref, m_ref, v_ref = refs[0], refs[1 + ng], refs[2 + ng]
        g_ref, d_ref, nm_ref, nv_ref = refs[3 + ng:]
        g = refs[1][...].astype(F32)
        for i in range(1, ng):
            g = g + refs[1 + i][...].astype(F32)
        mn = ADAM_B1 * m_ref[...] + (1.0 - ADAM_B1) * g
        vn = ADAM_B2 * v_ref[...] + (1.0 - ADAM_B2) * (g * g)
        m_hat = mn / (1.0 - ADAM_B1 ** ADAM_STEP)
        v_hat = vn / (1.0 - ADAM_B2 ** ADAM_STEP)
        g_ref[...] = g
        d_ref[...] = -ADAM_LR * (m_hat / (jnp.sqrt(v_hat) + ADAM_EPS) + ADAM_WD * w_ref[...])
        nm_ref[...] = mn
        nv_ref[...] = vn

    spec = pl.BlockSpec((t, c), lambda i: (i, 0))
    return _pcall(body, name=name, grid=(r // t,), in_specs=[spec] * (3 + ng), out_specs=[spec] * 4,
                  out_shape=[jax.ShapeDtypeStruct((r, c), F32)] * 4, compiler_params=_cparams("parallel"))(w, *gparts, m, v)


def _adamw_layers(w, halves, m, v, *, name):
    r, c = w.shape
    nl = len(halves)
    rl = r // nl
    t = _tile(rl, max(16, (1 << 17) // c // 16 * 16), 16)
    nbl = rl // t

    def body(*refs):
        w_ref, m_ref, v_ref = refs[0], refs[1 + 2 * nl], refs[2 + 2 * nl]
        g_ref, d_ref, nm_ref, nv_ref = refs[3 + 2 * nl:]
        g = None
        for l in range(nl):
            gl = refs[1 + 2 * l][...].astype(F32) + refs[2 + 2 * l][...].astype(F32)
            g = gl if g is None else jnp.where(pl.program_id(0) >= l * nbl, gl, g)
        mn = ADAM_B1 * m_ref[...] + (1.0 - ADAM_B1) * g
        vn = ADAM_B2 * v_ref[...] + (1.0 - ADAM_B2) * (g * g)
        m_hat = mn / (1.0 - ADAM_B1 ** ADAM_STEP)
        v_hat = vn / (1.0 - ADAM_B2 ** ADAM_STEP)
        g_ref[...] = g
        d_ref[...] = -ADAM_LR * (m_hat / (jnp.sqrt(v_hat) + ADAM_EPS) + ADAM_WD * w_ref[...])
        nm_ref[...] = mn
        nv_ref[...] = vn

    spec = pl.BlockSpec((t, c), lambda i: (i, 0))
    part = lambda l, core: pl.BlockSpec((None, t, c), lambda i, l=l, core=core: (core, jnp.clip(i - l * nbl, 0, nbl - 1), 0))
    gspecs = [part(l, core) for l in range(nl) for core in range(2)]
    gargs = [halves[l] for l in range(nl) for _ in range(2)]
    return _pcall(body, name=name, grid=(r // t,), in_specs=[spec] + gspecs + [spec, spec], out_specs=[spec] * 4,
                  out_shape=[jax.ShapeDtypeStruct((r, c), F32)] * 4, compiler_params=_cparams("parallel"))(w, *gargs, m, v)


FFN_TM = 512
FFN_CHUNK = 1408


def _resident(shape):
    return pl.BlockSpec(shape, lambda i: (0,) * len(shape), pipeline_mode=pl.Buffered(1))


def _ffn_up(x, sc, sh, wgu, *, name, side=None):
    s, d = x.shape
    f = wgu.shape[1] // 2
    t, fc = _rows(s, FFN_TM), _tile(f, FFN_CHUNK)

    def body(x_ref, sc_ref, sh_ref, w_ref, h_ref, uv_ref, a_ref):
        xv = x_ref[...]
        r = lax.rsqrt(jnp.mean(xv * xv, axis=1, keepdims=True) + EPS)
        hb = ((xv * r) * (1.0 + sc_ref[...]) + sh_ref[...]).astype(BF16)
        h_ref[...] = hb
        for j in range(f // fc):
            u = _dot(hb, w_ref[:, j * fc:(j + 1) * fc])
            v = _dot(hb, w_ref[:, f + j * fc:f + (j + 1) * fc])
            sg = _sigmoid(u)
            silu = u * sg
            uv_ref[:, j * fc:(j + 1) * fc] = (v * (sg * (1.0 + u * (1.0 - sg)))).astype(BF16)
            uv_ref[:, f + j * fc:f + (j + 1) * fc] = silu.astype(BF16)
            a_ref[:, j * fc:(j + 1) * fc] = (silu * v).astype(BF16)

    return _call_with_exchange(
        body, side, lambda: pl.program_id(0) == 0, lambda: pl.program_id(0) == s // t - 1, (x, sc, sh, wgu),
        name=name, grid=(s // t,), in_specs=[_row_spec(t, d), _vec_spec(d), _vec_spec(d), _resident(wgu.shape)],
        out_specs=[_row_spec(t, d), _row_spec(t, 2 * f), _row_spec(t, f)],
        out_shape=[jax.ShapeDtypeStruct((s, d), BF16), jax.ShapeDtypeStruct((s, 2 * f), BF16), jax.ShapeDtypeStruct((s, f), BF16)],
        compiler_params=_cparams("parallel" if side is None else "arbitrary"))


def _norm_mm(x, sc, sh, w, nb, *, name):
    s, d = x.shape
    n = w.shape[1]
    t, nc = _rows(s, FFN_TM), _tile(n, 1792)
    assert nb <= nc

    def body(x_ref, sc_ref, sh_ref, w_ref, h_ref, o_ref, ob_ref):
        xv = x_ref[...]
        r = lax.rsqrt(jnp.mean(xv * xv, axis=1, keepdims=True) + EPS)
        hb = ((xv * r) * (1.0 + sc_ref[...]) + sh_ref[...]).astype(BF16)
        h_ref[...] = hb
        for j in range(n // nc):
            part = _dot(hb, w_ref[:, j * nc:(j + 1) * nc])
            o_ref[:, j * nc:(j + 1) * nc] = part
            if j == 0:
                ob_ref[...] = part[:, :nb].astype(BF16)

    return _pcall(
        body, name=name, grid=(s // t,), in_specs=[_row_spec(t, d), _vec_spec(d), _vec_spec(d), _resident(w.shape)],
        out_specs=[_row_spec(t, d), _row_spec(t, n), _row_spec(t, nb)],
        out_shape=[jax.ShapeDtypeStruct((s, d), BF16), jax.ShapeDtypeStruct((s, n), F32), jax.ShapeDtypeStruct((s, nb), BF16)],
        compiler_params=_cparams("parallel"))(x, sc, sh, w)


def _ffn_dact(dxo, y, sg, wd, uv, *, name, side=None):
    s, d = y.shape
    f = wd.shape[0]
    t, fc = _rows(s, FFN_TM), _tile(f, FFN_CHUNK)

    def body(dxo_ref, y_ref, sg_ref, w_ref, uv_ref, dy_ref, duv_ref, ds_ref):
        @pl.when(pl.program_id(0) == 0)
        def _():
            ds_ref[...] = jnp.zeros_like(ds_ref)

        dv = dxo_ref[...]
        dyb = (sg_ref[...] * dv).astype(BF16)
        dy_ref[...] = dyb
        ds_ref[...] += jnp.sum(dv * y_ref[...], axis=0, keepdims=True)
        for j in range(f // fc):
            da = _dot(dyb, w_ref[j * fc:(j + 1) * fc, :], NT)
            duv_ref[:, j * fc:(j + 1) * fc] = (da * uv_ref[:, j * fc:(j + 1) * fc].astype(F32)).astype(BF16)
            duv_ref[:, f + j * fc:f + (j + 1) * fc] = (da * uv_ref[:, f + j * fc:f + (j + 1) * fc].astype(F32)).astype(BF16)

    return _call_with_exchange(
        body, side, lambda: pl.program_id(0) == 0, lambda: pl.program_id(0) == s // t - 1, (dxo, y, sg, wd, uv),
        name=name, grid=(s // t,),
        in_specs=[_row_spec(t, d), _row_spec(t, d), _vec_spec(d), _resident(wd.shape), _row_spec(t, 2 * f)],
        out_specs=[_row_spec(t, d), _row_spec(t, 2 * f), _vec_spec(d)],
        out_shape=[jax.ShapeDtypeStruct((s, d), BF16), jax.ShapeDtypeStruct((s, 2 * f), BF16), jax.ShapeDtypeStruct((1, d), F32)],
        compiler_params=_cparams("arbitrary"))


def _ffn_dh(duv, wgu, x, sc, dxo, *, name, side=None):
    s, d = x.shape
    f2 = wgu.shape[1]
    t = _rows(s, FFN_TM)

    def body(duv_ref, w_ref, x_ref, sc_ref, dxo_ref, dx_ref, dsc_ref, dsh_ref):
        @pl.when(pl.program_id(0) == 0)
        def _():
            dsc_ref[...] = jnp.zeros_like(dsc_ref)
            dsh_ref[...] = jnp.zeros_like(dsh_ref)

        dhv = _dot(duv_ref[...], w_ref[...], NT)
        xv = x_ref[...]
        r = lax.rsqrt(jnp.mean(xv * xv, axis=1, keepdims=True) + EPS)
        xn = xv * r
        dxn = dhv * (1.0 + sc_ref[...])
        dx_ref[...] = dxo_ref[...] + r * (dxn - xn * jnp.mean(dxn * xn, axis=1, keepdims=True))
        dsc_ref[...] += jnp.sum(dhv * xn, axis=0, keepdims=True)
        dsh_ref[...] += jnp.sum(dhv, axis=0, keepdims=True)

    return _call_with_exchange(
        body, side, lambda: pl.program_id(0) == 0, lambda: pl.program_id(0) == s // t - 1, (duv, wgu, x, sc, dxo),
        name=name, grid=(s // t,),
        in_specs=[_row_spec(t, f2), _resident(wgu.shape), _row_spec(t, d), _vec_spec(d), _row_spec(t, d)],
        out_specs=[_row_spec(t, d), _vec_spec(d), _vec_spec(d)],
        out_shape=[jax.ShapeDtypeStruct((s, d), F32), jax.ShapeDtypeStruct((1, d), F32), jax.ShapeDtypeStruct((1, d), F32)],
        compiler_params=_cparams("arbitrary"))


def _dh_pieces(pieces, w, x, sc, dxo, *, name):
    s, d = x.shape
    t = _rows(s, FFN_TM)
    widths = [p.shape[1] for p in pieces]
    offs = [sum(widths[:i]) for i in range(len(widths))]
    kd = sum(widths)
    npc = len(pieces)

    def body(*refs):
        p_refs = refs[:npc]
        w_ref, x_ref, sc_ref, dxo_ref, dx_ref, dsc_ref, dsh_ref, cat_ref = refs[npc:]

        @pl.when(pl.program_id(0) == 0)
        def _():
            dsc_ref[...] = jnp.zeros_like(dsc_ref)
            dsh_ref[...] = jnp.zeros_like(dsh_ref)

        dhv = None
        for p_ref, off, wd in zip(p_refs, offs, widths):
            pb = p_ref[...].astype(BF16)
            cat_ref[:, off:off + wd] = pb
            part = _dot(pb, w_ref[:, off:off + wd], NT)
            dhv = part if dhv is None else dhv + part
        xv = x_ref[...]
        r = lax.rsqrt(jnp.mean(xv * xv, axis=1, keepdims=True) + EPS)
        xn = xv * r
        dxn = dhv * (1.0 + sc_ref[...])
        dx_ref[...] = dxo_ref[...] + r * (dxn - xn * jnp.mean(dxn * xn, axis=1, keepdims=True))
        dsc_ref[...] += jnp.sum(dhv * xn, axis=0, keepdims=True)
        dsh_ref[...] += jnp.sum(dhv, axis=0, keepdims=True)

    return _pcall(
        body, name=name, grid=(s // t,),
        in_specs=[_row_spec(t, wd) for wd in widths] + [_resident(w.shape), _row_spec(t, d), _vec_spec(d), _row_spec(t, d)],
        out_specs=[_row_spec(t, d), _vec_spec(d), _vec_spec(d), _row_spec(t, kd)],
        out_shape=[jax.ShapeDtypeStruct((s, d), F32), jax.ShapeDtypeStruct((1, d), F32), jax.ShapeDtypeStruct((1, d), F32),
                   jax.ShapeDtypeStruct((s, kd), BF16)],
        compiler_params=_cparams("arbitrary"))(*pieces, w, x, sc, dxo)


def _ffn_fwd(x, sh, sc, g, wgu, wd, tag, side=None):
    (h, uv, a), got = _ffn_up(x, sc, sh, wgu, name=f"{tag}_up", side=side)
    y, xo = _mm(a, wd, name=f"{tag}_down", tm=512, tn=1024, tk=wd.shape[0], res=x, scale=0.5 * g)
    return xo, (x, h, uv, a, y), got


def _ffn_bwd(dxo, saved, sc, g, wgu, wd, tag, ride=None):
    x, h, uv, a, y = saved
    if ride is None:
        (dyb, duv, dgs), _ = _ffn_dact(dxo, y, 0.5 * g, wd, uv, name=f"{tag}_dact")
        (dx, dsc, dsh), _ = _ffn_dh(duv, wgu, x, sc, dxo, name=f"{tag}_dh")
        dwgu = _mm_tn(h, duv, name=f"{tag}_dwgu", tm=1024, tn=1408, tk=512)
        dwd = _mm_tn(a, dyb, name=f"{tag}_dwd", tm=1408, tn=1024, tk=512)
        return dx, dwgu, dwd, dsh, dsc, 0.5 * dgs, {}
    (dyb, duv, dgs), got = _ffn_dact(dxo, y, 0.5 * g, wd, uv, name=f"{tag}_dact", side=_Xchg([ride[1]], "chips", True))
    part = _sum_slots(got[0], name=f"{tag}_sum_a", out_dtype=BF16)
    dwd, swap_a = _mm_tn(a, dyb, name=f"{tag}_dwd", tm=1408, tn=1024, tk=512, side=_Xchg([part], "sib", False))
    (dx, dsc, dsh), got = _ffn_dh(duv, wgu, x, sc, dxo, name=f"{tag}_dh", side=_Xchg([_by_chip("ffn1_w_down", dwd)], "chips", True))
    part = _sum_slots(got[0], name=f"{tag}_sum_b", out_dtype=BF16)
    dwgu, swap_b = _mm_tn(h, duv, name=f"{tag}_dwgu", tm=1024, tn=1408, tk=512, side=_Xchg([part], "sib", False))
    return dx, dwgu, dwd, dsh, dsc, 0.5 * dgs, {ride[0]: swap_a[0], "ffn1_w_down": swap_b[0]}


def _layer_fwd(x0, mod, w, par, tag, carry=None, own=None):
    sh1, sc1, g1, sh2, sc2, g2, sh3, sc3, g3 = (mod[i:i + 1] for i in range(N_MOD))
    x1, f1, got = _ffn_fwd(x0, sh1, sc1, g1, w["gu1"], w["d1"], f"{tag}_ffn1",
                           side=None if own is None else _Xchg(own[0], "chips", False))
    if own is not None:
        w = {**w, **own[1](got)}
    h2, proj, qkv = _norm_mm(x1, sc2, sh2, w["in"], 3 * SB_W, name=f"{tag}_in")
    o_a, got_a = _sb_fwd(qkv, name=f"{tag}_sb", side=None if carry is None else _Xchg(carry[0], "chips", False))
    qd, kd = _dil_prep(proj, par["gq"], par["gk"], par["cos"], par["sin"], name=f"{tag}_dil_prep")
    nums, dens, mxs = [], [], []
    for _, r in DIL_PATTERNS:
        nu, de, mx = _dil_fwd(qd, kd, proj, r, name=f"{tag}_dil{r}")
        nums.append(nu)
        dens.append(de)
        mxs.append(mx)
    (oh, st), got_b = _hgrn_fwd(proj, par["la"], par["lc"], name=f"{tag}_hgrn",
                                side=None if carry is None else _Xchg(carry[1], "chips", False))
    ymix, od, mall, zall = _mix_out(o_a, nums, dens, mxs, oh, proj, par["hg"], name=f"{tag}_mix_out")
    out, x2 = _mm(ymix, w["out"], name=f"{tag}_out", tm=512, tn=1024, tk=1024, res=x1, scale=g2)
    x3, f2, _ = _ffn_fwd(x2, sh3, sc3, g3, w["gu2"], w["d2"], f"{tag}_ffn2")
    return x3, dict(f1=f1, f2=f2, x1=x1, h2=h2, proj=proj, qkv=qkv, o_a=o_a, qd=qd, kd=kd, oh=oh, st=st,
                    ymix=ymix, od=od, mall=mall, zall=zall, out=out), (None if carry is None else (got_a, got_b)), w


COL_SHARDED = ("ffn1_w_gate", "ffn1_w_up", "w_in", "ffn2_w_gate", "ffn2_w_up")
ROW_SHARDED = ("ffn1_w_down", "w_out", "ffn2_w_down")


def _by_chip(name, g):
    if name in COL_SHARDED:
        return jnp.moveaxis(g.reshape(g.shape[0], 4, -1), 1, 0)
    return g.reshape(4, -1, g.shape[1])


def _halves(x):
    f = x.shape[1] // 2
    return x[:, :f], x[:, f:]


def _layer_bwd(dx3, sv, mod, w, par, tag, pending):
    sh1, sc1, g1, sh2, sc2, g2, sh3, sc3, g3 = (mod[i:i + 1] for i in range(N_MOD))
    dx2, dwgu2, dwd2, dsh3, dsc3, dg3, _ = _ffn_bwd(dx3, sv["f2"], sc3, g3, w["gu2"], w["d2"], f"{tag}_ffn2")
    doutb, dg2 = _gate_bwd(dx2, sv["out"], g2, name=f"{tag}_dgate2")
    dmix = _mm(doutb, w["out"], name=f"{tag}_dmix", tb=True, tm=512, tn=1024, tk=1024)
    dwout = _mm_tn(sv["ymix"], doutb, name=f"{tag}_dwout", tm=1024, tn=1024, tk=512)
    proj = sv["proj"]
    doh, dgh, delta, dhg = _mix_out_bwd(dmix, sv["oh"], proj, par["hg"], sv["od"], name=f"{tag}_dmix_out")
    dgate2, dup2 = _halves(dwgu2)
    ride = dict(pending)
    for n, g in (("ffn2_w_gate", dgate2), ("ffn2_w_up", dup2), ("ffn2_w_down", dwd2), ("w_out", dwout)):
        ride[(tag, n)] = _by_chip(n, g)
    keys = list(ride)
    (dqa, dka, dva), got = _sb_bwd(sv["qkv"], sv["o_a"], dmix, name=f"{tag}_dsb", side=_Xchg([ride[k] for k in keys], "chips", True))
    parts = [_sum_slots(g, name=f"{tag}_sum{i}", out_dtype=BF16) for i, g in enumerate(got)]
    dqs, dks, dvs = [], [], []
    for _, r in DIL_PATTERNS:
        a, b, c = _dil_bwd(sv["qd"], sv["kd"], proj, dmix, sv["mall"], sv["zall"], delta, r, name=f"{tag}_ddil{r}")
        dqs.append(a)
        dks.append(b)
        dvs.append(c)
    dqd, dkd, dvd, dgq, dgk = _dil_prep_bwd(proj, par["gq"], par["gk"], par["cos"], par["sin"], dqs, dks, dvs,
                                             name=f"{tag}_ddil_prep")
    (dqh, dfh, dih, dla, dlc), swapped = _hgrn_bwd(proj, par["la"], par["lc"], sv["st"], doh, name=f"{tag}_dhgrn",
                                                   side=_Xchg(parts, "sib", False))
    dx1, dsc2, dsh2, dproj = _dh_pieces([dqa, dka, dva, dqd, dkd, dvd, dqh, dfh, dih, dgh], w["in"], sv["x1"], sc2, dx2,
                                         name=f"{tag}_dh2")
    dwin = _mm_tn(sv["h2"], dproj, name=f"{tag}_dwin", tm=1024, tn=1792, tk=512)
    dx0, dwgu1, _, dsh1, dsc1, dg1, rode = _ffn_bwd(dx1, sv["f1"], sc1, g1, w["gu1"], w["d1"], f"{tag}_ffn1",
                                                    ride=("w_in", _by_chip("w_in", dwin)))
    dmod = jnp.concatenate([dsh1, dsc1, dg1, dsh2, dsc2, dg2, dsh3, dsc3, dg3], axis=0)
    fold = lambda v: v.reshape(2, HEAD_DIM).sum(axis=0)
    dgate1, dup1 = _halves(dwgu1)
    late = {(tag, n): _by_chip(n, g) for n, g in (("ffn1_w_gate", dgate1), ("ffn1_w_up", dup1))}
    grads = dict(late=late, dmod=dmod, gq=fold(dgq), gk=fold(dgk), hg=dhg[0], la=dla[0], lc=dlc[0])
    return dx0, grads, {**dict(zip(keys, swapped)), **{(tag, n): v for n, v in rode.items()}}


def _pack(pieces):
    flat = jnp.concatenate([p.reshape(-1) for p in pieces])
    pad = (-flat.shape[0]) % (8 * LANES)
    return jnp.pad(flat, (0, pad)).reshape(-1, LANES)


def _unpack(flat, like):
    out, off = [], 0
    for p in like:
        out.append(flat[off:off + p.size].reshape(p.shape))
        off += p.size
    return out


def kernel(x, c, w_mod, b_mod, ffn1_w_gate, ffn1_w_up, ffn1_w_down, w_in, w_out, q_norm_g, k_norm_g, hgrn_norm_g, hgrn_lb_logits, ffn2_w_gate, ffn2_w_up, ffn2_w_down, loss_target, m_w_mod, m_b_mod, m_ffn1_w_gate, m_ffn1_w_up, m_ffn1_w_down, m_w_in, m_w_out, m_q_norm_g, m_k_norm_g, m_hgrn_norm_g, m_hgrn_lb_logits, m_ffn2_w_gate, m_ffn2_w_up, m_ffn2_w_down, v_w_mod, v_b_mod, v_ffn1_w_gate, v_ffn1_w_up, v_ffn1_w_down, v_w_in, v_w_out, v_q_norm_g, v_k_norm_g, v_hgrn_norm_g, v_hgrn_lb_logits, v_ffn2_w_gate, v_ffn2_w_up, v_ffn2_w_down):
    names = ["w_mod", "b_mod", "ffn1_w_gate", "ffn1_w_up", "ffn1_w_down", "w_in", "w_out", "q_norm_g", "k_norm_g",
             "hgrn_norm_g", "hgrn_lb_logits", "ffn2_w_gate", "ffn2_w_up", "ffn2_w_down"]
    wts = dict(zip(names, (w_mod, b_mod, ffn1_w_gate, ffn1_w_up, ffn1_w_down, w_in, w_out, q_norm_g, k_norm_g, hgrn_norm_g,
                           hgrn_lb_logits, ffn2_w_gate, ffn2_w_up, ffn2_w_down)))
    mom = dict(zip(names, (m_w_mod, m_b_mod, m_ffn1_w_gate, m_ffn1_w_up, m_ffn1_w_down, m_w_in, m_w_out, m_q_norm_g, m_k_norm_g,
                           m_hgrn_norm_g, m_hgrn_lb_logits, m_ffn2_w_gate, m_ffn2_w_up, m_ffn2_w_down)))
    var = dict(zip(names, (v_w_mod, v_b_mod, v_ffn1_w_gate, v_ffn1_w_up, v_ffn1_w_down, v_w_in, v_w_out, v_q_norm_g, v_k_norm_g,
                           v_hgrn_norm_g, v_hgrn_lb_logits, v_ffn2_w_gate, v_ffn2_w_up, v_ffn2_w_down)))
    depth = w_mod.shape[0]
    assert depth == 2 and x.shape[0] == 1
    s, d = x.shape[1:]
    assert s % (DIL_PATTERNS[-1][1] * QBLK) == 0 and d % LANES == 0
    xi, yi, ci = lax.axis_index("x"), lax.axis_index("y"), lax.axis_index("c")
    chip = 2 * xi + yi
    dev = 2 * chip + ci
    x0, tgt = x[0], loss_target[0]

    c8 = _exchange([c.reshape(d // LANES, LANES)], "all", False, name="gather_c")[0].reshape(8, d)
    ncol = w_mod.shape[2]
    b_loc = lax.dynamic_slice_in_dim(b_mod, chip * ncol, ncol, axis=1)
    m_loc = _mod_fwd(c8, w_mod, b_loc.reshape(depth, 1, ncol), name="mod_fwd")
    m_all = _exchange([m_loc], "chips", False, name="gather_mod")[0]
    mod = jnp.transpose(lax.dynamic_index_in_dim(m_all, dev, axis=2, keepdims=False), (1, 0, 2)).reshape(depth, N_MOD, d)

    col_sharded = ["ffn1_w_gate", "ffn1_w_up", "w_in", "ffn2_w_gate", "ffn2_w_up"]
    row_sharded = ["ffn1_w_down", "w_out", "ffn2_w_down"]
    big = col_sharded + row_sharded
    early = ["ffn1_w_gate", "ffn1_w_up", "ffn1_w_down", "w_in", "w_out"]
    late = ["ffn2_w_gate", "ffn2_w_up", "ffn2_w_down"]

    def shards(l, group):
        return [wts[n][l].astype(BF16) for n in group]

    def whole(group, gathered):
        out = {}
        for n, g in zip(group, gathered):
            out[n] = jnp.moveaxis(g, 0, 1).reshape(g.shape[1], -1) if n in col_sharded else g.reshape(-1, g.shape[2])
        return out

    def early_weights(gathered):
        full = whole(early, gathered)
        return {"gu1": jnp.concatenate([full["ffn1_w_gate"], full["ffn1_w_up"]], axis=1), "d1": full["ffn1_w_down"],
                "in": full["w_in"], "out": full["w_out"]}

    def late_weights(gathered):
        full = whole(late, gathered)
        return {"gu2": jnp.concatenate([full["ffn2_w_gate"], full["ffn2_w_up"]], axis=1), "d2": full["ffn2_w_down"]}

    ws = [early_weights(_exchange(shards(0, early), "chips", False, name="gather_w"))]

    _, la, lc = _lb_prep(hgrn_lb_logits, name="lb_prep")
    cos, sin = _rope_tables(s)
    pars = [dict(gq=jnp.tile(q_norm_g[l], 2)[None], gk=jnp.tile(k_norm_g[l], 2)[None], hg=hgrn_norm_g[l][None],
                 la=la[l:l + 1], lc=lc[l:l + 1], cos=cos, sin=sin) for l in range(depth)]

    xs, saved = x0, []
    for l in range(depth):
        carry = (shards(l + 1, early), shards(l + 1, late)) if l + 1 < depth else None
        own = (shards(l, late), late_weights) if l == 0 else None
        xs, sv, got, ws[l] = _layer_fwd(xs, mod[l], ws[l], pars[l], f"l{l}", carry, own)
        saved.append(sv)
        if got is not None:
            ws.append({**early_weights(got[0]), **late_weights(got[1])})
    dx, lpart = _loss_grad(xs, tgt, name="loss")

    grads, halves, pending = [None] * depth, {}, {}
    for l in reversed(range(depth)):
        dx, grads[l], swapped = _layer_bwd(dx, saved[l], mod[l], ws[l], pars[l], f"l{l}", pending)
        halves.update(swapped)
        pending = grads[l]["late"]
    keys = list(pending)
    got = _exchange([pending[k] for k in keys], "chips", True, name="scatter_grads")
    parts = [_sum_slots(g, name=f"sum_{k[1]}", out_dtype=BF16) for k, g in zip(keys, got)]
    halves.update(zip(keys, _exchange(parts, "sib", False, name="swap_grads")))

    stack = lambda k: jnp.stack([grads[l][k] for l in range(depth)])
    small = [stack("dmod"), stack("gq"), stack("gk"), stack("hg"), stack("la"), stack("lc"), lpart[0, :1]]
    packed = _pack(small)
    allp = _exchange([packed], "all", False, name="gather_small")[0]
    tot = _unpack(_sum_slots(allp, name="sum_small").reshape(-1), small)
    g_b_mod = tot[0].reshape(depth, N_MOD * d)
    loss = tot[6][0]
    g_small = {"b_mod": g_b_mod, "q_norm_g": tot[1], "k_norm_g": tot[2], "hgrn_norm_g": tot[3],
               "hgrn_lb_logits": _lb_bwd(hgrn_lb_logits, tot[4], tot[5], name="lb_bwd")}

    dm_all = allp.reshape(8, -1)[:, :depth * N_MOD * d].reshape(8, depth, N_MOD * d)
    dm_loc = jnp.transpose(lax.dynamic_slice_in_dim(dm_all, chip * ncol, ncol, axis=2), (1, 0, 2))
    g_w_mod = _mod_bwd(c8.T, dm_loc, name="mod_bwd")

    outs = {}
    for n in names:
        w2 = wts[n].reshape(-1, wts[n].shape[-1])
        m2, v2 = mom[n].reshape(w2.shape), var[n].reshape(w2.shape)
        if n in big:
            res = _adamw_layers(w2, [halves[(f"l{l}", n)] for l in range(depth)], m2, v2, name=f"adamw_{n}")
        else:
            g = g_w_mod if n == "w_mod" else g_small[n]
            res = _adamw(w2, [g.reshape(w2.shape)], m2, v2, name=f"adamw_{n}")
        outs[n] = [r.reshape(wts[n].shape) for r in res]
    return (loss, dx[None], *[outs[n][0] for n in names], *[outs[n][1] for n in names], *[outs[n][2] for n in names],
            *[outs[n][3] for n in names])
```

```python
import jax
import jax.numpy as jnp
from jax import lax
from jax.experimental import pallas as pl
from jax.experimental.pallas import tpu as pltpu

F32 = jnp.float32
BF16 = jnp.bfloat16
MESH_ID = pl.DeviceIdType.MESH

HEAD_DIM = 64
SB_W = 256
DIL_W = 256
HG_W = 512
HG_D = 128
IN_W = 3 * SB_W + 3 * DIL_W + 4 * HG_W
MIX_W = SB_W + DIL_W + HG_W
DIL_PATTERNS = ((128, 1), (512, 4), (2048, 16))
ROPE_THETA = 10000.0
EPS = 1e-6
LB_FLOOR = 1e-30
NEG_BIG = -1e30
N_MOD = 9
ADAM_LR = 0.001
ADAM_B1 = 0.9
ADAM_B2 = 0.999
ADAM_EPS = 1e-08
ADAM_WD = 0.01
ADAM_STEP = 10

LANES = 128
QBLK = 128
DIL_TILE = 1024
HG_BLK = 16
HG_TILE = 256
SB_EXIT = 88.0
VMEM_LIMIT = 48 * 1024 * 1024

NN = (((1,), (0,)), ((), ()))
NT = (((1,), (1,)), ((), ()))
TN = (((0,), (0,)), ((), ()))


def _pcall(body, **kw):
    return pl.pallas_call(body, **kw)


def _cparams(*sem):
    return pltpu.CompilerParams(dimension_semantics=sem if sem else None, vmem_limit_bytes=VMEM_LIMIT)


def _dot(a, b, dims=NN):
    return lax.dot_general(a, b, dims, preferred_element_type=F32)


def _split(x, n):
    parts = []
    r = x
    for i in range(n):
        p = r.astype(BF16)
        parts.append(p)
        if i + 1 < n:
            r = r - p.astype(F32)
    return parts


def _xdot(x, m, n=2):
    return sum(_dot(p, m) for p in _split(x, n))


def _iota(shape, dim):
    return lax.broadcasted_iota(jnp.int32, shape, dim)


def _sigmoid(x):
    return 1.0 / (1.0 + jnp.exp(-x))


def _tile(dim, pref, mult=LANES):
    t = (min(pref, dim) // mult) * mult
    while t >= mult:
        if dim % t == 0:
            return t
        t -= mult
    return dim


def _rows(dim, pref):
    return _tile(dim, pref, 8)


def _mm(a, b, *, name, tb=False, tm=512, tn=1024, tk=1024, out_dtype=F32, res=None, scale=None):
    m, kd = a.shape
    n = b.shape[0] if tb else b.shape[1]
    tm, tn, tk = _rows(m, tm), _tile(n, tn), _tile(kd, tk)
    nk = kd // tk
    epi = res is not None

    def body(*refs):
        if epi:
            a_ref, b_ref, r_ref, s_ref, o_ref, x_ref, acc = refs
        else:
            a_ref, b_ref, o_ref, acc = refs
        k = pl.program_id(2)

        @pl.when(k == 0)
        def _():
            acc[...] = jnp.zeros_like(acc)

        acc[...] += _dot(a_ref[...], b_ref[...], NT if tb else NN)

        @pl.when(k == nk - 1)
        def _():
            o_ref[...] = acc[...].astype(o_ref.dtype)
            if epi:
                x_ref[...] = r_ref[...] + s_ref[...] * acc[...]

    in_specs = [
        pl.BlockSpec((tm, tk), lambda i, j, k: (i, k)),
        pl.BlockSpec((tn, tk), lambda i, j, k: (j, k)) if tb else pl.BlockSpec((tk, tn), lambda i, j, k: (k, j)),
    ]
    out_shape = [jax.ShapeDtypeStruct((m, n), out_dtype)]
    out_specs = [pl.BlockSpec((tm, tn), lambda i, j, k: (i, j))]
    args = [a, b]
    if epi:
        in_specs += [pl.BlockSpec((tm, tn), lambda i, j, k: (i, j)), pl.BlockSpec((1, tn), lambda i, j, k: (0, j))]
        out_shape.append(jax.ShapeDtypeStruct((m, n), F32))
        out_specs.append(pl.BlockSpec((tm, tn), lambda i, j, k: (i, j)))
        args += [res, scale]
    out = _pcall(
        body, name=name, grid=(m // tm, n // tn, nk), in_specs=in_specs, out_specs=out_specs, out_shape=out_shape,
        scratch_shapes=[pltpu.VMEM((tm, tn), F32)], compiler_params=_cparams("parallel", "parallel", "arbitrary"),
    )(*args)
    return out if epi else out[0]


def _mm_tn(a, b, *, name, tm=1024, tn=1408, tk=512, out_dtype=BF16, side=None):
    s, m = a.shape
    n = b.shape[1]
    tm, tn, tk = _tile(m, tm), _tile(n, tn), _rows(s, tk)
    nk = s // tk
    ni, nj = m // tm, n // tn

    def body(a_ref, b_ref, o_ref, acc):
        k = pl.program_id(2)

        @pl.when(k == 0)
        def _():
            acc[...] = jnp.zeros_like(acc)

        acc[...] += _dot(a_ref[...], b_ref[...], TN)

        @pl.when(k == nk - 1)
        def _():
            o_ref[...] = acc[...].astype(o_ref.dtype)

    at = lambda i, j, k: (pl.program_id(0) == i) & (pl.program_id(1) == j) & (pl.program_id(2) == k)
    (out,), got = _call_with_exchange(
        body, side, lambda: at(0, 0, 0), lambda: at(ni - 1, nj - 1, nk - 1), (a, b), name=name, grid=(ni, nj, nk),
        in_specs=[pl.BlockSpec((tk, tm), lambda i, j, k: (k, i)), pl.BlockSpec((tk, tn), lambda i, j, k: (k, j))],
        out_specs=[pl.BlockSpec((tm, tn), lambda i, j, k: (i, j))], out_shape=[jax.ShapeDtypeStruct((m, n), out_dtype)],
        scratch_shapes=[pltpu.VMEM((tm, tn), F32)],
        compiler_params=_cparams(*(("parallel", "parallel", "arbitrary") if side is None else ("arbitrary",) * 3)))
    return out if side is None else (out, got)


TE = 512


def _row_spec(t, w, col=0):
    return pl.BlockSpec((t, w), lambda i, col=col: (i, col))


def _vec_spec(w, col=0):
    return pl.BlockSpec((1, w), lambda i, col=col: (0, col))


def _gate_bwd(dxo, y, sg, *, name):
    s, d = y.shape
    t = _rows(s, TE)

    def body(dxo_ref, y_ref, sg_ref, dy_ref, ds_ref):
        @pl.when(pl.program_id(0) == 0)
        def _():
            ds_ref[...] = jnp.zeros_like(ds_ref)

        dv = dxo_ref[...]
        dy_ref[...] = (sg_ref[...] * dv).astype(BF16)
        ds_ref[...] += jnp.sum(dv * y_ref[...], axis=0, keepdims=True)

    return _pcall(
        body, name=name, grid=(s // t,), in_specs=[_row_spec(t, d), _row_spec(t, d), _vec_spec(d)],
        out_specs=[_row_spec(t, d), _vec_spec(d)],
        out_shape=[jax.ShapeDtypeStruct((s, d), BF16), jax.ShapeDtypeStruct((1, d), F32)],
        compiler_params=_cparams("arbitrary"))(dxo, y, sg)


def _loss_grad(y, tgt, *, name):
    s, d = y.shape
    t = _rows(s, TE)
    nt = s // t

    def body(y_ref, t_ref, dy_ref, l_ref, acc):
        i = pl.program_id(0)

        @pl.when(i == 0)
        def _():
            acc[...] = jnp.zeros_like(acc)

        e = y_ref[...] - t_ref[...]
        dy_ref[...] = e * (1.0 / d)
        acc[...] += jnp.sum(e * e, axis=0, keepdims=True)

        @pl.when(i == nt - 1)
        def _():
            l_ref[...] = jnp.broadcast_to(jnp.sum(acc[...], axis=1, keepdims=True) * (0.5 / d), l_ref.shape)

    return _pcall(
        body, name=name, grid=(nt,), in_specs=[_row_spec(t, d), _row_spec(t, d)],
        out_specs=[_row_spec(t, d), pl.BlockSpec((1, LANES), lambda i: (0, 0))],
        out_shape=[jax.ShapeDtypeStruct((s, d), F32), jax.ShapeDtypeStruct((1, LANES), F32)],
        scratch_shapes=[pltpu.VMEM((1, d), F32)], compiler_params=_cparams("arbitrary"))(y, tgt)


SB_TQ = 256
SB_NK = SB_TQ // QBLK


def _sb_consts():
    r = _iota((QBLK, LANES), 0)
    c = _iota((QBLK, LANES), 1)
    ones = jnp.ones((QBLK, LANES), BF16)
    after = jnp.concatenate([jnp.where(r > c, 1.0, 0.0).astype(BF16), ones], axis=1)
    from_ = jnp.concatenate([jnp.where(r >= c, 1.0, 0.0).astype(BF16), ones], axis=1)
    return _iota((SB_TQ, LANES), 0), _iota((SB_TQ, LANES), 1), after, from_


def _sb_scores(qm, kb, strict):
    z = _dot(qm, kb, NT) * (HEAD_DIM ** -0.5)
    sp = jnp.log(1.0 + jnp.exp(-jnp.abs(z)))
    lnb = -(jnp.maximum(z, 0.0) + sp)
    lb = jnp.minimum(z, 0.0) - sp
    if strict is not None:
        lnb = jnp.where(strict, lnb, 0.0)
    return lnb, lb


def _sb_fwd(qkv, *, name, side=None):
    s = qkv.shape[0]
    nq = s // SB_TQ

    def body(q_ref, k_ref, v_ref, o_ref, *scr):
        acc, osc = scr[:4], scr[4:]
        qi = pl.program_id(0)
        row, lane, after, _ = _sb_consts()
        h0 = lane < HEAD_DIM
        q = q_ref[...]
        qms = []
        for p in range(2):
            qp = q[:, p * LANES:(p + 1) * LANES]
            qms += [jnp.where(h0, qp, jnp.zeros_like(qp)), jnp.where(h0, jnp.zeros_like(qp), qp)]

        def block(kj, mask, r0=0):
            rows = slice(r0, SB_TQ)
            off = pl.multiple_of(kj * QBLK, QBLK)
            kbs = [k_ref[pl.ds(off, QBLK), p * LANES:(p + 1) * LANES] for p in range(2)]
            vbs = [v_ref[pl.ds(off, QBLK), p * LANES:(p + 1) * LANES] for p in range(2)]
            mask = None if mask is None else mask[rows]
            sc = [_sb_scores(qms[c][rows], kbs[c // 2], mask) for c in range(4)]
            trs = [_xdot(sc[c][0], after) for c in range(4)]
            top = None
            for c in range(4):
                w = jnp.exp(sc[c][1] + trs[c][:, :QBLK] + acc[c][rows, :])
                if mask is not None:
                    w = jnp.where(mask, w, 0.0)
                osc[c][rows, :] += _xdot(w, vbs[c // 2])
                new = acc[c][rows, :] + trs[c][:, QBLK:]
                acc[c][rows, :] = new
                top = new if top is None else jnp.maximum(top, new)
            return jnp.max(top)

        for ref in scr:
            ref[...] = jnp.zeros_like(ref)
        top = None
        for j in reversed(range(SB_NK)):
            top = block(qi * SB_NK + j, (lane + j * QBLK) < row, j * QBLK)
        lax.while_loop(lambda c: (c[0] >= 0) & (c[1] > -SB_EXIT), lambda c: (c[0] - 1, block(c[0], None)),
                       (qi * SB_NK - 1, top))
        for p in range(2):
            o_ref[:, p * LANES:(p + 1) * LANES] = jnp.where(h0, osc[2 * p][...], osc[2 * p + 1][...])

    (o,), got = _call_with_exchange(
        body, side, lambda: pl.program_id(0) == 0, lambda: pl.program_id(0) == nq - 1, (qkv, qkv, qkv), name=name, grid=(nq,),
        in_specs=[pl.BlockSpec((SB_TQ, SB_W), lambda i: (i, 0)),
                  pl.BlockSpec((s, SB_W), lambda i: (0, 1)),
                  pl.BlockSpec((s, SB_W), lambda i: (0, 2))],
        out_specs=[pl.BlockSpec((SB_TQ, SB_W), lambda i: (i, 0))],
        out_shape=[jax.ShapeDtypeStruct((s, SB_W), F32)],
        scratch_shapes=[pltpu.VMEM((SB_TQ, LANES), F32)] * 8,
        compiler_params=_cparams("arbitrary"))
    return o, got


def _sb_bwd(qkv, o, dmix, *, name, side=None):
    s = qkv.shape[0]
    nq = s // SB_TQ
    scale = HEAD_DIM ** -0.5

    def body(q_ref, k_ref, v_ref, o_ref, do_ref, dq_ref, dk_ref, dv_ref, a0, a1, r0, r1, dqs, dks, dvs):
        acc, racc = (a0, a1), (r0, r1)
        i = pl.program_id(1)
        qi = nq - 1 - i
        row, lane, after, from_ = _sb_consts()
        klane = _iota((QBLK, LANES), 1)
        khms = (klane < HEAD_DIM, klane >= HEAD_DIM)

        @pl.when(i == 0)
        def _():
            dks[...] = jnp.zeros_like(dks)
            dvs[...] = jnp.zeros_like(dvs)

        q = q_ref[...]
        do = do_ref[...]
        dob = do.astype(BF16)
        dd = do * o_ref[...]
        dol = (do - dob.astype(F32)).astype(BF16)
        zero = jnp.zeros_like(q)
        hms = (lane < HEAD_DIM, lane >= HEAD_DIM)
        qms = [jnp.where(hm, q, zero) for hm in hms]
        doms = [jnp.where(hm, dob, zero) for hm in hms]
        dols = [jnp.where(hm, dol, zero) for hm in hms]
        dsums = [jnp.sum(jnp.where(hm, dd, 0.0), axis=1, keepdims=True) for hm in hms]

        def block(kj, mask, start=0):
            rows = slice(start, SB_TQ)
            off = pl.multiple_of(kj * QBLK, QBLK)
            kb = k_ref[pl.ds(off, QBLK), :]
            vb = v_ref[pl.ds(off, QBLK), :]
            mask = None if mask is None else mask[rows]
            top, dq, dk, dv = None, None, None, None
            sc = [_sb_scores(qms[h][rows], kb, mask) for h in range(2)]
            trs = [_xdot(sc[h][0], after) for h in range(2)]
            dws = [_dot(doms[h][rows], vb, NT) + _dot(dols[h][rows], vb, NT) for h in range(2)]
            for h in range(2):
                lb, tr = sc[h][1], trs[h]
                w = jnp.exp(lb + tr[:, :QBLK] + acc[h][rows, :])
                if mask is not None:
                    w = jnp.where(mask, w, 0.0)
                g = w * dws[h]
                tg = _xdot(g, from_)
                before = dsums[h][rows] - (tg[:, :QBLK] + racc[h][rows, :])
                dz = g - jnp.exp(lb) * (g + before)
                if mask is not None:
                    dz = jnp.where(mask, dz, 0.0)
                dzb = (dz * scale).astype(BF16)
                dqh = _dot(dzb, jnp.where(khms[h], kb, jnp.zeros_like(kb)))
                dkh = _dot(dzb, qms[h][rows], TN)
                dvh = _dot(w.astype(BF16), doms[h][rows], TN)
                dq, dk, dv = (dqh, dkh, dvh) if h == 0 else (dq + dqh, dk + dkh, dv + dvh)
                new = acc[h][rows, :] + tr[:, QBLK:]
                acc[h][rows, :] = new
                racc[h][rows, :] += tg[:, QBLK:]
                top = new if top is None else jnp.maximum(top, new)
            dqs[rows, :] += dq
            dks[pl.ds(off, QBLK), :] += dk
            dvs[pl.ds(off, QBLK), :] += dv
            return jnp.max(top)

        for ref in (dqs, a0, a1, r0, r1):
            ref[...] = jnp.zeros_like(ref)
        top = None
        for j in reversed(range(SB_NK)):
            top = block(qi * SB_NK + j, (lane + j * QBLK) < row, j * QBLK)
        lax.while_loop(lambda c: (c[0] >= 0) & (c[1] > -SB_EXIT), lambda c: (c[0] - 1, block(c[0], None)),
                       (qi * SB_NK - 1, top))
        dq_ref[...] = dqs[...]
        fin = pl.multiple_of(qi * SB_TQ, SB_TQ)
        dk_ref[...] = dks[pl.ds(fin, SB_TQ), :]
        dv_ref[...] = dvs[pl.ds(fin, SB_TQ), :]

    blk = lambda c0: pl.BlockSpec((SB_TQ, LANES), lambda p, i, c0=c0: (nq - 1 - i, c0 + p))
    return _call_with_exchange(
        body, side, lambda: (pl.program_id(0) == 0) & (pl.program_id(1) == 0),
        lambda: (pl.program_id(0) == 1) & (pl.program_id(1) == nq - 1), (qkv, qkv, qkv, o, dmix), name=name, grid=(2, nq),
        in_specs=[blk(0), pl.BlockSpec((s, LANES), lambda p, i: (0, 2 + p)), pl.BlockSpec((s, LANES), lambda p, i: (0, 4 + p)),
                  blk(0), blk(0)],
        out_specs=[blk(0), blk(0), blk(0)],
        out_shape=[jax.ShapeDtypeStruct((s, SB_W), F32)] * 3,
        scratch_shapes=[pltpu.VMEM((SB_TQ, LANES), F32)] * 5 + [pltpu.VMEM((s, LANES), F32), pltpu.VMEM((s, LANES), F32)],
        compiler_params=_cparams("arbitrary", "arbitrary"))


def _seg_consts():
    r = _iota((LANES, LANES), 0)
    c = _iota((LANES, LANES), 1)
    return jnp.where((r >> 6) == (c >> 6), 1.0, 0.0).astype(BF16)


def _rot_half(x, lane):
    half = HEAD_DIM // 2
    return jnp.where((lane & (HEAD_DIM - 1)) < half, pltpu.roll(x, LANES - half, 1), pltpu.roll(x, half, 1))


def _rope_tables(s):
    half = HEAD_DIM // 2
    inv_freq = ROPE_THETA ** (-jnp.arange(half, dtype=F32) * 2.0 / HEAD_DIM)
    ang = jnp.arange(s, dtype=F32)[:, None] * inv_freq[None, :]
    cos, sin = jnp.cos(ang), jnp.sin(ang)
    return jnp.tile(jnp.concatenate([cos, cos], axis=1), (1, 2)), jnp.tile(jnp.concatenate([-sin, sin], axis=1), (1, 2))


def _dil_prep(proj, gq, gk, cos, sin, *, name):
    s = proj.shape[0]
    t = _rows(s, TE)
    c0 = 3 * SB_W // LANES

    def body(q_ref, k_ref, gq_ref, gk_ref, cos_ref, sin_ref, qo_ref, ko_ref):
        seg = _seg_consts()
        lane = _iota((t, LANES), 1)
        cs, sn = cos_ref[...], sin_ref[...]
        for x_ref, g_ref, o_ref, mul in ((q_ref, gq_ref, qo_ref, HEAD_DIM ** -0.5), (k_ref, gk_ref, ko_ref, 1.0)):
            for j in range(2):
                xv = x_ref[:, j * LANES:(j + 1) * LANES]
                ms = _xdot(xv * xv, seg, 3) * (1.0 / HEAD_DIM)
                xn = xv * lax.rsqrt(ms + EPS) * g_ref[...]
                o_ref[:, j * LANES:(j + 1) * LANES] = (xn * cs + _rot_half(xn, lane) * sn) * mul

    return _pcall(
        body, name=name, grid=(s // t,),
        in_specs=[pl.BlockSpec((t, DIL_W), lambda i: (i, c0 // 2)), pl.BlockSpec((t, DIL_W), lambda i: (i, c0 // 2 + 1)),
                  _vec_spec(LANES), _vec_spec(LANES), _row_spec(t, LANES), _row_spec(t, LANES)],
        out_specs=[_row_spec(t, DIL_W), _row_spec(t, DIL_W)],
        out_shape=[jax.ShapeDtypeStruct((s, DIL_W), F32)] * 2, compiler_params=_cparams("parallel"))(proj, proj, gq, gk, cos, sin)


def _dil_prep_bwd(proj, gq, gk, cos, sin, dqs, dks, dvs, *, name):
    s = proj.shape[0]
    t = _rows(s, TE)
    c0 = 3 * SB_W // LANES

    def body(q_ref, k_ref, gq_ref, gk_ref, cos_ref, sin_ref, a0, a1, a2, b0, b1, b2, c0_ref, c1_ref, c2_ref,
             dq_ref, dk_ref, dv_ref, dgq_ref, dgk_ref):
        @pl.when(pl.program_id(0) == 0)
        def _():
            dgq_ref[...] = jnp.zeros_like(dgq_ref)
            dgk_ref[...] = jnp.zeros_like(dgk_ref)

        dv_ref[...] = c0_ref[...] + c1_ref[...] + c2_ref[...]
        seg = _seg_consts()
        lane = _iota((t, LANES), 1)
        cs, sn = cos_ref[...], sin_ref[...]
        for x_ref, g_ref, parts, o_ref, dg_ref, mul in ((q_ref, gq_ref, (a0, a1, a2), dq_ref, dgq_ref, HEAD_DIM ** -0.5),
                                                          (k_ref, gk_ref, (b0, b1, b2), dk_ref, dgk_ref, 1.0)):
            for j in range(2):
                sl = slice(j * LANES, (j + 1) * LANES)
                dout = (parts[0][:, sl] + parts[1][:, sl] + parts[2][:, sl]) * mul
                dxn = dout * cs + _rot_half(dout * sn, lane)
                xv = x_ref[:, sl]
                r = lax.rsqrt(_xdot(xv * xv, seg, 3) * (1.0 / HEAD_DIM) + EPS)
                xh = xv * r
                dg_ref[...] += jnp.sum(dxn * xh, axis=0, keepdims=True)
                dxh = dxn * g_ref[...]
                o_ref[:, sl] = r * (dxh - xh * (_xdot(dxh * xh, seg, 3) * (1.0 / HEAD_DIM)))

    rs = _row_spec(t, DIL_W)
    return _pcall(
        body, name=name, grid=(s // t,),
        in_specs=[pl.BlockSpec((t, DIL_W), lambda i: (i, c0 // 2)), pl.BlockSpec((t, DIL_W), lambda i: (i, c0 // 2 + 1)),
                  _vec_spec(LANES), _vec_spec(LANES), _row_spec(t, LANES), _row_spec(t, LANES)] + [rs] * 9,
        out_specs=[rs, rs, rs, _vec_spec(LANES), _vec_spec(LANES)],
        out_shape=[jax.ShapeDtypeStruct((s, DIL_W), F32)] * 3 + [jax.ShapeDtypeStruct((1, LANES), F32)] * 2,
        compiler_params=_cparams("arbitrary"))(proj, proj, gq, gk, cos, sin, *dqs, *dks, *dvs)


def _dil_masks(n):
    row = _iota((QBLK, 2 * LANES), 0)
    col = _iota((QBLK, 2 * LANES), 1)
    return ((col < LANES) & (col >= row) & (n > 0)) | ((col >= LANES) & (col - LANES <= row))


DIL_V0 = (3 * SB_W + 2 * DIL_W) // LANES
DIL_DO0 = SB_W // LANES


def _dil_tiles(s, r):
    span = QBLK * r
    nsub = max(1, DIL_TILE // span)
    while s % (nsub * span):
        nsub -= 1
    return span, nsub


def _dil_rows(j, rho, span, r):
    return pl.ds(j * span + rho, QBLK, stride=r) if r > 1 else pl.ds(j * span, QBLK)


def _dil_fwd(q, k, proj, r, *, name):
    s = q.shape[0]
    span, nsub = _dil_tiles(s, r)
    tr = nsub * span

    def body(q_ref, kc_ref, kp_ref, vc_ref, vp_ref, num_ref, den_ref, mx_ref):
        n = pl.program_id(1)
        lane = _iota((QBLK, LANES), 1)
        h0 = lane < HEAD_DIM
        ones = jnp.ones((2 * QBLK, LANES), BF16)
        for j in range(nsub):
            valid = _dil_masks(n if j == 0 else 1)
            for rho in range(r):
                rows = _dil_rows(j, rho, span, r)
                before = _dil_rows(max(j - 1, 0), rho, span, r)
                k_prev, v_prev = (kp_ref, vp_ref) if j == 0 else (kc_ref, vc_ref)
                qv = q_ref[rows, :].astype(BF16)
                kk = jnp.concatenate([k_prev[before, :], kc_ref[rows, :]], axis=0).astype(BF16)
                vv = jnp.concatenate([jnp.concatenate([v_prev[before, :], vc_ref[rows, :]], axis=0).astype(BF16), ones], axis=1)
                res = []
                for h in range(2):
                    qm = jnp.where(h0 if h == 0 else ~h0, qv, jnp.zeros_like(qv))
                    sc = jnp.where(valid, _dot(qm, kk, NT), NEG_BIG)
                    mx = jnp.max(sc, axis=1, keepdims=True)
                    nd = _dot(jnp.exp(sc - mx).astype(BF16), vv)
                    res.append((nd[:, :LANES], nd[:, LANES:], mx))
                num_ref[rows, :] = jnp.where(h0, res[0][0], res[1][0])
                den_ref[rows, :] = jnp.where(h0, res[0][1], res[1][1])
                mx_ref[rows, :] = jnp.where(h0, res[0][2], res[1][2])

    cur = lambda c0: pl.BlockSpec((tr, LANES), lambda p, n, c0=c0: (n, c0 + p))
    prev = lambda c0: pl.BlockSpec((span, LANES), lambda p, n, c0=c0: (jnp.maximum(n * nsub - 1, 0), c0 + p))
    return _pcall(
        body, name=name, grid=(2, s // tr), in_specs=[cur(0), cur(0), prev(0), cur(DIL_V0), prev(DIL_V0)],
        out_specs=[cur(0), cur(0), cur(0)], out_shape=[jax.ShapeDtypeStruct((s, DIL_W), F32)] * 3,
        compiler_params=_cparams("parallel", "arbitrary"))(q, k, k, proj, proj)


def _dil_bwd(q, k, proj, dmix, mall, zall, delta, r, *, name):
    s = q.shape[0]
    span, nsub = _dil_tiles(s, r)
    tr = nsub * span
    nbig = s // tr

    def body(q_ref, kc_ref, kp_ref, vc_ref, vp_ref, do_ref, m_ref, z_ref, dl_ref, dq_ref, dk_ref, dv_ref, pk, pv):
        n = pl.program_id(1)
        lane = _iota((QBLK, LANES), 1)
        h0 = lane < HEAD_DIM

        @pl.when(n == 0)
        def _():
            pk[...] = jnp.zeros_like(pk)
            pv[...] = jnp.zeros_like(pv)

        @pl.when(n < nbig)
        def _():
            dk_ref[...] = pk[...]
            dv_ref[...] = pv[...]
            for j in range(nsub):
                valid = _dil_masks(n if j == 0 else 1)
                for rho in range(r):
                    rows = _dil_rows(j, rho, span, r)
                    before = _dil_rows(max(j - 1, 0), rho, span, r)
                    k_prev, v_prev = (kp_ref, vp_ref) if j == 0 else (kc_ref, vc_ref)
                    qv = q_ref[rows, :].astype(BF16)
                    dob = do_ref[rows, :].astype(BF16)
                    zero = jnp.zeros_like(qv)
                    kk = jnp.concatenate([k_prev[before, :], kc_ref[rows, :]], axis=0).astype(BF16)
                    vv = jnp.concatenate([v_prev[before, :], vc_ref[rows, :]], axis=0).astype(BF16)
                    mall_v, z_v, dl_v = m_ref[rows, :], z_ref[rows, :], dl_ref[rows, :]
                    dq, dk, dv = None, None, None
                    for h in range(2):
                        hm = h0 if h == 0 else ~h0
                        qm = jnp.where(hm, qv, zero)
                        dom = jnp.where(hm, dob, zero)
                        c = h * HEAD_DIM
                        sc = jnp.where(valid, _dot(qm, kk, NT), NEG_BIG)
                        pr = jnp.exp(sc - mall_v[:, c:c + 1]) * (1.0 / z_v[:, c:c + 1])
                        ds = (pr * (_dot(dom, vv, NT) - dl_v[:, c:c + 1])).astype(BF16)
                        parts = (_dot(ds, jnp.where(jnp.concatenate([hm, hm], axis=0), kk, jnp.zeros_like(kk))),
                                 _dot(ds, qm, TN), _dot(pr.astype(BF16), dom, TN))
                        dq, dk, dv = parts if h == 0 else (dq + parts[0], dk + parts[1], dv + parts[2])
                    dq_ref[rows, :] = dq
                    pk[rows, :] = dk[QBLK:]
                    pv[rows, :] = dv[QBLK:]
                    if j == 0:
                        last_span = _dil_rows(nsub - 1, rho, span, r)
                        dk_ref[last_span, :] += dk[:QBLK]
                        dv_ref[last_span, :] += dv[:QBLK]
                    else:
                        pk[before, :] += dk[:QBLK]
                        pv[before, :] += dv[:QBLK]

        @pl.when(n == nbig)
        def _():
            dk_ref[...] = pk[...]
            dv_ref[...] = pv[...]

    last = nbig - 1
    cur = lambda c0: pl.BlockSpec((tr, LANES), lambda p, n, c0=c0: (jnp.minimum(n, last), c0 + p))
    prev = lambda c0: pl.BlockSpec((span, LANES), lambda p, n, c0=c0: (jnp.maximum(jnp.minimum(n, last) * nsub - 1, 0), c0 + p))
    late = pl.BlockSpec((tr, LANES), lambda p, n: (jnp.maximum(n - 1, 0), p))
    return _pcall(
        body, name=name, grid=(2, nbig + 1),
        in_specs=[cur(0), cur(0), prev(0), cur(DIL_V0), prev(DIL_V0), cur(DIL_DO0), cur(0), cur(0), cur(0)],
        out_specs=[cur(0), late, late], out_shape=[jax.ShapeDtypeStruct((s, DIL_W), F32)] * 3,
        scratch_shapes=[pltpu.VMEM((tr, LANES), F32)] * 2,
        compiler_params=_cparams("parallel", "arbitrary"))(q, k, k, proj, proj, dmix, mall, zall, delta)


HG_SHIFT = HG_BLK.bit_length() - 1
HG_Q0, HG_F0, HG_I0 = (3 * SB_W + 3 * DIL_W) // HG_D, (3 * SB_W + 3 * DIL_W + HG_W) // HG_D, (3 * SB_W + 3 * DIL_W + 2 * HG_W) // HG_D


def _hg_scan(x):
    t = x.shape[0]
    half = HG_BLK // 2
    rb = _iota((t, LANES), 0) & (HG_BLK - 1)
    rh = rb & (half - 1)
    p = x
    for s in (1, 2, 4):
        p = p + jnp.where(rh >= s, pltpu.roll(p, s, 0), 0.0)
    h = jnp.where(rh == half - 1, p, 0.0)
    for s in (1, 2, 4):
        h = h + jnp.where(rh + s < half, pltpu.roll(h, t - s, 0), 0.0)
    first = rb < half
    pref = jnp.where(first, p, p + pltpu.roll(h, half, 0))
    total = h + jnp.where(first, pltpu.roll(h, t - half, 0), pltpu.roll(h, half, 0))
    return p, h, pref, total, first


def _hg_same(t):
    i = jnp.arange(t) >> HG_SHIFT
    return (i[:, None] == i[None, :]).astype(F32)


def _hg_own(t):
    i = jnp.arange(t) >> HG_SHIFT
    j = jnp.arange(t // HG_BLK * HG_D) // HG_D
    return (i[:, None] == j[None, :]).astype(BF16)


def _hg_diag(x, nb):
    return jnp.concatenate([x[b * HG_BLK:(b + 1) * HG_BLK, b * HG_D:(b + 1) * HG_D] for b in range(nb)], axis=0)


def _hg_inputs(qh, z, v, la, lc, t):
    lsg = jnp.minimum(z, 0.0) - jnp.log(1.0 + jnp.exp(-jnp.abs(z)))
    b = lc + lsg
    lf = jnp.maximum(la, b) + jnp.log(1.0 + jnp.exp(-jnp.abs(la - b)))
    f = jnp.exp(lf)
    sq = _sigmoid(qh)
    p, h, g, gl, first = _hg_scan(lf)
    k = 1.0 - f
    qs = qh * sq
    eq = jnp.where(first, 0.0, jnp.exp(jnp.minimum(p, 0.0)))
    ek = jnp.where(first, jnp.exp(jnp.minimum(h - p, 0.0)), 0.0)
    return dict(lf=lf, b=b, f=f, k=k, sq=sq, qs=qs, g=g, eg=jnp.exp(g), egl=jnp.exp(gl - g), dec=jnp.exp(gl),
                eq=eq, ek=ek, qx=(qs * eq).astype(BF16), kx=(k * ek).astype(BF16))


def _hgrn_fwd(proj, la, lc, *, name, side=None):
    s = proj.shape[0]
    t = _rows(s, HG_TILE)
    nt, nb = s // t, t // HG_BLK

    def body(q_ref, f_ref, i_ref, la_ref, lc_ref, same_ref, own_ref, o_ref, st_ref, state):
        @pl.when(pl.program_id(1) == 0)
        def _():
            state[...] = jnp.zeros_like(state)

        v = i_ref[...]
        a = _hg_inputs(q_ref[...], f_ref[...], v, la_ref[...], lc_ref[...], t)
        qs, k = a["qs"], a["k"]
        vb = v.astype(BF16)
        rb = _iota((t, LANES), 0) & (HG_BLK // 2 - 1)
        o = jnp.sum(qs * k, axis=1, keepdims=True) * v
        e = None
        for d in range(1, HG_BLK // 2):
            m = rb >= d
            fr = a["f"] if d == 1 else pltpu.roll(a["f"], d - 1, 0)
            e = fr if e is None else e * fr
            cd = jnp.sum(qs * pltpu.roll(k, d, 0) * e, axis=1, keepdims=True)
            o = o + jnp.where(m, cd, 0.0) * pltpu.roll(v, d, 0)
        cross = _dot(a["qx"], a["kx"], NT) * same_ref[...]
        o = o + _dot(cross.astype(BF16), vb)
        qt = (qs * a["eg"]).astype(BF16)
        kt = (k * a["egl"]).astype(BF16)
        upd = _dot(vb, jnp.tile(kt, (1, nb)) * own_ref[...], TN)
        st = state[...]
        for blk in range(nb):
            st_ref[blk * HG_D:(blk + 1) * HG_D, :] = st.astype(BF16)
            st = a["dec"][blk * HG_BLK:blk * HG_BLK + 1] * st + upd[:, blk * HG_D:(blk + 1) * HG_D]
        state[...] = st
        o_ref[...] = o + _hg_diag(_dot(qt, st_ref[...], NT), nb)

    col = lambda c0: pl.BlockSpec((t, HG_D), lambda hd, i, c0=c0: (i, c0 + hd))
    vec = pl.BlockSpec((1, HG_D), lambda hd, i: (0, hd))
    return _call_with_exchange(
        body, side, lambda: (pl.program_id(0) == 0) & (pl.program_id(1) == 0),
        lambda: (pl.program_id(0) == 3) & (pl.program_id(1) == nt - 1),
        (proj, proj, proj, la, lc, _hg_same(t), _hg_own(t)), name=name, grid=(4, nt),
        in_specs=[col(HG_Q0), col(HG_F0), col(HG_I0), vec, vec, pl.BlockSpec((t, t), lambda hd, i: (0, 0)),
                  pl.BlockSpec((t, nb * HG_D), lambda hd, i: (0, 0))],
        out_specs=[col(0), pl.BlockSpec((None, nb * HG_D, HG_D), lambda hd, i: (hd, i, 0))],
        out_shape=[jax.ShapeDtypeStruct((s, HG_W), F32), jax.ShapeDtypeStruct((4, s // HG_BLK * HG_D, HG_D), BF16)],
        scratch_shapes=[pltpu.VMEM((HG_D, HG_D), F32)],
        compiler_params=_cparams("arbitrary", "arbitrary"))


def _hgrn_bwd(proj, la, lc, st, doh, *, name, side=None):
    s = proj.shape[0]
    t = _rows(s, HG_TILE)
    nt, nb = s // t, t // HG_BLK

    def body(q_ref, f_ref, i_ref, la_ref, lc_ref, st_ref, do_ref, same_ref, own_ref, dq_ref, df_ref, di_ref, dla_ref, dlc_ref,
             dstate, dsb):
        @pl.when(pl.program_id(1) == 0)
        def _():
            dstate[...] = jnp.zeros_like(dstate)
            dla_ref[...] = jnp.zeros_like(dla_ref)
            dlc_ref[...] = jnp.zeros_like(dlc_ref)

        qh, z, v, do = q_ref[...], f_ref[...], i_ref[...], do_ref[...]
        la = la_ref[...]
        a = _hg_inputs(qh, z, v, la, lc_ref[...], t)
        qs, k, g = a["qs"], a["k"], a["g"]
        vb = v.astype(BF16)
        dob = do.astype(BF16)
        rb = _iota((t, LANES), 0) & (HG_BLK // 2 - 1)
        dc0 = jnp.sum(do * v, axis=1, keepdims=True)
        dq = dc0 * k
        dk = dc0 * qs
        dv = jnp.sum(qs * k, axis=1, keepdims=True) * do
        e = None
        for d in range(1, HG_BLK // 2):
            m = rb >= d
            fr = a["f"] if d == 1 else pltpu.roll(a["f"], d - 1, 0)
            e = fr if e is None else e * fr
            ks = pltpu.roll(k, d, 0)
            qe = qs * e
            cd = jnp.where(m, jnp.sum(qe * ks, axis=1, keepdims=True), 0.0)
            dcd = jnp.where(m, jnp.sum(do * pltpu.roll(v, d, 0), axis=1, keepdims=True), 0.0)
            dq = dq + dcd * ks * e
            dk = dk + pltpu.roll(dcd * qe, t - d, 0)
            dv = dv + pltpu.roll(cd * do, t - d, 0)
        same = same_ref[...]
        cross = (_dot(a["qx"], a["kx"], NT) * same).astype(BF16)
        dcross = (_dot(dob, vb, NT) * same).astype(BF16)
        dq = dq + _dot(dcross, a["kx"]) * a["eq"]
        dk = dk + _dot(dcross, a["qx"], TN) * a["ek"]
        dv = dv + _dot(cross, dob, TN)
        qt = (qs * a["eg"]).astype(BF16)
        kt = (k * a["egl"]).astype(BF16)
        own = own_ref[...]
        upd = _dot(dob, jnp.tile(qt, (1, nb)) * own, TN)
        ds = dstate[...]
        dgs = [None] * nb
        for blk in reversed(range(nb)):
            rows = slice(blk * HG_D, (blk + 1) * HG_D)
            dec = a["dec"][blk * HG_BLK:blk * HG_BLK + 1]
            dsb[rows, :] = ds.astype(BF16)
            dgs[blk] = jnp.broadcast_to(jnp.sum(ds * st_ref[rows, :].astype(F32), axis=0, keepdims=True) * dec, (HG_BLK, HG_D))
            ds = dec * ds + upd[:, rows]
        dstate[...] = ds
        dki = _dot(jnp.tile(vb, (1, nb)) * own, dsb[...]) * a["egl"]
        dq = dq + _dot(jnp.tile(dob, (1, nb)) * own, st_ref[...]) * a["eg"]
        dk = dk + dki
        dv = dv + _hg_diag(_dot(kt, dsb[...], NT), nb)
        x = qs * dq - k * dk
        _, _, xpre, xtot, _ = _hg_scan(x)
        _, _, _, ktot, _ = _hg_scan(k * dki)
        dlf = (xtot - xpre + x) + ktot + jnp.concatenate(dgs, axis=0) - a["f"] * dk
        wb = jnp.exp(a["b"] - a["lf"])
        wa = jnp.exp(la - a["lf"])
        sq = a["sq"]
        dq_ref[...] = dq * (sq * (1.0 + qh * (1.0 - sq)))
        df_ref[...] = dlf * wb * (1.0 - _sigmoid(z))
        di_ref[...] = dv
        dla_ref[...] += jnp.sum(dlf * wa, axis=0, keepdims=True)
        dlc_ref[...] += jnp.sum(dlf * wb, axis=0, keepdims=True)

    col = lambda c0: pl.BlockSpec((t, HG_D), lambda hd, i, c0=c0: (nt - 1 - i, c0 + hd))
    vec = pl.BlockSpec((1, HG_D), lambda hd, i: (0, hd))
    return _call_with_exchange(
        body, side, lambda: (pl.program_id(0) == 0) & (pl.program_id(1) == 0),
        lambda: (pl.program_id(0) == 3) & (pl.program_id(1) == nt - 1),
        (proj, proj, proj, la, lc, st, doh, _hg_same(t), _hg_own(t)), name=name, grid=(4, nt),
        in_specs=[col(HG_Q0), col(HG_F0), col(HG_I0), vec, vec,
                  pl.BlockSpec((None, nb * HG_D, HG_D), lambda hd, i: (hd, nt - 1 - i, 0)), col(0),
                  pl.BlockSpec((t, t), lambda hd, i: (0, 0)), pl.BlockSpec((t, nb * HG_D), lambda hd, i: (0, 0))],
        out_specs=[col(0), col(0), col(0), vec, vec],
        out_shape=[jax.ShapeDtypeStruct((s, HG_W), F32)] * 3 + [jax.ShapeDtypeStruct((1, HG_W), F32)] * 2,
        scratch_shapes=[pltpu.VMEM((HG_D, HG_D), F32), pltpu.VMEM((nb * HG_D, HG_D), BF16)],
        compiler_params=_cparams("arbitrary", "arbitrary"))


GH0 = (IN_W - HG_W) // HG_W


def _mix_out(o_a, nums, dens, mxs, oh, proj, hg, *, name):
    s = o_a.shape[0]
    t = _rows(s, TE)

    def body(oa_ref, n0, n1, n2, d0, d1, d2, m0, m1, m2, oh_ref, gh_ref, hg_ref, y_ref, od_ref, mall_ref, z_ref):
        y_ref[:, :SB_W] = oa_ref[...].astype(BF16)
        m = jnp.maximum(jnp.maximum(m0[...], m1[...]), m2[...])
        num = jnp.zeros((t, DIL_W), F32)
        z = jnp.zeros((t, DIL_W), F32)
        for n_ref, d_ref, m_ref in ((n0, d0, m0), (n1, d1, m1), (n2, d2, m2)):
            sc = jnp.exp(m_ref[...] - m)
            num = num + n_ref[...] * sc
            z = z + d_ref[...] * sc
        od = num / z
        od_ref[...] = od
        mall_ref[...] = m
        z_ref[...] = z
        y_ref[:, SB_W:SB_W + DIL_W] = od.astype(BF16)
        for h in range(4):
            sl = slice(h * HG_D, (h + 1) * HG_D)
            ov = oh_ref[:, sl]
            g = gh_ref[:, sl]
            r = lax.rsqrt(jnp.mean(ov * ov, axis=1, keepdims=True) + EPS)
            y_ref[:, SB_W + DIL_W + h * HG_D:SB_W + DIL_W + (h + 1) * HG_D] = (ov * r * hg_ref[...] * (g * _sigmoid(g))).astype(BF16)

    rd = _row_spec(t, DIL_W)
    return _pcall(
        body, name=name, grid=(s // t,),
        in_specs=[rd] * 10 + [_row_spec(t, HG_W), _row_spec(t, HG_W, GH0), _vec_spec(HG_D)],
        out_specs=[_row_spec(t, MIX_W), rd, rd, rd],
        out_shape=[jax.ShapeDtypeStruct((s, MIX_W), BF16)] + [jax.ShapeDtypeStruct((s, DIL_W), F32)] * 3,
        compiler_params=_cparams("parallel"))(o_a, *nums, *dens, *mxs, oh, proj, hg)


def _mix_out_bwd(dmix, oh, proj, hg, od, *, name):
    s = oh.shape[0]
    t = _rows(s, TE)

    def body(dm_ref, oh_ref, gh_ref, hg_ref, od_ref, doh_ref, dgh_ref, dl_ref, dhg_ref):
        @pl.when(pl.program_id(0) == 0)
        def _():
            dhg_ref[...] = jnp.zeros_like(dhg_ref)

        seg = _seg_consts()
        for j in range(2):
            sl = slice(j * LANES, (j + 1) * LANES)
            dl_ref[:, sl] = _xdot(dm_ref[:, SB_W + j * LANES:SB_W + (j + 1) * LANES] * od_ref[:, sl], seg, 3)
        hgv = hg_ref[...]
        for h in range(4):
            sl = slice(h * HG_D, (h + 1) * HG_D)
            dy = dm_ref[:, SB_W + DIL_W + h * HG_D:SB_W + DIL_W + (h + 1) * HG_D]
            ov = oh_ref[:, sl]
            g = gh_ref[:, sl]
            sg = _sigmoid(g)
            silu = g * sg
            r = lax.rsqrt(jnp.mean(ov * ov, axis=1, keepdims=True) + EPS)
            nrm = ov * r
            dhg_ref[...] += jnp.sum(dy * nrm * silu, axis=0, keepdims=True)
            dgh_ref[:, sl] = dy * nrm * hgv * (sg * (1.0 + g * (1.0 - sg)))
            dn = dy * hgv * silu
            doh_ref[:, sl] = r * (dn - nrm * jnp.mean(dn * nrm, axis=1, keepdims=True))

    rh = _row_spec(t, HG_W)
    return _pcall(
        body, name=name, grid=(s // t,),
        in_specs=[_row_spec(t, MIX_W), rh, _row_spec(t, HG_W, GH0), _vec_spec(HG_D), _row_spec(t, DIL_W)],
        out_specs=[rh, rh, _row_spec(t, DIL_W), _vec_spec(HG_D)],
        out_shape=[jax.ShapeDtypeStruct((s, HG_W), F32)] * 2 + [jax.ShapeDtypeStruct((s, DIL_W), F32), jax.ShapeDtypeStruct((1, HG_D), F32)],
        compiler_params=_cparams("arbitrary"))(dmix, oh, proj, hg, od)


def _lb_terms(l):
    l0, l1 = l[0:1], l[1:2]
    m = jnp.maximum(l0, l1)
    e0, e1 = jnp.exp(l0 - m), jnp.exp(l1 - m)
    s0, s1 = e0 / (e0 + e1), e1 / (e0 + e1)
    args = (s0 - s0, (s0 + s1) - s0)
    lbs = tuple(jnp.minimum(jnp.maximum(a, 0.0), 1.0 - EPS) for a in args)
    return s0, s1, args, lbs


def _lb_prep(logits, *, name):
    def body(l_ref, lb_ref, la_ref, lc_ref):
        _, _, _, lbs = _lb_terms(l_ref[...])
        lb = jnp.concatenate(lbs, axis=0)
        lb_ref[...] = lb
        la_ref[...] = jnp.log(jnp.maximum(lb, LB_FLOOR))
        lc_ref[...] = jnp.log1p(-lb)

    return _pcall(body, name=name, out_shape=[jax.ShapeDtypeStruct(logits.shape, F32)] * 3)(logits)


def _lb_bwd(logits, dla, dlc, *, name):
    def half(hi, eq):
        return jnp.where(hi, 1.0, jnp.where(eq, 0.5, 0.0))

    def body(l_ref, dla_ref, dlc_ref, o_ref):
        s0, s1, args, lbs = _lb_terms(l_ref[...])
        da = []
        for i in range(2):
            a, lb = args[i], lbs[i]
            dlb = dla_ref[i:i + 1] * half(lb > LB_FLOOR, lb == LB_FLOOR) / jnp.maximum(lb, LB_FLOOR) - dlc_ref[i:i + 1] / (1.0 - lb)
            t = jnp.maximum(a, 0.0)
            da.append(dlb * half(a > 0.0, a == 0.0) * half(t < 1.0 - EPS, t == 1.0 - EPS))
        ds0 = (da[0] + da[1]) - (da[0] + da[1])
        ds1 = da[1]
        dot = s0 * ds0 + s1 * ds1
        o_ref[...] = jnp.concatenate([s0 * (ds0 - dot), s1 * (ds1 - dot)], axis=0)

    return _pcall(body, name=name, out_shape=jax.ShapeDtypeStruct(logits.shape, F32))(logits, dla, dlc)


def _mod_fwd(c8, w, b, *, name):
    _, d, n = w.shape
    tn = _tile(n, 768)

    def body(c_ref, w_ref, b_ref, o_ref):
        cv = c_ref[...]
        o_ref[...] = _dot((cv * _sigmoid(cv)).astype(BF16), w_ref[...].astype(BF16)) + b_ref[...]

    return _pcall(
        body, name=name, grid=(2, n // tn),
        in_specs=[pl.BlockSpec((8, d), lambda l, j: (0, 0)), pl.BlockSpec((None, d, tn), lambda l, j: (l, 0, j)),
                  pl.BlockSpec((None, 1, tn), lambda l, j: (l, 0, j))],
        out_specs=pl.BlockSpec((None, 8, tn), lambda l, j: (l, 0, j)),
        out_shape=jax.ShapeDtypeStruct((2, 8, n), F32), compiler_params=_cparams("parallel", "parallel"))(c8, w, b)


def _mod_bwd(ct, dm, *, name):
    d = ct.shape[0]
    n = dm.shape[2]
    tn = _tile(n, 768)

    def body(c_ref, dm_ref, o_ref):
        cv = c_ref[...]
        sc = cv * _sigmoid(cv)
        dv = dm_ref[...]
        acc = sc[:, 0:1] * dv[0:1, :]
        for b in range(1, 8):
            acc = acc + sc[:, b:b + 1] * dv[b:b + 1, :]
        o_ref[...] = acc

    return _pcall(
        body, name=name, grid=(2, n // tn),
        in_specs=[pl.BlockSpec((d, 8), lambda l, j: (0, 0)), pl.BlockSpec((None, 8, tn), lambda l, j: (l, 0, j))],
        out_specs=pl.BlockSpec((None, d, tn), lambda l, j: (l, 0, j)),
        out_shape=jax.ShapeDtypeStruct((2, d, n), F32), compiler_params=_cparams("parallel", "parallel"))(ct, dm)


_PEERS = {
    "chips": ((1, 0, 0), (0, 1, 0), (1, 1, 0)),
    "all": tuple((a, b, c) for a in (0, 1) for b in (0, 1) for c in (0, 1) if a + b + c),
    "sib": ((0, 0, 1),),
}
_SLOTS = {"chips": 4, "all": 8, "sib": 2}


def _slot(kind, x, y, c):
    return {"chips": 2 * x + y, "all": 4 * x + 2 * y + c, "sib": c}[kind]


class _Xchg:
    def __init__(self, arrs, kind, scatter):
        self.arrs, self.kind, self.scatter = list(arrs), kind, scatter
        self.n = len(self.arrs)
        self.peers = _PEERS[kind]
        total = self.n * len(self.peers)
        self.specs = [pl.BlockSpec(memory_space=pl.ANY)] * self.n
        self.out_shape = [jax.ShapeDtypeStruct(a.shape if scatter else (_SLOTS[kind],) + a.shape, a.dtype) for a in self.arrs]
        self.scratch = [pltpu.SemaphoreType.DMA((total,)), pltpu.SemaphoreType.DMA((total,)), pltpu.SemaphoreType.DMA((self.n,))]

    def copies(self, ins, outs, send, recv, loc):
        kind, scatter = self.kind, self.scatter
        x, y, c = lax.axis_index("x"), lax.axis_index("y"), lax.axis_index("c")
        me = _slot(kind, x, y, c)
        out = []
        for a in range(self.n):
            out.append(pltpu.make_async_copy(ins[a].at[me] if scatter else ins[a], outs[a].at[me], loc.at[a]))
            for j, (dx, dy, dc) in enumerate(self.peers):
                px, py, pc = (1 - x if dx else x), (1 - y if dy else y), (1 - c if dc else c)
                sem = a * len(self.peers) + j
                out.append(pltpu.make_async_remote_copy(
                    src_ref=ins[a].at[_slot(kind, px, py, pc)] if scatter else ins[a], dst_ref=outs[a].at[me],
                    send_sem=send.at[sem], recv_sem=recv.at[sem], device_id=(px, py, pc), device_id_type=MESH_ID))
        return out


def _exchange(arrs, kind, scatter, *, name):
    xc = _Xchg(arrs, kind, scatter)

    def body(*refs):
        copies = xc.copies(refs[:xc.n], refs[xc.n:2 * xc.n], *refs[2 * xc.n:])
        for cp in copies:
            cp.start()
        for cp in copies:
            cp.wait()

    return _pcall(body, name=name, in_specs=xc.specs, out_specs=xc.specs, out_shape=xc.out_shape, scratch_shapes=xc.scratch)(*arrs)


def _with_exchange(body, n_in, n_out, n_scr, xc, first, last):
    def wrapped(*refs):
        ins, side_in = refs[:n_in], refs[n_in:n_in + xc.n]
        o0 = n_in + xc.n
        outs, side_out = refs[o0:o0 + n_out], refs[o0 + n_out:o0 + n_out + xc.n]
        s0 = o0 + n_out + xc.n
        scr, sems = refs[s0:s0 + n_scr], refs[s0 + n_scr:]

        @pl.when(first())
        def _():
            for cp in xc.copies(side_in, side_out, *sems):
                cp.start()

        body(*ins, *outs, *scr)

        @pl.when(last())
        def _():
            for cp in xc.copies(side_in, side_out, *sems):
                cp.wait()

    return wrapped


def _call_with_exchange(body, xc, first, last, args, *, in_specs, out_specs, out_shape, scratch_shapes=(), **kw):
    if xc is None:
        return _pcall(body, in_specs=in_specs, out_specs=out_specs, out_shape=out_shape, scratch_shapes=scratch_shapes, **kw)(*args), None
    wrapped = _with_exchange(body, len(in_specs), len(out_specs), len(scratch_shapes), xc, first, last)
    res = _pcall(wrapped, in_specs=list(in_specs) + xc.specs, out_specs=list(out_specs) + xc.specs,
                 out_shape=list(out_shape) + xc.out_shape, scratch_shapes=list(scratch_shapes) + xc.scratch, **kw)(*args, *xc.arrs)
    return res[:len(out_specs)], res[len(out_specs):]


def _sum_slots(a, *, name, out_dtype=F32):
    ns, r, c = a.shape
    t = _tile(r, max(16, (1 << 18) // c // 16 * 16), 16)

    def body(a_ref, o_ref):
        acc = a_ref[0].astype(F32)
        for i in range(1, ns):
            acc = acc + a_ref[i].astype(F32)
        o_ref[...] = acc.astype(o_ref.dtype)

    return _pcall(body, name=name, grid=(r // t,), in_specs=[pl.BlockSpec((ns, t, c), lambda i: (0, i, 0))],
                  out_specs=pl.BlockSpec((t, c), lambda i: (i, 0)), out_shape=jax.ShapeDtypeStruct((r, c), out_dtype),
                  compiler_params=_cparams("parallel"))(a)


def _adamw(w, gparts, m, v, *, name):
    r, c = w.shape
    t = _tile(r, max(16, (1 << 17) // c // 16 * 16), 16)
    ng = len(gparts)

    def body(*refs):
        w_ref, m_ref, v_ref = refs[0], refs[1 + ng], refs[2 + ng]
        g_ref, d_ref, nm_ref, nv_ref = refs[3 + ng:]
        g = refs[1][...].astype(F32)
        for i in range(1, ng):
            g = g + refs[1 + i][...].astype(F32)
        mn = ADAM_B1 * m_ref[...] + (1.0 - ADAM_B1) * g
        vn = ADAM_B2 * v_ref[...] + (1.0 - ADAM_B2) * (g * g)
        m_hat = mn / (1.0 - ADAM_B1 ** ADAM_STEP)
        v_hat = vn / (1.0 - ADAM_B2 ** ADAM_STEP)
        g_ref[...] = g
        d_ref[...] = -ADAM_LR * (m_hat / (jnp.sqrt(v_hat) + ADAM_EPS) + ADAM_WD * w_ref[...])
        nm_ref[...] = mn
        nv_ref[...] = vn

    spec = pl.BlockSpec((t, c), lambda i: (i, 0))
    return _pcall(body, name=name, grid=(r // t,), in_specs=[spec] * (3 + ng), out_specs=[spec] * 4,
                  out_shape=[jax.ShapeDtypeStruct((r, c), F32)] * 4, compiler_params=_cparams("parallel"))(w, *gparts, m, v)


def _adamw_layers(w, halves, m, v, *, name):
    r, c = w.shape
    nl = len(halves)
    rl = r // nl
    t = _tile(rl, max(16, (1 << 17) // c // 16 * 16), 16)
    nbl = rl // t

    def body(*refs):
        w_ref, m_ref, v_ref = refs[0], refs[1 + 2 * nl], refs[2 + 2 * nl]
        g_ref, d_ref, nm_ref, nv_ref = refs[3 + 2 * nl:]
        g = None
        for l in range(nl):
            gl = refs[1 + 2 * l][...].astype(F32) + refs[2 + 2 * l][...].astype(F32)
            g = gl if g is None else jnp.where(pl.program_id(0) >= l * nbl, gl, g)
        mn = ADAM_B1 * m_ref[...] + (1.0 - ADAM_B1) * g
        vn = ADAM_B2 * v_ref[...] + (1.0 - ADAM_B2) * (g * g)
        m_hat = mn / (1.0 - ADAM_B1 ** ADAM_STEP)
        v_hat = vn / (1.0 - ADAM_B2 ** ADAM_STEP)
        g_ref[...] = g
        d_ref[...] = -ADAM_LR * (m_hat / (jnp.sqrt(v_hat) + ADAM_EPS) + ADAM_WD * w_ref[...])
        nm_ref[...] = mn
        nv_ref[...] = vn

    spec = pl.BlockSpec((t, c), lambda i: (i, 0))
    part = lambda l, core: pl.BlockSpec((None, t, c), lambda i, l=l, core=core: (core, jnp.clip(i - l * nbl, 0, nbl - 1), 0))
    gspecs = [part(l, core) for l in range(nl) for core in range(2)]
    gargs = [halves[l] for l in range(nl) for _ in range(2)]
    return _pcall(body, name=name, grid=(r // t,), in_specs=[spec] + gspecs + [spec, spec], out_specs=[spec] * 4,
                  out_shape=[jax.ShapeDtypeStruct((r, c), F32)] * 4, compiler_params=_cparams("parallel"))(w, *gargs, m, v)


FFN_TM = 512
FFN_CHUNK = 1408


def _resident(shape):
    return pl.BlockSpec(shape, lambda i: (0,) * len(shape), pipeline_mode=pl.Buffered(1))


def _ffn_up(x, sc, sh, wgu, *, name, side=None):
    s, d = x.shape
    f = wgu.shape[1] // 2
    t, fc = _rows(s, FFN_TM), _tile(f, FFN_CHUNK)

    def body(x_ref, sc_ref, sh_ref, w_ref, h_ref, uv_ref, a_ref):
        xv = x_ref[...]
        r = lax.rsqrt(jnp.mean(xv * xv, axis=1, keepdims=True) + EPS)
        hb = ((xv * r) * (1.0 + sc_ref[...]) + sh_ref[...]).astype(BF16)
        h_ref[...] = hb
        for j in range(f // fc):
            u = _dot(hb, w_ref[:, j * fc:(j + 1) * fc])
            v = _dot(hb, w_ref[:, f + j * fc:f + (j + 1) * fc])
            sg = _sigmoid(u)
            silu = u * sg
            uv_ref[:, j * fc:(j + 1) * fc] = (v * (sg * (1.0 + u * (1.0 - sg)))).astype(BF16)
            uv_ref[:, f + j * fc:f + (j + 1) * fc] = silu.astype(BF16)
            a_ref[:, j * fc:(j + 1) * fc] = (silu * v).astype(BF16)

    return _call_with_exchange(
        body, side, lambda: pl.program_id(0) == 0, lambda: pl.program_id(0) == s // t - 1, (x, sc, sh, wgu),
        name=name, grid=(s // t,), in_specs=[_row_spec(t, d), _vec_spec(d), _vec_spec(d), _resident(wgu.shape)],
        out_specs=[_row_spec(t, d), _row_spec(t, 2 * f), _row_spec(t, f)],
        out_shape=[jax.ShapeDtypeStruct((s, d), BF16), jax.ShapeDtypeStruct((s, 2 * f), BF16), jax.ShapeDtypeStruct((s, f), BF16)],
        compiler_params=_cparams("parallel" if side is None else "arbitrary"))


def _norm_mm(x, sc, sh, w, nb, *, name):
    s, d = x.shape
    n = w.shape[1]
    t, nc = _rows(s, FFN_TM), _tile(n, 1792)
    assert nb <= nc

    def body(x_ref, sc_ref, sh_ref, w_ref, h_ref, o_ref, ob_ref):
        xv = x_ref[...]
        r = lax.rsqrt(jnp.mean(xv * xv, axis=1, keepdims=True) + EPS)
        hb = ((xv * r) * (1.0 + sc_ref[...]) + sh_ref[...]).astype(BF16)
        h_ref[...] = hb
        for j in range(n // nc):
            part = _dot(hb, w_ref[:, j * nc:(j + 1) * nc])
            o_ref[:, j * nc:(j + 1) * nc] = part
            if j == 0:
                ob_ref[...] = part[:, :nb].astype(BF16)

    return _pcall(
        body, name=name, grid=(s // t,), in_specs=[_row_spec(t, d), _vec_spec(d), _vec_spec(d), _resident(w.shape)],
        out_specs=[_row_spec(t, d), _row_spec(t, n), _row_spec(t, nb)],
        out_shape=[jax.ShapeDtypeStruct((s, d), BF16), jax.ShapeDtypeStruct((s, n), F32), jax.ShapeDtypeStruct((s, nb), BF16)],
        compiler_params=_cparams("parallel"))(x, sc, sh, w)


def _ffn_dact(dxo, y, sg, wd, uv, *, name, side=None):
    s, d = y.shape
    f = wd.shape[0]
    t, fc = _rows(s, FFN_TM), _tile(f, FFN_CHUNK)

    def body(dxo_ref, y_ref, sg_ref, w_ref, uv_ref, dy_ref, duv_ref, ds_ref):
        @pl.when(pl.program_id(0) == 0)
        def _():
            ds_ref[...] = jnp.zeros_like(ds_ref)

        dv = dxo_ref[...]
        dyb = (sg_ref[...] * dv).astype(BF16)
        dy_ref[...] = dyb
        ds_ref[...] += jnp.sum(dv * y_ref[...], axis=0, keepdims=True)
        for j in range(f // fc):
            da = _dot(dyb, w_ref[j * fc:(j + 1) * fc, :], NT)
            duv_ref[:, j * fc:(j + 1) * fc] = (da * uv_ref[:, j * fc:(j + 1) * fc].astype(F32)).astype(BF16)
            duv_ref[:, f + j * fc:f + (j + 1) * fc] = (da * uv_ref[:, f + j * fc:f + (j + 1) * fc].astype(F32)).astype(BF16)

    return _call_with_exchange(
        body, side, lambda: pl.program_id(0) == 0, lambda: pl.program_id(0) == s // t - 1, (dxo, y, sg, wd, uv),
        name=name, grid=(s // t,),
        in_specs=[_row_spec(t, d), _row_spec(t, d), _vec_spec(d), _resident(wd.shape), _row_spec(t, 2 * f)],
        out_specs=[_row_spec(t, d), _row_spec(t, 2 * f), _vec_spec(d)],
        out_shape=[jax.ShapeDtypeStruct((s, d), BF16), jax.ShapeDtypeStruct((s, 2 * f), BF16), jax.ShapeDtypeStruct((1, d), F32)],
        compiler_params=_cparams("arbitrary"))


def _ffn_dh(duv, wgu, x, sc, dxo, *, name, side=None):
    s, d = x.shape
    f2 = wgu.shape[1]
    t = _rows(s, FFN_TM)

    def body(duv_ref, w_ref, x_ref, sc_ref, dxo_ref, dx_ref, dsc_ref, dsh_ref):
        @pl.when(pl.program_id(0) == 0)
        def _():
            dsc_ref[...] = jnp.zeros_like(dsc_ref)
            dsh_ref[...] = jnp.zeros_like(dsh_ref)

        dhv = _dot(duv_ref[...], w_ref[...], NT)
        xv = x_ref[...]
        r = lax.rsqrt(jnp.mean(xv * xv, axis=1, keepdims=True) + EPS)
        xn = xv * r
        dxn = dhv * (1.0 + sc_ref[...])
        dx_ref[...] = dxo_ref[...] + r * (dxn - xn * jnp.mean(dxn * xn, axis=1, keepdims=True))
        dsc_ref[...] += jnp.sum(dhv * xn, axis=0, keepdims=True)
        dsh_ref[...] += jnp.sum(dhv, axis=0, keepdims=True)

    return _call_with_exchange(
        body, side, lambda: pl.program_id(0) == 0, lambda: pl.program_id(0) == s // t - 1, (duv, wgu, x, sc, dxo),
        name=name, grid=(s // t,),
        in_specs=[_row_spec(t, f2), _resident(wgu.shape), _row_spec(t, d), _vec_spec(d), _row_spec(t, d)],
        out_specs=[_row_spec(t, d), _vec_spec(d), _vec_spec(d)],
        out_shape=[jax.ShapeDtypeStruct((s, d), F32), jax.ShapeDtypeStruct((1, d), F32), jax.ShapeDtypeStruct((1, d), F32)],
        compiler_params=_cparams("arbitrary"))


def _dh_pieces(pieces, w, x, sc, dxo, *, name):
    s, d = x.shape
    t = _rows(s, FFN_TM)
    widths = [p.shape[1] for p in pieces]
    offs = [sum(widths[:i]) for i in range(len(widths))]
    kd = sum(widths)
    npc = len(pieces)

    def body(*refs):
        p_refs = refs[:npc]
        w_ref, x_ref, sc_ref, dxo_ref, dx_ref, dsc_ref, dsh_ref, cat_ref = refs[npc:]

        @pl.when(pl.program_id(0) == 0)
        def _():
            dsc_ref[...] = jnp.zeros_like(dsc_ref)
            dsh_ref[...] = jnp.zeros_like(dsh_ref)

        dhv = None
        for p_ref, off, wd in zip(p_refs, offs, widths):
            pb = p_ref[...].astype(BF16)
            cat_ref[:, off:off + wd] = pb
            part = _dot(pb, w_ref[:, off:off + wd], NT)
            dhv = part if dhv is None else dhv + part
        xv = x_ref[...]
        r = lax.rsqrt(jnp.mean(xv * xv, axis=1, keepdims=True) + EPS)
        xn = xv * r
        dxn = dhv * (1.0 + sc_ref[...])
        dx_ref[...] = dxo_ref[...] + r * (dxn - xn * jnp.mean(dxn * xn, axis=1, keepdims=True))
        dsc_ref[...] += jnp.sum(dhv * xn, axis=0, keepdims=True)
        dsh_ref[...] += jnp.sum(dhv, axis=0, keepdims=True)

    return _pcall(
        body, name=name, grid=(s // t,),
        in_specs=[_row_spec(t, wd) for wd in widths] + [_resident(w.shape), _row_spec(t, d), _vec_spec(d), _row_spec(t, d)],
        out_specs=[_row_spec(t, d), _vec_spec(d), _vec_spec(d), _row_spec(t, kd)],
        out_shape=[jax.ShapeDtypeStruct((s, d), F32), jax.ShapeDtypeStruct((1, d), F32), jax.ShapeDtypeStruct((1, d), F32),
                   jax.ShapeDtypeStruct((s, kd), BF16)],
        compiler_params=_cparams("arbitrary"))(*pieces, w, x, sc, dxo)


def _ffn_fwd(x, sh, sc, g, wgu, wd, tag, side=None):
    (h, uv, a), got = _ffn_up(x, sc, sh, wgu, name=f"{tag}_up", side=side)
    y, xo = _mm(a, wd, name=f"{tag}_down", tm=512, tn=1024, tk=wd.shape[0], res=x, scale=0.5 * g)
    return xo, (x, h, uv, a, y), got


def _ffn_bwd(dxo, saved, sc, g, wgu, wd, tag, ride=None):
    x, h, uv, a, y = saved
    if ride is None:
        (dyb, duv, dgs), _ = _ffn_dact(dxo, y, 0.5 * g, wd, uv, name=f"{tag}_dact")
        (dx, dsc, dsh), _ = _ffn_dh(duv, wgu, x, sc, dxo, name=f"{tag}_dh")
        dwgu = _mm_tn(h, duv, name=f"{tag}_dwgu", tm=1024, tn=1408, tk=512)
        dwd = _mm_tn(a, dyb, name=f"{tag}_dwd", tm=1408, tn=1024, tk=512)
        return dx, dwgu, dwd, dsh, dsc, 0.5 * dgs, {}
    (dyb, duv, dgs), got = _ffn_dact(dxo, y, 0.5 * g, wd, uv, name=f"{tag}_dact", side=_Xchg([ride[1]], "chips", True))
    part = _sum_slots(got[0], name=f"{tag}_sum_a", out_dtype=BF16)
    dwd, swap_a = _mm_tn(a, dyb, name=f"{tag}_dwd", tm=1408, tn=1024, tk=512, side=_Xchg([part], "sib", False))
    (dx, dsc, dsh), got = _ffn_dh(duv, wgu, x, sc, dxo, name=f"{tag}_dh", side=_Xchg([_by_chip("ffn1_w_down", dwd)], "chips", True))
    part = _sum_slots(got[0], name=f"{tag}_sum_b", out_dtype=BF16)
    dwgu, swap_b = _mm_tn(h, duv, name=f"{tag}_dwgu", tm=1024, tn=1408, tk=512, side=_Xchg([part], "sib", False))
    return dx, dwgu, dwd, dsh, dsc, 0.5 * dgs, {ride[0]: swap_a[0], "ffn1_w_down": swap_b[0]}


def _layer_fwd(x0, mod, w, par, tag, carry=None, own=None):
    sh1, sc1, g1, sh2, sc2, g2, sh3, sc3, g3 = (mod[i:i + 1] for i in range(N_MOD))
    x1, f1, got = _ffn_fwd(x0, sh1, sc1, g1, w["gu1"], w["d1"], f"{tag}_ffn1",
                           side=None if own is None else _Xchg(own[0], "chips", False))
    if own is not None:
        w = {**w, **own[1](got)}
    h2, proj, qkv = _norm_mm(x1, sc2, sh2, w["in"], 3 * SB_W, name=f"{tag}_in")
    o_a, got_a = _sb_fwd(qkv, name=f"{tag}_sb", side=None if carry is None else _Xchg(carry[0], "chips", False))
    qd, kd = _dil_prep(proj, par["gq"], par["gk"], par["cos"], par["sin"], name=f"{tag}_dil_prep")
    nums, dens, mxs = [], [], []
    for _, r in DIL_PATTERNS:
        nu, de, mx = _dil_fwd(qd, kd, proj, r, name=f"{tag}_dil{r}")
        nums.append(nu)
        dens.append(de)
        mxs.append(mx)
    (oh, st), got_b = _hgrn_fwd(proj, par["la"], par["lc"], name=f"{tag}_hgrn",
                                side=None if carry is None else _Xchg(carry[1], "chips", False))
    ymix, od, mall, zall = _mix_out(o_a, nums, dens, mxs, oh, proj, par["hg"], name=f"{tag}_mix_out")
    out, x2 = _mm(ymix, w["out"], name=f"{tag}_out", tm=512, tn=1024, tk=1024, res=x1, scale=g2)
    x3, f2, _ = _ffn_fwd(x2, sh3, sc3, g3, w["gu2"], w["d2"], f"{tag}_ffn2")
    return x3, dict(f1=f1, f2=f2, x1=x1, h2=h2, proj=proj, qkv=qkv, o_a=o_a, qd=qd, kd=kd, oh=oh, st=st,
                    ymix=ymix, od=od, mall=mall, zall=zall, out=out), (None if carry is None else (got_a, got_b)), w


COL_SHARDED = ("ffn1_w_gate", "ffn1_w_up", "w_in", "ffn2_w_gate", "ffn2_w_up")
ROW_SHARDED = ("ffn1_w_down", "w_out", "ffn2_w_down")


def _by_chip(name, g):
    if name in COL_SHARDED:
        return jnp.moveaxis(g.reshape(g.shape[0], 4, -1), 1, 0)
    return g.reshape(4, -1, g.shape[1])


def _halves(x):
    f = x.shape[1] // 2
    return x[:, :f], x[:, f:]


def _layer_bwd(dx3, sv, mod, w, par, tag, pending):
    sh1, sc1, g1, sh2, sc2, g2, sh3, sc3, g3 = (mod[i:i + 1] for i in range(N_MOD))
    dx2, dwgu2, dwd2, dsh3, dsc3, dg3, _ = _ffn_bwd(dx3, sv["f2"], sc3, g3, w["gu2"], w["d2"], f"{tag}_ffn2")
    doutb, dg2 = _gate_bwd(dx2, sv["out"], g2, name=f"{tag}_dgate2")
    dmix = _mm(doutb, w["out"], name=f"{tag}_dmix", tb=True, tm=512, tn=1024, tk=1024)
    dwout = _mm_tn(sv["ymix"], doutb, name=f"{tag}_dwout", tm=1024, tn=1024, tk=512)
    proj = sv["proj"]
    doh, dgh, delta, dhg = _mix_out_bwd(dmix, sv["oh"], proj, par["hg"], sv["od"], name=f"{tag}_dmix_out")
    dgate2, dup2 = _halves(dwgu2)
    ride = dict(pending)
    for n, g in (("ffn2_w_gate", dgate2), ("ffn2_w_up", dup2), ("ffn2_w_down", dwd2), ("w_out", dwout)):
        ride[(tag, n)] = _by_chip(n, g)
    keys = list(ride)
    (dqa, dka, dva), got = _sb_bwd(sv["qkv"], sv["o_a"], dmix, name=f"{tag}_dsb", side=_Xchg([ride[k] for k in keys], "chips", True))
    parts = [_sum_slots(g, name=f"{tag}_sum{i}", out_dtype=BF16) for i, g in enumerate(got)]
    dqs, dks, dvs = [], [], []
    for _, r in DIL_PATTERNS:
        a, b, c = _dil_bwd(sv["qd"], sv["kd"], proj, dmix, sv["mall"], sv["zall"], delta, r, name=f"{tag}_ddil{r}")
        dqs.append(a)
        dks.append(b)
        dvs.append(c)
    dqd, dkd, dvd, dgq, dgk = _dil_prep_bwd(proj, par["gq"], par["gk"], par["cos"], par["sin"], dqs, dks, dvs,
                                             name=f"{tag}_ddil_prep")
    (dqh, dfh, dih, dla, dlc), swapped = _hgrn_bwd(proj, par["la"], par["lc"], sv["st"], doh, name=f"{tag}_dhgrn",
                                                   side=_Xchg(parts, "sib", False))
    dx1, dsc2, dsh2, dproj = _dh_pieces([dqa, dka, dva, dqd, dkd, dvd, dqh, dfh, dih, dgh], w["in"], sv["x1"], sc2, dx2,
                                         name=f"{tag}_dh2")
    dwin = _mm_tn(sv["h2"], dproj, name=f"{tag}_dwin", tm=1024, tn=1792, tk=512)
    dx0, dwgu1, _, dsh1, dsc1, dg1, rode = _ffn_bwd(dx1, sv["f1"], sc1, g1, w["gu1"], w["d1"], f"{tag}_ffn1",
                                                    ride=("w_in", _by_chip("w_in", dwin)))
    dmod = jnp.concatenate([dsh1, dsc1, dg1, dsh2, dsc2, dg2, dsh3, dsc3, dg3], axis=0)
    fold = lambda v: v.reshape(2, HEAD_DIM).sum(axis=0)
    dgate1, dup1 = _halves(dwgu1)
    late = {(tag, n): _by_chip(n, g) for n, g in (("ffn1_w_gate", dgate1), ("ffn1_w_up", dup1))}
    grads = dict(late=late, dmod=dmod, gq=fold(dgq), gk=fold(dgk), hg=dhg[0], la=dla[0], lc=dlc[0])
    return dx0, grads, {**dict(zip(keys, swapped)), **{(tag, n): v for n, v in rode.items()}}


def _pack(pieces):
    flat = jnp.concatenate([p.reshape(-1) for p in pieces])
    pad = (-flat.shape[0]) % (8 * LANES)
    return jnp.pad(flat, (0, pad)).reshape(-1, LANES)


def _unpack(flat, like):
    out, off = [], 0
    for p in like:
        out.append(flat[off:off + p.size].reshape(p.shape))
        off += p.size
    return out


def kernel(x, c, w_mod, b_mod, ffn1_w_gate, ffn1_w_up, ffn1_w_down, w_in, w_out, q_norm_g, k_norm_g, hgrn_norm_g, hgrn_lb_logits, ffn2_w_gate, ffn2_w_up, ffn2_w_down, loss_target, m_w_mod, m_b_mod, m_ffn1_w_gate, m_ffn1_w_up, m_ffn1_w_down, m_w_in, m_w_out, m_q_norm_g, m_k_norm_g, m_hgrn_norm_g, m_hgrn_lb_logits, m_ffn2_w_gate, m_ffn2_w_up, m_ffn2_w_down, v_w_mod, v_b_mod, v_ffn1_w_gate, v_ffn1_w_up, v_ffn1_w_down, v_w_in, v_w_out, v_q_norm_g, v_k_norm_g, v_hgrn_norm_g, v_hgrn_lb_logits, v_ffn2_w_gate, v_ffn2_w_up, v_ffn2_w_down):
    names = ["w_mod", "b_mod", "ffn1_w_gate", "ffn1_w_up", "ffn1_w_down", "w_in", "w_out", "q_norm_g", "k_norm_g",
             "hgrn_norm_g", "hgrn_lb_logits", "ffn2_w_gate", "ffn2_w_up", "ffn2_w_down"]
    wts = dict(zip(names, (w_mod, b_mod, ffn1_w_gate, ffn1_w_up, ffn1_w_down, w_in, w_out, q_norm_g, k_norm_g, hgrn_norm_g,
                           hgrn_lb_logits, ffn2_w_gate, ffn2_w_up, ffn2_w_down)))
    mom = dict(zip(names, (m_w_mod, m_b_mod, m_ffn1_w_gate, m_ffn1_w_up, m_ffn1_w_down, m_w_in, m_w_out, m_q_norm_g, m_k_norm_g,
                           m_hgrn_norm_g, m_hgrn_lb_logits, m_ffn2_w_gate, m_ffn2_w_up, m_ffn2_w_down)))
    var = dict(zip(names, (v_w_mod, v_b_mod, v_ffn1_w_gate, v_ffn1_w_up, v_ffn1_w_down, v_w_in, v_w_out, v_q_norm_g, v_k_norm_g,
                           v_hgrn_norm_g, v_hgrn_lb_logits, v_ffn2_w_gate, v_ffn2_w_up, v_ffn2_w_down)))
    depth = w_mod.shape[0]
    assert depth == 2 and x.shape[0] == 1
    s, d = x.shape[1:]
    assert s % (DIL_PATTERNS[-1][1] * QBLK) == 0 and d % LANES == 0
    xi, yi, ci = lax.axis_index("x"), lax.axis_index("y"), lax.axis_index("c")
    chip = 2 * xi + yi
    dev = 2 * chip + ci
    x0, tgt = x[0], loss_target[0]

    c8 = _exchange([c.reshape(d // LANES, LANES)], "all", False, name="gather_c")[0].reshape(8, d)
    ncol = w_mod.shape[2]
    b_loc = lax.dynamic_slice_in_dim(b_mod, chip * ncol, ncol, axis=1)
    m_loc = _mod_fwd(c8, w_mod, b_loc.reshape(depth, 1, ncol), name="mod_fwd")
    m_all = _exchange([m_loc], "chips", False, name="gather_mod")[0]
    mod = jnp.transpose(lax.dynamic_index_in_dim(m_all, dev, axis=2, keepdims=False), (1, 0, 2)).reshape(depth, N_MOD, d)

    col_sharded = ["ffn1_w_gate", "ffn1_w_up", "w_in", "ffn2_w_gate", "ffn2_w_up"]
    row_sharded = ["ffn1_w_down", "w_out", "ffn2_w_down"]
    big = col_sharded + row_sharded
    early = ["ffn1_w_gate", "ffn1_w_up", "ffn1_w_down", "w_in", "w_out"]
    late = ["ffn2_w_gate", "ffn2_w_up", "ffn2_w_down"]

    def shards(l, group):
        return [wts[n][l].astype(BF16) for n in group]

    def whole(group, gathered):
        out = {}
        for n, g in zip(group, gathered):
            out[n] = jnp.moveaxis(g, 0, 1).reshape(g.shape[1], -1) if n in col_sharded else g.reshape(-1, g.shape[2])
        return out

    def early_weights(gathered):
        full = whole(early, gathered)
        return {"gu1": jnp.concatenate([full["ffn1_w_gate"], full["ffn1_w_up"]], axis=1), "d1": full["ffn1_w_down"],
                "in": full["w_in"], "out": full["w_out"]}

    def late_weights(gathered):
        full = whole(late, gathered)
        return {"gu2": jnp.concatenate([full["ffn2_w_gate"], full["ffn2_w_up"]], axis=1), "d2": full["ffn2_w_down"]}

    ws = [early_weights(_exchange(shards(0, early), "chips", False, name="gather_w"))]

    _, la, lc = _lb_prep(hgrn_lb_logits, name="lb_prep")
    cos, sin = _rope_tables(s)
    pars = [dict(gq=jnp.tile(q_norm_g[l], 2)[None], gk=jnp.tile(k_norm_g[l], 2)[None], hg=hgrn_norm_g[l][None],
                 la=la[l:l + 1], lc=lc[l:l + 1], cos=cos, sin=sin) for l in range(depth)]

    xs, saved = x0, []
    for l in range(depth):
        carry = (shards(l + 1, early), shards(l + 1, late)) if l + 1 < depth else None
        own = (shards(l, late), late_weights) if l == 0 else None
        xs, sv, got, ws[l] = _layer_fwd(xs, mod[l], ws[l], pars[l], f"l{l}", carry, own)
        saved.append(sv)
        if got is not None:
            ws.append({**early_weights(got[0]), **late_weights(got[1])})
    dx, lpart = _loss_grad(xs, tgt, name="loss")

    grads, halves, pending = [None] * depth, {}, {}
    for l in reversed(range(depth)):
        dx, grads[l], swapped = _layer_bwd(dx, saved[l], mod[l], ws[l], pars[l], f"l{l}", pending)
        halves.update(swapped)
        pending = grads[l]["late"]
    keys = list(pending)
    got = _exchange([pending[k] for k in keys], "chips", True, name="scatter_grads")
    parts = [_sum_slots(g, name=f"sum_{k[1]}", out_dtype=BF16) for k, g in zip(keys, got)]
    halves.update(zip(keys, _exchange(parts, "sib", False, name="swap_grads")))

    stack = lambda k: jnp.stack([grads[l][k] for l in range(depth)])
    small = [stack("dmod"), stack("gq"), stack("gk"), stack("hg"), stack("la"), stack("lc"), lpart[0, :1]]
    packed = _pack(small)
    allp = _exchange([packed], "all", False, name="gather_small")[0]
    tot = _unpack(_sum_slots(allp, name="sum_small").reshape(-1), small)
    g_b_mod = tot[0].reshape(depth, N_MOD * d)
    loss = tot[6][0]
    g_small = {"b_mod": g_b_mod, "q_norm_g": tot[1], "k_norm_g": tot[2], "hgrn_norm_g": tot[3],
               "hgrn_lb_logits": _lb_bwd(hgrn_lb_logits, tot[4], tot[5], name="lb_bwd")}

    dm_all = allp.reshape(8, -1)[:, :depth * N_MOD * d].reshape(8, depth, N_MOD * d)
    dm_loc = jnp.transpose(lax.dynamic_slice_in_dim(dm_all, chip * ncol, ncol, axis=2), (1, 0, 2))
    g_w_mod = _mod_bwd(c8.T, dm_loc, name="mod_bwd")

    outs = {}
    for n in names:
        w2 = wts[n].reshape(-1, wts[n].shape[-1])
        m2, v2 = mom[n].reshape(w2.shape), var[n].reshape(w2.shape)
        if n in big:
            res = _adamw_layers(w2, [halves[(f"l{l}", n)] for l in range(depth)], m2, v2, name=f"adamw_{n}")
        else:
            g = g_w_mod if n == "w_mod" else g_small[n]
            res = _adamw(w2, [g.reshape(w2.shape)], m2, v2, name=f"adamw_{n}")
        outs[n] = [r.reshape(wts[n].shape) for r in res]
    return (loss, dx[None], *[outs[n][0] for n in names], *[outs[n][1] for n in names], *[outs[n][2] for n in names],
            *[outs[n][3] for n in names])
```

```python
import jax
import jax.numpy as jnp
from jax import lax
from jax.experimental import pallas as pl
from jax.experimental.pallas import tpu as pltpu

F32 = jnp.float32
BF16 = jnp.bfloat16
MESH_ID = pl.DeviceIdType.MESH

HEAD_DIM = 64
SB_W = 256
DIL_W = 256
HG_W = 512
HG_D = 128
IN_W = 3 * SB_W + 3 * DIL_W + 4 * HG_W
MIX_W = SB_W + DIL_W + HG_W
DIL_PATTERNS = ((128, 1), (512, 4), (2048, 16))
ROPE_THETA = 10000.0
EPS = 1e-6
LB_FLOOR = 1e-30
NEG_BIG = -1e30
N_MOD = 9
ADAM_LR = 0.001
ADAM_B1 = 0.9
ADAM_B2 = 0.999
ADAM_EPS = 1e-08
ADAM_WD = 0.01
ADAM_STEP = 10

LANES = 128
QBLK = 128
DIL_TILE = 1024
HG_BLK = 16
HG_TILE = 256
SB_EXIT = 88.0
VMEM_LIMIT = 48 * 1024 * 1024

NN = (((1,), (0,)), ((), ()))
NT = (((1,), (1,)), ((), ()))
TN = (((0,), (0,)), ((), ()))


def _pcall(body, **kw):
    return pl.pallas_call(body, **kw)


def _cparams(*sem):
    return pltpu.CompilerParams(dimension_semantics=sem if sem else None, vmem_limit_bytes=VMEM_LIMIT)


def _dot(a, b, dims=NN):
    return lax.dot_general(a, b, dims, preferred_element_type=F32)


def _split(x, n):
    parts = []
    r = x
    for i in range(n):
        p = r.astype(BF16)
        parts.append(p)
        if i + 1 < n:
            r = r - p.astype(F32)
    return parts


def _xdot(x, m, n=2):
    return sum(_dot(p, m) for p in _split(x, n))


def _iota(shape, dim):
    return lax.broadcasted_iota(jnp.int32, shape, dim)


def _sigmoid(x):
    return 1.0 / (1.0 + jnp.exp(-x))


def _tile(dim, pref, mult=LANES):
    t = (min(pref, dim) // mult) * mult
    while t >= mult:
        if dim % t == 0:
            return t
        t -= mult
    return dim


def _rows(dim, pref):
    return _tile(dim, pref, 8)


def _mm(a, b, *, name, tb=False, tm=512, tn=1024, tk=1024, out_dtype=F32, res=None, scale=None):
    m, kd = a.shape
    n = b.shape[0] if tb else b.shape[1]
    tm, tn, tk = _rows(m, tm), _tile(n, tn), _tile(kd, tk)
    nk = kd // tk
    epi = res is not None

    def body(*refs):
        if epi:
            a_ref, b_ref, r_ref, s_ref, o_ref, x_ref, acc = refs
        else:
            a_ref, b_ref, o_ref, acc = refs
        k = pl.program_id(2)

        @pl.when(k == 0)
        def _():
            acc[...] = jnp.zeros_like(acc)

        acc[...] += _dot(a_ref[...], b_ref[...], NT if tb else NN)

        @pl.when(k == nk - 1)
        def _():
            o_ref[...] = acc[...].astype(o_ref.dtype)
            if epi:
                x_ref[...] = r_ref[...] + s_ref[...] * acc[...]

    in_specs = [
        pl.BlockSpec((tm, tk), lambda i, j, k: (i, k)),
        pl.BlockSpec((tn, tk), lambda i, j, k: (j, k)) if tb else pl.BlockSpec((tk, tn), lambda i, j, k: (k, j)),
    ]
    out_shape = [jax.ShapeDtypeStruct((m, n), out_dtype)]
    out_specs = [pl.BlockSpec((tm, tn), lambda i, j, k: (i, j))]
    args = [a, b]
    if epi:
        in_specs += [pl.BlockSpec((tm, tn), lambda i, j, k: (i, j)), pl.BlockSpec((1, tn), lambda i, j, k: (0, j))]
        out_shape.append(jax.ShapeDtypeStruct((m, n), F32))
        out_specs.append(pl.BlockSpec((tm, tn), lambda i, j, k: (i, j)))
        args += [res, scale]
    out = _pcall(
        body, name=name, grid=(m // tm, n // tn, nk), in_specs=in_specs, out_specs=out_specs, out_shape=out_shape,
        scratch_shapes=[pltpu.VMEM((tm, tn), F32)], compiler_params=_cparams("parallel", "parallel", "arbitrary"),
    )(*args)
    return out if epi else out[0]


def _mm_tn(a, b, *, name, tm=1024, tn=1408, tk=512, out_dtype=BF16, side=None):
    s, m = a.shape
    n = b.shape[1]
    tm, tn, tk = _tile(m, tm), _tile(n, tn), _rows(s, tk)
    nk = s // tk
    ni, nj = m // tm, n // tn

    def body(a_ref, b_ref, o_ref, acc):
        k = pl.program_id(2)

        @pl.when(k == 0)
        def _():
            acc[...] = jnp.zeros_like(acc)

        acc[...] += _dot(a_ref[...], b_ref[...], TN)

        @pl.when(k == nk - 1)
        def _():
            o_ref[...] = acc[...].astype(o_ref.dtype)

    at = lambda i, j, k: (pl.program_id(0) == i) & (pl.program_id(1) == j) & (pl.program_id(2) == k)
    (out,), got = _call_with_exchange(
        body, side, lambda: at(0, 0, 0), lambda: at(ni - 1, nj - 1, nk - 1), (a, b), name=name, grid=(ni, nj, nk),
        in_specs=[pl.BlockSpec((tk, tm), lambda i, j, k: (k, i)), pl.BlockSpec((tk, tn), lambda i, j, k: (k, j))],
        out_specs=[pl.BlockSpec((tm, tn), lambda i, j, k: (i, j))], out_shape=[jax.ShapeDtypeStruct((m, n), out_dtype)],
        scratch_shapes=[pltpu.VMEM((tm, tn), F32)],
        compiler_params=_cparams(*(("parallel", "parallel", "arbitrary") if side is None else ("arbitrary",) * 3)))
    return out if side is None else (out, got)


TE = 512


def _row_spec(t, w, col=0):
    return pl.BlockSpec((t, w), lambda i, col=col: (i, col))


def _vec_spec(w, col=0):
    return pl.BlockSpec((1, w), lambda i, col=col: (0, col))


def _gate_bwd(dxo, y, sg, *, name):
    s, d = y.shape
    t = _rows(s, TE)

    def body(dxo_ref, y_ref, sg_ref, dy_ref, ds_ref):
        @pl.when(pl.program_id(0) == 0)
        def _():
            ds_ref[...] = jnp.zeros_like(ds_ref)

        dv = dxo_ref[...]
        dy_ref[...] = (sg_ref[...] * dv).astype(BF16)
        ds_ref[...] += jnp.sum(dv * y_ref[...].astype(F32), axis=0, keepdims=True)

    return _pcall(
        body, name=name, grid=(s // t,), in_specs=[_row_spec(t, d), _row_spec(t, d), _vec_spec(d)],
        out_specs=[_row_spec(t, d), _vec_spec(d)],
        out_shape=[jax.ShapeDtypeStruct((s, d), BF16), jax.ShapeDtypeStruct((1, d), F32)],
        compiler_params=_cparams("arbitrary"))(dxo, y, sg)


def _loss_grad(y, tgt, *, name):
    s, d = y.shape
    t = _rows(s, TE)
    nt = s // t

    def body(y_ref, t_ref, dy_ref, l_ref, acc):
        i = pl.program_id(0)

        @pl.when(i == 0)
        def _():
            acc[...] = jnp.zeros_like(acc)

        e = y_ref[...] - t_ref[...]
        dy_ref[...] = e * (1.0 / d)
        acc[...] += jnp.sum(e * e, axis=0, keepdims=True)

        @pl.when(i == nt - 1)
        def _():
            l_ref[...] = jnp.broadcast_to(jnp.sum(acc[...], axis=1, keepdims=True) * (0.5 / d), l_ref.shape)

    return _pcall(
        body, name=name, grid=(nt,), in_specs=[_row_spec(t, d), _row_spec(t, d)],
        out_specs=[_row_spec(t, d), pl.BlockSpec((1, LANES), lambda i: (0, 0))],
        out_shape=[jax.ShapeDtypeStruct((s, d), F32), jax.ShapeDtypeStruct((1, LANES), F32)],
        scratch_shapes=[pltpu.VMEM((1, d), F32)], compiler_params=_cparams("arbitrary"))(y, tgt)


SB_TQ = 256
SB_NK = SB_TQ // QBLK


def _sb_consts():
    r = _iota((QBLK, LANES), 0)
    c = _iota((QBLK, LANES), 1)
    ones = jnp.ones((QBLK, LANES), BF16)
    after = jnp.concatenate([jnp.where(r > c, 1.0, 0.0).astype(BF16), ones], axis=1)
    from_ = jnp.concatenate([jnp.where(r >= c, 1.0, 0.0).astype(BF16), ones], axis=1)
    return _iota((SB_TQ, LANES), 0), _iota((SB_TQ, LANES), 1), after, from_


def _sb_scores(qm, kb, strict):
    z = _dot(qm, kb, NT) * (HEAD_DIM ** -0.5)
    sp = jnp.log(1.0 + jnp.exp(-jnp.abs(z)))
    lnb = -(jnp.maximum(z, 0.0) + sp)
    lb = jnp.minimum(z, 0.0) - sp
    if strict is not None:
        lnb = jnp.where(strict, lnb, 0.0)
    return lnb, lb


def _sb_fwd(qkv, *, name, side=None):
    s = qkv.shape[0]
    nq = s // SB_TQ

    def body(q_ref, k_ref, v_ref, o_ref, *scr):
        acc, osc = scr[:4], scr[4:]
        qi = pl.program_id(0)
        row, lane, after, _ = _sb_consts()
        h0 = lane < HEAD_DIM
        q = q_ref[...]
        qms = []
        for p in range(2):
            qp = q[:, p * LANES:(p + 1) * LANES]
            qms += [jnp.where(h0, qp, jnp.zeros_like(qp)), jnp.where(h0, jnp.zeros_like(qp), qp)]

        def block(kj, mask, r0=0):
            rows = slice(r0, SB_TQ)
            off = pl.multiple_of(kj * QBLK, QBLK)
            kbs = [k_ref[pl.ds(off, QBLK), p * LANES:(p + 1) * LANES] for p in range(2)]
            vbs = [v_ref[pl.ds(off, QBLK), p * LANES:(p + 1) * LANES] for p in range(2)]
            mask = None if mask is None else mask[rows]
            sc = [_sb_scores(qms[c][rows], kbs[c // 2], mask) for c in range(4)]
            trs = [_xdot(sc[c][0], after) for c in range(4)]
            top = None
            for c in range(4):
                w = jnp.exp(sc[c][1] + trs[c][:, :QBLK] + acc[c][rows, :])
                if mask is not None:
                    w = jnp.where(mask, w, 0.0)
                osc[c][rows, :] += _xdot(w, vbs[c // 2])
                new = acc[c][rows, :] + trs[c][:, QBLK:]
                acc[c][rows, :] = new
                top = new if top is None else jnp.maximum(top, new)
            return jnp.max(top)

        for ref in scr:
            ref[...] = jnp.zeros_like(ref)
        top = None
        for j in reversed(range(SB_NK)):
            top = block(qi * SB_NK + j, (lane + j * QBLK) < row, j * QBLK)
        lax.while_loop(lambda c: (c[0] >= 0) & (c[1] > -SB_EXIT), lambda c: (c[0] - 1, block(c[0], None)),
                       (qi * SB_NK - 1, top))
        for p in range(2):
            o_ref[:, p * LANES:(p + 1) * LANES] = jnp.where(h0, osc[2 * p][...], osc[2 * p + 1][...])

    (o,), got = _call_with_exchange(
        body, side, lambda: pl.program_id(0) == 0, lambda: pl.program_id(0) == nq - 1, (qkv, qkv, qkv), name=name, grid=(nq,),
        in_specs=[pl.BlockSpec((SB_TQ, SB_W), lambda i: (i, 0)),
                  pl.BlockSpec((s, SB_W), lambda i: (0, 1)),
                  pl.BlockSpec((s, SB_W), lambda i: (0, 2))],
        out_specs=[pl.BlockSpec((SB_TQ, SB_W), lambda i: (i, 0))],
        out_shape=[jax.ShapeDtypeStruct((s, SB_W), F32)],
        scratch_shapes=[pltpu.VMEM((SB_TQ, LANES), F32)] * 8,
        compiler_params=_cparams("arbitrary"))
    return o, got


def _sb_bwd(qkv, o, dmix, *, name, side=None):
    s = qkv.shape[0]
    nq = s // SB_TQ
    scale = HEAD_DIM ** -0.5

    def body(q_ref, k_ref, v_ref, o_ref, do_ref, dq_ref, dk_ref, dv_ref, a0, a1, r0, r1, dqs, dks, dvs):
        acc, racc = (a0, a1), (r0, r1)
        i = pl.program_id(1)
        qi = nq - 1 - i
        row, lane, after, from_ = _sb_consts()
        klane = _iota((QBLK, LANES), 1)
        khms = (klane < HEAD_DIM, klane >= HEAD_DIM)

        @pl.when(i == 0)
        def _():
            dks[...] = jnp.zeros_like(dks)
            dvs[...] = jnp.zeros_like(dvs)

        q = q_ref[...]
        do = do_ref[...]
        dob = do.astype(BF16)
        dd = do * o_ref[...]
        dol = (do - dob.astype(F32)).astype(BF16)
        zero = jnp.zeros_like(q)
        hms = (lane < HEAD_DIM, lane >= HEAD_DIM)
        qms = [jnp.where(hm, q, zero) for hm in hms]
        doms = [jnp.where(hm, dob, zero) for hm in hms]
        dols = [jnp.where(hm, dol, zero) for hm in hms]
        dsums = [jnp.sum(jnp.where(hm, dd, 0.0), axis=1, keepdims=True) for hm in hms]

        def block(kj, mask, start=0):
            rows = slice(start, SB_TQ)
            off = pl.multiple_of(kj * QBLK, QBLK)
            kb = k_ref[pl.ds(off, QBLK), :]
            vb = v_ref[pl.ds(off, QBLK), :]
            mask = None if mask is None else mask[rows]
            top, dq, dk, dv = None, None, None, None
            sc = [_sb_scores(qms[h][rows], kb, mask) for h in range(2)]
            trs = [_xdot(sc[h][0], after) for h in range(2)]
            dws = [_dot(doms[h][rows], vb, NT) + _dot(dols[h][rows], vb, NT) for h in range(2)]
            for h in range(2):
                lb, tr = sc[h][1], trs[h]
                w = jnp.exp(lb + tr[:, :QBLK] + acc[h][rows, :])
                if mask is not None:
                    w = jnp.where(mask, w, 0.0)
                g = w * dws[h]
                tg = _xdot(g, from_)
                before = dsums[h][rows] - (tg[:, :QBLK] + racc[h][rows, :])
                dz = g - jnp.exp(lb) * (g + before)
                if mask is not None:
                    dz = jnp.where(mask, dz, 0.0)
                dzb = (dz * scale).astype(BF16)
                dqh = _dot(dzb, jnp.where(khms[h], kb, jnp.zeros_like(kb)))
                dkh = _dot(dzb, qms[h][rows], TN)
                dvh = _dot(w.astype(BF16), doms[h][rows], TN)
                dq, dk, dv = (dqh, dkh, dvh) if h == 0 else (dq + dqh, dk + dkh, dv + dvh)
                new = acc[h][rows, :] + tr[:, QBLK:]
                acc[h][rows, :] = new
                racc[h][rows, :] += tg[:, QBLK:]
                top = new if top is None else jnp.maximum(top, new)
            dqs[rows, :] += dq
            dks[pl.ds(off, QBLK), :] += dk
            dvs[pl.ds(off, QBLK), :] += dv
            return jnp.max(top)

        for ref in (dqs, a0, a1, r0, r1):
            ref[...] = jnp.zeros_like(ref)
        top = None
        for j in reversed(range(SB_NK)):
            top = block(qi * SB_NK + j, (lane + j * QBLK) < row, j * QBLK)
        lax.while_loop(lambda c: (c[0] >= 0) & (c[1] > -SB_EXIT), lambda c: (c[0] - 1, block(c[0], None)),
                       (qi * SB_NK - 1, top))
        dq_ref[...] = dqs[...]
        fin = pl.multiple_of(qi * SB_TQ, SB_TQ)
        dk_ref[...] = dks[pl.ds(fin, SB_TQ), :]
        dv_ref[...] = dvs[pl.ds(fin, SB_TQ), :]

    blk = lambda c0: pl.BlockSpec((SB_TQ, LANES), lambda p, i, c0=c0: (nq - 1 - i, c0 + p))
    return _call_with_exchange(
        body, side, lambda: (pl.program_id(0) == 0) & (pl.program_id(1) == 0),
        lambda: (pl.program_id(0) == 1) & (pl.program_id(1) == nq - 1), (qkv, qkv, qkv, o, dmix), name=name, grid=(2, nq),
        in_specs=[blk(0), pl.BlockSpec((s, LANES), lambda p, i: (0, 2 + p)), pl.BlockSpec((s, LANES), lambda p, i: (0, 4 + p)),
                  blk(0), blk(0)],
        out_specs=[blk(0), blk(0), blk(0)],
        out_shape=[jax.ShapeDtypeStruct((s, SB_W), F32)] * 3,
        scratch_shapes=[pltpu.VMEM((SB_TQ, LANES), F32)] * 5 + [pltpu.VMEM((s, LANES), F32), pltpu.VMEM((s, LANES), F32)],
        compiler_params=_cparams("arbitrary", "arbitrary"))


def _seg_consts():
    r = _iota((LANES, LANES), 0)
    c = _iota((LANES, LANES), 1)
    return jnp.where((r >> 6) == (c >> 6), 1.0, 0.0).astype(BF16)


def _rot_half(x, lane):
    half = HEAD_DIM // 2
    return jnp.where((lane & (HEAD_DIM - 1)) < half, pltpu.roll(x, LANES - half, 1), pltpu.roll(x, half, 1))


def _rope_tables(s):
    half = HEAD_DIM // 2
    inv_freq = ROPE_THETA ** (-jnp.arange(half, dtype=F32) * 2.0 / HEAD_DIM)
    ang = jnp.arange(s, dtype=F32)[:, None] * inv_freq[None, :]
    cos, sin = jnp.cos(ang), jnp.sin(ang)
    return jnp.tile(jnp.concatenate([cos, cos], axis=1), (1, 2)), jnp.tile(jnp.concatenate([-sin, sin], axis=1), (1, 2))


def _dil_prep(proj, gq, gk, cos, sin, *, name):
    s = proj.shape[0]
    t = _rows(s, TE)
    c0 = 3 * SB_W // LANES

    def body(q_ref, k_ref, gq_ref, gk_ref, cos_ref, sin_ref, qo_ref, ko_ref):
        seg = _seg_consts()
        lane = _iota((t, LANES), 1)
        cs, sn = cos_ref[...], sin_ref[...]
        for x_ref, g_ref, o_ref, mul in ((q_ref, gq_ref, qo_ref, HEAD_DIM ** -0.5), (k_ref, gk_ref, ko_ref, 1.0)):
            for j in range(2):
                xv = x_ref[:, j * LANES:(j + 1) * LANES]
                ms = _xdot(xv * xv, seg, 3) * (1.0 / HEAD_DIM)
                xn = xv * lax.rsqrt(ms + EPS) * g_ref[...]
                o_ref[:, j * LANES:(j + 1) * LANES] = (xn * cs + _rot_half(xn, lane) * sn) * mul

    return _pcall(
        body, name=name, grid=(s // t,),
        in_specs=[pl.BlockSpec((t, DIL_W), lambda i: (i, c0 // 2)), pl.BlockSpec((t, DIL_W), lambda i: (i, c0 // 2 + 1)),
                  _vec_spec(LANES), _vec_spec(LANES), _row_spec(t, LANES), _row_spec(t, LANES)],
        out_specs=[_row_spec(t, DIL_W), _row_spec(t, DIL_W)],
        out_shape=[jax.ShapeDtypeStruct((s, DIL_W), F32)] * 2, compiler_params=_cparams("parallel"))(proj, proj, gq, gk, cos, sin)


def _dil_prep_bwd(proj, gq, gk, cos, sin, dqs, dks, dvs, *, name):
    s = proj.shape[0]
    t = _rows(s, TE)
    c0 = 3 * SB_W // LANES

    def body(q_ref, k_ref, gq_ref, gk_ref, cos_ref, sin_ref, a0, a1, a2, b0, b1, b2, c0_ref, c1_ref, c2_ref,
             dq_ref, dk_ref, dv_ref, dgq_ref, dgk_ref):
        @pl.when(pl.program_id(0) == 0)
        def _():
            dgq_ref[...] = jnp.zeros_like(dgq_ref)
            dgk_ref[...] = jnp.zeros_like(dgk_ref)

        dv_ref[...] = c0_ref[...] + c1_ref[...] + c2_ref[...]
        seg = _seg_consts()
        lane = _iota((t, LANES), 1)
        cs, sn = cos_ref[...], sin_ref[...]
        for x_ref, g_ref, parts, o_ref, dg_ref, mul in ((q_ref, gq_ref, (a0, a1, a2), dq_ref, dgq_ref, HEAD_DIM ** -0.5),
                                                          (k_ref, gk_ref, (b0, b1, b2), dk_ref, dgk_ref, 1.0)):
            for j in range(2):
                sl = slice(j * LANES, (j + 1) * LANES)
                dout = (parts[0][:, sl] + parts[1][:, sl] + parts[2][:, sl]) * mul
                dxn = dout * cs + _rot_half(dout * sn, lane)
                xv = x_ref[:, sl]
                r = lax.rsqrt(_xdot(xv * xv, seg, 3) * (1.0 / HEAD_DIM) + EPS)
                xh = xv * r
                dg_ref[...] += jnp.sum(dxn * xh, axis=0, keepdims=True)
                dxh = dxn * g_ref[...]
                o_ref[:, sl] = r * (dxh - xh * (_xdot(dxh * xh, seg, 3) * (1.0 / HEAD_DIM)))

    rs = _row_spec(t, DIL_W)
    return _pcall(
        body, name=name, grid=(s // t,),
        in_specs=[pl.BlockSpec((t, DIL_W), lambda i: (i, c0 // 2)), pl.BlockSpec((t, DIL_W), lambda i: (i, c0 // 2 + 1)),
                  _vec_spec(LANES), _vec_spec(LANES), _row_spec(t, LANES), _row_spec(t, LANES)] + [rs] * 9,
        out_specs=[rs, rs, rs, _vec_spec(LANES), _vec_spec(LANES)],
        out_shape=[jax.ShapeDtypeStruct((s, DIL_W), F32)] * 3 + [jax.ShapeDtypeStruct((1, LANES), F32)] * 2,
        compiler_params=_cparams("arbitrary"))(proj, proj, gq, gk, cos, sin, *dqs, *dks, *dvs)


def _dil_masks(n):
    row = _iota((QBLK, 2 * LANES), 0)
    col = _iota((QBLK, 2 * LANES), 1)
    return ((col < LANES) & (col >= row) & (n > 0)) | ((col >= LANES) & (col - LANES <= row))


DIL_V0 = (3 * SB_W + 2 * DIL_W) // LANES
DIL_DO0 = SB_W // LANES


def _dil_tiles(s, r):
    span = QBLK * r
    nsub = max(1, DIL_TILE // span)
    while s % (nsub * span):
        nsub -= 1
    return span, nsub


def _dil_rows(j, rho, span, r):
    return pl.ds(j * span + rho, QBLK, stride=r) if r > 1 else pl.ds(j * span, QBLK)


def _dil_fwd(q, k, proj, r, *, name):
    s = q.shape[0]
    span, nsub = _dil_tiles(s, r)
    tr = nsub * span

    def body(q_ref, kc_ref, kp_ref, vc_ref, vp_ref, num_ref, den_ref, mx_ref):
        n = pl.program_id(1)
        lane = _iota((QBLK, LANES), 1)
        h0 = lane < HEAD_DIM
        ones = jnp.ones((2 * QBLK, LANES), BF16)
        for j in range(nsub):
            valid = _dil_masks(n if j == 0 else 1)
            for rho in range(r):
                rows = _dil_rows(j, rho, span, r)
                before = _dil_rows(max(j - 1, 0), rho, span, r)
                k_prev, v_prev = (kp_ref, vp_ref) if j == 0 else (kc_ref, vc_ref)
                qv = q_ref[rows, :].astype(BF16)
                kk = jnp.concatenate([k_prev[before, :], kc_ref[rows, :]], axis=0).astype(BF16)
                vv = jnp.concatenate([jnp.concatenate([v_prev[before, :], vc_ref[rows, :]], axis=0).astype(BF16), ones], axis=1)
                res = []
                for h in range(2):
                    qm = jnp.where(h0 if h == 0 else ~h0, qv, jnp.zeros_like(qv))
                    sc = jnp.where(valid, _dot(qm, kk, NT), NEG_BIG)
                    mx = jnp.max(sc, axis=1, keepdims=True)
                    nd = _dot(jnp.exp(sc - mx).astype(BF16), vv)
                    res.append((nd[:, :LANES], nd[:, LANES:], mx))
                num_ref[rows, :] = jnp.where(h0, res[0][0], res[1][0])
                den_ref[rows, :] = jnp.where(h0, res[0][1], res[1][1])
                mx_ref[rows, :] = jnp.where(h0, res[0][2], res[1][2])

    cur = lambda c0: pl.BlockSpec((tr, LANES), lambda p, n, c0=c0: (n, c0 + p))
    prev = lambda c0: pl.BlockSpec((span, LANES), lambda p, n, c0=c0: (jnp.maximum(n * nsub - 1, 0), c0 + p))
    return _pcall(
        body, name=name, grid=(2, s // tr), in_specs=[cur(0), cur(0), prev(0), cur(DIL_V0), prev(DIL_V0)],
        out_specs=[cur(0), cur(0), cur(0)], out_shape=[jax.ShapeDtypeStruct((s, DIL_W), F32)] * 3,
        compiler_params=_cparams("parallel", "arbitrary"))(q, k, k, proj, proj)


def _dil_bwd(q, k, proj, dmix, mall, zall, delta, r, *, name):
    s = q.shape[0]
    span, nsub = _dil_tiles(s, r)
    tr = nsub * span
    nbig = s // tr

    def body(q_ref, kc_ref, kp_ref, vc_ref, vp_ref, do_ref, m_ref, z_ref, dl_ref, dq_ref, dk_ref, dv_ref, pk, pv):
        n = pl.program_id(1)
        lane = _iota((QBLK, LANES), 1)
        h0 = lane < HEAD_DIM

        @pl.when(n == 0)
        def _():
            pk[...] = jnp.zeros_like(pk)
            pv[...] = jnp.zeros_like(pv)

        @pl.when(n < nbig)
        def _():
            dk_ref[...] = pk[...]
            dv_ref[...] = pv[...]
            for j in range(nsub):
                valid = _dil_masks(n if j == 0 else 1)
                for rho in range(r):
                    rows = _dil_rows(j, rho, span, r)
                    before = _dil_rows(max(j - 1, 0), rho, span, r)
                    k_prev, v_prev = (kp_ref, vp_ref) if j == 0 else (kc_ref, vc_ref)
                    qv = q_ref[rows, :].astype(BF16)
                    dob = do_ref[rows, :].astype(BF16)
                    zero = jnp.zeros_like(qv)
                    kk = jnp.concatenate([k_prev[before, :], kc_ref[rows, :]], axis=0).astype(BF16)
                    vv = jnp.concatenate([v_prev[before, :], vc_ref[rows, :]], axis=0).astype(BF16)
                    mall_v, z_v, dl_v = m_ref[rows, :], z_ref[rows, :], dl_ref[rows, :]
                    dq, dk, dv = None, None, None
                    for h in range(2):
                        hm = h0 if h == 0 else ~h0
                        qm = jnp.where(hm, qv, zero)
                        dom = jnp.where(hm, dob, zero)
                        c = h * HEAD_DIM
                        sc = jnp.where(valid, _dot(qm, kk, NT), NEG_BIG)
                        pr = jnp.exp(sc - mall_v[:, c:c + 1]) * (1.0 / z_v[:, c:c + 1])
                        ds = (pr * (_dot(dom, vv, NT) - dl_v[:, c:c + 1])).astype(BF16)
                        parts = (_dot(ds, jnp.where(jnp.concatenate([hm, hm], axis=0), kk, jnp.zeros_like(kk))),
                                 _dot(ds, qm, TN), _dot(pr.astype(BF16), dom, TN))
                        dq, dk, dv = parts if h == 0 else (dq + parts[0], dk + parts[1], dv + parts[2])
                    dq_ref[rows, :] = dq
                    pk[rows, :] = dk[QBLK:]
                    pv[rows, :] = dv[QBLK:]
                    if j == 0:
                        last_span = _dil_rows(nsub - 1, rho, span, r)
                        dk_ref[last_span, :] += dk[:QBLK]
                        dv_ref[last_span, :] += dv[:QBLK]
                    else:
                        pk[before, :] += dk[:QBLK]
                        pv[before, :] += dv[:QBLK]

        @pl.when(n == nbig)
        def _():
            dk_ref[...] = pk[...]
            dv_ref[...] = pv[...]

    last = nbig - 1
    cur = lambda c0: pl.BlockSpec((tr, LANES), lambda p, n, c0=c0: (jnp.minimum(n, last), c0 + p))
    prev = lambda c0: pl.BlockSpec((span, LANES), lambda p, n, c0=c0: (jnp.maximum(jnp.minimum(n, last) * nsub - 1, 0), c0 + p))
    late = pl.BlockSpec((tr, LANES), lambda p, n: (jnp.maximum(n - 1, 0), p))
    return _pcall(
        body, name=name, grid=(2, nbig + 1),
        in_specs=[cur(0), cur(0), prev(0), cur(DIL_V0), prev(DIL_V0), cur(DIL_DO0), cur(0), cur(0), cur(0)],
        out_specs=[cur(0), late, late], out_shape=[jax.ShapeDtypeStruct((s, DIL_W), F32)] * 3,
        scratch_shapes=[pltpu.VMEM((tr, LANES), F32)] * 2,
        compiler_params=_cparams("parallel", "arbitrary"))(q, k, k, proj, proj, dmix, mall, zall, delta)


HG_SHIFT = HG_BLK.bit_length() - 1
HG_Q0, HG_F0, HG_I0 = (3 * SB_W + 3 * DIL_W) // HG_D, (3 * SB_W + 3 * DIL_W + HG_W) // HG_D, (3 * SB_W + 3 * DIL_W + 2 * HG_W) // HG_D


def _hg_scan(x):
    t = x.shape[0]
    half = HG_BLK // 2
    rb = _iota((t, LANES), 0) & (HG_BLK - 1)
    rh = rb & (half - 1)
    p = x
    for s in (1, 2, 4):
        p = p + jnp.where(rh >= s, pltpu.roll(p, s, 0), 0.0)
    h = jnp.where(rh == half - 1, p, 0.0)
    for s in (1, 2, 4):
        h = h + jnp.where(rh + s < half, pltpu.roll(h, t - s, 0), 0.0)
    first = rb < half
    pref = jnp.where(first, p, p + pltpu.roll(h, half, 0))
    total = h + jnp.where(first, pltpu.roll(h, t - half, 0), pltpu.roll(h, half, 0))
    return p, h, pref, total, first


def _hg_same(t):
    i = jnp.arange(t) >> HG_SHIFT
    return (i[:, None] == i[None, :]).astype(F32)


def _hg_own(t):
    i = jnp.arange(t) >> HG_SHIFT
    j = jnp.arange(t // HG_BLK * HG_D) // HG_D
    return (i[:, None] == j[None, :]).astype(BF16)


def _hg_diag(x, nb):
    return jnp.concatenate([x[b * HG_BLK:(b + 1) * HG_BLK, b * HG_D:(b + 1) * HG_D] for b in range(nb)], axis=0)


def _hg_inputs(qh, z, v, la, lc, t):
    lsg = jnp.minimum(z, 0.0) - jnp.log(1.0 + jnp.exp(-jnp.abs(z)))
    b = lc + lsg
    lf = jnp.maximum(la, b) + jnp.log(1.0 + jnp.exp(-jnp.abs(la - b)))
    f = jnp.exp(lf)
    sq = _sigmoid(qh)
    p, h, g, gl, first = _hg_scan(lf)
    k = 1.0 - f
    qs = qh * sq
    eq = jnp.where(first, 0.0, jnp.exp(jnp.minimum(p, 0.0)))
    ek = jnp.where(first, jnp.exp(jnp.minimum(h - p, 0.0)), 0.0)
    return dict(lf=lf, b=b, f=f, k=k, sq=sq, qs=qs, g=g, eg=jnp.exp(g), egl=jnp.exp(gl - g), dec=jnp.exp(gl),
                eq=eq, ek=ek, qx=(qs * eq).astype(BF16), kx=(k * ek).astype(BF16))


def _hgrn_fwd(proj, la, lc, *, name, side=None):
    s = proj.shape[0]
    t = _rows(s, HG_TILE)
    nt, nb = s // t, t // HG_BLK

    def body(q_ref, f_ref, i_ref, la_ref, lc_ref, same_ref, own_ref, o_ref, st_ref, state):
        @pl.when(pl.program_id(1) == 0)
        def _():
            state[...] = jnp.zeros_like(state)

        v = i_ref[...]
        a = _hg_inputs(q_ref[...], f_ref[...], v, la_ref[...], lc_ref[...], t)
        qs, k = a["qs"], a["k"]
        vb = v.astype(BF16)
        rb = _iota((t, LANES), 0) & (HG_BLK // 2 - 1)
        o = jnp.sum(qs * k, axis=1, keepdims=True) * v
        e = None
        for d in range(1, HG_BLK // 2):
            m = rb >= d
            fr = a["f"] if d == 1 else pltpu.roll(a["f"], d - 1, 0)
            e = fr if e is None else e * fr
            cd = jnp.sum(qs * pltpu.roll(k, d, 0) * e, axis=1, keepdims=True)
            o = o + jnp.where(m, cd, 0.0) * pltpu.roll(v, d, 0)
        cross = _dot(a["qx"], a["kx"], NT) * same_ref[...]
        o = o + _dot(cross.astype(BF16), vb)
        qt = (qs * a["eg"]).astype(BF16)
        kt = (k * a["egl"]).astype(BF16)
        upd = _dot(vb, jnp.tile(kt, (1, nb)) * own_ref[...], TN)
        st = state[...]
        for blk in range(nb):
            st_ref[blk * HG_D:(blk + 1) * HG_D, :] = st.astype(BF16)
            st = a["dec"][blk * HG_BLK:blk * HG_BLK + 1] * st + upd[:, blk * HG_D:(blk + 1) * HG_D]
        state[...] = st
        o_ref[...] = o + _hg_diag(_dot(qt, st_ref[...], NT), nb)

    col = lambda c0: pl.BlockSpec((t, HG_D), lambda hd, i, c0=c0: (i, c0 + hd))
    vec = pl.BlockSpec((1, HG_D), lambda hd, i: (0, hd))
    return _call_with_exchange(
        body, side, lambda: (pl.program_id(0) == 0) & (pl.program_id(1) == 0),
        lambda: (pl.program_id(0) == 3) & (pl.program_id(1) == nt - 1),
        (proj, proj, proj, la, lc, _hg_same(t), _hg_own(t)), name=name, grid=(4, nt),
        in_specs=[col(HG_Q0), col(HG_F0), col(HG_I0), vec, vec, pl.BlockSpec((t, t), lambda hd, i: (0, 0)),
                  pl.BlockSpec((t, nb * HG_D), lambda hd, i: (0, 0))],
        out_specs=[col(0), pl.BlockSpec((None, nb * HG_D, HG_D), lambda hd, i: (hd, i, 0))],
        out_shape=[jax.ShapeDtypeStruct((s, HG_W), F32), jax.ShapeDtypeStruct((4, s // HG_BLK * HG_D, HG_D), BF16)],
        scratch_shapes=[pltpu.VMEM((HG_D, HG_D), F32)],
        compiler_params=_cparams("arbitrary", "arbitrary"))


def _hgrn_bwd(proj, la, lc, st, doh, *, name, side=None):
    s = proj.shape[0]
    t = _rows(s, HG_TILE)
    nt, nb = s // t, t // HG_BLK

    def body(q_ref, f_ref, i_ref, la_ref, lc_ref, st_ref, do_ref, same_ref, own_ref, dq_ref, df_ref, di_ref, dla_ref, dlc_ref,
             dstate, dsb):
        @pl.when(pl.program_id(1) == 0)
        def _():
            dstate[...] = jnp.zeros_like(dstate)
            dla_ref[...] = jnp.zeros_like(dla_ref)
            dlc_ref[...] = jnp.zeros_like(dlc_ref)

        qh, z, v, do = q_ref[...], f_ref[...], i_ref[...], do_ref[...]
        la = la_ref[...]
        a = _hg_inputs(qh, z, v, la, lc_ref[...], t)
        qs, k, g = a["qs"], a["k"], a["g"]
        vb = v.astype(BF16)
        dob = do.astype(BF16)
        rb = _iota((t, LANES), 0) & (HG_BLK // 2 - 1)
        dc0 = jnp.sum(do * v, axis=1, keepdims=True)
        dq = dc0 * k
        dk = dc0 * qs
        dv = jnp.sum(qs * k, axis=1, keepdims=True) * do
        e = None
        for d in range(1, HG_BLK // 2):
            m = rb >= d
            fr = a["f"] if d == 1 else pltpu.roll(a["f"], d - 1, 0)
            e = fr if e is None else e * fr
            ks = pltpu.roll(k, d, 0)
            qe = qs * e
            cd = jnp.where(m, jnp.sum(qe * ks, axis=1, keepdims=True), 0.0)
            dcd = jnp.where(m, jnp.sum(do * pltpu.roll(v, d, 0), axis=1, keepdims=True), 0.0)
            dq = dq + dcd * ks * e
            dk = dk + pltpu.roll(dcd * qe, t - d, 0)
            dv = dv + pltpu.roll(cd * do, t - d, 0)
        same = same_ref[...]
        cross = (_dot(a["qx"], a["kx"], NT) * same).astype(BF16)
        dcross = (_dot(dob, vb, NT) * same).astype(BF16)
        dq = dq + _dot(dcross, a["kx"]) * a["eq"]
        dk = dk + _dot(dcross, a["qx"], TN) * a["ek"]
        dv = dv + _dot(cross, dob, TN)
        qt = (qs * a["eg"]).astype(BF16)
        kt = (k * a["egl"]).astype(BF16)
        own = own_ref[...]
        upd = _dot(dob, jnp.tile(qt, (1, nb)) * own, TN)
        ds = dstate[...]
        dgs = [None] * nb
        for blk in reversed(range(nb)):
            rows = slice(blk * HG_D, (blk + 1) * HG_D)
            dec = a["dec"][blk * HG_BLK:blk * HG_BLK + 1]
            dsb[rows, :] = ds.astype(BF16)
            dgs[blk] = jnp.broadcast_to(jnp.sum(ds * st_ref[rows, :].astype(F32), axis=0, keepdims=True) * dec, (HG_BLK, HG_D))
            ds = dec * ds + upd[:, rows]
        dstate[...] = ds
        dki = _dot(jnp.tile(vb, (1, nb)) * own, dsb[...]) * a["egl"]
        dq = dq + _dot(jnp.tile(dob, (1, nb)) * own, st_ref[...]) * a["eg"]
        dk = dk + dki
        dv = dv + _hg_diag(_dot(kt, dsb[...], NT), nb)
        x = qs * dq - k * dk
        _, _, xpre, xtot, _ = _hg_scan(x)
        _, _, _, ktot, _ = _hg_scan(k * dki)
        dlf = (xtot - xpre + x) + ktot + jnp.concatenate(dgs, axis=0) - a["f"] * dk
        wb = jnp.exp(a["b"] - a["lf"])
        wa = jnp.exp(la - a["lf"])
        sq = a["sq"]
        dq_ref[...] = dq * (sq * (1.0 + qh * (1.0 - sq)))
        df_ref[...] = dlf * wb * (1.0 - _sigmoid(z))
        di_ref[...] = dv
        dla_ref[...] += jnp.sum(dlf * wa, axis=0, keepdims=True)
        dlc_ref[...] += jnp.sum(dlf * wb, axis=0, keepdims=True)

    col = lambda c0: pl.BlockSpec((t, HG_D), lambda hd, i, c0=c0: (nt - 1 - i, c0 + hd))
    vec = pl.BlockSpec((1, HG_D), lambda hd, i: (0, hd))
    return _call_with_exchange(
        body, side, lambda: (pl.program_id(0) == 0) & (pl.program_id(1) == 0),
        lambda: (pl.program_id(0) == 3) & (pl.program_id(1) == nt - 1),
        (proj, proj, proj, la, lc, st, doh, _hg_same(t), _hg_own(t)), name=name, grid=(4, nt),
        in_specs=[col(HG_Q0), col(HG_F0), col(HG_I0), vec, vec,
                  pl.BlockSpec((None, nb * HG_D, HG_D), lambda hd, i: (hd, nt - 1 - i, 0)), col(0),
                  pl.BlockSpec((t, t), lambda hd, i: (0, 0)), pl.BlockSpec((t, nb * HG_D), lambda hd, i: (0, 0))],
        out_specs=[col(0), col(0), col(0), vec, vec],
        out_shape=[jax.ShapeDtypeStruct((s, HG_W), F32)] * 3 + [jax.ShapeDtypeStruct((1, HG_W), F32)] * 2,
        scratch_shapes=[pltpu.VMEM((HG_D, HG_D), F32), pltpu.VMEM((nb * HG_D, HG_D), BF16)],
        compiler_params=_cparams("arbitrary", "arbitrary"))


GH0 = (IN_W - HG_W) // HG_W


def _mix_out(o_a, nums, dens, mxs, oh, proj, hg, *, name):
    s = o_a.shape[0]
    t = _rows(s, TE)

    def body(oa_ref, n0, n1, n2, d0, d1, d2, m0, m1, m2, oh_ref, gh_ref, hg_ref, y_ref, od_ref, mall_ref, z_ref):
        y_ref[:, :SB_W] = oa_ref[...].astype(BF16)
        m = jnp.maximum(jnp.maximum(m0[...], m1[...]), m2[...])
        num = jnp.zeros((t, DIL_W), F32)
        z = jnp.zeros((t, DIL_W), F32)
        for n_ref, d_ref, m_ref in ((n0, d0, m0), (n1, d1, m1), (n2, d2, m2)):
            sc = jnp.exp(m_ref[...] - m)
            num = num + n_ref[...] * sc
            z = z + d_ref[...] * sc
        od = num / z
        od_ref[...] = od
        mall_ref[...] = m
        z_ref[...] = z
        y_ref[:, SB_W:SB_W + DIL_W] = od.astype(BF16)
        for h in range(4):
            sl = slice(h * HG_D, (h + 1) * HG_D)
            ov = oh_ref[:, sl]
            g = gh_ref[:, sl]
            r = lax.rsqrt(jnp.mean(ov * ov, axis=1, keepdims=True) + EPS)
            y_ref[:, SB_W + DIL_W + h * HG_D:SB_W + DIL_W + (h + 1) * HG_D] = (ov * r * hg_ref[...] * (g * _sigmoid(g))).astype(BF16)

    rd = _row_spec(t, DIL_W)
    return _pcall(
        body, name=name, grid=(s // t,),
        in_specs=[rd] * 10 + [_row_spec(t, HG_W), _row_spec(t, HG_W, GH0), _vec_spec(HG_D)],
        out_specs=[_row_spec(t, MIX_W), rd, rd, rd],
        out_shape=[jax.ShapeDtypeStruct((s, MIX_W), BF16)] + [jax.ShapeDtypeStruct((s, DIL_W), F32)] * 3,
        compiler_params=_cparams("parallel"))(o_a, *nums, *dens, *mxs, oh, proj, hg)


def _mix_out_bwd(dmix, oh, proj, hg, od, *, name):
    s = oh.shape[0]
    t = _rows(s, TE)

    def body(dm_ref, oh_ref, gh_ref, hg_ref, od_ref, doh_ref, dgh_ref, dl_ref, dhg_ref):
        @pl.when(pl.program_id(0) == 0)
        def _():
            dhg_ref[...] = jnp.zeros_like(dhg_ref)

        seg = _seg_consts()
        for j in range(2):
            sl = slice(j * LANES, (j + 1) * LANES)
            dl_ref[:, sl] = _xdot(dm_ref[:, SB_W + j * LANES:SB_W + (j + 1) * LANES] * od_ref[:, sl], seg, 3)
        hgv = hg_ref[...]
        for h in range(4):
            sl = slice(h * HG_D, (h + 1) * HG_D)
            dy = dm_ref[:, SB_W + DIL_W + h * HG_D:SB_W + DIL_W + (h + 1) * HG_D]
            ov = oh_ref[:, sl]
            g = gh_ref[:, sl]
            sg = _sigmoid(g)
            silu = g * sg
            r = lax.rsqrt(jnp.mean(ov * ov, axis=1, keepdims=True) + EPS)
            nrm = ov * r
            dhg_ref[...] += jnp.sum(dy * nrm * silu, axis=0, keepdims=True)
            dgh_ref[:, sl] = dy * nrm * hgv * (sg * (1.0 + g * (1.0 - sg)))
            dn = dy * hgv * silu
            doh_ref[:, sl] = r * (dn - nrm * jnp.mean(dn * nrm, axis=1, keepdims=True))

    rh = _row_spec(t, HG_W)
    return _pcall(
        body, name=name, grid=(s // t,),
        in_specs=[_row_spec(t, MIX_W), rh, _row_spec(t, HG_W, GH0), _vec_spec(HG_D), _row_spec(t, DIL_W)],
        out_specs=[rh, rh, _row_spec(t, DIL_W), _vec_spec(HG_D)],
        out_shape=[jax.ShapeDtypeStruct((s, HG_W), F32)] * 2 + [jax.ShapeDtypeStruct((s, DIL_W), F32), jax.ShapeDtypeStruct((1, HG_D), F32)],
        compiler_params=_cparams("arbitrary"))(dmix, oh, proj, hg, od)


def _lb_terms(l):
    l0, l1 = l[0:1], l[1:2]
    m = jnp.maximum(l0, l1)
    e0, e1 = jnp.exp(l0 - m), jnp.exp(l1 - m)
    s0, s1 = e0 / (e0 + e1), e1 / (e0 + e1)
    args = (s0 - s0, (s0 + s1) - s0)
    lbs = tuple(jnp.minimum(jnp.maximum(a, 0.0), 1.0 - EPS) for a in args)
    return s0, s1, args, lbs


def _lb_prep(logits, *, name):
    def body(l_ref, lb_ref, la_ref, lc_ref):
        _, _, _, lbs = _lb_terms(l_ref[...])
        lb = jnp.concatenate(lbs, axis=0)
        lb_ref[...] = lb
        la_ref[...] = jnp.log(jnp.maximum(lb, LB_FLOOR))
        lc_ref[...] = jnp.log1p(-lb)

    return _pcall(body, name=name, out_shape=[jax.ShapeDtypeStruct(logits.shape, F32)] * 3)(logits)


def _lb_bwd(logits, dla, dlc, *, name):
    def half(hi, eq):
        return jnp.where(hi, 1.0, jnp.where(eq, 0.5, 0.0))

    def body(l_ref, dla_ref, dlc_ref, o_ref):
        s0, s1, args, lbs = _lb_terms(l_ref[...])
        da = []
        for i in range(2):
            a, lb = args[i], lbs[i]
            dlb = dla_ref[i:i + 1] * half(lb > LB_FLOOR, lb == LB_FLOOR) / jnp.maximum(lb, LB_FLOOR) - dlc_ref[i:i + 1] / (1.0 - lb)
            t = jnp.maximum(a, 0.0)
            da.append(dlb * half(a > 0.0, a == 0.0) * half(t < 1.0 - EPS, t == 1.0 - EPS))
        ds0 = (da[0] + da[1]) - (da[0] + da[1])
        ds1 = da[1]
        dot = s0 * ds0 + s1 * ds1
        o_ref[...] = jnp.concatenate([s0 * (ds0 - dot), s1 * (ds1 - dot)], axis=0)

    return _pcall(body, name=name, out_shape=jax.ShapeDtypeStruct(logits.shape, F32))(logits, dla, dlc)


def _mod_fwd(c8, w, b, *, name):
    _, d, n = w.shape
    tn = _tile(n, 768)

    def body(c_ref, w_ref, b_ref, o_ref):
        cv = c_ref[...]
        o_ref[...] = _dot((cv * _sigmoid(cv)).astype(BF16), w_ref[...].astype(BF16)) + b_ref[...]

    return _pcall(
        body, name=name, grid=(2, n // tn),
        in_specs=[pl.BlockSpec((8, d), lambda l, j: (0, 0)), pl.BlockSpec((None, d, tn), lambda l, j: (l, 0, j)),
                  pl.BlockSpec((None, 1, tn), lambda l, j: (l, 0, j))],
        out_specs=pl.BlockSpec((None, 8, tn), lambda l, j: (l, 0, j)),
        out_shape=jax.ShapeDtypeStruct((2, 8, n), F32), compiler_params=_cparams("parallel", "parallel"))(c8, w, b)


def _mod_bwd(ct, dm, *, name):
    d = ct.shape[0]
    n = dm.shape[2]
    tn = _tile(n, 768)

    def body(c_ref, dm_ref, o_ref):
        cv = c_ref[...]
        sc = cv * _sigmoid(cv)
        dv = dm_ref[...]
        acc = sc[:, 0:1] * dv[0:1, :]
        for b in range(1, 8):
            acc = acc + sc[:, b:b + 1] * dv[b:b + 1, :]
        o_ref[...] = acc

    return _pcall(
        body, name=name, grid=(2, n // tn),
        in_specs=[pl.BlockSpec((d, 8), lambda l, j: (0, 0)), pl.BlockSpec((None, 8, tn), lambda l, j: (l, 0, j))],
        out_specs=pl.BlockSpec((None, d, tn), lambda l, j: (l, 0, j)),
        out_shape=jax.ShapeDtypeStruct((2, d, n), F32), compiler_params=_cparams("parallel", "parallel"))(ct, dm)


_PEERS = {
    "chips": ((1, 0, 0), (0, 1, 0), (1, 1, 0)),
    "all": tuple((a, b, c) for a in (0, 1) for b in (0, 1) for c in (0, 1) if a + b + c),
    "sib": ((0, 0, 1),),
}
_SLOTS = {"chips": 4, "all": 8, "sib": 2}


def _slot(kind, x, y, c):
    return {"chips": 2 * x + y, "all": 4 * x + 2 * y + c, "sib": c}[kind]


class _Xchg:
    def __init__(self, arrs, kind, scatter):
        self.arrs, self.kind, self.scatter = list(arrs), kind, scatter
        self.n = len(self.arrs)
        self.peers = _PEERS[kind]
        total = self.n * len(self.peers)
        self.specs = [pl.BlockSpec(memory_space=pl.ANY)] * self.n
        self.out_shape = [jax.ShapeDtypeStruct(a.shape if scatter else (_SLOTS[kind],) + a.shape, a.dtype) for a in self.arrs]
        self.scratch = [pltpu.SemaphoreType.DMA((total,)), pltpu.SemaphoreType.DMA((total,)), pltpu.SemaphoreType.DMA((self.n,))]

    def copies(self, ins, outs, send, recv, loc):
        kind, scatter = self.kind, self.scatter
        x, y, c = lax.axis_index("x"), lax.axis_index("y"), lax.axis_index("c")
        me = _slot(kind, x, y, c)
        out = []
        for a in range(self.n):
            out.append(pltpu.make_async_copy(ins[a].at[me] if scatter else ins[a], outs[a].at[me], loc.at[a]))
            for j, (dx, dy, dc) in enumerate(self.peers):
                px, py, pc = (1 - x if dx else x), (1 - y if dy else y), (1 - c if dc else c)
                sem = a * len(self.peers) + j
                out.append(pltpu.make_async_remote_copy(
                    src_ref=ins[a].at[_slot(kind, px, py, pc)] if scatter else ins[a], dst_ref=outs[a].at[me],
                    send_sem=send.at[sem], recv_sem=recv.at[sem], device_id=(px, py, pc), device_id_type=MESH_ID))
        return out


def _exchange(arrs, kind, scatter, *, name):
    xc = _Xchg(arrs, kind, scatter)

    def body(*refs):
        copies = xc.copies(refs[:xc.n], refs[xc.n:2 * xc.n], *refs[2 * xc.n:])
        for cp in copies:
            cp.start()
        for cp in copies:
            cp.wait()

    return _pcall(body, name=name, in_specs=xc.specs, out_specs=xc.specs, out_shape=xc.out_shape, scratch_shapes=xc.scratch)(*arrs)


def _with_exchange(body, n_in, n_out, n_scr, xc, first, last):
    def wrapped(*refs):
        ins, side_in = refs[:n_in], refs[n_in:n_in + xc.n]
        o0 = n_in + xc.n
        outs, side_out = refs[o0:o0 + n_out], refs[o0 + n_out:o0 + n_out + xc.n]
        s0 = o0 + n_out + xc.n
        scr, sems = refs[s0:s0 + n_scr], refs[s0 + n_scr:]

        @pl.when(first())
        def _():
            for cp in xc.copies(side_in, side_out, *sems):
                cp.start()

        body(*ins, *outs, *scr)

        @pl.when(last())
        def _():
            for cp in xc.copies(side_in, side_out, *sems):
                cp.wait()

    return wrapped


def _call_with_exchange(body, xc, first, last, args, *, in_specs, out_specs, out_shape, scratch_shapes=(), **kw):
    if xc is None:
        return _pcall(body, in_specs=in_specs, out_specs=out_specs, out_shape=out_shape, scratch_shapes=scratch_shapes, **kw)(*args), None
    wrapped = _with_exchange(body, len(in_specs), len(out_specs), len(scratch_shapes), xc, first, last)
    res = _pcall(wrapped, in_specs=list(in_specs) + xc.specs, out_specs=list(out_specs) + xc.specs,
                 out_shape=list(out_shape) + xc.out_shape, scratch_shapes=list(scratch_shapes) + xc.scratch, **kw)(*args, *xc.arrs)
    return res[:len(out_specs)], res[len(out_specs):]


def _sum_slots(a, *, name, out_dtype=F32):
    ns, r, c = a.shape
    t = _tile(r, max(16, (1 << 18) // c // 16 * 16), 16)

    def body(a_ref, o_ref):
        acc = a_ref[0].astype(F32)
        for i in range(1, ns):
            acc = acc + a_ref[i].astype(F32)
        o_ref[...] = acc.astype(o_ref.dtype)

    return _pcall(body, name=name, grid=(r // t,), in_specs=[pl.BlockSpec((ns, t, c), lambda i: (0, i, 0))],
                  out_specs=pl.BlockSpec((t, c), lambda i: (i, 0)), out_shape=jax.ShapeDtypeStruct((r, c), out_dtype),
                  compiler_params=_cparams("parallel"))(a)


def _adamw(w, gparts, m, v, *, name):
    r, c = w.shape
    t = _tile(r, max(16, (1 << 17) // c // 16 * 16), 16)
    ng = len(gparts)

    def body(*refs):
        w_ref, m_ref, v_ref = refs[0], refs[1 + ng], refs[2 + ng]
        g_ref, d_ref, nm_ref, nv_ref = refs[3 + ng:]
        g = refs[1][...].astype(F32)
        for i in range(1, ng):
            g = g + refs[1 + i][...].astype(F32)
        mn = ADAM_B1 * m_ref[...] + (1.0 - ADAM_B1) * g
        vn = ADAM_B2 * v_ref[...] + (1.0 - ADAM_B2) * (g * g)
        m_hat = mn / (1.0 - ADAM_B1 ** ADAM_STEP)
        v_hat = vn / (1.0 - ADAM_B2 ** ADAM_STEP)
        g_ref[...] = g
        d_ref[...] = -ADAM_LR * (m_hat / (jnp.sqrt(v_hat) + ADAM_EPS) + ADAM_WD * w_ref[...])
        nm_ref[...] = mn
        nv_ref[...] = vn

    spec = pl.BlockSpec((t, c), lambda i: (i, 0))
    return _pcall(body, name=name, grid=(r // t,), in_specs=[spec] * (3 + ng), out_specs=[spec] * 4,
                  out_shape=[jax.ShapeDtypeStruct((r, c), F32)] * 4, compiler_params=_cparams("parallel"))(w, *gparts, m, v)


def _adamw_layers(w, halves, m, v, *, name):
    r, c = w.shape
    nl = len(halves)
    rl = r // nl
    t = _tile(rl, max(16, (1 << 17) // c // 16 * 16), 16)
    nbl = rl // t

    def body(*refs):
        w_ref, m_ref, v_ref = refs[0], refs[1 + 2 * nl], refs[2 + 2 * nl]
        g_ref, d_ref, nm_ref, nv_ref = refs[3 + 2 * nl:]
        g = None
        for l in range(nl):
            gl = refs[1 + 2 * l][...].astype(F32) + refs[2 + 2 * l][...].astype(F32)
            g = gl if g is None else jnp.where(pl.program_id(0) >= l * nbl, gl, g)
        mn = ADAM_B1 * m_ref[...] + (1.0 - ADAM_B1) * g
        vn = ADAM_B2 * v_ref[...] + (1.0 - ADAM_B2) * (g * g)
        m_hat = mn / (1.0 - ADAM_B1 ** ADAM_STEP)
        v_hat = vn / (1.0 - ADAM_B2 ** ADAM_STEP)
        g_ref[...] = g
        d_ref[...] = -ADAM_LR * (m_hat / (jnp.sqrt(v_hat) + ADAM_EPS) + ADAM_WD * w_ref[...])
        nm_ref[...] = mn
        nv_ref[...] = vn

    spec = pl.BlockSpec((t, c), lambda i: (i, 0))
    part = lambda l, core: pl.BlockSpec((None, t, c), lambda i, l=l, core=core: (core, jnp.clip(i - l * nbl, 0, nbl - 1), 0))
    gspecs = [part(l, core) for l in range(nl) for core in range(2)]
    gargs = [halves[l] for l in range(nl) for _ in range(2)]
    return _pcall(body, name=name, grid=(r // t,), in_specs=[spec] + gspecs + [spec, spec], out_specs=[spec] * 4,
                  out_shape=[jax.ShapeDtypeStruct((r, c), F32)] * 4, compiler_params=_cparams("parallel"))(w, *gargs, m, v)


FFN_TM = 512
FFN_CHUNK = 1408


def _resident(shape):
    return pl.BlockSpec(shape, lambda i: (0,) * len(shape), pipeline_mode=pl.Buffered(1))


def _ffn_up(x, sc, sh, wgu, *, name, side=None):
    s, d = x.shape
    f = wgu.shape[1] // 2
    t, fc = _rows(s, FFN_TM), _tile(f, FFN_CHUNK)

    def body(x_ref, sc_ref, sh_ref, w_ref, h_ref, uv_ref, a_ref):
        xv = x_ref[...]
        r = lax.rsqrt(jnp.mean(xv * xv, axis=1, keepdims=True) + EPS)
        hb = ((xv * r) * (1.0 + sc_ref[...]) + sh_ref[...]).astype(BF16)
        h_ref[...] = hb
        for j in range(f // fc):
            u = _dot(hb, w_ref[:, j * fc:(j + 1) * fc])
            v = _dot(hb, w_ref[:, f + j * fc:f + (j + 1) * fc])
            sg = _sigmoid(u)
            silu = u * sg
            uv_ref[:, j * fc:(j + 1) * fc] = (v * (sg * (1.0 + u * (1.0 - sg)))).astype(BF16)
            uv_ref[:, f + j * fc:f + (j + 1) * fc] = silu.astype(BF16)
            a_ref[:, j * fc:(j + 1) * fc] = (silu * v).astype(BF16)

    return _call_with_exchange(
        body, side, lambda: pl.program_id(0) == 0, lambda: pl.program_id(0) == s // t - 1, (x, sc, sh, wgu),
        name=name, grid=(s // t,), in_specs=[_row_spec(t, d), _vec_spec(d), _vec_spec(d), _resident(wgu.shape)],
        out_specs=[_row_spec(t, d), _row_spec(t, 2 * f), _row_spec(t, f)],
        out_shape=[jax.ShapeDtypeStruct((s, d), BF16), jax.ShapeDtypeStruct((s, 2 * f), BF16), jax.ShapeDtypeStruct((s, f), BF16)],
        compiler_params=_cparams("parallel" if side is None else "arbitrary"))


def _norm_mm(x, sc, sh, w, nb, *, name):
    s, d = x.shape
    n = w.shape[1]
    t, nc = _rows(s, FFN_TM), _tile(n, 1792)
    assert nb <= nc

    def body(x_ref, sc_ref, sh_ref, w_ref, h_ref, o_ref, ob_ref):
        xv = x_ref[...]
        r = lax.rsqrt(jnp.mean(xv * xv, axis=1, keepdims=True) + EPS)
        hb = ((xv * r) * (1.0 + sc_ref[...]) + sh_ref[...]).astype(BF16)
        h_ref[...] = hb
        for j in range(n // nc):
            part = _dot(hb, w_ref[:, j * nc:(j + 1) * nc])
            o_ref[:, j * nc:(j + 1) * nc] = part
            if j == 0:
                ob_ref[...] = part[:, :nb].astype(BF16)

    return _pcall(
        body, name=name, grid=(s // t,), in_specs=[_row_spec(t, d), _vec_spec(d), _vec_spec(d), _resident(w.shape)],
        out_specs=[_row_spec(t, d), _row_spec(t, n), _row_spec(t, nb)],
        out_shape=[jax.ShapeDtypeStruct((s, d), BF16), jax.ShapeDtypeStruct((s, n), F32), jax.ShapeDtypeStruct((s, nb), BF16)],
        compiler_params=_cparams("parallel"))(x, sc, sh, w)


def _ffn_dact(dxo, y, sg, wd, uv, *, name, side=None):
    s, d = y.shape
    f = wd.shape[0]
    t, fc = _rows(s, FFN_TM), _tile(f, FFN_CHUNK)

    def body(dxo_ref, y_ref, sg_ref, w_ref, uv_ref, dy_ref, duv_ref, ds_ref):
        @pl.when(pl.program_id(0) == 0)
        def _():
            ds_ref[...] = jnp.zeros_like(ds_ref)

        dv = dxo_ref[...]
        dyb = (sg_ref[...] * dv).astype(BF16)
        dy_ref[...] = dyb
        ds_ref[...] += jnp.sum(dv * y_ref[...].astype(F32), axis=0, keepdims=True)
        for j in range(f // fc):
            da = _dot(dyb, w_ref[j * fc:(j + 1) * fc, :], NT)
            duv_ref[:, j * fc:(j + 1) * fc] = (da * uv_ref[:, j * fc:(j + 1) * fc].astype(F32)).astype(BF16)
            duv_ref[:, f + j * fc:f + (j + 1) * fc] = (da * uv_ref[:, f + j * fc:f + (j + 1) * fc].astype(F32)).astype(BF16)

    return _call_with_exchange(
        body, side, lambda: pl.program_id(0) == 0, lambda: pl.program_id(0) == s // t - 1, (dxo, y, sg, wd, uv),
        name=name, grid=(s // t,),
        in_specs=[_row_spec(t, d), _row_spec(t, d), _vec_spec(d), _resident(wd.shape), _row_spec(t, 2 * f)],
        out_specs=[_row_spec(t, d), _row_spec(t, 2 * f), _vec_spec(d)],
        out_shape=[jax.ShapeDtypeStruct((s, d), BF16), jax.ShapeDtypeStruct((s, 2 * f), BF16), jax.ShapeDtypeStruct((1, d), F32)],
        compiler_params=_cparams("arbitrary"))


def _ffn_dh(duv, wgu, x, sc, dxo, *, name, side=None):
    s, d = x.shape
    f2 = wgu.shape[1]
    t = _rows(s, FFN_TM)

    def body(duv_ref, w_ref, x_ref, sc_ref, dxo_ref, dx_ref, dsc_ref, dsh_ref):
        @pl.when(pl.program_id(0) == 0)
        def _():
            dsc_ref[...] = jnp.zeros_like(dsc_ref)
            dsh_ref[...] = jnp.zeros_like(dsh_ref)

        dhv = _dot(duv_ref[...], w_ref[...], NT)
        xv = x_ref[...]
        r = lax.rsqrt(jnp.mean(xv * xv, axis=1, keepdims=True) + EPS)
        xn = xv * r
        dxn = dhv * (1.0 + sc_ref[...])
        dx_ref[...] = dxo_ref[...] + r * (dxn - xn * jnp.mean(dxn * xn, axis=1, keepdims=True))
        dsc_ref[...] += jnp.sum(dhv * xn, axis=0, keepdims=True)
        dsh_ref[...] += jnp.sum(dhv, axis=0, keepdims=True)

    return _call_with_exchange(
        body, side, lambda: pl.program_id(0) == 0, lambda: pl.program_id(0) == s // t - 1, (duv, wgu, x, sc, dxo),
        name=name, grid=(s // t,),
        in_specs=[_row_spec(t, f2), _resident(wgu.shape), _row_spec(t, d), _vec_spec(d), _row_spec(t, d)],
        out_specs=[_row_spec(t, d), _vec_spec(d), _vec_spec(d)],
        out_shape=[jax.ShapeDtypeStruct((s, d), F32), jax.ShapeDtypeStruct((1, d), F32), jax.ShapeDtypeStruct((1, d), F32)],
        compiler_params=_cparams("arbitrary"))


def _dh_pieces(pieces, w, x, sc, dxo, *, name):
    s, d = x.shape
    t = _rows(s, FFN_TM)
    widths = [p.shape[1] for p in pieces]
    offs = [sum(widths[:i]) for i in range(len(widths))]
    kd = sum(widths)
    npc = len(pieces)

    def body(*refs):
        p_refs = refs[:npc]
        w_ref, x_ref, sc_ref, dxo_ref, dx_ref, dsc_ref, dsh_ref, cat_ref = refs[npc:]

        @pl.when(pl.program_id(0) == 0)
        def _():
            dsc_ref[...] = jnp.zeros_like(dsc_ref)
            dsh_ref[...] = jnp.zeros_like(dsh_ref)

        dhv = None
        for p_ref, off, wd in zip(p_refs, offs, widths):
            pb = p_ref[...].astype(BF16)
            cat_ref[:, off:off + wd] = pb
            part = _dot(pb, w_ref[:, off:off + wd], NT)
            dhv = part if dhv is None else dhv + part
        xv = x_ref[...]
        r = lax.rsqrt(jnp.mean(xv * xv, axis=1, keepdims=True) + EPS)
        xn = xv * r
        dxn = dhv * (1.0 + sc_ref[...])
        dx_ref[...] = dxo_ref[...] + r * (dxn - xn * jnp.mean(dxn * xn, axis=1, keepdims=True))
        dsc_ref[...] += jnp.sum(dhv * xn, axis=0, keepdims=True)
        dsh_ref[...] += jnp.sum(dhv, axis=0, keepdims=True)

    return _pcall(
        body, name=name, grid=(s // t,),
        in_specs=[_row_spec(t, wd) for wd in widths] + [_resident(w.shape), _row_spec(t, d), _vec_spec(d), _row_spec(t, d)],
        out_specs=[_row_spec(t, d), _vec_spec(d), _vec_spec(d), _row_spec(t, kd)],
        out_shape=[jax.ShapeDtypeStruct((s, d), F32), jax.ShapeDtypeStruct((1, d), F32), jax.ShapeDtypeStruct((1, d), F32),
                   jax.ShapeDtypeStruct((s, kd), BF16)],
        compiler_params=_cparams("arbitrary"))(*pieces, w, x, sc, dxo)


def _ffn_fwd(x, sh, sc, g, wgu, wd, tag, side=None):
    (h, uv, a), got = _ffn_up(x, sc, sh, wgu, name=f"{tag}_up", side=side)
    y, xo = _mm(a, wd, name=f"{tag}_down", tm=512, tn=1024, tk=wd.shape[0], res=x, scale=0.5 * g, out_dtype=BF16)
    return xo, (x, h, uv, a, y), got


def _ffn_bwd(dxo, saved, sc, g, wgu, wd, tag, ride=None):
    x, h, uv, a, y = saved
    if ride is None:
        (dyb, duv, dgs), _ = _ffn_dact(dxo, y, 0.5 * g, wd, uv, name=f"{tag}_dact")
        (dx, dsc, dsh), _ = _ffn_dh(duv, wgu, x, sc, dxo, name=f"{tag}_dh")
        dwgu = _mm_tn(h, duv, name=f"{tag}_dwgu", tm=1024, tn=1408, tk=512)
        dwd = _mm_tn(a, dyb, name=f"{tag}_dwd", tm=1408, tn=1024, tk=512)
        return dx, dwgu, dwd, dsh, dsc, 0.5 * dgs, {}
    (dyb, duv, dgs), got = _ffn_dact(dxo, y, 0.5 * g, wd, uv, name=f"{tag}_dact", side=_Xchg([ride[1]], "chips", True))
    part = _sum_slots(got[0], name=f"{tag}_sum_a", out_dtype=BF16)
    dwd, swap_a = _mm_tn(a, dyb, name=f"{tag}_dwd", tm=1408, tn=1024, tk=512, side=_Xchg([part], "sib", False))
    (dx, dsc, dsh), got = _ffn_dh(duv, wgu, x, sc, dxo, name=f"{tag}_dh", side=_Xchg([_by_chip("ffn1_w_down", dwd)], "chips", True))
    part = _sum_slots(got[0], name=f"{tag}_sum_b", out_dtype=BF16)
    dwgu, swap_b = _mm_tn(h, duv, name=f"{tag}_dwgu", tm=1024, tn=1408, tk=512, side=_Xchg([part], "sib", False))
    return dx, dwgu, dwd, dsh, dsc, 0.5 * dgs, {ride[0]: swap_a[0], "ffn1_w_down": swap_b[0]}


def _layer_fwd(x0, mod, w, par, tag, carry=None, own=None):
    sh1, sc1, g1, sh2, sc2, g2, sh3, sc3, g3 = (mod[i:i + 1] for i in range(N_MOD))
    x1, f1, got = _ffn_fwd(x0, sh1, sc1, g1, w["gu1"], w["d1"], f"{tag}_ffn1",
                           side=None if own is None else _Xchg(own[0], "chips", False))
    if own is not None:
        w = {**w, **own[1](got)}
    h2, proj, qkv = _norm_mm(x1, sc2, sh2, w["in"], 3 * SB_W, name=f"{tag}_in")
    o_a, got_a = _sb_fwd(qkv, name=f"{tag}_sb", side=None if carry is None else _Xchg(carry[0], "chips", False))
    qd, kd = _dil_prep(proj, par["gq"], par["gk"], par["cos"], par["sin"], name=f"{tag}_dil_prep")
    nums, dens, mxs = [], [], []
    for _, r in DIL_PATTERNS:
        nu, de, mx = _dil_fwd(qd, kd, proj, r, name=f"{tag}_dil{r}")
        nums.append(nu)
        dens.append(de)
        mxs.append(mx)
    (oh, st), got_b = _hgrn_fwd(proj, par["la"], par["lc"], name=f"{tag}_hgrn",
                                side=None if carry is None else _Xchg(carry[1], "chips", False))
    ymix, od, mall, zall = _mix_out(o_a, nums, dens, mxs, oh, proj, par["hg"], name=f"{tag}_mix_out")
    out, x2 = _mm(ymix, w["out"], name=f"{tag}_out", tm=512, tn=1024, tk=1024, res=x1, scale=g2, out_dtype=BF16)
    x3, f2, _ = _ffn_fwd(x2, sh3, sc3, g3, w["gu2"], w["d2"], f"{tag}_ffn2")
    return x3, dict(f1=f1, f2=f2, x1=x1, h2=h2, proj=proj, qkv=qkv, o_a=o_a, qd=qd, kd=kd, oh=oh, st=st,
                    ymix=ymix, od=od, mall=mall, zall=zall, out=out), (None if carry is None else (got_a, got_b)), w


COL_SHARDED = ("ffn1_w_gate", "ffn1_w_up", "w_in", "ffn2_w_gate", "ffn2_w_up")
ROW_SHARDED = ("ffn1_w_down", "w_out", "ffn2_w_down")


def _by_chip(name, g):
    if name in COL_SHARDED:
        return jnp.moveaxis(g.reshape(g.shape[0], 4, -1), 1, 0)
    return g.reshape(4, -1, g.shape[1])


def _halves(x):
    f = x.shape[1] // 2
    return x[:, :f], x[:, f:]


def _layer_bwd(dx3, sv, mod, w, par, tag, pending):
    sh1, sc1, g1, sh2, sc2, g2, sh3, sc3, g3 = (mod[i:i + 1] for i in range(N_MOD))
    dx2, dwgu2, dwd2, dsh3, dsc3, dg3, _ = _ffn_bwd(dx3, sv["f2"], sc3, g3, w["gu2"], w["d2"], f"{tag}_ffn2")
    doutb, dg2 = _gate_bwd(dx2, sv["out"], g2, name=f"{tag}_dgate2")
    dmix = _mm(doutb, w["out"], name=f"{tag}_dmix", tb=True, tm=512, tn=1024, tk=1024)
    dwout = _mm_tn(sv["ymix"], doutb, name=f"{tag}_dwout", tm=1024, tn=1024, tk=512)
    proj = sv["proj"]
    doh, dgh, delta, dhg = _mix_out_bwd(dmix, sv["oh"], proj, par["hg"], sv["od"], name=f"{tag}_dmix_out")
    dgate2, dup2 = _halves(dwgu2)
    ride = dict(pending)
    for n, g in (("ffn2_w_gate", dgate2), ("ffn2_w_up", dup2), ("ffn2_w_down", dwd2), ("w_out", dwout)):
        ride[(tag, n)] = _by_chip(n, g)
    keys = list(ride)
    (dqa, dka, dva), got = _sb_bwd(sv["qkv"], sv["o_a"], dmix, name=f"{tag}_dsb", side=_Xchg([ride[k] for k in keys], "chips", True))
    parts = [_sum_slots(g, name=f"{tag}_sum{i}", out_dtype=BF16) for i, g in enumerate(got)]
    dqs, dks, dvs = [], [], []
    for _, r in DIL_PATTERNS:
        a, b, c = _dil_bwd(sv["qd"], sv["kd"], proj, dmix, sv["mall"], sv["zall"], delta, r, name=f"{tag}_ddil{r}")
        dqs.append(a)
        dks.append(b)
        dvs.append(c)
    dqd, dkd, dvd, dgq, dgk = _dil_prep_bwd(proj, par["gq"], par["gk"], par["cos"], par["sin"], dqs, dks, dvs,
                                             name=f"{tag}_ddil_prep")
    (dqh, dfh, dih, dla, dlc), swapped = _hgrn_bwd(proj, par["la"], par["lc"], sv["st"], doh, name=f"{tag}_dhgrn",
                                                   side=_Xchg(parts, "sib", False))
    dx1, dsc2, dsh2, dproj = _dh_pieces([dqa, dka, dva, dqd, dkd, dvd, dqh, dfh, dih, dgh], w["in"], sv["x1"], sc2, dx2,
                                         name=f"{tag}_dh2")
    dwin = _mm_tn(sv["h2"], dproj, name=f"{tag}_dwin", tm=1024, tn=1792, tk=512)
    dx0, dwgu1, _, dsh1, dsc1, dg1, rode = _ffn_bwd(dx1, sv["f1"], sc1, g1, w["gu1"], w["d1"], f"{tag}_ffn1",
                                                    ride=("w_in", _by_chip("w_in", dwin)))
    dmod = jnp.concatenate([dsh1, dsc1, dg1, dsh2, dsc2, dg2, dsh3, dsc3, dg3], axis=0)
    fold = lambda v: v.reshape(2, HEAD_DIM).sum(axis=0)
    dgate1, dup1 = _halves(dwgu1)
    late = {(tag, n): _by_chip(n, g) for n, g in (("ffn1_w_gate", dgate1), ("ffn1_w_up", dup1))}
    grads = dict(late=late, dmod=dmod, gq=fold(dgq), gk=fold(dgk), hg=dhg[0], la=dla[0], lc=dlc[0])
    return dx0, grads, {**dict(zip(keys, swapped)), **{(tag, n): v for n, v in rode.items()}}


def _pack(pieces):
    flat = jnp.concatenate([p.reshape(-1) for p in pieces])
    pad = (-flat.shape[0]) % (8 * LANES)
    return jnp.pad(flat, (0, pad)).reshape(-1, LANES)


def _unpack(flat, like):
    out, off = [], 0
    for p in like:
        out.append(flat[off:off + p.size].reshape(p.shape))
        off += p.size
    return out


def kernel(x, c, w_mod, b_mod, ffn1_w_gate, ffn1_w_up, ffn1_w_down, w_in, w_out, q_norm_g, k_norm_g, hgrn_norm_g, hgrn_lb_logits, ffn2_w_gate, ffn2_w_up, ffn2_w_down, loss_target, m_w_mod, m_b_mod, m_ffn1_w_gate, m_ffn1_w_up, m_ffn1_w_down, m_w_in, m_w_out, m_q_norm_g, m_k_norm_g, m_hgrn_norm_g, m_hgrn_lb_logits, m_ffn2_w_gate, m_ffn2_w_up, m_ffn2_w_down, v_w_mod, v_b_mod, v_ffn1_w_gate, v_ffn1_w_up, v_ffn1_w_down, v_w_in, v_w_out, v_q_norm_g, v_k_norm_g, v_hgrn_norm_g, v_hgrn_lb_logits, v_ffn2_w_gate, v_ffn2_w_up, v_ffn2_w_down):
    names = ["w_mod", "b_mod", "ffn1_w_gate", "ffn1_w_up", "ffn1_w_down", "w_in", "w_out", "q_norm_g", "k_norm_g",
             "hgrn_norm_g", "hgrn_lb_logits", "ffn2_w_gate", "ffn2_w_up", "ffn2_w_down"]
    wts = dict(zip(names, (w_mod, b_mod, ffn1_w_gate, ffn1_w_up, ffn1_w_down, w_in, w_out, q_norm_g, k_norm_g, hgrn_norm_g,
                           hgrn_lb_logits, ffn2_w_gate, ffn2_w_up, ffn2_w_down)))
    mom = dict(zip(names, (m_w_mod, m_b_mod, m_ffn1_w_gate, m_ffn1_w_up, m_ffn1_w_down, m_w_in, m_w_out, m_q_norm_g, m_k_norm_g,
                           m_hgrn_norm_g, m_hgrn_lb_logits, m_ffn2_w_gate, m_ffn2_w_up, m_ffn2_w_down)))
    var = dict(zip(names, (v_w_mod, v_b_mod, v_ffn1_w_gate, v_ffn1_w_up, v_ffn1_w_down, v_w_in, v_w_out, v_q_norm_g, v_k_norm_g,
                           v_hgrn_norm_g, v_hgrn_lb_logits, v_ffn2_w_gate, v_ffn2_w_up, v_ffn2_w_down)))
    depth = w_mod.shape[0]
    assert depth == 2 and x.shape[0] == 1
    s, d = x.shape[1:]
    assert s % (DIL_PATTERNS[-1][1] * QBLK) == 0 and d % LANES == 0
    xi, yi, ci = lax.axis_index("x"), lax.axis_index("y"), lax.axis_index("c")
    chip = 2 * xi + yi
    dev = 2 * chip + ci
    x0, tgt = x[0], loss_target[0]

    c8 = _exchange([c.reshape(d // LANES, LANES)], "all", False, name="gather_c")[0].reshape(8, d)
    ncol = w_mod.shape[2]
    b_loc = lax.dynamic_slice_in_dim(b_mod, chip * ncol, ncol, axis=1)
    m_loc = _mod_fwd(c8, w_mod, b_loc.reshape(depth, 1, ncol), name="mod_fwd")
    m_all = _exchange([m_loc], "chips", False, name="gather_mod")[0]
    mod = jnp.transpose(lax.dynamic_index_in_dim(m_all, dev, axis=2, keepdims=False), (1, 0, 2)).reshape(depth, N_MOD, d)

    col_sharded = ["ffn1_w_gate", "ffn1_w_up", "w_in", "ffn2_w_gate", "ffn2_w_up"]
    row_sharded = ["ffn1_w_down", "w_out", "ffn2_w_down"]
    big = col_sharded + row_sharded
    early = ["ffn1_w_gate", "ffn1_w_up", "ffn1_w_down", "w_in", "w_out"]
    late = ["ffn2_w_gate", "ffn2_w_up", "ffn2_w_down"]

    def shards(l, group):
        return [wts[n][l].astype(BF16) for n in group]

    def whole(group, gathered):
        out = {}
        for n, g in zip(group, gathered):
            out[n] = jnp.moveaxis(g, 0, 1).reshape(g.shape[1], -1) if n in col_sharded else g.reshape(-1, g.shape[2])
        return out

    def early_weights(gathered):
        full = whole(early, gathered)
        return {"gu1": jnp.concatenate([full["ffn1_w_gate"], full["ffn1_w_up"]], axis=1), "d1": full["ffn1_w_down"],
                "in": full["w_in"], "out": full["w_out"]}

    def late_weights(gathered):
        full = whole(late, gathered)
        return {"gu2": jnp.concatenate([full["ffn2_w_gate"], full["ffn2_w_up"]], axis=1), "d2": full["ffn2_w_down"]}

    ws = [early_weights(_exchange(shards(0, early), "chips", False, name="gather_w"))]

    _, la, lc = _lb_prep(hgrn_lb_logits, name="lb_prep")
    cos, sin = _rope_tables(s)
    pars = [dict(gq=jnp.tile(q_norm_g[l], 2)[None], gk=jnp.tile(k_norm_g[l], 2)[None], hg=hgrn_norm_g[l][None],
                 la=la[l:l + 1], lc=lc[l:l + 1], cos=cos, sin=sin) for l in range(depth)]

    xs, saved = x0, []
    for l in range(depth):
        carry = (shards(l + 1, early), shards(l + 1, late)) if l + 1 < depth else None
        own = (shards(l, late), late_weights) if l == 0 else None
        xs, sv, got, ws[l] = _layer_fwd(xs, mod[l], ws[l], pars[l], f"l{l}", carry, own)
        saved.append(sv)
        if got is not None:
            ws.append({**early_weights(got[0]), **late_weights(got[1])})
    dx, lpart = _loss_grad(xs, tgt, name="loss")

    grads, halves, pending = [None] * depth, {}, {}
    for l in reversed(range(depth)):
        dx, grads[l], swapped = _layer_bwd(dx, saved[l], mod[l], ws[l], pars[l], f"l{l}", pending)
        halves.update(swapped)
        pending = grads[l]["late"]
    keys = list(pending)
    got = _exchange([pending[k] for k in keys], "chips", True, name="scatter_grads")
    parts = [_sum_slots(g, name=f"sum_{k[1]}", out_dtype=BF16) for k, g in zip(keys, got)]
    halves.update(zip(keys, _exchange(parts, "sib", False, name="swap_grads")))

    stack = lambda k: jnp.stack([grads[l][k] for l in range(depth)])
    small = [stack("dmod"), stack("gq"), stack("gk"), stack("hg"), stack("la"), stack("lc"), lpart[0, :1]]
    packed = _pack(small)
    allp = _exchange([packed], "all", False, name="gather_small")[0]
    tot = _unpack(_sum_slots(allp, name="sum_small").reshape(-1), small)
    g_b_mod = tot[0].reshape(depth, N_MOD * d)
    loss = tot[6][0]
    g_small = {"b_mod": g_b_mod, "q_norm_g": tot[1], "k_norm_g": tot[2], "hgrn_norm_g": tot[3],
               "hgrn_lb_logits": _lb_bwd(hgrn_lb_logits, tot[4], tot[5], name="lb_bwd")}

    dm_all = allp.reshape(8, -1)[:, :depth * N_MOD * d].reshape(8, depth, N_MOD * d)
    dm_loc = jnp.transpose(lax.dynamic_slice_in_dim(dm_all, chip * ncol, ncol, axis=2), (1, 0, 2))
    g_w_mod = _mod_bwd(c8.T, dm_loc, name="mod_bwd")

    outs = {}
    for n in names:
        w2 = wts[n].reshape(-1, wts[n].shape[-1])
        m2, v2 = mom[n].reshape(w2.shape), var[n].reshape(w2.shape)
        if n in big:
            res = _adamw_layers(w2, [halves[(f"l{l}", n)] for l in range(depth)], m2, v2, name=f"adamw_{n}")
        else:
            g = g_w_mod if n == "w_mod" else g_small[n]
            res = _adamw(w2, [g.reshape(w2.shape)], m2, v2, name=f"adamw_{n}")
        outs[n] = [r.reshape(wts[n].shape) for r in res]
    return (loss, dx[None], *[outs[n][0] for n in names], *[outs[n][1] for n in names], *[outs[n][2] for n in names],
            *[outs[n][3] for n in names])
```
